```python
import jax, jax.numpy as jnp
from jax import lax
import numpy as np

D_MODEL = 1024
BATCH = 8
SEQ = 8192
DEPTH = 1

HG_HEADS = 8
HG_EXPAND = 128
HG_KEY_DIM = HG_HEADS * HG_EXPAND
HG_VAL_DIM = D_MODEL
HG_HEAD_V = HG_VAL_DIM // HG_HEADS
HG_CHUNK = 64
ATT_Q_HEADS = 16
ATT_KV_HEADS = 4
ATT_HEAD_DIM = 64
ATT_GROUP = ATT_Q_HEADS // ATT_KV_HEADS
ATT_WIDTH = ATT_Q_HEADS * ATT_HEAD_DIM
KV_WIDTH = ATT_KV_HEADS * ATT_HEAD_DIM
WINDOW = 128
ATT_BLOCK = WINDOW
EPS = 1e-6

IN_SIZES = [
    HG_KEY_DIM,
    HG_KEY_DIM,
    HG_VAL_DIM,
    HG_VAL_DIM,
    ATT_WIDTH,
    KV_WIDTH,
    KV_WIDTH,
    ATT_WIDTH,
    D_MODEL,
    D_MODEL,
]
D_IN = sum(IN_SIZES)
SPLIT_POINTS = [int(s) for s in np.cumsum(IN_SIZES)[:-1]]

kernel_name = "hybrid_hgrn2_swa_sink_gated_block"


def rms_norm(x, w):
    xf = x.astype(jnp.float32)
    xf = xf * lax.rsqrt(jnp.mean(xf * xf, axis=-1, keepdims=True) + EPS)
    return xf.astype(x.dtype) * w


def hgrn2_chunkwise(q, k, v, log_f):
    B, T, H, dk = q.shape
    dv = v.shape[-1]
    n = T // HG_CHUNK

    def to_chunks(a):
        return a.reshape(B, n, HG_CHUNK, H, a.shape[-1]).transpose(1, 0, 3, 2, 4)

    qc, kc, vc, gc = to_chunks(q), to_chunks(k), to_chunks(v), to_chunks(log_f)
    causal = jnp.tril(jnp.ones((HG_CHUNK, HG_CHUNK), dtype=bool))

    def step(S, inp):
        q_, k_, v_, g_ = inp
        b = jnp.cumsum(g_, axis=2)
        o_inter = jnp.einsum('bhtd,bhdv->bhtv', q_ * jnp.exp(b), S)
        diff = b[:, :, :, None, :] - b[:, :, None, :, :]
        decay = jnp.exp(jnp.where(causal[:, :, None], diff, -jnp.inf))
        scores = jnp.einsum('bhtd,bhsd,bhtsd->bhts', q_, k_, decay)
        o_intra = jnp.einsum('bhts,bhsv->bhtv', scores, v_)
        b_last = b[:, :, -1:, :]
        S_new = jnp.exp(b_last[:, :, 0, :])[..., None] * S + jnp.einsum(
            'bhsd,bhsv->bhdv', k_ * jnp.exp(b_last - b), v_)
        return S_new, o_inter + o_intra

    S0 = jnp.zeros((B, H, dk, dv), jnp.float32)
    _, o = lax.scan(step, S0, (qc, kc, vc, gc))
    return o.transpose(1, 0, 3, 2, 4).reshape(B, T, H, dv)


def sliding_window_attention_with_sinks(q, k, v, sinks):
    B, T = q.shape[0], q.shape[1]
    n = T // ATT_BLOCK
    qb = q.reshape(B, n, ATT_BLOCK, ATT_KV_HEADS, ATT_GROUP, ATT_HEAD_DIM)

    def banded(a):
        ab = a.reshape(B, n, ATT_BLOCK, ATT_KV_HEADS, ATT_HEAD_DIM)
        prev = jnp.pad(ab[:, :-1], ((0, 0), (1, 0), (0, 0), (0, 0), (0, 0)))
        return jnp.concatenate([prev, ab], axis=2)

    keys, vals = banded(k), banded(v)
    scores = jnp.einsum('bnqhgd,bnkhd->bnhgqk', qb, keys).astype(jnp.float32) * (ATT_HEAD_DIM ** -0.5)
    qi = jnp.arange(ATT_BLOCK)[:, None]
    kj = jnp.arange(2 * ATT_BLOCK)[None, :]
    rel = qi + ATT_BLOCK - kj
    band = (rel >= 0) & (rel < WINDOW)
    pad_keys = (jnp.arange(n) == 0)[:, None, None] & (kj < ATT_BLOCK)[None]
    valid = band[None] & ~pad_keys
    scores = jnp.where(valid[None, :, None, None], scores, -jnp.inf)
    sink = sinks.astype(jnp.float32).reshape(ATT_KV_HEADS, ATT_GROUP)[None, None, :, :, None, None]
    m = jnp.maximum(jnp.max(scores, axis=-1, keepdims=True), sink)
    p = jnp.exp(scores - m)
    probs = p / (jnp.sum(p, axis=-1, keepdims=True) + jnp.exp(sink - m))
    out = jnp.einsum('bnhgqk,bnkhd->bnqhgd', probs.astype(v.dtype), vals)
    return out.reshape(B, T, ATT_WIDTH)


def _fwd_setup_inputs(seed: int = 0) -> dict:
    key = jax.random.key(seed)
    ks = jax.random.split(key, 12)
    f32 = jnp.float32
    return {
        "x": jax.random.normal(ks[0], (BATCH, SEQ, D_MODEL), f32),
        "norm_w": 1.0 + 0.02 * jax.random.normal(ks[1], (DEPTH, D_MODEL), f32),
        "w_in": jax.random.normal(ks[2], (DEPTH, D_MODEL, D_IN), f32) * D_MODEL ** -0.5,
        "hgrn_lower_bound": 0.1 * jax.random.normal(ks[3], (DEPTH + 1, HG_KEY_DIM), f32),
        "hgrn_norm_w": 1.0 + 0.02 * jax.random.normal(ks[4], (DEPTH, HG_VAL_DIM), f32),
        "w_branch_hgrn": jax.random.normal(ks[5], (DEPTH, HG_VAL_DIM, D_MODEL), f32) * HG_VAL_DIM ** -0.5,
        "attn_sinks": 0.5 * jax.random.normal(ks[6], (DEPTH, ATT_Q_HEADS), f32),
        "w_branch_attn": jax.random.normal(ks[7], (DEPTH, ATT_WIDTH, D_MODEL), f32) * ATT_WIDTH ** -0.5,
        "w_out": jax.random.normal(ks[8], (DEPTH, D_MODEL, D_MODEL), f32) * D_MODEL ** -0.5,
        "final_norm_w": 1.0 + 0.02 * jax.random.normal(ks[9], (D_MODEL,), f32),
    }


def _fwd_reference(x, norm_w, w_in, hgrn_lower_bound, hgrn_norm_w, w_branch_hgrn, attn_sinks,
              w_branch_attn, w_out, final_norm_w):
    B, T, _ = x.shape
    lb_all = jnp.cumsum(jax.nn.softmax(hgrn_lower_bound.astype(jnp.float32), axis=0), axis=0)
    for l in range(DEPTH):
        xn = rms_norm(x, norm_w[l])
        proj = xn @ w_in[l]
        hq, hf, hi, hg, aq, ak, av, ag, mh, ma = jnp.split(proj, SPLIT_POINTS, axis=-1)

        lb = lb_all[l]
        f = lb + (1.0 - lb) * jax.nn.sigmoid(hf.astype(jnp.float32))
        log_f = jnp.log(f).reshape(B, T, HG_HEADS, HG_EXPAND)
        k_h = (1.0 - f).reshape(B, T, HG_HEADS, HG_EXPAND)
        q_h = jax.nn.silu(hq.astype(jnp.float32)).reshape(B, T, HG_HEADS, HG_EXPAND)
        v_h = hi.astype(jnp.float32).reshape(B, T, HG_HEADS, HG_HEAD_V)
        o_h = hgrn2_chunkwise(q_h, k_h, v_h, log_f)
        o_h = rms_norm(o_h, hgrn_norm_w[l].reshape(HG_HEADS, HG_HEAD_V)).reshape(B, T, HG_VAL_DIM)
        y_h = (o_h.astype(x.dtype) * jax.nn.silu(hg)) @ w_branch_hgrn[l]

        o_a = sliding_window_attention_with_sinks(
            aq.reshape(B, T, ATT_Q_HEADS, ATT_HEAD_DIM),
            ak.reshape(B, T, ATT_KV_HEADS, ATT_HEAD_DIM),
            av.reshape(B, T, ATT_KV_HEADS, ATT_HEAD_DIM),
            attn_sinks[l])
        y_a = (o_a * jax.nn.silu(ag)) @ w_branch_attn[l]

        merged = jax.nn.sigmoid(mh) * y_h + jax.nn.sigmoid(ma) * y_a
        x = x + merged @ w_out[l]
    return rms_norm(x, final_norm_w)


import jax as _jax
import jax.numpy as _jnp

TWIN_FORMAT = 'train_step'
FWD_PARAMS = ['x', 'norm_w', 'w_in', 'hgrn_lower_bound', 'hgrn_norm_w', 'w_branch_hgrn', 'attn_sinks', 'w_branch_attn', 'w_out', 'final_norm_w']
TWIN_WEIGHTS = ['norm_w', 'w_in', 'hgrn_lower_bound', 'hgrn_norm_w', 'w_branch_hgrn', 'attn_sinks', 'w_branch_attn', 'w_out', 'final_norm_w']
TWIN_DIFF_INPUT = 'x'
TWIN_INPUTS = ['x', 'norm_w', 'w_in', 'hgrn_lower_bound', 'hgrn_norm_w', 'w_branch_hgrn', 'attn_sinks', 'w_branch_attn', 'w_out', 'final_norm_w', 'loss_target', 'm_norm_w', 'm_w_in', 'm_hgrn_lower_bound', 'm_hgrn_norm_w', 'm_w_branch_hgrn', 'm_attn_sinks', 'm_w_branch_attn', 'm_w_out', 'm_final_norm_w', 'v_norm_w', 'v_w_in', 'v_hgrn_lower_bound', 'v_hgrn_norm_w', 'v_w_branch_hgrn', 'v_attn_sinks', 'v_w_branch_attn', 'v_w_out', 'v_final_norm_w']
TWIN_OUTPUTS = ['loss', 'grad_x', 'grad_norm_w', 'grad_w_in', 'grad_hgrn_lower_bound', 'grad_hgrn_norm_w', 'grad_w_branch_hgrn', 'grad_attn_sinks', 'grad_w_branch_attn', 'grad_w_out', 'grad_final_norm_w', 'delta_norm_w', 'delta_w_in', 'delta_hgrn_lower_bound', 'delta_hgrn_norm_w', 'delta_w_branch_hgrn', 'delta_attn_sinks', 'delta_w_branch_attn', 'delta_w_out', 'delta_final_norm_w', 'new_m_norm_w', 'new_m_w_in', 'new_m_hgrn_lower_bound', 'new_m_hgrn_norm_w', 'new_m_w_branch_hgrn', 'new_m_attn_sinks', 'new_m_w_branch_attn', 'new_m_w_out', 'new_m_final_norm_w', 'new_v_norm_w', 'new_v_w_in', 'new_v_hgrn_lower_bound', 'new_v_hgrn_norm_w', 'new_v_w_branch_hgrn', 'new_v_attn_sinks', 'new_v_w_branch_attn', 'new_v_w_out', 'new_v_final_norm_w']
TWIN_LEAF_KINDS = {'loss': 'loss', 'grad_x': 'grad_x', 'grad_norm_w': 'grad_w', 'grad_w_in': 'grad_w', 'grad_hgrn_lower_bound': 'grad_w', 'grad_hgrn_norm_w': 'grad_w', 'grad_w_branch_hgrn': 'grad_w', 'grad_attn_sinks': 'grad_w', 'grad_w_branch_attn': 'grad_w', 'grad_w_out': 'grad_w', 'grad_final_norm_w': 'grad_w', 'delta_norm_w': 'delta_w', 'delta_w_in': 'delta_w', 'delta_hgrn_lower_bound': 'delta_w', 'delta_hgrn_norm_w': 'delta_w', 'delta_w_branch_hgrn': 'delta_w', 'delta_attn_sinks': 'delta_w', 'delta_w_branch_attn': 'delta_w', 'delta_w_out': 'delta_w', 'delta_final_norm_w': 'delta_w', 'new_m_norm_w': 'new_m', 'new_m_w_in': 'new_m', 'new_m_hgrn_lower_bound': 'new_m', 'new_m_hgrn_norm_w': 'new_m', 'new_m_w_branch_hgrn': 'new_m', 'new_m_attn_sinks': 'new_m', 'new_m_w_branch_attn': 'new_m', 'new_m_w_out': 'new_m', 'new_m_final_norm_w': 'new_m', 'new_v_norm_w': 'new_v', 'new_v_w_in': 'new_v', 'new_v_hgrn_lower_bound': 'new_v', 'new_v_hgrn_norm_w': 'new_v', 'new_v_w_branch_hgrn': 'new_v', 'new_v_attn_sinks': 'new_v', 'new_v_w_branch_attn': 'new_v', 'new_v_w_out': 'new_v', 'new_v_final_norm_w': 'new_v'}


def _forward(args):
    return _fwd_reference(*[args[k] for k in FWD_PARAMS])


def _output_shape():
    def fwd():
        inp = _fwd_setup_inputs(0)
        return _fwd_reference(*[inp[k] for k in FWD_PARAMS])
    out = _jax.eval_shape(fwd)
    return out.shape, out.dtype

N_MICROBATCH = 1
ADAM_LR = 0.001
ADAM_B1 = 0.9
ADAM_B2 = 0.999
ADAM_EPS = 1e-08
ADAM_WD = 0.01
ADAM_STEP = 10
PER_EXAMPLE_BATCH_AXIS = {'x': 0, 'loss_target': 0}
SHARED_INPUTS = []
_WEIGHT_DTYPES = {'norm_w': _jnp.float32, 'w_in': _jnp.float32, 'hgrn_lower_bound': _jnp.float32, 'hgrn_norm_w': _jnp.float32, 'w_branch_hgrn': _jnp.float32, 'attn_sinks': _jnp.float32, 'w_branch_attn': _jnp.float32, 'w_out': _jnp.float32, 'final_norm_w': _jnp.float32}
MOMENT_SCALE = {'norm_w': 1.208803e-01, 'w_in': 4.079551e-02, 'hgrn_lower_bound': 7.614466e-03, 'hgrn_norm_w': 8.195984e-02, 'w_branch_hgrn': 7.687394e-02, 'attn_sinks': 1.470779e-02, 'w_branch_attn': 1.553212e-02, 'w_out': 7.806072e-02, 'final_norm_w': 6.403149e+01}


def _to_microbatches(a, axis):
    t = _jnp.moveaxis(a, axis, 0)
    t = t.reshape((N_MICROBATCH, t.shape[0] // N_MICROBATCH) + t.shape[1:])
    return _jnp.moveaxis(t, 1, axis + 1)


def setup_inputs(seed: int = 0) -> dict:
    inp = _fwd_setup_inputs(seed)
    key = _jax.random.fold_in(_jax.random.key(seed), 7919)
    shape, _ = _output_shape()
    out = dict(inp)
    out["loss_target"] = _jax.random.normal(_jax.random.fold_in(key, 0), shape, _jnp.float32)
    for i, name in enumerate(TWIN_WEIGHTS):
        w = inp[name].astype(_jnp.float32)
        if MOMENT_SCALE is None:
            s = _jnp.sqrt(_jnp.mean(_jnp.square(w)) + 1e-30)
        else:
            s = MOMENT_SCALE[name]
        km, kv = _jax.random.split(_jax.random.fold_in(key, i + 1))
        out[name] = w
        out["m_" + name] = s * _jax.random.normal(km, w.shape, _jnp.float32)
        out["v_" + name] = (s * s) * _jax.random.uniform(kv, w.shape, _jnp.float32, 0.5, 1.5)
    if N_MICROBATCH > 1:
        for name, axis in PER_EXAMPLE_BATCH_AXIS.items():
            out[name] = _to_microbatches(out[name], axis)
    return {'x': out['x'], 'norm_w': out['norm_w'], 'w_in': out['w_in'], 'hgrn_lower_bound': out['hgrn_lower_bound'], 'hgrn_norm_w': out['hgrn_norm_w'], 'w_branch_hgrn': out['w_branch_hgrn'], 'attn_sinks': out['attn_sinks'], 'w_branch_attn': out['w_branch_attn'], 'w_out': out['w_out'], 'final_norm_w': out['final_norm_w'], 'loss_target': out['loss_target'], 'm_norm_w': out['m_norm_w'], 'm_w_in': out['m_w_in'], 'm_hgrn_lower_bound': out['m_hgrn_lower_bound'], 'm_hgrn_norm_w': out['m_hgrn_norm_w'], 'm_w_branch_hgrn': out['m_w_branch_hgrn'], 'm_attn_sinks': out['m_attn_sinks'], 'm_w_branch_attn': out['m_w_branch_attn'], 'm_w_out': out['m_w_out'], 'm_final_norm_w': out['m_final_norm_w'], 'v_norm_w': out['v_norm_w'], 'v_w_in': out['v_w_in'], 'v_hgrn_lower_bound': out['v_hgrn_lower_bound'], 'v_hgrn_norm_w': out['v_hgrn_norm_w'], 'v_w_branch_hgrn': out['v_w_branch_hgrn'], 'v_attn_sinks': out['v_attn_sinks'], 'v_w_branch_attn': out['v_w_branch_attn'], 'v_w_out': out['v_w_out'], 'v_final_norm_w': out['v_final_norm_w']}


def _loss(weights, diff, rest, loss_target):
    with _jax.named_scope("forward"):
        args = {**rest, TWIN_DIFF_INPUT: diff, **{k: w.astype(_WEIGHT_DTYPES[k]) for k, w in weights.items()}}
        y = _forward(args)
    with _jax.named_scope("loss_head"):
        err = _jnp.square(y.astype(_jnp.float32) - loss_target)
        return 0.5 * _jnp.sum(_jnp.mean(err, axis=-1)) if err.ndim else 0.5 * err


def _adamw(w, g, m, v):
    m = ADAM_B1 * m + (1.0 - ADAM_B1) * g
    v = ADAM_B2 * v + (1.0 - ADAM_B2) * _jnp.square(g)
    m_hat = m / (1.0 - ADAM_B1 ** ADAM_STEP)
    v_hat = v / (1.0 - ADAM_B2 ** ADAM_STEP)
    delta = -ADAM_LR * (m_hat / (_jnp.sqrt(v_hat) + ADAM_EPS) + ADAM_WD * w)
    return delta, m, v


def reference(x, norm_w, w_in, hgrn_lower_bound, hgrn_norm_w, w_branch_hgrn, attn_sinks, w_branch_attn, w_out, final_norm_w, loss_target, m_norm_w, m_w_in, m_hgrn_lower_bound, m_hgrn_norm_w, m_w_branch_hgrn, m_attn_sinks, m_w_branch_attn, m_w_out, m_final_norm_w, v_norm_w, v_w_in, v_hgrn_lower_bound, v_hgrn_norm_w, v_w_branch_hgrn, v_attn_sinks, v_w_branch_attn, v_w_out, v_final_norm_w):
    given = dict(x=x, norm_w=norm_w, w_in=w_in, hgrn_lower_bound=hgrn_lower_bound, hgrn_norm_w=hgrn_norm_w, w_branch_hgrn=w_branch_hgrn, attn_sinks=attn_sinks, w_branch_attn=w_branch_attn, w_out=w_out, final_norm_w=final_norm_w, loss_target=loss_target, m_norm_w=m_norm_w, m_w_in=m_w_in, m_hgrn_lower_bound=m_hgrn_lower_bound, m_hgrn_norm_w=m_hgrn_norm_w, m_w_branch_hgrn=m_w_branch_hgrn, m_attn_sinks=m_attn_sinks, m_w_branch_attn=m_w_branch_attn, m_w_out=m_w_out, m_final_norm_w=m_final_norm_w, v_norm_w=v_norm_w, v_w_in=v_w_in, v_hgrn_lower_bound=v_hgrn_lower_bound, v_hgrn_norm_w=v_hgrn_norm_w, v_w_branch_hgrn=v_w_branch_hgrn, v_attn_sinks=v_attn_sinks, v_w_branch_attn=v_w_branch_attn, v_w_out=v_w_out, v_final_norm_w=v_final_norm_w)
    weights = {n: given[n] for n in TWIN_WEIGHTS}
    shared = {n: given[n] for n in SHARED_INPUTS}
    per_example = {n: given[n] for n in ['x']}
    grad_fn = _jax.value_and_grad(_loss, argnums=(0, 1))

    def one_microbatch(ex, loss_target):
        ex = dict(ex)
        diff = ex.pop(TWIN_DIFF_INPUT)
        return grad_fn(weights, diff, {**shared, **ex}, loss_target)

    if N_MICROBATCH == 1:
        loss, (grad_w, grad_x) = one_microbatch(per_example, given["loss_target"])
    else:
        def body(carry, xs):
            loss_sum, grad_sum = carry
            l_k, (gw_k, gx_k) = one_microbatch(xs[0], xs[1])
            with _jax.named_scope("update"):
                return (loss_sum + l_k, _jax.tree.map(_jnp.add, grad_sum, gw_k)), gx_k

        init = (_jnp.zeros((), _jnp.float32), _jax.tree.map(_jnp.zeros_like, weights))
        (loss, grad_w), grad_x = _jax.lax.scan(body, init, (per_example, given["loss_target"]))
    with _jax.named_scope("update"):
        delta_w, new_m, new_v = {}, {}, {}
        for n in TWIN_WEIGHTS:
            delta_w[n], new_m[n], new_v[n] = _adamw(weights[n], grad_w[n], given["m_" + n], given["v_" + n])
    return (loss, grad_x, *[grad_w[n] for n in TWIN_WEIGHTS], *[delta_w[n] for n in TWIN_WEIGHTS],
            *[new_m[n] for n in TWIN_WEIGHTS], *[new_v[n] for n in TWIN_WEIGHTS])
```

```python
import functools

import jax
import jax.numpy as jnp
from jax import lax
from jax.experimental import pallas as pl
from jax.experimental.pallas import tpu as pltpu

F32 = jnp.float32
BF16 = jnp.bfloat16

D = 1024
DIN = 8704
NDEV = 8
IN_SHARD = DIN // NDEV
ROW_SHARD = D // NDEV
HEADS = 8
HD = 128
CH = 64
QH = 16
AB = 128
EPS = 1e-6
NEG = -1e30
ATT_SCALE = 0.125

C_HGRN = 0
C_AQ = 3072
C_HG = 4096
C_AG = 5120
C_MH = 6144
C_MA = 7168
C_AK = 8192
C_AV = 8448
CB = 512
PIECES = ((0, 6), (6, 2), (8, 8), (16, 1))

LR = 0.001
B1 = 0.9
B2 = 0.999
ADAM_EPS = 1e-08
WD = 0.01
STEP = 10

V7X_VMEM_BYTES = 64 * 1024 * 1024
MESH = pl.DeviceIdType.MESH


def _cp(vmem_mb):
    return pltpu.CompilerParams(vmem_limit_bytes=vmem_mb * 1024 * 1024)


def _mm(a, b):
    return jnp.dot(a, b, preferred_element_type=F32)


def _mm_nt(a, b):
    return lax.dot_general(a, b, (((1,), (1,)), ((), ())), preferred_element_type=F32)


def _mm_tn(a, b):
    return lax.dot_general(a, b, (((0,), (0,)), ((), ())), preferred_element_type=F32)


def _mm_exact(a, b):
    return jnp.dot(a, b, preferred_element_type=F32, precision=lax.Precision.HIGHEST)


def _sigmoid(v):
    return 1.0 / (1.0 + jnp.exp(-v))


def _bf(v):
    return v.astype(BF16)


def _peer(r):
    x, y, c = lax.axis_index("x"), lax.axis_index("y"), lax.axis_index("c")
    px = 1 - x if r & 4 else x
    py = 1 - y if r & 2 else y
    pc = 1 - c if r & 1 else c
    return (px, py, pc), 4 * px + 2 * py + pc


def _gather_weights(w_in_s, wbh_s, wba_s, wout_s):
    def body(win_ref, wbh_ref, wba_ref, wout_ref, win_g, wbh_g, wba_g, wout_g, sin, s3, send_sems, recv_sems, loc_sems):
        _, me = _peer(0)
        sin[...] = win_ref[...].astype(BF16)
        s3[0] = wbh_ref[...].astype(BF16)
        s3[1] = wba_ref[...].astype(BF16)
        s3[2] = wout_ref[...].astype(BF16)
        srcs = [sin, s3.at[0], s3.at[1], s3.at[2]]
        outs = [win_g, wbh_g, wba_g, wout_g]
        local = [pltpu.make_async_copy(srcs[k], outs[k].at[me], loc_sems.at[k]) for k in range(4)]
        for cp in local:
            cp.start()
        sends = []
        for r in range(1, NDEV):
            peer, _ = _peer(r)
            for k in range(4):
                cp = pltpu.make_async_remote_copy(
                    src_ref=srcs[k], dst_ref=outs[k].at[me], send_sem=send_sems.at[r - 1, k],
                    recv_sem=recv_sems.at[r - 1, k], device_id=peer, device_id_type=MESH)
                cp.start()
                sends.append(cp)
        for r in range(1, NDEV):
            peer, pidx = _peer(r)
            for k in range(4):
                pltpu.make_async_remote_copy(
                    src_ref=srcs[k], dst_ref=outs[k].at[pidx], send_sem=send_sems.at[r - 1, k],
                    recv_sem=recv_sems.at[r - 1, k], device_id=peer, device_id_type=MESH).wait_recv()
        for cp in sends:
            cp.wait_send()
        for cp in local:
            cp.wait()

    vm = pl.BlockSpec(memory_space=pltpu.VMEM)
    hbm = pl.BlockSpec(memory_space=pl.ANY)
    return pl.pallas_call(
        body,
        out_shape=[jax.ShapeDtypeStruct((NDEV, D, IN_SHARD), BF16)] + [jax.ShapeDtypeStruct((NDEV, ROW_SHARD, D), BF16)] * 3,
        in_specs=[vm, vm, vm, vm],
        out_specs=[hbm, hbm, hbm, hbm],
        scratch_shapes=[pltpu.VMEM((D, IN_SHARD), BF16), pltpu.VMEM((3, ROW_SHARD, D), BF16),
                        pltpu.SemaphoreType.DMA((NDEV - 1, 4)), pltpu.SemaphoreType.DMA((NDEV - 1, 4)),
                        pltpu.SemaphoreType.DMA((4,))],
        name="gather_weights", compiler_params=_cp(32),
    )(w_in_s, wbh_s, wba_s, wout_s)


def _exchange_grads(dwin_r, dwbh_r, dwba_r, dwout_r, small):
    arrs = [dwin_r, dwbh_r, dwba_r, dwout_r]
    na = len(arrs) + 1

    def body(a0, a1, a2, a3, sm, r0, r1, r2, r3, rsm, send_sems, recv_sems, loc_sems):
        ins = [a0, a1, a2, a3]
        outs = [r0, r1, r2, r3]
        _, me = _peer(0)
        local = [pltpu.make_async_copy(ins[k].at[me], outs[k].at[me], loc_sems.at[k]) for k in range(4)]
        local.append(pltpu.make_async_copy(sm, rsm.at[me], loc_sems.at[4]))
        for cp in local:
            cp.start()
        sends = []
        for r in range(1, NDEV):
            peer, pidx = _peer(r)
            for k in range(na):
                src = ins[k].at[pidx] if k < 4 else sm
                dst = outs[k].at[me] if k < 4 else rsm.at[me]
                cp = pltpu.make_async_remote_copy(
                    src_ref=src, dst_ref=dst, send_sem=send_sems.at[r - 1, k], recv_sem=recv_sems.at[r - 1, k],
                    device_id=peer, device_id_type=MESH)
                cp.start()
                sends.append(cp)
        for r in range(1, NDEV):
            peer, pidx = _peer(r)
            for k in range(na):
                src = ins[k].at[pidx] if k < 4 else sm
                dst = outs[k].at[pidx] if k < 4 else rsm.at[pidx]
                pltpu.make_async_remote_copy(
                    src_ref=src, dst_ref=dst, send_sem=send_sems.at[r - 1, k], recv_sem=recv_sems.at[r - 1, k],
                    device_id=peer, device_id_type=MESH).wait_recv()
        for cp in sends:
            cp.wait_send()
        for cp in local:
            cp.wait()

    hbm = pl.BlockSpec(memory_space=pl.ANY)
    return pl.pallas_call(
        body,
        out_shape=[jax.ShapeDtypeStruct(a.shape, F32) for a in arrs] + [jax.ShapeDtypeStruct((NDEV,) + small.shape, F32)],
        in_specs=[hbm] * na,
        out_specs=[hbm] * na,
        scratch_shapes=[pltpu.SemaphoreType.DMA((NDEV - 1, na)), pltpu.SemaphoreType.DMA((NDEV - 1, na)),
                        pltpu.SemaphoreType.DMA((na,))],
        name="exchange_grads",
    )(*arrs, small)


def _inproj_fwd(x2d, norm_w, w_p, tb, nb):
    t = x2d.shape[0]

    def body(x_ref, nw_ref, w_ref, proj_ref, xnt_ref, xn_s):
        @pl.when(pl.program_id(1) == 0)
        def _():
            xv = x_ref[...]
            r = lax.rsqrt(jnp.mean(xv * xv, axis=-1, keepdims=True) + EPS)
            xn = (xv * r) * nw_ref[...]
            xn_s[...] = xn.astype(BF16)
            xnt_ref[...] = xn.T.astype(BF16)

        proj_ref[...] = _mm(xn_s[...], w_ref[...])

    return pl.pallas_call(
        body, grid=(t // tb, DIN // nb),
        in_specs=[pl.BlockSpec((tb, D), lambda i, j: (i, 0)), pl.BlockSpec((1, D), lambda i, j: (0, 0)),
                  pl.BlockSpec((D, nb), lambda i, j: (0, j))],
        out_specs=[pl.BlockSpec((tb, nb), lambda i, j: (i, j)), pl.BlockSpec((D, tb), lambda i, j: (0, i))],
        out_shape=[jax.ShapeDtypeStruct((t, DIN), F32), jax.ShapeDtypeStruct((D, t), BF16)],
        scratch_shapes=[pltpu.VMEM((tb, D), BF16)],
        name="inproj_fwd", compiler_params=_cp(48),
    )(x2d, norm_w, w_p)


def _lower_bound(lb_ref):
    a0 = lb_ref[0:1, :]
    a1 = lb_ref[1:2, :]
    mx = jnp.maximum(a0, a1)
    e0 = jnp.exp(a0 - mx)
    e1 = jnp.exp(a1 - mx)
    return e0 / (e0 + e1)


def _hgrn_chunk_fwd(hq, hf, lb, tril):
    sg = _sigmoid(hf)
    f = lb + (1.0 - lb) * sg
    g = jnp.log(f)
    k = 1.0 - f
    sq = _sigmoid(hq)
    q = hq * sq
    b = _mm_exact(tril, g)
    last_row = lax.broadcasted_iota(jnp.int32, b.shape, 0) == CH - 1
    b_last = jnp.sum(jnp.where(last_row, b, 0.0), axis=0, keepdims=True)
    c = 0.5 * b_last
    eb = jnp.exp(b)
    ea = jnp.exp(b - c)
    ek = jnp.exp(c - b)
    ed = jnp.exp(b_last - b)
    ebl = jnp.exp(b_last)
    return dict(sg=sg, f=f, k=k, sq=sq, q=q, eb=eb, ea=ea, ek=ek, ed=ed, ebl=ebl,
                qe=q * eb, qa=q * ea, ka=k * ek, kd=k * ed)


def _tri(lower):
    r = lax.broadcasted_iota(jnp.int32, (CH, CH), 0)
    c = lax.broadcasted_iota(jnp.int32, (CH, CH), 1)
    return (r >= c) if lower else (c >= r)


def _hgrn_fwd(proj, lbw, rb):
    t = proj.shape[0]
    ncb = rb // CH

    def body(p_ref, lb_ref, o_ref, st_ref, s_scr):
        @pl.when(pl.program_id(1) == 0)
        def _():
            s_scr[...] = jnp.zeros_like(s_scr)

        lb = _lower_bound(lb_ref)
        causal = _tri(True)
        tril = causal.astype(F32)

        def chunk(cc, carry):
            r0 = pl.multiple_of(cc * CH, CH)
            rows = pl.ds(r0, CH)
            e = _hgrn_chunk_fwd(p_ref[rows, 0:HD], p_ref[rows, HD:2 * HD], lb, tril)
            v = _bf(p_ref[rows, 2 * HD:3 * HD])
            st = s_scr[...]
            st_ref[cc, 0] = st
            a = jnp.where(causal, _mm_nt(_bf(e["qa"]), _bf(e["ka"])), 0.0)
            o_ref[rows, :] = _mm_nt(_bf(e["qe"]), _bf(st)) + _mm(_bf(a), v)
            s_scr[...] = st * e["ebl"] + _mm_tn(v, _bf(e["kd"]))
            return carry

        lax.fori_loop(0, ncb, chunk, 0)

    return pl.pallas_call(
        body, grid=(HEADS, t // rb),
        in_specs=[pl.BlockSpec((rb, 3 * HD), lambda h, i: (i, h)), pl.BlockSpec((2, HD), lambda h, i: (0, h))],
        out_specs=[pl.BlockSpec((rb, HD), lambda h, i: (i, h)),
                   pl.BlockSpec((ncb, 1, HD, HD), lambda h, i: (i, h, 0, 0))],
        out_shape=[jax.ShapeDtypeStruct((t, D), F32), jax.ShapeDtypeStruct((t // CH, HEADS, HD, HD), F32)],
        scratch_shapes=[pltpu.VMEM((HD, HD), F32)],
        name="hgrn_fwd", compiler_params=_cp(32),
    )(proj, lbw)


def _hgrn_bwd(proj, lbw, do_raw, states, rb):
    t = proj.shape[0]
    nblk = t // rb
    ncb = rb // CH

    def body(p_ref, lb_ref, do_ref, st_ref, dp_ref, dlb_ref, ds_scr):
        @pl.when(pl.program_id(1) == 0)
        def _():
            ds_scr[...] = jnp.zeros_like(ds_scr)
            dlb_ref[...] = jnp.zeros_like(dlb_ref)

        lb = _lower_bound(lb_ref)
        causal = _tri(True)
        tril = causal.astype(F32)
        triu = _tri(False).astype(F32)
        last_row = lax.broadcasted_iota(jnp.int32, (CH, HD), 0) == CH - 1
        row0 = lax.broadcasted_iota(jnp.int32, (8, HD), 0) == 0

        def chunk(it, carry):
            cc = ncb - 1 - it
            r0 = pl.multiple_of(cc * CH, CH)
            rows = pl.ds(r0, CH)
            hq = p_ref[rows, 0:HD]
            e = _hgrn_chunk_fwd(hq, p_ref[rows, HD:2 * HD], lb, tril)
            v = _bf(p_ref[rows, 2 * HD:3 * HD])
            do = _bf(do_ref[rows, :])
            st = st_ref[cc, 0]
            dst = ds_scr[...]
            dst_b = _bf(dst)
            qa_b, ka_b, qe_b, kd_b = _bf(e["qa"]), _bf(e["ka"]), _bf(e["qe"]), _bf(e["kd"])
            a = jnp.where(causal, _mm_nt(qa_b, ka_b), 0.0)
            da = _bf(jnp.where(causal, _mm_nt(do, v), 0.0))
            dqe = _mm(do, _bf(st))
            dv = _mm_tn(_bf(a), do) + _mm_nt(kd_b, dst_b)
            dqa = _mm(da, ka_b)
            dka = _mm_tn(da, qa_b)
            dkd = _mm(v, dst_b)
            dbl = e["ebl"] * jnp.sum(st * dst, axis=0, keepdims=True)
            ds_scr[...] = _mm_tn(do, qe_b) + dst * e["ebl"]
            dq = dqe * e["eb"] + dqa * e["ea"]
            dk = dka * e["ek"] + dkd * e["ed"]
            dkd_kd = dkd * kd_b.astype(F32)
            db = dqe * qe_b.astype(F32) + dqa * qa_b.astype(F32) - dka * ka_b.astype(F32) - dkd_kd
            db = db + jnp.where(last_row, dbl + jnp.sum(dkd_kd, axis=0, keepdims=True), 0.0)
            dg = _mm_exact(triu, db)
            df = dg / e["f"] - dk
            sg = e["sg"]
            sq = e["sq"]
            dp_ref[rows, 0:HD] = _bf(dq * (sq * (1.0 + hq * (1.0 - sq))))
            dp_ref[rows, HD:2 * HD] = _bf(df * (1.0 - lb) * sg * (1.0 - sg))
            dp_ref[rows, 2 * HD:3 * HD] = _bf(dv)
            dlb_ref[...] += jnp.where(row0, jnp.sum(df * (1.0 - sg), axis=0, keepdims=True), 0.0)
            return carry

        lax.fori_loop(0, ncb, chunk, 0)

    rev = lambda h, i: (nblk - 1 - i, h)
    return pl.pallas_call(
        body, grid=(HEADS, nblk),
        in_specs=[pl.BlockSpec((rb, 3 * HD), rev), pl.BlockSpec((2, HD), lambda h, i: (0, h)),
                  pl.BlockSpec((rb, HD), rev), pl.BlockSpec((ncb, 1, HD, HD), lambda h, i: (nblk - 1 - i, h, 0, 0))],
        out_specs=[pl.BlockSpec((rb, 3 * HD), rev), pl.BlockSpec((8, HD), lambda h, i: (0, h))],
        out_shape=[jax.ShapeDtypeStruct((t, 3 * D), BF16), jax.ShapeDtypeStruct((8, D), F32)],
        scratch_shapes=[pltpu.VMEM((HD, HD), F32)],
        name="hgrn_bwd", compiler_params=_cp(32),
    )(proj, lbw, do_raw, states)


def _kv_variants(tile, odd):
    low = lax.broadcasted_iota(jnp.int32, tile.shape, 1) < 64
    if odd:
        hi = jnp.where(low, 0.0, tile)
        lo = pltpu.roll(hi, 64, 1)
    else:
        lo = jnp.where(low, tile, 0.0)
        hi = pltpu.roll(lo, 64, 1)
    return _bf(lo), _bf(hi)


def _attn_valid(n):
    qi = lax.broadcasted_iota(jnp.int32, (AB, 2 * AB), 0)
    kj = lax.broadcasted_iota(jnp.int32, (AB, 2 * AB), 1)
    rel = qi + AB - kj
    return (rel >= 0) & (rel < AB) & ((kj >= AB) | (n > 0))


def _attn_probs(qp, kx, sink, valid):
    s = jnp.where(valid, _mm_nt(qp, kx) * ATT_SCALE, NEG)
    m = jnp.maximum(jnp.max(s, axis=-1, keepdims=True), sink)
    p = jnp.exp(s - m)
    es = jnp.exp(sink - m)
    inv = 1.0 / (jnp.sum(p, axis=-1, keepdims=True) + es)
    return p * inv, es * inv


def _attn_fwd(proj, sinks):
    t = proj.shape[0]
    nb = t // AB

    def body(q_ref, kc_ref, kp_ref, vc_ref, vp_ref, sink_ref, o_ref):
        n = pl.program_id(0)
        valid = _attn_valid(n)
        for tl in range(2):
            cols = slice(tl * 128, (tl + 1) * 128)
            kt = jnp.concatenate([kp_ref[:, cols], kc_ref[:, cols]], axis=0)
            vt = jnp.concatenate([vp_ref[:, cols], vc_ref[:, cols]], axis=0)
            for odd in range(2):
                g = 2 * tl + odd
                klo, khi = _kv_variants(kt, odd)
                vlo, vhi = _kv_variants(vt, odd)
                for i in range(2):
                    qc = slice(256 * g + 128 * i, 256 * g + 128 * (i + 1))
                    qp = _bf(q_ref[:, qc])
                    pa, _ = _attn_probs(qp, klo, sink_ref[0, 4 * g + 2 * i], valid)
                    pb, _ = _attn_probs(qp, khi, sink_ref[0, 4 * g + 2 * i + 1], valid)
                    o_ref[:, qc] = _mm(_bf(pa), vlo) + _mm(_bf(pb), vhi)

    prev = lambda n: jnp.maximum(n - 1, 0)
    return pl.pallas_call(
        body, grid=(nb,),
        in_specs=[pl.BlockSpec((AB, D), lambda n: (n, C_AQ // D)),
                  pl.BlockSpec((AB, 256), lambda n: (n, C_AK // 256)),
                  pl.BlockSpec((AB, 256), lambda n: (prev(n), C_AK // 256)),
                  pl.BlockSpec((AB, 256), lambda n: (n, C_AV // 256)),
                  pl.BlockSpec((AB, 256), lambda n: (prev(n), C_AV // 256)),
                  pl.BlockSpec(memory_space=pltpu.SMEM)],
        out_specs=pl.BlockSpec((AB, D), lambda n: (n, 0)),
        out_shape=jax.ShapeDtypeStruct((t, D), F32),
        name="attn_fwd", compiler_params=_cp(32),
    )(proj, proj, proj, proj, proj, sinks)


def _attn_bwd(proj, sinks, do_a):
    t = proj.shape[0]
    nb = t // AB

    def body(q_ref, kc_ref, kp_ref, vc_ref, vp_ref, do_ref, sink_ref, dq_ref, dkv_ref, dsink_ref, carry):
        n = pl.program_id(0)

        @pl.when(n == 0)
        def _():
            dsink_ref[...] = jnp.zeros_like(dsink_ref)
            carry[...] = jnp.zeros_like(carry)

        @pl.when(n < nb)
        def _():
            valid = _attn_valid(n)
            low = lax.broadcasted_iota(jnp.int32, (2 * AB, 128), 1) < 64
            lane = lax.broadcasted_iota(jnp.int32, (8, 128), 1)
            row0 = lax.broadcasted_iota(jnp.int32, (8, 128), 0) == 0
            dsink = jnp.zeros((8, 128), F32)
            for tl in range(2):
                cols = slice(tl * 128, (tl + 1) * 128)
                kt = jnp.concatenate([kp_ref[:, cols], kc_ref[:, cols]], axis=0)
                vt = jnp.concatenate([vp_ref[:, cols], vc_ref[:, cols]], axis=0)
                dkt = jnp.zeros((2 * AB, 128), F32)
                dvt = jnp.zeros((2 * AB, 128), F32)
                for odd in range(2):
                    g = 2 * tl + odd
                    kvar = _kv_variants(kt, odd)
                    vvar = _kv_variants(vt, odd)
                    dk_ab = [jnp.zeros((2 * AB, 128), F32), jnp.zeros((2 * AB, 128), F32)]
                    dv_ab = [jnp.zeros((2 * AB, 128), F32), jnp.zeros((2 * AB, 128), F32)]
                    for i in range(2):
                        qc = slice(256 * g + 128 * i, 256 * g + 128 * (i + 1))
                        qp = _bf(q_ref[:, qc])
                        dop = _bf(do_ref[:, qc])
                        dqp = jnp.zeros((AB, 128), F32)
                        for ab in range(2):
                            head = 4 * g + 2 * i + ab
                            pr, ps = _attn_probs(qp, kvar[ab], sink_ref[0, head], valid)
                            dp = _mm_nt(dop, vvar[ab])
                            rs = jnp.sum(pr * dp, axis=-1, keepdims=True)
                            ds = _bf(pr * (dp - rs))
                            dsink = dsink + jnp.where(row0 & (lane == head), -jnp.sum(ps * rs), 0.0)
                            dqp = dqp + _mm(ds, kvar[ab])
                            dk_ab[ab] = dk_ab[ab] + _mm_tn(ds, qp)
                            dv_ab[ab] = dv_ab[ab] + _mm_tn(_bf(pr), dop)
                        dq_ref[:, qc] = _bf(dqp * ATT_SCALE)
                    if odd:
                        dkt = dkt + jnp.where(low, 0.0, dk_ab[1]) + pltpu.roll(jnp.where(low, dk_ab[0], 0.0), 64, 1)
                        dvt = dvt + jnp.where(low, 0.0, dv_ab[1]) + pltpu.roll(jnp.where(low, dv_ab[0], 0.0), 64, 1)
                    else:
                        dkt = dkt + jnp.where(low, dk_ab[0], 0.0) + pltpu.roll(jnp.where(low, 0.0, dk_ab[1]), 64, 1)
                        dvt = dvt + jnp.where(low, dv_ab[0], 0.0) + pltpu.roll(jnp.where(low, 0.0, dv_ab[1]), 64, 1)
                dkt = dkt * ATT_SCALE
                kcols = slice(tl * 128, (tl + 1) * 128)
                vcols = slice(256 + tl * 128, 256 + (tl + 1) * 128)
                dkv_ref[:, kcols] = _bf(carry[:, kcols] + dkt[0:AB])
                dkv_ref[:, vcols] = _bf(carry[:, vcols] + dvt[0:AB])
                carry[:, kcols] = dkt[AB:2 * AB]
                carry[:, vcols] = dvt[AB:2 * AB]
            dsink_ref[...] += dsink

        @pl.when(n == nb)
        def _():
            dkv_ref[...] = _bf(carry[...])

    cur = lambda n: jnp.minimum(n, nb - 1)
    prev = lambda n: jnp.clip(n - 1, 0, nb - 1)
    return pl.pallas_call(
        body, grid=(nb + 1,),
        in_specs=[pl.BlockSpec((AB, D), lambda n: (cur(n), C_AQ // D)),
                  pl.BlockSpec((AB, 256), lambda n: (cur(n), C_AK // 256)),
                  pl.BlockSpec((AB, 256), lambda n: (prev(n), C_AK // 256)),
                  pl.BlockSpec((AB, 256), lambda n: (cur(n), C_AV // 256)),
                  pl.BlockSpec((AB, 256), lambda n: (prev(n), C_AV // 256)),
                  pl.BlockSpec((AB, D), lambda n: (cur(n), 0)),
                  pl.BlockSpec(memory_space=pltpu.SMEM)],
        out_specs=[pl.BlockSpec((AB, D), lambda n: (cur(n), 0)),
                   pl.BlockSpec((AB, 512), lambda n: (prev(n), 0)),
                   pl.BlockSpec((8, 128), lambda n: (0, 0))],
        out_shape=[jax.ShapeDtypeStruct((t, D), BF16), jax.ShapeDtypeStruct((t, 512), BF16),
                   jax.ShapeDtypeStruct((8, 128), F32)],
        scratch_shapes=[pltpu.VMEM((AB, 512), F32)],
        name="attn_bwd", compiler_params=_cp(32),
    )(proj, proj, proj, proj, proj, do_a, sinks)


def _silu_and_grad(v):
    s = _sigmoid(v)
    return v * s, s * (1.0 + v * (1.0 - s))


def _tail(o_raw, o_a, proj, x2d, tgt, wbh, wba, wout, hnw, fnw, tb):
    t = x2d.shape[0]

    def body(or_ref, oa_ref, hg_ref, ag_ref, mh_ref, ma_ref, x_ref, t_ref, wbh_ref, wba_ref, wout_ref, hnw_ref, fnw_ref,
             dx2_ref, dor_ref, doa_ref, dt_ref, ght_ref, gat_ref, mgt_ref, dyh_ref, dya_ref, dx2b_ref, sums_ref):
        @pl.when(pl.program_id(0) == 0)
        def _():
            sums_ref[...] = jnp.zeros_like(sums_ref)

        hnw_v = hnw_ref[...]
        fnw_v = fnw_ref[...]
        o = or_ref[...]
        rs, xhs = [], []
        for h in range(HEADS):
            oh = o[:, h * HD:(h + 1) * HD]
            r = lax.rsqrt(jnp.mean(oh * oh, axis=-1, keepdims=True) + EPS)
            rs.append(r)
            xhs.append(oh * r)
        xh = jnp.concatenate(xhs, axis=1)
        on = xh * hnw_v
        sil_hg, dsil_hg = _silu_and_grad(hg_ref[...])
        gh = on * sil_hg
        gh_b = _bf(gh)
        y_h = _mm(gh_b, wbh_ref[...])
        oa = oa_ref[...]
        sil_ag, dsil_ag = _silu_and_grad(ag_ref[...])
        ga_b = _bf(oa * sil_ag)
        y_a = _mm(ga_b, wba_ref[...])
        s_mh = _sigmoid(mh_ref[...])
        s_ma = _sigmoid(ma_ref[...])
        mg_b = _bf(s_mh * y_h + s_ma * y_a)
        x2 = x_ref[...] + _mm(mg_b, wout_ref[...])
        r2 = lax.rsqrt(jnp.mean(x2 * x2, axis=-1, keepdims=True) + EPS)
        xh2 = x2 * r2
        err = xh2 * fnw_v - t_ref[...]
        loss = 0.5 * jnp.sum(jnp.mean(err * err, axis=-1, keepdims=True))
        dy = err * (1.0 / D)
        dfnw = jnp.sum(dy * xh2, axis=0, keepdims=True)
        dxh2 = dy * fnw_v
        dx2 = r2 * (dxh2 - xh2 * jnp.mean(dxh2 * xh2, axis=-1, keepdims=True))
        dx2_ref[...] = dx2
        dx2_b = _bf(dx2)
        dmg = _mm_nt(dx2_b, wout_ref[...])
        dyh_b = _bf(dmg * s_mh)
        dya_b = _bf(dmg * s_ma)
        dt_ref[:, 2 * D:3 * D] = _bf(dmg * y_h * s_mh * (1.0 - s_mh))
        dt_ref[:, 3 * D:4 * D] = _bf(dmg * y_a * s_ma * (1.0 - s_ma))
        dgh = _mm_nt(dyh_b, wbh_ref[...])
        dga = _mm_nt(dya_b, wba_ref[...])
        doa_ref[...] = dga * sil_ag
        dt_ref[:, D:2 * D] = _bf(dga * oa * dsil_ag)
        dt_ref[:, 0:D] = _bf(dgh * on * dsil_hg)
        don = dgh * sil_hg
        dhnw = jnp.sum(don * xh, axis=0, keepdims=True)
        dxh = don * hnw_v
        dos = []
        for h in range(HEADS):
            sl = slice(h * HD, (h + 1) * HD)
            dos.append(rs[h] * (dxh[:, sl] - xhs[h] * jnp.mean(dxh[:, sl] * xhs[h], axis=-1, keepdims=True)))
        dor_ref[...] = jnp.concatenate(dos, axis=1)
        ght_ref[...] = _bf(gh.T)
        gat_ref[...] = _bf((oa * sil_ag).T)
        mgt_ref[...] = _bf((s_mh * y_h + s_ma * y_a).T)
        dyh_ref[...] = dyh_b
        dya_ref[...] = dya_b
        dx2b_ref[...] = dx2_b
        row = lax.broadcasted_iota(jnp.int32, (8, D), 0)
        sums_ref[...] += jnp.where(row == 0, dfnw, 0.0) + jnp.where(row == 1, dhnw, 0.0) + jnp.where(row == 2, loss, 0.0)

    rowblk = lambda c: pl.BlockSpec((tb, D), lambda i: (i, c))
    full = lambda shape: pl.BlockSpec(shape, lambda i: (0, 0))
    tblk = pl.BlockSpec((D, tb), lambda i: (0, i))
    return pl.pallas_call(
        body, grid=(t // tb,),
        in_specs=[rowblk(0), rowblk(0), rowblk(C_HG // D), rowblk(C_AG // D), rowblk(C_MH // D), rowblk(C_MA // D),
                  rowblk(0), rowblk(0), full((D, D)), full((D, D)), full((D, D)), full((1, D)), full((1, D))],
        out_specs=[rowblk(0), rowblk(0), rowblk(0), pl.BlockSpec((tb, 4 * D), lambda i: (i, 0)),
                   tblk, tblk, tblk, rowblk(0), rowblk(0), rowblk(0), full((8, D))],
        out_shape=[jax.ShapeDtypeStruct((t, D), F32)] * 3 + [jax.ShapeDtypeStruct((t, 4 * D), BF16)]
        + [jax.ShapeDtypeStruct((D, t), BF16)] * 3 + [jax.ShapeDtypeStruct((t, D), BF16)] * 3
        + [jax.ShapeDtypeStruct((8, D), F32)],
        name="tail", compiler_params=_cp(56),
    )(o_raw, o_a, proj, proj, proj, proj, x2d, tgt, wbh, wba, wout, hnw, fnw)


def _wgrad3(ght, dyh, gat, dya, mgt, dx2b, tk):
    t = dyh.shape[0]

    def body(a0, b0, a1, b1, a2, b2, o0, o1, o2):
        @pl.when(pl.program_id(0) == 0)
        def _():
            o0[...] = jnp.zeros_like(o0)
            o1[...] = jnp.zeros_like(o1)
            o2[...] = jnp.zeros_like(o2)

        o0[...] += _mm(a0[...], b0[...])
        o1[...] += _mm(a1[...], b1[...])
        o2[...] += _mm(a2[...], b2[...])

    lhs = pl.BlockSpec((D, tk), lambda k: (0, k))
    rhs = pl.BlockSpec((tk, D), lambda k: (k, 0))
    out = pl.BlockSpec((D, D), lambda k: (0, 0))
    return pl.pallas_call(
        body, grid=(t // tk,), in_specs=[lhs, rhs] * 3, out_specs=[out] * 3,
        out_shape=[jax.ShapeDtypeStruct((D, D), F32)] * 3,
        name="wgrad3", compiler_params=_cp(48),
    )(ght, dyh, gat, dya, mgt, dx2b)


def _piece_specs(tb, row_of, col_of):
    specs = []
    for first, count in PIECES:
        def index(a, b, first=first, count=count):
            i, j = row_of(a, b), col_of(a, b)
            inside = (j >= first) & (j < first + count)
            return (jnp.where(inside, i, 0), jnp.clip(j - first, 0, count - 1))
        specs.append(pl.BlockSpec((tb, CB), index))
    return specs


def _inproj_wgrad(xnt, pieces, tb):
    t = xnt.shape[1]

    def body(xnt_ref, p0, p1, p2, p3, o_ref):
        j = pl.program_id(0)

        @pl.when(pl.program_id(1) == 0)
        def _():
            o_ref[...] = jnp.zeros_like(o_ref)

        for (first, count), p in zip(PIECES, (p0, p1, p2, p3)):
            @pl.when((j >= first) & (j < first + count))
            def _(p=p):
                o_ref[...] += _mm(xnt_ref[...], p[...])

    return pl.pallas_call(
        body, grid=(DIN // CB, t // tb),
        in_specs=[pl.BlockSpec((D, tb), lambda j, i: (0, i))] + _piece_specs(tb, lambda j, i: i, lambda j, i: j),
        out_specs=pl.BlockSpec((D, CB), lambda j, i: (0, j)),
        out_shape=jax.ShapeDtypeStruct((D, DIN), F32),
        name="inproj_wgrad", compiler_params=_cp(48),
    )(xnt, *pieces)


def _inproj_dgrad(pieces, w_p, x2d, dx2, norm_w, tb):
    t = x2d.shape[0]
    ncb = DIN // CB

    def body(p0, p1, p2, p3, w_ref, x_ref, dx2_ref, nw_ref, gx_ref, dnw_ref, acc):
        i = pl.program_id(0)
        j = pl.program_id(1)

        @pl.when((i == 0) & (j == 0))
        def _():
            dnw_ref[...] = jnp.zeros_like(dnw_ref)

        @pl.when(j == 0)
        def _():
            acc[...] = jnp.zeros_like(acc)

        for (first, count), p in zip(PIECES, (p0, p1, p2, p3)):
            @pl.when((j >= first) & (j < first + count))
            def _(p=p):
                acc[...] += _mm_nt(p[...], w_ref[...])

        @pl.when(j == ncb - 1)
        def _():
            xv = x_ref[...]
            r = lax.rsqrt(jnp.mean(xv * xv, axis=-1, keepdims=True) + EPS)
            xh = xv * r
            dxn = acc[...]
            dxh = dxn * nw_ref[...]
            gx_ref[...] = dx2_ref[...] + r * (dxh - xh * jnp.mean(dxh * xh, axis=-1, keepdims=True))
            row0 = lax.broadcasted_iota(jnp.int32, (8, D), 0) == 0
            dnw_ref[...] += jnp.where(row0, jnp.sum(dxn * xh, axis=0, keepdims=True), 0.0)

    rowblk = pl.BlockSpec((tb, D), lambda i, j: (i, 0))
    return pl.pallas_call(
        body, grid=(t // tb, ncb),
        in_specs=_piece_specs(tb, lambda i, j: i, lambda i, j: j)
        + [pl.BlockSpec((D, CB), lambda i, j: (0, j)), rowblk, rowblk, pl.BlockSpec((1, D), lambda i, j: (0, 0))],
        out_specs=[rowblk, pl.BlockSpec((8, D), lambda i, j: (0, 0))],
        out_shape=[jax.ShapeDtypeStruct((t, D), F32), jax.ShapeDtypeStruct((8, D), F32)],
        scratch_shapes=[pltpu.VMEM((tb, D), F32)],
        name="inproj_dgrad", compiler_params=_cp(48),
    )(*pieces, w_p, x2d, dx2, norm_w)


def _adamw_math(w, g, m, v):
    m = B1 * m + (1.0 - B1) * g
    v = B2 * v + (1.0 - B2) * (g * g)
    m_hat = m / (1.0 - B1 ** STEP)
    v_hat = v / (1.0 - B2 ** STEP)
    delta = -LR * (m_hat / (jnp.sqrt(v_hat) + ADAM_EPS) + WD * w)
    return delta, m, v


def _adamw_shard(parts, w, m, v, rows, name):
    _, nr, nc = parts.shape

    def body(p_ref, w_ref, m_ref, v_ref, g_ref, d_ref, nm_ref, nv_ref):
        g = p_ref[0]
        for s in range(1, NDEV):
            g = g + p_ref[s]
        d, nm, nv = _adamw_math(w_ref[...], g, m_ref[...], v_ref[...])
        g_ref[...] = g
        d_ref[...] = d
        nm_ref[...] = nm
        nv_ref[...] = nv

    blk = pl.BlockSpec((rows, nc), lambda i: (i, 0))
    return pl.pallas_call(
        body, grid=(nr // rows,),
        in_specs=[pl.BlockSpec((NDEV, rows, nc), lambda i: (0, i, 0)), blk, blk, blk],
        out_specs=[blk] * 4, out_shape=[jax.ShapeDtypeStruct((nr, nc), F32)] * 4,
        name=name, compiler_params=_cp(48),
    )(parts, w, m, v)


def _adamw_small(parts, w, m, v):
    def body(p_ref, w_ref, m_ref, v_ref, g_ref, d_ref, nm_ref, nv_ref):
        g = p_ref[0]
        for s in range(1, NDEV):
            g = g + p_ref[s]
        g_ref[...] = g
        lb = _lower_bound(w_ref.at[1:3, :])
        ga0 = g_ref[1:2, :] * lb * (1.0 - lb)
        g_ref[1:2, :] = ga0
        g_ref[2:3, :] = -ga0
        d, nm, nv = _adamw_math(w_ref[...], g_ref[...], m_ref[...], v_ref[...])
        d_ref[...] = d
        nm_ref[...] = nm
        nv_ref[...] = nv

    return pl.pallas_call(
        body, out_shape=[jax.ShapeDtypeStruct((8, D), F32)] * 4, name="adamw_small",
    )(parts, w, m, v)


def _to_internal_cols(w):
    hg = w[:, 0:3072].reshape(D, 3, HEADS, HD).transpose(0, 2, 1, 3).reshape(D, 3072)
    return jnp.concatenate([hg, w[:, 4096:5120], w[:, 3072:4096], w[:, 5632:8704], w[:, 5120:5632]], axis=1)


def _to_reference_cols(w):
    hg = w[:, 0:3072].reshape(D, HEADS, 3, HD).transpose(0, 2, 1, 3).reshape(D, 3072)
    return jnp.concatenate([hg, w[:, C_HG:C_HG + D], w[:, C_AQ:C_AQ + D], w[:, C_AK:DIN], w[:, C_AG:C_AK]], axis=1)


def _pack_small(rows):
    out = [jnp.pad(r.reshape(1, -1), ((0, 0), (0, D - r.size))) for r in rows]
    out.append(jnp.zeros((8 - len(rows), D), F32))
    return jnp.concatenate(out, axis=0)


def kernel(x, norm_w, w_in, hgrn_lower_bound, hgrn_norm_w, w_branch_hgrn, attn_sinks, w_branch_attn, w_out, final_norm_w, loss_target, m_norm_w, m_w_in, m_hgrn_lower_bound, m_hgrn_norm_w, m_w_branch_hgrn, m_attn_sinks, m_w_branch_attn, m_w_out, m_final_norm_w, v_norm_w, v_w_in, v_hgrn_lower_bound, v_hgrn_norm_w, v_w_branch_hgrn, v_attn_sinks, v_w_branch_attn, v_w_out, v_final_norm_w):
    t = x.shape[1]
    x2d = x.reshape(t, D)
    tgt = loss_target.reshape(t, D)
    fnw = final_norm_w.reshape(1, D)
    row_blk = min(256, t)
    big_blk = min(512, t)

    win_g, wbh_g, wba_g, wout_g = _gather_weights(w_in[0], w_branch_hgrn[0], w_branch_attn[0], w_out[0])
    w_p = _to_internal_cols(win_g.transpose(1, 0, 2).reshape(D, DIN))
    wbh, wba, wout = wbh_g.reshape(D, D), wba_g.reshape(D, D), wout_g.reshape(D, D)

    proj, xnt = _inproj_fwd(x2d, norm_w, w_p, big_blk, 2176)
    o_raw, states = _hgrn_fwd(proj, hgrn_lower_bound, big_blk)
    o_a = _attn_fwd(proj, attn_sinks)
    (dx2, do_raw, do_a, d_tail, ght, gat, mgt, dyh, dya, dx2b, sums) = _tail(
        o_raw, o_a, proj, x2d, tgt, wbh, wba, wout, hgrn_norm_w, fnw, row_blk)
    dwbh, dwba, dwout = _wgrad3(ght, dyh, gat, dya, mgt, dx2b, big_blk)
    d_aq, d_kv, dsink = _attn_bwd(proj, attn_sinks, do_a)
    d_hgrn, dlb = _hgrn_bwd(proj, hgrn_lower_bound, do_raw, states, big_blk)
    pieces = (d_hgrn, d_aq, d_tail, d_kv)
    dw_p = _inproj_wgrad(xnt, pieces, big_blk)
    grad_x, dnw = _inproj_dgrad(pieces, w_p, x2d, dx2, norm_w, big_blk)

    dwin_r = _to_reference_cols(dw_p).reshape(D, NDEV, IN_SHARD).transpose(1, 0, 2)
    small = _pack_small([dnw[0:1], dlb[0:1], jnp.zeros((1, D), F32), sums[1:2], sums[0:1], dsink[0:1, 0:QH]])
    rin, rbh, rba, rout, rsm = _exchange_grads(
        dwin_r, dwbh.reshape(NDEV, ROW_SHARD, D), dwba.reshape(NDEV, ROW_SHARD, D), dwout.reshape(NDEV, ROW_SHARD, D), small)
    g_in, d_in, nm_in, nv_in = _adamw_shard(rin, w_in[0], m_w_in[0], v_w_in[0], 128, "adamw_w_in")
    g_bh, d_bh, nm_bh, nv_bh = _adamw_shard(rbh, w_branch_hgrn[0], m_w_branch_hgrn[0], v_w_branch_hgrn[0], 128, "adamw_w_bh")
    g_ba, d_ba, nm_ba, nv_ba = _adamw_shard(rba, w_branch_attn[0], m_w_branch_attn[0], v_w_branch_attn[0], 128, "adamw_w_ba")
    g_out, d_out, nm_out, nv_out = _adamw_shard(rout, w_out[0], m_w_out[0], v_w_out[0], 128, "adamw_w_out")
    pack = lambda a, b, c, d, e: _pack_small([a, b[0:1], b[1:2], c, d.reshape(1, D), e])
    small_out = _adamw_small(
        rsm,
        pack(norm_w, hgrn_lower_bound, hgrn_norm_w, final_norm_w, attn_sinks),
        pack(m_norm_w, m_hgrn_lower_bound, m_hgrn_norm_w, m_final_norm_w, m_attn_sinks),
        pack(v_norm_w, v_hgrn_lower_bound, v_hgrn_norm_w, v_final_norm_w, v_attn_sinks))

    loss = lax.psum(sums[2, 0], ("x", "y", "c"))

    def unpack(p):
        return dict(norm_w=p[0:1], lb=p[1:3], hnw=p[3:4], fnw=p[4], sinks=p[5:6, 0:QH])

    sg, sd, sm, sv = (unpack(p) for p in small_out)

    def group(s, w_in_v, bh, ba, out):
        return (s["norm_w"], w_in_v[None], s["lb"], s["hnw"], bh[None], s["sinks"], ba[None], out[None], s["fnw"])

    return (loss, grad_x.reshape(1, t, D),
            *group(sg, g_in, g_bh, g_ba, g_out), *group(sd, d_in, d_bh, d_ba, d_out),
            *group(sm, nm_in, nm_bh, nm_ba, nm_out), *group(sv, nv_in, nv_bh, nv_ba, nv_out))
```

```python
import functools

import jax
import jax.numpy as jnp
from jax import lax
from jax.experimental import pallas as pl
from jax.experimental.pallas import tpu as pltpu

F32 = jnp.float32
BF16 = jnp.bfloat16

D = 1024
DIN = 8704
NDEV = 8
IN_SHARD = DIN // NDEV
ROW_SHARD = D // NDEV
HEADS = 8
HD = 128
CH = 64
HGRN_GROUP = 8
QH = 16
AB = 128
EPS = 1e-6
NEG = -1e30
ATT_SCALE = 0.125

C_HGRN = 0
C_AQ = 3072
C_HG = 4096
C_AG = 5120
C_MH = 6144
C_MA = 7168
C_AK = 8192
C_AV = 8448
CB = 512
PIECES = ((0, 6), (6, 2), (8, 8), (16, 1))

LR = 0.001
B1 = 0.9
B2 = 0.999
ADAM_EPS = 1e-08
WD = 0.01
STEP = 10

V7X_VMEM_BYTES = 64 * 1024 * 1024
MESH = pl.DeviceIdType.MESH


def _cp(vmem_mb):
    return pltpu.CompilerParams(vmem_limit_bytes=vmem_mb * 1024 * 1024)


def _mm(a, b):
    return jnp.dot(a, b, preferred_element_type=F32)


def _mm_nt(a, b):
    return lax.dot_general(a, b, (((1,), (1,)), ((), ())), preferred_element_type=F32)


def _mm_tn(a, b):
    return lax.dot_general(a, b, (((0,), (0,)), ((), ())), preferred_element_type=F32)


def _mm_exact(a, b):
    return jnp.dot(a, b, preferred_element_type=F32, precision=lax.Precision.HIGHEST)


def _sigmoid(v):
    return 1.0 / (1.0 + jnp.exp(-v))


def _bf(v):
    return v.astype(BF16)


def _peer(r):
    x, y, c = lax.axis_index("x"), lax.axis_index("y"), lax.axis_index("c")
    px = 1 - x if r & 4 else x
    py = 1 - y if r & 2 else y
    pc = 1 - c if r & 1 else c
    return (px, py, pc), 4 * px + 2 * py + pc


def _gather_weights(w_in_s, wbh_s, wba_s, wout_s):
    def body(win_ref, wbh_ref, wba_ref, wout_ref, win_g, wbh_g, wba_g, wout_g, sin, s3, send_sems, recv_sems, loc_sems):
        _, me = _peer(0)
        sin[...] = win_ref[...].astype(BF16)
        s3[0] = wbh_ref[...].astype(BF16)
        s3[1] = wba_ref[...].astype(BF16)
        s3[2] = wout_ref[...].astype(BF16)
        srcs = [sin, s3.at[0], s3.at[1], s3.at[2]]
        outs = [win_g, wbh_g, wba_g, wout_g]
        local = [pltpu.make_async_copy(srcs[k], outs[k].at[me], loc_sems.at[k]) for k in range(4)]
        for cp in local:
            cp.start()
        sends = []
        for r in range(1, NDEV):
            peer, _ = _peer(r)
            for k in range(4):
                cp = pltpu.make_async_remote_copy(
                    src_ref=srcs[k], dst_ref=outs[k].at[me], send_sem=send_sems.at[r - 1, k],
                    recv_sem=recv_sems.at[r - 1, k], device_id=peer, device_id_type=MESH)
                cp.start()
                sends.append(cp)
        for r in range(1, NDEV):
            peer, pidx = _peer(r)
            for k in range(4):
                pltpu.make_async_remote_copy(
                    src_ref=srcs[k], dst_ref=outs[k].at[pidx], send_sem=send_sems.at[r - 1, k],
                    recv_sem=recv_sems.at[r - 1, k], device_id=peer, device_id_type=MESH).wait_recv()
        for cp in sends:
            cp.wait_send()
        for cp in local:
            cp.wait()

    vm = pl.BlockSpec(memory_space=pltpu.VMEM)
    hbm = pl.BlockSpec(memory_space=pl.ANY)
    return pl.pallas_call(
        body,
        out_shape=[jax.ShapeDtypeStruct((NDEV, D, IN_SHARD), BF16)] + [jax.ShapeDtypeStruct((NDEV, ROW_SHARD, D), BF16)] * 3,
        in_specs=[vm, vm, vm, vm],
        out_specs=[hbm, hbm, hbm, hbm],
        scratch_shapes=[pltpu.VMEM((D, IN_SHARD), BF16), pltpu.VMEM((3, ROW_SHARD, D), BF16),
                        pltpu.SemaphoreType.DMA((NDEV - 1, 4)), pltpu.SemaphoreType.DMA((NDEV - 1, 4)),
                        pltpu.SemaphoreType.DMA((4,))],
        name="gather_weights", compiler_params=_cp(32),
    )(w_in_s, wbh_s, wba_s, wout_s)


def _exchange_grads(dwin_r, dwbh_r, dwba_r, dwout_r, small):
    arrs = [dwin_r, dwbh_r, dwba_r, dwout_r]
    na = len(arrs) + 1

    def body(a0, a1, a2, a3, sm, r0, r1, r2, r3, rsm, send_sems, recv_sems, loc_sems):
        ins = [a0, a1, a2, a3]
        outs = [r0, r1, r2, r3]
        _, me = _peer(0)
        local = [pltpu.make_async_copy(ins[k].at[me], outs[k].at[me], loc_sems.at[k]) for k in range(4)]
        local.append(pltpu.make_async_copy(sm, rsm.at[me], loc_sems.at[4]))
        for cp in local:
            cp.start()
        sends = []
        for r in range(1, NDEV):
            peer, pidx = _peer(r)
            for k in range(na):
                src = ins[k].at[pidx] if k < 4 else sm
                dst = outs[k].at[me] if k < 4 else rsm.at[me]
                cp = pltpu.make_async_remote_copy(
                    src_ref=src, dst_ref=dst, send_sem=send_sems.at[r - 1, k], recv_sem=recv_sems.at[r - 1, k],
                    device_id=peer, device_id_type=MESH)
                cp.start()
                sends.append(cp)
        for r in range(1, NDEV):
            peer, pidx = _peer(r)
            for k in range(na):
                src = ins[k].at[pidx] if k < 4 else sm
                dst = outs[k].at[pidx] if k < 4 else rsm.at[pidx]
                pltpu.make_async_remote_copy(
                    src_ref=src, dst_ref=dst, send_sem=send_sems.at[r - 1, k], recv_sem=recv_sems.at[r - 1, k],
                    device_id=peer, device_id_type=MESH).wait_recv()
        for cp in sends:
            cp.wait_send()
        for cp in local:
            cp.wait()

    hbm = pl.BlockSpec(memory_space=pl.ANY)
    return pl.pallas_call(
        body,
        out_shape=[jax.ShapeDtypeStruct(a.shape, a.dtype) for a in arrs] + [jax.ShapeDtypeStruct((NDEV,) + small.shape, F32)],
        in_specs=[hbm] * na,
        out_specs=[hbm] * na,
        scratch_shapes=[pltpu.SemaphoreType.DMA((NDEV - 1, na)), pltpu.SemaphoreType.DMA((NDEV - 1, na)),
                        pltpu.SemaphoreType.DMA((na,))],
        name="exchange_grads",
    )(*arrs, small)


def _inproj_fwd(x2d, norm_w, w_p, tb, nb):
    t = x2d.shape[0]

    def body(x_ref, nw_ref, w_ref, proj_ref, xnt_ref, xn_s):
        @pl.when(pl.program_id(1) == 0)
        def _():
            xv = x_ref[...]
            r = lax.rsqrt(jnp.mean(xv * xv, axis=-1, keepdims=True) + EPS)
            xn = (xv * r) * nw_ref[...]
            xn_s[...] = xn.astype(BF16)
            xnt_ref[...] = xn.T.astype(BF16)

        proj_ref[...] = _mm(xn_s[...], w_ref[...])

    return pl.pallas_call(
        body, grid=(t // tb, DIN // nb),
        in_specs=[pl.BlockSpec((tb, D), lambda i, j: (i, 0)), pl.BlockSpec((1, D), lambda i, j: (0, 0)),
                  pl.BlockSpec((D, nb), lambda i, j: (0, j))],
        out_specs=[pl.BlockSpec((tb, nb), lambda i, j: (i, j)), pl.BlockSpec((D, tb), lambda i, j: (0, i))],
        out_shape=[jax.ShapeDtypeStruct((t, DIN), F32), jax.ShapeDtypeStruct((D, t), BF16)],
        scratch_shapes=[pltpu.VMEM((tb, D), BF16)],
        name="inproj_fwd", compiler_params=_cp(48),
    )(x2d, norm_w, w_p)


def _lower_bound(lb_ref):
    a0 = lb_ref[0:1, :]
    a1 = lb_ref[1:2, :]
    mx = jnp.maximum(a0, a1)
    e0 = jnp.exp(a0 - mx)
    e1 = jnp.exp(a1 - mx)
    return e0 / (e0 + e1)


def _hgrn_chunk_fwd(hq, hf, lb, tril):
    sg = _sigmoid(hf)
    f = lb + (1.0 - lb) * sg
    g = jnp.log(f)
    k = 1.0 - f
    sq = _sigmoid(hq)
    q = hq * sq
    b = _mm_exact(tril, g)
    last_row = lax.broadcasted_iota(jnp.int32, b.shape, 0) == CH - 1
    b_last = jnp.sum(jnp.where(last_row, b, 0.0), axis=0, keepdims=True)
    c = 0.5 * b_last
    eb = jnp.exp(b)
    ea = jnp.exp(b - c)
    ek = jnp.exp(c - b)
    ed = jnp.exp(b_last - b)
    ebl = jnp.exp(b_last)
    return dict(sg=sg, f=f, k=k, sq=sq, q=q, eb=eb, ea=ea, ek=ek, ed=ed, ebl=ebl,
                qe=q * eb, qa=q * ea, ka=k * ek, kd=k * ed)


def _tri(lower):
    r = lax.broadcasted_iota(jnp.int32, (CH, CH), 0)
    c = lax.broadcasted_iota(jnp.int32, (CH, CH), 1)
    return (r >= c) if lower else (c >= r)


def _head_segment(p_ref, rows, j, hg):
    return jnp.concatenate([p_ref[rows, 3 * HD * k + j * HD:3 * HD * k + (j + 1) * HD] for k in range(hg)], axis=1)


def _head(a, k):
    return a[:, k * HD:(k + 1) * HD]


def _hgrn_fwd(proj, lbw, rb, hg):
    t = proj.shape[0]
    ncb = rb // CH

    def body(p_ref, lb_ref, o_ref, st_ref, s_scr):
        @pl.when(pl.program_id(1) == 0)
        def _():
            s_scr[...] = jnp.zeros_like(s_scr)

        lb = _lower_bound(lb_ref)
        causal = _tri(True)
        tril = causal.astype(F32)
        heads = range(hg)

        def chunk(cc, carry):
            r0 = pl.multiple_of(cc * CH, CH)
            rows = pl.ds(r0, CH)
            e = _hgrn_chunk_fwd(_head_segment(p_ref, rows, 0, hg), _head_segment(p_ref, rows, 1, hg), lb, tril)
            v = _bf(_head_segment(p_ref, rows, 2, hg))
            sts = [s_scr[k] for k in heads]
            qa, ka, qe, kd = _bf(e["qa"]), _bf(e["ka"]), _bf(e["qe"]), _bf(e["kd"])
            a = [_bf(jnp.where(causal, _mm_nt(_head(qa, k), _head(ka, k)), 0.0)) for k in heads]
            o_inter = [_mm_nt(_head(qe, k), _bf(sts[k])) for k in heads]
            kv = [_mm_tn(_head(v, k), _head(kd, k)) for k in heads]
            o_intra = [_mm(a[k], _head(v, k)) for k in heads]
            for k in heads:
                st_ref[cc, k] = sts[k]
                o_ref[rows, k * HD:(k + 1) * HD] = o_inter[k] + o_intra[k]
                s_scr[k] = sts[k] * _head(e["ebl"], k) + kv[k]
            return carry

        lax.fori_loop(0, ncb, chunk, 0)

    return pl.pallas_call(
        body, grid=(HEADS // hg, t // rb),
        in_specs=[pl.BlockSpec((rb, 3 * HD * hg), lambda h, i: (i, h)), pl.BlockSpec((2, HD * hg), lambda h, i: (0, h))],
        out_specs=[pl.BlockSpec((rb, HD * hg), lambda h, i: (i, h)),
                   pl.BlockSpec((ncb, hg, HD, HD), lambda h, i: (i, h, 0, 0))],
        out_shape=[jax.ShapeDtypeStruct((t, D), F32), jax.ShapeDtypeStruct((t // CH, HEADS, HD, HD), F32)],
        scratch_shapes=[pltpu.VMEM((hg, HD, HD), F32)],
        name="hgrn_fwd", compiler_params=_cp(48),
    )(proj, lbw)


def _hgrn_bwd(proj, lbw, do_raw, states, rb, hg):
    t = proj.shape[0]
    nblk = t // rb
    ncb = rb // CH

    def body(p_ref, lb_ref, do_ref, st_ref, dp_ref, dlb_ref, ds_scr):
        @pl.when(pl.program_id(1) == 0)
        def _():
            ds_scr[...] = jnp.zeros_like(ds_scr)
            dlb_ref[...] = jnp.zeros_like(dlb_ref)

        lb = _lower_bound(lb_ref)
        causal = _tri(True)
        tril = causal.astype(F32)
        triu = _tri(False).astype(F32)
        last_row = lax.broadcasted_iota(jnp.int32, (CH, HD * hg), 0) == CH - 1
        row0 = lax.broadcasted_iota(jnp.int32, (8, HD * hg), 0) == 0
        heads = range(hg)
        wide = lambda parts: jnp.concatenate(parts, axis=1)

        def chunk(it, carry):
            cc = ncb - 1 - it
            r0 = pl.multiple_of(cc * CH, CH)
            rows = pl.ds(r0, CH)
            hq = _head_segment(p_ref, rows, 0, hg)
            e = _hgrn_chunk_fwd(hq, _head_segment(p_ref, rows, 1, hg), lb, tril)
            v = _bf(_head_segment(p_ref, rows, 2, hg))
            do = _bf(do_ref[rows, :])
            sts = [st_ref[cc, k] for k in heads]
            dsts = [ds_scr[k] for k in heads]
            dlb_acc = dlb_ref[...]
            qa, ka, qe, kd = _bf(e["qa"]), _bf(e["ka"]), _bf(e["qe"]), _bf(e["kd"])
            a = [_bf(jnp.where(causal, _mm_nt(_head(qa, k), _head(ka, k)), 0.0)) for k in heads]
            da = [_bf(jnp.where(causal, _mm_nt(_head(do, k), _head(v, k)), 0.0)) for k in heads]
            dqe = wide([_mm(_head(do, k), _bf(sts[k])) for k in heads])
            dkd = wide([_mm(_head(v, k), _bf(dsts[k])) for k in heads])
            dv_state = [_mm_nt(_head(kd, k), _bf(dsts[k])) for k in heads]
            ds_new = [_mm_tn(_head(do, k), _head(qe, k)) for k in heads]
            dv_intra = [_mm_tn(a[k], _head(do, k)) for k in heads]
            dqa = wide([_mm(da[k], _head(ka, k)) for k in heads])
            dka = wide([_mm_tn(da[k], _head(qa, k)) for k in heads])
            dv = wide([dv_intra[k] + dv_state[k] for k in heads])
            dbl = e["ebl"] * wide([jnp.sum(sts[k] * dsts[k], axis=0, keepdims=True) for k in heads])
            dq = dqe * e["eb"] + dqa * e["ea"]
            dk = dka * e["ek"] + dkd * e["ed"]
            dkd_kd = dkd * kd.astype(F32)
            db = dqe * qe.astype(F32) + dqa * qa.astype(F32) - dka * ka.astype(F32) - dkd_kd
            db = db + jnp.where(last_row, dbl + jnp.sum(dkd_kd, axis=0, keepdims=True), 0.0)
            dg = _mm_exact(triu, db)
            df = dg / e["f"] - dk
            sg = e["sg"]
            sq = e["sq"]
            dhq = _bf(dq * (sq * (1.0 + hq * (1.0 - sq))))
            dhf = _bf(df * (1.0 - lb) * sg * (1.0 - sg))
            dhi = _bf(dv)
            dlb_new = dlb_acc + jnp.where(row0, jnp.sum(df * (1.0 - sg), axis=0, keepdims=True), 0.0)
            for k in heads:
                c0 = 3 * HD * k
                ds_scr[k] = ds_new[k] + dsts[k] * _head(e["ebl"], k)
                dp_ref[rows, c0:c0 + HD] = _head(dhq, k)
                dp_ref[rows, c0 + HD:c0 + 2 * HD] = _head(dhf, k)
                dp_ref[rows, c0 + 2 * HD:c0 + 3 * HD] = _head(dhi, k)
            dlb_ref[...] = dlb_new
            return carry

        lax.fori_loop(0, ncb, chunk, 0)

    rev = lambda h, i: (nblk - 1 - i, h)
    return pl.pallas_call(
        body, grid=(HEADS // hg, nblk),
        in_specs=[pl.BlockSpec((rb, 3 * HD * hg), rev), pl.BlockSpec((2, HD * hg), lambda h, i: (0, h)),
                  pl.BlockSpec((rb, HD * hg), rev), pl.BlockSpec((ncb, hg, HD, HD), lambda h, i: (nblk - 1 - i, h, 0, 0))],
        out_specs=[pl.BlockSpec((rb, 3 * HD * hg), rev), pl.BlockSpec((8, HD * hg), lambda h, i: (0, h))],
        out_shape=[jax.ShapeDtypeStruct((t, 3 * D), BF16), jax.ShapeDtypeStruct((8, D), F32)],
        scratch_shapes=[pltpu.VMEM((hg, HD, HD), F32)],
        name="hgrn_bwd", compiler_params=_cp(48),
    )(proj, lbw, do_raw, states)


def _kv_variants(tile, odd):
    low = lax.broadcasted_iota(jnp.int32, tile.shape, 1) < 64
    if odd:
        hi = jnp.where(low, 0.0, tile)
        lo = pltpu.roll(hi, 64, 1)
    else:
        lo = jnp.where(low, tile, 0.0)
        hi = pltpu.roll(lo, 64, 1)
    return _bf(lo), _bf(hi)


def _attn_valid(n):
    qi = lax.broadcasted_iota(jnp.int32, (AB, 2 * AB), 0)
    kj = lax.broadcasted_iota(jnp.int32, (AB, 2 * AB), 1)
    rel = qi + AB - kj
    return (rel >= 0) & (rel < AB) & ((kj >= AB) | (n > 0))


def _attn_probs(qp, kx, sink, valid):
    s = jnp.where(valid, _mm_nt(qp, kx) * ATT_SCALE, NEG)
    m = jnp.maximum(jnp.max(s, axis=-1, keepdims=True), sink)
    p = jnp.exp(s - m)
    es = jnp.exp(sink - m)
    inv = 1.0 / (jnp.sum(p, axis=-1, keepdims=True) + es)
    return p * inv, es * inv


def _attn_fwd(proj, sinks):
    t = proj.shape[0]
    nb = t // AB

    def body(q_ref, kc_ref, kp_ref, vc_ref, vp_ref, sink_ref, o_ref):
        n = pl.program_id(0)
        valid = _attn_valid(n)
        for tl in range(2):
            cols = slice(tl * 128, (tl + 1) * 128)
            kt = jnp.concatenate([kp_ref[:, cols], kc_ref[:, cols]], axis=0)
            vt = jnp.concatenate([vp_ref[:, cols], vc_ref[:, cols]], axis=0)
            for odd in range(2):
                g = 2 * tl + odd
                klo, khi = _kv_variants(kt, odd)
                vlo, vhi = _kv_variants(vt, odd)
                for i in range(2):
                    qc = slice(256 * g + 128 * i, 256 * g + 128 * (i + 1))
                    qp = _bf(q_ref[:, qc])
                    pa, _ = _attn_probs(qp, klo, sink_ref[0, 4 * g + 2 * i], valid)
                    pb, _ = _attn_probs(qp, khi, sink_ref[0, 4 * g + 2 * i + 1], valid)
                    o_ref[:, qc] = _mm(_bf(pa), vlo) + _mm(_bf(pb), vhi)

    prev = lambda n: jnp.maximum(n - 1, 0)
    return pl.pallas_call(
        body, grid=(nb,),
        in_specs=[pl.BlockSpec((AB, D), lambda n: (n, C_AQ // D)),
                  pl.BlockSpec((AB, 256), lambda n: (n, C_AK // 256)),
                  pl.BlockSpec((AB, 256), lambda n: (prev(n), C_AK // 256)),
                  pl.BlockSpec((AB, 256), lambda n: (n, C_AV // 256)),
                  pl.BlockSpec((AB, 256), lambda n: (prev(n), C_AV // 256)),
                  pl.BlockSpec(memory_space=pltpu.SMEM)],
        out_specs=pl.BlockSpec((AB, D), lambda n: (n, 0)),
        out_shape=jax.ShapeDtypeStruct((t, D), F32),
        name="attn_fwd", compiler_params=_cp(32),
    )(proj, proj, proj, proj, proj, sinks)


def _attn_bwd(proj, sinks, do_a):
    t = proj.shape[0]
    nb = t // AB

    def body(q_ref, kc_ref, kp_ref, vc_ref, vp_ref, do_ref, sink_ref, dq_ref, dkv_ref, dsink_ref, carry):
        n = pl.program_id(0)

        @pl.when(n == 0)
        def _():
            dsink_ref[...] = jnp.zeros_like(dsink_ref)
            carry[...] = jnp.zeros_like(carry)

        @pl.when(n < nb)
        def _():
            valid = _attn_valid(n)
            low = lax.broadcasted_iota(jnp.int32, (2 * AB, 128), 1) < 64
            lane = lax.broadcasted_iota(jnp.int32, (8, 128), 1)
            row0 = lax.broadcasted_iota(jnp.int32, (8, 128), 0) == 0
            dsink = jnp.zeros((8, 128), F32)
            for tl in range(2):
                cols = slice(tl * 128, (tl + 1) * 128)
                kt = jnp.concatenate([kp_ref[:, cols], kc_ref[:, cols]], axis=0)
                vt = jnp.concatenate([vp_ref[:, cols], vc_ref[:, cols]], axis=0)
                dkt = jnp.zeros((2 * AB, 128), F32)
                dvt = jnp.zeros((2 * AB, 128), F32)
                for odd in range(2):
                    g = 2 * tl + odd
                    kvar = _kv_variants(kt, odd)
                    vvar = _kv_variants(vt, odd)
                    dk_ab = [jnp.zeros((2 * AB, 128), F32), jnp.zeros((2 * AB, 128), F32)]
                    dv_ab = [jnp.zeros((2 * AB, 128), F32), jnp.zeros((2 * AB, 128), F32)]
                    for i in range(2):
                        qc = slice(256 * g + 128 * i, 256 * g + 128 * (i + 1))
                        qp = _bf(q_ref[:, qc])
                        dop = _bf(do_ref[:, qc])
                        dqp = jnp.zeros((AB, 128), F32)
                        for ab in range(2):
                            head = 4 * g + 2 * i + ab
                            pr, ps = _attn_probs(qp, kvar[ab], sink_ref[0, head], valid)
                            dp = _mm_nt(dop, vvar[ab])
                            rs = jnp.sum(pr * dp, axis=-1, keepdims=True)
                            ds = _bf(pr * (dp - rs))
                            dsink = dsink + jnp.where(row0 & (lane == head), -jnp.sum(ps * rs), 0.0)
                            dqp = dqp + _mm(ds, kvar[ab])
                            dk_ab[ab] = dk_ab[ab] + _mm_tn(ds, qp)
                            dv_ab[ab] = dv_ab[ab] + _mm_tn(_bf(pr), dop)
                        dq_ref[:, qc] = _bf(dqp * ATT_SCALE)
                    if odd:
                        dkt = dkt + jnp.where(low, 0.0, dk_ab[1]) + pltpu.roll(jnp.where(low, dk_ab[0], 0.0), 64, 1)
                        dvt = dvt + jnp.where(low, 0.0, dv_ab[1]) + pltpu.roll(jnp.where(low, dv_ab[0], 0.0), 64, 1)
                    else:
                        dkt = dkt + jnp.where(low, dk_ab[0], 0.0) + pltpu.roll(jnp.where(low, 0.0, dk_ab[1]), 64, 1)
                        dvt = dvt + jnp.where(low, dv_ab[0], 0.0) + pltpu.roll(jnp.where(low, 0.0, dv_ab[1]), 64, 1)
                dkt = dkt * ATT_SCALE
                kcols = slice(tl * 128, (tl + 1) * 128)
                vcols = slice(256 + tl * 128, 256 + (tl + 1) * 128)
                dkv_ref[:, kcols] = _bf(carry[:, kcols] + dkt[0:AB])
                dkv_ref[:, vcols] = _bf(carry[:, vcols] + dvt[0:AB])
                carry[:, kcols] = dkt[AB:2 * AB]
                carry[:, vcols] = dvt[AB:2 * AB]
            dsink_ref[...] += dsink

        @pl.when(n == nb)
        def _():
            dkv_ref[...] = _bf(carry[...])

    cur = lambda n: jnp.minimum(n, nb - 1)
    prev = lambda n: jnp.clip(n - 1, 0, nb - 1)
    return pl.pallas_call(
        body, grid=(nb + 1,),
        in_specs=[pl.BlockSpec((AB, D), lambda n: (cur(n), C_AQ // D)),
                  pl.BlockSpec((AB, 256), lambda n: (cur(n), C_AK // 256)),
                  pl.BlockSpec((AB, 256), lambda n: (prev(n), C_AK // 256)),
                  pl.BlockSpec((AB, 256), lambda n: (cur(n), C_AV // 256)),
                  pl.BlockSpec((AB, 256), lambda n: (prev(n), C_AV // 256)),
                  pl.BlockSpec((AB, D), lambda n: (cur(n), 0)),
                  pl.BlockSpec(memory_space=pltpu.SMEM)],
        out_specs=[pl.BlockSpec((AB, D), lambda n: (cur(n), 0)),
                   pl.BlockSpec((AB, 512), lambda n: (prev(n), 0)),
                   pl.BlockSpec((8, 128), lambda n: (0, 0))],
        out_shape=[jax.ShapeDtypeStruct((t, D), BF16), jax.ShapeDtypeStruct((t, 512), BF16),
                   jax.ShapeDtypeStruct((8, 128), F32)],
        scratch_shapes=[pltpu.VMEM((AB, 512), F32)],
        name="attn_bwd", compiler_params=_cp(32),
    )(proj, proj, proj, proj, proj, do_a, sinks)


def _silu_and_grad(v):
    s = _sigmoid(v)
    return v * s, s * (1.0 + v * (1.0 - s))


def _tail(o_raw, o_a, proj, x2d, tgt, wbh, wba, wout, hnw, fnw, tb):
    t = x2d.shape[0]

    def body(or_ref, oa_ref, hg_ref, ag_ref, mh_ref, ma_ref, x_ref, t_ref, wbh_ref, wba_ref, wout_ref, hnw_ref, fnw_ref,
             dx2_ref, dor_ref, doa_ref, dt_ref, ght_ref, gat_ref, mgt_ref, dyh_ref, dya_ref, dx2b_ref, sums_ref):
        @pl.when(pl.program_id(0) == 0)
        def _():
            sums_ref[...] = jnp.zeros_like(sums_ref)

        hnw_v = hnw_ref[...]
        fnw_v = fnw_ref[...]
        o = or_ref[...]
        rs, xhs = [], []
        for h in range(HEADS):
            oh = o[:, h * HD:(h + 1) * HD]
            r = lax.rsqrt(jnp.mean(oh * oh, axis=-1, keepdims=True) + EPS)
            rs.append(r)
            xhs.append(oh * r)
        xh = jnp.concatenate(xhs, axis=1)
        on = xh * hnw_v
        sil_hg, dsil_hg = _silu_and_grad(hg_ref[...])
        gh = on * sil_hg
        gh_b = _bf(gh)
        y_h = _mm(gh_b, wbh_ref[...])
        oa = oa_ref[...]
        sil_ag, dsil_ag = _silu_and_grad(ag_ref[...])
        ga_b = _bf(oa * sil_ag)
        y_a = _mm(ga_b, wba_ref[...])
        s_mh = _sigmoid(mh_ref[...])
        s_ma = _sigmoid(ma_ref[...])
        mg_b = _bf(s_mh * y_h + s_ma * y_a)
        x2 = x_ref[...] + _mm(mg_b, wout_ref[...])
        r2 = lax.rsqrt(jnp.mean(x2 * x2, axis=-1, keepdims=True) + EPS)
        xh2 = x2 * r2
        err = xh2 * fnw_v - t_ref[...]
        loss = 0.5 * jnp.sum(jnp.mean(err * err, axis=-1, keepdims=True))
        dy = err * (1.0 / D)
        dfnw = jnp.sum(dy * xh2, axis=0, keepdims=True)
        dxh2 = dy * fnw_v
        dx2 = r2 * (dxh2 - xh2 * jnp.mean(dxh2 * xh2, axis=-1, keepdims=True))
        dx2_ref[...] = dx2
        dx2_b = _bf(dx2)
        dmg = _mm_nt(dx2_b, wout_ref[...])
        dyh_b = _bf(dmg * s_mh)
        dya_b = _bf(dmg * s_ma)
        dt_ref[:, 2 * D:3 * D] = _bf(dmg * y_h * s_mh * (1.0 - s_mh))
        dt_ref[:, 3 * D:4 * D] = _bf(dmg * y_a * s_ma * (1.0 - s_ma))
        dgh = _mm_nt(dyh_b, wbh_ref[...])
        dga = _mm_nt(dya_b, wba_ref[...])
        doa_ref[...] = dga * sil_ag
        dt_ref[:, D:2 * D] = _bf(dga * oa * dsil_ag)
        dt_ref[:, 0:D] = _bf(dgh * on * dsil_hg)
        don = dgh * sil_hg
        dhnw = jnp.sum(don * xh, axis=0, keepdims=True)
        dxh = don * hnw_v
        dos = []
        for h in range(HEADS):
            sl = slice(h * HD, (h + 1) * HD)
            dos.append(rs[h] * (dxh[:, sl] - xhs[h] * jnp.mean(dxh[:, sl] * xhs[h], axis=-1, keepdims=True)))
        dor_ref[...] = jnp.concatenate(dos, axis=1)
        ght_ref[...] = _bf(gh.T)
        gat_ref[...] = _bf((oa * sil_ag).T)
        mgt_ref[...] = _bf((s_mh * y_h + s_ma * y_a).T)
        dyh_ref[...] = dyh_b
        dya_ref[...] = dya_b
        dx2b_ref[...] = dx2_b
        row = lax.broadcasted_iota(jnp.int32, (8, D), 0)
        sums_ref[...] += jnp.where(row == 0, dfnw, 0.0) + jnp.where(row == 1, dhnw, 0.0) + jnp.where(row == 2, loss, 0.0)

    rowblk = lambda c: pl.BlockSpec((tb, D), lambda i: (i, c))
    full = lambda shape: pl.BlockSpec(shape, lambda i: (0, 0))
    tblk = pl.BlockSpec((D, tb), lambda i: (0, i))
    return pl.pallas_call(
        body, grid=(t // tb,),
        in_specs=[rowblk(0), rowblk(0), rowblk(C_HG // D), rowblk(C_AG // D), rowblk(C_MH // D), rowblk(C_MA // D),
                  rowblk(0), rowblk(0), full((D, D)), full((D, D)), full((D, D)), full((1, D)), full((1, D))],
        out_specs=[rowblk(0), rowblk(0), rowblk(0), pl.BlockSpec((tb, 4 * D), lambda i: (i, 0)),
                   tblk, tblk, tblk, rowblk(0), rowblk(0), rowblk(0), full((8, D))],
        out_shape=[jax.ShapeDtypeStruct((t, D), F32)] * 3 + [jax.ShapeDtypeStruct((t, 4 * D), BF16)]
        + [jax.ShapeDtypeStruct((D, t), BF16)] * 3 + [jax.ShapeDtypeStruct((t, D), BF16)] * 3
        + [jax.ShapeDtypeStruct((8, D), F32)],
        name="tail", compiler_params=_cp(56),
    )(o_raw, o_a, proj, proj, proj, proj, x2d, tgt, wbh, wba, wout, hnw, fnw)


def _wgrad3(ght, dyh, gat, dya, mgt, dx2b, tk):
    t = dyh.shape[0]

    def body(a0, b0, a1, b1, a2, b2, o0, o1, o2):
        @pl.when(pl.program_id(0) == 0)
        def _():
            o0[...] = jnp.zeros_like(o0)
            o1[...] = jnp.zeros_like(o1)
            o2[...] = jnp.zeros_like(o2)

        o0[...] += _mm(a0[...], b0[...])
        o1[...] += _mm(a1[...], b1[...])
        o2[...] += _mm(a2[...], b2[...])

    lhs = pl.BlockSpec((D, tk), lambda k: (0, k))
    rhs = pl.BlockSpec((tk, D), lambda k: (k, 0))
    out = pl.BlockSpec((D, D), lambda k: (0, 0))
    return pl.pallas_call(
        body, grid=(t // tk,), in_specs=[lhs, rhs] * 3, out_specs=[out] * 3,
        out_shape=[jax.ShapeDtypeStruct((D, D), F32)] * 3,
        name="wgrad3", compiler_params=_cp(48),
    )(ght, dyh, gat, dya, mgt, dx2b)


def _piece_specs(tb, row_of, col_of):
    specs = []
    for first, count in PIECES:
        def index(a, b, first=first, count=count):
            i, j = row_of(a, b), col_of(a, b)
            inside = (j >= first) & (j < first + count)
            return (jnp.where(inside, i, 0), jnp.clip(j - first, 0, count - 1))
        specs.append(pl.BlockSpec((tb, CB), index))
    return specs


def _inproj_wgrad(xnt, pieces, tb):
    t = xnt.shape[1]

    def body(xnt_ref, p0, p1, p2, p3, o_ref):
        j = pl.program_id(0)

        @pl.when(pl.program_id(1) == 0)
        def _():
            o_ref[...] = jnp.zeros_like(o_ref)

        for (first, count), p in zip(PIECES, (p0, p1, p2, p3)):
            @pl.when((j >= first) & (j < first + count))
            def _(p=p):
                o_ref[...] += _mm(xnt_ref[...], p[...])

    return pl.pallas_call(
        body, grid=(DIN // CB, t // tb),
        in_specs=[pl.BlockSpec((D, tb), lambda j, i: (0, i))] + _piece_specs(tb, lambda j, i: i, lambda j, i: j),
        out_specs=pl.BlockSpec((D, CB), lambda j, i: (0, j)),
        out_shape=jax.ShapeDtypeStruct((D, DIN), F32),
        name="inproj_wgrad", compiler_params=_cp(48),
    )(xnt, *pieces)


def _inproj_dgrad(pieces, w_p, x2d, dx2, norm_w, tb):
    t = x2d.shape[0]
    ncb = DIN // CB

    def body(p0, p1, p2, p3, w_ref, x_ref, dx2_ref, nw_ref, gx_ref, dnw_ref, acc):
        i = pl.program_id(0)
        j = pl.program_id(1)

        @pl.when((i == 0) & (j == 0))
        def _():
            dnw_ref[...] = jnp.zeros_like(dnw_ref)

        @pl.when(j == 0)
        def _():
            acc[...] = jnp.zeros_like(acc)

        for (first, count), p in zip(PIECES, (p0, p1, p2, p3)):
            @pl.when((j >= first) & (j < first + count))
            def _(p=p):
                acc[...] += _mm_nt(p[...], w_ref[...])

        @pl.when(j == ncb - 1)
        def _():
            xv = x_ref[...]
            r = lax.rsqrt(jnp.mean(xv * xv, axis=-1, keepdims=True) + EPS)
            xh = xv * r
            dxn = acc[...]
            dxh = dxn * nw_ref[...]
            gx_ref[...] = dx2_ref[...] + r * (dxh - xh * jnp.mean(dxh * xh, axis=-1, keepdims=True))
            row0 = lax.broadcasted_iota(jnp.int32, (8, D), 0) == 0
            dnw_ref[...] += jnp.where(row0, jnp.sum(dxn * xh, axis=0, keepdims=True), 0.0)

    rowblk = pl.BlockSpec((tb, D), lambda i, j: (i, 0))
    return pl.pallas_call(
        body, grid=(t // tb, ncb),
        in_specs=_piece_specs(tb, lambda i, j: i, lambda i, j: j)
        + [pl.BlockSpec((D, CB), lambda i, j: (0, j)), rowblk, rowblk, pl.BlockSpec((1, D), lambda i, j: (0, 0))],
        out_specs=[rowblk, pl.BlockSpec((8, D), lambda i, j: (0, 0))],
        out_shape=[jax.ShapeDtypeStruct((t, D), F32), jax.ShapeDtypeStruct((8, D), F32)],
        scratch_shapes=[pltpu.VMEM((tb, D), F32)],
        name="inproj_dgrad", compiler_params=_cp(48),
    )(*pieces, w_p, x2d, dx2, norm_w)


def _adamw_math(w, g, m, v):
    m = B1 * m + (1.0 - B1) * g
    v = B2 * v + (1.0 - B2) * (g * g)
    m_hat = m / (1.0 - B1 ** STEP)
    v_hat = v / (1.0 - B2 ** STEP)
    delta = -LR * (m_hat / (jnp.sqrt(v_hat) + ADAM_EPS) + WD * w)
    return delta, m, v


def _adamw_shard(parts, w, m, v, rows, name):
    _, nr, nc = parts.shape

    def body(p_ref, w_ref, m_ref, v_ref, g_ref, d_ref, nm_ref, nv_ref):
        g = p_ref[0].astype(F32)
        for s in range(1, NDEV):
            g = g + p_ref[s].astype(F32)
        d, nm, nv = _adamw_math(w_ref[...], g, m_ref[...], v_ref[...])
        g_ref[...] = g
        d_ref[...] = d
        nm_ref[...] = nm
        nv_ref[...] = nv

    blk = pl.BlockSpec((rows, nc), lambda i: (i, 0))
    return pl.pallas_call(
        body, grid=(nr // rows,),
        in_specs=[pl.BlockSpec((NDEV, rows, nc), lambda i: (0, i, 0)), blk, blk, blk],
        out_specs=[blk] * 4, out_shape=[jax.ShapeDtypeStruct((nr, nc), F32)] * 4,
        name=name, compiler_params=_cp(48),
    )(parts, w, m, v)


def _adamw_small(parts, w, m, v):
    def body(p_ref, w_ref, m_ref, v_ref, g_ref, d_ref, nm_ref, nv_ref):
        g = p_ref[0]
        for s in range(1, NDEV):
            g = g + p_ref[s]
        g_ref[...] = g
        lb = _lower_bound(w_ref.at[1:3, :])
        ga0 = g_ref[1:2, :] * lb * (1.0 - lb)
        g_ref[1:2, :] = ga0
        g_ref[2:3, :] = -ga0
        d, nm, nv = _adamw_math(w_ref[...], g_ref[...], m_ref[...], v_ref[...])
        d_ref[...] = d
        nm_ref[...] = nm
        nv_ref[...] = nv

    return pl.pallas_call(
        body, out_shape=[jax.ShapeDtypeStruct((8, D), F32)] * 4, name="adamw_small",
    )(parts, w, m, v)


def _to_internal_cols(w):
    hg = w[:, 0:3072].reshape(D, 3, HEADS, HD).transpose(0, 2, 1, 3).reshape(D, 3072)
    return jnp.concatenate([hg, w[:, 4096:5120], w[:, 3072:4096], w[:, 5632:8704], w[:, 5120:5632]], axis=1)


def _to_reference_cols(w):
    hg = w[:, 0:3072].reshape(D, HEADS, 3, HD).transpose(0, 2, 1, 3).reshape(D, 3072)
    return jnp.concatenate([hg, w[:, C_HG:C_HG + D], w[:, C_AQ:C_AQ + D], w[:, C_AK:DIN], w[:, C_AG:C_AK]], axis=1)


def _pack_small(rows):
    out = [jnp.pad(r.reshape(1, -1), ((0, 0), (0, D - r.size))) for r in rows]
    out.append(jnp.zeros((8 - len(rows), D), F32))
    return jnp.concatenate(out, axis=0)


def kernel(x, norm_w, w_in, hgrn_lower_bound, hgrn_norm_w, w_branch_hgrn, attn_sinks, w_branch_attn, w_out, final_norm_w, loss_target, m_norm_w, m_w_in, m_hgrn_lower_bound, m_hgrn_norm_w, m_w_branch_hgrn, m_attn_sinks, m_w_branch_attn, m_w_out, m_final_norm_w, v_norm_w, v_w_in, v_hgrn_lower_bound, v_hgrn_norm_w, v_w_branch_hgrn, v_attn_sinks, v_w_branch_attn, v_w_out, v_final_norm_w):
    t = x.shape[1]
    x2d = x.reshape(t, D)
    tgt = loss_target.reshape(t, D)
    fnw = final_norm_w.reshape(1, D)
    row_blk = min(256, t)
    big_blk = min(512, t)

    win_g, wbh_g, wba_g, wout_g = _gather_weights(w_in[0], w_branch_hgrn[0], w_branch_attn[0], w_out[0])
    w_p = _to_internal_cols(win_g.transpose(1, 0, 2).reshape(D, DIN))
    wbh, wba, wout = wbh_g.reshape(D, D), wba_g.reshape(D, D), wout_g.reshape(D, D)

    proj, xnt = _inproj_fwd(x2d, norm_w, w_p, big_blk, 2176)
    o_raw, states = _hgrn_fwd(proj, hgrn_lower_bound, big_blk, HGRN_GROUP)
    o_a = _attn_fwd(proj, attn_sinks)
    (dx2, do_raw, do_a, d_tail, ght, gat, mgt, dyh, dya, dx2b, sums) = _tail(
        o_raw, o_a, proj, x2d, tgt, wbh, wba, wout, hgrn_norm_w, fnw, row_blk)
    dwbh, dwba, dwout = _wgrad3(ght, dyh, gat, dya, mgt, dx2b, big_blk)
    d_aq, d_kv, dsink = _attn_bwd(proj, attn_sinks, do_a)
    d_hgrn, dlb = _hgrn_bwd(proj, hgrn_lower_bound, do_raw, states, big_blk, HGRN_GROUP)
    pieces = (d_hgrn, d_aq, d_tail, d_kv)
    dw_p = _inproj_wgrad(xnt, pieces, big_blk)
    grad_x, dnw = _inproj_dgrad(pieces, w_p, x2d, dx2, norm_w, big_blk)

    dwin_r = _to_reference_cols(dw_p).reshape(D, NDEV, IN_SHARD).transpose(1, 0, 2).astype(BF16)
    small = _pack_small([dnw[0:1], dlb[0:1], jnp.zeros((1, D), F32), sums[1:2], sums[0:1], dsink[0:1, 0:QH]])
    slots = lambda a: a.reshape(NDEV, ROW_SHARD, D).astype(BF16)
    rin, rbh, rba, rout, rsm = _exchange_grads(dwin_r, slots(dwbh), slots(dwba), slots(dwout), small)
    g_in, d_in, nm_in, nv_in = _adamw_shard(rin, w_in[0], m_w_in[0], v_w_in[0], 128, "adamw_w_in")
    g_bh, d_bh, nm_bh, nv_bh = _adamw_shard(rbh, w_branch_hgrn[0], m_w_branch_hgrn[0], v_w_branch_hgrn[0], 128, "adamw_w_bh")
    g_ba, d_ba, nm_ba, nv_ba = _adamw_shard(rba, w_branch_attn[0], m_w_branch_attn[0], v_w_branch_attn[0], 128, "adamw_w_ba")
    g_out, d_out, nm_out, nv_out = _adamw_shard(rout, w_out[0], m_w_out[0], v_w_out[0], 128, "adamw_w_out")
    pack = lambda a, b, c, d, e: _pack_small([a, b[0:1], b[1:2], c, d.reshape(1, D), e])
    small_out = _adamw_small(
        rsm,
        pack(norm_w, hgrn_lower_bound, hgrn_norm_w, final_norm_w, attn_sinks),
        pack(m_norm_w, m_hgrn_lower_bound, m_hgrn_norm_w, m_final_norm_w, m_attn_sinks),
        pack(v_norm_w, v_hgrn_lower_bound, v_hgrn_norm_w, v_final_norm_w, v_attn_sinks))

    loss = lax.psum(sums[2, 0], ("x", "y", "c"))

    def unpack(p):
        return dict(norm_w=p[0:1], lb=p[1:3], hnw=p[3:4], fnw=p[4], sinks=p[5:6, 0:QH])

    sg, sd, sm, sv = (unpack(p) for p in small_out)

    def group(s, w_in_v, bh, ba, out):
        return (s["norm_w"], w_in_v[None], s["lb"], s["hnw"], bh[None], s["sinks"], ba[None], out[None], s["fnw"])

    return (loss, grad_x.reshape(1, t, D),
            *group(sg, g_in, g_bh, g_ba, g_out), *group(sd, d_in, d_bh, d_ba, d_out),
            *group(sm, nm_in, nm_bh, nm_ba, nm_out), *group(sv, nv_in, nv_bh, nv_ba, nv_out))
```

```python
import functools

import jax
import jax.numpy as jnp
from jax import lax
from jax.experimental import pallas as pl
from jax.experimental.pallas import tpu as pltpu

F32 = jnp.float32
BF16 = jnp.bfloat16

D = 1024
DIN = 8704
NDEV = 8
IN_SHARD = DIN // NDEV
ROW_SHARD = D // NDEV
HEADS = 8
HD = 128
CH = 64
HGRN_GROUP = 8
QH = 16
AB = 128
EPS = 1e-6
NEG = -1e30
ATT_SCALE = 0.125

C_HGRN = 0
C_AQ = 3072
C_HG = 4096
C_AG = 5120
C_MH = 6144
C_MA = 7168
C_AK = 8192
C_AV = 8448
CB = 512

LR = 0.001
B1 = 0.9
B2 = 0.999
ADAM_EPS = 1e-08
WD = 0.01
STEP = 10

V7X_VMEM_BYTES = 64 * 1024 * 1024
MESH = pl.DeviceIdType.MESH


def _cp(vmem_mb):
    return pltpu.CompilerParams(vmem_limit_bytes=vmem_mb * 1024 * 1024)


def _mm(a, b):
    return jnp.dot(a, b, preferred_element_type=F32)


def _mm_nt(a, b):
    return lax.dot_general(a, b, (((1,), (1,)), ((), ())), preferred_element_type=F32)


def _mm_tn(a, b):
    return lax.dot_general(a, b, (((0,), (0,)), ((), ())), preferred_element_type=F32)


def _mm_exact(a, b):
    return jnp.dot(a, b, preferred_element_type=F32, precision=lax.Precision.HIGHEST)


def _sigmoid(v):
    return 1.0 / (1.0 + jnp.exp(-v))


def _bf(v):
    return v.astype(BF16)


def _peer(r):
    x, y, c = lax.axis_index("x"), lax.axis_index("y"), lax.axis_index("c")
    px = 1 - x if r & 4 else x
    py = 1 - y if r & 2 else y
    pc = 1 - c if r & 1 else c
    return (px, py, pc), 4 * px + 2 * py + pc


def _gather_weights(w_in_s, wbh_s, wba_s, wout_s):
    def body(win_ref, wbh_ref, wba_ref, wout_ref, win_g, wbh_g, wba_g, wout_g, sin, s3, send_sems, recv_sems, loc_sems):
        _, me = _peer(0)
        sin[...] = win_ref[...].astype(BF16)
        s3[0] = wbh_ref[...].astype(BF16)
        s3[1] = wba_ref[...].astype(BF16)
        s3[2] = wout_ref[...].astype(BF16)
        srcs = [sin, s3.at[0], s3.at[1], s3.at[2]]
        outs = [win_g, wbh_g, wba_g, wout_g]
        local = [pltpu.make_async_copy(srcs[k], outs[k].at[me], loc_sems.at[k]) for k in range(4)]
        for cp in local:
            cp.start()
        sends = []
        for r in range(1, NDEV):
            peer, _ = _peer(r)
            for k in range(4):
                cp = pltpu.make_async_remote_copy(
                    src_ref=srcs[k], dst_ref=outs[k].at[me], send_sem=send_sems.at[r - 1, k],
                    recv_sem=recv_sems.at[r - 1, k], device_id=peer, device_id_type=MESH)
                cp.start()
                sends.append(cp)
        for r in range(1, NDEV):
            peer, pidx = _peer(r)
            for k in range(4):
                pltpu.make_async_remote_copy(
                    src_ref=srcs[k], dst_ref=outs[k].at[pidx], send_sem=send_sems.at[r - 1, k],
                    recv_sem=recv_sems.at[r - 1, k], device_id=peer, device_id_type=MESH).wait_recv()
        for cp in sends:
            cp.wait_send()
        for cp in local:
            cp.wait()

    vm = pl.BlockSpec(memory_space=pltpu.VMEM)
    hbm = pl.BlockSpec(memory_space=pl.ANY)
    return pl.pallas_call(
        body,
        out_shape=[jax.ShapeDtypeStruct((NDEV, D, IN_SHARD), BF16)] + [jax.ShapeDtypeStruct((NDEV, ROW_SHARD, D), BF16)] * 3,
        in_specs=[vm, vm, vm, vm],
        out_specs=[hbm, hbm, hbm, hbm],
        scratch_shapes=[pltpu.VMEM((D, IN_SHARD), BF16), pltpu.VMEM((3, ROW_SHARD, D), BF16),
                        pltpu.SemaphoreType.DMA((NDEV - 1, 4)), pltpu.SemaphoreType.DMA((NDEV - 1, 4)),
                        pltpu.SemaphoreType.DMA((4,))],
        name="gather_weights", compiler_params=_cp(32),
    )(w_in_s, wbh_s, wba_s, wout_s)


def _exchange_grads(dwin_r, dwbh_r, dwba_r, dwout_r, small):
    arrs = [dwin_r, dwbh_r, dwba_r, dwout_r]
    na = len(arrs) + 1

    def body(a0, a1, a2, a3, sm, r0, r1, r2, r3, rsm, send_sems, recv_sems, loc_sems):
        ins = [a0, a1, a2, a3]
        outs = [r0, r1, r2, r3]
        _, me = _peer(0)
        local = [pltpu.make_async_copy(ins[k].at[me], outs[k].at[me], loc_sems.at[k]) for k in range(4)]
        local.append(pltpu.make_async_copy(sm, rsm.at[me], loc_sems.at[4]))
        for cp in local:
            cp.start()
        sends = []
        for r in range(1, NDEV):
            peer, pidx = _peer(r)
            for k in range(na):
                src = ins[k].at[pidx] if k < 4 else sm
                dst = outs[k].at[me] if k < 4 else rsm.at[me]
                cp = pltpu.make_async_remote_copy(
                    src_ref=src, dst_ref=dst, send_sem=send_sems.at[r - 1, k], recv_sem=recv_sems.at[r - 1, k],
                    device_id=peer, device_id_type=MESH)
                cp.start()
                sends.append(cp)
        for r in range(1, NDEV):
            peer, pidx = _peer(r)
            for k in range(na):
                src = ins[k].at[pidx] if k < 4 else sm
                dst = outs[k].at[pidx] if k < 4 else rsm.at[pidx]
                pltpu.make_async_remote_copy(
                    src_ref=src, dst_ref=dst, send_sem=send_sems.at[r - 1, k], recv_sem=recv_sems.at[r - 1, k],
                    device_id=peer, device_id_type=MESH).wait_recv()
        for cp in sends:
            cp.wait_send()
        for cp in local:
            cp.wait()

    hbm = pl.BlockSpec(memory_space=pl.ANY)
    return pl.pallas_call(
        body,
        out_shape=[jax.ShapeDtypeStruct(a.shape, a.dtype) for a in arrs] + [jax.ShapeDtypeStruct((NDEV,) + small.shape, F32)],
        in_specs=[hbm] * na,
        out_specs=[hbm] * na,
        scratch_shapes=[pltpu.SemaphoreType.DMA((NDEV - 1, na)), pltpu.SemaphoreType.DMA((NDEV - 1, na)),
                        pltpu.SemaphoreType.DMA((na,))],
        name="exchange_grads",
    )(*arrs, small)


def _inproj_fwd(x2d, norm_w, w_p, tb, nb):
    t = x2d.shape[0]

    def body(x_ref, nw_ref, w_ref, proj_ref, xnt_ref, xn_s):
        @pl.when(pl.program_id(1) == 0)
        def _():
            xv = x_ref[...]
            r = lax.rsqrt(jnp.mean(xv * xv, axis=-1, keepdims=True) + EPS)
            xn = (xv * r) * nw_ref[...]
            xn_s[...] = xn.astype(BF16)
            xnt_ref[...] = xn.T.astype(BF16)

        proj_ref[...] = _mm(xn_s[...], w_ref[...])

    return pl.pallas_call(
        body, grid=(t // tb, DIN // nb),
        in_specs=[pl.BlockSpec((tb, D), lambda i, j: (i, 0)), pl.BlockSpec((1, D), lambda i, j: (0, 0)),
                  pl.BlockSpec((D, nb), lambda i, j: (0, j))],
        out_specs=[pl.BlockSpec((tb, nb), lambda i, j: (i, j)), pl.BlockSpec((D, tb), lambda i, j: (0, i))],
        out_shape=[jax.ShapeDtypeStruct((t, DIN), F32), jax.ShapeDtypeStruct((D, t), BF16)],
        scratch_shapes=[pltpu.VMEM((tb, D), BF16)],
        name="inproj_fwd", compiler_params=_cp(56),
    )(x2d, norm_w, w_p)


def _lower_bound(lb_ref):
    a0 = lb_ref[0:1, :]
    a1 = lb_ref[1:2, :]
    mx = jnp.maximum(a0, a1)
    e0 = jnp.exp(a0 - mx)
    e1 = jnp.exp(a1 - mx)
    return e0 / (e0 + e1)


def _hgrn_chunk_fwd(hq, hf, lb, tril):
    sg = _sigmoid(hf)
    f = lb + (1.0 - lb) * sg
    g = jnp.log(f)
    k = 1.0 - f
    sq = _sigmoid(hq)
    q = hq * sq
    b = _mm_exact(tril, g)
    last_row = lax.broadcasted_iota(jnp.int32, b.shape, 0) == CH - 1
    b_last = jnp.sum(jnp.where(last_row, b, 0.0), axis=0, keepdims=True)
    c = 0.5 * b_last
    eb = jnp.exp(b)
    ea = jnp.exp(b - c)
    ek = jnp.exp(c - b)
    ed = jnp.exp(b_last - b)
    ebl = jnp.exp(b_last)
    return dict(sg=sg, f=f, k=k, sq=sq, q=q, eb=eb, ea=ea, ek=ek, ed=ed, ebl=ebl,
                qe=q * eb, qa=q * ea, ka=k * ek, kd=k * ed)


def _tri(lower):
    r = lax.broadcasted_iota(jnp.int32, (CH, CH), 0)
    c = lax.broadcasted_iota(jnp.int32, (CH, CH), 1)
    return (r >= c) if lower else (c >= r)


def _head_segment(p_ref, rows, j, hg):
    return jnp.concatenate([p_ref[rows, 3 * HD * k + j * HD:3 * HD * k + (j + 1) * HD] for k in range(hg)], axis=1)


def _head(a, k):
    return a[:, k * HD:(k + 1) * HD]


def _hgrn_fwd(proj, lbw, rb, hg):
    t = proj.shape[0]
    ncb = rb // CH

    def body(p_ref, lb_ref, o_ref, st_ref, s_scr):
        @pl.when(pl.program_id(1) == 0)
        def _():
            s_scr[...] = jnp.zeros_like(s_scr)

        lb = _lower_bound(lb_ref)
        causal = _tri(True)
        tril = causal.astype(F32)
        heads = range(hg)

        def chunk(cc, carry):
            r0 = pl.multiple_of(cc * CH, CH)
            rows = pl.ds(r0, CH)
            e = _hgrn_chunk_fwd(_head_segment(p_ref, rows, 0, hg), _head_segment(p_ref, rows, 1, hg), lb, tril)
            v = _bf(_head_segment(p_ref, rows, 2, hg))
            sts = [s_scr[k] for k in heads]
            qa, ka, qe, kd = _bf(e["qa"]), _bf(e["ka"]), _bf(e["qe"]), _bf(e["kd"])
            a = [_bf(jnp.where(causal, _mm_nt(_head(qa, k), _head(ka, k)), 0.0)) for k in heads]
            o_inter = [_mm_nt(_head(qe, k), _bf(sts[k])) for k in heads]
            kv = [_mm_tn(_head(v, k), _head(kd, k)) for k in heads]
            o_intra = [_mm(a[k], _head(v, k)) for k in heads]
            for k in heads:
                st_ref[cc, k] = sts[k]
                o_ref[rows, k * HD:(k + 1) * HD] = o_inter[k] + o_intra[k]
                s_scr[k] = sts[k] * _head(e["ebl"], k) + kv[k]
            return carry

        lax.fori_loop(0, ncb, chunk, 0)

    return pl.pallas_call(
        body, grid=(HEADS // hg, t // rb),
        in_specs=[pl.BlockSpec((rb, 3 * HD * hg), lambda h, i: (i, h)), pl.BlockSpec((2, HD * hg), lambda h, i: (0, h))],
        out_specs=[pl.BlockSpec((rb, HD * hg), lambda h, i: (i, h)),
                   pl.BlockSpec((ncb, hg, HD, HD), lambda h, i: (i, h, 0, 0))],
        out_shape=[jax.ShapeDtypeStruct((t, D), F32), jax.ShapeDtypeStruct((t // CH, HEADS, HD, HD), F32)],
        scratch_shapes=[pltpu.VMEM((hg, HD, HD), F32)],
        name="hgrn_fwd", compiler_params=_cp(48),
    )(proj, lbw)


def _hgrn_bwd(proj, lbw, do_raw, states, rb, hg):
    t = proj.shape[0]
    nblk = t // rb
    ncb = rb // CH

    def body(p_ref, lb_ref, do_ref, st_ref, dp_ref, dlb_ref, ds_scr):
        @pl.when(pl.program_id(1) == 0)
        def _():
            ds_scr[...] = jnp.zeros_like(ds_scr)
            dlb_ref[...] = jnp.zeros_like(dlb_ref)

        lb = _lower_bound(lb_ref)
        causal = _tri(True)
        tril = causal.astype(F32)
        triu = _tri(False).astype(F32)
        last_row = lax.broadcasted_iota(jnp.int32, (CH, HD * hg), 0) == CH - 1
        row0 = lax.broadcasted_iota(jnp.int32, (8, HD * hg), 0) == 0
        heads = range(hg)
        wide = lambda parts: jnp.concatenate(parts, axis=1)

        def chunk(it, carry):
            cc = ncb - 1 - it
            r0 = pl.multiple_of(cc * CH, CH)
            rows = pl.ds(r0, CH)
            hq = _head_segment(p_ref, rows, 0, hg)
            e = _hgrn_chunk_fwd(hq, _head_segment(p_ref, rows, 1, hg), lb, tril)
            v = _bf(_head_segment(p_ref, rows, 2, hg))
            do = _bf(do_ref[rows, :])
            sts = [st_ref[cc, k] for k in heads]
            dsts = [ds_scr[k] for k in heads]
            dlb_acc = dlb_ref[...]
            qa, ka, qe, kd = _bf(e["qa"]), _bf(e["ka"]), _bf(e["qe"]), _bf(e["kd"])
            a = [_bf(jnp.where(causal, _mm_nt(_head(qa, k), _head(ka, k)), 0.0)) for k in heads]
            da = [_bf(jnp.where(causal, _mm_nt(_head(do, k), _head(v, k)), 0.0)) for k in heads]
            dqe = wide([_mm(_head(do, k), _bf(sts[k])) for k in heads])
            dkd = wide([_mm(_head(v, k), _bf(dsts[k])) for k in heads])
            dv_state = [_mm_nt(_head(kd, k), _bf(dsts[k])) for k in heads]
            ds_new = [_mm_tn(_head(do, k), _head(qe, k)) for k in heads]
            dv_intra = [_mm_tn(a[k], _head(do, k)) for k in heads]
            dqa = wide([_mm(da[k], _head(ka, k)) for k in heads])
            dka = wide([_mm_tn(da[k], _head(qa, k)) for k in heads])
            dv = wide([dv_intra[k] + dv_state[k] for k in heads])
            dbl = e["ebl"] * wide([jnp.sum(sts[k] * dsts[k], axis=0, keepdims=True) for k in heads])
            dq = dqe * e["eb"] + dqa * e["ea"]
            dk = dka * e["ek"] + dkd * e["ed"]
            dkd_kd = dkd * kd.astype(F32)
            db = dqe * qe.astype(F32) + dqa * qa.astype(F32) - dka * ka.astype(F32) - dkd_kd
            db = db + jnp.where(last_row, dbl + jnp.sum(dkd_kd, axis=0, keepdims=True), 0.0)
            dg = _mm_exact(triu, db)
            df = dg / e["f"] - dk
            sg = e["sg"]
            sq = e["sq"]
            dhq = _bf(dq * (sq * (1.0 + hq * (1.0 - sq))))
            dhf = _bf(df * (1.0 - lb) * sg * (1.0 - sg))
            dhi = _bf(dv)
            dlb_new = dlb_acc + jnp.where(row0, jnp.sum(df * (1.0 - sg), axis=0, keepdims=True), 0.0)
            for k in heads:
                c0 = 3 * HD * k
                ds_scr[k] = ds_new[k] + dsts[k] * _head(e["ebl"], k)
                dp_ref[rows, c0:c0 + HD] = _head(dhq, k)
                dp_ref[rows, c0 + HD:c0 + 2 * HD] = _head(dhf, k)
                dp_ref[rows, c0 + 2 * HD:c0 + 3 * HD] = _head(dhi, k)
            dlb_ref[...] = dlb_new
            return carry

        lax.fori_loop(0, ncb, chunk, 0)

    rev = lambda h, i: (nblk - 1 - i, h)
    return pl.pallas_call(
        body, grid=(HEADS // hg, nblk),
        in_specs=[pl.BlockSpec((rb, 3 * HD * hg), rev), pl.BlockSpec((2, HD * hg), lambda h, i: (0, h)),
                  pl.BlockSpec((rb, HD * hg), rev), pl.BlockSpec((ncb, hg, HD, HD), lambda h, i: (nblk - 1 - i, h, 0, 0))],
        out_specs=[pl.BlockSpec((rb, 3 * HD * hg), rev), pl.BlockSpec((8, HD * hg), lambda h, i: (0, h))],
        out_shape=[jax.ShapeDtypeStruct((t, 3 * D), BF16), jax.ShapeDtypeStruct((8, D), F32)],
        scratch_shapes=[pltpu.VMEM((hg, HD, HD), F32)],
        name="hgrn_bwd", compiler_params=_cp(48),
    )(proj, lbw, do_raw, states)


def _kv_variants(tile, odd):
    low = lax.broadcasted_iota(jnp.int32, tile.shape, 1) < 64
    if odd:
        hi = jnp.where(low, 0.0, tile)
        lo = pltpu.roll(hi, 64, 1)
    else:
        lo = jnp.where(low, tile, 0.0)
        hi = pltpu.roll(lo, 64, 1)
    return _bf(lo), _bf(hi)


def _attn_masks(n):
    qi = lax.broadcasted_iota(jnp.int32, (AB, AB), 0)
    kj = lax.broadcasted_iota(jnp.int32, (AB, AB), 1)
    cur = kj <= qi
    return cur, cur | (n > 0), qi <= kj


def _kv_all(prev_ref, cur_ref):
    out = []
    for tl in range(2):
        cols = slice(tl * 128, (tl + 1) * 128)
        tile = jnp.concatenate([prev_ref[:, cols], cur_ref[:, cols]], axis=0)
        out.append(_kv_variants(tile, 0))
        out.append(_kv_variants(tile, 1))
    return out


def _attn_softmax(s2, sink, cur, ok):
    s = jnp.where(ok, jnp.where(cur, s2[:, AB:], s2[:, :AB]) * ATT_SCALE, NEG)
    m = jnp.maximum(jnp.max(s, axis=-1, keepdims=True), sink)
    p = jnp.exp(s - m)
    es = jnp.exp(sink - m)
    inv = 1.0 / (jnp.sum(p, axis=-1, keepdims=True) + es)
    return p * inv, es * inv


def _spread(pc, cur):
    return jnp.concatenate([jnp.where(cur, 0.0, pc), jnp.where(cur, pc, 0.0)], axis=1)


def _spread_t(pct, cur_t):
    return jnp.concatenate([jnp.where(cur_t, 0.0, pct), jnp.where(cur_t, pct, 0.0)], axis=0)


def _attn_fwd(proj, sinks):
    t = proj.shape[0]
    nb = t // AB

    def body(q_ref, kc_ref, kp_ref, vc_ref, vp_ref, sink_ref, o_ref):
        cur, ok, _ = _attn_masks(pl.program_id(0))
        kvars = _kv_all(kp_ref, kc_ref)
        vvars = _kv_all(vp_ref, vc_ref)
        qps = [_bf(q_ref[:, 128 * j:128 * (j + 1)]) for j in range(8)]
        scores = [[_mm_nt(qps[j], kvars[j // 2][ab]) for ab in range(2)] for j in range(8)]
        for j in range(8):
            parts = []
            for ab in range(2):
                pc, _ = _attn_softmax(scores[j][ab], sink_ref[0, 2 * j + ab], cur, ok)
                parts.append(_mm(_bf(_spread(pc, cur)), vvars[j // 2][ab]))
            o_ref[:, 128 * j:128 * (j + 1)] = parts[0] + parts[1]

    prev = lambda n: jnp.maximum(n - 1, 0)
    return pl.pallas_call(
        body, grid=(nb,),
        in_specs=[pl.BlockSpec((AB, D), lambda n: (n, C_AQ // D)),
                  pl.BlockSpec((AB, 256), lambda n: (n, C_AK // 256)),
                  pl.BlockSpec((AB, 256), lambda n: (prev(n), C_AK // 256)),
                  pl.BlockSpec((AB, 256), lambda n: (n, C_AV // 256)),
                  pl.BlockSpec((AB, 256), lambda n: (prev(n), C_AV // 256)),
                  pl.BlockSpec(memory_space=pltpu.SMEM)],
        out_specs=pl.BlockSpec((AB, D), lambda n: (n, 0)),
        out_shape=jax.ShapeDtypeStruct((t, D), F32),
        name="attn_fwd", compiler_params=_cp(32),
    )(proj, proj, proj, proj, proj, sinks)


def _attn_bwd(proj, sinks, do_a):
    t = proj.shape[0]
    nb = t // AB

    def body(q_ref, kc_ref, kp_ref, vc_ref, vp_ref, do_ref, sink_ref, dq_ref, dkv_ref, dsink_ref, carry):
        n = pl.program_id(0)

        @pl.when(n == 0)
        def _():
            dsink_ref[...] = jnp.zeros_like(dsink_ref)
            carry[...] = jnp.zeros_like(carry)

        @pl.when(n < nb)
        def _():
            cur, ok, cur_t = _attn_masks(n)
            low = lax.broadcasted_iota(jnp.int32, (2 * AB, 128), 1) < 64
            lane = lax.broadcasted_iota(jnp.int32, (8, 128), 1)
            row0 = lax.broadcasted_iota(jnp.int32, (8, 128), 0) == 0
            kvars = _kv_all(kp_ref, kc_ref)
            vvars = _kv_all(vp_ref, vc_ref)
            qps = [_bf(q_ref[:, 128 * j:128 * (j + 1)]) for j in range(8)]
            dops = [_bf(do_ref[:, 128 * j:128 * (j + 1)]) for j in range(8)]
            scores = [[_mm_nt(qps[j], kvars[j // 2][ab]) for ab in range(2)] for j in range(8)]
            dps = [[_mm_nt(dops[j], vvars[j // 2][ab]) for ab in range(2)] for j in range(8)]
            dsink = jnp.zeros((8, 128), F32)
            dk_ab = [[None, None] for _ in range(4)]
            dv_ab = [[None, None] for _ in range(4)]
            for j in range(8):
                g = j // 2
                dqp = None
                for ab in range(2):
                    head = 2 * j + ab
                    pc, ps = _attn_softmax(scores[j][ab], sink_ref[0, head], cur, ok)
                    dpc = jnp.where(cur, dps[j][ab][:, AB:], dps[j][ab][:, :AB])
                    rs = jnp.sum(pc * dpc, axis=-1, keepdims=True)
                    dsc = pc * (dpc - rs)
                    dsink = dsink + jnp.where(row0 & (lane == head), -jnp.sum(ps * rs), 0.0)
                    term = _mm(_bf(_spread(dsc, cur)), kvars[g][ab])
                    dqp = term if dqp is None else dqp + term
                    dk_t = _mm(_bf(_spread_t(dsc.T, cur_t)), qps[j])
                    dv_t = _mm(_bf(_spread_t(pc.T, cur_t)), dops[j])
                    dk_ab[g][ab] = dk_t if dk_ab[g][ab] is None else dk_ab[g][ab] + dk_t
                    dv_ab[g][ab] = dv_t if dv_ab[g][ab] is None else dv_ab[g][ab] + dv_t
                dq_ref[:, 128 * j:128 * (j + 1)] = _bf(dqp * ATT_SCALE)
            for tl in range(2):
                ke, ko = dk_ab[2 * tl], dk_ab[2 * tl + 1]
                ve, vo = dv_ab[2 * tl], dv_ab[2 * tl + 1]
                dkt = (jnp.where(low, ke[0], 0.0) + pltpu.roll(jnp.where(low, 0.0, ke[1]), 64, 1)
                       + jnp.where(low, 0.0, ko[1]) + pltpu.roll(jnp.where(low, ko[0], 0.0), 64, 1)) * ATT_SCALE
                dvt = (jnp.where(low, ve[0], 0.0) + pltpu.roll(jnp.where(low, 0.0, ve[1]), 64, 1)
                       + jnp.where(low, 0.0, vo[1]) + pltpu.roll(jnp.where(low, vo[0], 0.0), 64, 1))
                kcols = slice(tl * 128, (tl + 1) * 128)
                vcols = slice(256 + tl * 128, 256 + (tl + 1) * 128)
                dkv_ref[:, kcols] = _bf(carry[:, kcols] + dkt[0:AB])
                dkv_ref[:, vcols] = _bf(carry[:, vcols] + dvt[0:AB])
                carry[:, kcols] = dkt[AB:2 * AB]
                carry[:, vcols] = dvt[AB:2 * AB]
            dsink_ref[...] += dsink

        @pl.when(n == nb)
        def _():
            dkv_ref[...] = _bf(carry[...])

    cur = lambda n: jnp.minimum(n, nb - 1)
    prev = lambda n: jnp.clip(n - 1, 0, nb - 1)
    return pl.pallas_call(
        body, grid=(nb + 1,),
        in_specs=[pl.BlockSpec((AB, D), lambda n: (cur(n), C_AQ // D)),
                  pl.BlockSpec((AB, 256), lambda n: (cur(n), C_AK // 256)),
                  pl.BlockSpec((AB, 256), lambda n: (prev(n), C_AK // 256)),
                  pl.BlockSpec((AB, 256), lambda n: (cur(n), C_AV // 256)),
                  pl.BlockSpec((AB, 256), lambda n: (prev(n), C_AV // 256)),
                  pl.BlockSpec((AB, D), lambda n: (cur(n), 0)),
                  pl.BlockSpec(memory_space=pltpu.SMEM)],
        out_specs=[pl.BlockSpec((AB, D), lambda n: (cur(n), 0)),
                   pl.BlockSpec((AB, 512), lambda n: (prev(n), 0)),
                   pl.BlockSpec((8, 128), lambda n: (0, 0))],
        out_shape=[jax.ShapeDtypeStruct((t, D), BF16), jax.ShapeDtypeStruct((t, 512), BF16),
                   jax.ShapeDtypeStruct((8, 128), F32)],
        scratch_shapes=[pltpu.VMEM((AB, 512), F32)],
        name="attn_bwd", compiler_params=_cp(32),
    )(proj, proj, proj, proj, proj, do_a, sinks)


def _silu_and_grad(v):
    s = _sigmoid(v)
    return v * s, s * (1.0 + v * (1.0 - s))


def _tail(o_raw, o_a, proj, x2d, tgt, wbh, wba, wout, hnw, fnw, tb):
    t = x2d.shape[0]

    def body(or_ref, oa_ref, hg_ref, ag_ref, mh_ref, ma_ref, x_ref, t_ref, wbh_ref, wba_ref, wout_ref, hnw_ref, fnw_ref,
             dx2_ref, dor_ref, doa_ref, dt_ref, ght_ref, gat_ref, mgt_ref, dyh_ref, dya_ref, dx2b_ref, sums_ref):
        @pl.when(pl.program_id(0) == 0)
        def _():
            sums_ref[...] = jnp.zeros_like(sums_ref)

        hnw_v = hnw_ref[...]
        fnw_v = fnw_ref[...]
        o = or_ref[...]
        rs, xhs = [], []
        for h in range(HEADS):
            oh = o[:, h * HD:(h + 1) * HD]
            r = lax.rsqrt(jnp.mean(oh * oh, axis=-1, keepdims=True) + EPS)
            rs.append(r)
            xhs.append(oh * r)
        xh = jnp.concatenate(xhs, axis=1)
        on = xh * hnw_v
        sil_hg, dsil_hg = _silu_and_grad(hg_ref[...])
        gh = on * sil_hg
        gh_b = _bf(gh)
        y_h = _mm(gh_b, wbh_ref[...])
        oa = oa_ref[...]
        sil_ag, dsil_ag = _silu_and_grad(ag_ref[...])
        ga_b = _bf(oa * sil_ag)
        y_a = _mm(ga_b, wba_ref[...])
        s_mh = _sigmoid(mh_ref[...])
        s_ma = _sigmoid(ma_ref[...])
        mg_b = _bf(s_mh * y_h + s_ma * y_a)
        x2 = x_ref[...] + _mm(mg_b, wout_ref[...])
        r2 = lax.rsqrt(jnp.mean(x2 * x2, axis=-1, keepdims=True) + EPS)
        xh2 = x2 * r2
        err = xh2 * fnw_v - t_ref[...]
        loss = 0.5 * jnp.sum(jnp.mean(err * err, axis=-1, keepdims=True))
        dy = err * (1.0 / D)
        dfnw = jnp.sum(dy * xh2, axis=0, keepdims=True)
        dxh2 = dy * fnw_v
        dx2 = r2 * (dxh2 - xh2 * jnp.mean(dxh2 * xh2, axis=-1, keepdims=True))
        dx2_ref[...] = dx2
        dx2_b = _bf(dx2)
        dmg = _mm_nt(dx2_b, wout_ref[...])
        dyh_b = _bf(dmg * s_mh)
        dya_b = _bf(dmg * s_ma)
        dt_ref[:, 2 * D:3 * D] = _bf(dmg * y_h * s_mh * (1.0 - s_mh))
        dt_ref[:, 3 * D:4 * D] = _bf(dmg * y_a * s_ma * (1.0 - s_ma))
        dgh = _mm_nt(dyh_b, wbh_ref[...])
        dga = _mm_nt(dya_b, wba_ref[...])
        doa_ref[...] = dga * sil_ag
        dt_ref[:, D:2 * D] = _bf(dga * oa * dsil_ag)
        dt_ref[:, 0:D] = _bf(dgh * on * dsil_hg)
        don = dgh * sil_hg
        dhnw = jnp.sum(don * xh, axis=0, keepdims=True)
        dxh = don * hnw_v
        dos = []
        for h in range(HEADS):
            sl = slice(h * HD, (h + 1) * HD)
            dos.append(rs[h] * (dxh[:, sl] - xhs[h] * jnp.mean(dxh[:, sl] * xhs[h], axis=-1, keepdims=True)))
        dor_ref[...] = jnp.concatenate(dos, axis=1)
        ght_ref[...] = _bf(gh.T)
        gat_ref[...] = _bf((oa * sil_ag).T)
        mgt_ref[...] = _bf((s_mh * y_h + s_ma * y_a).T)
        dyh_ref[...] = dyh_b
        dya_ref[...] = dya_b
        dx2b_ref[...] = dx2_b
        row = lax.broadcasted_iota(jnp.int32, (8, D), 0)
        sums_ref[...] += jnp.where(row == 0, dfnw, 0.0) + jnp.where(row == 1, dhnw, 0.0) + jnp.where(row == 2, loss, 0.0)

    rowblk = lambda c: pl.BlockSpec((tb, D), lambda i: (i, c))
    full = lambda shape: pl.BlockSpec(shape, lambda i: (0, 0))
    tblk = pl.BlockSpec((D, tb), lambda i: (0, i))
    return pl.pallas_call(
        body, grid=(t // tb,),
        in_specs=[rowblk(0), rowblk(0), rowblk(C_HG // D), rowblk(C_AG // D), rowblk(C_MH // D), rowblk(C_MA // D),
                  rowblk(0), rowblk(0), full((D, D)), full((D, D)), full((D, D)), full((1, D)), full((1, D))],
        out_specs=[rowblk(0), rowblk(0), rowblk(0), pl.BlockSpec((tb, 4 * D), lambda i: (i, 0)),
                   tblk, tblk, tblk, rowblk(0), rowblk(0), rowblk(0), full((8, D))],
        out_shape=[jax.ShapeDtypeStruct((t, D), F32)] * 3 + [jax.ShapeDtypeStruct((t, 4 * D), BF16)]
        + [jax.ShapeDtypeStruct((D, t), BF16)] * 3 + [jax.ShapeDtypeStruct((t, D), BF16)] * 3
        + [jax.ShapeDtypeStruct((8, D), F32)],
        name="tail", compiler_params=_cp(56),
    )(o_raw, o_a, proj, proj, proj, proj, x2d, tgt, wbh, wba, wout, hnw, fnw)


def _wgrad3(ght, dyh, gat, dya, mgt, dx2b, tk):
    t = dyh.shape[0]

    def body(a0, b0, a1, b1, a2, b2, o0, o1, o2):
        @pl.when(pl.program_id(0) == 0)
        def _():
            o0[...] = jnp.zeros_like(o0)
            o1[...] = jnp.zeros_like(o1)
            o2[...] = jnp.zeros_like(o2)

        o0[...] += _mm(a0[...], b0[...])
        o1[...] += _mm(a1[...], b1[...])
        o2[...] += _mm(a2[...], b2[...])

    lhs = pl.BlockSpec((D, tk), lambda k: (0, k))
    rhs = pl.BlockSpec((tk, D), lambda k: (k, 0))
    out = pl.BlockSpec((D, D), lambda k: (0, 0))
    return pl.pallas_call(
        body, grid=(t // tk,), in_specs=[lhs, rhs] * 3, out_specs=[out] * 3,
        out_shape=[jax.ShapeDtypeStruct((D, D), F32)] * 3,
        name="wgrad3", compiler_params=_cp(48),
    )(ght, dyh, gat, dya, mgt, dx2b)


def _inproj_wgrad_piece(xnt, piece, nb, name):
    t = xnt.shape[1]
    width = piece.shape[1]

    def body(xnt_ref, p_ref, o_ref):
        o_ref[...] = _mm(xnt_ref[...], p_ref[...])

    return pl.pallas_call(
        body, grid=(width // nb,),
        in_specs=[pl.BlockSpec((D, t), lambda j: (0, 0), pipeline_mode=pl.Buffered(1)),
                  pl.BlockSpec((t, nb), lambda j: (0, j))],
        out_specs=pl.BlockSpec((D, nb), lambda j: (0, j)),
        out_shape=jax.ShapeDtypeStruct((D, width), F32),
        name=name, compiler_params=_cp(56),
    )(xnt, piece)


def _inproj_dgrad(pieces, w_p, x2d, dx2, norm_w, tb):
    t = x2d.shape[0]

    def body(p0, p1, p2, p3, w_ref, x_ref, dx2_ref, nw_ref, gx_ref, dnw_ref):
        @pl.when(pl.program_id(0) == 0)
        def _():
            dnw_ref[...] = jnp.zeros_like(dnw_ref)

        dxn = None
        off = 0
        for p in (p0, p1, p2, p3):
            width = p.shape[1]
            term = _mm_nt(p[...], w_ref[:, off:off + width])
            dxn = term if dxn is None else dxn + term
            off += width
        xv = x_ref[...]
        r = lax.rsqrt(jnp.mean(xv * xv, axis=-1, keepdims=True) + EPS)
        xh = xv * r
        dxh = dxn * nw_ref[...]
        gx_ref[...] = dx2_ref[...] + r * (dxh - xh * jnp.mean(dxh * xh, axis=-1, keepdims=True))
        row0 = lax.broadcasted_iota(jnp.int32, (8, D), 0) == 0
        dnw_ref[...] += jnp.where(row0, jnp.sum(dxn * xh, axis=0, keepdims=True), 0.0)

    rowblk = pl.BlockSpec((tb, D), lambda i: (i, 0))
    return pl.pallas_call(
        body, grid=(t // tb,),
        in_specs=[pl.BlockSpec((tb, p.shape[1]), lambda i: (i, 0)) for p in pieces]
        + [pl.BlockSpec((D, DIN), lambda i: (0, 0), pipeline_mode=pl.Buffered(1)), rowblk, rowblk,
           pl.BlockSpec((1, D), lambda i: (0, 0))],
        out_specs=[rowblk, pl.BlockSpec((8, D), lambda i: (0, 0))],
        out_shape=[jax.ShapeDtypeStruct((t, D), F32), jax.ShapeDtypeStruct((8, D), F32)],
        name="inproj_dgrad", compiler_params=_cp(60),
    )(*pieces, w_p, x2d, dx2, norm_w)


def _adamw_math(w, g, m, v):
    m = B1 * m + (1.0 - B1) * g
    v = B2 * v + (1.0 - B2) * (g * g)
    m_hat = m / (1.0 - B1 ** STEP)
    v_hat = v / (1.0 - B2 ** STEP)
    delta = -LR * (m_hat / (jnp.sqrt(v_hat) + ADAM_EPS) + WD * w)
    return delta, m, v


def _adamw_shard(parts, w, m, v, rows, name):
    _, nr, nc = parts.shape

    def body(p_ref, w_ref, m_ref, v_ref, g_ref, d_ref, nm_ref, nv_ref):
        g = p_ref[0].astype(F32)
        for s in range(1, NDEV):
            g = g + p_ref[s].astype(F32)
        d, nm, nv = _adamw_math(w_ref[...], g, m_ref[...], v_ref[...])
        g_ref[...] = g
        d_ref[...] = d
        nm_ref[...] = nm
        nv_ref[...] = nv

    blk = pl.BlockSpec((rows, nc), lambda i: (i, 0))
    return pl.pallas_call(
        body, grid=(nr // rows,),
        in_specs=[pl.BlockSpec((NDEV, rows, nc), lambda i: (0, i, 0)), blk, blk, blk],
        out_specs=[blk] * 4, out_shape=[jax.ShapeDtypeStruct((nr, nc), F32)] * 4,
        name=name, compiler_params=_cp(48),
    )(parts, w, m, v)


def _adamw_small(parts, w, m, v):
    def body(p_ref, w_ref, m_ref, v_ref, g_ref, d_ref, nm_ref, nv_ref):
        g = p_ref[0]
        for s in range(1, NDEV):
            g = g + p_ref[s]
        g_ref[...] = g
        lb = _lower_bound(w_ref.at[1:3, :])
        ga0 = g_ref[1:2, :] * lb * (1.0 - lb)
        g_ref[1:2, :] = ga0
        g_ref[2:3, :] = -ga0
        d, nm, nv = _adamw_math(w_ref[...], g_ref[...], m_ref[...], v_ref[...])
        d_ref[...] = d
        nm_ref[...] = nm
        nv_ref[...] = nv

    return pl.pallas_call(
        body, out_shape=[jax.ShapeDtypeStruct((8, D), F32)] * 4, name="adamw_small",
    )(parts, w, m, v)


def _to_internal_cols(w):
    hg = w[:, 0:3072].reshape(D, 3, HEADS, HD).transpose(0, 2, 1, 3).reshape(D, 3072)
    return jnp.concatenate([hg, w[:, 4096:5120], w[:, 3072:4096], w[:, 5632:8704], w[:, 5120:5632]], axis=1)


def _to_reference_cols(d_hgrn, d_aq, d_gates, d_kv):
    hg = d_hgrn.reshape(D, HEADS, 3, HD).transpose(0, 2, 1, 3).reshape(D, 3072)
    return jnp.concatenate([hg, d_gates[:, 0:D], d_aq, d_kv, d_gates[:, D:4 * D]], axis=1)


def _pack_small(rows):
    out = [jnp.pad(r.reshape(1, -1), ((0, 0), (0, D - r.size))) for r in rows]
    out.append(jnp.zeros((8 - len(rows), D), F32))
    return jnp.concatenate(out, axis=0)


def kernel(x, norm_w, w_in, hgrn_lower_bound, hgrn_norm_w, w_branch_hgrn, attn_sinks, w_branch_attn, w_out, final_norm_w, loss_target, m_norm_w, m_w_in, m_hgrn_lower_bound, m_hgrn_norm_w, m_w_branch_hgrn, m_attn_sinks, m_w_branch_attn, m_w_out, m_final_norm_w, v_norm_w, v_w_in, v_hgrn_lower_bound, v_hgrn_norm_w, v_w_branch_hgrn, v_attn_sinks, v_w_branch_attn, v_w_out, v_final_norm_w):
    t = x.shape[1]
    x2d = x.reshape(t, D)
    tgt = loss_target.reshape(t, D)
    fnw = final_norm_w.reshape(1, D)
    row_blk = min(256, t)
    big_blk = min(512, t)

    win_g, wbh_g, wba_g, wout_g = _gather_weights(w_in[0], w_branch_hgrn[0], w_branch_attn[0], w_out[0])
    w_p = _to_internal_cols(win_g.transpose(1, 0, 2).reshape(D, DIN))
    wbh, wba, wout = wbh_g.reshape(D, D), wba_g.reshape(D, D), wout_g.reshape(D, D)

    proj, xnt = _inproj_fwd(x2d, norm_w, w_p, min(1024, t), 2176)
    o_raw, states = _hgrn_fwd(proj, hgrn_lower_bound, big_blk, HGRN_GROUP)
    o_a = _attn_fwd(proj, attn_sinks)
    (dx2, do_raw, do_a, d_tail, ght, gat, mgt, dyh, dya, dx2b, sums) = _tail(
        o_raw, o_a, proj, x2d, tgt, wbh, wba, wout, hgrn_norm_w, fnw, row_blk)
    dwbh, dwba, dwout = _wgrad3(ght, dyh, gat, dya, mgt, dx2b, big_blk)
    d_aq, d_kv, dsink = _attn_bwd(proj, attn_sinks, do_a)
    d_hgrn, dlb = _hgrn_bwd(proj, hgrn_lower_bound, do_raw, states, big_blk, HGRN_GROUP)
    pieces = (d_hgrn, d_aq, d_tail, d_kv)
    dw_pieces = [_inproj_wgrad_piece(xnt, p, CB, "inproj_wgrad_" + n)
                 for p, n in zip(pieces, ("hgrn", "aq", "gates", "kv"))]
    grad_x, dnw = _inproj_dgrad(pieces, w_p, x2d, dx2, norm_w, big_blk)

    dwin_r = _to_reference_cols(*dw_pieces).reshape(D, NDEV, IN_SHARD).transpose(1, 0, 2).astype(BF16)
    small = _pack_small([dnw[0:1], dlb[0:1], jnp.zeros((1, D), F32), sums[1:2], sums[0:1], dsink[0:1, 0:QH]])
    slots = lambda a: a.reshape(NDEV, ROW_SHARD, D).astype(BF16)
    rin, rbh, rba, rout, rsm = _exchange_grads(dwin_r, slots(dwbh), slots(dwba), slots(dwout), small)
    g_in, d_in, nm_in, nv_in = _adamw_shard(rin, w_in[0], m_w_in[0], v_w_in[0], 128, "adamw_w_in")
    g_bh, d_bh, nm_bh, nv_bh = _adamw_shard(rbh, w_branch_hgrn[0], m_w_branch_hgrn[0], v_w_branch_hgrn[0], 128, "adamw_w_bh")
    g_ba, d_ba, nm_ba, nv_ba = _adamw_shard(rba, w_branch_attn[0], m_w_branch_attn[0], v_w_branch_attn[0], 128, "adamw_w_ba")
    g_out, d_out, nm_out, nv_out = _adamw_shard(rout, w_out[0], m_w_out[0], v_w_out[0], 128, "adamw_w_out")
    pack = lambda a, b, c, d, e: _pack_small([a, b[0:1], b[1:2], c, d.reshape(1, D), e])
    small_out = _adamw_small(
        rsm,
        pack(norm_w, hgrn_lower_bound, hgrn_norm_w, final_norm_w, attn_sinks),
        pack(m_norm_w, m_hgrn_lower_bound, m_hgrn_norm_w, m_final_norm_w, m_attn_sinks),
        pack(v_norm_w, v_hgrn_lower_bound, v_hgrn_norm_w, v_final_norm_w, v_attn_sinks))

    loss = lax.psum(sums[2, 0], ("x", "y", "c"))

    def unpack(p):
        return dict(norm_w=p[0:1], lb=p[1:3], hnw=p[3:4], fnw=p[4], sinks=p[5:6, 0:QH])

    sg, sd, sm, sv = (unpack(p) for p in small_out)

    def group(s, w_in_v, bh, ba, out):
        return (s["norm_w"], w_in_v[None], s["lb"], s["hnw"], bh[None], s["sinks"], ba[None], out[None], s["fnw"])

    return (loss, grad_x.reshape(1, t, D),
            *group(sg, g_in, g_bh, g_ba, g_out), *group(sd, d_in, d_bh, d_ba, d_out),
            *group(sm, nm_in, nm_bh, nm_ba, nm_out), *group(sv, nv_in, nv_bh, nv_ba, nv_out))
```

```python
import functools

import jax
import jax.numpy as jnp
from jax import lax
from jax.experimental import pallas as pl
from jax.experimental.pallas import tpu as pltpu

F32 = jnp.float32
BF16 = jnp.bfloat16

D = 1024
DIN = 8704
NDEV = 8
IN_SHARD = DIN // NDEV
ROW_SHARD = D // NDEV
HEADS = 8
HD = 128
CH = 64
HGRN_GROUP = 8
QH = 16
AB = 128
EPS = 1e-6
NEG = -1e30
ATT_SCALE = 0.125

C_HGRN = 0
C_AQ = 3072
C_HG = 4096
C_AG = 5120
C_MH = 6144
C_MA = 7168
C_AK = 8192
C_AV = 8448
CB = 512

LR = 0.001
B1 = 0.9
B2 = 0.999
ADAM_EPS = 1e-08
WD = 0.01
STEP = 10

V7X_VMEM_BYTES = 64 * 1024 * 1024
MESH = pl.DeviceIdType.MESH


def _cp(vmem_mb):
    return pltpu.CompilerParams(vmem_limit_bytes=vmem_mb * 1024 * 1024)


def _mm(a, b):
    return jnp.dot(a, b, preferred_element_type=F32)


def _mm_nt(a, b):
    return lax.dot_general(a, b, (((1,), (1,)), ((), ())), preferred_element_type=F32)


def _mm_tn(a, b):
    return lax.dot_general(a, b, (((0,), (0,)), ((), ())), preferred_element_type=F32)


def _mm_exact(a, b):
    return jnp.dot(a, b, preferred_element_type=F32, precision=lax.Precision.HIGHEST)


def _sigmoid(v):
    return 1.0 / (1.0 + jnp.exp(-v))


def _bf(v):
    return v.astype(BF16)


def _place():
    x, y, c = lax.axis_index("x"), lax.axis_index("y"), lax.axis_index("c")
    return (x, y, c), (x, y, 1 - c), [(1 - x, y), (x, 1 - y), (1 - x, 1 - y)]


def _dev_index(px, py, pc):
    return 4 * px + 2 * py + pc


def _gather_weights(w_in_s, wbh_s, wba_s, wout_s):
    def body(win_ref, wbh_ref, wba_ref, wout_ref, win_g, wbh_g, wba_g, wout_g, sin, s3, send_sems, recv_sems, loc_sems):
        (x, y, c), sibling, chips = _place()
        me = _dev_index(x, y, c)
        sin[...] = win_ref[...].astype(BF16)
        s3[0] = wbh_ref[...].astype(BF16)
        s3[1] = wba_ref[...].astype(BF16)
        s3[2] = wout_ref[...].astype(BF16)
        srcs = [sin, s3.at[0], s3.at[1], s3.at[2]]
        outs = [win_g, wbh_g, wba_g, wout_g]

        def copy(kind, k, block, to, src=None):
            return pltpu.make_async_remote_copy(
                src_ref=srcs[k] if src is None else src, dst_ref=outs[k].at[block], send_sem=send_sems.at[kind, k],
                recv_sem=recv_sems.at[kind, k], device_id=to, device_id_type=MESH)

        local = [pltpu.make_async_copy(srcs[k], outs[k].at[me], loc_sems.at[k]) for k in range(4)]
        for cp in local:
            cp.start()
        first = [copy(0, k, me, sibling) for k in range(4)]
        first += [copy(1 + j, k, me, (*chip, c)) for j, chip in enumerate(chips) for k in range(4)]
        for cp in first:
            cp.start()
        passed = []
        for j, chip in enumerate(chips):
            block = _dev_index(*chip, c)
            for k in range(4):
                copy(1 + j, k, block, (x, y, c)).wait_recv()
            for k in range(4):
                cp = copy(4 + j, k, block, sibling, src=outs[k].at[block])
                cp.start()
                passed.append(cp)
        for k in range(4):
            copy(0, k, _dev_index(x, y, 1 - c), (x, y, c)).wait_recv()
        for j, chip in enumerate(chips):
            for k in range(4):
                copy(4 + j, k, _dev_index(*chip, 1 - c), (x, y, c)).wait_recv()
        for cp in first + passed:
            cp.wait_send()
        for cp in local:
            cp.wait()

    vm = pl.BlockSpec(memory_space=pltpu.VMEM)
    hbm = pl.BlockSpec(memory_space=pl.ANY)
    return pl.pallas_call(
        body,
        out_shape=[jax.ShapeDtypeStruct((NDEV, D, IN_SHARD), BF16)] + [jax.ShapeDtypeStruct((NDEV, ROW_SHARD, D), BF16)] * 3,
        in_specs=[vm, vm, vm, vm],
        out_specs=[hbm, hbm, hbm, hbm],
        scratch_shapes=[pltpu.VMEM((D, IN_SHARD), BF16), pltpu.VMEM((3, ROW_SHARD, D), BF16),
                        pltpu.SemaphoreType.DMA((NDEV - 1, 4)), pltpu.SemaphoreType.DMA((NDEV - 1, 4)),
                        pltpu.SemaphoreType.DMA((4,))],
        name="gather_weights", compiler_params=_cp(32),
    )(w_in_s, wbh_s, wba_s, wout_s)


def _exchange_pair(arrs):
    n = len(arrs)

    def body(*refs):
        ins, own, got = refs[:n], refs[n:2 * n], refs[2 * n:3 * n]
        send_sems, recv_sems, loc_sems = refs[3 * n:]
        (x, y, c), sibling, _ = _place()
        local, sends = [], []
        for q in range(4):
            for k in range(n):
                local.append(pltpu.make_async_copy(ins[k].at[_dev_index(q // 2, q % 2, c)], own[k].at[q], loc_sems.at[q, k]))
                sends.append(pltpu.make_async_remote_copy(
                    src_ref=ins[k].at[_dev_index(q // 2, q % 2, 1 - c)], dst_ref=got[k].at[q], send_sem=send_sems.at[q, k],
                    recv_sem=recv_sems.at[q, k], device_id=sibling, device_id_type=MESH))
        for cp in sends + local:
            cp.start()
        for cp in sends:
            cp.wait_recv()
        for cp in sends:
            cp.wait_send()
        for cp in local:
            cp.wait()

    hbm = pl.BlockSpec(memory_space=pl.ANY)
    quarter = [jax.ShapeDtypeStruct((4,) + a.shape[1:], a.dtype) for a in arrs]
    res = pl.pallas_call(
        body, out_shape=quarter + quarter, in_specs=[hbm] * n, out_specs=[hbm] * (2 * n),
        scratch_shapes=[pltpu.SemaphoreType.DMA((4, n)), pltpu.SemaphoreType.DMA((4, n)), pltpu.SemaphoreType.DMA((4, n))],
        name="exchange_pair",
    )(*arrs)
    return res[:n], res[n:]


def _pair_sum(own, got, rows, name):
    _, nr, nc = own.shape

    def body(a_ref, b_ref, o_ref):
        o_ref[...] = (a_ref[...].astype(F32) + b_ref[...].astype(F32)).astype(BF16)

    blk = pl.BlockSpec((1, rows, nc), lambda q, i: (q, i, 0))
    return pl.pallas_call(
        body, grid=(4, nr // rows), in_specs=[blk, blk], out_specs=blk,
        out_shape=jax.ShapeDtypeStruct(own.shape, BF16), name=name,
    )(own, got)


def _exchange_chips(sums, small):
    n = len(sums)

    def body(*refs):
        ins, sm = refs[:n], refs[n]
        outs, rsm = refs[n + 1:2 * n + 1], refs[2 * n + 1]
        send_sems, recv_sems, loc_sems, ssend, srecv = refs[2 * n + 2:]
        (x, y, c), _, chips = _place()
        me = _dev_index(x, y, c)
        my_chip = 2 * x + y
        peers = [(1 - x if r & 4 else x, 1 - y if r & 2 else y, 1 - c if r & 1 else c) for r in range(1, NDEV)]
        local = [pltpu.make_async_copy(ins[k].at[my_chip], outs[k].at[my_chip], loc_sems.at[k]) for k in range(n)]
        local.append(pltpu.make_async_copy(sm, rsm.at[me], loc_sems.at[n]))
        copies = []
        for j, (px, py) in enumerate(chips):
            for k in range(n):
                copies.append(pltpu.make_async_remote_copy(
                    src_ref=ins[k].at[2 * px + py], dst_ref=outs[k].at[my_chip], send_sem=send_sems.at[j, k],
                    recv_sem=recv_sems.at[j, k], device_id=(px, py, c), device_id_type=MESH))
        for r, peer in enumerate(peers):
            copies.append(pltpu.make_async_remote_copy(
                src_ref=sm, dst_ref=rsm.at[me], send_sem=ssend.at[r], recv_sem=srecv.at[r],
                device_id=peer, device_id_type=MESH))
        for cp in copies + local:
            cp.start()
        for j, (px, py) in enumerate(chips):
            for k in range(n):
                pltpu.make_async_remote_copy(
                    src_ref=ins[k].at[my_chip], dst_ref=outs[k].at[2 * px + py], send_sem=send_sems.at[j, k],
                    recv_sem=recv_sems.at[j, k], device_id=(px, py, c), device_id_type=MESH).wait_recv()
        for r, peer in enumerate(peers):
            pltpu.make_async_remote_copy(
                src_ref=sm, dst_ref=rsm.at[_dev_index(*peer)], send_sem=ssend.at[r], recv_sem=srecv.at[r],
                device_id=peer, device_id_type=MESH).wait_recv()
        for cp in copies:
            cp.wait_send()
        for cp in local:
            cp.wait()

    hbm = pl.BlockSpec(memory_space=pl.ANY)
    res = pl.pallas_call(
        body,
        out_shape=[jax.ShapeDtypeStruct(a.shape, a.dtype) for a in sums] + [jax.ShapeDtypeStruct((NDEV,) + small.shape, F32)],
        in_specs=[hbm] * (n + 1), out_specs=[hbm] * (n + 1),
        scratch_shapes=[pltpu.SemaphoreType.DMA((3, n)), pltpu.SemaphoreType.DMA((3, n)), pltpu.SemaphoreType.DMA((n + 1,)),
                        pltpu.SemaphoreType.DMA((NDEV - 1,)), pltpu.SemaphoreType.DMA((NDEV - 1,))],
        name="exchange_chips",
    )(*sums, small)
    return res[:n], res[n]


def _inproj_fwd(x2d, norm_w, w_p, tb, nb):
    t = x2d.shape[0]

    def body(x_ref, nw_ref, w_ref, proj_ref, xnt_ref, xn_s):
        @pl.when(pl.program_id(1) == 0)
        def _():
            xv = x_ref[...]
            r = lax.rsqrt(jnp.mean(xv * xv, axis=-1, keepdims=True) + EPS)
            xn = (xv * r) * nw_ref[...]
            xn_s[...] = xn.astype(BF16)
            xnt_ref[...] = xn.T.astype(BF16)

        proj_ref[...] = _mm(xn_s[...], w_ref[...])

    return pl.pallas_call(
        body, grid=(t // tb, DIN // nb),
        in_specs=[pl.BlockSpec((tb, D), lambda i, j: (i, 0)), pl.BlockSpec((1, D), lambda i, j: (0, 0)),
                  pl.BlockSpec((D, nb), lambda i, j: (0, j))],
        out_specs=[pl.BlockSpec((tb, nb), lambda i, j: (i, j)), pl.BlockSpec((D, tb), lambda i, j: (0, i))],
        out_shape=[jax.ShapeDtypeStruct((t, DIN), F32), jax.ShapeDtypeStruct((D, t), BF16)],
        scratch_shapes=[pltpu.VMEM((tb, D), BF16)],
        name="inproj_fwd", compiler_params=_cp(56),
    )(x2d, norm_w, w_p)


def _lower_bound(lb_ref):
    a0 = lb_ref[0:1, :]
    a1 = lb_ref[1:2, :]
    mx = jnp.maximum(a0, a1)
    e0 = jnp.exp(a0 - mx)
    e1 = jnp.exp(a1 - mx)
    return e0 / (e0 + e1)


def _hgrn_chunk_fwd(hq, hf, lb, tril):
    sg = _sigmoid(hf)
    f = lb + (1.0 - lb) * sg
    g = jnp.log(f)
    k = 1.0 - f
    sq = _sigmoid(hq)
    q = hq * sq
    b = _mm_exact(tril, g)
    last_row = lax.broadcasted_iota(jnp.int32, b.shape, 0) == CH - 1
    b_last = jnp.sum(jnp.where(last_row, b, 0.0), axis=0, keepdims=True)
    c = 0.5 * b_last
    eb = jnp.exp(b)
    ea = jnp.exp(b - c)
    ek = jnp.exp(c - b)
    ed = jnp.exp(b_last - b)
    ebl = jnp.exp(b_last)
    return dict(sg=sg, f=f, k=k, sq=sq, q=q, eb=eb, ea=ea, ek=ek, ed=ed, ebl=ebl,
                qe=q * eb, qa=q * ea, ka=k * ek, kd=k * ed)


def _tri(lower):
    r = lax.broadcasted_iota(jnp.int32, (CH, CH), 0)
    c = lax.broadcasted_iota(jnp.int32, (CH, CH), 1)
    return (r >= c) if lower else (c >= r)


def _head_segment(p_ref, rows, j, hg):
    return jnp.concatenate([p_ref[rows, 3 * HD * k + j * HD:3 * HD * k + (j + 1) * HD] for k in range(hg)], axis=1)


def _head(a, k):
    return a[:, k * HD:(k + 1) * HD]


def _hgrn_fwd(proj, lbw, rb, hg):
    t = proj.shape[0]
    ncb = rb // CH

    def body(p_ref, lb_ref, o_ref, st_ref, s_scr):
        @pl.when(pl.program_id(1) == 0)
        def _():
            s_scr[...] = jnp.zeros_like(s_scr)

        lb = _lower_bound(lb_ref)
        causal = _tri(True)
        tril = causal.astype(F32)
        heads = range(hg)

        def chunk(cc, carry):
            r0 = pl.multiple_of(cc * CH, CH)
            rows = pl.ds(r0, CH)
            e = _hgrn_chunk_fwd(_head_segment(p_ref, rows, 0, hg), _head_segment(p_ref, rows, 1, hg), lb, tril)
            v = _bf(_head_segment(p_ref, rows, 2, hg))
            sts = [s_scr[k] for k in heads]
            qa, ka, qe, kd = _bf(e["qa"]), _bf(e["ka"]), _bf(e["qe"]), _bf(e["kd"])
            a = [_bf(jnp.where(causal, _mm_nt(_head(qa, k), _head(ka, k)), 0.0)) for k in heads]
            o_inter = [_mm_nt(_head(qe, k), _bf(sts[k])) for k in heads]
            kv = [_mm_tn(_head(v, k), _head(kd, k)) for k in heads]
            o_intra = [_mm(a[k], _head(v, k)) for k in heads]
            for k in heads:
                st_ref[cc, k] = sts[k]
                o_ref[rows, k * HD:(k + 1) * HD] = o_inter[k] + o_intra[k]
                s_scr[k] = sts[k] * _head(e["ebl"], k) + kv[k]
            return carry

        lax.fori_loop(0, ncb, chunk, 0)

    return pl.pallas_call(
        body, grid=(HEADS // hg, t // rb),
        in_specs=[pl.BlockSpec((rb, 3 * HD * hg), lambda h, i: (i, h)), pl.BlockSpec((2, HD * hg), lambda h, i: (0, h))],
        out_specs=[pl.BlockSpec((rb, HD * hg), lambda h, i: (i, h)),
                   pl.BlockSpec((ncb, hg, HD, HD), lambda h, i: (i, h, 0, 0))],
        out_shape=[jax.ShapeDtypeStruct((t, D), F32), jax.ShapeDtypeStruct((t // CH, HEADS, HD, HD), F32)],
        scratch_shapes=[pltpu.VMEM((hg, HD, HD), F32)],
        name="hgrn_fwd", compiler_params=_cp(48),
    )(proj, lbw)


def _hgrn_bwd(proj, lbw, do_raw, states, rb, hg):
    t = proj.shape[0]
    nblk = t // rb
    ncb = rb // CH

    def body(p_ref, lb_ref, do_ref, st_ref, dp_ref, dlb_ref, ds_scr):
        @pl.when(pl.program_id(1) == 0)
        def _():
            ds_scr[...] = jnp.zeros_like(ds_scr)
            dlb_ref[...] = jnp.zeros_like(dlb_ref)

        lb = _lower_bound(lb_ref)
        causal = _tri(True)
        tril = causal.astype(F32)
        triu = _tri(False).astype(F32)
        last_row = lax.broadcasted_iota(jnp.int32, (CH, HD * hg), 0) == CH - 1
        row0 = lax.broadcasted_iota(jnp.int32, (8, HD * hg), 0) == 0
        heads = range(hg)
        wide = lambda parts: jnp.concatenate(parts, axis=1)

        def chunk(it, carry):
            cc = ncb - 1 - it
            r0 = pl.multiple_of(cc * CH, CH)
            rows = pl.ds(r0, CH)
            hq = _head_segment(p_ref, rows, 0, hg)
            e = _hgrn_chunk_fwd(hq, _head_segment(p_ref, rows, 1, hg), lb, tril)
            v = _bf(_head_segment(p_ref, rows, 2, hg))
            do = _bf(do_ref[rows, :])
            sts = [st_ref[cc, k] for k in heads]
            dsts = [ds_scr[k] for k in heads]
            dlb_acc = dlb_ref[...]
            qa, ka, qe, kd = _bf(e["qa"]), _bf(e["ka"]), _bf(e["qe"]), _bf(e["kd"])
            a = [_bf(jnp.where(causal, _mm_nt(_head(qa, k), _head(ka, k)), 0.0)) for k in heads]
            da = [_bf(jnp.where(causal, _mm_nt(_head(do, k), _head(v, k)), 0.0)) for k in heads]
            dqe = wide([_mm(_head(do, k), _bf(sts[k])) for k in heads])
            dkd = wide([_mm(_head(v, k), _bf(dsts[k])) for k in heads])
            dv_state = [_mm_nt(_head(kd, k), _bf(dsts[k])) for k in heads]
            ds_new = [_mm_tn(_head(do, k), _head(qe, k)) for k in heads]
            dv_intra = [_mm_tn(a[k], _head(do, k)) for k in heads]
            dqa = wide([_mm(da[k], _head(ka, k)) for k in heads])
            dka = wide([_mm_tn(da[k], _head(qa, k)) for k in heads])
            dv = wide([dv_intra[k] + dv_state[k] for k in heads])
            dbl = e["ebl"] * wide([jnp.sum(sts[k] * dsts[k], axis=0, keepdims=True) for k in heads])
            dq = dqe * e["eb"] + dqa * e["ea"]
            dk = dka * e["ek"] + dkd * e["ed"]
            dkd_kd = dkd * kd.astype(F32)
            db = dqe * qe.astype(F32) + dqa * qa.astype(F32) - dka * ka.astype(F32) - dkd_kd
            db = db + jnp.where(last_row, dbl + jnp.sum(dkd_kd, axis=0, keepdims=True), 0.0)
            dg = _mm_exact(triu, db)
            df = dg / e["f"] - dk
            sg = e["sg"]
            sq = e["sq"]
            dhq = _bf(dq * (sq * (1.0 + hq * (1.0 - sq))))
            dhf = _bf(df * (1.0 - lb) * sg * (1.0 - sg))
            dhi = _bf(dv)
            dlb_new = dlb_acc + jnp.where(row0, jnp.sum(df * (1.0 - sg), axis=0, keepdims=True), 0.0)
            for k in heads:
                c0 = 3 * HD * k
                ds_scr[k] = ds_new[k] + dsts[k] * _head(e["ebl"], k)
                dp_ref[rows, c0:c0 + HD] = _head(dhq, k)
                dp_ref[rows, c0 + HD:c0 + 2 * HD] = _head(dhf, k)
                dp_ref[rows, c0 + 2 * HD:c0 + 3 * HD] = _head(dhi, k)
            dlb_ref[...] = dlb_new
            return carry

        lax.fori_loop(0, ncb, chunk, 0)

    rev = lambda h, i: (nblk - 1 - i, h)
    return pl.pallas_call(
        body, grid=(HEADS // hg, nblk),
        in_specs=[pl.BlockSpec((rb, 3 * HD * hg), rev), pl.BlockSpec((2, HD * hg), lambda h, i: (0, h)),
                  pl.BlockSpec((rb, HD * hg), rev), pl.BlockSpec((ncb, hg, HD, HD), lambda h, i: (nblk - 1 - i, h, 0, 0))],
        out_specs=[pl.BlockSpec((rb, 3 * HD * hg), rev), pl.BlockSpec((8, HD * hg), lambda h, i: (0, h))],
        out_shape=[jax.ShapeDtypeStruct((t, 3 * D), BF16), jax.ShapeDtypeStruct((8, D), F32)],
        scratch_shapes=[pltpu.VMEM((hg, HD, HD), F32)],
        name="hgrn_bwd", compiler_params=_cp(48),
    )(proj, lbw, do_raw, states)


def _kv_variants(tile, odd):
    low = lax.broadcasted_iota(jnp.int32, tile.shape, 1) < 64
    if odd:
        hi = jnp.where(low, 0.0, tile)
        lo = pltpu.roll(hi, 64, 1)
    else:
        lo = jnp.where(low, tile, 0.0)
        hi = pltpu.roll(lo, 64, 1)
    return _bf(lo), _bf(hi)


def _attn_masks(n):
    qi = lax.broadcasted_iota(jnp.int32, (AB, AB), 0)
    kj = lax.broadcasted_iota(jnp.int32, (AB, AB), 1)
    cur = kj <= qi
    return cur, cur | (n > 0), qi <= kj


def _kv_all(prev_ref, cur_ref):
    out = []
    for tl in range(2):
        cols = slice(tl * 128, (tl + 1) * 128)
        tile = jnp.concatenate([prev_ref[:, cols], cur_ref[:, cols]], axis=0)
        out.append(_kv_variants(tile, 0))
        out.append(_kv_variants(tile, 1))
    return out


def _attn_softmax(s2, sink, cur, ok):
    s = jnp.where(ok, jnp.where(cur, s2[:, AB:], s2[:, :AB]) * ATT_SCALE, NEG)
    m = jnp.maximum(jnp.max(s, axis=-1, keepdims=True), sink)
    p = jnp.exp(s - m)
    es = jnp.exp(sink - m)
    inv = 1.0 / (jnp.sum(p, axis=-1, keepdims=True) + es)
    return p * inv, es * inv


def _spread(pc, cur):
    return jnp.concatenate([jnp.where(cur, 0.0, pc), jnp.where(cur, pc, 0.0)], axis=1)


def _spread_t(pct, cur_t):
    return jnp.concatenate([jnp.where(cur_t, 0.0, pct), jnp.where(cur_t, pct, 0.0)], axis=0)


def _attn_fwd(proj, sinks):
    t = proj.shape[0]
    nb = t // AB

    def body(q_ref, kc_ref, kp_ref, vc_ref, vp_ref, sink_ref, o_ref):
        cur, ok, _ = _attn_masks(pl.program_id(0))
        kvars = _kv_all(kp_ref, kc_ref)
        vvars = _kv_all(vp_ref, vc_ref)
        qps = [_bf(q_ref[:, 128 * j:128 * (j + 1)]) for j in range(8)]
        scores = [[_mm_nt(qps[j], kvars[j // 2][ab]) for ab in range(2)] for j in range(8)]
        for j in range(8):
            parts = []
            for ab in range(2):
                pc, _ = _attn_softmax(scores[j][ab], sink_ref[0, 2 * j + ab], cur, ok)
                parts.append(_mm(_bf(_spread(pc, cur)), vvars[j // 2][ab]))
            o_ref[:, 128 * j:128 * (j + 1)] = parts[0] + parts[1]

    prev = lambda n: jnp.maximum(n - 1, 0)
    return pl.pallas_call(
        body, grid=(nb,),
        in_specs=[pl.BlockSpec((AB, D), lambda n: (n, C_AQ // D)),
                  pl.BlockSpec((AB, 256), lambda n: (n, C_AK // 256)),
                  pl.BlockSpec((AB, 256), lambda n: (prev(n), C_AK // 256)),
                  pl.BlockSpec((AB, 256), lambda n: (n, C_AV // 256)),
                  pl.BlockSpec((AB, 256), lambda n: (prev(n), C_AV // 256)),
                  pl.BlockSpec(memory_space=pltpu.SMEM)],
        out_specs=pl.BlockSpec((AB, D), lambda n: (n, 0)),
        out_shape=jax.ShapeDtypeStruct((t, D), F32),
        name="attn_fwd", compiler_params=_cp(32),
    )(proj, proj, proj, proj, proj, sinks)


def _attn_bwd(proj, sinks, do_a):
    t = proj.shape[0]
    nb = t // AB

    def body(q_ref, kc_ref, kp_ref, vc_ref, vp_ref, do_ref, sink_ref, dq_ref, dkv_ref, dsink_ref, carry):
        n = pl.program_id(0)

        @pl.when(n == 0)
        def _():
            dsink_ref[...] = jnp.zeros_like(dsink_ref)
            carry[...] = jnp.zeros_like(carry)

        @pl.when(n < nb)
        def _():
            cur, ok, cur_t = _attn_masks(n)
            low = lax.broadcasted_iota(jnp.int32, (2 * AB, 128), 1) < 64
            lane = lax.broadcasted_iota(jnp.int32, (8, 128), 1)
            row0 = lax.broadcasted_iota(jnp.int32, (8, 128), 0) == 0
            kvars = _kv_all(kp_ref, kc_ref)
            vvars = _kv_all(vp_ref, vc_ref)
            qps = [_bf(q_ref[:, 128 * j:128 * (j + 1)]) for j in range(8)]
            dops = [_bf(do_ref[:, 128 * j:128 * (j + 1)]) for j in range(8)]
            scores = [[_mm_nt(qps[j], kvars[j // 2][ab]) for ab in range(2)] for j in range(8)]
            dps = [[_mm_nt(dops[j], vvars[j // 2][ab]) for ab in range(2)] for j in range(8)]
            dsink = jnp.zeros((8, 128), F32)
            dk_ab = [[None, None] for _ in range(4)]
            dv_ab = [[None, None] for _ in range(4)]
            for j in range(8):
                g = j // 2
                dqp = None
                for ab in range(2):
                    head = 2 * j + ab
                    pc, ps = _attn_softmax(scores[j][ab], sink_ref[0, head], cur, ok)
                    dpc = jnp.where(cur, dps[j][ab][:, AB:], dps[j][ab][:, :AB])
                    rs = jnp.sum(pc * dpc, axis=-1, keepdims=True)
                    dsc = pc * (dpc - rs)
                    dsink = dsink + jnp.where(row0 & (lane == head), -jnp.sum(ps * rs), 0.0)
                    term = _mm(_bf(_spread(dsc, cur)), kvars[g][ab])
                    dqp = term if dqp is None else dqp + term
                    dk_t = _mm(_bf(_spread_t(dsc.T, cur_t)), qps[j])
                    dv_t = _mm(_bf(_spread_t(pc.T, cur_t)), dops[j])
                    dk_ab[g][ab] = dk_t if dk_ab[g][ab] is None else dk_ab[g][ab] + dk_t
                    dv_ab[g][ab] = dv_t if dv_ab[g][ab] is None else dv_ab[g][ab] + dv_t
                dq_ref[:, 128 * j:128 * (j + 1)] = _bf(dqp * ATT_SCALE)
            for tl in range(2):
                ke, ko = dk_ab[2 * tl], dk_ab[2 * tl + 1]
                ve, vo = dv_ab[2 * tl], dv_ab[2 * tl + 1]
                dkt = (jnp.where(low, ke[0], 0.0) + pltpu.roll(jnp.where(low, 0.0, ke[1]), 64, 1)
                       + jnp.where(low, 0.0, ko[1]) + pltpu.roll(jnp.where(low, ko[0], 0.0), 64, 1)) * ATT_SCALE
                dvt = (jnp.where(low, ve[0], 0.0) + pltpu.roll(jnp.where(low, 0.0, ve[1]), 64, 1)
                       + jnp.where(low, 0.0, vo[1]) + pltpu.roll(jnp.where(low, vo[0], 0.0), 64, 1))
                kcols = slice(tl * 128, (tl + 1) * 128)
                vcols = slice(256 + tl * 128, 256 + (tl + 1) * 128)
                dkv_ref[:, kcols] = _bf(carry[:, kcols] + dkt[0:AB])
                dkv_ref[:, vcols] = _bf(carry[:, vcols] + dvt[0:AB])
                carry[:, kcols] = dkt[AB:2 * AB]
                carry[:, vcols] = dvt[AB:2 * AB]
            dsink_ref[...] += dsink

        @pl.when(n == nb)
        def _():
            dkv_ref[...] = _bf(carry[...])

    cur = lambda n: jnp.minimum(n, nb - 1)
    prev = lambda n: jnp.clip(n - 1, 0, nb - 1)
    return pl.pallas_call(
        body, grid=(nb + 1,),
        in_specs=[pl.BlockSpec((AB, D), lambda n: (cur(n), C_AQ // D)),
                  pl.BlockSpec((AB, 256), lambda n: (cur(n), C_AK // 256)),
                  pl.BlockSpec((AB, 256), lambda n: (prev(n), C_AK // 256)),
                  pl.BlockSpec((AB, 256), lambda n: (cur(n), C_AV // 256)),
                  pl.BlockSpec((AB, 256), lambda n: (prev(n), C_AV // 256)),
                  pl.BlockSpec((AB, D), lambda n: (cur(n), 0)),
                  pl.BlockSpec(memory_space=pltpu.SMEM)],
        out_specs=[pl.BlockSpec((AB, D), lambda n: (cur(n), 0)),
                   pl.BlockSpec((AB, 512), lambda n: (prev(n), 0)),
                   pl.BlockSpec((8, 128), lambda n: (0, 0))],
        out_shape=[jax.ShapeDtypeStruct((t, D), BF16), jax.ShapeDtypeStruct((t, 512), BF16),
                   jax.ShapeDtypeStruct((8, 128), F32)],
        scratch_shapes=[pltpu.VMEM((AB, 512), F32)],
        name="attn_bwd", compiler_params=_cp(32),
    )(proj, proj, proj, proj, proj, do_a, sinks)


def _silu_and_grad(v):
    s = _sigmoid(v)
    return v * s, s * (1.0 + v * (1.0 - s))


def _tail(o_raw, o_a, proj, x2d, tgt, wbh, wba, wout, hnw, fnw, tb):
    t = x2d.shape[0]

    def body(or_ref, oa_ref, hg_ref, ag_ref, mh_ref, ma_ref, x_ref, t_ref, wbh_ref, wba_ref, wout_ref, hnw_ref, fnw_ref,
             dx2_ref, dor_ref, doa_ref, dt_ref, ght_ref, gat_ref, mgt_ref, dyh_ref, dya_ref, dx2b_ref, sums_ref):
        @pl.when(pl.program_id(0) == 0)
        def _():
            sums_ref[...] = jnp.zeros_like(sums_ref)

        hnw_v = hnw_ref[...]
        fnw_v = fnw_ref[...]
        o = or_ref[...]
        rs, xhs = [], []
        for h in range(HEADS):
            oh = o[:, h * HD:(h + 1) * HD]
            r = lax.rsqrt(jnp.mean(oh * oh, axis=-1, keepdims=True) + EPS)
            rs.append(r)
            xhs.append(oh * r)
        xh = jnp.concatenate(xhs, axis=1)
        on = xh * hnw_v
        sil_hg, dsil_hg = _silu_and_grad(hg_ref[...])
        gh = on * sil_hg
        gh_b = _bf(gh)
        y_h = _mm(gh_b, wbh_ref[...])
        oa = oa_ref[...]
        sil_ag, dsil_ag = _silu_and_grad(ag_ref[...])
        ga_b = _bf(oa * sil_ag)
        y_a = _mm(ga_b, wba_ref[...])
        s_mh = _sigmoid(mh_ref[...])
        s_ma = _sigmoid(ma_ref[...])
        mg_b = _bf(s_mh * y_h + s_ma * y_a)
        x2 = x_ref[...] + _mm(mg_b, wout_ref[...])
        r2 = lax.rsqrt(jnp.mean(x2 * x2, axis=-1, keepdims=True) + EPS)
        xh2 = x2 * r2
        err = xh2 * fnw_v - t_ref[...]
        loss = 0.5 * jnp.sum(jnp.mean(err * err, axis=-1, keepdims=True))
        dy = err * (1.0 / D)
        dfnw = jnp.sum(dy * xh2, axis=0, keepdims=True)
        dxh2 = dy * fnw_v
        dx2 = r2 * (dxh2 - xh2 * jnp.mean(dxh2 * xh2, axis=-1, keepdims=True))
        dx2_ref[...] = dx2
        dx2_b = _bf(dx2)
        dmg = _mm_nt(dx2_b, wout_ref[...])
        dyh_b = _bf(dmg * s_mh)
        dya_b = _bf(dmg * s_ma)
        dt_ref[:, 2 * D:3 * D] = _bf(dmg * y_h * s_mh * (1.0 - s_mh))
        dt_ref[:, 3 * D:4 * D] = _bf(dmg * y_a * s_ma * (1.0 - s_ma))
        dgh = _mm_nt(dyh_b, wbh_ref[...])
        dga = _mm_nt(dya_b, wba_ref[...])
        doa_ref[...] = dga * sil_ag
        dt_ref[:, D:2 * D] = _bf(dga * oa * dsil_ag)
        dt_ref[:, 0:D] = _bf(dgh * on * dsil_hg)
        don = dgh * sil_hg
        dhnw = jnp.sum(don * xh, axis=0, keepdims=True)
        dxh = don * hnw_v
        dos = []
        for h in range(HEADS):
            sl = slice(h * HD, (h + 1) * HD)
            dos.append(rs[h] * (dxh[:, sl] - xhs[h] * jnp.mean(dxh[:, sl] * xhs[h], axis=-1, keepdims=True)))
        dor_ref[...] = jnp.concatenate(dos, axis=1)
        ght_ref[...] = _bf(gh.T)
        gat_ref[...] = _bf((oa * sil_ag).T)
        mgt_ref[...] = _bf((s_mh * y_h + s_ma * y_a).T)
        dyh_ref[...] = dyh_b
        dya_ref[...] = dya_b
        dx2b_ref[...] = dx2_b
        row = lax.broadcasted_iota(jnp.int32, (8, D), 0)
        sums_ref[...] += jnp.where(row == 0, dfnw, 0.0) + jnp.where(row == 1, dhnw, 0.0) + jnp.where(row == 2, loss, 0.0)

    rowblk = lambda c: pl.BlockSpec((tb, D), lambda i: (i, c))
    full = lambda shape: pl.BlockSpec(shape, lambda i: (0, 0))
    tblk = pl.BlockSpec((D, tb), lambda i: (0, i))
    return pl.pallas_call(
        body, grid=(t // tb,),
        in_specs=[rowblk(0), rowblk(0), rowblk(C_HG // D), rowblk(C_AG // D), rowblk(C_MH // D), rowblk(C_MA // D),
                  rowblk(0), rowblk(0), full((D, D)), full((D, D)), full((D, D)), full((1, D)), full((1, D))],
        out_specs=[rowblk(0), rowblk(0), rowblk(0), pl.BlockSpec((tb, 4 * D), lambda i: (i, 0)),
                   tblk, tblk, tblk, rowblk(0), rowblk(0), rowblk(0), full((8, D))],
        out_shape=[jax.ShapeDtypeStruct((t, D), F32)] * 3 + [jax.ShapeDtypeStruct((t, 4 * D), BF16)]
        + [jax.ShapeDtypeStruct((D, t), BF16)] * 3 + [jax.ShapeDtypeStruct((t, D), BF16)] * 3
        + [jax.ShapeDtypeStruct((8, D), F32)],
        name="tail", compiler_params=_cp(56),
    )(o_raw, o_a, proj, proj, proj, proj, x2d, tgt, wbh, wba, wout, hnw, fnw)


def _wgrad3(ght, dyh, gat, dya, mgt, dx2b, tk):
    t = dyh.shape[0]

    def body(a0, b0, a1, b1, a2, b2, o0, o1, o2):
        @pl.when(pl.program_id(0) == 0)
        def _():
            o0[...] = jnp.zeros_like(o0)
            o1[...] = jnp.zeros_like(o1)
            o2[...] = jnp.zeros_like(o2)

        o0[...] += _mm(a0[...], b0[...])
        o1[...] += _mm(a1[...], b1[...])
        o2[...] += _mm(a2[...], b2[...])

    lhs = pl.BlockSpec((D, tk), lambda k: (0, k))
    rhs = pl.BlockSpec((tk, D), lambda k: (k, 0))
    out = pl.BlockSpec((D, D), lambda k: (0, 0))
    return pl.pallas_call(
        body, grid=(t // tk,), in_specs=[lhs, rhs] * 3, out_specs=[out] * 3,
        out_shape=[jax.ShapeDtypeStruct((D, D), F32)] * 3,
        name="wgrad3", compiler_params=_cp(48),
    )(ght, dyh, gat, dya, mgt, dx2b)


def _inproj_wgrad_piece(xnt, piece, nb, name):
    t = xnt.shape[1]
    width = piece.shape[1]

    def body(xnt_ref, p_ref, o_ref):
        o_ref[...] = _mm(xnt_ref[...], p_ref[...])

    return pl.pallas_call(
        body, grid=(width // nb,),
        in_specs=[pl.BlockSpec((D, t), lambda j: (0, 0), pipeline_mode=pl.Buffered(1)),
                  pl.BlockSpec((t, nb), lambda j: (0, j))],
        out_specs=pl.BlockSpec((D, nb), lambda j: (0, j)),
        out_shape=jax.ShapeDtypeStruct((D, width), F32),
        name=name, compiler_params=_cp(56),
    )(xnt, piece)


def _inproj_dgrad(pieces, w_p, x2d, dx2, norm_w, tb):
    t = x2d.shape[0]

    def body(p0, p1, p2, p3, w_ref, x_ref, dx2_ref, nw_ref, gx_ref, dnw_ref):
        @pl.when(pl.program_id(0) == 0)
        def _():
            dnw_ref[...] = jnp.zeros_like(dnw_ref)

        dxn = None
        off = 0
        for p in (p0, p1, p2, p3):
            width = p.shape[1]
            term = _mm_nt(p[...], w_ref[:, off:off + width])
            dxn = term if dxn is None else dxn + term
            off += width
        xv = x_ref[...]
        r = lax.rsqrt(jnp.mean(xv * xv, axis=-1, keepdims=True) + EPS)
        xh = xv * r
        dxh = dxn * nw_ref[...]
        gx_ref[...] = dx2_ref[...] + r * (dxh - xh * jnp.mean(dxh * xh, axis=-1, keepdims=True))
        row0 = lax.broadcasted_iota(jnp.int32, (8, D), 0) == 0
        dnw_ref[...] += jnp.where(row0, jnp.sum(dxn * xh, axis=0, keepdims=True), 0.0)

    rowblk = pl.BlockSpec((tb, D), lambda i: (i, 0))
    return pl.pallas_call(
        body, grid=(t // tb,),
        in_specs=[pl.BlockSpec((tb, p.shape[1]), lambda i: (i, 0)) for p in pieces]
        + [pl.BlockSpec((D, DIN), lambda i: (0, 0), pipeline_mode=pl.Buffered(1)), rowblk, rowblk,
           pl.BlockSpec((1, D), lambda i: (0, 0))],
        out_specs=[rowblk, pl.BlockSpec((8, D), lambda i: (0, 0))],
        out_shape=[jax.ShapeDtypeStruct((t, D), F32), jax.ShapeDtypeStruct((8, D), F32)],
        name="inproj_dgrad", compiler_params=_cp(60),
    )(*pieces, w_p, x2d, dx2, norm_w)


def _adamw_math(w, g, m, v):
    m = B1 * m + (1.0 - B1) * g
    v = B2 * v + (1.0 - B2) * (g * g)
    m_hat = m / (1.0 - B1 ** STEP)
    v_hat = v / (1.0 - B2 ** STEP)
    delta = -LR * (m_hat / (jnp.sqrt(v_hat) + ADAM_EPS) + WD * w)
    return delta, m, v


def _adamw_shard(parts, w, m, v, rows, name):
    nparts, nr, nc = parts.shape

    def body(p_ref, w_ref, m_ref, v_ref, g_ref, d_ref, nm_ref, nv_ref):
        g = p_ref[0].astype(F32)
        for s in range(1, nparts):
            g = g + p_ref[s].astype(F32)
        d, nm, nv = _adamw_math(w_ref[...], g, m_ref[...], v_ref[...])
        g_ref[...] = g
        d_ref[...] = d
        nm_ref[...] = nm
        nv_ref[...] = nv

    blk = pl.BlockSpec((rows, nc), lambda i: (i, 0))
    return pl.pallas_call(
        body, grid=(nr // rows,),
        in_specs=[pl.BlockSpec((nparts, rows, nc), lambda i: (0, i, 0)), blk, blk, blk],
        out_specs=[blk] * 4, out_shape=[jax.ShapeDtypeStruct((nr, nc), F32)] * 4,
        name=name, compiler_params=_cp(48),
    )(parts, w, m, v)


def _adamw_small(parts, w, m, v):
    def body(p_ref, w_ref, m_ref, v_ref, g_ref, d_ref, nm_ref, nv_ref):
        g = p_ref[0]
        for s in range(1, NDEV):
            g = g + p_ref[s]
        g_ref[...] = g
        lb = _lower_bound(w_ref.at[1:3, :])
        ga0 = g_ref[1:2, :] * lb * (1.0 - lb)
        g_ref[1:2, :] = ga0
        g_ref[2:3, :] = -ga0
        d, nm, nv = _adamw_math(w_ref[...], g_ref[...], m_ref[...], v_ref[...])
        d_ref[...] = d
        nm_ref[...] = nm
        nv_ref[...] = nv

    return pl.pallas_call(
        body, out_shape=[jax.ShapeDtypeStruct((8, D), F32)] * 4, name="adamw_small",
    )(parts, w, m, v)


def _to_internal_cols(w):
    hg = w[:, 0:3072].reshape(D, 3, HEADS, HD).transpose(0, 2, 1, 3).reshape(D, 3072)
    return jnp.concatenate([hg, w[:, 4096:5120], w[:, 3072:4096], w[:, 5632:8704], w[:, 5120:5632]], axis=1)


def _to_reference_cols(d_hgrn, d_aq, d_gates, d_kv):
    hg = d_hgrn.reshape(D, HEADS, 3, HD).transpose(0, 2, 1, 3).reshape(D, 3072)
    return jnp.concatenate([hg, d_gates[:, 0:D], d_aq, d_kv, d_gates[:, D:4 * D]], axis=1)


def _pack_small(rows):
    out = [jnp.pad(r.reshape(1, -1), ((0, 0), (0, D - r.size))) for r in rows]
    out.append(jnp.zeros((8 - len(rows), D), F32))
    return jnp.concatenate(out, axis=0)


def kernel(x, norm_w, w_in, hgrn_lower_bound, hgrn_norm_w, w_branch_hgrn, attn_sinks, w_branch_attn, w_out, final_norm_w, loss_target, m_norm_w, m_w_in, m_hgrn_lower_bound, m_hgrn_norm_w, m_w_branch_hgrn, m_attn_sinks, m_w_branch_attn, m_w_out, m_final_norm_w, v_norm_w, v_w_in, v_hgrn_lower_bound, v_hgrn_norm_w, v_w_branch_hgrn, v_attn_sinks, v_w_branch_attn, v_w_out, v_final_norm_w):
    t = x.shape[1]
    x2d = x.reshape(t, D)
    tgt = loss_target.reshape(t, D)
    fnw = final_norm_w.reshape(1, D)
    row_blk = min(256, t)
    big_blk = min(512, t)

    win_g, wbh_g, wba_g, wout_g = _gather_weights(w_in[0], w_branch_hgrn[0], w_branch_attn[0], w_out[0])
    w_p = _to_internal_cols(win_g.transpose(1, 0, 2).reshape(D, DIN))
    wbh, wba, wout = wbh_g.reshape(D, D), wba_g.reshape(D, D), wout_g.reshape(D, D)

    proj, xnt = _inproj_fwd(x2d, norm_w, w_p, min(1024, t), 2176)
    o_raw, states = _hgrn_fwd(proj, hgrn_lower_bound, big_blk, HGRN_GROUP)
    o_a = _attn_fwd(proj, attn_sinks)
    (dx2, do_raw, do_a, d_tail, ght, gat, mgt, dyh, dya, dx2b, sums) = _tail(
        o_raw, o_a, proj, x2d, tgt, wbh, wba, wout, hgrn_norm_w, fnw, row_blk)
    dwbh, dwba, dwout = _wgrad3(ght, dyh, gat, dya, mgt, dx2b, big_blk)
    d_aq, d_kv, dsink = _attn_bwd(proj, attn_sinks, do_a)
    d_hgrn, dlb = _hgrn_bwd(proj, hgrn_lower_bound, do_raw, states, big_blk, HGRN_GROUP)
    pieces = (d_hgrn, d_aq, d_tail, d_kv)
    dw_pieces = [_inproj_wgrad_piece(xnt, p, CB, "inproj_wgrad_" + n)
                 for p, n in zip(pieces, ("hgrn", "aq", "gates", "kv"))]
    grad_x, dnw = _inproj_dgrad(pieces, w_p, x2d, dx2, norm_w, big_blk)

    dwin_r = _to_reference_cols(*dw_pieces).reshape(D, NDEV, IN_SHARD).transpose(1, 0, 2).astype(BF16)
    small = _pack_small([dnw[0:1], dlb[0:1], jnp.zeros((1, D), F32), sums[1:2], sums[0:1], dsink[0:1, 0:QH]])
    slots = lambda a: a.reshape(NDEV, ROW_SHARD, D).astype(BF16)
    own, got = _exchange_pair([dwin_r, slots(dwbh), slots(dwba), slots(dwout)])
    pair_sums = [_pair_sum(o, g, ROW_SHARD, "pair_sum_" + n) for o, g, n in zip(own, got, ("w_in", "w_bh", "w_ba", "w_out"))]
    (rin, rbh, rba, rout), rsm = _exchange_chips(pair_sums, small)
    g_in, d_in, nm_in, nv_in = _adamw_shard(rin, w_in[0], m_w_in[0], v_w_in[0], 128, "adamw_w_in")
    g_bh, d_bh, nm_bh, nv_bh = _adamw_shard(rbh, w_branch_hgrn[0], m_w_branch_hgrn[0], v_w_branch_hgrn[0], 128, "adamw_w_bh")
    g_ba, d_ba, nm_ba, nv_ba = _adamw_shard(rba, w_branch_attn[0], m_w_branch_attn[0], v_w_branch_attn[0], 128, "adamw_w_ba")
    g_out, d_out, nm_out, nv_out = _adamw_shard(rout, w_out[0], m_w_out[0], v_w_out[0], 128, "adamw_w_out")
    pack = lambda a, b, c, d, e: _pack_small([a, b[0:1], b[1:2], c, d.reshape(1, D), e])
    small_out = _adamw_small(
        rsm,
        pack(norm_w, hgrn_lower_bound, hgrn_norm_w, final_norm_w, attn_sinks),
        pack(m_norm_w, m_hgrn_lower_bound, m_hgrn_norm_w, m_final_norm_w, m_attn_sinks),
        pack(v_norm_w, v_hgrn_lower_bound, v_hgrn_norm_w, v_final_norm_w, v_attn_sinks))

    loss = lax.psum(sums[2, 0], ("x", "y", "c"))

    def unpack(p):
        return dict(norm_w=p[0:1], lb=p[1:3], hnw=p[3:4], fnw=p[4], sinks=p[5:6, 0:QH])

    sg, sd, sm, sv = (unpack(p) for p in small_out)

    def group(s, w_in_v, bh, ba, out):
        return (s["norm_w"], w_in_v[None], s["lb"], s["hnw"], bh[None], s["sinks"], ba[None], out[None], s["fnw"])

    return (loss, grad_x.reshape(1, t, D),
            *group(sg, g_in, g_bh, g_ba, g_out), *group(sd, d_in, d_bh, d_ba, d_out),
            *group(sm, nm_in, nm_bh, nm_ba, nm_out), *group(sv, nv_in, nv_bh, nv_ba, nv_out))
```

```python
import functools

import jax
import jax.numpy as jnp
from jax import lax
from jax.experimental import pallas as pl
from jax.experimental.pallas import tpu as pltpu

F32 = jnp.float32
BF16 = jnp.bfloat16

D = 1024
DIN = 8704
NDEV = 8
IN_SHARD = DIN // NDEV
ROW_SHARD = D // NDEV
HEADS = 8
HD = 128
CH = 64
HGRN_GROUP = 8
QH = 16
AB = 128
EPS = 1e-6
NEG = -1e30
ATT_SCALE = 0.125

C_HGRN = 0
C_AQ = 3072
C_HG = 4096
C_AG = 5120
C_MH = 6144
C_MA = 7168
C_AK = 8192
C_AV = 8448
CB = 512

LR = 0.001
B1 = 0.9
B2 = 0.999
ADAM_EPS = 1e-08
WD = 0.01
STEP = 10

V7X_VMEM_BYTES = 64 * 1024 * 1024
MESH = pl.DeviceIdType.MESH


def _cp(vmem_mb):
    return pltpu.CompilerParams(vmem_limit_bytes=vmem_mb * 1024 * 1024)


def _mm(a, b):
    return jnp.dot(a, b, preferred_element_type=F32)


def _mm_nt(a, b):
    return lax.dot_general(a, b, (((1,), (1,)), ((), ())), preferred_element_type=F32)


def _mm_tn(a, b):
    return lax.dot_general(a, b, (((0,), (0,)), ((), ())), preferred_element_type=F32)


def _mm_exact(a, b):
    return jnp.dot(a, b, preferred_element_type=F32, precision=lax.Precision.HIGHEST)


def _sigmoid(v):
    return 1.0 / (1.0 + jnp.exp(-v))


def _bf(v):
    return v.astype(BF16)


def _place():
    x, y, c = lax.axis_index("x"), lax.axis_index("y"), lax.axis_index("c")
    return (x, y, c), (x, y, 1 - c), [(1 - x, y), (x, 1 - y), (1 - x, 1 - y)]


def _dev_index(px, py, pc):
    return 4 * px + 2 * py + pc


def _gather_weights(w_in_s, wbh_s, wba_s, wout_s):
    def body(win_ref, wbh_ref, wba_ref, wout_ref, win_g, wbh_g, wba_g, wout_g, sin, s3, send_sems, recv_sems, loc_sems):
        (x, y, c), sibling, chips = _place()
        me = _dev_index(x, y, c)
        sin[...] = win_ref[...].astype(BF16)
        s3[0] = wbh_ref[...].astype(BF16)
        s3[1] = wba_ref[...].astype(BF16)
        s3[2] = wout_ref[...].astype(BF16)
        srcs = [sin, s3.at[0], s3.at[1], s3.at[2]]
        outs = [win_g, wbh_g, wba_g, wout_g]

        def copy(kind, k, block, to, src=None):
            return pltpu.make_async_remote_copy(
                src_ref=srcs[k] if src is None else src, dst_ref=outs[k].at[block], send_sem=send_sems.at[kind, k],
                recv_sem=recv_sems.at[kind, k], device_id=to, device_id_type=MESH)

        local = [pltpu.make_async_copy(srcs[k], outs[k].at[me], loc_sems.at[k]) for k in range(4)]
        for cp in local:
            cp.start()
        first = [copy(0, k, me, sibling) for k in range(4)]
        first += [copy(1 + j, k, me, (*chip, c)) for j, chip in enumerate(chips) for k in range(4)]
        for cp in first:
            cp.start()
        passed = []
        for j, chip in enumerate(chips):
            block = _dev_index(*chip, c)
            for k in range(4):
                copy(1 + j, k, block, (x, y, c)).wait_recv()
            for k in range(4):
                cp = copy(4 + j, k, block, sibling, src=outs[k].at[block])
                cp.start()
                passed.append(cp)
        for k in range(4):
            copy(0, k, _dev_index(x, y, 1 - c), (x, y, c)).wait_recv()
        for j, chip in enumerate(chips):
            for k in range(4):
                copy(4 + j, k, _dev_index(*chip, 1 - c), (x, y, c)).wait_recv()
        for cp in first + passed:
            cp.wait_send()
        for cp in local:
            cp.wait()

    vm = pl.BlockSpec(memory_space=pltpu.VMEM)
    hbm = pl.BlockSpec(memory_space=pl.ANY)
    return pl.pallas_call(
        body,
        out_shape=[jax.ShapeDtypeStruct((NDEV, D, IN_SHARD), BF16)] + [jax.ShapeDtypeStruct((NDEV, ROW_SHARD, D), BF16)] * 3,
        in_specs=[vm, vm, vm, vm],
        out_specs=[hbm, hbm, hbm, hbm],
        scratch_shapes=[pltpu.VMEM((D, IN_SHARD), BF16), pltpu.VMEM((3, ROW_SHARD, D), BF16),
                        pltpu.SemaphoreType.DMA((NDEV - 1, 4)), pltpu.SemaphoreType.DMA((NDEV - 1, 4)),
                        pltpu.SemaphoreType.DMA((4,))],
        name="gather_weights", compiler_params=_cp(32),
    )(w_in_s, wbh_s, wba_s, wout_s)


def _exchange_pair(arrs):
    n = len(arrs)

    def body(*refs):
        ins, got = refs[:n], refs[n:2 * n]
        send_sems, recv_sems = refs[2 * n:]
        (x, y, c), sibling, _ = _place()
        sends = [pltpu.make_async_remote_copy(
            src_ref=ins[k].at[_dev_index(q // 2, q % 2, 1 - c)], dst_ref=got[k].at[q], send_sem=send_sems.at[q, k],
            recv_sem=recv_sems.at[q, k], device_id=sibling, device_id_type=MESH) for q in range(4) for k in range(n)]
        for cp in sends:
            cp.start()
        for cp in sends:
            cp.wait_recv()
        for cp in sends:
            cp.wait_send()

    hbm = pl.BlockSpec(memory_space=pl.ANY)
    return pl.pallas_call(
        body, out_shape=[jax.ShapeDtypeStruct((4,) + a.shape[1:], a.dtype) for a in arrs],
        in_specs=[hbm] * n, out_specs=[hbm] * n,
        scratch_shapes=[pltpu.SemaphoreType.DMA((4, n)), pltpu.SemaphoreType.DMA((4, n))],
        name="exchange_pair",
    )(*arrs)


def _pair_sum(full, got, core, rows, name):
    _, nr, nc = got.shape

    def body(core_ref, a_ref, b_ref, o_ref):
        o_ref[...] = (a_ref[...].astype(F32) + b_ref[...].astype(F32)).astype(BF16)

    blk = pl.BlockSpec((1, rows, nc), lambda q, i, core_ref: (q, i, 0))
    return pl.pallas_call(
        body,
        grid_spec=pltpu.PrefetchScalarGridSpec(
            num_scalar_prefetch=1, grid=(4, nr // rows),
            in_specs=[pl.BlockSpec((1, rows, nc), lambda q, i, core_ref: (2 * q + core_ref[0], i, 0)), blk],
            out_specs=blk),
        out_shape=jax.ShapeDtypeStruct(got.shape, BF16), name=name,
    )(core, full, got)


def _exchange_chips(sums, small):
    n = len(sums)

    def body(*refs):
        ins, sm = refs[:n], refs[n]
        outs, rsm = refs[n + 1:2 * n + 1], refs[2 * n + 1]
        send_sems, recv_sems, loc_sems, out_sems, ssend, srecv = refs[2 * n + 2:2 * n + 8]
        stage = refs[2 * n + 8:]
        (x, y, c), _, chips = _place()
        me = _dev_index(x, y, c)
        my_chip = 2 * x + y
        peers = [(1 - x if r & 4 else x, 1 - y if r & 2 else y, 1 - c if r & 1 else c) for r in range(1, NDEV)]
        local = [pltpu.make_async_copy(ins[k].at[my_chip], stage[k], loc_sems.at[k]) for k in range(n)]
        local.append(pltpu.make_async_copy(sm, rsm.at[me], loc_sems.at[n]))
        copies = []
        for j, (px, py) in enumerate(chips):
            for k in range(n):
                copies.append(pltpu.make_async_remote_copy(
                    src_ref=ins[k].at[2 * px + py], dst_ref=outs[k].at[my_chip], send_sem=send_sems.at[j, k],
                    recv_sem=recv_sems.at[j, k], device_id=(px, py, c), device_id_type=MESH))
        for r, peer in enumerate(peers):
            copies.append(pltpu.make_async_remote_copy(
                src_ref=sm, dst_ref=rsm.at[me], send_sem=ssend.at[r], recv_sem=srecv.at[r],
                device_id=peer, device_id_type=MESH))
        for cp in copies + local:
            cp.start()
        for cp in local:
            cp.wait()
        local = [pltpu.make_async_copy(stage[k], outs[k].at[my_chip], out_sems.at[k]) for k in range(n)]
        for cp in local:
            cp.start()
        for j, (px, py) in enumerate(chips):
            for k in range(n):
                pltpu.make_async_remote_copy(
                    src_ref=ins[k].at[my_chip], dst_ref=outs[k].at[2 * px + py], send_sem=send_sems.at[j, k],
                    recv_sem=recv_sems.at[j, k], device_id=(px, py, c), device_id_type=MESH).wait_recv()
        for r, peer in enumerate(peers):
            pltpu.make_async_remote_copy(
                src_ref=sm, dst_ref=rsm.at[_dev_index(*peer)], send_sem=ssend.at[r], recv_sem=srecv.at[r],
                device_id=peer, device_id_type=MESH).wait_recv()
        for cp in copies:
            cp.wait_send()
        for cp in local:
            cp.wait()

    hbm = pl.BlockSpec(memory_space=pl.ANY)
    res = pl.pallas_call(
        body,
        out_shape=[jax.ShapeDtypeStruct(a.shape, a.dtype) for a in sums] + [jax.ShapeDtypeStruct((NDEV,) + small.shape, F32)],
        in_specs=[hbm] * (n + 1), out_specs=[hbm] * (n + 1),
        scratch_shapes=[pltpu.SemaphoreType.DMA((3, n)), pltpu.SemaphoreType.DMA((3, n)), pltpu.SemaphoreType.DMA((n + 1,)),
                        pltpu.SemaphoreType.DMA((n,)), pltpu.SemaphoreType.DMA((NDEV - 1,)), pltpu.SemaphoreType.DMA((NDEV - 1,))]
        + [pltpu.VMEM(a.shape[1:], a.dtype) for a in sums],
        name="exchange_chips",
    )(*sums, small)
    return res[:n], res[n]


def _inproj_fwd(x2d, norm_w, w_p, tb, nb):
    t = x2d.shape[0]

    def body(x_ref, nw_ref, w_ref, proj_ref, xnt_ref, xn_s):
        @pl.when(pl.program_id(1) == 0)
        def _():
            xv = x_ref[...]
            r = lax.rsqrt(jnp.mean(xv * xv, axis=-1, keepdims=True) + EPS)
            xn = (xv * r) * nw_ref[...]
            xn_s[...] = xn.astype(BF16)
            xnt_ref[...] = xn.T.astype(BF16)

        proj_ref[...] = _mm(xn_s[...], w_ref[...])

    return pl.pallas_call(
        body, grid=(t // tb, DIN // nb),
        in_specs=[pl.BlockSpec((tb, D), lambda i, j: (i, 0)), pl.BlockSpec((1, D), lambda i, j: (0, 0)),
                  pl.BlockSpec((D, nb), lambda i, j: (0, j))],
        out_specs=[pl.BlockSpec((tb, nb), lambda i, j: (i, j)), pl.BlockSpec((D, tb), lambda i, j: (0, i))],
        out_shape=[jax.ShapeDtypeStruct((t, DIN), F32), jax.ShapeDtypeStruct((D, t), BF16)],
        scratch_shapes=[pltpu.VMEM((tb, D), BF16)],
        name="inproj_fwd", compiler_params=_cp(56),
    )(x2d, norm_w, w_p)


def _lower_bound(lb_ref):
    a0 = lb_ref[0:1, :]
    a1 = lb_ref[1:2, :]
    mx = jnp.maximum(a0, a1)
    e0 = jnp.exp(a0 - mx)
    e1 = jnp.exp(a1 - mx)
    return e0 / (e0 + e1)


def _hgrn_chunk_fwd(hq, hf, lb, tril):
    sg = _sigmoid(hf)
    f = lb + (1.0 - lb) * sg
    g = jnp.log(f)
    k = 1.0 - f
    sq = _sigmoid(hq)
    q = hq * sq
    b = _mm_exact(tril, g)
    last_row = lax.broadcasted_iota(jnp.int32, b.shape, 0) == CH - 1
    b_last = jnp.sum(jnp.where(last_row, b, 0.0), axis=0, keepdims=True)
    c = 0.5 * b_last
    eb = jnp.exp(b)
    ea = jnp.exp(b - c)
    ek = jnp.exp(c - b)
    ed = jnp.exp(b_last - b)
    ebl = jnp.exp(b_last)
    return dict(sg=sg, f=f, k=k, sq=sq, q=q, eb=eb, ea=ea, ek=ek, ed=ed, ebl=ebl,
                qe=q * eb, qa=q * ea, ka=k * ek, kd=k * ed)


def _tri(lower):
    r = lax.broadcasted_iota(jnp.int32, (CH, CH), 0)
    c = lax.broadcasted_iota(jnp.int32, (CH, CH), 1)
    return (r >= c) if lower else (c >= r)


def _head_segment(p_ref, rows, j, hg):
    return jnp.concatenate([p_ref[rows, 3 * HD * k + j * HD:3 * HD * k + (j + 1) * HD] for k in range(hg)], axis=1)


def _head(a, k):
    return a[:, k * HD:(k + 1) * HD]


def _hgrn_fwd(proj, lbw, rb, hg):
    t = proj.shape[0]
    ncb = rb // CH

    def body(p_ref, lb_ref, o_ref, st_ref, s_scr):
        @pl.when(pl.program_id(1) == 0)
        def _():
            s_scr[...] = jnp.zeros_like(s_scr)

        lb = _lower_bound(lb_ref)
        causal = _tri(True)
        tril = causal.astype(F32)
        heads = range(hg)

        def chunk(cc, carry):
            r0 = pl.multiple_of(cc * CH, CH)
            rows = pl.ds(r0, CH)
            e = _hgrn_chunk_fwd(_head_segment(p_ref, rows, 0, hg), _head_segment(p_ref, rows, 1, hg), lb, tril)
            v = _bf(_head_segment(p_ref, rows, 2, hg))
            sts = [s_scr[k] for k in heads]
            qa, ka, qe, kd = _bf(e["qa"]), _bf(e["ka"]), _bf(e["qe"]), _bf(e["kd"])
            a = [_bf(jnp.where(causal, _mm_nt(_head(qa, k), _head(ka, k)), 0.0)) for k in heads]
            o_inter = [_mm_nt(_head(qe, k), _bf(sts[k])) for k in heads]
            kv = [_mm_tn(_head(v, k), _head(kd, k)) for k in heads]
            o_intra = [_mm(a[k], _head(v, k)) for k in heads]
            for k in heads:
                st_ref[cc, k] = sts[k]
                o_ref[rows, k * HD:(k + 1) * HD] = o_inter[k] + o_intra[k]
                s_scr[k] = sts[k] * _head(e["ebl"], k) + kv[k]
            return carry

        lax.fori_loop(0, ncb, chunk, 0)

    return pl.pallas_call(
        body, grid=(HEADS // hg, t // rb),
        in_specs=[pl.BlockSpec((rb, 3 * HD * hg), lambda h, i: (i, h)), pl.BlockSpec((2, HD * hg), lambda h, i: (0, h))],
        out_specs=[pl.BlockSpec((rb, HD * hg), lambda h, i: (i, h)),
                   pl.BlockSpec((ncb, hg, HD, HD), lambda h, i: (i, h, 0, 0))],
        out_shape=[jax.ShapeDtypeStruct((t, D), F32), jax.ShapeDtypeStruct((t // CH, HEADS, HD, HD), F32)],
        scratch_shapes=[pltpu.VMEM((hg, HD, HD), F32)],
        name="hgrn_fwd", compiler_params=_cp(48),
    )(proj, lbw)


def _hgrn_bwd(proj, lbw, do_raw, states, rb, hg):
    t = proj.shape[0]
    nblk = t // rb
    ncb = rb // CH

    def body(p_ref, lb_ref, do_ref, st_ref, dp_ref, dlb_ref, ds_scr):
        @pl.when(pl.program_id(1) == 0)
        def _():
            ds_scr[...] = jnp.zeros_like(ds_scr)
            dlb_ref[...] = jnp.zeros_like(dlb_ref)

        lb = _lower_bound(lb_ref)
        causal = _tri(True)
        tril = causal.astype(F32)
        triu = _tri(False).astype(F32)
        last_row = lax.broadcasted_iota(jnp.int32, (CH, HD * hg), 0) == CH - 1
        row0 = lax.broadcasted_iota(jnp.int32, (8, HD * hg), 0) == 0
        heads = range(hg)
        wide = lambda parts: jnp.concatenate(parts, axis=1)

        def chunk(it, carry):
            cc = ncb - 1 - it
            r0 = pl.multiple_of(cc * CH, CH)
            rows = pl.ds(r0, CH)
            hq = _head_segment(p_ref, rows, 0, hg)
            e = _hgrn_chunk_fwd(hq, _head_segment(p_ref, rows, 1, hg), lb, tril)
            v = _bf(_head_segment(p_ref, rows, 2, hg))
            do = _bf(do_ref[rows, :])
            sts = [st_ref[cc, k] for k in heads]
            dsts = [ds_scr[k] for k in heads]
            dlb_acc = dlb_ref[...]
            qa, ka, qe, kd = _bf(e["qa"]), _bf(e["ka"]), _bf(e["qe"]), _bf(e["kd"])
            a = [_bf(jnp.where(causal, _mm_nt(_head(qa, k), _head(ka, k)), 0.0)) for k in heads]
            da = [_bf(jnp.where(causal, _mm_nt(_head(do, k), _head(v, k)), 0.0)) for k in heads]
            dqe = wide([_mm(_head(do, k), _bf(sts[k])) for k in heads])
            dkd = wide([_mm(_head(v, k), _bf(dsts[k])) for k in heads])
            dv_state = [_mm_nt(_head(kd, k), _bf(dsts[k])) for k in heads]
            ds_new = [_mm_tn(_head(do, k), _head(qe, k)) for k in heads]
            dv_intra = [_mm_tn(a[k], _head(do, k)) for k in heads]
            dqa = wide([_mm(da[k], _head(ka, k)) for k in heads])
            dka = wide([_mm_tn(da[k], _head(qa, k)) for k in heads])
            dv = wide([dv_intra[k] + dv_state[k] for k in heads])
            dbl = e["ebl"] * wide([jnp.sum(sts[k] * dsts[k], axis=0, keepdims=True) for k in heads])
            dq = dqe * e["eb"] + dqa * e["ea"]
            dk = dka * e["ek"] + dkd * e["ed"]
            dkd_kd = dkd * kd.astype(F32)
            db = dqe * qe.astype(F32) + dqa * qa.astype(F32) - dka * ka.astype(F32) - dkd_kd
            db = db + jnp.where(last_row, dbl + jnp.sum(dkd_kd, axis=0, keepdims=True), 0.0)
            dg = _mm_exact(triu, db)
            df = dg / e["f"] - dk
            sg = e["sg"]
            sq = e["sq"]
            dhq = _bf(dq * (sq * (1.0 + hq * (1.0 - sq))))
            dhf = _bf(df * (1.0 - lb) * sg * (1.0 - sg))
            dhi = _bf(dv)
            dlb_new = dlb_acc + jnp.where(row0, jnp.sum(df * (1.0 - sg), axis=0, keepdims=True), 0.0)
            for k in heads:
                c0 = 3 * HD * k
                ds_scr[k] = ds_new[k] + dsts[k] * _head(e["ebl"], k)
                dp_ref[rows, c0:c0 + HD] = _head(dhq, k)
                dp_ref[rows, c0 + HD:c0 + 2 * HD] = _head(dhf, k)
                dp_ref[rows, c0 + 2 * HD:c0 + 3 * HD] = _head(dhi, k)
            dlb_ref[...] = dlb_new
            return carry

        lax.fori_loop(0, ncb, chunk, 0)

    rev = lambda h, i: (nblk - 1 - i, h)
    return pl.pallas_call(
        body, grid=(HEADS // hg, nblk),
        in_specs=[pl.BlockSpec((rb, 3 * HD * hg), rev), pl.BlockSpec((2, HD * hg), lambda h, i: (0, h)),
                  pl.BlockSpec((rb, HD * hg), rev), pl.BlockSpec((ncb, hg, HD, HD), lambda h, i: (nblk - 1 - i, h, 0, 0))],
        out_specs=[pl.BlockSpec((rb, 3 * HD * hg), rev), pl.BlockSpec((8, HD * hg), lambda h, i: (0, h))],
        out_shape=[jax.ShapeDtypeStruct((t, 3 * D), BF16), jax.ShapeDtypeStruct((8, D), F32)],
        scratch_shapes=[pltpu.VMEM((hg, HD, HD), F32)],
        name="hgrn_bwd", compiler_params=_cp(48),
    )(proj, lbw, do_raw, states)


def _kv_variants(tile, odd):
    low = lax.broadcasted_iota(jnp.int32, tile.shape, 1) < 64
    if odd:
        hi = jnp.where(low, 0.0, tile)
        lo = pltpu.roll(hi, 64, 1)
    else:
        lo = jnp.where(low, tile, 0.0)
        hi = pltpu.roll(lo, 64, 1)
    return _bf(lo), _bf(hi)


def _attn_masks(n):
    qi = lax.broadcasted_iota(jnp.int32, (AB, AB), 0)
    kj = lax.broadcasted_iota(jnp.int32, (AB, AB), 1)
    cur = kj <= qi
    return cur, cur | (n > 0), qi <= kj


def _kv_all(prev_ref, cur_ref):
    out = []
    for tl in range(2):
        cols = slice(tl * 128, (tl + 1) * 128)
        tile = jnp.concatenate([prev_ref[:, cols], cur_ref[:, cols]], axis=0)
        out.append(_kv_variants(tile, 0))
        out.append(_kv_variants(tile, 1))
    return out


def _attn_softmax(s2, sink, cur, ok):
    s = jnp.where(ok, jnp.where(cur, s2[:, AB:], s2[:, :AB]) * ATT_SCALE, NEG)
    m = jnp.maximum(jnp.max(s, axis=-1, keepdims=True), sink)
    p = jnp.exp(s - m)
    es = jnp.exp(sink - m)
    inv = 1.0 / (jnp.sum(p, axis=-1, keepdims=True) + es)
    return p * inv, es * inv


def _spread(pc, cur):
    return jnp.concatenate([jnp.where(cur, 0.0, pc), jnp.where(cur, pc, 0.0)], axis=1)


def _spread_t(pct, cur_t):
    return jnp.concatenate([jnp.where(cur_t, 0.0, pct), jnp.where(cur_t, pct, 0.0)], axis=0)


def _attn_fwd(proj, sinks):
    t = proj.shape[0]
    nb = t // AB

    def body(q_ref, kc_ref, kp_ref, vc_ref, vp_ref, sink_ref, o_ref):
        cur, ok, _ = _attn_masks(pl.program_id(0))
        kvars = _kv_all(kp_ref, kc_ref)
        vvars = _kv_all(vp_ref, vc_ref)
        qps = [_bf(q_ref[:, 128 * j:128 * (j + 1)]) for j in range(8)]
        scores = [[_mm_nt(qps[j], kvars[j // 2][ab]) for ab in range(2)] for j in range(8)]
        for j in range(8):
            parts = []
            for ab in range(2):
                pc, _ = _attn_softmax(scores[j][ab], sink_ref[0, 2 * j + ab], cur, ok)
                parts.append(_mm(_bf(_spread(pc, cur)), vvars[j // 2][ab]))
            o_ref[:, 128 * j:128 * (j + 1)] = parts[0] + parts[1]

    prev = lambda n: jnp.maximum(n - 1, 0)
    return pl.pallas_call(
        body, grid=(nb,),
        in_specs=[pl.BlockSpec((AB, D), lambda n: (n, C_AQ // D)),
                  pl.BlockSpec((AB, 256), lambda n: (n, C_AK // 256)),
                  pl.BlockSpec((AB, 256), lambda n: (prev(n), C_AK // 256)),
                  pl.BlockSpec((AB, 256), lambda n: (n, C_AV // 256)),
                  pl.BlockSpec((AB, 256), lambda n: (prev(n), C_AV // 256)),
                  pl.BlockSpec(memory_space=pltpu.SMEM)],
        out_specs=pl.BlockSpec((AB, D), lambda n: (n, 0)),
        out_shape=jax.ShapeDtypeStruct((t, D), F32),
        name="attn_fwd", compiler_params=_cp(32),
    )(proj, proj, proj, proj, proj, sinks)


def _attn_bwd(proj, sinks, do_a):
    t = proj.shape[0]
    nb = t // AB

    def body(q_ref, kc_ref, kp_ref, vc_ref, vp_ref, do_ref, sink_ref, dq_ref, dkv_ref, dsink_ref, carry):
        n = pl.program_id(0)

        @pl.when(n == 0)
        def _():
            dsink_ref[...] = jnp.zeros_like(dsink_ref)
            carry[...] = jnp.zeros_like(carry)

        @pl.when(n < nb)
        def _():
            cur, ok, cur_t = _attn_masks(n)
            low = lax.broadcasted_iota(jnp.int32, (2 * AB, 128), 1) < 64
            lane = lax.broadcasted_iota(jnp.int32, (8, 128), 1)
            row0 = lax.broadcasted_iota(jnp.int32, (8, 128), 0) == 0
            kvars = _kv_all(kp_ref, kc_ref)
            vvars = _kv_all(vp_ref, vc_ref)
            qps = [_bf(q_ref[:, 128 * j:128 * (j + 1)]) for j in range(8)]
            dops = [_bf(do_ref[:, 128 * j:128 * (j + 1)]) for j in range(8)]
            scores = [[_mm_nt(qps[j], kvars[j // 2][ab]) for ab in range(2)] for j in range(8)]
            dps = [[_mm_nt(dops[j], vvars[j // 2][ab]) for ab in range(2)] for j in range(8)]
            dsink = jnp.zeros((8, 128), F32)
            dk_ab = [[None, None] for _ in range(4)]
            dv_ab = [[None, None] for _ in range(4)]
            for j in range(8):
                g = j // 2
                dqp = None
                for ab in range(2):
                    head = 2 * j + ab
                    pc, ps = _attn_softmax(scores[j][ab], sink_ref[0, head], cur, ok)
                    dpc = jnp.where(cur, dps[j][ab][:, AB:], dps[j][ab][:, :AB])
                    rs = jnp.sum(pc * dpc, axis=-1, keepdims=True)
                    dsc = pc * (dpc - rs)
                    dsink = dsink + jnp.where(row0 & (lane == head), -jnp.sum(ps * rs), 0.0)
                    term = _mm(_bf(_spread(dsc, cur)), kvars[g][ab])
                    dqp = term if dqp is None else dqp + term
                    dk_t = _mm(_bf(_spread_t(dsc.T, cur_t)), qps[j])
                    dv_t = _mm(_bf(_spread_t(pc.T, cur_t)), dops[j])
                    dk_ab[g][ab] = dk_t if dk_ab[g][ab] is None else dk_ab[g][ab] + dk_t
                    dv_ab[g][ab] = dv_t if dv_ab[g][ab] is None else dv_ab[g][ab] + dv_t
                dq_ref[:, 128 * j:128 * (j + 1)] = _bf(dqp * ATT_SCALE)
            for tl in range(2):
                ke, ko = dk_ab[2 * tl], dk_ab[2 * tl + 1]
                ve, vo = dv_ab[2 * tl], dv_ab[2 * tl + 1]
                dkt = (jnp.where(low, ke[0], 0.0) + pltpu.roll(jnp.where(low, 0.0, ke[1]), 64, 1)
                       + jnp.where(low, 0.0, ko[1]) + pltpu.roll(jnp.where(low, ko[0], 0.0), 64, 1)) * ATT_SCALE
                dvt = (jnp.where(low, ve[0], 0.0) + pltpu.roll(jnp.where(low, 0.0, ve[1]), 64, 1)
                       + jnp.where(low, 0.0, vo[1]) + pltpu.roll(jnp.where(low, vo[0], 0.0), 64, 1))
                kcols = slice(tl * 128, (tl + 1) * 128)
                vcols = slice(256 + tl * 128, 256 + (tl + 1) * 128)
                dkv_ref[:, kcols] = _bf(carry[:, kcols] + dkt[0:AB])
                dkv_ref[:, vcols] = _bf(carry[:, vcols] + dvt[0:AB])
                carry[:, kcols] = dkt[AB:2 * AB]
                carry[:, vcols] = dvt[AB:2 * AB]
            dsink_ref[...] += dsink

        @pl.when(n == nb)
        def _():
            dkv_ref[...] = _bf(carry[...])

    cur = lambda n: jnp.minimum(n, nb - 1)
    prev = lambda n: jnp.clip(n - 1, 0, nb - 1)
    return pl.pallas_call(
        body, grid=(nb + 1,),
        in_specs=[pl.BlockSpec((AB, D), lambda n: (cur(n), C_AQ // D)),
                  pl.BlockSpec((AB, 256), lambda n: (cur(n), C_AK // 256)),
                  pl.BlockSpec((AB, 256), lambda n: (prev(n), C_AK // 256)),
                  pl.BlockSpec((AB, 256), lambda n: (cur(n), C_AV // 256)),
                  pl.BlockSpec((AB, 256), lambda n: (prev(n), C_AV // 256)),
                  pl.BlockSpec((AB, D), lambda n: (cur(n), 0)),
                  pl.BlockSpec(memory_space=pltpu.SMEM)],
        out_specs=[pl.BlockSpec((AB, D), lambda n: (cur(n), 0)),
                   pl.BlockSpec((AB, 512), lambda n: (prev(n), 0)),
                   pl.BlockSpec((8, 128), lambda n: (0, 0))],
        out_shape=[jax.ShapeDtypeStruct((t, D), BF16), jax.ShapeDtypeStruct((t, 512), BF16),
                   jax.ShapeDtypeStruct((8, 128), F32)],
        scratch_shapes=[pltpu.VMEM((AB, 512), F32)],
        name="attn_bwd", compiler_params=_cp(32),
    )(proj, proj, proj, proj, proj, do_a, sinks)


def _silu_and_grad(v):
    s = _sigmoid(v)
    return v * s, s * (1.0 + v * (1.0 - s))


def _tail(o_raw, o_a, proj, x2d, tgt, wbh, wba, wout, hnw, fnw, tb):
    t = x2d.shape[0]

    def body(or_ref, oa_ref, hg_ref, ag_ref, mh_ref, ma_ref, x_ref, t_ref, wbh_ref, wba_ref, wout_ref, hnw_ref, fnw_ref,
             dx2_ref, dor_ref, doa_ref, dt_ref, ght_ref, gat_ref, mgt_ref, dyh_ref, dya_ref, dx2b_ref, sums_ref):
        @pl.when(pl.program_id(0) == 0)
        def _():
            sums_ref[...] = jnp.zeros_like(sums_ref)

        hnw_v = hnw_ref[...]
        fnw_v = fnw_ref[...]
        o = or_ref[...]
        rs, xhs = [], []
        for h in range(HEADS):
            oh = o[:, h * HD:(h + 1) * HD]
            r = lax.rsqrt(jnp.mean(oh * oh, axis=-1, keepdims=True) + EPS)
            rs.append(r)
            xhs.append(oh * r)
        xh = jnp.concatenate(xhs, axis=1)
        on = xh * hnw_v
        sil_hg, dsil_hg = _silu_and_grad(hg_ref[...])
        gh = on * sil_hg
        gh_b = _bf(gh)
        y_h = _mm(gh_b, wbh_ref[...])
        oa = oa_ref[...]
        sil_ag, dsil_ag = _silu_and_grad(ag_ref[...])
        ga_b = _bf(oa * sil_ag)
        y_a = _mm(ga_b, wba_ref[...])
        s_mh = _sigmoid(mh_ref[...])
        s_ma = _sigmoid(ma_ref[...])
        mg_b = _bf(s_mh * y_h + s_ma * y_a)
        x2 = x_ref[...] + _mm(mg_b, wout_ref[...])
        r2 = lax.rsqrt(jnp.mean(x2 * x2, axis=-1, keepdims=True) + EPS)
        xh2 = x2 * r2
        err = xh2 * fnw_v - t_ref[...]
        loss = 0.5 * jnp.sum(jnp.mean(err * err, axis=-1, keepdims=True))
        dy = err * (1.0 / D)
        dfnw = jnp.sum(dy * xh2, axis=0, keepdims=True)
        dxh2 = dy * fnw_v
        dx2 = r2 * (dxh2 - xh2 * jnp.mean(dxh2 * xh2, axis=-1, keepdims=True))
        dx2_ref[...] = dx2
        dx2_b = _bf(dx2)
        dmg = _mm_nt(dx2_b, wout_ref[...])
        dyh_b = _bf(dmg * s_mh)
        dya_b = _bf(dmg * s_ma)
        dt_ref[:, 2 * D:3 * D] = _bf(dmg * y_h * s_mh * (1.0 - s_mh))
        dt_ref[:, 3 * D:4 * D] = _bf(dmg * y_a * s_ma * (1.0 - s_ma))
        dgh = _mm_nt(dyh_b, wbh_ref[...])
        dga = _mm_nt(dya_b, wba_ref[...])
        doa_ref[...] = dga * sil_ag
        dt_ref[:, D:2 * D] = _bf(dga * oa * dsil_ag)
        dt_ref[:, 0:D] = _bf(dgh * on * dsil_hg)
        don = dgh * sil_hg
        dhnw = jnp.sum(don * xh, axis=0, keepdims=True)
        dxh = don * hnw_v
        dos = []
        for h in range(HEADS):
            sl = slice(h * HD, (h + 1) * HD)
            dos.append(rs[h] * (dxh[:, sl] - xhs[h] * jnp.mean(dxh[:, sl] * xhs[h], axis=-1, keepdims=True)))
        dor_ref[...] = jnp.concatenate(dos, axis=1)
        ght_ref[...] = _bf(gh.T)
        gat_ref[...] = _bf((oa * sil_ag).T)
        mgt_ref[...] = _bf((s_mh * y_h + s_ma * y_a).T)
        dyh_ref[...] = dyh_b
        dya_ref[...] = dya_b
        dx2b_ref[...] = dx2_b
        row = lax.broadcasted_iota(jnp.int32, (8, D), 0)
        sums_ref[...] += jnp.where(row == 0, dfnw, 0.0) + jnp.where(row == 1, dhnw, 0.0) + jnp.where(row == 2, loss, 0.0)

    rowblk = lambda c: pl.BlockSpec((tb, D), lambda i: (i, c))
    full = lambda shape: pl.BlockSpec(shape, lambda i: (0, 0))
    tblk = pl.BlockSpec((D, tb), lambda i: (0, i))
    return pl.pallas_call(
        body, grid=(t // tb,),
        in_specs=[rowblk(0), rowblk(0), rowblk(C_HG // D), rowblk(C_AG // D), rowblk(C_MH // D), rowblk(C_MA // D),
                  rowblk(0), rowblk(0), full((D, D)), full((D, D)), full((D, D)), full((1, D)), full((1, D))],
        out_specs=[rowblk(0), rowblk(0), rowblk(0), pl.BlockSpec((tb, 4 * D), lambda i: (i, 0)),
                   tblk, tblk, tblk, rowblk(0), rowblk(0), rowblk(0), full((8, D))],
        out_shape=[jax.ShapeDtypeStruct((t, D), F32)] * 3 + [jax.ShapeDtypeStruct((t, 4 * D), BF16)]
        + [jax.ShapeDtypeStruct((D, t), BF16)] * 3 + [jax.ShapeDtypeStruct((t, D), BF16)] * 3
        + [jax.ShapeDtypeStruct((8, D), F32)],
        name="tail", compiler_params=_cp(56),
    )(o_raw, o_a, proj, proj, proj, proj, x2d, tgt, wbh, wba, wout, hnw, fnw)


def _wgrad3(ght, dyh, gat, dya, mgt, dx2b, tk):
    t = dyh.shape[0]

    def body(a0, b0, a1, b1, a2, b2, o0, o1, o2):
        @pl.when(pl.program_id(0) == 0)
        def _():
            o0[...] = jnp.zeros_like(o0)
            o1[...] = jnp.zeros_like(o1)
            o2[...] = jnp.zeros_like(o2)

        o0[...] += _mm(a0[...], b0[...])
        o1[...] += _mm(a1[...], b1[...])
        o2[...] += _mm(a2[...], b2[...])

    lhs = pl.BlockSpec((D, tk), lambda k: (0, k))
    rhs = pl.BlockSpec((tk, D), lambda k: (k, 0))
    out = pl.BlockSpec((D, D), lambda k: (0, 0))
    return pl.pallas_call(
        body, grid=(t // tk,), in_specs=[lhs, rhs] * 3, out_specs=[out] * 3,
        out_shape=[jax.ShapeDtypeStruct((D, D), F32)] * 3,
        name="wgrad3", compiler_params=_cp(48),
    )(ght, dyh, gat, dya, mgt, dx2b)


def _inproj_wgrad_piece(xnt, piece, nb, name):
    t = xnt.shape[1]
    width = piece.shape[1]

    def body(xnt_ref, p_ref, o_ref):
        o_ref[...] = _mm(xnt_ref[...], p_ref[...])

    return pl.pallas_call(
        body, grid=(width // nb,),
        in_specs=[pl.BlockSpec((D, t), lambda j: (0, 0), pipeline_mode=pl.Buffered(1)),
                  pl.BlockSpec((t, nb), lambda j: (0, j))],
        out_specs=pl.BlockSpec((D, nb), lambda j: (0, j)),
        out_shape=jax.ShapeDtypeStruct((D, width), F32),
        name=name, compiler_params=_cp(56),
    )(xnt, piece)


def _inproj_dgrad(pieces, w_p, x2d, dx2, norm_w, tb):
    t = x2d.shape[0]

    def body(p0, p1, p2, p3, w_ref, x_ref, dx2_ref, nw_ref, gx_ref, dnw_ref):
        @pl.when(pl.program_id(0) == 0)
        def _():
            dnw_ref[...] = jnp.zeros_like(dnw_ref)

        dxn = None
        off = 0
        for p in (p0, p1, p2, p3):
            width = p.shape[1]
            term = _mm_nt(p[...], w_ref[:, off:off + width])
            dxn = term if dxn is None else dxn + term
            off += width
        xv = x_ref[...]
        r = lax.rsqrt(jnp.mean(xv * xv, axis=-1, keepdims=True) + EPS)
        xh = xv * r
        dxh = dxn * nw_ref[...]
        gx_ref[...] = dx2_ref[...] + r * (dxh - xh * jnp.mean(dxh * xh, axis=-1, keepdims=True))
        row0 = lax.broadcasted_iota(jnp.int32, (8, D), 0) == 0
        dnw_ref[...] += jnp.where(row0, jnp.sum(dxn * xh, axis=0, keepdims=True), 0.0)

    rowblk = pl.BlockSpec((tb, D), lambda i: (i, 0))
    return pl.pallas_call(
        body, grid=(t // tb,),
        in_specs=[pl.BlockSpec((tb, p.shape[1]), lambda i: (i, 0)) for p in pieces]
        + [pl.BlockSpec((D, DIN), lambda i: (0, 0), pipeline_mode=pl.Buffered(1)), rowblk, rowblk,
           pl.BlockSpec((1, D), lambda i: (0, 0))],
        out_specs=[rowblk, pl.BlockSpec((8, D), lambda i: (0, 0))],
        out_shape=[jax.ShapeDtypeStruct((t, D), F32), jax.ShapeDtypeStruct((8, D), F32)],
        name="inproj_dgrad", compiler_params=_cp(60),
    )(*pieces, w_p, x2d, dx2, norm_w)


def _adamw_math(w, g, m, v):
    m = B1 * m + (1.0 - B1) * g
    v = B2 * v + (1.0 - B2) * (g * g)
    m_hat = m / (1.0 - B1 ** STEP)
    v_hat = v / (1.0 - B2 ** STEP)
    delta = -LR * (m_hat / (jnp.sqrt(v_hat) + ADAM_EPS) + WD * w)
    return delta, m, v


def _adamw_shard(parts, w, m, v, rows, name):
    nparts, nr, nc = parts.shape

    def body(p_ref, w_ref, m_ref, v_ref, g_ref, d_ref, nm_ref, nv_ref):
        g = p_ref[0].astype(F32)
        for s in range(1, nparts):
            g = g + p_ref[s].astype(F32)
        d, nm, nv = _adamw_math(w_ref[...], g, m_ref[...], v_ref[...])
        g_ref[...] = g
        d_ref[...] = d
        nm_ref[...] = nm
        nv_ref[...] = nv

    blk = pl.BlockSpec((rows, nc), lambda i: (i, 0))
    return pl.pallas_call(
        body, grid=(nr // rows,),
        in_specs=[pl.BlockSpec((nparts, rows, nc), lambda i: (0, i, 0)), blk, blk, blk],
        out_specs=[blk] * 4, out_shape=[jax.ShapeDtypeStruct((nr, nc), F32)] * 4,
        name=name, compiler_params=_cp(48),
    )(parts, w, m, v)


SMALL_ROWS = dict(norm_w=0, lower_bound=1, hgrn_norm_w=3, final_norm_w=4, sinks=5)


def _pack_small_grads(dnw, dlb, sums, dsink):
    def body(dnw_ref, dlb_ref, sums_ref, dsink_ref, o_ref):
        o_ref[...] = jnp.zeros_like(o_ref)
        o_ref[0:1, :] = dnw_ref[0:1, :]
        o_ref[1:2, :] = dlb_ref[0:1, :]
        o_ref[3:4, :] = sums_ref[1:2, :]
        o_ref[4:5, :] = sums_ref[0:1, :]
        o_ref[5:6, 0:128] = dsink_ref[0:1, :]

    return pl.pallas_call(body, out_shape=jax.ShapeDtypeStruct((8, D), F32), name="pack_small_grads")(dnw, dlb, sums, dsink)


def _adamw_small(parts, ws, ms, vs):
    shapes = [a.shape for a in ws]

    def body(p_ref, *refs):
        w, m, v = refs[0:5], refs[5:10], refs[10:15]
        outs = [refs[15 + 5 * i:20 + 5 * i] for i in range(4)]

        def total(row, width):
            g = p_ref[0, row:row + 1, 0:width]
            for s in range(1, NDEV):
                g = g + p_ref[s, row:row + 1, 0:width]
            return g

        lb = _lower_bound(w[1])
        ga0 = total(1, D) * lb * (1.0 - lb)
        grads = [total(0, D), None, total(3, D), total(4, D), total(5, QH)]
        for i in (0, 2, 3, 4):
            res = (grads[i],) + _adamw_math(w[i][...], grads[i], m[i][...], v[i][...])
            for o, val in zip(outs, res):
                o[i][...] = val
        for r, g in ((0, ga0), (1, -ga0)):
            res = (g,) + _adamw_math(w[1][r:r + 1, :], g, m[1][r:r + 1, :], v[1][r:r + 1, :])
            for o, val in zip(outs, res):
                o[1][r:r + 1, :] = val

    res = pl.pallas_call(
        body, out_shape=[jax.ShapeDtypeStruct(s, F32) for s in shapes] * 4, name="adamw_small",
    )(parts, *ws, *ms, *vs)
    return [res[5 * i:5 * i + 5] for i in range(4)]


def _to_internal_cols(w):
    hg = w[:, 0:3072].reshape(D, 3, HEADS, HD).transpose(0, 2, 1, 3).reshape(D, 3072)
    return jnp.concatenate([hg, w[:, 4096:5120], w[:, 3072:4096], w[:, 5632:8704], w[:, 5120:5632]], axis=1)


def _to_reference_cols(d_hgrn, d_aq, d_gates, d_kv):
    hg = d_hgrn.reshape(D, HEADS, 3, HD).transpose(0, 2, 1, 3).reshape(D, 3072)
    return jnp.concatenate([hg, d_gates[:, 0:D], d_aq, d_kv, d_gates[:, D:4 * D]], axis=1)


def kernel(x, norm_w, w_in, hgrn_lower_bound, hgrn_norm_w, w_branch_hgrn, attn_sinks, w_branch_attn, w_out, final_norm_w, loss_target, m_norm_w, m_w_in, m_hgrn_lower_bound, m_hgrn_norm_w, m_w_branch_hgrn, m_attn_sinks, m_w_branch_attn, m_w_out, m_final_norm_w, v_norm_w, v_w_in, v_hgrn_lower_bound, v_hgrn_norm_w, v_w_branch_hgrn, v_attn_sinks, v_w_branch_attn, v_w_out, v_final_norm_w):
    t = x.shape[1]
    x2d = x.reshape(t, D)
    tgt = loss_target.reshape(t, D)
    fnw = final_norm_w.reshape(1, D)
    row_blk = min(256, t)
    big_blk = min(512, t)

    win_g, wbh_g, wba_g, wout_g = _gather_weights(w_in[0], w_branch_hgrn[0], w_branch_attn[0], w_out[0])
    w_p = _to_internal_cols(win_g.transpose(1, 0, 2).reshape(D, DIN))
    wbh, wba, wout = wbh_g.reshape(D, D), wba_g.reshape(D, D), wout_g.reshape(D, D)

    proj, xnt = _inproj_fwd(x2d, norm_w, w_p, min(1024, t), 2176)
    o_raw, states = _hgrn_fwd(proj, hgrn_lower_bound, big_blk, HGRN_GROUP)
    o_a = _attn_fwd(proj, attn_sinks)
    (dx2, do_raw, do_a, d_tail, ght, gat, mgt, dyh, dya, dx2b, sums) = _tail(
        o_raw, o_a, proj, x2d, tgt, wbh, wba, wout, hgrn_norm_w, fnw, row_blk)
    dwbh, dwba, dwout = _wgrad3(ght, dyh, gat, dya, mgt, dx2b, big_blk)
    d_aq, d_kv, dsink = _attn_bwd(proj, attn_sinks, do_a)
    d_hgrn, dlb = _hgrn_bwd(proj, hgrn_lower_bound, do_raw, states, big_blk, HGRN_GROUP)
    pieces = (d_hgrn, d_aq, d_tail, d_kv)
    dw_pieces = [_inproj_wgrad_piece(xnt, p, CB, "inproj_wgrad_" + n)
                 for p, n in zip(pieces, ("hgrn", "aq", "gates", "kv"))]
    grad_x, dnw = _inproj_dgrad(pieces, w_p, x2d, dx2, norm_w, big_blk)

    dwin_r = _to_reference_cols(*dw_pieces).reshape(D, NDEV, IN_SHARD).transpose(1, 0, 2).astype(BF16)
    small = _pack_small_grads(dnw, dlb, sums, dsink)
    slots = lambda a: a.reshape(NDEV, ROW_SHARD, D).astype(BF16)
    partials = [dwin_r, slots(dwbh), slots(dwba), slots(dwout)]
    got = _exchange_pair(partials)
    core = lax.axis_index("c").astype(jnp.int32).reshape(1)
    pair_sums = [_pair_sum(a, g, core, ROW_SHARD, "pair_sum_" + n)
                 for a, g, n in zip(partials, got, ("w_in", "w_bh", "w_ba", "w_out"))]
    (rin, rbh, rba, rout), rsm = _exchange_chips(pair_sums, small)
    g_in, d_in, nm_in, nv_in = _adamw_shard(rin, w_in[0], m_w_in[0], v_w_in[0], 128, "adamw_w_in")
    g_bh, d_bh, nm_bh, nv_bh = _adamw_shard(rbh, w_branch_hgrn[0], m_w_branch_hgrn[0], v_w_branch_hgrn[0], 128, "adamw_w_bh")
    g_ba, d_ba, nm_ba, nv_ba = _adamw_shard(rba, w_branch_attn[0], m_w_branch_attn[0], v_w_branch_attn[0], 128, "adamw_w_ba")
    g_out, d_out, nm_out, nv_out = _adamw_shard(rout, w_out[0], m_w_out[0], v_w_out[0], 128, "adamw_w_out")
    sg, sd, sm, sv = _adamw_small(
        rsm,
        (norm_w, hgrn_lower_bound, hgrn_norm_w, fnw, attn_sinks),
        (m_norm_w, m_hgrn_lower_bound, m_hgrn_norm_w, m_final_norm_w.reshape(1, D), m_attn_sinks),
        (v_norm_w, v_hgrn_lower_bound, v_hgrn_norm_w, v_final_norm_w.reshape(1, D), v_attn_sinks))

    loss = lax.psum(sums[2, 0], ("x", "y", "c"))

    def group(s, w_in_v, bh, ba, out):
        nw, lb, hnw, fn, sinks = s
        return (nw, w_in_v[None], lb, hnw, bh[None], sinks, ba[None], out[None], fn.reshape(D))

    return (loss, grad_x.reshape(1, t, D),
            *group(sg, g_in, g_bh, g_ba, g_out), *group(sd, d_in, d_bh, d_ba, d_out),
            *group(sm, nm_in, nm_bh, nm_ba, nm_out), *group(sv, nv_in, nv_bh, nv_ba, nv_out))
```

```python
import functools

import jax
import jax.numpy as jnp
from jax import lax
from jax.experimental import pallas as pl
from jax.experimental.pallas import tpu as pltpu

F32 = jnp.float32
BF16 = jnp.bfloat16

D = 1024
DIN = 8704
NDEV = 8
IN_SHARD = DIN // NDEV
ROW_SHARD = D // NDEV
HEADS = 8
HD = 128
CH = 64
HGRN_GROUP = 8
QH = 16
AB = 128
EPS = 1e-6
NEG = -1e30
ATT_SCALE = 0.125

C_HGRN = 0
C_HG = 3072
C_AQ = 4096
C_AK = 5120
C_AV = 5376
C_AG = 5632
C_MH = 6656
C_MA = 7680
CB = 512

LR = 0.001
B1 = 0.9
B2 = 0.999
ADAM_EPS = 1e-08
WD = 0.01
STEP = 10

V7X_VMEM_BYTES = 64 * 1024 * 1024
MESH = pl.DeviceIdType.MESH


def _cp(vmem_mb):
    return pltpu.CompilerParams(vmem_limit_bytes=vmem_mb * 1024 * 1024)


def _mm(a, b):
    return jnp.dot(a, b, preferred_element_type=F32)


def _mm_nt(a, b):
    return lax.dot_general(a, b, (((1,), (1,)), ((), ())), preferred_element_type=F32)


def _mm_tn(a, b):
    return lax.dot_general(a, b, (((0,), (0,)), ((), ())), preferred_element_type=F32)


def _mm_exact(a, b):
    return jnp.dot(a, b, preferred_element_type=F32, precision=lax.Precision.HIGHEST)


def _sigmoid(v):
    return 0.5 * jnp.tanh(0.5 * v) + 0.5


def _bf(v):
    return v.astype(BF16)


def _place():
    x, y, c = lax.axis_index("x"), lax.axis_index("y"), lax.axis_index("c")
    return (x, y, c), (x, y, 1 - c), [(1 - x, y), (x, 1 - y), (1 - x, 1 - y)]


def _dev_index(px, py, pc):
    return 4 * px + 2 * py + pc


def _gather_weights(w_in_s, wbh_s, wba_s, wout_s):
    def body(win_ref, wbh_ref, wba_ref, wout_ref, win_g, wbh_g, wba_g, wout_g, sin, s3, send_sems, recv_sems, loc_sems):
        (x, y, c), sibling, chips = _place()
        me = _dev_index(x, y, c)
        sin[...] = win_ref[...].astype(BF16)
        s3[0] = wbh_ref[...].astype(BF16)
        s3[1] = wba_ref[...].astype(BF16)
        s3[2] = wout_ref[...].astype(BF16)
        srcs = [sin, s3.at[0], s3.at[1], s3.at[2]]
        outs = [win_g, wbh_g, wba_g, wout_g]

        def copy(kind, k, block, to, src=None):
            return pltpu.make_async_remote_copy(
                src_ref=srcs[k] if src is None else src, dst_ref=outs[k].at[block], send_sem=send_sems.at[kind, k],
                recv_sem=recv_sems.at[kind, k], device_id=to, device_id_type=MESH)

        local = [pltpu.make_async_copy(srcs[k], outs[k].at[me], loc_sems.at[k]) for k in range(4)]
        for cp in local:
            cp.start()
        first = [copy(0, k, me, sibling) for k in range(4)]
        first += [copy(1 + j, k, me, (*chip, c)) for j, chip in enumerate(chips) for k in range(4)]
        for cp in first:
            cp.start()
        passed = []
        for j, chip in enumerate(chips):
            block = _dev_index(*chip, c)
            for k in range(4):
                copy(1 + j, k, block, (x, y, c)).wait_recv()
            for k in range(4):
                cp = copy(4 + j, k, block, sibling, src=outs[k].at[block])
                cp.start()
                passed.append(cp)
        for k in range(4):
            copy(0, k, _dev_index(x, y, 1 - c), (x, y, c)).wait_recv()
        for j, chip in enumerate(chips):
            for k in range(4):
                copy(4 + j, k, _dev_index(*chip, 1 - c), (x, y, c)).wait_recv()
        for cp in first + passed:
            cp.wait_send()
        for cp in local:
            cp.wait()

    vm = pl.BlockSpec(memory_space=pltpu.VMEM)
    hbm = pl.BlockSpec(memory_space=pl.ANY)
    return pl.pallas_call(
        body,
        out_shape=[jax.ShapeDtypeStruct((NDEV, D, IN_SHARD), BF16)] + [jax.ShapeDtypeStruct((NDEV, ROW_SHARD, D), BF16)] * 3,
        in_specs=[vm, vm, vm, vm],
        out_specs=[hbm, hbm, hbm, hbm],
        scratch_shapes=[pltpu.VMEM((D, IN_SHARD), BF16), pltpu.VMEM((3, ROW_SHARD, D), BF16),
                        pltpu.SemaphoreType.DMA((NDEV - 1, 4)), pltpu.SemaphoreType.DMA((NDEV - 1, 4)),
                        pltpu.SemaphoreType.DMA((4,))],
        name="gather_weights", compiler_params=_cp(32),
    )(w_in_s, wbh_s, wba_s, wout_s)


def _exchange_pair(arrs):
    n = len(arrs)

    def body(*refs):
        ins, got = refs[:n], refs[n:2 * n]
        send_sems, recv_sems = refs[2 * n:]
        (x, y, c), sibling, _ = _place()
        sends = [pltpu.make_async_remote_copy(
            src_ref=ins[k].at[_dev_index(q // 2, q % 2, 1 - c)], dst_ref=got[k].at[q], send_sem=send_sems.at[q, k],
            recv_sem=recv_sems.at[q, k], device_id=sibling, device_id_type=MESH) for q in range(4) for k in range(n)]
        for cp in sends:
            cp.start()
        for cp in sends:
            cp.wait_recv()
        for cp in sends:
            cp.wait_send()

    hbm = pl.BlockSpec(memory_space=pl.ANY)
    return pl.pallas_call(
        body, out_shape=[jax.ShapeDtypeStruct((4,) + a.shape[1:], a.dtype) for a in arrs],
        in_specs=[hbm] * n, out_specs=[hbm] * n,
        scratch_shapes=[pltpu.SemaphoreType.DMA((4, n)), pltpu.SemaphoreType.DMA((4, n))],
        name="exchange_pair",
    )(*arrs)


def _pair_sum(full, got, core, rows, name):
    _, nr, nc = got.shape

    def body(core_ref, a_ref, b_ref, o_ref):
        o_ref[...] = (a_ref[...].astype(F32) + b_ref[...].astype(F32)).astype(BF16)

    blk = pl.BlockSpec((1, rows, nc), lambda q, i, core_ref: (q, i, 0))
    return pl.pallas_call(
        body,
        grid_spec=pltpu.PrefetchScalarGridSpec(
            num_scalar_prefetch=1, grid=(4, nr // rows),
            in_specs=[pl.BlockSpec((1, rows, nc), lambda q, i, core_ref: (2 * q + core_ref[0], i, 0)), blk],
            out_specs=blk),
        out_shape=jax.ShapeDtypeStruct(got.shape, BF16), name=name,
    )(core, full, got)


def _exchange_chips(sums, small):
    n = len(sums)

    def body(*refs):
        ins, sm = refs[:n], refs[n]
        outs, rsm = refs[n + 1:2 * n + 1], refs[2 * n + 1]
        send_sems, recv_sems, loc_sems, out_sems, ssend, srecv = refs[2 * n + 2:2 * n + 8]
        stage = refs[2 * n + 8:]
        (x, y, c), _, chips = _place()
        me = _dev_index(x, y, c)
        my_chip = 2 * x + y
        peers = [(1 - x if r & 4 else x, 1 - y if r & 2 else y, 1 - c if r & 1 else c) for r in range(1, NDEV)]
        local = [pltpu.make_async_copy(ins[k].at[my_chip], stage[k], loc_sems.at[k]) for k in range(n)]
        local.append(pltpu.make_async_copy(sm, rsm.at[me], loc_sems.at[n]))
        copies = []
        for j, (px, py) in enumerate(chips):
            for k in range(n):
                copies.append(pltpu.make_async_remote_copy(
                    src_ref=ins[k].at[2 * px + py], dst_ref=outs[k].at[my_chip], send_sem=send_sems.at[j, k],
                    recv_sem=recv_sems.at[j, k], device_id=(px, py, c), device_id_type=MESH))
        for r, peer in enumerate(peers):
            copies.append(pltpu.make_async_remote_copy(
                src_ref=sm, dst_ref=rsm.at[me], send_sem=ssend.at[r], recv_sem=srecv.at[r],
                device_id=peer, device_id_type=MESH))
        for cp in copies + local:
            cp.start()
        for cp in local:
            cp.wait()
        local = [pltpu.make_async_copy(stage[k], outs[k].at[my_chip], out_sems.at[k]) for k in range(n)]
        for cp in local:
            cp.start()
        for j, (px, py) in enumerate(chips):
            for k in range(n):
                pltpu.make_async_remote_copy(
                    src_ref=ins[k].at[my_chip], dst_ref=outs[k].at[2 * px + py], send_sem=send_sems.at[j, k],
                    recv_sem=recv_sems.at[j, k], device_id=(px, py, c), device_id_type=MESH).wait_recv()
        for r, peer in enumerate(peers):
            pltpu.make_async_remote_copy(
                src_ref=sm, dst_ref=rsm.at[_dev_index(*peer)], send_sem=ssend.at[r], recv_sem=srecv.at[r],
                device_id=peer, device_id_type=MESH).wait_recv()
        for cp in copies:
            cp.wait_send()
        for cp in local:
            cp.wait()

    hbm = pl.BlockSpec(memory_space=pl.ANY)
    res = pl.pallas_call(
        body,
        out_shape=[jax.ShapeDtypeStruct(a.shape, a.dtype) for a in sums] + [jax.ShapeDtypeStruct((NDEV,) + small.shape, F32)],
        in_specs=[hbm] * (n + 1), out_specs=[hbm] * (n + 1),
        scratch_shapes=[pltpu.SemaphoreType.DMA((3, n)), pltpu.SemaphoreType.DMA((3, n)), pltpu.SemaphoreType.DMA((n + 1,)),
                        pltpu.SemaphoreType.DMA((n,)), pltpu.SemaphoreType.DMA((NDEV - 1,)), pltpu.SemaphoreType.DMA((NDEV - 1,))]
        + [pltpu.VMEM(a.shape[1:], a.dtype) for a in sums],
        name="exchange_chips",
    )(*sums, small)
    return res[:n], res[n]


def _inproj_fwd(x2d, norm_w, w_p, tb, nb):
    t = x2d.shape[0]

    def body(x_ref, nw_ref, w_ref, proj_ref, xnt_ref, xn_s):
        @pl.when(pl.program_id(1) == 0)
        def _():
            xv = x_ref[...]
            r = lax.rsqrt(jnp.mean(xv * xv, axis=-1, keepdims=True) + EPS)
            xn = (xv * r) * nw_ref[...]
            xn_s[...] = xn.astype(BF16)
            xnt_ref[...] = xn.T.astype(BF16)

        proj_ref[...] = _mm(xn_s[...], w_ref[...])

    return pl.pallas_call(
        body, grid=(t // tb, DIN // nb),
        in_specs=[pl.BlockSpec((tb, D), lambda i, j: (i, 0)), pl.BlockSpec((1, D), lambda i, j: (0, 0)),
                  pl.BlockSpec((D, nb), lambda i, j: (0, j))],
        out_specs=[pl.BlockSpec((tb, nb), lambda i, j: (i, j)), pl.BlockSpec((D, tb), lambda i, j: (0, i))],
        out_shape=[jax.ShapeDtypeStruct((t, DIN), F32), jax.ShapeDtypeStruct((D, t), BF16)],
        scratch_shapes=[pltpu.VMEM((tb, D), BF16)],
        name="inproj_fwd", compiler_params=_cp(56),
    )(x2d, norm_w, w_p)


def _lower_bound(lb_ref):
    a0 = lb_ref[0:1, :]
    a1 = lb_ref[1:2, :]
    mx = jnp.maximum(a0, a1)
    e0 = jnp.exp(a0 - mx)
    e1 = jnp.exp(a1 - mx)
    return e0 / (e0 + e1)


def _hgrn_chunk_fwd(hq, hf, lb, tril):
    sg = _sigmoid(hf)
    f = lb + (1.0 - lb) * sg
    g = jnp.log(f)
    k = 1.0 - f
    sq = _sigmoid(hq)
    q = hq * sq
    b = _mm_exact(tril, g)
    last_row = lax.broadcasted_iota(jnp.int32, b.shape, 0) == CH - 1
    b_last = jnp.sum(jnp.where(last_row, b, 0.0), axis=0, keepdims=True)
    c = 0.5 * b_last
    eb = jnp.exp(b)
    ea = jnp.exp(b - c)
    ek = jnp.exp(c - b)
    ed = jnp.exp(b_last - b)
    ebl = jnp.exp(b_last)
    return dict(sg=sg, f=f, k=k, sq=sq, q=q, eb=eb, ea=ea, ek=ek, ed=ed, ebl=ebl,
                qe=q * eb, qa=q * ea, ka=k * ek, kd=k * ed)


def _tri(lower):
    r = lax.broadcasted_iota(jnp.int32, (CH, CH), 0)
    c = lax.broadcasted_iota(jnp.int32, (CH, CH), 1)
    return (r >= c) if lower else (c >= r)


def _head_segment(p_ref, rows, j, hg):
    return p_ref[rows, j * HD * hg:(j + 1) * HD * hg]


def _head(a, k):
    return a[:, k * HD:(k + 1) * HD]


def _hgrn_fwd(proj, lbw, rb, hg):
    assert hg == HEADS
    t = proj.shape[0]
    ncb = rb // CH

    def body(p_ref, lb_ref, o_ref, st_ref, s_scr):
        @pl.when(pl.program_id(1) == 0)
        def _():
            s_scr[...] = jnp.zeros_like(s_scr)

        lb = _lower_bound(lb_ref)
        causal = _tri(True)
        tril = causal.astype(F32)
        heads = range(hg)

        def chunk(cc, carry):
            r0 = pl.multiple_of(cc * CH, CH)
            rows = pl.ds(r0, CH)
            e = _hgrn_chunk_fwd(_head_segment(p_ref, rows, 0, hg), _head_segment(p_ref, rows, 1, hg), lb, tril)
            v = _bf(_head_segment(p_ref, rows, 2, hg))
            sts = [s_scr[k] for k in heads]
            qa, ka, qe, kd = _bf(e["qa"]), _bf(e["ka"]), _bf(e["qe"]), _bf(e["kd"])
            a = [_bf(jnp.where(causal, _mm_nt(_head(qa, k), _head(ka, k)), 0.0)) for k in heads]
            o_inter = [_mm_nt(_head(qe, k), _bf(sts[k])) for k in heads]
            kv = [_mm_tn(_head(v, k), _head(kd, k)) for k in heads]
            o_intra = [_mm(a[k], _head(v, k)) for k in heads]
            for k in heads:
                st_ref[cc, k] = sts[k]
                o_ref[rows, k * HD:(k + 1) * HD] = o_inter[k] + o_intra[k]
                s_scr[k] = sts[k] * _head(e["ebl"], k) + kv[k]
            return carry

        lax.fori_loop(0, ncb, chunk, 0)

    return pl.pallas_call(
        body, grid=(HEADS // hg, t // rb),
        in_specs=[pl.BlockSpec((rb, 3 * HD * hg), lambda h, i: (i, h)), pl.BlockSpec((2, HD * hg), lambda h, i: (0, h))],
        out_specs=[pl.BlockSpec((rb, HD * hg), lambda h, i: (i, h)),
                   pl.BlockSpec((ncb, hg, HD, HD), lambda h, i: (i, h, 0, 0))],
        out_shape=[jax.ShapeDtypeStruct((t, D), F32), jax.ShapeDtypeStruct((t // CH, HEADS, HD, HD), F32)],
        scratch_shapes=[pltpu.VMEM((hg, HD, HD), F32)],
        name="hgrn_fwd", compiler_params=_cp(48),
    )(proj, lbw)


def _hgrn_bwd(proj, lbw, do_raw, states, rb, hg):
    assert hg == HEADS
    t = proj.shape[0]
    nblk = t // rb
    ncb = rb // CH
    wd = HD * hg

    def body(p_ref, lb_ref, do_ref, st_ref, dp_ref, dlb_ref, ds_scr):
        @pl.when(pl.program_id(1) == 0)
        def _():
            ds_scr[...] = jnp.zeros_like(ds_scr)
            dlb_ref[...] = jnp.zeros_like(dlb_ref)

        lb = _lower_bound(lb_ref)
        causal = _tri(True)
        tril = causal.astype(F32)
        triu = _tri(False).astype(F32)
        last_row = lax.broadcasted_iota(jnp.int32, (CH, HD * hg), 0) == CH - 1
        row0 = lax.broadcasted_iota(jnp.int32, (8, HD * hg), 0) == 0
        heads = range(hg)
        wide = lambda parts: jnp.concatenate(parts, axis=1)

        def chunk(it, carry):
            cc = ncb - 1 - it
            r0 = pl.multiple_of(cc * CH, CH)
            rows = pl.ds(r0, CH)
            hq = _head_segment(p_ref, rows, 0, hg)
            e = _hgrn_chunk_fwd(hq, _head_segment(p_ref, rows, 1, hg), lb, tril)
            v = _bf(_head_segment(p_ref, rows, 2, hg))
            do = _bf(do_ref[rows, :])
            sts = [st_ref[cc, k] for k in heads]
            dsts = [ds_scr[k] for k in heads]
            dlb_acc = dlb_ref[...]
            qa, ka, qe, kd = _bf(e["qa"]), _bf(e["ka"]), _bf(e["qe"]), _bf(e["kd"])
            a = [_bf(jnp.where(causal, _mm_nt(_head(qa, k), _head(ka, k)), 0.0)) for k in heads]
            da = [_bf(jnp.where(causal, _mm_nt(_head(do, k), _head(v, k)), 0.0)) for k in heads]
            dqe = wide([_mm(_head(do, k), _bf(sts[k])) for k in heads])
            dkd = wide([_mm(_head(v, k), _bf(dsts[k])) for k in heads])
            dv_state = [_mm_nt(_head(kd, k), _bf(dsts[k])) for k in heads]
            ds_new = [_mm_tn(_head(do, k), _head(qe, k)) for k in heads]
            dv_intra = [_mm_tn(a[k], _head(do, k)) for k in heads]
            dqa = wide([_mm(da[k], _head(ka, k)) for k in heads])
            dka = wide([_mm_tn(da[k], _head(qa, k)) for k in heads])
            dv = wide([dv_intra[k] + dv_state[k] for k in heads])
            dbl = e["ebl"] * wide([jnp.sum(sts[k] * dsts[k], axis=0, keepdims=True) for k in heads])
            dq = dqe * e["eb"] + dqa * e["ea"]
            dk = dka * e["ek"] + dkd * e["ed"]
            dkd_kd = dkd * kd.astype(F32)
            db = dqe * qe.astype(F32) + dqa * qa.astype(F32) - dka * ka.astype(F32) - dkd_kd
            db = db + jnp.where(last_row, dbl + jnp.sum(dkd_kd, axis=0, keepdims=True), 0.0)
            dg = _mm_exact(triu, db)
            df = dg / e["f"] - dk
            sg = e["sg"]
            sq = e["sq"]
            dhq = _bf(dq * (sq * (1.0 + hq * (1.0 - sq))))
            dhf = _bf(df * (1.0 - lb) * sg * (1.0 - sg))
            dhi = _bf(dv)
            dlb_new = dlb_acc + jnp.where(row0, jnp.sum(df * (1.0 - sg), axis=0, keepdims=True), 0.0)
            for k in heads:
                ds_scr[k] = ds_new[k] + dsts[k] * _head(e["ebl"], k)
            dp_ref[rows, 0:wd] = dhq
            dp_ref[rows, wd:2 * wd] = dhf
            dp_ref[rows, 2 * wd:3 * wd] = dhi
            dlb_ref[...] = dlb_new
            return carry

        lax.fori_loop(0, ncb, chunk, 0)

    rev = lambda h, i: (nblk - 1 - i, h)
    return pl.pallas_call(
        body, grid=(HEADS // hg, nblk),
        in_specs=[pl.BlockSpec((rb, 3 * HD * hg), rev), pl.BlockSpec((2, HD * hg), lambda h, i: (0, h)),
                  pl.BlockSpec((rb, HD * hg), rev), pl.BlockSpec((ncb, hg, HD, HD), lambda h, i: (nblk - 1 - i, h, 0, 0))],
        out_specs=[pl.BlockSpec((rb, 3 * HD * hg), rev), pl.BlockSpec((8, HD * hg), lambda h, i: (0, h))],
        out_shape=[jax.ShapeDtypeStruct((t, 3 * D), BF16), jax.ShapeDtypeStruct((8, D), F32)],
        scratch_shapes=[pltpu.VMEM((hg, HD, HD), F32)],
        name="hgrn_bwd", compiler_params=_cp(48),
    )(proj, lbw, do_raw, states)


def _kv_variants(tile, odd):
    low = lax.broadcasted_iota(jnp.int32, tile.shape, 1) < 64
    if odd:
        hi = jnp.where(low, 0.0, tile)
        lo = pltpu.roll(hi, 64, 1)
    else:
        lo = jnp.where(low, tile, 0.0)
        hi = pltpu.roll(lo, 64, 1)
    return _bf(lo), _bf(hi)


def _attn_masks(n):
    qi = lax.broadcasted_iota(jnp.int32, (AB, AB), 0)
    kj = lax.broadcasted_iota(jnp.int32, (AB, AB), 1)
    cur = kj <= qi
    return cur, cur | (n > 0), qi <= kj


def _kv_all(prev_ref, cur_ref):
    out = []
    for tl in range(2):
        cols = slice(tl * 128, (tl + 1) * 128)
        tile = jnp.concatenate([prev_ref[:, cols], cur_ref[:, cols]], axis=0)
        out.append(_kv_variants(tile, 0))
        out.append(_kv_variants(tile, 1))
    return out


def _attn_softmax(s2, sink, cur, ok):
    s = jnp.where(ok, jnp.where(cur, s2[:, AB:], s2[:, :AB]) * ATT_SCALE, NEG)
    m = jnp.maximum(jnp.max(s, axis=-1, keepdims=True), sink)
    p = jnp.exp(s - m)
    es = jnp.exp(sink - m)
    inv = 1.0 / (jnp.sum(p, axis=-1, keepdims=True) + es)
    return p * inv, es * inv


def _spread(pc, cur):
    return jnp.concatenate([jnp.where(cur, 0.0, pc), jnp.where(cur, pc, 0.0)], axis=1)


def _spread_t(pct, cur_t):
    return jnp.concatenate([jnp.where(cur_t, 0.0, pct), jnp.where(cur_t, pct, 0.0)], axis=0)


def _attn_fwd(proj, sinks):
    t = proj.shape[0]
    nb = t // AB

    def body(q_ref, kc_ref, kp_ref, vc_ref, vp_ref, sink_ref, o_ref):
        cur, ok, _ = _attn_masks(pl.program_id(0))
        kvars = _kv_all(kp_ref, kc_ref)
        vvars = _kv_all(vp_ref, vc_ref)
        qps = [_bf(q_ref[:, 128 * j:128 * (j + 1)]) for j in range(8)]
        scores = [[_mm_nt(qps[j], kvars[j // 2][ab]) for ab in range(2)] for j in range(8)]
        for j in range(8):
            parts = []
            for ab in range(2):
                pc, _ = _attn_softmax(scores[j][ab], sink_ref[0, 2 * j + ab], cur, ok)
                parts.append(_mm(_bf(_spread(pc, cur)), vvars[j // 2][ab]))
            o_ref[:, 128 * j:128 * (j + 1)] = parts[0] + parts[1]

    prev = lambda n: jnp.maximum(n - 1, 0)
    return pl.pallas_call(
        body, grid=(nb,),
        in_specs=[pl.BlockSpec((AB, D), lambda n: (n, C_AQ // D)),
                  pl.BlockSpec((AB, 256), lambda n: (n, C_AK // 256)),
                  pl.BlockSpec((AB, 256), lambda n: (prev(n), C_AK // 256)),
                  pl.BlockSpec((AB, 256), lambda n: (n, C_AV // 256)),
                  pl.BlockSpec((AB, 256), lambda n: (prev(n), C_AV // 256)),
                  pl.BlockSpec(memory_space=pltpu.SMEM)],
        out_specs=pl.BlockSpec((AB, D), lambda n: (n, 0)),
        out_shape=jax.ShapeDtypeStruct((t, D), F32),
        name="attn_fwd", compiler_params=_cp(32),
    )(proj, proj, proj, proj, proj, sinks)


def _attn_bwd(proj, sinks, do_a):
    t = proj.shape[0]
    nb = t // AB

    def body(q_ref, kc_ref, kp_ref, vc_ref, vp_ref, do_ref, sink_ref, dq_ref, dkv_ref, dsink_ref, carry):
        n = pl.program_id(0)

        @pl.when(n == 0)
        def _():
            dsink_ref[...] = jnp.zeros_like(dsink_ref)
            carry[...] = jnp.zeros_like(carry)

        @pl.when(n < nb)
        def _():
            cur, ok, cur_t = _attn_masks(n)
            low = lax.broadcasted_iota(jnp.int32, (2 * AB, 128), 1) < 64
            lane = lax.broadcasted_iota(jnp.int32, (8, 128), 1)
            row0 = lax.broadcasted_iota(jnp.int32, (8, 128), 0) == 0
            kvars = _kv_all(kp_ref, kc_ref)
            vvars = _kv_all(vp_ref, vc_ref)
            qps = [_bf(q_ref[:, 128 * j:128 * (j + 1)]) for j in range(8)]
            dops = [_bf(do_ref[:, 128 * j:128 * (j + 1)]) for j in range(8)]
            scores = [[_mm_nt(qps[j], kvars[j // 2][ab]) for ab in range(2)] for j in range(8)]
            dps = [[_mm_nt(dops[j], vvars[j // 2][ab]) for ab in range(2)] for j in range(8)]
            dsink = jnp.zeros((8, 128), F32)
            dk_ab = [[None, None] for _ in range(4)]
            dv_ab = [[None, None] for _ in range(4)]
            for j in range(8):
                g = j // 2
                dqp = None
                for ab in range(2):
                    head = 2 * j + ab
                    pc, ps = _attn_softmax(scores[j][ab], sink_ref[0, head], cur, ok)
                    dpc = jnp.where(cur, dps[j][ab][:, AB:], dps[j][ab][:, :AB])
                    rs = jnp.sum(pc * dpc, axis=-1, keepdims=True)
                    dsc = pc * (dpc - rs)
                    dsink = dsink + jnp.where(row0 & (lane == head), -jnp.sum(ps * rs), 0.0)
                    term = _mm(_bf(_spread(dsc, cur)), kvars[g][ab])
                    dqp = term if dqp is None else dqp + term
                    dk_t = _mm(_bf(_spread_t(dsc.T, cur_t)), qps[j])
                    dv_t = _mm(_bf(_spread_t(pc.T, cur_t)), dops[j])
                    dk_ab[g][ab] = dk_t if dk_ab[g][ab] is None else dk_ab[g][ab] + dk_t
                    dv_ab[g][ab] = dv_t if dv_ab[g][ab] is None else dv_ab[g][ab] + dv_t
                dq_ref[:, 128 * j:128 * (j + 1)] = _bf(dqp * ATT_SCALE)
            for tl in range(2):
                ke, ko = dk_ab[2 * tl], dk_ab[2 * tl + 1]
                ve, vo = dv_ab[2 * tl], dv_ab[2 * tl + 1]
                dkt = (jnp.where(low, ke[0], 0.0) + pltpu.roll(jnp.where(low, 0.0, ke[1]), 64, 1)
                       + jnp.where(low, 0.0, ko[1]) + pltpu.roll(jnp.where(low, ko[0], 0.0), 64, 1)) * ATT_SCALE
                dvt = (jnp.where(low, ve[0], 0.0) + pltpu.roll(jnp.where(low, 0.0, ve[1]), 64, 1)
                       + jnp.where(low, 0.0, vo[1]) + pltpu.roll(jnp.where(low, vo[0], 0.0), 64, 1))
                kcols = slice(tl * 128, (tl + 1) * 128)
                vcols = slice(256 + tl * 128, 256 + (tl + 1) * 128)
                dkv_ref[:, kcols] = _bf(carry[:, kcols] + dkt[0:AB])
                dkv_ref[:, vcols] = _bf(carry[:, vcols] + dvt[0:AB])
                carry[:, kcols] = dkt[AB:2 * AB]
                carry[:, vcols] = dvt[AB:2 * AB]
            dsink_ref[...] += dsink

        @pl.when(n == nb)
        def _():
            dkv_ref[...] = _bf(carry[...])

    cur = lambda n: jnp.minimum(n, nb - 1)
    prev = lambda n: jnp.clip(n - 1, 0, nb - 1)
    return pl.pallas_call(
        body, grid=(nb + 1,),
        in_specs=[pl.BlockSpec((AB, D), lambda n: (cur(n), C_AQ // D)),
                  pl.BlockSpec((AB, 256), lambda n: (cur(n), C_AK // 256)),
                  pl.BlockSpec((AB, 256), lambda n: (prev(n), C_AK // 256)),
                  pl.BlockSpec((AB, 256), lambda n: (cur(n), C_AV // 256)),
                  pl.BlockSpec((AB, 256), lambda n: (prev(n), C_AV // 256)),
                  pl.BlockSpec((AB, D), lambda n: (cur(n), 0)),
                  pl.BlockSpec(memory_space=pltpu.SMEM)],
        out_specs=[pl.BlockSpec((AB, D), lambda n: (cur(n), 0)),
                   pl.BlockSpec((AB, 512), lambda n: (prev(n), 0)),
                   pl.BlockSpec((8, 128), lambda n: (0, 0))],
        out_shape=[jax.ShapeDtypeStruct((t, D), BF16), jax.ShapeDtypeStruct((t, 512), BF16),
                   jax.ShapeDtypeStruct((8, 128), F32)],
        scratch_shapes=[pltpu.VMEM((AB, 512), F32)],
        name="attn_bwd", compiler_params=_cp(32),
    )(proj, proj, proj, proj, proj, do_a, sinks)


def _silu_and_grad(v):
    s = _sigmoid(v)
    return v * s, s * (1.0 + v * (1.0 - s))


def _tail(o_raw, o_a, proj, x2d, tgt, wbh, wba, wout, hnw, fnw, tb):
    t = x2d.shape[0]

    def body(or_ref, oa_ref, hg_ref, ag0, ag1, mh0, mh1, ma0, ma1, x_ref, t_ref, wbh_ref, wba_ref, wout_ref, hnw_ref,
             fnw_ref, dx2_ref, dor_ref, doa_ref, dhg_ref, dagm_ref, gh_ref, ga_ref, mg_ref, dyh_ref, dya_ref, dx2b_ref,
             sums_ref):
        @pl.when(pl.program_id(0) == 0)
        def _():
            sums_ref[...] = jnp.zeros_like(sums_ref)

        halves = lambda a, b: jnp.concatenate([a[...], b[...]], axis=1)
        hnw_v = hnw_ref[...]
        fnw_v = fnw_ref[...]
        o = or_ref[...]
        rs, xhs = [], []
        for h in range(HEADS):
            oh = o[:, h * HD:(h + 1) * HD]
            r = lax.rsqrt(jnp.mean(oh * oh, axis=-1, keepdims=True) + EPS)
            rs.append(r)
            xhs.append(oh * r)
        xh = jnp.concatenate(xhs, axis=1)
        on = xh * hnw_v
        sil_hg, dsil_hg = _silu_and_grad(hg_ref[...])
        gh_b = _bf(on * sil_hg)
        y_h = _mm(gh_b, wbh_ref[...])
        oa = oa_ref[...]
        sil_ag, dsil_ag = _silu_and_grad(halves(ag0, ag1))
        ga_b = _bf(oa * sil_ag)
        y_a = _mm(ga_b, wba_ref[...])
        s_mh = _sigmoid(halves(mh0, mh1))
        s_ma = _sigmoid(halves(ma0, ma1))
        mg_b = _bf(s_mh * y_h + s_ma * y_a)
        x2 = x_ref[...] + _mm(mg_b, wout_ref[...])
        r2 = lax.rsqrt(jnp.mean(x2 * x2, axis=-1, keepdims=True) + EPS)
        xh2 = x2 * r2
        err = xh2 * fnw_v - t_ref[...]
        loss = 0.5 * jnp.sum(jnp.mean(err * err, axis=-1, keepdims=True))
        dy = err * (1.0 / D)
        dfnw = jnp.sum(dy * xh2, axis=0, keepdims=True)
        dxh2 = dy * fnw_v
        dx2 = r2 * (dxh2 - xh2 * jnp.mean(dxh2 * xh2, axis=-1, keepdims=True))
        dx2_ref[...] = dx2
        dx2_b = _bf(dx2)
        dmg = _mm_nt(dx2_b, wout_ref[...])
        dmg_h = dmg * s_mh
        dmg_a = dmg * s_ma
        dyh_b = _bf(dmg_h)
        dya_b = _bf(dmg_a)
        dagm_ref[:, D:2 * D] = _bf(dmg_h * y_h * (1.0 - s_mh))
        dagm_ref[:, 2 * D:3 * D] = _bf(dmg_a * y_a * (1.0 - s_ma))
        dgh = _mm_nt(dyh_b, wbh_ref[...])
        dga = _mm_nt(dya_b, wba_ref[...])
        doa_ref[...] = dga * sil_ag
        dagm_ref[:, 0:D] = _bf(dga * oa * dsil_ag)
        dhg_ref[...] = _bf(dgh * on * dsil_hg)
        don = dgh * sil_hg
        dhnw = jnp.sum(don * xh, axis=0, keepdims=True)
        dxh = don * hnw_v
        dos = []
        for h in range(HEADS):
            sl = slice(h * HD, (h + 1) * HD)
            dos.append(rs[h] * (dxh[:, sl] - xhs[h] * jnp.mean(dxh[:, sl] * xhs[h], axis=-1, keepdims=True)))
        dor_ref[...] = jnp.concatenate(dos, axis=1)
        gh_ref[...] = gh_b
        ga_ref[...] = ga_b
        mg_ref[...] = mg_b
        dyh_ref[...] = dyh_b
        dya_ref[...] = dya_b
        dx2b_ref[...] = dx2_b
        row = lax.broadcasted_iota(jnp.int32, (8, D), 0)
        sums_ref[...] += jnp.where(row == 0, dfnw, 0.0) + jnp.where(row == 1, dhnw, 0.0) + jnp.where(row == 2, loss, 0.0)

    rowblk = lambda c: pl.BlockSpec((tb, D), lambda i: (i, c))
    half = lambda c: pl.BlockSpec((tb, 512), lambda i: (i, c))
    full = lambda shape: pl.BlockSpec(shape, lambda i: (0, 0))
    return pl.pallas_call(
        body, grid=(t // tb,),
        in_specs=[rowblk(0), rowblk(0), rowblk(C_HG // D), half(C_AG // 512), half(C_AG // 512 + 1), half(C_MH // 512),
                  half(C_MH // 512 + 1), half(C_MA // 512), half(C_MA // 512 + 1), rowblk(0), rowblk(0),
                  full((D, D)), full((D, D)), full((D, D)), full((1, D)), full((1, D))],
        out_specs=[rowblk(0), rowblk(0), rowblk(0), rowblk(0), pl.BlockSpec((tb, 3 * D), lambda i: (i, 0))]
        + [rowblk(0)] * 6 + [full((8, D))],
        out_shape=[jax.ShapeDtypeStruct((t, D), F32)] * 3
        + [jax.ShapeDtypeStruct((t, D), BF16), jax.ShapeDtypeStruct((t, 3 * D), BF16)]
        + [jax.ShapeDtypeStruct((t, D), BF16)] * 6 + [jax.ShapeDtypeStruct((8, D), F32)],
        name="tail", compiler_params=_cp(56),
    )(o_raw, o_a, proj, proj, proj, proj, proj, proj, proj, x2d, tgt, wbh, wba, wout, hnw, fnw)


def _wgrad3(gh, dyh, ga, dya, mg, dx2b, tk):
    t = dyh.shape[0]

    def body(a0, b0, a1, b1, a2, b2, o0, o1, o2):
        @pl.when(pl.program_id(0) == 0)
        def _():
            o0[...] = jnp.zeros_like(o0)
            o1[...] = jnp.zeros_like(o1)
            o2[...] = jnp.zeros_like(o2)

        o0[...] += _mm_tn(a0[...], b0[...])
        o1[...] += _mm_tn(a1[...], b1[...])
        o2[...] += _mm_tn(a2[...], b2[...])

    blk = pl.BlockSpec((tk, D), lambda k: (k, 0))
    out = pl.BlockSpec((D, D), lambda k: (0, 0))
    return pl.pallas_call(
        body, grid=(t // tk,), in_specs=[blk] * 6, out_specs=[out] * 3,
        out_shape=[jax.ShapeDtypeStruct((D, D), F32)] * 3,
        name="wgrad3", compiler_params=_cp(48),
    )(gh, dyh, ga, dya, mg, dx2b)


def _inproj_wgrad_piece(xnt, piece, nb, name):
    t = xnt.shape[1]
    width = piece.shape[1]

    def body(xnt_ref, p_ref, o_ref):
        o_ref[...] = _mm(xnt_ref[...], p_ref[...])

    return pl.pallas_call(
        body, grid=(width // nb,),
        in_specs=[pl.BlockSpec((D, t), lambda j: (0, 0), pipeline_mode=pl.Buffered(1)),
                  pl.BlockSpec((t, nb), lambda j: (0, j))],
        out_specs=pl.BlockSpec((D, nb), lambda j: (0, j)),
        out_shape=jax.ShapeDtypeStruct((D, width), F32),
        name=name, compiler_params=_cp(56),
    )(xnt, piece)


def _inproj_dgrad(pieces, w_p, x2d, dx2, norm_w, tb):
    t = x2d.shape[0]

    def body(*refs):
        piece_refs = refs[:len(pieces)]
        w_ref, x_ref, dx2_ref, nw_ref, gx_ref, dnw_ref = refs[len(pieces):]

        @pl.when(pl.program_id(0) == 0)
        def _():
            dnw_ref[...] = jnp.zeros_like(dnw_ref)

        dxn = None
        off = 0
        for p in piece_refs:
            width = p.shape[1]
            term = _mm_nt(p[...], w_ref[:, off:off + width])
            dxn = term if dxn is None else dxn + term
            off += width
        xv = x_ref[...]
        r = lax.rsqrt(jnp.mean(xv * xv, axis=-1, keepdims=True) + EPS)
        xh = xv * r
        dxh = dxn * nw_ref[...]
        gx_ref[...] = dx2_ref[...] + r * (dxh - xh * jnp.mean(dxh * xh, axis=-1, keepdims=True))
        row0 = lax.broadcasted_iota(jnp.int32, (8, D), 0) == 0
        dnw_ref[...] += jnp.where(row0, jnp.sum(dxn * xh, axis=0, keepdims=True), 0.0)

    rowblk = pl.BlockSpec((tb, D), lambda i: (i, 0))
    return pl.pallas_call(
        body, grid=(t // tb,),
        in_specs=[pl.BlockSpec((tb, p.shape[1]), lambda i: (i, 0)) for p in pieces]
        + [pl.BlockSpec((D, DIN), lambda i: (0, 0), pipeline_mode=pl.Buffered(1)), rowblk, rowblk,
           pl.BlockSpec((1, D), lambda i: (0, 0))],
        out_specs=[rowblk, pl.BlockSpec((8, D), lambda i: (0, 0))],
        out_shape=[jax.ShapeDtypeStruct((t, D), F32), jax.ShapeDtypeStruct((8, D), F32)],
        name="inproj_dgrad", compiler_params=_cp(60),
    )(*pieces, w_p, x2d, dx2, norm_w)


def _adamw_math(w, g, m, v):
    m = B1 * m + (1.0 - B1) * g
    v = B2 * v + (1.0 - B2) * (g * g)
    m_hat = m / (1.0 - B1 ** STEP)
    v_hat = v / (1.0 - B2 ** STEP)
    delta = -LR * (m_hat / (jnp.sqrt(v_hat) + ADAM_EPS) + WD * w)
    return delta, m, v


def _adamw_shard(parts, w, m, v, rows, name):
    nparts, nr, nc = parts.shape

    def body(p_ref, w_ref, m_ref, v_ref, g_ref, d_ref, nm_ref, nv_ref):
        g = p_ref[0].astype(F32)
        for s in range(1, nparts):
            g = g + p_ref[s].astype(F32)
        d, nm, nv = _adamw_math(w_ref[...], g, m_ref[...], v_ref[...])
        g_ref[...] = g
        d_ref[...] = d
        nm_ref[...] = nm
        nv_ref[...] = nv

    blk = pl.BlockSpec((rows, nc), lambda i: (i, 0))
    return pl.pallas_call(
        body, grid=(nr // rows,),
        in_specs=[pl.BlockSpec((nparts, rows, nc), lambda i: (0, i, 0)), blk, blk, blk],
        out_specs=[blk] * 4, out_shape=[jax.ShapeDtypeStruct((nr, nc), F32)] * 4,
        name=name, compiler_params=_cp(48),
    )(parts, w, m, v)


SMALL_ROWS = dict(norm_w=0, lower_bound=1, hgrn_norm_w=3, final_norm_w=4, sinks=5)


def _pack_small_grads(dnw, dlb, sums, dsink):
    def body(dnw_ref, dlb_ref, sums_ref, dsink_ref, o_ref):
        o_ref[...] = jnp.zeros_like(o_ref)
        o_ref[0:1, :] = dnw_ref[0:1, :]
        o_ref[1:2, :] = dlb_ref[0:1, :]
        o_ref[3:4, :] = sums_ref[1:2, :]
        o_ref[4:5, :] = sums_ref[0:1, :]
        o_ref[5:6, 0:128] = dsink_ref[0:1, :]

    return pl.pallas_call(body, out_shape=jax.ShapeDtypeStruct((8, D), F32), name="pack_small_grads")(dnw, dlb, sums, dsink)


def _adamw_small(parts, ws, ms, vs):
    shapes = [a.shape for a in ws]

    def body(p_ref, *refs):
        w, m, v = refs[0:5], refs[5:10], refs[10:15]
        outs = [refs[15 + 5 * i:20 + 5 * i] for i in range(4)]

        def total(row, width):
            g = p_ref[0, row:row + 1, 0:width]
            for s in range(1, NDEV):
                g = g + p_ref[s, row:row + 1, 0:width]
            return g

        lb = _lower_bound(w[1])
        ga0 = total(1, D) * lb * (1.0 - lb)
        grads = [total(0, D), None, total(3, D), total(4, D), total(5, QH)]
        for i in (0, 2, 3, 4):
            res = (grads[i],) + _adamw_math(w[i][...], grads[i], m[i][...], v[i][...])
            for o, val in zip(outs, res):
                o[i][...] = val
        for r, g in ((0, ga0), (1, -ga0)):
            res = (g,) + _adamw_math(w[1][r:r + 1, :], g, m[1][r:r + 1, :], v[1][r:r + 1, :])
            for o, val in zip(outs, res):
                o[1][r:r + 1, :] = val

    res = pl.pallas_call(
        body, out_shape=[jax.ShapeDtypeStruct(s, F32) for s in shapes] * 4, name="adamw_small",
    )(parts, *ws, *ms, *vs)
    return [res[5 * i:5 * i + 5] for i in range(4)]


def kernel(x, norm_w, w_in, hgrn_lower_bound, hgrn_norm_w, w_branch_hgrn, attn_sinks, w_branch_attn, w_out, final_norm_w, loss_target, m_norm_w, m_w_in, m_hgrn_lower_bound, m_hgrn_norm_w, m_w_branch_hgrn, m_attn_sinks, m_w_branch_attn, m_w_out, m_final_norm_w, v_norm_w, v_w_in, v_hgrn_lower_bound, v_hgrn_norm_w, v_w_branch_hgrn, v_attn_sinks, v_w_branch_attn, v_w_out, v_final_norm_w):
    t = x.shape[1]
    x2d = x.reshape(t, D)
    tgt = loss_target.reshape(t, D)
    fnw = final_norm_w.reshape(1, D)
    row_blk = min(256, t)
    big_blk = min(512, t)

    win_g, wbh_g, wba_g, wout_g = _gather_weights(w_in[0], w_branch_hgrn[0], w_branch_attn[0], w_out[0])
    w_p = win_g.transpose(1, 0, 2).reshape(D, DIN)
    wbh, wba, wout = wbh_g.reshape(D, D), wba_g.reshape(D, D), wout_g.reshape(D, D)

    proj, xnt = _inproj_fwd(x2d, norm_w, w_p, min(1024, t), 2176)
    o_raw, states = _hgrn_fwd(proj, hgrn_lower_bound, big_blk, HGRN_GROUP)
    o_a = _attn_fwd(proj, attn_sinks)
    (dx2, do_raw, do_a, d_hg, d_agm, gh, ga, mg, dyh, dya, dx2b, sums) = _tail(
        o_raw, o_a, proj, x2d, tgt, wbh, wba, wout, hgrn_norm_w, fnw, row_blk)
    dwbh, dwba, dwout = _wgrad3(gh, dyh, ga, dya, mg, dx2b, big_blk)
    d_aq, d_kv, dsink = _attn_bwd(proj, attn_sinks, do_a)
    d_hgrn, dlb = _hgrn_bwd(proj, hgrn_lower_bound, do_raw, states, big_blk, HGRN_GROUP)
    pieces = (d_hgrn, d_hg, d_aq, d_kv, d_agm)
    dw_pieces = [_inproj_wgrad_piece(xnt, p, CB, "inproj_wgrad_" + n)
                 for p, n in zip(pieces, ("hgrn", "hgate", "aq", "kv", "gates"))]
    grad_x, dnw = _inproj_dgrad(pieces, w_p, x2d, dx2, norm_w, big_blk)

    dwin_r = jnp.concatenate(dw_pieces, axis=1).reshape(D, NDEV, IN_SHARD).transpose(1, 0, 2).astype(BF16)
    small = _pack_small_grads(dnw, dlb, sums, dsink)
    slots = lambda a: a.reshape(NDEV, ROW_SHARD, D).astype(BF16)
    partials = [dwin_r, slots(dwbh), slots(dwba), slots(dwout)]
    got = _exchange_pair(partials)
    core = lax.axis_index("c").astype(jnp.int32).reshape(1)
    pair_sums = [_pair_sum(a, g, core, ROW_SHARD, "pair_sum_" + n)
                 for a, g, n in zip(partials, got, ("w_in", "w_bh", "w_ba", "w_out"))]
    (rin, rbh, rba, rout), rsm = _exchange_chips(pair_sums, small)
    g_in, d_in, nm_in, nv_in = _adamw_shard(rin, w_in[0], m_w_in[0], v_w_in[0], 128, "adamw_w_in")
    g_bh, d_bh, nm_bh, nv_bh = _adamw_shard(rbh, w_branch_hgrn[0], m_w_branch_hgrn[0], v_w_branch_hgrn[0], 128, "adamw_w_bh")
    g_ba, d_ba, nm_ba, nv_ba = _adamw_shard(rba, w_branch_attn[0], m_w_branch_attn[0], v_w_branch_attn[0], 128, "adamw_w_ba")
    g_out, d_out, nm_out, nv_out = _adamw_shard(rout, w_out[0], m_w_out[0], v_w_out[0], 128, "adamw_w_out")
    sg, sd, sm, sv = _adamw_small(
        rsm,
        (norm_w, hgrn_lower_bound, hgrn_norm_w, fnw, attn_sinks),
        (m_norm_w, m_hgrn_lower_bound, m_hgrn_norm_w, m_final_norm_w.reshape(1, D), m_attn_sinks),
        (v_norm_w, v_hgrn_lower_bound, v_hgrn_norm_w, v_final_norm_w.reshape(1, D), v_attn_sinks))

    loss = lax.psum(sums[2, 0], ("x", "y", "c"))

    def group(s, w_in_v, bh, ba, out):
        nw, lb, hnw, fn, sinks = s
        return (nw, w_in_v[None], lb, hnw, bh[None], sinks, ba[None], out[None], fn.reshape(D))

    return (loss, grad_x.reshape(1, t, D),
            *group(sg, g_in, g_bh, g_ba, g_out), *group(sd, d_in, d_bh, d_ba, d_out),
            *group(sm, nm_in, nm_bh, nm_ba, nm_out), *group(sv, nv_in, nv_bh, nv_ba, nv_out))
```

```python
import functools

import jax
import jax.numpy as jnp
from jax import lax
from jax.experimental import pallas as pl
from jax.experimental.pallas import tpu as pltpu
from jax.experimental.pallas import tpu_sc as plsc

F32 = jnp.float32
BF16 = jnp.bfloat16

D = 1024
DIN = 8704
NDEV = 8
IN_SHARD = DIN // NDEV
ROW_SHARD = D // NDEV
HEADS = 8
HD = 128
CH = 64
HGRN_GROUP = 8
QH = 16
AB = 128
EPS = 1e-6
NEG = -1e30
ATT_SCALE = 0.125

C_HGRN = 0
C_HG = 3072
C_AQ = 4096
C_AK = 5120
C_AV = 5376
C_AG = 5632
C_MH = 6656
C_MA = 7680
CB = 512

LR = 0.001
B1 = 0.9
B2 = 0.999
ADAM_EPS = 1e-08
WD = 0.01
STEP = 10

V7X_VMEM_BYTES = 64 * 1024 * 1024
MESH = pl.DeviceIdType.MESH


def _cp(vmem_mb):
    return pltpu.CompilerParams(vmem_limit_bytes=vmem_mb * 1024 * 1024)


def _mm(a, b):
    return jnp.dot(a, b, preferred_element_type=F32)


def _mm_nt(a, b):
    return lax.dot_general(a, b, (((1,), (1,)), ((), ())), preferred_element_type=F32)


def _mm_tn(a, b):
    return lax.dot_general(a, b, (((0,), (0,)), ((), ())), preferred_element_type=F32)


def _mm_exact(a, b):
    return jnp.dot(a, b, preferred_element_type=F32, precision=lax.Precision.HIGHEST)


def _sigmoid(v):
    return 0.5 * jnp.tanh(0.5 * v) + 0.5


def _bf(v):
    return v.astype(BF16)


def _place():
    x, y, c = lax.axis_index("x"), lax.axis_index("y"), lax.axis_index("c")
    return (x, y, c), (x, y, 1 - c), [(1 - x, y), (x, 1 - y), (1 - x, 1 - y)]


def _dev_index(px, py, pc):
    return 4 * px + 2 * py + pc


def _gather_weights(w_in_s, wbh_s, wba_s, wout_s):
    def body(win_ref, wbh_ref, wba_ref, wout_ref, win_g, wbh_g, wba_g, wout_g, sin, s3, send_sems, recv_sems, loc_sems):
        (x, y, c), sibling, chips = _place()
        me = _dev_index(x, y, c)
        sin[...] = win_ref[...].astype(BF16)
        s3[0] = wbh_ref[...].astype(BF16)
        s3[1] = wba_ref[...].astype(BF16)
        s3[2] = wout_ref[...].astype(BF16)
        srcs = [sin, s3.at[0], s3.at[1], s3.at[2]]
        outs = [win_g, wbh_g, wba_g, wout_g]

        def copy(kind, k, block, to, src=None):
            return pltpu.make_async_remote_copy(
                src_ref=srcs[k] if src is None else src, dst_ref=outs[k].at[block], send_sem=send_sems.at[kind, k],
                recv_sem=recv_sems.at[kind, k], device_id=to, device_id_type=MESH)

        local = [pltpu.make_async_copy(srcs[k], outs[k].at[me], loc_sems.at[k]) for k in range(4)]
        for cp in local:
            cp.start()
        first = [copy(0, k, me, sibling) for k in range(4)]
        first += [copy(1 + j, k, me, (*chip, c)) for j, chip in enumerate(chips) for k in range(4)]
        for cp in first:
            cp.start()
        passed = []
        for j, chip in enumerate(chips):
            block = _dev_index(*chip, c)
            for k in range(4):
                copy(1 + j, k, block, (x, y, c)).wait_recv()
            for k in range(4):
                cp = copy(4 + j, k, block, sibling, src=outs[k].at[block])
                cp.start()
                passed.append(cp)
        for k in range(4):
            copy(0, k, _dev_index(x, y, 1 - c), (x, y, c)).wait_recv()
        for j, chip in enumerate(chips):
            for k in range(4):
                copy(4 + j, k, _dev_index(*chip, 1 - c), (x, y, c)).wait_recv()
        for cp in first + passed:
            cp.wait_send()
        for cp in local:
            cp.wait()

    vm = pl.BlockSpec(memory_space=pltpu.VMEM)
    hbm = pl.BlockSpec(memory_space=pl.ANY)
    return pl.pallas_call(
        body,
        out_shape=[jax.ShapeDtypeStruct((NDEV, D, IN_SHARD), BF16)] + [jax.ShapeDtypeStruct((NDEV, ROW_SHARD, D), BF16)] * 3,
        in_specs=[vm, vm, vm, vm],
        out_specs=[hbm, hbm, hbm, hbm],
        scratch_shapes=[pltpu.VMEM((D, IN_SHARD), BF16), pltpu.VMEM((3, ROW_SHARD, D), BF16),
                        pltpu.SemaphoreType.DMA((NDEV - 1, 4)), pltpu.SemaphoreType.DMA((NDEV - 1, 4)),
                        pltpu.SemaphoreType.DMA((4,))],
        name="gather_weights", compiler_params=_cp(32),
    )(w_in_s, wbh_s, wba_s, wout_s)


def _exchange_pair(arrs):
    n = len(arrs)
    ins = [jax.new_ref(a, memory_space=pltpu.MemorySpace.HBM) for a in arrs]
    got = [jax.empty_ref(jax.ShapeDtypeStruct((4,) + a.shape[1:], a.dtype), memory_space=pltpu.MemorySpace.HBM)
           for a in arrs]

    @pl.kernel(mesh=plsc.ScalarSubcoreMesh(axis_name="sequencer", num_cores=1), name="exchange_pair",
               scratch_types=(pltpu.SemaphoreType.DMA((4, n)), pltpu.SemaphoreType.DMA((4, n))),
               compiler_params=pltpu.CompilerParams(collective_id=0))
    def launch(send_sems, recv_sems):
        (x, y, c), sibling, _ = _place()
        barrier = pltpu.get_barrier_semaphore()
        pl.semaphore_signal(barrier, inc=1, device_id=sibling, device_id_type=MESH)
        pl.semaphore_wait(barrier, 1)
        sends = [pltpu.make_async_remote_copy(
            src_ref=ins[k].at[_dev_index(q // 2, q % 2, 1 - c)], dst_ref=got[k].at[q], send_sem=send_sems.at[q, k],
            recv_sem=recv_sems.at[q, k], device_id=sibling, device_id_type=MESH) for q in range(4) for k in range(n)]
        for cp in sends:
            cp.start()
        for cp in sends:
            cp.wait_recv()
        for cp in sends:
            cp.wait_send()

    launch()
    return [g[...] for g in got]


def _pair_sum(full, got, core, rows, name):
    _, nr, nc = got.shape

    def body(core_ref, a_ref, b_ref, o_ref):
        o_ref[...] = (a_ref[...].astype(F32) + b_ref[...].astype(F32)).astype(BF16)

    blk = pl.BlockSpec((1, rows, nc), lambda q, i, core_ref: (q, i, 0))
    return pl.pallas_call(
        body,
        grid_spec=pltpu.PrefetchScalarGridSpec(
            num_scalar_prefetch=1, grid=(4, nr // rows),
            in_specs=[pl.BlockSpec((1, rows, nc), lambda q, i, core_ref: (2 * q + core_ref[0], i, 0)), blk],
            out_specs=blk),
        out_shape=jax.ShapeDtypeStruct(got.shape, BF16), name=name,
    )(core, full, got)


def _exchange_chips(sums, small):
    n = len(sums)
    ins = [jax.new_ref(a, memory_space=pltpu.MemorySpace.HBM) for a in sums]
    sm = jax.new_ref(small, memory_space=pltpu.MemorySpace.HBM)
    outs = [jax.empty_ref(jax.ShapeDtypeStruct((3,) + a.shape[1:], a.dtype), memory_space=pltpu.MemorySpace.HBM)
            for a in sums]
    rsm = jax.empty_ref(jax.ShapeDtypeStruct((NDEV,) + small.shape, F32), memory_space=pltpu.MemorySpace.HBM)

    @pl.kernel(mesh=plsc.ScalarSubcoreMesh(axis_name="sequencer", num_cores=1), name="exchange_chips",
               scratch_types=(pltpu.SemaphoreType.DMA((3, n)), pltpu.SemaphoreType.DMA((3, n)), pltpu.SemaphoreType.DMA,
                              pltpu.SemaphoreType.DMA((NDEV - 1,)), pltpu.SemaphoreType.DMA((NDEV - 1,))),
               compiler_params=pltpu.CompilerParams(collective_id=1))
    def launch(send_sems, recv_sems, loc_sem, ssend, srecv):
        (x, y, c), _, chips = _place()
        me = _dev_index(x, y, c)
        peers = [(1 - x if r & 4 else x, 1 - y if r & 2 else y, 1 - c if r & 1 else c) for r in range(1, NDEV)]
        barrier = pltpu.get_barrier_semaphore()
        for peer in peers:
            pl.semaphore_signal(barrier, inc=1, device_id=peer, device_id_type=MESH)
        pl.semaphore_wait(barrier, NDEV - 1)
        local = pltpu.make_async_copy(sm, rsm.at[me], loc_sem)
        copies = []
        for j, (px, py) in enumerate(chips):
            for k in range(n):
                copies.append(pltpu.make_async_remote_copy(
                    src_ref=ins[k].at[2 * px + py], dst_ref=outs[k].at[j], send_sem=send_sems.at[j, k],
                    recv_sem=recv_sems.at[j, k], device_id=(px, py, c), device_id_type=MESH))
        for r, peer in enumerate(peers):
            copies.append(pltpu.make_async_remote_copy(
                src_ref=sm, dst_ref=rsm.at[me], send_sem=ssend.at[r], recv_sem=srecv.at[r],
                device_id=peer, device_id_type=MESH))
        local.start()
        for cp in copies:
            cp.start()
        for cp in copies[:3 * n]:
            cp.wait_recv()
        for r, peer in enumerate(peers):
            pltpu.make_async_remote_copy(
                src_ref=sm, dst_ref=rsm.at[_dev_index(*peer)], send_sem=ssend.at[r], recv_sem=srecv.at[r],
                device_id=peer, device_id_type=MESH).wait_recv()
        for cp in copies:
            cp.wait_send()
        local.wait()

    launch()
    return [o[...] for o in outs], rsm[...]


def _inproj_fwd(x2d, norm_w, w_p, tb, nb):
    t = x2d.shape[0]

    def body(x_ref, nw_ref, w_ref, proj_ref, xnt_ref, xn_s):
        @pl.when(pl.program_id(1) == 0)
        def _():
            xv = x_ref[...]
            r = lax.rsqrt(jnp.mean(xv * xv, axis=-1, keepdims=True) + EPS)
            xn = (xv * r) * nw_ref[...]
            xn_s[...] = xn.astype(BF16)
            xnt_ref[...] = xn.T.astype(BF16)

        proj_ref[...] = _mm(xn_s[...], w_ref[...])

    return pl.pallas_call(
        body, grid=(t // tb, DIN // nb),
        in_specs=[pl.BlockSpec((tb, D), lambda i, j: (i, 0)), pl.BlockSpec((1, D), lambda i, j: (0, 0)),
                  pl.BlockSpec((D, nb), lambda i, j: (0, j))],
        out_specs=[pl.BlockSpec((tb, nb), lambda i, j: (i, j)), pl.BlockSpec((D, tb), lambda i, j: (0, i))],
        out_shape=[jax.ShapeDtypeStruct((t, DIN), F32), jax.ShapeDtypeStruct((D, t), BF16)],
        scratch_shapes=[pltpu.VMEM((tb, D), BF16)],
        name="inproj_fwd", compiler_params=_cp(56),
    )(x2d, norm_w, w_p)


def _lower_bound(lb_ref):
    a0 = lb_ref[0:1, :]
    a1 = lb_ref[1:2, :]
    mx = jnp.maximum(a0, a1)
    e0 = jnp.exp(a0 - mx)
    e1 = jnp.exp(a1 - mx)
    return e0 / (e0 + e1)


def _hgrn_chunk_fwd(hq, hf, lb, tril):
    sg = _sigmoid(hf)
    f = lb + (1.0 - lb) * sg
    g = jnp.log(f)
    k = 1.0 - f
    sq = _sigmoid(hq)
    q = hq * sq
    b = _mm_exact(tril, g)
    last_row = lax.broadcasted_iota(jnp.int32, b.shape, 0) == CH - 1
    b_last = jnp.sum(jnp.where(last_row, b, 0.0), axis=0, keepdims=True)
    c = 0.5 * b_last
    eb = jnp.exp(b)
    ea = jnp.exp(b - c)
    ek = jnp.exp(c - b)
    ed = jnp.exp(b_last - b)
    ebl = jnp.exp(b_last)
    return dict(sg=sg, f=f, k=k, sq=sq, q=q, eb=eb, ea=ea, ek=ek, ed=ed, ebl=ebl,
                qe=q * eb, qa=q * ea, ka=k * ek, kd=k * ed)


def _tri(lower):
    r = lax.broadcasted_iota(jnp.int32, (CH, CH), 0)
    c = lax.broadcasted_iota(jnp.int32, (CH, CH), 1)
    return (r >= c) if lower else (c >= r)


def _head_segment(p_ref, rows, j, hg):
    return p_ref[rows, j * HD * hg:(j + 1) * HD * hg]


def _head(a, k):
    return a[:, k * HD:(k + 1) * HD]


def _hgrn_fwd(proj, lbw, rb, hg):
    assert hg == HEADS
    t = proj.shape[0]
    ncb = rb // CH

    def body(p_ref, lb_ref, o_ref, st_ref, s_scr):
        @pl.when(pl.program_id(1) == 0)
        def _():
            s_scr[...] = jnp.zeros_like(s_scr)

        lb = _lower_bound(lb_ref)
        causal = _tri(True)
        tril = causal.astype(F32)
        heads = range(hg)

        def chunk(cc, carry):
            r0 = pl.multiple_of(cc * CH, CH)
            rows = pl.ds(r0, CH)
            e = _hgrn_chunk_fwd(_head_segment(p_ref, rows, 0, hg), _head_segment(p_ref, rows, 1, hg), lb, tril)
            v = _bf(_head_segment(p_ref, rows, 2, hg))
            sts = [s_scr[k] for k in heads]
            qa, ka, qe, kd = _bf(e["qa"]), _bf(e["ka"]), _bf(e["qe"]), _bf(e["kd"])
            a = [_bf(jnp.where(causal, _mm_nt(_head(qa, k), _head(ka, k)), 0.0)) for k in heads]
            o_inter = [_mm_nt(_head(qe, k), _bf(sts[k])) for k in heads]
            kv = [_mm_tn(_head(v, k), _head(kd, k)) for k in heads]
            o_intra = [_mm(a[k], _head(v, k)) for k in heads]
            for k in heads:
                st_ref[cc, k] = sts[k]
                o_ref[rows, k * HD:(k + 1) * HD] = o_inter[k] + o_intra[k]
                s_scr[k] = sts[k] * _head(e["ebl"], k) + kv[k]
            return carry

        lax.fori_loop(0, ncb, chunk, 0)

    return pl.pallas_call(
        body, grid=(HEADS // hg, t // rb),
        in_specs=[pl.BlockSpec((rb, 3 * HD * hg), lambda h, i: (i, h)), pl.BlockSpec((2, HD * hg), lambda h, i: (0, h))],
        out_specs=[pl.BlockSpec((rb, HD * hg), lambda h, i: (i, h)),
                   pl.BlockSpec((ncb, hg, HD, HD), lambda h, i: (i, h, 0, 0))],
        out_shape=[jax.ShapeDtypeStruct((t, D), F32), jax.ShapeDtypeStruct((t // CH, HEADS, HD, HD), F32)],
        scratch_shapes=[pltpu.VMEM((hg, HD, HD), F32)],
        name="hgrn_fwd", compiler_params=_cp(48),
    )(proj, lbw)


def _hgrn_bwd(proj, lbw, do_raw, states, rb, hg):
    assert hg == HEADS
    t = proj.shape[0]
    nblk = t // rb
    ncb = rb // CH
    wd = HD * hg

    def body(p_ref, lb_ref, do_ref, st_ref, dp_ref, dlb_ref, ds_scr):
        @pl.when(pl.program_id(1) == 0)
        def _():
            ds_scr[...] = jnp.zeros_like(ds_scr)
            dlb_ref[...] = jnp.zeros_like(dlb_ref)

        lb = _lower_bound(lb_ref)
        causal = _tri(True)
        tril = causal.astype(F32)
        triu = _tri(False).astype(F32)
        last_row = lax.broadcasted_iota(jnp.int32, (CH, HD * hg), 0) == CH - 1
        row0 = lax.broadcasted_iota(jnp.int32, (8, HD * hg), 0) == 0
        heads = range(hg)
        wide = lambda parts: jnp.concatenate(parts, axis=1)

        def chunk(it, carry):
            cc = ncb - 1 - it
            r0 = pl.multiple_of(cc * CH, CH)
            rows = pl.ds(r0, CH)
            hq = _head_segment(p_ref, rows, 0, hg)
            e = _hgrn_chunk_fwd(hq, _head_segment(p_ref, rows, 1, hg), lb, tril)
            v = _bf(_head_segment(p_ref, rows, 2, hg))
            do = _bf(do_ref[rows, :])
            sts = [st_ref[cc, k] for k in heads]
            dsts = [ds_scr[k] for k in heads]
            dlb_acc = dlb_ref[...]
            qa, ka, qe, kd = _bf(e["qa"]), _bf(e["ka"]), _bf(e["qe"]), _bf(e["kd"])
            a = [_bf(jnp.where(causal, _mm_nt(_head(qa, k), _head(ka, k)), 0.0)) for k in heads]
            da = [_bf(jnp.where(causal, _mm_nt(_head(do, k), _head(v, k)), 0.0)) for k in heads]
            dqe = wide([_mm(_head(do, k), _bf(sts[k])) for k in heads])
            dkd = wide([_mm(_head(v, k), _bf(dsts[k])) for k in heads])
            dv_state = [_mm_nt(_head(kd, k), _bf(dsts[k])) for k in heads]
            ds_new = [_mm_tn(_head(do, k), _head(qe, k)) for k in heads]
            dv_intra = [_mm_tn(a[k], _head(do, k)) for k in heads]
            dqa = wide([_mm(da[k], _head(ka, k)) for k in heads])
            dka = wide([_mm_tn(da[k], _head(qa, k)) for k in heads])
            dv = wide([dv_intra[k] + dv_state[k] for k in heads])
            dbl = e["ebl"] * wide([jnp.sum(sts[k] * dsts[k], axis=0, keepdims=True) for k in heads])
            dq = dqe * e["eb"] + dqa * e["ea"]
            dk = dka * e["ek"] + dkd * e["ed"]
            dkd_kd = dkd * kd.astype(F32)
            db = dqe * qe.astype(F32) + dqa * qa.astype(F32) - dka * ka.astype(F32) - dkd_kd
            db = db + jnp.where(last_row, dbl + jnp.sum(dkd_kd, axis=0, keepdims=True), 0.0)
            dg = _mm_exact(triu, db)
            df = dg / e["f"] - dk
            sg = e["sg"]
            sq = e["sq"]
            dhq = _bf(dq * (sq * (1.0 + hq * (1.0 - sq))))
            dhf = _bf(df * (1.0 - lb) * sg * (1.0 - sg))
            dhi = _bf(dv)
            dlb_new = dlb_acc + jnp.where(row0, jnp.sum(df * (1.0 - sg), axis=0, keepdims=True), 0.0)
            for k in heads:
                ds_scr[k] = ds_new[k] + dsts[k] * _head(e["ebl"], k)
            dp_ref[rows, 0:wd] = dhq
            dp_ref[rows, wd:2 * wd] = dhf
            dp_ref[rows, 2 * wd:3 * wd] = dhi
            dlb_ref[...] = dlb_new
            return carry

        lax.fori_loop(0, ncb, chunk, 0)

    rev = lambda h, i: (nblk - 1 - i, h)
    return pl.pallas_call(
        body, grid=(HEADS // hg, nblk),
        in_specs=[pl.BlockSpec((rb, 3 * HD * hg), rev), pl.BlockSpec((2, HD * hg), lambda h, i: (0, h)),
                  pl.BlockSpec((rb, HD * hg), rev), pl.BlockSpec((ncb, hg, HD, HD), lambda h, i: (nblk - 1 - i, h, 0, 0))],
        out_specs=[pl.BlockSpec((rb, 3 * HD * hg), rev), pl.BlockSpec((8, HD * hg), lambda h, i: (0, h))],
        out_shape=[jax.ShapeDtypeStruct((t, 3 * D), BF16), jax.ShapeDtypeStruct((8, D), F32)],
        scratch_shapes=[pltpu.VMEM((hg, HD, HD), F32)],
        name="hgrn_bwd", compiler_params=_cp(48),
    )(proj, lbw, do_raw, states)


def _kv_variants(tile, odd):
    low = lax.broadcasted_iota(jnp.int32, tile.shape, 1) < 64
    if odd:
        hi = jnp.where(low, 0.0, tile)
        lo = pltpu.roll(hi, 64, 1)
    else:
        lo = jnp.where(low, tile, 0.0)
        hi = pltpu.roll(lo, 64, 1)
    return _bf(lo), _bf(hi)


def _attn_masks(n):
    qi = lax.broadcasted_iota(jnp.int32, (AB, AB), 0)
    kj = lax.broadcasted_iota(jnp.int32, (AB, AB), 1)
    cur = kj <= qi
    return cur, cur | (n > 0), qi <= kj


def _kv_all(prev_ref, cur_ref):
    out = []
    for tl in range(2):
        cols = slice(tl * 128, (tl + 1) * 128)
        tile = jnp.concatenate([prev_ref[:, cols], cur_ref[:, cols]], axis=0)
        out.append(_kv_variants(tile, 0))
        out.append(_kv_variants(tile, 1))
    return out


def _attn_softmax(s2, sink, cur, ok):
    s = jnp.where(ok, jnp.where(cur, s2[:, AB:], s2[:, :AB]) * ATT_SCALE, NEG)
    m = jnp.maximum(jnp.max(s, axis=-1, keepdims=True), sink)
    p = jnp.exp(s - m)
    es = jnp.exp(sink - m)
    inv = 1.0 / (jnp.sum(p, axis=-1, keepdims=True) + es)
    return p * inv, es * inv


def _spread(pc, cur):
    return jnp.concatenate([jnp.where(cur, 0.0, pc), jnp.where(cur, pc, 0.0)], axis=1)


def _spread_t(pct, cur_t):
    return jnp.concatenate([jnp.where(cur_t, 0.0, pct), jnp.where(cur_t, pct, 0.0)], axis=0)


def _attn_fwd(proj, sinks):
    t = proj.shape[0]
    nb = t // AB

    def body(q_ref, kc_ref, kp_ref, vc_ref, vp_ref, sink_ref, o_ref):
        cur, ok, _ = _attn_masks(pl.program_id(0))
        kvars = _kv_all(kp_ref, kc_ref)
        vvars = _kv_all(vp_ref, vc_ref)
        qps = [_bf(q_ref[:, 128 * j:128 * (j + 1)]) for j in range(8)]
        scores = [[_mm_nt(qps[j], kvars[j // 2][ab]) for ab in range(2)] for j in range(8)]
        for j in range(8):
            parts = []
            for ab in range(2):
                pc, _ = _attn_softmax(scores[j][ab], sink_ref[0, 2 * j + ab], cur, ok)
                parts.append(_mm(_bf(_spread(pc, cur)), vvars[j // 2][ab]))
            o_ref[:, 128 * j:128 * (j + 1)] = parts[0] + parts[1]

    prev = lambda n: jnp.maximum(n - 1, 0)
    return pl.pallas_call(
        body, grid=(nb,),
        in_specs=[pl.BlockSpec((AB, D), lambda n: (n, C_AQ // D)),
                  pl.BlockSpec((AB, 256), lambda n: (n, C_AK // 256)),
                  pl.BlockSpec((AB, 256), lambda n: (prev(n), C_AK // 256)),
                  pl.BlockSpec((AB, 256), lambda n: (n, C_AV // 256)),
                  pl.BlockSpec((AB, 256), lambda n: (prev(n), C_AV // 256)),
                  pl.BlockSpec(memory_space=pltpu.SMEM)],
        out_specs=pl.BlockSpec((AB, D), lambda n: (n, 0)),
        out_shape=jax.ShapeDtypeStruct((t, D), F32),
        name="attn_fwd", compiler_params=_cp(32),
    )(proj, proj, proj, proj, proj, sinks)


def _attn_bwd(proj, sinks, do_a):
    t = proj.shape[0]
    nb = t // AB

    def body(q_ref, kc_ref, kp_ref, vc_ref, vp_ref, do_ref, sink_ref, dq_ref, dkv_ref, dsink_ref, carry):
        n = pl.program_id(0)

        @pl.when(n == 0)
        def _():
            dsink_ref[...] = jnp.zeros_like(dsink_ref)
            carry[...] = jnp.zeros_like(carry)

        @pl.when(n < nb)
        def _():
            cur, ok, cur_t = _attn_masks(n)
            low = lax.broadcasted_iota(jnp.int32, (2 * AB, 128), 1) < 64
            lane = lax.broadcasted_iota(jnp.int32, (8, 128), 1)
            row0 = lax.broadcasted_iota(jnp.int32, (8, 128), 0) == 0
            kvars = _kv_all(kp_ref, kc_ref)
            vvars = _kv_all(vp_ref, vc_ref)
            qps = [_bf(q_ref[:, 128 * j:128 * (j + 1)]) for j in range(8)]
            dops = [_bf(do_ref[:, 128 * j:128 * (j + 1)]) for j in range(8)]
            scores = [[_mm_nt(qps[j], kvars[j // 2][ab]) for ab in range(2)] for j in range(8)]
            dps = [[_mm_nt(dops[j], vvars[j // 2][ab]) for ab in range(2)] for j in range(8)]
            dsink = jnp.zeros((8, 128), F32)
            dk_ab = [[None, None] for _ in range(4)]
            dv_ab = [[None, None] for _ in range(4)]
            for j in range(8):
                g = j // 2
                dqp = None
                for ab in range(2):
                    head = 2 * j + ab
                    pc, ps = _attn_softmax(scores[j][ab], sink_ref[0, head], cur, ok)
                    dpc = jnp.where(cur, dps[j][ab][:, AB:], dps[j][ab][:, :AB])
                    rs = jnp.sum(pc * dpc, axis=-1, keepdims=True)
                    dsc = pc * (dpc - rs)
                    dsink = dsink + jnp.where(row0 & (lane == head), -jnp.sum(ps * rs), 0.0)
                    term = _mm(_bf(_spread(dsc, cur)), kvars[g][ab])
                    dqp = term if dqp is None else dqp + term
                    dk_t = _mm(_bf(_spread_t(dsc.T, cur_t)), qps[j])
                    dv_t = _mm(_bf(_spread_t(pc.T, cur_t)), dops[j])
                    dk_ab[g][ab] = dk_t if dk_ab[g][ab] is None else dk_ab[g][ab] + dk_t
                    dv_ab[g][ab] = dv_t if dv_ab[g][ab] is None else dv_ab[g][ab] + dv_t
                dq_ref[:, 128 * j:128 * (j + 1)] = _bf(dqp * ATT_SCALE)
            for tl in range(2):
                ke, ko = dk_ab[2 * tl], dk_ab[2 * tl + 1]
                ve, vo = dv_ab[2 * tl], dv_ab[2 * tl + 1]
                dkt = (jnp.where(low, ke[0], 0.0) + pltpu.roll(jnp.where(low, 0.0, ke[1]), 64, 1)
                       + jnp.where(low, 0.0, ko[1]) + pltpu.roll(jnp.where(low, ko[0], 0.0), 64, 1)) * ATT_SCALE
                dvt = (jnp.where(low, ve[0], 0.0) + pltpu.roll(jnp.where(low, 0.0, ve[1]), 64, 1)
                       + jnp.where(low, 0.0, vo[1]) + pltpu.roll(jnp.where(low, vo[0], 0.0), 64, 1))
                kcols = slice(tl * 128, (tl + 1) * 128)
                vcols = slice(256 + tl * 128, 256 + (tl + 1) * 128)
                dkv_ref[:, kcols] = _bf(carry[:, kcols] + dkt[0:AB])
                dkv_ref[:, vcols] = _bf(carry[:, vcols] + dvt[0:AB])
                carry[:, kcols] = dkt[AB:2 * AB]
                carry[:, vcols] = dvt[AB:2 * AB]
            dsink_ref[...] += dsink

        @pl.when(n == nb)
        def _():
            dkv_ref[...] = _bf(carry[...])

    cur = lambda n: jnp.minimum(n, nb - 1)
    prev = lambda n: jnp.clip(n - 1, 0, nb - 1)
    return pl.pallas_call(
        body, grid=(nb + 1,),
        in_specs=[pl.BlockSpec((AB, D), lambda n: (cur(n), C_AQ // D)),
                  pl.BlockSpec((AB, 256), lambda n: (cur(n), C_AK // 256)),
                  pl.BlockSpec((AB, 256), lambda n: (prev(n), C_AK // 256)),
                  pl.BlockSpec((AB, 256), lambda n: (cur(n), C_AV // 256)),
                  pl.BlockSpec((AB, 256), lambda n: (prev(n), C_AV // 256)),
                  pl.BlockSpec((AB, D), lambda n: (cur(n), 0)),
                  pl.BlockSpec(memory_space=pltpu.SMEM)],
        out_specs=[pl.BlockSpec((AB, D), lambda n: (cur(n), 0)),
                   pl.BlockSpec((AB, 512), lambda n: (prev(n), 0)),
                   pl.BlockSpec((8, 128), lambda n: (0, 0))],
        out_shape=[jax.ShapeDtypeStruct((t, D), BF16), jax.ShapeDtypeStruct((t, 512), BF16),
                   jax.ShapeDtypeStruct((8, 128), F32)],
        scratch_shapes=[pltpu.VMEM((AB, 512), F32)],
        name="attn_bwd", compiler_params=_cp(32),
    )(proj, proj, proj, proj, proj, do_a, sinks)


def _silu_and_grad(v):
    s = _sigmoid(v)
    return v * s, s * (1.0 + v * (1.0 - s))


def _tail(o_raw, o_a, proj, x2d, tgt, wbh, wba, wout, hnw, fnw, tb):
    t = x2d.shape[0]

    def body(or_ref, oa_ref, hg_ref, ag0, ag1, mh0, mh1, ma0, ma1, x_ref, t_ref, wbh_ref, wba_ref, wout_ref, hnw_ref,
             fnw_ref, dx2_ref, dor_ref, doa_ref, dhg_ref, dagm_ref, gh_ref, ga_ref, mg_ref, dyh_ref, dya_ref, dx2b_ref,
             sums_ref):
        @pl.when(pl.program_id(0) == 0)
        def _():
            sums_ref[...] = jnp.zeros_like(sums_ref)

        halves = lambda a, b: jnp.concatenate([a[...], b[...]], axis=1)
        hnw_v = hnw_ref[...]
        fnw_v = fnw_ref[...]
        o = or_ref[...]
        rs, xhs = [], []
        for h in range(HEADS):
            oh = o[:, h * HD:(h + 1) * HD]
            r = lax.rsqrt(jnp.mean(oh * oh, axis=-1, keepdims=True) + EPS)
            rs.append(r)
            xhs.append(oh * r)
        xh = jnp.concatenate(xhs, axis=1)
        on = xh * hnw_v
        sil_hg, dsil_hg = _silu_and_grad(hg_ref[...])
        gh_b = _bf(on * sil_hg)
        y_h = _mm(gh_b, wbh_ref[...])
        oa = oa_ref[...]
        sil_ag, dsil_ag = _silu_and_grad(halves(ag0, ag1))
        ga_b = _bf(oa * sil_ag)
        y_a = _mm(ga_b, wba_ref[...])
        s_mh = _sigmoid(halves(mh0, mh1))
        s_ma = _sigmoid(halves(ma0, ma1))
        mg_b = _bf(s_mh * y_h + s_ma * y_a)
        x2 = x_ref[...] + _mm(mg_b, wout_ref[...])
        r2 = lax.rsqrt(jnp.mean(x2 * x2, axis=-1, keepdims=True) + EPS)
        xh2 = x2 * r2
        err = xh2 * fnw_v - t_ref[...]
        loss = 0.5 * jnp.sum(jnp.mean(err * err, axis=-1, keepdims=True))
        dy = err * (1.0 / D)
        dfnw = jnp.sum(dy * xh2, axis=0, keepdims=True)
        dxh2 = dy * fnw_v
        dx2 = r2 * (dxh2 - xh2 * jnp.mean(dxh2 * xh2, axis=-1, keepdims=True))
        dx2_ref[...] = dx2
        dx2_b = _bf(dx2)
        dmg = _mm_nt(dx2_b, wout_ref[...])
        dmg_h = dmg * s_mh
        dmg_a = dmg * s_ma
        dyh_b = _bf(dmg_h)
        dya_b = _bf(dmg_a)
        dagm_ref[:, D:2 * D] = _bf(dmg_h * y_h * (1.0 - s_mh))
        dagm_ref[:, 2 * D:3 * D] = _bf(dmg_a * y_a * (1.0 - s_ma))
        dgh = _mm_nt(dyh_b, wbh_ref[...])
        dga = _mm_nt(dya_b, wba_ref[...])
        doa_ref[...] = dga * sil_ag
        dagm_ref[:, 0:D] = _bf(dga * oa * dsil_ag)
        dhg_ref[...] = _bf(dgh * on * dsil_hg)
        don = dgh * sil_hg
        dhnw = jnp.sum(don * xh, axis=0, keepdims=True)
        dxh = don * hnw_v
        dos = []
        for h in range(HEADS):
            sl = slice(h * HD, (h + 1) * HD)
            dos.append(rs[h] * (dxh[:, sl] - xhs[h] * jnp.mean(dxh[:, sl] * xhs[h], axis=-1, keepdims=True)))
        dor_ref[...] = jnp.concatenate(dos, axis=1)
        gh_ref[...] = gh_b
        ga_ref[...] = ga_b
        mg_ref[...] = mg_b
        dyh_ref[...] = dyh_b
        dya_ref[...] = dya_b
        dx2b_ref[...] = dx2_b
        row = lax.broadcasted_iota(jnp.int32, (8, D), 0)
        sums_ref[...] += jnp.where(row == 0, dfnw, 0.0) + jnp.where(row == 1, dhnw, 0.0) + jnp.where(row == 2, loss, 0.0)

    rowblk = lambda c: pl.BlockSpec((tb, D), lambda i: (i, c))
    half = lambda c: pl.BlockSpec((tb, 512), lambda i: (i, c))
    full = lambda shape: pl.BlockSpec(shape, lambda i: (0, 0))
    return pl.pallas_call(
        body, grid=(t // tb,),
        in_specs=[rowblk(0), rowblk(0), rowblk(C_HG // D), half(C_AG // 512), half(C_AG // 512 + 1), half(C_MH // 512),
                  half(C_MH // 512 + 1), half(C_MA // 512), half(C_MA // 512 + 1), rowblk(0), rowblk(0),
                  full((D, D)), full((D, D)), full((D, D)), full((1, D)), full((1, D))],
        out_specs=[rowblk(0), rowblk(0), rowblk(0), rowblk(0), pl.BlockSpec((tb, 3 * D), lambda i: (i, 0))]
        + [rowblk(0)] * 6 + [full((8, D))],
        out_shape=[jax.ShapeDtypeStruct((t, D), F32)] * 3
        + [jax.ShapeDtypeStruct((t, D), BF16), jax.ShapeDtypeStruct((t, 3 * D), BF16)]
        + [jax.ShapeDtypeStruct((t, D), BF16)] * 6 + [jax.ShapeDtypeStruct((8, D), F32)],
        name="tail", compiler_params=_cp(56),
    )(o_raw, o_a, proj, proj, proj, proj, proj, proj, proj, x2d, tgt, wbh, wba, wout, hnw, fnw)


def _wgrad3(gh, dyh, ga, dya, mg, dx2b, tk):
    t = dyh.shape[0]

    def body(a0, b0, a1, b1, a2, b2, o0, o1, o2):
        @pl.when(pl.program_id(0) == 0)
        def _():
            o0[...] = jnp.zeros_like(o0)
            o1[...] = jnp.zeros_like(o1)
            o2[...] = jnp.zeros_like(o2)

        o0[...] += _mm_tn(a0[...], b0[...])
        o1[...] += _mm_tn(a1[...], b1[...])
        o2[...] += _mm_tn(a2[...], b2[...])

    blk = pl.BlockSpec((tk, D), lambda k: (k, 0))
    out = pl.BlockSpec((D, D), lambda k: (0, 0))
    return pl.pallas_call(
        body, grid=(t // tk,), in_specs=[blk] * 6, out_specs=[out] * 3,
        out_shape=[jax.ShapeDtypeStruct((D, D), F32)] * 3,
        name="wgrad3", compiler_params=_cp(48),
    )(gh, dyh, ga, dya, mg, dx2b)


def _inproj_wgrad_piece(xnt, piece, nb, name):
    t = xnt.shape[1]
    width = piece.shape[1]

    def body(xnt_ref, p_ref, o_ref):
        o_ref[...] = _mm(xnt_ref[...], p_ref[...])

    return pl.pallas_call(
        body, grid=(width // nb,),
        in_specs=[pl.BlockSpec((D, t), lambda j: (0, 0), pipeline_mode=pl.Buffered(1)),
                  pl.BlockSpec((t, nb), lambda j: (0, j))],
        out_specs=pl.BlockSpec((D, nb), lambda j: (0, j)),
        out_shape=jax.ShapeDtypeStruct((D, width), F32),
        name=name, compiler_params=_cp(56),
    )(xnt, piece)


def _inproj_dgrad(pieces, w_p, x2d, dx2, norm_w, tb):
    t = x2d.shape[0]

    def body(*refs):
        piece_refs = refs[:len(pieces)]
        w_ref, x_ref, dx2_ref, nw_ref, gx_ref, dnw_ref = refs[len(pieces):]

        @pl.when(pl.program_id(0) == 0)
        def _():
            dnw_ref[...] = jnp.zeros_like(dnw_ref)

        dxn = None
        off = 0
        for p in piece_refs:
            width = p.shape[1]
            term = _mm_nt(p[...], w_ref[:, off:off + width])
            dxn = term if dxn is None else dxn + term
            off += width
        xv = x_ref[...]
        r = lax.rsqrt(jnp.mean(xv * xv, axis=-1, keepdims=True) + EPS)
        xh = xv * r
        dxh = dxn * nw_ref[...]
        gx_ref[...] = dx2_ref[...] + r * (dxh - xh * jnp.mean(dxh * xh, axis=-1, keepdims=True))
        row0 = lax.broadcasted_iota(jnp.int32, (8, D), 0) == 0
        dnw_ref[...] += jnp.where(row0, jnp.sum(dxn * xh, axis=0, keepdims=True), 0.0)

    rowblk = pl.BlockSpec((tb, D), lambda i: (i, 0))
    return pl.pallas_call(
        body, grid=(t // tb,),
        in_specs=[pl.BlockSpec((tb, p.shape[1]), lambda i: (i, 0)) for p in pieces]
        + [pl.BlockSpec((D, DIN), lambda i: (0, 0), pipeline_mode=pl.Buffered(1)), rowblk, rowblk,
           pl.BlockSpec((1, D), lambda i: (0, 0))],
        out_specs=[rowblk, pl.BlockSpec((8, D), lambda i: (0, 0))],
        out_shape=[jax.ShapeDtypeStruct((t, D), F32), jax.ShapeDtypeStruct((8, D), F32)],
        name="inproj_dgrad", compiler_params=_cp(60),
    )(*pieces, w_p, x2d, dx2, norm_w)


def _adamw_math(w, g, m, v):
    m = B1 * m + (1.0 - B1) * g
    v = B2 * v + (1.0 - B2) * (g * g)
    m_hat = m / (1.0 - B1 ** STEP)
    v_hat = v / (1.0 - B2 ** STEP)
    delta = -LR * (m_hat / (jnp.sqrt(v_hat) + ADAM_EPS) + WD * w)
    return delta, m, v


def _adamw_shard(recv, sums, chip, w, m, v, rows, name):
    nparts, nr, nc = recv.shape

    def body(chip_ref, own_ref, p_ref, w_ref, m_ref, v_ref, g_ref, d_ref, nm_ref, nv_ref):
        g = own_ref[0].astype(F32)
        for s in range(nparts):
            g = g + p_ref[s].astype(F32)
        d, nm, nv = _adamw_math(w_ref[...], g, m_ref[...], v_ref[...])
        g_ref[...] = g
        d_ref[...] = d
        nm_ref[...] = nm
        nv_ref[...] = nv

    blk = pl.BlockSpec((rows, nc), lambda i, chip_ref: (i, 0))
    return pl.pallas_call(
        body,
        grid_spec=pltpu.PrefetchScalarGridSpec(
            num_scalar_prefetch=1, grid=(nr // rows,),
            in_specs=[pl.BlockSpec((1, rows, nc), lambda i, chip_ref: (chip_ref[0], i, 0)),
                      pl.BlockSpec((nparts, rows, nc), lambda i, chip_ref: (0, i, 0)), blk, blk, blk],
            out_specs=[blk] * 4),
        out_shape=[jax.ShapeDtypeStruct((nr, nc), F32)] * 4,
        name=name, compiler_params=_cp(48),
    )(chip, sums, recv, w, m, v)


SMALL_ROWS = dict(norm_w=0, lower_bound=1, hgrn_norm_w=3, final_norm_w=4, sinks=5)


def _pack_small_grads(dnw, dlb, sums, dsink):
    def body(dnw_ref, dlb_ref, sums_ref, dsink_ref, o_ref):
        o_ref[...] = jnp.zeros_like(o_ref)
        o_ref[0:1, :] = dnw_ref[0:1, :]
        o_ref[1:2, :] = dlb_ref[0:1, :]
        o_ref[3:4, :] = sums_ref[1:2, :]
        o_ref[4:5, :] = sums_ref[0:1, :]
        o_ref[5:6, 0:128] = dsink_ref[0:1, :]

    return pl.pallas_call(body, out_shape=jax.ShapeDtypeStruct((8, D), F32), name="pack_small_grads")(dnw, dlb, sums, dsink)


def _adamw_small(parts, ws, ms, vs):
    shapes = [a.shape for a in ws]

    def body(p_ref, *refs):
        w, m, v = refs[0:5], refs[5:10], refs[10:15]
        outs = [refs[15 + 5 * i:20 + 5 * i] for i in range(4)]

        def total(row, width):
            g = p_ref[0, row:row + 1, 0:width]
            for s in range(1, NDEV):
                g = g + p_ref[s, row:row + 1, 0:width]
            return g

        lb = _lower_bound(w[1])
        ga0 = total(1, D) * lb * (1.0 - lb)
        grads = [total(0, D), None, total(3, D), total(4, D), total(5, QH)]
        for i in (0, 2, 3, 4):
            res = (grads[i],) + _adamw_math(w[i][...], grads[i], m[i][...], v[i][...])
            for o, val in zip(outs, res):
                o[i][...] = val
        for r, g in ((0, ga0), (1, -ga0)):
            res = (g,) + _adamw_math(w[1][r:r + 1, :], g, m[1][r:r + 1, :], v[1][r:r + 1, :])
            for o, val in zip(outs, res):
                o[1][r:r + 1, :] = val

    res = pl.pallas_call(
        body, out_shape=[jax.ShapeDtypeStruct(s, F32) for s in shapes] * 4, name="adamw_small",
    )(parts, *ws, *ms, *vs)
    return [res[5 * i:5 * i + 5] for i in range(4)]


def kernel(x, norm_w, w_in, hgrn_lower_bound, hgrn_norm_w, w_branch_hgrn, attn_sinks, w_branch_attn, w_out, final_norm_w, loss_target, m_norm_w, m_w_in, m_hgrn_lower_bound, m_hgrn_norm_w, m_w_branch_hgrn, m_attn_sinks, m_w_branch_attn, m_w_out, m_final_norm_w, v_norm_w, v_w_in, v_hgrn_lower_bound, v_hgrn_norm_w, v_w_branch_hgrn, v_attn_sinks, v_w_branch_attn, v_w_out, v_final_norm_w):
    t = x.shape[1]
    x2d = x.reshape(t, D)
    tgt = loss_target.reshape(t, D)
    fnw = final_norm_w.reshape(1, D)
    row_blk = min(256, t)
    big_blk = min(512, t)

    win_g, wbh_g, wba_g, wout_g = _gather_weights(w_in[0], w_branch_hgrn[0], w_branch_attn[0], w_out[0])
    w_p = win_g.transpose(1, 0, 2).reshape(D, DIN)
    wbh, wba, wout = wbh_g.reshape(D, D), wba_g.reshape(D, D), wout_g.reshape(D, D)

    proj, xnt = _inproj_fwd(x2d, norm_w, w_p, min(1024, t), 2176)
    o_raw, states = _hgrn_fwd(proj, hgrn_lower_bound, big_blk, HGRN_GROUP)
    o_a = _attn_fwd(proj, attn_sinks)
    (dx2, do_raw, do_a, d_hg, d_agm, gh, ga, mg, dyh, dya, dx2b, sums) = _tail(
        o_raw, o_a, proj, x2d, tgt, wbh, wba, wout, hgrn_norm_w, fnw, row_blk)
    dwbh, dwba, dwout = _wgrad3(gh, dyh, ga, dya, mg, dx2b, big_blk)
    d_aq, d_kv, dsink = _attn_bwd(proj, attn_sinks, do_a)
    d_hgrn, dlb = _hgrn_bwd(proj, hgrn_lower_bound, do_raw, states, big_blk, HGRN_GROUP)
    pieces = (d_hgrn, d_hg, d_aq, d_kv, d_agm)
    dw_pieces = [_inproj_wgrad_piece(xnt, p, CB, "inproj_wgrad_" + n)
                 for p, n in zip(pieces, ("hgrn", "hgate", "aq", "kv", "gates"))]
    grad_x, dnw = _inproj_dgrad(pieces, w_p, x2d, dx2, norm_w, big_blk)

    dwin_r = jnp.concatenate(dw_pieces, axis=1).reshape(D, NDEV, IN_SHARD).transpose(1, 0, 2).astype(BF16)
    small = _pack_small_grads(dnw, dlb, sums, dsink)
    slots = lambda a: a.reshape(NDEV, ROW_SHARD, D).astype(BF16)
    partials = [dwin_r, slots(dwbh), slots(dwba), slots(dwout)]
    got = _exchange_pair(partials)
    core = lax.axis_index("c").astype(jnp.int32).reshape(1)
    pair_sums = [_pair_sum(a, g, core, ROW_SHARD, "pair_sum_" + n)
                 for a, g, n in zip(partials, got, ("w_in", "w_bh", "w_ba", "w_out"))]
    (rin, rbh, rba, rout), rsm = _exchange_chips(pair_sums, small)
    chip = (2 * lax.axis_index("x") + lax.axis_index("y")).astype(jnp.int32).reshape(1)
    s_in, s_bh, s_ba, s_out = pair_sums
    g_in, d_in, nm_in, nv_in = _adamw_shard(rin, s_in, chip, w_in[0], m_w_in[0], v_w_in[0], 128, "adamw_w_in")
    g_bh, d_bh, nm_bh, nv_bh = _adamw_shard(
        rbh, s_bh, chip, w_branch_hgrn[0], m_w_branch_hgrn[0], v_w_branch_hgrn[0], 128, "adamw_w_bh")
    g_ba, d_ba, nm_ba, nv_ba = _adamw_shard(
        rba, s_ba, chip, w_branch_attn[0], m_w_branch_attn[0], v_w_branch_attn[0], 128, "adamw_w_ba")
    g_out, d_out, nm_out, nv_out = _adamw_shard(rout, s_out, chip, w_out[0], m_w_out[0], v_w_out[0], 128, "adamw_w_out")
    sg, sd, sm, sv = _adamw_small(
        rsm,
        (norm_w, hgrn_lower_bound, hgrn_norm_w, fnw, attn_sinks),
        (m_norm_w, m_hgrn_lower_bound, m_hgrn_norm_w, m_final_norm_w.reshape(1, D), m_attn_sinks),
        (v_norm_w, v_hgrn_lower_bound, v_hgrn_norm_w, v_final_norm_w.reshape(1, D), v_attn_sinks))

    loss = lax.psum(sums[2, 0], ("x", "y", "c"))

    def group(s, w_in_v, bh, ba, out):
        nw, lb, hnw, fn, sinks = s
        return (nw, w_in_v[None], lb, hnw, bh[None], sinks, ba[None], out[None], fn.reshape(D))

    return (loss, grad_x.reshape(1, t, D),
            *group(sg, g_in, g_bh, g_ba, g_out), *group(sd, d_in, d_bh, d_ba, d_out),
            *group(sm, nm_in, nm_bh, nm_ba, nm_out), *group(sv, nv_in, nv_bh, nv_ba, nv_out))
```

```python
import functools

import jax
import jax.numpy as jnp
from jax import lax
from jax.experimental import pallas as pl
from jax.experimental.pallas import tpu as pltpu
from jax.experimental.pallas import tpu_sc as plsc

F32 = jnp.float32
BF16 = jnp.bfloat16

D = 1024
DIN = 8704
NDEV = 8
IN_SHARD = DIN // NDEV
ROW_SHARD = D // NDEV
HEADS = 8
HD = 128
CH = 64
HGRN_GROUP = 8
QH = 16
AB = 128
EPS = 1e-6
NEG = -1e30
ATT_SCALE = 0.125

C_HGRN = 0
C_HG = 3072
C_AQ = 4096
C_AK = 5120
C_AV = 5376
C_AG = 5632
C_MH = 6656
C_MA = 7680
CB = 512

LR = 0.001
B1 = 0.9
B2 = 0.999
ADAM_EPS = 1e-08
WD = 0.01
STEP = 10

V7X_VMEM_BYTES = 64 * 1024 * 1024
MESH = pl.DeviceIdType.MESH


def _cp(vmem_mb):
    return pltpu.CompilerParams(vmem_limit_bytes=vmem_mb * 1024 * 1024)


def _mm(a, b):
    return jnp.dot(a, b, preferred_element_type=F32)


def _mm_nt(a, b):
    return lax.dot_general(a, b, (((1,), (1,)), ((), ())), preferred_element_type=F32)


def _mm_tn(a, b):
    return lax.dot_general(a, b, (((0,), (0,)), ((), ())), preferred_element_type=F32)


def _mm_exact(a, b):
    return jnp.dot(a, b, preferred_element_type=F32, precision=lax.Precision.HIGHEST)


def _sigmoid(v):
    return 0.5 * jnp.tanh(0.5 * v) + 0.5


def _bf(v):
    return v.astype(BF16)


def _place():
    x, y, c = lax.axis_index("x"), lax.axis_index("y"), lax.axis_index("c")
    return (x, y, c), (x, y, 1 - c), [(1 - x, y), (x, 1 - y), (1 - x, 1 - y)]


def _dev_index(px, py, pc):
    return 4 * px + 2 * py + pc


def _gather_weights(w_in_s, wbh_s, wba_s, wout_s):
    def body(win_ref, wbh_ref, wba_ref, wout_ref, win_g, wbh_g, wba_g, wout_g, sin, s3, send_sems, recv_sems, loc_sems):
        (x, y, c), sibling, chips = _place()
        me = _dev_index(x, y, c)
        sin[...] = win_ref[...].astype(BF16)
        s3[0] = wbh_ref[...].astype(BF16)
        s3[1] = wba_ref[...].astype(BF16)
        s3[2] = wout_ref[...].astype(BF16)
        srcs = [sin, s3.at[0], s3.at[1], s3.at[2]]
        outs = [win_g, wbh_g, wba_g, wout_g]

        def copy(kind, k, block, to, src=None):
            return pltpu.make_async_remote_copy(
                src_ref=srcs[k] if src is None else src, dst_ref=outs[k].at[block], send_sem=send_sems.at[kind, k],
                recv_sem=recv_sems.at[kind, k], device_id=to, device_id_type=MESH)

        local = [pltpu.make_async_copy(srcs[k], outs[k].at[me], loc_sems.at[k]) for k in range(4)]
        for cp in local:
            cp.start()
        first = [copy(0, k, me, sibling) for k in range(4)]
        first += [copy(1 + j, k, me, (*chip, c)) for j, chip in enumerate(chips) for k in range(4)]
        for cp in first:
            cp.start()
        passed = []
        for j, chip in enumerate(chips):
            block = _dev_index(*chip, c)
            for k in range(4):
                copy(1 + j, k, block, (x, y, c)).wait_recv()
            for k in range(4):
                cp = copy(4 + j, k, block, sibling, src=outs[k].at[block])
                cp.start()
                passed.append(cp)
        for k in range(4):
            copy(0, k, _dev_index(x, y, 1 - c), (x, y, c)).wait_recv()
        for j, chip in enumerate(chips):
            for k in range(4):
                copy(4 + j, k, _dev_index(*chip, 1 - c), (x, y, c)).wait_recv()
        for cp in first + passed:
            cp.wait_send()
        for cp in local:
            cp.wait()

    vm = pl.BlockSpec(memory_space=pltpu.VMEM)
    hbm = pl.BlockSpec(memory_space=pl.ANY)
    return pl.pallas_call(
        body,
        out_shape=[jax.ShapeDtypeStruct((NDEV, D, IN_SHARD), BF16)] + [jax.ShapeDtypeStruct((NDEV, ROW_SHARD, D), BF16)] * 3,
        in_specs=[vm, vm, vm, vm],
        out_specs=[hbm, hbm, hbm, hbm],
        scratch_shapes=[pltpu.VMEM((D, IN_SHARD), BF16), pltpu.VMEM((3, ROW_SHARD, D), BF16),
                        pltpu.SemaphoreType.DMA((NDEV - 1, 4)), pltpu.SemaphoreType.DMA((NDEV - 1, 4)),
                        pltpu.SemaphoreType.DMA((4,))],
        name="gather_weights", compiler_params=_cp(32),
    )(w_in_s, wbh_s, wba_s, wout_s)


def _exchange_pair(arrs):
    n = len(arrs)
    ins = [jax.new_ref(a, memory_space=pltpu.MemorySpace.HBM) for a in arrs]
    got = [jax.empty_ref(jax.ShapeDtypeStruct((4,) + a.shape[1:], a.dtype), memory_space=pltpu.MemorySpace.HBM)
           for a in arrs]

    @pl.kernel(mesh=plsc.ScalarSubcoreMesh(axis_name="sequencer", num_cores=1), name="exchange_pair",
               scratch_types=(pltpu.SemaphoreType.DMA((4, n)), pltpu.SemaphoreType.DMA((4, n))),
               compiler_params=pltpu.CompilerParams(collective_id=0))
    def launch(send_sems, recv_sems):
        (x, y, c), sibling, _ = _place()
        barrier = pltpu.get_barrier_semaphore()
        pl.semaphore_signal(barrier, inc=1, device_id=sibling, device_id_type=MESH)
        pl.semaphore_wait(barrier, 1)
        sends = [pltpu.make_async_remote_copy(
            src_ref=ins[k].at[_dev_index(q // 2, q % 2, 1 - c)], dst_ref=got[k].at[q], send_sem=send_sems.at[q, k],
            recv_sem=recv_sems.at[q, k], device_id=sibling, device_id_type=MESH) for q in range(4) for k in range(n)]
        for cp in sends:
            cp.start()
        for cp in sends:
            cp.wait_recv()
        for cp in sends:
            cp.wait_send()

    launch()
    return [g[...] for g in got]


def _pair_sum(full, got, core, rows, name):
    _, nr, nc = got.shape

    def body(core_ref, a_ref, b_ref, o_ref):
        o_ref[...] = (a_ref[...].astype(F32) + b_ref[...].astype(F32)).astype(BF16)

    blk = pl.BlockSpec((1, rows, nc), lambda q, i, core_ref: (q, i, 0))
    return pl.pallas_call(
        body,
        grid_spec=pltpu.PrefetchScalarGridSpec(
            num_scalar_prefetch=1, grid=(4, nr // rows),
            in_specs=[pl.BlockSpec((1, rows, nc), lambda q, i, core_ref: (2 * q + core_ref[0], i, 0)), blk],
            out_specs=blk),
        out_shape=jax.ShapeDtypeStruct(got.shape, BF16), name=name,
    )(core, full, got)


def _exchange_chips(sums):
    n = len(sums)
    ins = [jax.new_ref(a, memory_space=pltpu.MemorySpace.HBM) for a in sums]
    outs = [jax.empty_ref(jax.ShapeDtypeStruct((3,) + a.shape[1:], a.dtype), memory_space=pltpu.MemorySpace.HBM)
            for a in sums]

    @pl.kernel(mesh=plsc.ScalarSubcoreMesh(axis_name="sequencer", num_cores=1), name="exchange_chips",
               scratch_types=(pltpu.SemaphoreType.DMA((3, n)), pltpu.SemaphoreType.DMA((3, n))),
               compiler_params=pltpu.CompilerParams(collective_id=1))
    def launch(send_sems, recv_sems):
        (x, y, c), _, chips = _place()
        barrier = pltpu.get_barrier_semaphore()
        for px, py in chips:
            pl.semaphore_signal(barrier, inc=1, device_id=(px, py, c), device_id_type=MESH)
        pl.semaphore_wait(barrier, len(chips))
        copies = [pltpu.make_async_remote_copy(
            src_ref=ins[k].at[2 * px + py], dst_ref=outs[k].at[j], send_sem=send_sems.at[j, k],
            recv_sem=recv_sems.at[j, k], device_id=(px, py, c), device_id_type=MESH)
            for j, (px, py) in enumerate(chips) for k in range(n)]
        for cp in copies:
            cp.start()
        for cp in copies:
            cp.wait_recv()
        for cp in copies:
            cp.wait_send()

    launch()
    return [o[...] for o in outs]


def _exchange_small(small):
    def body(sm_ref, out_ref, send_sems, recv_sems):
        (x, y, c), _, _ = _place()
        me = _dev_index(x, y, c)
        peers = [(1 - x if r & 4 else x, 1 - y if r & 2 else y, 1 - c if r & 1 else c) for r in range(1, NDEV)]
        out_ref[me] = sm_ref[...]
        copies = [pltpu.make_async_remote_copy(
            src_ref=sm_ref, dst_ref=out_ref.at[me], send_sem=send_sems.at[r], recv_sem=recv_sems.at[r],
            device_id=peer, device_id_type=MESH) for r, peer in enumerate(peers)]
        for cp in copies:
            cp.start()
        for r, peer in enumerate(peers):
            pltpu.make_async_remote_copy(
                src_ref=sm_ref, dst_ref=out_ref.at[_dev_index(*peer)], send_sem=send_sems.at[r], recv_sem=recv_sems.at[r],
                device_id=peer, device_id_type=MESH).wait_recv()
        for cp in copies:
            cp.wait_send()

    vm = pl.BlockSpec(memory_space=pltpu.VMEM)
    return pl.pallas_call(
        body, out_shape=jax.ShapeDtypeStruct((NDEV,) + small.shape, F32), in_specs=[vm], out_specs=vm,
        scratch_shapes=[pltpu.SemaphoreType.DMA((NDEV - 1,)), pltpu.SemaphoreType.DMA((NDEV - 1,))],
        name="exchange_small",
    )(small)


def _inproj_fwd(x2d, norm_w, w_p, tb, nb):
    t = x2d.shape[0]

    def body(x_ref, nw_ref, w_ref, proj_ref, xnt_ref, xn_s):
        @pl.when(pl.program_id(1) == 0)
        def _():
            xv = x_ref[...]
            r = lax.rsqrt(jnp.mean(xv * xv, axis=-1, keepdims=True) + EPS)
            xn = (xv * r) * nw_ref[...]
            xn_s[...] = xn.astype(BF16)
            xnt_ref[...] = xn.T.astype(BF16)

        proj_ref[...] = _mm(xn_s[...], w_ref[...])

    return pl.pallas_call(
        body, grid=(t // tb, DIN // nb),
        in_specs=[pl.BlockSpec((tb, D), lambda i, j: (i, 0)), pl.BlockSpec((1, D), lambda i, j: (0, 0)),
                  pl.BlockSpec((D, nb), lambda i, j: (0, j))],
        out_specs=[pl.BlockSpec((tb, nb), lambda i, j: (i, j)), pl.BlockSpec((D, tb), lambda i, j: (0, i))],
        out_shape=[jax.ShapeDtypeStruct((t, DIN), F32), jax.ShapeDtypeStruct((D, t), BF16)],
        scratch_shapes=[pltpu.VMEM((tb, D), BF16)],
        name="inproj_fwd", compiler_params=_cp(56),
    )(x2d, norm_w, w_p)


def _lower_bound(lb_ref):
    a0 = lb_ref[0:1, :]
    a1 = lb_ref[1:2, :]
    mx = jnp.maximum(a0, a1)
    e0 = jnp.exp(a0 - mx)
    e1 = jnp.exp(a1 - mx)
    return e0 / (e0 + e1)


def _hgrn_chunk_fwd(hq, hf, lb, tril):
    sg = _sigmoid(hf)
    f = lb + (1.0 - lb) * sg
    g = jnp.log(f)
    k = 1.0 - f
    sq = _sigmoid(hq)
    q = hq * sq
    b = _mm_exact(tril, g)
    last_row = lax.broadcasted_iota(jnp.int32, b.shape, 0) == CH - 1
    b_last = jnp.sum(jnp.where(last_row, b, 0.0), axis=0, keepdims=True)
    c = 0.5 * b_last
    eb = jnp.exp(b)
    ea = jnp.exp(b - c)
    ek = jnp.exp(c - b)
    ed = jnp.exp(b_last - b)
    ebl = jnp.exp(b_last)
    return dict(sg=sg, f=f, k=k, sq=sq, q=q, eb=eb, ea=ea, ek=ek, ed=ed, ebl=ebl,
                qe=q * eb, qa=q * ea, ka=k * ek, kd=k * ed)


def _tri(lower):
    r = lax.broadcasted_iota(jnp.int32, (CH, CH), 0)
    c = lax.broadcasted_iota(jnp.int32, (CH, CH), 1)
    return (r >= c) if lower else (c >= r)


def _head_segment(p_ref, rows, j, hg):
    return p_ref[rows, j * HD * hg:(j + 1) * HD * hg]


def _head(a, k):
    return a[:, k * HD:(k + 1) * HD]


def _hgrn_fwd(proj, lbw, rb, hg):
    assert hg == HEADS
    t = proj.shape[0]
    ncb = rb // CH

    def body(p_ref, lb_ref, o_ref, st_ref, s_scr):
        @pl.when(pl.program_id(1) == 0)
        def _():
            s_scr[...] = jnp.zeros_like(s_scr)

        lb = _lower_bound(lb_ref)
        causal = _tri(True)
        tril = causal.astype(F32)
        heads = range(hg)

        def chunk(cc, carry):
            r0 = pl.multiple_of(cc * CH, CH)
            rows = pl.ds(r0, CH)
            e = _hgrn_chunk_fwd(_head_segment(p_ref, rows, 0, hg), _head_segment(p_ref, rows, 1, hg), lb, tril)
            v = _bf(_head_segment(p_ref, rows, 2, hg))
            sts = [s_scr[k] for k in heads]
            qa, ka, qe, kd = _bf(e["qa"]), _bf(e["ka"]), _bf(e["qe"]), _bf(e["kd"])
            a = [_bf(jnp.where(causal, _mm_nt(_head(qa, k), _head(ka, k)), 0.0)) for k in heads]
            o_inter = [_mm_nt(_head(qe, k), _bf(sts[k])) for k in heads]
            kv = [_mm_tn(_head(v, k), _head(kd, k)) for k in heads]
            o_intra = [_mm(a[k], _head(v, k)) for k in heads]
            for k in heads:
                st_ref[cc, k] = sts[k]
                o_ref[rows, k * HD:(k + 1) * HD] = o_inter[k] + o_intra[k]
                s_scr[k] = sts[k] * _head(e["ebl"], k) + kv[k]
            return carry

        lax.fori_loop(0, ncb, chunk, 0)

    return pl.pallas_call(
        body, grid=(HEADS // hg, t // rb),
        in_specs=[pl.BlockSpec((rb, 3 * HD * hg), lambda h, i: (i, h)), pl.BlockSpec((2, HD * hg), lambda h, i: (0, h))],
        out_specs=[pl.BlockSpec((rb, HD * hg), lambda h, i: (i, h)),
                   pl.BlockSpec((ncb, hg, HD, HD), lambda h, i: (i, h, 0, 0))],
        out_shape=[jax.ShapeDtypeStruct((t, D), F32), jax.ShapeDtypeStruct((t // CH, HEADS, HD, HD), F32)],
        scratch_shapes=[pltpu.VMEM((hg, HD, HD), F32)],
        name="hgrn_fwd", compiler_params=_cp(48),
    )(proj, lbw)


def _hgrn_bwd(proj, lbw, do_raw, states, rb, hg):
    assert hg == HEADS
    t = proj.shape[0]
    nblk = t // rb
    ncb = rb // CH
    wd = HD * hg

    def body(p_ref, lb_ref, do_ref, st_ref, dp_ref, dlb_ref, ds_scr):
        @pl.when(pl.program_id(1) == 0)
        def _():
            ds_scr[...] = jnp.zeros_like(ds_scr)
            dlb_ref[...] = jnp.zeros_like(dlb_ref)

        lb = _lower_bound(lb_ref)
        causal = _tri(True)
        tril = causal.astype(F32)
        triu = _tri(False).astype(F32)
        last_row = lax.broadcasted_iota(jnp.int32, (CH, HD * hg), 0) == CH - 1
        row0 = lax.broadcasted_iota(jnp.int32, (8, HD * hg), 0) == 0
        heads = range(hg)
        wide = lambda parts: jnp.concatenate(parts, axis=1)

        def chunk(it, carry):
            cc = ncb - 1 - it
            r0 = pl.multiple_of(cc * CH, CH)
            rows = pl.ds(r0, CH)
            hq = _head_segment(p_ref, rows, 0, hg)
            e = _hgrn_chunk_fwd(hq, _head_segment(p_ref, rows, 1, hg), lb, tril)
            v = _bf(_head_segment(p_ref, rows, 2, hg))
            do = _bf(do_ref[rows, :])
            sts = [st_ref[cc, k] for k in heads]
            dsts = [ds_scr[k] for k in heads]
            dlb_acc = dlb_ref[...]
            qa, ka, qe, kd = _bf(e["qa"]), _bf(e["ka"]), _bf(e["qe"]), _bf(e["kd"])
            a = [_bf(jnp.where(causal, _mm_nt(_head(qa, k), _head(ka, k)), 0.0)) for k in heads]
            da = [_bf(jnp.where(causal, _mm_nt(_head(do, k), _head(v, k)), 0.0)) for k in heads]
            dqe = wide([_mm(_head(do, k), _bf(sts[k])) for k in heads])
            dkd = wide([_mm(_head(v, k), _bf(dsts[k])) for k in heads])
            dv_state = [_mm_nt(_head(kd, k), _bf(dsts[k])) for k in heads]
            ds_new = [_mm_tn(_head(do, k), _head(qe, k)) for k in heads]
            dv_intra = [_mm_tn(a[k], _head(do, k)) for k in heads]
            dqa = wide([_mm(da[k], _head(ka, k)) for k in heads])
            dka = wide([_mm_tn(da[k], _head(qa, k)) for k in heads])
            dv = wide([dv_intra[k] + dv_state[k] for k in heads])
            dbl = e["ebl"] * wide([jnp.sum(sts[k] * dsts[k], axis=0, keepdims=True) for k in heads])
            dq = dqe * e["eb"] + dqa * e["ea"]
            dk = dka * e["ek"] + dkd * e["ed"]
            dkd_kd = dkd * kd.astype(F32)
            db = dqe * qe.astype(F32) + dqa * qa.astype(F32) - dka * ka.astype(F32) - dkd_kd
            db = db + jnp.where(last_row, dbl + jnp.sum(dkd_kd, axis=0, keepdims=True), 0.0)
            dg = _mm_exact(triu, db)
            df = dg / e["f"] - dk
            sg = e["sg"]
            sq = e["sq"]
            dhq = _bf(dq * (sq * (1.0 + hq * (1.0 - sq))))
            dhf = _bf(df * (1.0 - lb) * sg * (1.0 - sg))
            dhi = _bf(dv)
            dlb_new = dlb_acc + jnp.where(row0, jnp.sum(df * (1.0 - sg), axis=0, keepdims=True), 0.0)
            for k in heads:
                ds_scr[k] = ds_new[k] + dsts[k] * _head(e["ebl"], k)
            dp_ref[rows, 0:wd] = dhq
            dp_ref[rows, wd:2 * wd] = dhf
            dp_ref[rows, 2 * wd:3 * wd] = dhi
            dlb_ref[...] = dlb_new
            return carry

        lax.fori_loop(0, ncb, chunk, 0)

    rev = lambda h, i: (nblk - 1 - i, h)
    return pl.pallas_call(
        body, grid=(HEADS // hg, nblk),
        in_specs=[pl.BlockSpec((rb, 3 * HD * hg), rev), pl.BlockSpec((2, HD * hg), lambda h, i: (0, h)),
                  pl.BlockSpec((rb, HD * hg), rev), pl.BlockSpec((ncb, hg, HD, HD), lambda h, i: (nblk - 1 - i, h, 0, 0))],
        out_specs=[pl.BlockSpec((rb, 3 * HD * hg), rev), pl.BlockSpec((8, HD * hg), lambda h, i: (0, h))],
        out_shape=[jax.ShapeDtypeStruct((t, 3 * D), BF16), jax.ShapeDtypeStruct((8, D), F32)],
        scratch_shapes=[pltpu.VMEM((hg, HD, HD), F32)],
        name="hgrn_bwd", compiler_params=_cp(48),
    )(proj, lbw, do_raw, states)


def _kv_variants(tile, odd):
    low = lax.broadcasted_iota(jnp.int32, tile.shape, 1) < 64
    if odd:
        hi = jnp.where(low, 0.0, tile)
        lo = pltpu.roll(hi, 64, 1)
    else:
        lo = jnp.where(low, tile, 0.0)
        hi = pltpu.roll(lo, 64, 1)
    return _bf(lo), _bf(hi)


def _attn_masks(n):
    qi = lax.broadcasted_iota(jnp.int32, (AB, AB), 0)
    kj = lax.broadcasted_iota(jnp.int32, (AB, AB), 1)
    cur = kj <= qi
    return cur, cur | (n > 0), qi <= kj


def _kv_all(prev_ref, cur_ref):
    out = []
    for tl in range(2):
        cols = slice(tl * 128, (tl + 1) * 128)
        tile = jnp.concatenate([prev_ref[:, cols], cur_ref[:, cols]], axis=0)
        out.append(_kv_variants(tile, 0))
        out.append(_kv_variants(tile, 1))
    return out


def _attn_softmax(s2, sink, cur, ok):
    s = jnp.where(ok, jnp.where(cur, s2[:, AB:], s2[:, :AB]) * ATT_SCALE, NEG)
    m = jnp.maximum(jnp.max(s, axis=-1, keepdims=True), sink)
    p = jnp.exp(s - m)
    es = jnp.exp(sink - m)
    inv = 1.0 / (jnp.sum(p, axis=-1, keepdims=True) + es)
    return p * inv, es * inv


def _spread(pc, cur):
    return jnp.concatenate([jnp.where(cur, 0.0, pc), jnp.where(cur, pc, 0.0)], axis=1)


def _spread_t(pct, cur_t):
    return jnp.concatenate([jnp.where(cur_t, 0.0, pct), jnp.where(cur_t, pct, 0.0)], axis=0)


def _attn_fwd(proj, sinks):
    t = proj.shape[0]
    nb = t // AB

    def body(q_ref, kc_ref, kp_ref, vc_ref, vp_ref, sink_ref, o_ref):
        cur, ok, _ = _attn_masks(pl.program_id(0))
        kvars = _kv_all(kp_ref, kc_ref)
        vvars = _kv_all(vp_ref, vc_ref)
        qps = [_bf(q_ref[:, 128 * j:128 * (j + 1)]) for j in range(8)]
        scores = [[_mm_nt(qps[j], kvars[j // 2][ab]) for ab in range(2)] for j in range(8)]
        for j in range(8):
            parts = []
            for ab in range(2):
                pc, _ = _attn_softmax(scores[j][ab], sink_ref[0, 2 * j + ab], cur, ok)
                parts.append(_mm(_bf(_spread(pc, cur)), vvars[j // 2][ab]))
            o_ref[:, 128 * j:128 * (j + 1)] = parts[0] + parts[1]

    prev = lambda n: jnp.maximum(n - 1, 0)
    return pl.pallas_call(
        body, grid=(nb,),
        in_specs=[pl.BlockSpec((AB, D), lambda n: (n, C_AQ // D)),
                  pl.BlockSpec((AB, 256), lambda n: (n, C_AK // 256)),
                  pl.BlockSpec((AB, 256), lambda n: (prev(n), C_AK // 256)),
                  pl.BlockSpec((AB, 256), lambda n: (n, C_AV // 256)),
                  pl.BlockSpec((AB, 256), lambda n: (prev(n), C_AV // 256)),
                  pl.BlockSpec(memory_space=pltpu.SMEM)],
        out_specs=pl.BlockSpec((AB, D), lambda n: (n, 0)),
        out_shape=jax.ShapeDtypeStruct((t, D), F32),
        name="attn_fwd", compiler_params=_cp(32),
    )(proj, proj, proj, proj, proj, sinks)


def _attn_bwd(proj, sinks, do_a):
    t = proj.shape[0]
    nb = t // AB

    def body(q_ref, kc_ref, kp_ref, vc_ref, vp_ref, do_ref, sink_ref, dq_ref, dkv_ref, dsink_ref, carry):
        n = pl.program_id(0)

        @pl.when(n == 0)
        def _():
            dsink_ref[...] = jnp.zeros_like(dsink_ref)
            carry[...] = jnp.zeros_like(carry)

        @pl.when(n < nb)
        def _():
            cur, ok, cur_t = _attn_masks(n)
            low = lax.broadcasted_iota(jnp.int32, (2 * AB, 128), 1) < 64
            lane = lax.broadcasted_iota(jnp.int32, (8, 128), 1)
            row0 = lax.broadcasted_iota(jnp.int32, (8, 128), 0) == 0
            kvars = _kv_all(kp_ref, kc_ref)
            vvars = _kv_all(vp_ref, vc_ref)
            qps = [_bf(q_ref[:, 128 * j:128 * (j + 1)]) for j in range(8)]
            dops = [_bf(do_ref[:, 128 * j:128 * (j + 1)]) for j in range(8)]
            scores = [[_mm_nt(qps[j], kvars[j // 2][ab]) for ab in range(2)] for j in range(8)]
            dps = [[_mm_nt(dops[j], vvars[j // 2][ab]) for ab in range(2)] for j in range(8)]
            dsink = jnp.zeros((8, 128), F32)
            dk_ab = [[None, None] for _ in range(4)]
            dv_ab = [[None, None] for _ in range(4)]
            for j in range(8):
                g = j // 2
                dqp = None
                for ab in range(2):
                    head = 2 * j + ab
                    pc, ps = _attn_softmax(scores[j][ab], sink_ref[0, head], cur, ok)
                    dpc = jnp.where(cur, dps[j][ab][:, AB:], dps[j][ab][:, :AB])
                    rs = jnp.sum(pc * dpc, axis=-1, keepdims=True)
                    dsc = pc * (dpc - rs)
                    dsink = dsink + jnp.where(row0 & (lane == head), -jnp.sum(ps * rs), 0.0)
                    term = _mm(_bf(_spread(dsc, cur)), kvars[g][ab])
                    dqp = term if dqp is None else dqp + term
                    dk_t = _mm(_bf(_spread_t(dsc.T, cur_t)), qps[j])
                    dv_t = _mm(_bf(_spread_t(pc.T, cur_t)), dops[j])
                    dk_ab[g][ab] = dk_t if dk_ab[g][ab] is None else dk_ab[g][ab] + dk_t
                    dv_ab[g][ab] = dv_t if dv_ab[g][ab] is None else dv_ab[g][ab] + dv_t
                dq_ref[:, 128 * j:128 * (j + 1)] = _bf(dqp * ATT_SCALE)
            for tl in range(2):
                ke, ko = dk_ab[2 * tl], dk_ab[2 * tl + 1]
                ve, vo = dv_ab[2 * tl], dv_ab[2 * tl + 1]
                dkt = (jnp.where(low, ke[0], 0.0) + pltpu.roll(jnp.where(low, 0.0, ke[1]), 64, 1)
                       + jnp.where(low, 0.0, ko[1]) + pltpu.roll(jnp.where(low, ko[0], 0.0), 64, 1)) * ATT_SCALE
                dvt = (jnp.where(low, ve[0], 0.0) + pltpu.roll(jnp.where(low, 0.0, ve[1]), 64, 1)
                       + jnp.where(low, 0.0, vo[1]) + pltpu.roll(jnp.where(low, vo[0], 0.0), 64, 1))
                kcols = slice(tl * 128, (tl + 1) * 128)
                vcols = slice(256 + tl * 128, 256 + (tl + 1) * 128)
                dkv_ref[:, kcols] = _bf(carry[:, kcols] + dkt[0:AB])
                dkv_ref[:, vcols] = _bf(carry[:, vcols] + dvt[0:AB])
                carry[:, kcols] = dkt[AB:2 * AB]
                carry[:, vcols] = dvt[AB:2 * AB]
            dsink_ref[...] += dsink

        @pl.when(n == nb)
        def _():
            dkv_ref[...] = _bf(carry[...])

    cur = lambda n: jnp.minimum(n, nb - 1)
    prev = lambda n: jnp.clip(n - 1, 0, nb - 1)
    return pl.pallas_call(
        body, grid=(nb + 1,),
        in_specs=[pl.BlockSpec((AB, D), lambda n: (cur(n), C_AQ // D)),
                  pl.BlockSpec((AB, 256), lambda n: (cur(n), C_AK // 256)),
                  pl.BlockSpec((AB, 256), lambda n: (prev(n), C_AK // 256)),
                  pl.BlockSpec((AB, 256), lambda n: (cur(n), C_AV // 256)),
                  pl.BlockSpec((AB, 256), lambda n: (prev(n), C_AV // 256)),
                  pl.BlockSpec((AB, D), lambda n: (cur(n), 0)),
                  pl.BlockSpec(memory_space=pltpu.SMEM)],
        out_specs=[pl.BlockSpec((AB, D), lambda n: (cur(n), 0)),
                   pl.BlockSpec((AB, 512), lambda n: (prev(n), 0)),
                   pl.BlockSpec((8, 128), lambda n: (0, 0))],
        out_shape=[jax.ShapeDtypeStruct((t, D), BF16), jax.ShapeDtypeStruct((t, 512), BF16),
                   jax.ShapeDtypeStruct((8, 128), F32)],
        scratch_shapes=[pltpu.VMEM((AB, 512), F32)],
        name="attn_bwd", compiler_params=_cp(32),
    )(proj, proj, proj, proj, proj, do_a, sinks)


def _silu_and_grad(v):
    s = _sigmoid(v)
    return v * s, s * (1.0 + v * (1.0 - s))


def _tail(o_raw, o_a, proj, x2d, tgt, wbh, wba, wout, hnw, fnw, tb):
    t = x2d.shape[0]

    def body(or_ref, oa_ref, hg_ref, ag0, ag1, mh0, mh1, ma0, ma1, x_ref, t_ref, wbh_ref, wba_ref, wout_ref, hnw_ref,
             fnw_ref, dx2_ref, dor_ref, doa_ref, dhg_ref, dagm_ref, gh_ref, ga_ref, mg_ref, dyh_ref, dya_ref, dx2b_ref,
             sums_ref):
        @pl.when(pl.program_id(0) == 0)
        def _():
            sums_ref[...] = jnp.zeros_like(sums_ref)

        halves = lambda a, b: jnp.concatenate([a[...], b[...]], axis=1)
        hnw_v = hnw_ref[...]
        fnw_v = fnw_ref[...]
        o = or_ref[...]
        rs, xhs = [], []
        for h in range(HEADS):
            oh = o[:, h * HD:(h + 1) * HD]
            r = lax.rsqrt(jnp.mean(oh * oh, axis=-1, keepdims=True) + EPS)
            rs.append(r)
            xhs.append(oh * r)
        xh = jnp.concatenate(xhs, axis=1)
        on = xh * hnw_v
        sil_hg, dsil_hg = _silu_and_grad(hg_ref[...])
        gh_b = _bf(on * sil_hg)
        y_h = _mm(gh_b, wbh_ref[...])
        oa = oa_ref[...]
        sil_ag, dsil_ag = _silu_and_grad(halves(ag0, ag1))
        ga_b = _bf(oa * sil_ag)
        y_a = _mm(ga_b, wba_ref[...])
        s_mh = _sigmoid(halves(mh0, mh1))
        s_ma = _sigmoid(halves(ma0, ma1))
        mg_b = _bf(s_mh * y_h + s_ma * y_a)
        x2 = x_ref[...] + _mm(mg_b, wout_ref[...])
        r2 = lax.rsqrt(jnp.mean(x2 * x2, axis=-1, keepdims=True) + EPS)
        xh2 = x2 * r2
        err = xh2 * fnw_v - t_ref[...]
        loss = 0.5 * jnp.sum(jnp.mean(err * err, axis=-1, keepdims=True))
        dy = err * (1.0 / D)
        dfnw = jnp.sum(dy * xh2, axis=0, keepdims=True)
        dxh2 = dy * fnw_v
        dx2 = r2 * (dxh2 - xh2 * jnp.mean(dxh2 * xh2, axis=-1, keepdims=True))
        dx2_ref[...] = dx2
        dx2_b = _bf(dx2)
        dmg = _mm_nt(dx2_b, wout_ref[...])
        dmg_h = dmg * s_mh
        dmg_a = dmg * s_ma
        dyh_b = _bf(dmg_h)
        dya_b = _bf(dmg_a)
        dagm_ref[:, D:2 * D] = _bf(dmg_h * y_h * (1.0 - s_mh))
        dagm_ref[:, 2 * D:3 * D] = _bf(dmg_a * y_a * (1.0 - s_ma))
        dgh = _mm_nt(dyh_b, wbh_ref[...])
        dga = _mm_nt(dya_b, wba_ref[...])
        doa_ref[...] = dga * sil_ag
        dagm_ref[:, 0:D] = _bf(dga * oa * dsil_ag)
        dhg_ref[...] = _bf(dgh * on * dsil_hg)
        don = dgh * sil_hg
        dhnw = jnp.sum(don * xh, axis=0, keepdims=True)
        dxh = don * hnw_v
        dos = []
        for h in range(HEADS):
            sl = slice(h * HD, (h + 1) * HD)
            dos.append(rs[h] * (dxh[:, sl] - xhs[h] * jnp.mean(dxh[:, sl] * xhs[h], axis=-1, keepdims=True)))
        dor_ref[...] = jnp.concatenate(dos, axis=1)
        gh_ref[...] = gh_b
        ga_ref[...] = ga_b
        mg_ref[...] = mg_b
        dyh_ref[...] = dyh_b
        dya_ref[...] = dya_b
        dx2b_ref[...] = dx2_b
        row = lax.broadcasted_iota(jnp.int32, (8, D), 0)
        sums_ref[...] += jnp.where(row == 0, dfnw, 0.0) + jnp.where(row == 1, dhnw, 0.0) + jnp.where(row == 2, loss, 0.0)

    rowblk = lambda c: pl.BlockSpec((tb, D), lambda i: (i, c))
    half = lambda c: pl.BlockSpec((tb, 512), lambda i: (i, c))
    full = lambda shape: pl.BlockSpec(shape, lambda i: (0, 0))
    return pl.pallas_call(
        body, grid=(t // tb,),
        in_specs=[rowblk(0), rowblk(0), rowblk(C_HG // D), half(C_AG // 512), half(C_AG // 512 + 1), half(C_MH // 512),
                  half(C_MH // 512 + 1), half(C_MA // 512), half(C_MA // 512 + 1), rowblk(0), rowblk(0),
                  full((D, D)), full((D, D)), full((D, D)), full((1, D)), full((1, D))],
        out_specs=[rowblk(0), rowblk(0), rowblk(0), rowblk(0), pl.BlockSpec((tb, 3 * D), lambda i: (i, 0))]
        + [rowblk(0)] * 6 + [full((8, D))],
        out_shape=[jax.ShapeDtypeStruct((t, D), F32)] * 3
        + [jax.ShapeDtypeStruct((t, D), BF16), jax.ShapeDtypeStruct((t, 3 * D), BF16)]
        + [jax.ShapeDtypeStruct((t, D), BF16)] * 6 + [jax.ShapeDtypeStruct((8, D), F32)],
        name="tail", compiler_params=_cp(56),
    )(o_raw, o_a, proj, proj, proj, proj, proj, proj, proj, x2d, tgt, wbh, wba, wout, hnw, fnw)


def _wgrad3(gh, dyh, ga, dya, mg, dx2b, tk):
    t = dyh.shape[0]

    def body(a0, b0, a1, b1, a2, b2, o0, o1, o2):
        @pl.when(pl.program_id(0) == 0)
        def _():
            o0[...] = jnp.zeros_like(o0)
            o1[...] = jnp.zeros_like(o1)
            o2[...] = jnp.zeros_like(o2)

        o0[...] += _mm_tn(a0[...], b0[...])
        o1[...] += _mm_tn(a1[...], b1[...])
        o2[...] += _mm_tn(a2[...], b2[...])

    blk = pl.BlockSpec((tk, D), lambda k: (k, 0))
    out = pl.BlockSpec((D, D), lambda k: (0, 0))
    return pl.pallas_call(
        body, grid=(t // tk,), in_specs=[blk] * 6, out_specs=[out] * 3,
        out_shape=[jax.ShapeDtypeStruct((D, D), F32)] * 3,
        name="wgrad3", compiler_params=_cp(48),
    )(gh, dyh, ga, dya, mg, dx2b)


def _inproj_wgrad_piece(xnt, piece, nb, name):
    t = xnt.shape[1]
    width = piece.shape[1]

    def body(xnt_ref, p_ref, o_ref):
        o_ref[...] = _mm(xnt_ref[...], p_ref[...])

    return pl.pallas_call(
        body, grid=(width // nb,),
        in_specs=[pl.BlockSpec((D, t), lambda j: (0, 0), pipeline_mode=pl.Buffered(1)),
                  pl.BlockSpec((t, nb), lambda j: (0, j))],
        out_specs=pl.BlockSpec((D, nb), lambda j: (0, j)),
        out_shape=jax.ShapeDtypeStruct((D, width), F32),
        name=name, compiler_params=_cp(56),
    )(xnt, piece)


def _inproj_dgrad(pieces, w_p, x2d, dx2, norm_w, tb, after):
    t = x2d.shape[0]

    def body(*refs):
        piece_refs = refs[:len(pieces)]
        w_ref, x_ref, dx2_ref, nw_ref, _, gx_ref, dnw_ref = refs[len(pieces):]

        @pl.when(pl.program_id(0) == 0)
        def _():
            dnw_ref[...] = jnp.zeros_like(dnw_ref)

        dxn = None
        off = 0
        for p in piece_refs:
            width = p.shape[1]
            term = _mm_nt(p[...], w_ref[:, off:off + width])
            dxn = term if dxn is None else dxn + term
            off += width
        xv = x_ref[...]
        r = lax.rsqrt(jnp.mean(xv * xv, axis=-1, keepdims=True) + EPS)
        xh = xv * r
        dxh = dxn * nw_ref[...]
        gx_ref[...] = dx2_ref[...] + r * (dxh - xh * jnp.mean(dxh * xh, axis=-1, keepdims=True))
        row0 = lax.broadcasted_iota(jnp.int32, (8, D), 0) == 0
        dnw_ref[...] += jnp.where(row0, jnp.sum(dxn * xh, axis=0, keepdims=True), 0.0)

    rowblk = pl.BlockSpec((tb, D), lambda i: (i, 0))
    return pl.pallas_call(
        body, grid=(t // tb,),
        in_specs=[pl.BlockSpec((tb, p.shape[1]), lambda i: (i, 0)) for p in pieces]
        + [pl.BlockSpec((D, DIN), lambda i: (0, 0), pipeline_mode=pl.Buffered(1)), rowblk, rowblk,
           pl.BlockSpec((1, D), lambda i: (0, 0)), pl.BlockSpec(memory_space=pl.ANY)],
        out_specs=[rowblk, pl.BlockSpec((8, D), lambda i: (0, 0))],
        out_shape=[jax.ShapeDtypeStruct((t, D), F32), jax.ShapeDtypeStruct((8, D), F32)],
        name="inproj_dgrad", compiler_params=_cp(60),
    )(*pieces, w_p, x2d, dx2, norm_w, after)


def _adamw_math(w, g, m, v):
    m = B1 * m + (1.0 - B1) * g
    v = B2 * v + (1.0 - B2) * (g * g)
    m_hat = m / (1.0 - B1 ** STEP)
    v_hat = v / (1.0 - B2 ** STEP)
    delta = -LR * (m_hat / (jnp.sqrt(v_hat) + ADAM_EPS) + WD * w)
    return delta, m, v


def _adamw_shard(recv, sums, chip, w, m, v, rows, name):
    nparts, nr, nc = recv.shape

    def body(chip_ref, own_ref, p_ref, w_ref, m_ref, v_ref, g_ref, d_ref, nm_ref, nv_ref):
        g = own_ref[0].astype(F32)
        for s in range(nparts):
            g = g + p_ref[s].astype(F32)
        d, nm, nv = _adamw_math(w_ref[...], g, m_ref[...], v_ref[...])
        g_ref[...] = g
        d_ref[...] = d
        nm_ref[...] = nm
        nv_ref[...] = nv

    blk = pl.BlockSpec((rows, nc), lambda i, chip_ref: (i, 0))
    return pl.pallas_call(
        body,
        grid_spec=pltpu.PrefetchScalarGridSpec(
            num_scalar_prefetch=1, grid=(nr // rows,),
            in_specs=[pl.BlockSpec((1, rows, nc), lambda i, chip_ref: (chip_ref[0], i, 0)),
                      pl.BlockSpec((nparts, rows, nc), lambda i, chip_ref: (0, i, 0)), blk, blk, blk],
            out_specs=[blk] * 4),
        out_shape=[jax.ShapeDtypeStruct((nr, nc), F32)] * 4,
        name=name, compiler_params=_cp(48),
    )(chip, sums, recv, w, m, v)


SMALL_ROWS = dict(norm_w=0, lower_bound=1, hgrn_norm_w=3, final_norm_w=4, sinks=5)


def _pack_small_grads(dnw, dlb, sums, dsink):
    def body(dnw_ref, dlb_ref, sums_ref, dsink_ref, o_ref):
        o_ref[...] = jnp.zeros_like(o_ref)
        o_ref[0:1, :] = dnw_ref[0:1, :]
        o_ref[1:2, :] = dlb_ref[0:1, :]
        o_ref[3:4, :] = sums_ref[1:2, :]
        o_ref[4:5, :] = sums_ref[0:1, :]
        o_ref[5:6, 0:128] = dsink_ref[0:1, :]

    return pl.pallas_call(body, out_shape=jax.ShapeDtypeStruct((8, D), F32), name="pack_small_grads")(dnw, dlb, sums, dsink)


def _adamw_small(parts, ws, ms, vs):
    shapes = [a.shape for a in ws]

    def body(p_ref, *refs):
        w, m, v = refs[0:5], refs[5:10], refs[10:15]
        outs = [refs[15 + 5 * i:20 + 5 * i] for i in range(4)]

        def total(row, width):
            g = p_ref[0, row:row + 1, 0:width]
            for s in range(1, NDEV):
                g = g + p_ref[s, row:row + 1, 0:width]
            return g

        lb = _lower_bound(w[1])
        ga0 = total(1, D) * lb * (1.0 - lb)
        grads = [total(0, D), None, total(3, D), total(4, D), total(5, QH)]
        for i in (0, 2, 3, 4):
            res = (grads[i],) + _adamw_math(w[i][...], grads[i], m[i][...], v[i][...])
            for o, val in zip(outs, res):
                o[i][...] = val
        for r, g in ((0, ga0), (1, -ga0)):
            res = (g,) + _adamw_math(w[1][r:r + 1, :], g, m[1][r:r + 1, :], v[1][r:r + 1, :])
            for o, val in zip(outs, res):
                o[1][r:r + 1, :] = val

    res = pl.pallas_call(
        body, out_shape=[jax.ShapeDtypeStruct(s, F32) for s in shapes] * 4, name="adamw_small",
    )(parts, *ws, *ms, *vs)
    return [res[5 * i:5 * i + 5] for i in range(4)]


def kernel(x, norm_w, w_in, hgrn_lower_bound, hgrn_norm_w, w_branch_hgrn, attn_sinks, w_branch_attn, w_out, final_norm_w, loss_target, m_norm_w, m_w_in, m_hgrn_lower_bound, m_hgrn_norm_w, m_w_branch_hgrn, m_attn_sinks, m_w_branch_attn, m_w_out, m_final_norm_w, v_norm_w, v_w_in, v_hgrn_lower_bound, v_hgrn_norm_w, v_w_branch_hgrn, v_attn_sinks, v_w_branch_attn, v_w_out, v_final_norm_w):
    t = x.shape[1]
    x2d = x.reshape(t, D)
    tgt = loss_target.reshape(t, D)
    fnw = final_norm_w.reshape(1, D)
    row_blk = min(256, t)
    big_blk = min(512, t)

    win_g, wbh_g, wba_g, wout_g = _gather_weights(w_in[0], w_branch_hgrn[0], w_branch_attn[0], w_out[0])
    w_p = win_g.transpose(1, 0, 2).reshape(D, DIN)
    wbh, wba, wout = wbh_g.reshape(D, D), wba_g.reshape(D, D), wout_g.reshape(D, D)

    proj, xnt = _inproj_fwd(x2d, norm_w, w_p, min(1024, t), 2176)
    o_raw, states = _hgrn_fwd(proj, hgrn_lower_bound, big_blk, HGRN_GROUP)
    o_a = _attn_fwd(proj, attn_sinks)
    (dx2, do_raw, do_a, d_hg, d_agm, gh, ga, mg, dyh, dya, dx2b, sums) = _tail(
        o_raw, o_a, proj, x2d, tgt, wbh, wba, wout, hgrn_norm_w, fnw, row_blk)
    dwbh, dwba, dwout = _wgrad3(gh, dyh, ga, dya, mg, dx2b, big_blk)
    d_aq, d_kv, dsink = _attn_bwd(proj, attn_sinks, do_a)
    d_hgrn, dlb = _hgrn_bwd(proj, hgrn_lower_bound, do_raw, states, big_blk, HGRN_GROUP)
    pieces = (d_hgrn, d_hg, d_aq, d_kv, d_agm)
    dw_pieces = [_inproj_wgrad_piece(xnt, p, CB, "inproj_wgrad_" + n)
                 for p, n in zip(pieces, ("hgrn", "hgate", "aq", "kv", "gates"))]

    dwin_r = jnp.concatenate(dw_pieces, axis=1).reshape(D, NDEV, IN_SHARD).transpose(1, 0, 2).astype(BF16)
    slots = lambda a: a.reshape(NDEV, ROW_SHARD, D).astype(BF16)
    partials = [dwin_r, slots(dwbh), slots(dwba), slots(dwout)]
    got = _exchange_pair(partials)
    core = lax.axis_index("c").astype(jnp.int32).reshape(1)
    pair_sums = [_pair_sum(a, g, core, ROW_SHARD, "pair_sum_" + n)
                 for a, g, n in zip(partials, got, ("w_in", "w_bh", "w_ba", "w_out"))]
    rin, rbh, rba, rout = _exchange_chips(pair_sums)
    grad_x, dnw = _inproj_dgrad(pieces, w_p, x2d, dx2, norm_w, big_blk, after=pair_sums[0])
    rsm = _exchange_small(_pack_small_grads(dnw, dlb, sums, dsink))
    chip = (2 * lax.axis_index("x") + lax.axis_index("y")).astype(jnp.int32).reshape(1)
    s_in, s_bh, s_ba, s_out = pair_sums
    g_in, d_in, nm_in, nv_in = _adamw_shard(rin, s_in, chip, w_in[0], m_w_in[0], v_w_in[0], 128, "adamw_w_in")
    g_bh, d_bh, nm_bh, nv_bh = _adamw_shard(
        rbh, s_bh, chip, w_branch_hgrn[0], m_w_branch_hgrn[0], v_w_branch_hgrn[0], 128, "adamw_w_bh")
    g_ba, d_ba, nm_ba, nv_ba = _adamw_shard(
        rba, s_ba, chip, w_branch_attn[0], m_w_branch_attn[0], v_w_branch_attn[0], 128, "adamw_w_ba")
    g_out, d_out, nm_out, nv_out = _adamw_shard(rout, s_out, chip, w_out[0], m_w_out[0], v_w_out[0], 128, "adamw_w_out")
    sg, sd, sm, sv = _adamw_small(
        rsm,
        (norm_w, hgrn_lower_bound, hgrn_norm_w, fnw, attn_sinks),
        (m_norm_w, m_hgrn_lower_bound, m_hgrn_norm_w, m_final_norm_w.reshape(1, D), m_attn_sinks),
        (v_norm_w, v_hgrn_lower_bound, v_hgrn_norm_w, v_final_norm_w.reshape(1, D), v_attn_sinks))

    loss = lax.psum(sums[2, 0], ("x", "y", "c"))

    def group(s, w_in_v, bh, ba, out):
        nw, lb, hnw, fn, sinks = s
        return (nw, w_in_v[None], lb, hnw, bh[None], sinks, ba[None], out[None], fn.reshape(D))

    return (loss, grad_x.reshape(1, t, D),
            *group(sg, g_in, g_bh, g_ba, g_out), *group(sd, d_in, d_bh, d_ba, d_out),
            *group(sm, nm_in, nm_bh, nm_ba, nm_out), *group(sv, nv_in, nv_bh, nv_ba, nv_out))
```

```python
import functools

import jax
import jax.numpy as jnp
from jax import lax
from jax.experimental import pallas as pl
from jax.experimental.pallas import tpu as pltpu
from jax.experimental.pallas import tpu_sc as plsc

F32 = jnp.float32
BF16 = jnp.bfloat16

D = 1024
DIN = 8704
NDEV = 8
IN_SHARD = DIN // NDEV
ROW_SHARD = D // NDEV
HEADS = 8
HD = 128
CH = 64
HGRN_GROUP = 8
QH = 16
AB = 128
EPS = 1e-6
NEG = -1e30
ATT_SCALE = 0.125

C_HGRN = 0
C_HG = 3072
C_AQ = 4096
C_AK = 5120
C_AV = 5376
C_AG = 5632
C_MH = 6656
C_MA = 7680
CB = 512

LR = 0.001
B1 = 0.9
B2 = 0.999
ADAM_EPS = 1e-08
WD = 0.01
STEP = 10

V7X_VMEM_BYTES = 64 * 1024 * 1024
MESH = pl.DeviceIdType.MESH


def _cp(vmem_mb):
    return pltpu.CompilerParams(vmem_limit_bytes=vmem_mb * 1024 * 1024)


def _mm(a, b):
    return jnp.dot(a, b, preferred_element_type=F32)


def _mm_nt(a, b):
    return lax.dot_general(a, b, (((1,), (1,)), ((), ())), preferred_element_type=F32)


def _mm_tn(a, b):
    return lax.dot_general(a, b, (((0,), (0,)), ((), ())), preferred_element_type=F32)


def _mm_exact(a, b):
    return jnp.dot(a, b, preferred_element_type=F32, precision=lax.Precision.HIGHEST)


def _sigmoid(v):
    return 0.5 * jnp.tanh(0.5 * v) + 0.5


def _bf(v):
    return v.astype(BF16)


def _place():
    x, y, c = lax.axis_index("x"), lax.axis_index("y"), lax.axis_index("c")
    return (x, y, c), (x, y, 1 - c), [(1 - x, y), (x, 1 - y), (1 - x, 1 - y)]


def _dev_index(px, py, pc):
    return 4 * px + 2 * py + pc


def _gather_weights(w_in_s, wbh_s, wba_s, wout_s):
    def body(win_ref, wbh_ref, wba_ref, wout_ref, win_g, wbh_g, wba_g, wout_g, sin, s3, send_sems, recv_sems, loc_sems):
        (x, y, c), sibling, chips = _place()
        me = _dev_index(x, y, c)
        sin[...] = win_ref[...].astype(BF16)
        s3[0] = wbh_ref[...].astype(BF16)
        s3[1] = wba_ref[...].astype(BF16)
        s3[2] = wout_ref[...].astype(BF16)
        srcs = [sin, s3.at[0], s3.at[1], s3.at[2]]
        outs = [win_g, wbh_g, wba_g, wout_g]

        def copy(kind, k, block, to, src=None):
            return pltpu.make_async_remote_copy(
                src_ref=srcs[k] if src is None else src, dst_ref=outs[k].at[block], send_sem=send_sems.at[kind, k],
                recv_sem=recv_sems.at[kind, k], device_id=to, device_id_type=MESH)

        local = [pltpu.make_async_copy(srcs[k], outs[k].at[me], loc_sems.at[k]) for k in range(4)]
        for cp in local:
            cp.start()
        first = [copy(0, k, me, sibling) for k in range(4)]
        first += [copy(1 + j, k, me, (*chip, c)) for j, chip in enumerate(chips) for k in range(4)]
        for cp in first:
            cp.start()
        passed = []
        for j, chip in enumerate(chips):
            block = _dev_index(*chip, c)
            for k in range(4):
                copy(1 + j, k, block, (x, y, c)).wait_recv()
            for k in range(4):
                cp = copy(4 + j, k, block, sibling, src=outs[k].at[block])
                cp.start()
                passed.append(cp)
        for k in range(4):
            copy(0, k, _dev_index(x, y, 1 - c), (x, y, c)).wait_recv()
        for j, chip in enumerate(chips):
            for k in range(4):
                copy(4 + j, k, _dev_index(*chip, 1 - c), (x, y, c)).wait_recv()
        for cp in first + passed:
            cp.wait_send()
        for cp in local:
            cp.wait()

    vm = pl.BlockSpec(memory_space=pltpu.VMEM)
    hbm = pl.BlockSpec(memory_space=pl.ANY)
    return pl.pallas_call(
        body,
        out_shape=[jax.ShapeDtypeStruct((NDEV, D, IN_SHARD), BF16)] + [jax.ShapeDtypeStruct((NDEV, ROW_SHARD, D), BF16)] * 3,
        in_specs=[vm, vm, vm, vm],
        out_specs=[hbm, hbm, hbm, hbm],
        scratch_shapes=[pltpu.VMEM((D, IN_SHARD), BF16), pltpu.VMEM((3, ROW_SHARD, D), BF16),
                        pltpu.SemaphoreType.DMA((NDEV - 1, 4)), pltpu.SemaphoreType.DMA((NDEV - 1, 4)),
                        pltpu.SemaphoreType.DMA((4,))],
        name="gather_weights", compiler_params=_cp(32),
    )(w_in_s, wbh_s, wba_s, wout_s)


def _exchange_pair(arrs):
    n = len(arrs)
    ins = [jax.new_ref(a, memory_space=pltpu.MemorySpace.HBM) for a in arrs]
    got = [jax.empty_ref(jax.ShapeDtypeStruct((4,) + a.shape[1:], a.dtype), memory_space=pltpu.MemorySpace.HBM)
           for a in arrs]

    @pl.kernel(mesh=plsc.ScalarSubcoreMesh(axis_name="sequencer", num_cores=1), name="exchange_pair",
               scratch_types=(pltpu.SemaphoreType.DMA((4, n)), pltpu.SemaphoreType.DMA((4, n))),
               compiler_params=pltpu.CompilerParams(collective_id=0))
    def launch(send_sems, recv_sems):
        (x, y, c), sibling, _ = _place()
        barrier = pltpu.get_barrier_semaphore()
        pl.semaphore_signal(barrier, inc=1, device_id=sibling, device_id_type=MESH)
        pl.semaphore_wait(barrier, 1)
        sends = [pltpu.make_async_remote_copy(
            src_ref=ins[k].at[_dev_index(q // 2, q % 2, 1 - c)], dst_ref=got[k].at[q], send_sem=send_sems.at[q, k],
            recv_sem=recv_sems.at[q, k], device_id=sibling, device_id_type=MESH) for q in range(4) for k in range(n)]
        for cp in sends:
            cp.start()
        for cp in sends:
            cp.wait_recv()
        for cp in sends:
            cp.wait_send()

    launch()
    return [g[...] for g in got]


def _pair_sum(full, got, core, rows, name):
    _, nr, nc = got.shape

    def body(core_ref, a_ref, b_ref, o_ref):
        o_ref[...] = (a_ref[...].astype(F32) + b_ref[...].astype(F32)).astype(BF16)

    blk = pl.BlockSpec((1, rows, nc), lambda q, i, core_ref: (q, i, 0))
    return pl.pallas_call(
        body,
        grid_spec=pltpu.PrefetchScalarGridSpec(
            num_scalar_prefetch=1, grid=(4, nr // rows),
            in_specs=[pl.BlockSpec((1, rows, nc), lambda q, i, core_ref: (2 * q + core_ref[0], i, 0)), blk],
            out_specs=blk),
        out_shape=jax.ShapeDtypeStruct(got.shape, BF16), name=name,
    )(core, full, got)


def _exchange_chips(sums):
    n = len(sums)
    ins = [jax.new_ref(a, memory_space=pltpu.MemorySpace.HBM) for a in sums]
    outs = [jax.empty_ref(jax.ShapeDtypeStruct((3,) + a.shape[1:], a.dtype), memory_space=pltpu.MemorySpace.HBM)
            for a in sums]

    @pl.kernel(mesh=plsc.ScalarSubcoreMesh(axis_name="sequencer", num_cores=1), name="exchange_chips",
               scratch_types=(pltpu.SemaphoreType.DMA((3, n)), pltpu.SemaphoreType.DMA((3, n))),
               compiler_params=pltpu.CompilerParams(collective_id=1))
    def launch(send_sems, recv_sems):
        (x, y, c), _, chips = _place()
        barrier = pltpu.get_barrier_semaphore()
        for px, py in chips:
            pl.semaphore_signal(barrier, inc=1, device_id=(px, py, c), device_id_type=MESH)
        pl.semaphore_wait(barrier, len(chips))
        copies = [pltpu.make_async_remote_copy(
            src_ref=ins[k].at[2 * px + py], dst_ref=outs[k].at[j], send_sem=send_sems.at[j, k],
            recv_sem=recv_sems.at[j, k], device_id=(px, py, c), device_id_type=MESH)
            for j, (px, py) in enumerate(chips) for k in range(n)]
        for cp in copies:
            cp.start()
        for cp in copies:
            cp.wait_recv()
        for cp in copies:
            cp.wait_send()

    launch()
    return [o[...] for o in outs]


def _exchange_small(small):
    def body(sm_ref, out_ref, send_sems, recv_sems):
        (x, y, c), _, _ = _place()
        me = _dev_index(x, y, c)
        peers = [(1 - x if r & 4 else x, 1 - y if r & 2 else y, 1 - c if r & 1 else c) for r in range(1, NDEV)]
        out_ref[me] = sm_ref[...]
        copies = [pltpu.make_async_remote_copy(
            src_ref=sm_ref, dst_ref=out_ref.at[me], send_sem=send_sems.at[r], recv_sem=recv_sems.at[r],
            device_id=peer, device_id_type=MESH) for r, peer in enumerate(peers)]
        for cp in copies:
            cp.start()
        for r, peer in enumerate(peers):
            pltpu.make_async_remote_copy(
                src_ref=sm_ref, dst_ref=out_ref.at[_dev_index(*peer)], send_sem=send_sems.at[r], recv_sem=recv_sems.at[r],
                device_id=peer, device_id_type=MESH).wait_recv()
        for cp in copies:
            cp.wait_send()

    vm = pl.BlockSpec(memory_space=pltpu.VMEM)
    return pl.pallas_call(
        body, out_shape=jax.ShapeDtypeStruct((NDEV,) + small.shape, F32), in_specs=[vm], out_specs=vm,
        scratch_shapes=[pltpu.SemaphoreType.DMA((NDEV - 1,)), pltpu.SemaphoreType.DMA((NDEV - 1,))],
        name="exchange_small",
    )(small)


def _inproj_fwd(x2d, norm_w, w_p, tb, nb):
    t = x2d.shape[0]

    def body(x_ref, nw_ref, w_ref, proj_ref, xnt_ref, xn_s):
        @pl.when(pl.program_id(1) == 0)
        def _():
            xv = x_ref[...]
            r = lax.rsqrt(jnp.mean(xv * xv, axis=-1, keepdims=True) + EPS)
            xn = (xv * r) * nw_ref[...]
            xn_s[...] = xn.astype(BF16)
            xnt_ref[...] = xn.T.astype(BF16)

        proj_ref[...] = _mm(xn_s[...], w_ref[...])

    return pl.pallas_call(
        body, grid=(t // tb, DIN // nb),
        in_specs=[pl.BlockSpec((tb, D), lambda i, j: (i, 0)), pl.BlockSpec((1, D), lambda i, j: (0, 0)),
                  pl.BlockSpec((D, nb), lambda i, j: (0, j))],
        out_specs=[pl.BlockSpec((tb, nb), lambda i, j: (i, j)), pl.BlockSpec((D, tb), lambda i, j: (0, i))],
        out_shape=[jax.ShapeDtypeStruct((t, DIN), F32), jax.ShapeDtypeStruct((D, t), BF16)],
        scratch_shapes=[pltpu.VMEM((tb, D), BF16)],
        name="inproj_fwd", compiler_params=_cp(56),
    )(x2d, norm_w, w_p)


def _lower_bound(lb_ref):
    a0 = lb_ref[0:1, :]
    a1 = lb_ref[1:2, :]
    mx = jnp.maximum(a0, a1)
    e0 = jnp.exp(a0 - mx)
    e1 = jnp.exp(a1 - mx)
    return e0 / (e0 + e1)


def _hgrn_chunk_fwd(hq, hf, lb, tril):
    sg = _sigmoid(hf)
    f = lb + (1.0 - lb) * sg
    g = jnp.log(f)
    k = 1.0 - f
    sq = _sigmoid(hq)
    q = hq * sq
    b = _mm_exact(tril, g)
    last_row = lax.broadcasted_iota(jnp.int32, b.shape, 0) == CH - 1
    b_last = jnp.sum(jnp.where(last_row, b, 0.0), axis=0, keepdims=True)
    c = 0.5 * b_last
    eb = jnp.exp(b)
    ea = jnp.exp(b - c)
    ek = jnp.exp(c - b)
    ed = jnp.exp(b_last - b)
    ebl = jnp.exp(b_last)
    return dict(sg=sg, f=f, k=k, sq=sq, q=q, eb=eb, ea=ea, ek=ek, ed=ed, ebl=ebl,
                qe=q * eb, qa=q * ea, ka=k * ek, kd=k * ed)


def _tri(lower):
    r = lax.broadcasted_iota(jnp.int32, (CH, CH), 0)
    c = lax.broadcasted_iota(jnp.int32, (CH, CH), 1)
    return (r >= c) if lower else (c >= r)


def _head_segment(p_ref, rows, j, hg):
    return p_ref[rows, j * HD * hg:(j + 1) * HD * hg]


def _head(a, k):
    return a[:, k * HD:(k + 1) * HD]


def _hgrn_fwd(proj, lbw, rb, hg):
    assert hg == HEADS
    t = proj.shape[0]
    ncb = rb // CH

    def body(p_ref, lb_ref, o_ref, st_ref, s_scr):
        @pl.when(pl.program_id(1) == 0)
        def _():
            s_scr[...] = jnp.zeros_like(s_scr)

        lb = _lower_bound(lb_ref)
        causal = _tri(True)
        tril = causal.astype(F32)
        heads = range(hg)

        def chunk(cc, carry):
            r0 = pl.multiple_of(cc * CH, CH)
            rows = pl.ds(r0, CH)
            e = _hgrn_chunk_fwd(_head_segment(p_ref, rows, 0, hg), _head_segment(p_ref, rows, 1, hg), lb, tril)
            v = _bf(_head_segment(p_ref, rows, 2, hg))
            sts = [s_scr[k] for k in heads]
            qa, ka, qe, kd = _bf(e["qa"]), _bf(e["ka"]), _bf(e["qe"]), _bf(e["kd"])
            a = [_bf(jnp.where(causal, _mm_nt(_head(qa, k), _head(ka, k)), 0.0)) for k in heads]
            o_inter = [_mm_nt(_head(qe, k), _bf(sts[k])) for k in heads]
            kv = [_mm_tn(_head(v, k), _head(kd, k)) for k in heads]
            o_intra = [_mm(a[k], _head(v, k)) for k in heads]
            for k in heads:
                st_ref[cc, k] = sts[k]
                o_ref[rows, k * HD:(k + 1) * HD] = o_inter[k] + o_intra[k]
                s_scr[k] = sts[k] * _head(e["ebl"], k) + kv[k]
            return carry

        lax.fori_loop(0, ncb, chunk, 0)

    return pl.pallas_call(
        body, grid=(HEADS // hg, t // rb),
        in_specs=[pl.BlockSpec((rb, 3 * HD * hg), lambda h, i: (i, h)), pl.BlockSpec((2, HD * hg), lambda h, i: (0, h))],
        out_specs=[pl.BlockSpec((rb, HD * hg), lambda h, i: (i, h)),
                   pl.BlockSpec((ncb, hg, HD, HD), lambda h, i: (i, h, 0, 0))],
        out_shape=[jax.ShapeDtypeStruct((t, D), F32), jax.ShapeDtypeStruct((t // CH, HEADS, HD, HD), F32)],
        scratch_shapes=[pltpu.VMEM((hg, HD, HD), F32)],
        name="hgrn_fwd", compiler_params=_cp(48),
    )(proj, lbw)


def _hgrn_bwd(proj, lbw, do_raw, states, rb, hg):
    assert hg == HEADS
    t = proj.shape[0]
    nblk = t // rb
    ncb = rb // CH
    wd = HD * hg

    def body(p_ref, lb_ref, do_ref, st_ref, dp_ref, dlb_ref, ds_scr):
        @pl.when(pl.program_id(1) == 0)
        def _():
            ds_scr[...] = jnp.zeros_like(ds_scr)
            dlb_ref[...] = jnp.zeros_like(dlb_ref)

        lb = _lower_bound(lb_ref)
        causal = _tri(True)
        tril = causal.astype(F32)
        triu = _tri(False).astype(F32)
        last_row = lax.broadcasted_iota(jnp.int32, (CH, HD * hg), 0) == CH - 1
        row0 = lax.broadcasted_iota(jnp.int32, (8, HD * hg), 0) == 0
        heads = range(hg)
        wide = lambda parts: jnp.concatenate(parts, axis=1)

        def chunk(it, carry):
            cc = ncb - 1 - it
            r0 = pl.multiple_of(cc * CH, CH)
            rows = pl.ds(r0, CH)
            hq = _head_segment(p_ref, rows, 0, hg)
            e = _hgrn_chunk_fwd(hq, _head_segment(p_ref, rows, 1, hg), lb, tril)
            v = _bf(_head_segment(p_ref, rows, 2, hg))
            do = _bf(do_ref[rows, :])
            sts = [st_ref[cc, k] for k in heads]
            dsts = [ds_scr[k] for k in heads]
            dlb_acc = dlb_ref[...]
            qa, ka, qe, kd = _bf(e["qa"]), _bf(e["ka"]), _bf(e["qe"]), _bf(e["kd"])
            a = [_bf(jnp.where(causal, _mm_nt(_head(qa, k), _head(ka, k)), 0.0)) for k in heads]
            da = [_bf(jnp.where(causal, _mm_nt(_head(do, k), _head(v, k)), 0.0)) for k in heads]
            dqe = wide([_mm(_head(do, k), _bf(sts[k])) for k in heads])
            dkd = wide([_mm(_head(v, k), _bf(dsts[k])) for k in heads])
            dv_state = [_mm_nt(_head(kd, k), _bf(dsts[k])) for k in heads]
            ds_new = [_mm_tn(_head(do, k), _head(qe, k)) for k in heads]
            dv_intra = [_mm_tn(a[k], _head(do, k)) for k in heads]
            dqa = wide([_mm(da[k], _head(ka, k)) for k in heads])
            dka = wide([_mm_tn(da[k], _head(qa, k)) for k in heads])
            dv = wide([dv_intra[k] + dv_state[k] for k in heads])
            dbl = e["ebl"] * wide([jnp.sum(sts[k] * dsts[k], axis=0, keepdims=True) for k in heads])
            dq = dqe * e["eb"] + dqa * e["ea"]
            dk = dka * e["ek"] + dkd * e["ed"]
            dkd_kd = dkd * kd.astype(F32)
            db = dqe * qe.astype(F32) + dqa * qa.astype(F32) - dka * ka.astype(F32) - dkd_kd
            db = db + jnp.where(last_row, dbl + jnp.sum(dkd_kd, axis=0, keepdims=True), 0.0)
            dg = _mm_exact(triu, db)
            df = dg / e["f"] - dk
            sg = e["sg"]
            sq = e["sq"]
            dhq = _bf(dq * (sq * (1.0 + hq * (1.0 - sq))))
            dhf = _bf(df * (1.0 - lb) * sg * (1.0 - sg))
            dhi = _bf(dv)
            dlb_new = dlb_acc + jnp.where(row0, jnp.sum(df * (1.0 - sg), axis=0, keepdims=True), 0.0)
            for k in heads:
                ds_scr[k] = ds_new[k] + dsts[k] * _head(e["ebl"], k)
            dp_ref[rows, 0:wd] = dhq
            dp_ref[rows, wd:2 * wd] = dhf
            dp_ref[rows, 2 * wd:3 * wd] = dhi
            dlb_ref[...] = dlb_new
            return carry

        lax.fori_loop(0, ncb, chunk, 0)

    rev = lambda h, i: (nblk - 1 - i, h)
    return pl.pallas_call(
        body, grid=(HEADS // hg, nblk),
        in_specs=[pl.BlockSpec((rb, 3 * HD * hg), rev), pl.BlockSpec((2, HD * hg), lambda h, i: (0, h)),
                  pl.BlockSpec((rb, HD * hg), rev), pl.BlockSpec((ncb, hg, HD, HD), lambda h, i: (nblk - 1 - i, h, 0, 0))],
        out_specs=[pl.BlockSpec((rb, 3 * HD * hg), rev), pl.BlockSpec((8, HD * hg), lambda h, i: (0, h))],
        out_shape=[jax.ShapeDtypeStruct((t, 3 * D), BF16), jax.ShapeDtypeStruct((8, D), F32)],
        scratch_shapes=[pltpu.VMEM((hg, HD, HD), F32)],
        name="hgrn_bwd", compiler_params=_cp(48),
    )(proj, lbw, do_raw, states)


def _kv_variants(tile, odd):
    low = lax.broadcasted_iota(jnp.int32, tile.shape, 1) < 64
    if odd:
        hi = jnp.where(low, 0.0, tile)
        lo = pltpu.roll(hi, 64, 1)
    else:
        lo = jnp.where(low, tile, 0.0)
        hi = pltpu.roll(lo, 64, 1)
    return _bf(lo), _bf(hi)


def _attn_masks(n):
    qi = lax.broadcasted_iota(jnp.int32, (AB, AB), 0)
    kj = lax.broadcasted_iota(jnp.int32, (AB, AB), 1)
    cur = kj <= qi
    return cur, cur | (n > 0), qi <= kj


def _kv_all(prev_ref, cur_ref):
    out = []
    for tl in range(2):
        cols = slice(tl * 128, (tl + 1) * 128)
        tile = jnp.concatenate([prev_ref[:, cols], cur_ref[:, cols]], axis=0)
        out.append(_kv_variants(tile, 0))
        out.append(_kv_variants(tile, 1))
    return out


def _window(a2, cur):
    return jnp.where(cur, a2[:, AB:], a2[:, :AB])


def _attn_softmax(scores, sinks, cur, ok):
    s = [jnp.where(ok, _window(s2, cur) * ATT_SCALE, NEG) for s2 in scores]
    m = [jnp.maximum(jnp.max(si, axis=-1, keepdims=True), sink) for si, sink in zip(s, sinks)]
    p = [jnp.exp(si - mi) for si, mi in zip(s, m)]
    es = [jnp.exp(sink - mi) for sink, mi in zip(sinks, m)]
    inv = [1.0 / (jnp.sum(pi, axis=-1, keepdims=True) + ei) for pi, ei in zip(p, es)]
    return [pi * ii for pi, ii in zip(p, inv)], [ei * ii for ei, ii in zip(es, inv)]


def _spread(pc, cur):
    return jnp.concatenate([jnp.where(cur, 0.0, pc), jnp.where(cur, pc, 0.0)], axis=1)


def _spread_t(pct, cur_t):
    return jnp.concatenate([jnp.where(cur_t, 0.0, pct), jnp.where(cur_t, pct, 0.0)], axis=0)


def _attn_fwd(proj, sinks):
    t = proj.shape[0]
    nb = t // AB

    def body(q_ref, kc_ref, kp_ref, vc_ref, vp_ref, sink_ref, o_ref):
        cur, ok, _ = _attn_masks(pl.program_id(0))
        kvars = _kv_all(kp_ref, kc_ref)
        vvars = _kv_all(vp_ref, vc_ref)
        qps = [_bf(q_ref[:, 128 * j:128 * (j + 1)]) for j in range(8)]
        heads = [(j, ab) for j in range(8) for ab in range(2)]
        scores = [_mm_nt(qps[j], kvars[j // 2][ab]) for j, ab in heads]
        pcs, _ = _attn_softmax(scores, [sink_ref[0, h] for h in range(QH)], cur, ok)
        parts = [_mm(_bf(_spread(pcs[h], cur)), vvars[j // 2][ab]) for h, (j, ab) in enumerate(heads)]
        for j in range(8):
            o_ref[:, 128 * j:128 * (j + 1)] = parts[2 * j] + parts[2 * j + 1]

    prev = lambda n: jnp.maximum(n - 1, 0)
    return pl.pallas_call(
        body, grid=(nb,),
        in_specs=[pl.BlockSpec((AB, D), lambda n: (n, C_AQ // D)),
                  pl.BlockSpec((AB, 256), lambda n: (n, C_AK // 256)),
                  pl.BlockSpec((AB, 256), lambda n: (prev(n), C_AK // 256)),
                  pl.BlockSpec((AB, 256), lambda n: (n, C_AV // 256)),
                  pl.BlockSpec((AB, 256), lambda n: (prev(n), C_AV // 256)),
                  pl.BlockSpec(memory_space=pltpu.SMEM)],
        out_specs=pl.BlockSpec((AB, D), lambda n: (n, 0)),
        out_shape=jax.ShapeDtypeStruct((t, D), F32),
        name="attn_fwd", compiler_params=_cp(32),
    )(proj, proj, proj, proj, proj, sinks)


def _attn_bwd(proj, sinks, do_a):
    t = proj.shape[0]
    nb = t // AB

    def body(q_ref, kc_ref, kp_ref, vc_ref, vp_ref, do_ref, sink_ref, dq_ref, dkv_ref, dsink_ref, carry):
        n = pl.program_id(0)

        @pl.when(n == 0)
        def _():
            dsink_ref[...] = jnp.zeros_like(dsink_ref)
            carry[...] = jnp.zeros_like(carry)

        @pl.when(n < nb)
        def _():
            cur, ok, cur_t = _attn_masks(n)
            low = lax.broadcasted_iota(jnp.int32, (2 * AB, 128), 1) < 64
            lane = lax.broadcasted_iota(jnp.int32, (8, 128), 1)
            row0 = lax.broadcasted_iota(jnp.int32, (8, 128), 0) == 0
            kvars = _kv_all(kp_ref, kc_ref)
            vvars = _kv_all(vp_ref, vc_ref)
            qps = [_bf(q_ref[:, 128 * j:128 * (j + 1)]) for j in range(8)]
            dops = [_bf(do_ref[:, 128 * j:128 * (j + 1)]) for j in range(8)]
            heads = [(j, ab) for j in range(8) for ab in range(2)]
            scores = [_mm_nt(qps[j], kvars[j // 2][ab]) for j, ab in heads]
            dps = [_mm_nt(dops[j], vvars[j // 2][ab]) for j, ab in heads]
            pcs, pss = _attn_softmax(scores, [sink_ref[0, h] for h in range(QH)], cur, ok)
            dpcs = [_window(dp2, cur) for dp2 in dps]
            rss = [jnp.sum(pc * dpc, axis=-1, keepdims=True) for pc, dpc in zip(pcs, dpcs)]
            dscs = [pc * (dpc - rs) for pc, dpc, rs in zip(pcs, dpcs, rss)]
            dsink = jnp.zeros((8, 128), F32)
            for h in range(QH):
                dsink = dsink + jnp.where(row0 & (lane == h), -jnp.sum(pss[h] * rss[h]), 0.0)
            dq_terms = [_mm(_bf(_spread(dscs[h], cur)), kvars[j // 2][ab]) for h, (j, ab) in enumerate(heads)]
            for j in range(8):
                dq_ref[:, 128 * j:128 * (j + 1)] = _bf((dq_terms[2 * j] + dq_terms[2 * j + 1]) * ATT_SCALE)
            dsc_t = [_bf(_spread_t(dsc.T, cur_t)) for dsc in dscs]
            pc_t = [_bf(_spread_t(pc.T, cur_t)) for pc in pcs]
            dk_terms = [_mm(dsc_t[h], qps[j]) for h, (j, ab) in enumerate(heads)]
            dv_terms = [_mm(pc_t[h], dops[j]) for h, (j, ab) in enumerate(heads)]
            dk_ab = [[dk_terms[4 * g + ab] + dk_terms[4 * g + 2 + ab] for ab in range(2)] for g in range(4)]
            dv_ab = [[dv_terms[4 * g + ab] + dv_terms[4 * g + 2 + ab] for ab in range(2)] for g in range(4)]
            for tl in range(2):
                ke, ko = dk_ab[2 * tl], dk_ab[2 * tl + 1]
                ve, vo = dv_ab[2 * tl], dv_ab[2 * tl + 1]
                dkt = (jnp.where(low, ke[0], 0.0) + pltpu.roll(jnp.where(low, 0.0, ke[1]), 64, 1)
                       + jnp.where(low, 0.0, ko[1]) + pltpu.roll(jnp.where(low, ko[0], 0.0), 64, 1)) * ATT_SCALE
                dvt = (jnp.where(low, ve[0], 0.0) + pltpu.roll(jnp.where(low, 0.0, ve[1]), 64, 1)
                       + jnp.where(low, 0.0, vo[1]) + pltpu.roll(jnp.where(low, vo[0], 0.0), 64, 1))
                kcols = slice(tl * 128, (tl + 1) * 128)
                vcols = slice(256 + tl * 128, 256 + (tl + 1) * 128)
                dkv_ref[:, kcols] = _bf(carry[:, kcols] + dkt[0:AB])
                dkv_ref[:, vcols] = _bf(carry[:, vcols] + dvt[0:AB])
                carry[:, kcols] = dkt[AB:2 * AB]
                carry[:, vcols] = dvt[AB:2 * AB]
            dsink_ref[...] += dsink

        @pl.when(n == nb)
        def _():
            dkv_ref[...] = _bf(carry[...])

    cur = lambda n: jnp.minimum(n, nb - 1)
    prev = lambda n: jnp.clip(n - 1, 0, nb - 1)
    return pl.pallas_call(
        body, grid=(nb + 1,),
        in_specs=[pl.BlockSpec((AB, D), lambda n: (cur(n), C_AQ // D)),
                  pl.BlockSpec((AB, 256), lambda n: (cur(n), C_AK // 256)),
                  pl.BlockSpec((AB, 256), lambda n: (prev(n), C_AK // 256)),
                  pl.BlockSpec((AB, 256), lambda n: (cur(n), C_AV // 256)),
                  pl.BlockSpec((AB, 256), lambda n: (prev(n), C_AV // 256)),
                  pl.BlockSpec((AB, D), lambda n: (cur(n), 0)),
                  pl.BlockSpec(memory_space=pltpu.SMEM)],
        out_specs=[pl.BlockSpec((AB, D), lambda n: (cur(n), 0)),
                   pl.BlockSpec((AB, 512), lambda n: (prev(n), 0)),
                   pl.BlockSpec((8, 128), lambda n: (0, 0))],
        out_shape=[jax.ShapeDtypeStruct((t, D), BF16), jax.ShapeDtypeStruct((t, 512), BF16),
                   jax.ShapeDtypeStruct((8, 128), F32)],
        scratch_shapes=[pltpu.VMEM((AB, 512), F32)],
        name="attn_bwd", compiler_params=_cp(32),
    )(proj, proj, proj, proj, proj, do_a, sinks)


def _silu_and_grad(v):
    s = _sigmoid(v)
    return v * s, s * (1.0 + v * (1.0 - s))


def _tail(o_raw, o_a, proj, x2d, tgt, wbh, wba, wout, hnw, fnw, tb):
    t = x2d.shape[0]

    def body(or_ref, oa_ref, hg_ref, ag0, ag1, mh0, mh1, ma0, ma1, x_ref, t_ref, wbh_ref, wba_ref, wout_ref, hnw_ref,
             fnw_ref, dx2_ref, dor_ref, doa_ref, dhg_ref, dagm_ref, gh_ref, ga_ref, mg_ref, dyh_ref, dya_ref, dx2b_ref,
             sums_ref):
        @pl.when(pl.program_id(0) == 0)
        def _():
            sums_ref[...] = jnp.zeros_like(sums_ref)

        halves = lambda a, b: jnp.concatenate([a[...], b[...]], axis=1)
        hnw_v = hnw_ref[...]
        fnw_v = fnw_ref[...]
        o = or_ref[...]
        rs, xhs = [], []
        for h in range(HEADS):
            oh = o[:, h * HD:(h + 1) * HD]
            r = lax.rsqrt(jnp.mean(oh * oh, axis=-1, keepdims=True) + EPS)
            rs.append(r)
            xhs.append(oh * r)
        xh = jnp.concatenate(xhs, axis=1)
        on = xh * hnw_v
        sil_hg, dsil_hg = _silu_and_grad(hg_ref[...])
        gh_b = _bf(on * sil_hg)
        y_h = _mm(gh_b, wbh_ref[...])
        oa = oa_ref[...]
        sil_ag, dsil_ag = _silu_and_grad(halves(ag0, ag1))
        ga_b = _bf(oa * sil_ag)
        y_a = _mm(ga_b, wba_ref[...])
        s_mh = _sigmoid(halves(mh0, mh1))
        s_ma = _sigmoid(halves(ma0, ma1))
        mg_b = _bf(s_mh * y_h + s_ma * y_a)
        x2 = x_ref[...] + _mm(mg_b, wout_ref[...])
        r2 = lax.rsqrt(jnp.mean(x2 * x2, axis=-1, keepdims=True) + EPS)
        xh2 = x2 * r2
        err = xh2 * fnw_v - t_ref[...]
        loss = 0.5 * jnp.sum(jnp.mean(err * err, axis=-1, keepdims=True))
        dy = err * (1.0 / D)
        dfnw = jnp.sum(dy * xh2, axis=0, keepdims=True)
        dxh2 = dy * fnw_v
        dx2 = r2 * (dxh2 - xh2 * jnp.mean(dxh2 * xh2, axis=-1, keepdims=True))
        dx2_ref[...] = dx2
        dx2_b = _bf(dx2)
        dmg = _mm_nt(dx2_b, wout_ref[...])
        dmg_h = dmg * s_mh
        dmg_a = dmg * s_ma
        dyh_b = _bf(dmg_h)
        dya_b = _bf(dmg_a)
        dagm_ref[:, D:2 * D] = _bf(dmg_h * y_h * (1.0 - s_mh))
        dagm_ref[:, 2 * D:3 * D] = _bf(dmg_a * y_a * (1.0 - s_ma))
        dgh = _mm_nt(dyh_b, wbh_ref[...])
        dga = _mm_nt(dya_b, wba_ref[...])
        doa_ref[...] = dga * sil_ag
        dagm_ref[:, 0:D] = _bf(dga * oa * dsil_ag)
        dhg_ref[...] = _bf(dgh * on * dsil_hg)
        don = dgh * sil_hg
        dhnw = jnp.sum(don * xh, axis=0, keepdims=True)
        dxh = don * hnw_v
        dos = []
        for h in range(HEADS):
            sl = slice(h * HD, (h + 1) * HD)
            dos.append(rs[h] * (dxh[:, sl] - xhs[h] * jnp.mean(dxh[:, sl] * xhs[h], axis=-1, keepdims=True)))
        dor_ref[...] = jnp.concatenate(dos, axis=1)
        gh_ref[...] = gh_b
        ga_ref[...] = ga_b
        mg_ref[...] = mg_b
        dyh_ref[...] = dyh_b
        dya_ref[...] = dya_b
        dx2b_ref[...] = dx2_b
        row = lax.broadcasted_iota(jnp.int32, (8, D), 0)
        sums_ref[...] += jnp.where(row == 0, dfnw, 0.0) + jnp.where(row == 1, dhnw, 0.0) + jnp.where(row == 2, loss, 0.0)

    rowblk = lambda c: pl.BlockSpec((tb, D), lambda i: (i, c))
    half = lambda c: pl.BlockSpec((tb, 512), lambda i: (i, c))
    full = lambda shape: pl.BlockSpec(shape, lambda i: (0, 0))
    return pl.pallas_call(
        body, grid=(t // tb,),
        in_specs=[rowblk(0), rowblk(0), rowblk(C_HG // D), half(C_AG // 512), half(C_AG // 512 + 1), half(C_MH // 512),
                  half(C_MH // 512 + 1), half(C_MA // 512), half(C_MA // 512 + 1), rowblk(0), rowblk(0),
                  full((D, D)), full((D, D)), full((D, D)), full((1, D)), full((1, D))],
        out_specs=[rowblk(0), rowblk(0), rowblk(0), rowblk(0), pl.BlockSpec((tb, 3 * D), lambda i: (i, 0))]
        + [rowblk(0)] * 6 + [full((8, D))],
        out_shape=[jax.ShapeDtypeStruct((t, D), F32)] * 3
        + [jax.ShapeDtypeStruct((t, D), BF16), jax.ShapeDtypeStruct((t, 3 * D), BF16)]
        + [jax.ShapeDtypeStruct((t, D), BF16)] * 6 + [jax.ShapeDtypeStruct((8, D), F32)],
        name="tail", compiler_params=_cp(56),
    )(o_raw, o_a, proj, proj, proj, proj, proj, proj, proj, x2d, tgt, wbh, wba, wout, hnw, fnw)


def _wgrad3(gh, dyh, ga, dya, mg, dx2b, tk):
    t = dyh.shape[0]

    def body(a0, b0, a1, b1, a2, b2, o0, o1, o2):
        @pl.when(pl.program_id(0) == 0)
        def _():
            o0[...] = jnp.zeros_like(o0)
            o1[...] = jnp.zeros_like(o1)
            o2[...] = jnp.zeros_like(o2)

        o0[...] += _mm_tn(a0[...], b0[...])
        o1[...] += _mm_tn(a1[...], b1[...])
        o2[...] += _mm_tn(a2[...], b2[...])

    blk = pl.BlockSpec((tk, D), lambda k: (k, 0))
    out = pl.BlockSpec((D, D), lambda k: (0, 0))
    return pl.pallas_call(
        body, grid=(t // tk,), in_specs=[blk] * 6, out_specs=[out] * 3,
        out_shape=[jax.ShapeDtypeStruct((D, D), F32)] * 3,
        name="wgrad3", compiler_params=_cp(48),
    )(gh, dyh, ga, dya, mg, dx2b)


def _inproj_wgrad_piece(xnt, piece, nb, name):
    t = xnt.shape[1]
    width = piece.shape[1]

    def body(xnt_ref, p_ref, o_ref):
        o_ref[...] = _mm(xnt_ref[...], p_ref[...])

    return pl.pallas_call(
        body, grid=(width // nb,),
        in_specs=[pl.BlockSpec((D, t), lambda j: (0, 0), pipeline_mode=pl.Buffered(1)),
                  pl.BlockSpec((t, nb), lambda j: (0, j))],
        out_specs=pl.BlockSpec((D, nb), lambda j: (0, j)),
        out_shape=jax.ShapeDtypeStruct((D, width), F32),
        name=name, compiler_params=_cp(56),
    )(xnt, piece)


def _inproj_dgrad(pieces, w_p, x2d, dx2, norm_w, tb, after):
    t = x2d.shape[0]

    def body(*refs):
        piece_refs = refs[:len(pieces)]
        w_ref, x_ref, dx2_ref, nw_ref, _, gx_ref, dnw_ref = refs[len(pieces):]

        @pl.when(pl.program_id(0) == 0)
        def _():
            dnw_ref[...] = jnp.zeros_like(dnw_ref)

        dxn = None
        off = 0
        for p in piece_refs:
            width = p.shape[1]
            term = _mm_nt(p[...], w_ref[:, off:off + width])
            dxn = term if dxn is None else dxn + term
            off += width
        xv = x_ref[...]
        r = lax.rsqrt(jnp.mean(xv * xv, axis=-1, keepdims=True) + EPS)
        xh = xv * r
        dxh = dxn * nw_ref[...]
        gx_ref[...] = dx2_ref[...] + r * (dxh - xh * jnp.mean(dxh * xh, axis=-1, keepdims=True))
        row0 = lax.broadcasted_iota(jnp.int32, (8, D), 0) == 0
        dnw_ref[...] += jnp.where(row0, jnp.sum(dxn * xh, axis=0, keepdims=True), 0.0)

    rowblk = pl.BlockSpec((tb, D), lambda i: (i, 0))
    return pl.pallas_call(
        body, grid=(t // tb,),
        in_specs=[pl.BlockSpec((tb, p.shape[1]), lambda i: (i, 0)) for p in pieces]
        + [pl.BlockSpec((D, DIN), lambda i: (0, 0), pipeline_mode=pl.Buffered(1)), rowblk, rowblk,
           pl.BlockSpec((1, D), lambda i: (0, 0)), pl.BlockSpec(memory_space=pl.ANY)],
        out_specs=[rowblk, pl.BlockSpec((8, D), lambda i: (0, 0))],
        out_shape=[jax.ShapeDtypeStruct((t, D), F32), jax.ShapeDtypeStruct((8, D), F32)],
        name="inproj_dgrad", compiler_params=_cp(60),
    )(*pieces, w_p, x2d, dx2, norm_w, after)


def _adamw_math(w, g, m, v):
    m = B1 * m + (1.0 - B1) * g
    v = B2 * v + (1.0 - B2) * (g * g)
    m_hat = m / (1.0 - B1 ** STEP)
    v_hat = v / (1.0 - B2 ** STEP)
    delta = -LR * (m_hat / (jnp.sqrt(v_hat) + ADAM_EPS) + WD * w)
    return delta, m, v


def _adamw_shard(recv, sums, chip, w, m, v, rows, name):
    nparts, nr, nc = recv.shape

    def body(chip_ref, own_ref, p_ref, w_ref, m_ref, v_ref, g_ref, d_ref, nm_ref, nv_ref):
        g = own_ref[0].astype(F32)
        for s in range(nparts):
            g = g + p_ref[s].astype(F32)
        d, nm, nv = _adamw_math(w_ref[...], g, m_ref[...], v_ref[...])
        g_ref[...] = g
        d_ref[...] = d
        nm_ref[...] = nm
        nv_ref[...] = nv

    blk = pl.BlockSpec((rows, nc), lambda i, chip_ref: (i, 0))
    return pl.pallas_call(
        body,
        grid_spec=pltpu.PrefetchScalarGridSpec(
            num_scalar_prefetch=1, grid=(nr // rows,),
            in_specs=[pl.BlockSpec((1, rows, nc), lambda i, chip_ref: (chip_ref[0], i, 0)),
                      pl.BlockSpec((nparts, rows, nc), lambda i, chip_ref: (0, i, 0)), blk, blk, blk],
            out_specs=[blk] * 4),
        out_shape=[jax.ShapeDtypeStruct((nr, nc), F32)] * 4,
        name=name, compiler_params=_cp(48),
    )(chip, sums, recv, w, m, v)


SMALL_ROWS = dict(norm_w=0, lower_bound=1, hgrn_norm_w=3, final_norm_w=4, sinks=5)


def _pack_small_grads(dnw, dlb, sums, dsink):
    def body(dnw_ref, dlb_ref, sums_ref, dsink_ref, o_ref):
        o_ref[...] = jnp.zeros_like(o_ref)
        o_ref[0:1, :] = dnw_ref[0:1, :]
        o_ref[1:2, :] = dlb_ref[0:1, :]
        o_ref[3:4, :] = sums_ref[1:2, :]
        o_ref[4:5, :] = sums_ref[0:1, :]
        o_ref[5:6, 0:128] = dsink_ref[0:1, :]

    return pl.pallas_call(body, out_shape=jax.ShapeDtypeStruct((8, D), F32), name="pack_small_grads")(dnw, dlb, sums, dsink)


def _adamw_small(parts, ws, ms, vs):
    shapes = [a.shape for a in ws]

    def body(p_ref, *refs):
        w, m, v = refs[0:5], refs[5:10], refs[10:15]
        outs = [refs[15 + 5 * i:20 + 5 * i] for i in range(4)]

        def total(row, width):
            g = p_ref[0, row:row + 1, 0:width]
            for s in range(1, NDEV):
                g = g + p_ref[s, row:row + 1, 0:width]
            return g

        lb = _lower_bound(w[1])
        ga0 = total(1, D) * lb * (1.0 - lb)
        grads = [total(0, D), None, total(3, D), total(4, D), total(5, QH)]
        for i in (0, 2, 3, 4):
            res = (grads[i],) + _adamw_math(w[i][...], grads[i], m[i][...], v[i][...])
            for o, val in zip(outs, res):
                o[i][...] = val
        for r, g in ((0, ga0), (1, -ga0)):
            res = (g,) + _adamw_math(w[1][r:r + 1, :], g, m[1][r:r + 1, :], v[1][r:r + 1, :])
            for o, val in zip(outs, res):
                o[1][r:r + 1, :] = val

    res = pl.pallas_call(
        body, out_shape=[jax.ShapeDtypeStruct(s, F32) for s in shapes] * 4, name="adamw_small",
    )(parts, *ws, *ms, *vs)
    return [res[5 * i:5 * i + 5] for i in range(4)]


def kernel(x, norm_w, w_in, hgrn_lower_bound, hgrn_norm_w, w_branch_hgrn, attn_sinks, w_branch_attn, w_out, final_norm_w, loss_target, m_norm_w, m_w_in, m_hgrn_lower_bound, m_hgrn_norm_w, m_w_branch_hgrn, m_attn_sinks, m_w_branch_attn, m_w_out, m_final_norm_w, v_norm_w, v_w_in, v_hgrn_lower_bound, v_hgrn_norm_w, v_w_branch_hgrn, v_attn_sinks, v_w_branch_attn, v_w_out, v_final_norm_w):
    t = x.shape[1]
    x2d = x.reshape(t, D)
    tgt = loss_target.reshape(t, D)
    fnw = final_norm_w.reshape(1, D)
    row_blk = min(256, t)
    big_blk = min(512, t)

    win_g, wbh_g, wba_g, wout_g = _gather_weights(w_in[0], w_branch_hgrn[0], w_branch_attn[0], w_out[0])
    w_p = pltpu.with_memory_space_constraint(win_g.transpose(1, 0, 2).reshape(D, DIN), pltpu.HBM)
    wbh, wba, wout = wbh_g.reshape(D, D), wba_g.reshape(D, D), wout_g.reshape(D, D)

    proj, xnt = _inproj_fwd(x2d, norm_w, w_p, min(1024, t), 2176)
    o_raw, states = _hgrn_fwd(proj, hgrn_lower_bound, big_blk, HGRN_GROUP)
    o_a = _attn_fwd(proj, attn_sinks)
    (dx2, do_raw, do_a, d_hg, d_agm, gh, ga, mg, dyh, dya, dx2b, sums) = _tail(
        o_raw, o_a, proj, x2d, tgt, wbh, wba, wout, hgrn_norm_w, fnw, row_blk)
    dwbh, dwba, dwout = _wgrad3(gh, dyh, ga, dya, mg, dx2b, big_blk)
    d_aq, d_kv, dsink = _attn_bwd(proj, attn_sinks, do_a)
    d_hgrn, dlb = _hgrn_bwd(proj, hgrn_lower_bound, do_raw, states, big_blk, HGRN_GROUP)
    pieces = (d_hgrn, d_hg, d_aq, d_kv, d_agm)
    dw_pieces = [_inproj_wgrad_piece(xnt, p, CB, "inproj_wgrad_" + n)
                 for p, n in zip(pieces, ("hgrn", "hgate", "aq", "kv", "gates"))]

    dwin_r = jnp.concatenate(dw_pieces, axis=1).reshape(D, NDEV, IN_SHARD).transpose(1, 0, 2).astype(BF16)
    slots = lambda a: a.reshape(NDEV, ROW_SHARD, D).astype(BF16)
    partials = [dwin_r, slots(dwbh), slots(dwba), slots(dwout)]
    got = _exchange_pair(partials)
    core = lax.axis_index("c").astype(jnp.int32).reshape(1)
    pair_sums = [_pair_sum(a, g, core, ROW_SHARD, "pair_sum_" + n)
                 for a, g, n in zip(partials, got, ("w_in", "w_bh", "w_ba", "w_out"))]
    rin, rbh, rba, rout = _exchange_chips(pair_sums)
    grad_x, dnw = _inproj_dgrad(pieces, w_p, x2d, dx2, norm_w, big_blk, after=pair_sums[0])
    rsm = _exchange_small(_pack_small_grads(dnw, dlb, sums, dsink))
    chip = (2 * lax.axis_index("x") + lax.axis_index("y")).astype(jnp.int32).reshape(1)
    s_in, s_bh, s_ba, s_out = pair_sums
    g_in, d_in, nm_in, nv_in = _adamw_shard(rin, s_in, chip, w_in[0], m_w_in[0], v_w_in[0], 128, "adamw_w_in")
    g_bh, d_bh, nm_bh, nv_bh = _adamw_shard(
        rbh, s_bh, chip, w_branch_hgrn[0], m_w_branch_hgrn[0], v_w_branch_hgrn[0], 128, "adamw_w_bh")
    g_ba, d_ba, nm_ba, nv_ba = _adamw_shard(
        rba, s_ba, chip, w_branch_attn[0], m_w_branch_attn[0], v_w_branch_attn[0], 128, "adamw_w_ba")
    g_out, d_out, nm_out, nv_out = _adamw_shard(rout, s_out, chip, w_out[0], m_w_out[0], v_w_out[0], 128, "adamw_w_out")
    sg, sd, sm, sv = _adamw_small(
        rsm,
        (norm_w, hgrn_lower_bound, hgrn_norm_w, fnw, attn_sinks),
        (m_norm_w, m_hgrn_lower_bound, m_hgrn_norm_w, m_final_norm_w.reshape(1, D), m_attn_sinks),
        (v_norm_w, v_hgrn_lower_bound, v_hgrn_norm_w, v_final_norm_w.reshape(1, D), v_attn_sinks))

    loss = lax.psum(sums[2, 0], ("x", "y", "c"))

    def group(s, w_in_v, bh, ba, out):
        nw, lb, hnw, fn, sinks = s
        return (nw, w_in_v[None], lb, hnw, bh[None], sinks, ba[None], out[None], fn.reshape(D))

    return (loss, grad_x.reshape(1, t, D),
            *group(sg, g_in, g_bh, g_ba, g_out), *group(sd, d_in, d_bh, d_ba, d_out),
            *group(sm, nm_in, nm_bh, nm_ba, nm_out), *group(sv, nv_in, nv_bh, nv_ba, nv_out))
```

```python
import functools

import jax
import jax.numpy as jnp
from jax import lax
from jax.experimental import pallas as pl
from jax.experimental.pallas import tpu as pltpu
from jax.experimental.pallas import tpu_sc as plsc

F32 = jnp.float32
BF16 = jnp.bfloat16

D = 1024
DIN = 8704
NDEV = 8
IN_SHARD = DIN // NDEV
PAIR = 2 * IN_SHARD
ROW_SHARD = D // NDEV
HEADS = 8
HD = 128
CH = 64
HGRN_GROUP = 8
QH = 16
AB = 128
EPS = 1e-6
NEG = -1e30
ATT_SCALE = 0.125

C_HGRN = 0
C_HG = 3072
C_AQ = 4096
C_AK = 5120
C_AV = 5376
C_AG = 5632
C_MH = 6656
C_MA = 7680
CB = 512

LR = 0.001
B1 = 0.9
B2 = 0.999
ADAM_EPS = 1e-08
WD = 0.01
STEP = 10

V7X_VMEM_BYTES = 64 * 1024 * 1024
MESH = pl.DeviceIdType.MESH


def _cp(vmem_mb):
    return pltpu.CompilerParams(vmem_limit_bytes=vmem_mb * 1024 * 1024)


def _mm(a, b):
    return jnp.dot(a, b, preferred_element_type=F32)


def _mm_nt(a, b):
    return lax.dot_general(a, b, (((1,), (1,)), ((), ())), preferred_element_type=F32)


def _mm_tn(a, b):
    return lax.dot_general(a, b, (((0,), (0,)), ((), ())), preferred_element_type=F32)


def _mm_exact(a, b):
    return jnp.dot(a, b, preferred_element_type=F32, precision=lax.Precision.HIGHEST)


def _sigmoid(v):
    return 0.5 * jnp.tanh(0.5 * v) + 0.5


def _bf(v):
    return v.astype(BF16)


def _place():
    x, y, c = lax.axis_index("x"), lax.axis_index("y"), lax.axis_index("c")
    return (x, y, c), (x, y, 1 - c), [(1 - x, y), (x, 1 - y), (1 - x, 1 - y)]


def _dev_index(px, py, pc):
    return 4 * px + 2 * py + pc


def _gather_weights(w_in_s, wbh_s, wba_s, wout_s):
    half = D // 2

    def body(win_ref, wbh_ref, wba_ref, wout_ref, wp_g, wbh_g, wba_g, wout_g, give, take, mine, s3,
             send_sems, recv_sems, loc_sems, swap_sems):
        (x, y, c), sibling, chips = _place()
        give[...] = win_ref[pl.ds(pl.multiple_of(half * (1 - c), half), half), :].astype(BF16)
        swap = pltpu.make_async_remote_copy(src_ref=give, dst_ref=take, send_sem=swap_sems.at[0], recv_sem=swap_sems.at[1],
                                            device_id=sibling, device_id_type=MESH)
        swap.start()
        s3[0] = wbh_ref[...].astype(BF16)
        s3[1] = wba_ref[...].astype(BF16)
        s3[2] = wout_ref[...].astype(BF16)
        swap.wait()
        own = win_ref[pl.ds(pl.multiple_of(half * c, half), half), :]
        other = take[...].astype(F32)
        mine[...] = jnp.where(c == 0, jnp.concatenate([own, other], axis=1),
                              jnp.concatenate([other, own], axis=1)).astype(BF16)
        srcs = [mine, s3.at[0], s3.at[1], s3.at[2]]
        outs = [wp_g, wbh_g, wba_g, wout_g]

        def place(k, px, py, pc):
            if k == 0:
                return wp_g.at[2 * px + py, pl.ds(pl.multiple_of(half * pc, half), half), :]
            return outs[k].at[_dev_index(px, py, pc)]

        def copy(kind, k, origin, to, src=None):
            return pltpu.make_async_remote_copy(
                src_ref=srcs[k] if src is None else src, dst_ref=place(k, *origin), send_sem=send_sems.at[kind, k],
                recv_sem=recv_sems.at[kind, k], device_id=to, device_id_type=MESH)

        me = (x, y, c)
        local = [pltpu.make_async_copy(srcs[k], place(k, *me), loc_sems.at[k]) for k in range(4)]
        for cp in local:
            cp.start()
        first = [copy(0, k, me, sibling) for k in range(4)]
        first += [copy(1 + j, k, me, (*chip, c)) for j, chip in enumerate(chips) for k in range(4)]
        for cp in first:
            cp.start()
        passed = []
        for j, chip in enumerate(chips):
            for k in range(4):
                copy(1 + j, k, (*chip, c), me).wait_recv()
            for k in range(4):
                cp = copy(4 + j, k, (*chip, c), sibling, src=place(k, *chip, c))
                cp.start()
                passed.append(cp)
        for k in range(4):
            copy(0, k, (x, y, 1 - c), me).wait_recv()
        for j, chip in enumerate(chips):
            for k in range(4):
                copy(4 + j, k, (*chip, 1 - c), me).wait_recv()
        for cp in first + passed:
            cp.wait_send()
        for cp in local:
            cp.wait()

    vm = pl.BlockSpec(memory_space=pltpu.VMEM)
    hbm = pl.BlockSpec(memory_space=pl.ANY)
    return pl.pallas_call(
        body,
        out_shape=[jax.ShapeDtypeStruct((NDEV // 2, D, PAIR), BF16)] + [jax.ShapeDtypeStruct((NDEV, ROW_SHARD, D), BF16)] * 3,
        in_specs=[vm, vm, vm, vm],
        out_specs=[hbm, hbm, hbm, hbm],
        scratch_shapes=[pltpu.VMEM((half, IN_SHARD), BF16), pltpu.VMEM((half, IN_SHARD), BF16),
                        pltpu.VMEM((half, PAIR), BF16), pltpu.VMEM((3, ROW_SHARD, D), BF16),
                        pltpu.SemaphoreType.DMA((NDEV - 1, 4)), pltpu.SemaphoreType.DMA((NDEV - 1, 4)),
                        pltpu.SemaphoreType.DMA((4,)), pltpu.SemaphoreType.DMA((2,))],
        name="gather_weights", compiler_params=_cp(40),
    )(w_in_s, wbh_s, wba_s, wout_s)


def _exchange_pair(arrs):
    n = len(arrs)

    def launch(*refs):
        ins, got = refs[:n], refs[n:2 * n]
        send_sems, recv_sems = refs[2 * n:]
        (x, y, c), sibling, _ = _place()
        barrier = pltpu.get_barrier_semaphore()
        pl.semaphore_signal(barrier, inc=1, device_id=sibling, device_id_type=MESH)
        pl.semaphore_wait(barrier, 1)
        sends = [pltpu.make_async_remote_copy(
            src_ref=ins[k].at[_dev_index(q // 2, q % 2, 1 - c)], dst_ref=got[k].at[q], send_sem=send_sems.at[q, k],
            recv_sem=recv_sems.at[q, k], device_id=sibling, device_id_type=MESH) for q in range(4) for k in range(n)]
        for cp in sends:
            cp.start()
        for cp in sends:
            cp.wait_recv()
        for cp in sends:
            cp.wait_send()

    return pl.kernel(
        launch, out_type=[jax.ShapeDtypeStruct((4,) + a.shape[1:], a.dtype) for a in arrs],
        mesh=plsc.ScalarSubcoreMesh(axis_name="sequencer", num_cores=1), name="exchange_pair",
        scratch_types=(pltpu.SemaphoreType.DMA((4, n)), pltpu.SemaphoreType.DMA((4, n))),
        compiler_params=pltpu.CompilerParams(collective_id=0),
    )(*arrs)


def _pair_sum(full, got, core, rows, name):
    _, nr, nc = got.shape

    def body(core_ref, a_ref, b_ref, o_ref):
        o_ref[...] = (a_ref[...].astype(F32) + b_ref[...].astype(F32)).astype(BF16)

    blk = pl.BlockSpec((1, rows, nc), lambda q, i, core_ref: (q, i, 0))
    return pl.pallas_call(
        body,
        grid_spec=pltpu.PrefetchScalarGridSpec(
            num_scalar_prefetch=1, grid=(4, nr // rows),
            in_specs=[pl.BlockSpec((1, rows, nc), lambda q, i, core_ref: (2 * q + core_ref[0], i, 0)), blk],
            out_specs=blk),
        out_shape=jax.ShapeDtypeStruct(got.shape, BF16), name=name,
    )(core, full, got)


def _exchange_chips(sums):
    n = len(sums)

    def launch(*refs):
        ins, outs = refs[:n], refs[n:2 * n]
        send_sems, recv_sems = refs[2 * n:]
        (x, y, c), _, chips = _place()
        barrier = pltpu.get_barrier_semaphore()
        for px, py in chips:
            pl.semaphore_signal(barrier, inc=1, device_id=(px, py, c), device_id_type=MESH)
        pl.semaphore_wait(barrier, len(chips))
        copies = [pltpu.make_async_remote_copy(
            src_ref=ins[k].at[2 * px + py], dst_ref=outs[k].at[j], send_sem=send_sems.at[j, k],
            recv_sem=recv_sems.at[j, k], device_id=(px, py, c), device_id_type=MESH)
            for j, (px, py) in enumerate(chips) for k in range(n)]
        for cp in copies:
            cp.start()
        for cp in copies:
            cp.wait_recv()
        for cp in copies:
            cp.wait_send()

    return pl.kernel(
        launch, out_type=[jax.ShapeDtypeStruct((3,) + a.shape[1:], a.dtype) for a in sums],
        mesh=plsc.ScalarSubcoreMesh(axis_name="sequencer", num_cores=1), name="exchange_chips",
        scratch_types=(pltpu.SemaphoreType.DMA((3, n)), pltpu.SemaphoreType.DMA((3, n))),
        compiler_params=pltpu.CompilerParams(collective_id=1),
    )(*sums)


def _exchange_small(small):
    def body(sm_ref, out_ref, send_sems, recv_sems):
        (x, y, c), _, _ = _place()
        me = _dev_index(x, y, c)
        peers = [(1 - x if r & 4 else x, 1 - y if r & 2 else y, 1 - c if r & 1 else c) for r in range(1, NDEV)]
        out_ref[me] = sm_ref[...]
        copies = [pltpu.make_async_remote_copy(
            src_ref=sm_ref, dst_ref=out_ref.at[me], send_sem=send_sems.at[r], recv_sem=recv_sems.at[r],
            device_id=peer, device_id_type=MESH) for r, peer in enumerate(peers)]
        for cp in copies:
            cp.start()
        for r, peer in enumerate(peers):
            pltpu.make_async_remote_copy(
                src_ref=sm_ref, dst_ref=out_ref.at[_dev_index(*peer)], send_sem=send_sems.at[r], recv_sem=recv_sems.at[r],
                device_id=peer, device_id_type=MESH).wait_recv()
        for cp in copies:
            cp.wait_send()

    vm = pl.BlockSpec(memory_space=pltpu.VMEM)
    return pl.pallas_call(
        body, out_shape=jax.ShapeDtypeStruct((NDEV,) + small.shape, F32), in_specs=[vm], out_specs=vm,
        scratch_shapes=[pltpu.SemaphoreType.DMA((NDEV - 1,)), pltpu.SemaphoreType.DMA((NDEV - 1,))],
        name="exchange_small",
    )(small)


def _inproj_fwd(x2d, norm_w, w_pairs, tb):
    t = x2d.shape[0]
    nblk, _, nb = w_pairs.shape

    def body(x_ref, nw_ref, w_ref, proj_ref, xnt_ref, xn_s):
        @pl.when(pl.program_id(1) == 0)
        def _():
            xv = x_ref[...]
            r = lax.rsqrt(jnp.mean(xv * xv, axis=-1, keepdims=True) + EPS)
            xn = (xv * r) * nw_ref[...]
            xn_s[...] = xn.astype(BF16)
            xnt_ref[...] = xn.T.astype(BF16)

        proj_ref[...] = _mm(xn_s[...], w_ref[0])

    return pl.pallas_call(
        body, grid=(t // tb, nblk),
        in_specs=[pl.BlockSpec((tb, D), lambda i, j: (i, 0)), pl.BlockSpec((1, D), lambda i, j: (0, 0)),
                  pl.BlockSpec((1, D, nb), lambda i, j: (j, 0, 0))],
        out_specs=[pl.BlockSpec((tb, nb), lambda i, j: (i, j)), pl.BlockSpec((D, tb), lambda i, j: (0, i))],
        out_shape=[jax.ShapeDtypeStruct((t, DIN), F32), jax.ShapeDtypeStruct((D, t), BF16)],
        scratch_shapes=[pltpu.VMEM((tb, D), BF16)],
        name="inproj_fwd", compiler_params=_cp(56),
    )(x2d, norm_w, w_pairs)


def _lower_bound(lb_ref):
    a0 = lb_ref[0:1, :]
    a1 = lb_ref[1:2, :]
    mx = jnp.maximum(a0, a1)
    e0 = jnp.exp(a0 - mx)
    e1 = jnp.exp(a1 - mx)
    return e0 / (e0 + e1)


def _hgrn_chunk_fwd(hq, hf, lb, tril):
    sg = _sigmoid(hf)
    f = lb + (1.0 - lb) * sg
    g = jnp.log(f)
    k = 1.0 - f
    sq = _sigmoid(hq)
    q = hq * sq
    b = _mm_exact(tril, g)
    last_row = lax.broadcasted_iota(jnp.int32, b.shape, 0) == CH - 1
    b_last = jnp.sum(jnp.where(last_row, b, 0.0), axis=0, keepdims=True)
    c = 0.5 * b_last
    eb = jnp.exp(b)
    ea = jnp.exp(b - c)
    ek = jnp.exp(c - b)
    ed = jnp.exp(b_last - b)
    ebl = jnp.exp(b_last)
    return dict(sg=sg, f=f, k=k, sq=sq, q=q, eb=eb, ea=ea, ek=ek, ed=ed, ebl=ebl,
                qe=q * eb, qa=q * ea, ka=k * ek, kd=k * ed)


def _tri(lower):
    r = lax.broadcasted_iota(jnp.int32, (CH, CH), 0)
    c = lax.broadcasted_iota(jnp.int32, (CH, CH), 1)
    return (r >= c) if lower else (c >= r)


def _head_segment(p_ref, rows, j, hg):
    return p_ref[rows, j * HD * hg:(j + 1) * HD * hg]


def _head(a, k):
    return a[:, k * HD:(k + 1) * HD]


def _hgrn_fwd(proj, lbw, rb, hg):
    assert hg == HEADS
    t = proj.shape[0]
    ncb = rb // CH

    def body(p_ref, lb_ref, o_ref, st_ref, s_scr):
        @pl.when(pl.program_id(1) == 0)
        def _():
            s_scr[...] = jnp.zeros_like(s_scr)

        lb = _lower_bound(lb_ref)
        causal = _tri(True)
        tril = causal.astype(F32)
        heads = range(hg)

        def chunk(cc, carry):
            r0 = pl.multiple_of(cc * CH, CH)
            rows = pl.ds(r0, CH)
            e = _hgrn_chunk_fwd(_head_segment(p_ref, rows, 0, hg), _head_segment(p_ref, rows, 1, hg), lb, tril)
            v = _bf(_head_segment(p_ref, rows, 2, hg))
            sts = [s_scr[k] for k in heads]
            qa, ka, qe, kd = _bf(e["qa"]), _bf(e["ka"]), _bf(e["qe"]), _bf(e["kd"])
            a = [_bf(jnp.where(causal, _mm_nt(_head(qa, k), _head(ka, k)), 0.0)) for k in heads]
            o_inter = [_mm_nt(_head(qe, k), _bf(sts[k])) for k in heads]
            kv = [_mm_tn(_head(v, k), _head(kd, k)) for k in heads]
            o_intra = [_mm(a[k], _head(v, k)) for k in heads]
            for k in heads:
                st_ref[cc, k] = sts[k]
                o_ref[rows, k * HD:(k + 1) * HD] = o_inter[k] + o_intra[k]
                s_scr[k] = sts[k] * _head(e["ebl"], k) + kv[k]
            return carry

        lax.fori_loop(0, ncb, chunk, 0, unroll=2)

    return pl.pallas_call(
        body, grid=(HEADS // hg, t // rb),
        in_specs=[pl.BlockSpec((rb, 3 * HD * hg), lambda h, i: (i, h)), pl.BlockSpec((2, HD * hg), lambda h, i: (0, h))],
        out_specs=[pl.BlockSpec((rb, HD * hg), lambda h, i: (i, h)),
                   pl.BlockSpec((ncb, hg, HD, HD), lambda h, i: (i, h, 0, 0))],
        out_shape=[jax.ShapeDtypeStruct((t, D), F32), jax.ShapeDtypeStruct((t // CH, HEADS, HD, HD), F32)],
        scratch_shapes=[pltpu.VMEM((hg, HD, HD), F32)],
        name="hgrn_fwd", compiler_params=_cp(48),
    )(proj, lbw)


def _hgrn_bwd(proj, lbw, do_raw, states, rb, hg):
    assert hg == HEADS
    t = proj.shape[0]
    nblk = t // rb
    ncb = rb // CH
    wd = HD * hg

    def body(p_ref, lb_ref, do_ref, st_ref, dp_ref, dlb_ref, ds_scr):
        @pl.when(pl.program_id(1) == 0)
        def _():
            ds_scr[...] = jnp.zeros_like(ds_scr)
            dlb_ref[...] = jnp.zeros_like(dlb_ref)

        lb = _lower_bound(lb_ref)
        causal = _tri(True)
        tril = causal.astype(F32)
        triu = _tri(False).astype(F32)
        last_row = lax.broadcasted_iota(jnp.int32, (CH, HD * hg), 0) == CH - 1
        row0 = lax.broadcasted_iota(jnp.int32, (8, HD * hg), 0) == 0
        heads = range(hg)
        wide = lambda parts: jnp.concatenate(parts, axis=1)

        def chunk(it, carry):
            cc = ncb - 1 - it
            r0 = pl.multiple_of(cc * CH, CH)
            rows = pl.ds(r0, CH)
            hq = _head_segment(p_ref, rows, 0, hg)
            e = _hgrn_chunk_fwd(hq, _head_segment(p_ref, rows, 1, hg), lb, tril)
            v = _bf(_head_segment(p_ref, rows, 2, hg))
            do = _bf(do_ref[rows, :])
            sts = [st_ref[cc, k] for k in heads]
            dsts = [ds_scr[k] for k in heads]
            dlb_acc = dlb_ref[...]
            qa, ka, qe, kd = _bf(e["qa"]), _bf(e["ka"]), _bf(e["qe"]), _bf(e["kd"])
            a = [_bf(jnp.where(causal, _mm_nt(_head(qa, k), _head(ka, k)), 0.0)) for k in heads]
            da = [_bf(jnp.where(causal, _mm_nt(_head(do, k), _head(v, k)), 0.0)) for k in heads]
            dqe = wide([_mm(_head(do, k), _bf(sts[k])) for k in heads])
            dkd = wide([_mm(_head(v, k), _bf(dsts[k])) for k in heads])
            dv_state = [_mm_nt(_head(kd, k), _bf(dsts[k])) for k in heads]
            ds_new = [_mm_tn(_head(do, k), _head(qe, k)) for k in heads]
            dv_intra = [_mm_tn(a[k], _head(do, k)) for k in heads]
            dqa = wide([_mm(da[k], _head(ka, k)) for k in heads])
            dka = wide([_mm_tn(da[k], _head(qa, k)) for k in heads])
            dv = wide([dv_intra[k] + dv_state[k] for k in heads])
            dbl = e["ebl"] * wide([jnp.sum(sts[k] * dsts[k], axis=0, keepdims=True) for k in heads])
            dq = dqe * e["eb"] + dqa * e["ea"]
            dk = dka * e["ek"] + dkd * e["ed"]
            dkd_kd = dkd * kd.astype(F32)
            db = dqe * qe.astype(F32) + dqa * qa.astype(F32) - dka * ka.astype(F32) - dkd_kd
            db = db + jnp.where(last_row, dbl + jnp.sum(dkd_kd, axis=0, keepdims=True), 0.0)
            dg = _mm_exact(triu, db)
            df = dg / e["f"] - dk
            sg = e["sg"]
            sq = e["sq"]
            dhq = _bf(dq * (sq * (1.0 + hq * (1.0 - sq))))
            dhf = _bf(df * (1.0 - lb) * sg * (1.0 - sg))
            dhi = _bf(dv)
            dlb_new = dlb_acc + jnp.where(row0, jnp.sum(df * (1.0 - sg), axis=0, keepdims=True), 0.0)
            for k in heads:
                ds_scr[k] = ds_new[k] + dsts[k] * _head(e["ebl"], k)
            dp_ref[rows, 0:wd] = dhq
            dp_ref[rows, wd:2 * wd] = dhf
            dp_ref[rows, 2 * wd:3 * wd] = dhi
            dlb_ref[...] = dlb_new
            return carry

        lax.fori_loop(0, ncb, chunk, 0, unroll=2)

    rev = lambda h, i: (nblk - 1 - i, h)
    return pl.pallas_call(
        body, grid=(HEADS // hg, nblk),
        in_specs=[pl.BlockSpec((rb, 3 * HD * hg), rev), pl.BlockSpec((2, HD * hg), lambda h, i: (0, h)),
                  pl.BlockSpec((rb, HD * hg), rev), pl.BlockSpec((ncb, hg, HD, HD), lambda h, i: (nblk - 1 - i, h, 0, 0))],
        out_specs=[pl.BlockSpec((rb, 3 * HD * hg), rev), pl.BlockSpec((8, HD * hg), lambda h, i: (0, h))],
        out_shape=[jax.ShapeDtypeStruct((t, 3 * D), BF16), jax.ShapeDtypeStruct((8, D), F32)],
        scratch_shapes=[pltpu.VMEM((hg, HD, HD), F32)],
        name="hgrn_bwd", compiler_params=_cp(48),
    )(proj, lbw, do_raw, states)


def _kv_variants(tile, odd):
    low = lax.broadcasted_iota(jnp.int32, tile.shape, 1) < 64
    if odd:
        hi = jnp.where(low, 0.0, tile)
        lo = pltpu.roll(hi, 64, 1)
    else:
        lo = jnp.where(low, tile, 0.0)
        hi = pltpu.roll(lo, 64, 1)
    return _bf(lo), _bf(hi)


def _attn_masks(n):
    qi = lax.broadcasted_iota(jnp.int32, (AB, AB), 0)
    kj = lax.broadcasted_iota(jnp.int32, (AB, AB), 1)
    cur = kj <= qi
    return cur, cur | (n > 0), qi <= kj


def _kv_all(prev_ref, cur_ref):
    out = []
    for tl in range(2):
        cols = slice(tl * 128, (tl + 1) * 128)
        tile = jnp.concatenate([prev_ref[:, cols], cur_ref[:, cols]], axis=0)
        out.append(_kv_variants(tile, 0))
        out.append(_kv_variants(tile, 1))
    return out


def _window(a2, cur):
    return jnp.where(cur, a2[:, AB:], a2[:, :AB])


def _attn_softmax(scores, sinks, cur, ok):
    s = [jnp.where(ok, _window(s2, cur) * ATT_SCALE, NEG) for s2 in scores]
    m = [jnp.maximum(jnp.max(si, axis=-1, keepdims=True), sink) for si, sink in zip(s, sinks)]
    p = [jnp.exp(si - mi) for si, mi in zip(s, m)]
    es = [jnp.exp(sink - mi) for sink, mi in zip(sinks, m)]
    inv = [1.0 / (jnp.sum(pi, axis=-1, keepdims=True) + ei) for pi, ei in zip(p, es)]
    return [pi * ii for pi, ii in zip(p, inv)], [ei * ii for ei, ii in zip(es, inv)]


def _spread(pc, cur):
    return jnp.concatenate([jnp.where(cur, 0.0, pc), jnp.where(cur, pc, 0.0)], axis=1)


def _spread_t(pct, cur_t):
    return jnp.concatenate([jnp.where(cur_t, 0.0, pct), jnp.where(cur_t, pct, 0.0)], axis=0)


def _attn_fwd(proj, sinks):
    t = proj.shape[0]
    nb = t // AB

    def body(q_ref, kc_ref, kp_ref, vc_ref, vp_ref, sink_ref, o_ref):
        cur, ok, _ = _attn_masks(pl.program_id(0))
        kvars = _kv_all(kp_ref, kc_ref)
        vvars = _kv_all(vp_ref, vc_ref)
        qps = [_bf(q_ref[:, 128 * j:128 * (j + 1)]) for j in range(8)]
        heads = [(j, ab) for j in range(8) for ab in range(2)]
        scores = [_mm_nt(qps[j], kvars[j // 2][ab]) for j, ab in heads]
        pcs, _ = _attn_softmax(scores, [sink_ref[0, h] for h in range(QH)], cur, ok)
        parts = [_mm(_bf(_spread(pcs[h], cur)), vvars[j // 2][ab]) for h, (j, ab) in enumerate(heads)]
        for j in range(8):
            o_ref[:, 128 * j:128 * (j + 1)] = parts[2 * j] + parts[2 * j + 1]

    prev = lambda n: jnp.maximum(n - 1, 0)
    return pl.pallas_call(
        body, grid=(nb,),
        in_specs=[pl.BlockSpec((AB, D), lambda n: (n, C_AQ // D)),
                  pl.BlockSpec((AB, 256), lambda n: (n, C_AK // 256)),
                  pl.BlockSpec((AB, 256), lambda n: (prev(n), C_AK // 256)),
                  pl.BlockSpec((AB, 256), lambda n: (n, C_AV // 256)),
                  pl.BlockSpec((AB, 256), lambda n: (prev(n), C_AV // 256)),
                  pl.BlockSpec(memory_space=pltpu.SMEM)],
        out_specs=pl.BlockSpec((AB, D), lambda n: (n, 0)),
        out_shape=jax.ShapeDtypeStruct((t, D), F32),
        name="attn_fwd", compiler_params=_cp(32),
    )(proj, proj, proj, proj, proj, sinks)


def _attn_bwd(proj, sinks, do_a):
    t = proj.shape[0]
    nb = t // AB

    def body(q_ref, kc_ref, kp_ref, vc_ref, vp_ref, do_ref, sink_ref, dq_ref, dkv_ref, dsink_ref, carry):
        n = pl.program_id(0)

        @pl.when(n == 0)
        def _():
            dsink_ref[...] = jnp.zeros_like(dsink_ref)
            carry[...] = jnp.zeros_like(carry)

        @pl.when(n < nb)
        def _():
            cur, ok, cur_t = _attn_masks(n)
            low = lax.broadcasted_iota(jnp.int32, (2 * AB, 128), 1) < 64
            lane = lax.broadcasted_iota(jnp.int32, (8, 128), 1)
            row0 = lax.broadcasted_iota(jnp.int32, (8, 128), 0) == 0
            kvars = _kv_all(kp_ref, kc_ref)
            vvars = _kv_all(vp_ref, vc_ref)
            qps = [_bf(q_ref[:, 128 * j:128 * (j + 1)]) for j in range(8)]
            dops = [_bf(do_ref[:, 128 * j:128 * (j + 1)]) for j in range(8)]
            heads = [(j, ab) for j in range(8) for ab in range(2)]
            scores = [_mm_nt(qps[j], kvars[j // 2][ab]) for j, ab in heads]
            dps = [_mm_nt(dops[j], vvars[j // 2][ab]) for j, ab in heads]
            pcs, pss = _attn_softmax(scores, [sink_ref[0, h] for h in range(QH)], cur, ok)
            dpcs = [_window(dp2, cur) for dp2 in dps]
            rss = [jnp.sum(pc * dpc, axis=-1, keepdims=True) for pc, dpc in zip(pcs, dpcs)]
            dscs = [pc * (dpc - rs) for pc, dpc, rs in zip(pcs, dpcs, rss)]
            dsink = jnp.zeros((8, 128), F32)
            for h in range(QH):
                dsink = dsink + jnp.where(row0 & (lane == h), -jnp.sum(pss[h] * rss[h]), 0.0)
            dq_terms = [_mm(_bf(_spread(dscs[h], cur)), kvars[j // 2][ab]) for h, (j, ab) in enumerate(heads)]
            for j in range(8):
                dq_ref[:, 128 * j:128 * (j + 1)] = _bf((dq_terms[2 * j] + dq_terms[2 * j + 1]) * ATT_SCALE)
            dsc_t = [_bf(_spread_t(dsc.T, cur_t)) for dsc in dscs]
            pc_t = [_bf(_spread_t(pc.T, cur_t)) for pc in pcs]
            dk_terms = [_mm(dsc_t[h], qps[j]) for h, (j, ab) in enumerate(heads)]
            dv_terms = [_mm(pc_t[h], dops[j]) for h, (j, ab) in enumerate(heads)]
            dk_ab = [[dk_terms[4 * g + ab] + dk_terms[4 * g + 2 + ab] for ab in range(2)] for g in range(4)]
            dv_ab = [[dv_terms[4 * g + ab] + dv_terms[4 * g + 2 + ab] for ab in range(2)] for g in range(4)]
            for tl in range(2):
                ke, ko = dk_ab[2 * tl], dk_ab[2 * tl + 1]
                ve, vo = dv_ab[2 * tl], dv_ab[2 * tl + 1]
                dkt = (jnp.where(low, ke[0], 0.0) + pltpu.roll(jnp.where(low, 0.0, ke[1]), 64, 1)
                       + jnp.where(low, 0.0, ko[1]) + pltpu.roll(jnp.where(low, ko[0], 0.0), 64, 1)) * ATT_SCALE
                dvt = (jnp.where(low, ve[0], 0.0) + pltpu.roll(jnp.where(low, 0.0, ve[1]), 64, 1)
                       + jnp.where(low, 0.0, vo[1]) + pltpu.roll(jnp.where(low, vo[0], 0.0), 64, 1))
                kcols = slice(tl * 128, (tl + 1) * 128)
                vcols = slice(256 + tl * 128, 256 + (tl + 1) * 128)
                dkv_ref[:, kcols] = _bf(carry[:, kcols] + dkt[0:AB])
                dkv_ref[:, vcols] = _bf(carry[:, vcols] + dvt[0:AB])
                carry[:, kcols] = dkt[AB:2 * AB]
                carry[:, vcols] = dvt[AB:2 * AB]
            dsink_ref[...] += dsink

        @pl.when(n == nb)
        def _():
            dkv_ref[...] = _bf(carry[...])

    cur = lambda n: jnp.minimum(n, nb - 1)
    prev = lambda n: jnp.clip(n - 1, 0, nb - 1)
    return pl.pallas_call(
        body, grid=(nb + 1,),
        in_specs=[pl.BlockSpec((AB, D), lambda n: (cur(n), C_AQ // D)),
                  pl.BlockSpec((AB, 256), lambda n: (cur(n), C_AK // 256)),
                  pl.BlockSpec((AB, 256), lambda n: (prev(n), C_AK // 256)),
                  pl.BlockSpec((AB, 256), lambda n: (cur(n), C_AV // 256)),
                  pl.BlockSpec((AB, 256), lambda n: (prev(n), C_AV // 256)),
                  pl.BlockSpec((AB, D), lambda n: (cur(n), 0)),
                  pl.BlockSpec(memory_space=pltpu.SMEM)],
        out_specs=[pl.BlockSpec((AB, D), lambda n: (cur(n), 0)),
                   pl.BlockSpec((AB, 512), lambda n: (prev(n), 0)),
                   pl.BlockSpec((8, 128), lambda n: (0, 0))],
        out_shape=[jax.ShapeDtypeStruct((t, D), BF16), jax.ShapeDtypeStruct((t, 512), BF16),
                   jax.ShapeDtypeStruct((8, 128), F32)],
        scratch_shapes=[pltpu.VMEM((AB, 512), F32)],
        name="attn_bwd", compiler_params=_cp(32),
    )(proj, proj, proj, proj, proj, do_a, sinks)


def _silu_and_grad(v):
    s = _sigmoid(v)
    return v * s, s * (1.0 + v * (1.0 - s))


def _tail(o_raw, o_a, proj, x2d, tgt, wbh, wba, wout, hnw, fnw, tb):
    t = x2d.shape[0]

    def body(or_ref, oa_ref, hg_ref, ag0, ag1, mh0, mh1, ma0, ma1, x_ref, t_ref, wbh_ref, wba_ref, wout_ref, hnw_ref,
             fnw_ref, dx2_ref, dor_ref, doa_ref, dhg_ref, dagm_ref, gh_ref, ga_ref, mg_ref, dyh_ref, dya_ref, dx2b_ref,
             sums_ref):
        @pl.when(pl.program_id(0) == 0)
        def _():
            sums_ref[...] = jnp.zeros_like(sums_ref)

        halves = lambda a, b: jnp.concatenate([a[...], b[...]], axis=1)
        hnw_v = hnw_ref[...]
        fnw_v = fnw_ref[...]
        o = or_ref[...]
        rs, xhs = [], []
        for h in range(HEADS):
            oh = o[:, h * HD:(h + 1) * HD]
            r = lax.rsqrt(jnp.mean(oh * oh, axis=-1, keepdims=True) + EPS)
            rs.append(r)
            xhs.append(oh * r)
        xh = jnp.concatenate(xhs, axis=1)
        on = xh * hnw_v
        sil_hg, dsil_hg = _silu_and_grad(hg_ref[...])
        gh_b = _bf(on * sil_hg)
        y_h = _mm(gh_b, wbh_ref[...])
        oa = oa_ref[...]
        sil_ag, dsil_ag = _silu_and_grad(halves(ag0, ag1))
        ga_b = _bf(oa * sil_ag)
        y_a = _mm(ga_b, wba_ref[...])
        s_mh = _sigmoid(halves(mh0, mh1))
        s_ma = _sigmoid(halves(ma0, ma1))
        mg_b = _bf(s_mh * y_h + s_ma * y_a)
        x2 = x_ref[...] + _mm(mg_b, wout_ref[...])
        r2 = lax.rsqrt(jnp.mean(x2 * x2, axis=-1, keepdims=True) + EPS)
        xh2 = x2 * r2
        err = xh2 * fnw_v - t_ref[...]
        loss = 0.5 * jnp.sum(jnp.mean(err * err, axis=-1, keepdims=True))
        dy = err * (1.0 / D)
        dfnw = jnp.sum(dy * xh2, axis=0, keepdims=True)
        dxh2 = dy * fnw_v
        dx2 = r2 * (dxh2 - xh2 * jnp.mean(dxh2 * xh2, axis=-1, keepdims=True))
        dx2_ref[...] = dx2
        dx2_b = _bf(dx2)
        dmg = _mm_nt(dx2_b, wout_ref[...])
        dmg_h = dmg * s_mh
        dmg_a = dmg * s_ma
        dyh_b = _bf(dmg_h)
        dya_b = _bf(dmg_a)
        dagm_ref[:, D:2 * D] = _bf(dmg_h * y_h * (1.0 - s_mh))
        dagm_ref[:, 2 * D:3 * D] = _bf(dmg_a * y_a * (1.0 - s_ma))
        dgh = _mm_nt(dyh_b, wbh_ref[...])
        dga = _mm_nt(dya_b, wba_ref[...])
        doa_ref[...] = dga * sil_ag
        dagm_ref[:, 0:D] = _bf(dga * oa * dsil_ag)
        dhg_ref[...] = _bf(dgh * on * dsil_hg)
        don = dgh * sil_hg
        dhnw = jnp.sum(don * xh, axis=0, keepdims=True)
        dxh = don * hnw_v
        dos = []
        for h in range(HEADS):
            sl = slice(h * HD, (h + 1) * HD)
            dos.append(rs[h] * (dxh[:, sl] - xhs[h] * jnp.mean(dxh[:, sl] * xhs[h], axis=-1, keepdims=True)))
        dor_ref[...] = jnp.concatenate(dos, axis=1)
        gh_ref[...] = gh_b
        ga_ref[...] = ga_b
        mg_ref[...] = mg_b
        dyh_ref[...] = dyh_b
        dya_ref[...] = dya_b
        dx2b_ref[...] = dx2_b
        row = lax.broadcasted_iota(jnp.int32, (8, D), 0)
        sums_ref[...] += jnp.where(row == 0, dfnw, 0.0) + jnp.where(row == 1, dhnw, 0.0) + jnp.where(row == 2, loss, 0.0)

    rowblk = lambda c: pl.BlockSpec((tb, D), lambda i: (i, c))
    half = lambda c: pl.BlockSpec((tb, 512), lambda i: (i, c))
    full = lambda shape: pl.BlockSpec(shape, lambda i: (0, 0))
    return pl.pallas_call(
        body, grid=(t // tb,),
        in_specs=[rowblk(0), rowblk(0), rowblk(C_HG // D), half(C_AG // 512), half(C_AG // 512 + 1), half(C_MH // 512),
                  half(C_MH // 512 + 1), half(C_MA // 512), half(C_MA // 512 + 1), rowblk(0), rowblk(0),
                  full((D, D)), full((D, D)), full((D, D)), full((1, D)), full((1, D))],
        out_specs=[rowblk(0), rowblk(0), rowblk(0), rowblk(0), pl.BlockSpec((tb, 3 * D), lambda i: (i, 0))]
        + [rowblk(0)] * 6 + [full((8, D))],
        out_shape=[jax.ShapeDtypeStruct((t, D), F32)] * 3
        + [jax.ShapeDtypeStruct((t, D), BF16), jax.ShapeDtypeStruct((t, 3 * D), BF16)]
        + [jax.ShapeDtypeStruct((t, D), BF16)] * 6 + [jax.ShapeDtypeStruct((8, D), F32)],
        name="tail", compiler_params=_cp(56),
    )(o_raw, o_a, proj, proj, proj, proj, proj, proj, proj, x2d, tgt, wbh, wba, wout, hnw, fnw)


def _wgrad3(gh, dyh, ga, dya, mg, dx2b, tk):
    t = dyh.shape[0]

    def body(a0, b0, a1, b1, a2, b2, o0, o1, o2):
        @pl.when(pl.program_id(0) == 0)
        def _():
            o0[...] = jnp.zeros_like(o0)
            o1[...] = jnp.zeros_like(o1)
            o2[...] = jnp.zeros_like(o2)

        o0[...] += _mm_tn(a0[...], b0[...])
        o1[...] += _mm_tn(a1[...], b1[...])
        o2[...] += _mm_tn(a2[...], b2[...])

    blk = pl.BlockSpec((tk, D), lambda k: (k, 0))
    out = pl.BlockSpec((D, D), lambda k: (0, 0))
    return pl.pallas_call(
        body, grid=(t // tk,), in_specs=[blk] * 6, out_specs=[out] * 3,
        out_shape=[jax.ShapeDtypeStruct((D, D), F32)] * 3,
        name="wgrad3", compiler_params=_cp(48),
    )(gh, dyh, ga, dya, mg, dx2b)


def _inproj_wgrad_piece(xnt, piece, nb, name):
    t = xnt.shape[1]
    width = piece.shape[1]

    def body(xnt_ref, p_ref, o_ref):
        o_ref[...] = _mm(xnt_ref[...], p_ref[...])

    return pl.pallas_call(
        body, grid=(width // nb,),
        in_specs=[pl.BlockSpec((D, t), lambda j: (0, 0), pipeline_mode=pl.Buffered(1)),
                  pl.BlockSpec((t, nb), lambda j: (0, j))],
        out_specs=pl.BlockSpec((D, nb), lambda j: (0, j)),
        out_shape=jax.ShapeDtypeStruct((D, width), F32),
        name=name, compiler_params=_cp(56),
    )(xnt, piece)


def _inproj_dgrad(pieces, w_p, x2d, dx2, norm_w, tb, after):
    t = x2d.shape[0]

    def body(*refs):
        piece_refs = refs[:len(pieces)]
        w_ref, x_ref, dx2_ref, nw_ref, _, gx_ref, dnw_ref = refs[len(pieces):]

        @pl.when(pl.program_id(0) == 0)
        def _():
            dnw_ref[...] = jnp.zeros_like(dnw_ref)

        dxn = None
        off = 0
        for p in piece_refs:
            width = p.shape[1]
            for q in range(w_ref.shape[0]):
                lo, hi = max(off, q * PAIR), min(off + width, (q + 1) * PAIR)
                if lo < hi:
                    term = _mm_nt(p[:, lo - off:hi - off], w_ref[q, :, lo - q * PAIR:hi - q * PAIR])
                    dxn = term if dxn is None else dxn + term
            off += width
        xv = x_ref[...]
        r = lax.rsqrt(jnp.mean(xv * xv, axis=-1, keepdims=True) + EPS)
        xh = xv * r
        dxh = dxn * nw_ref[...]
        gx_ref[...] = dx2_ref[...] + r * (dxh - xh * jnp.mean(dxh * xh, axis=-1, keepdims=True))
        row0 = lax.broadcasted_iota(jnp.int32, (8, D), 0) == 0
        dnw_ref[...] += jnp.where(row0, jnp.sum(dxn * xh, axis=0, keepdims=True), 0.0)

    rowblk = pl.BlockSpec((tb, D), lambda i: (i, 0))
    return pl.pallas_call(
        body, grid=(t // tb,),
        in_specs=[pl.BlockSpec((tb, p.shape[1]), lambda i: (i, 0)) for p in pieces]
        + [pl.BlockSpec(w_p.shape, lambda i: (0, 0, 0), pipeline_mode=pl.Buffered(1)), rowblk, rowblk,
           pl.BlockSpec((1, D), lambda i: (0, 0)), pl.BlockSpec(memory_space=pl.ANY)],
        out_specs=[rowblk, pl.BlockSpec((8, D), lambda i: (0, 0))],
        out_shape=[jax.ShapeDtypeStruct((t, D), F32), jax.ShapeDtypeStruct((8, D), F32)],
        name="inproj_dgrad", compiler_params=_cp(60),
    )(*pieces, w_p, x2d, dx2, norm_w, after)


def _adamw_math(w, g, m, v):
    m = B1 * m + (1.0 - B1) * g
    v = B2 * v + (1.0 - B2) * (g * g)
    m_hat = m / (1.0 - B1 ** STEP)
    v_hat = v / (1.0 - B2 ** STEP)
    delta = -LR * (m_hat / (jnp.sqrt(v_hat) + ADAM_EPS) + WD * w)
    return delta, m, v


def _adamw_shard(recv, sums, chip, w, m, v, rows, name):
    nparts, nr, nc = recv.shape

    def body(chip_ref, own_ref, p_ref, w_ref, m_ref, v_ref, g_ref, d_ref, nm_ref, nv_ref):
        g = own_ref[0].astype(F32)
        for s in range(nparts):
            g = g + p_ref[s].astype(F32)
        d, nm, nv = _adamw_math(w_ref[...], g, m_ref[...], v_ref[...])
        g_ref[...] = g
        d_ref[...] = d
        nm_ref[...] = nm
        nv_ref[...] = nv

    blk = pl.BlockSpec((rows, nc), lambda i, chip_ref: (i, 0))
    return pl.pallas_call(
        body,
        grid_spec=pltpu.PrefetchScalarGridSpec(
            num_scalar_prefetch=1, grid=(nr // rows,),
            in_specs=[pl.BlockSpec((1, rows, nc), lambda i, chip_ref: (chip_ref[0], i, 0)),
                      pl.BlockSpec((nparts, rows, nc), lambda i, chip_ref: (0, i, 0)), blk, blk, blk],
            out_specs=[blk] * 4),
        out_shape=[jax.ShapeDtypeStruct((nr, nc), F32)] * 4,
        name=name, compiler_params=_cp(48),
    )(chip, sums, recv, w, m, v)


SMALL_ROWS = dict(norm_w=0, lower_bound=1, hgrn_norm_w=3, final_norm_w=4, sinks=5)


def _pack_small_grads(dnw, dlb, sums, dsink):
    def body(dnw_ref, dlb_ref, sums_ref, dsink_ref, o_ref):
        o_ref[...] = jnp.zeros_like(o_ref)
        o_ref[0:1, :] = dnw_ref[0:1, :]
        o_ref[1:2, :] = dlb_ref[0:1, :]
        o_ref[3:4, :] = sums_ref[1:2, :]
        o_ref[4:5, :] = sums_ref[0:1, :]
        o_ref[5:6, 0:128] = dsink_ref[0:1, :]

    return pl.pallas_call(body, out_shape=jax.ShapeDtypeStruct((8, D), F32), name="pack_small_grads")(dnw, dlb, sums, dsink)


def _adamw_small(parts, ws, ms, vs):
    shapes = [a.shape for a in ws]

    def body(p_ref, *refs):
        w, m, v = refs[0:5], refs[5:10], refs[10:15]
        outs = [refs[15 + 5 * i:20 + 5 * i] for i in range(4)]

        def total(row, width):
            g = p_ref[0, row:row + 1, 0:width]
            for s in range(1, NDEV):
                g = g + p_ref[s, row:row + 1, 0:width]
            return g

        lb = _lower_bound(w[1])
        ga0 = total(1, D) * lb * (1.0 - lb)
        grads = [total(0, D), None, total(3, D), total(4, D), total(5, QH)]
        for i in (0, 2, 3, 4):
            res = (grads[i],) + _adamw_math(w[i][...], grads[i], m[i][...], v[i][...])
            for o, val in zip(outs, res):
                o[i][...] = val
        for r, g in ((0, ga0), (1, -ga0)):
            res = (g,) + _adamw_math(w[1][r:r + 1, :], g, m[1][r:r + 1, :], v[1][r:r + 1, :])
            for o, val in zip(outs, res):
                o[1][r:r + 1, :] = val

    res = pl.pallas_call(
        body, out_shape=[jax.ShapeDtypeStruct(s, F32) for s in shapes] * 4, name="adamw_small",
    )(parts, *ws, *ms, *vs)
    return [res[5 * i:5 * i + 5] for i in range(4)]


def kernel(x, norm_w, w_in, hgrn_lower_bound, hgrn_norm_w, w_branch_hgrn, attn_sinks, w_branch_attn, w_out, final_norm_w, loss_target, m_norm_w, m_w_in, m_hgrn_lower_bound, m_hgrn_norm_w, m_w_branch_hgrn, m_attn_sinks, m_w_branch_attn, m_w_out, m_final_norm_w, v_norm_w, v_w_in, v_hgrn_lower_bound, v_hgrn_norm_w, v_w_branch_hgrn, v_attn_sinks, v_w_branch_attn, v_w_out, v_final_norm_w):
    t = x.shape[1]
    x2d = x.reshape(t, D)
    tgt = loss_target.reshape(t, D)
    fnw = final_norm_w.reshape(1, D)
    row_blk = min(256, t)
    big_blk = min(512, t)

    w_p, wbh_g, wba_g, wout_g = _gather_weights(w_in[0], w_branch_hgrn[0], w_branch_attn[0], w_out[0])
    wbh, wba, wout = wbh_g.reshape(D, D), wba_g.reshape(D, D), wout_g.reshape(D, D)

    proj, xnt = _inproj_fwd(x2d, norm_w, w_p, min(1024, t))
    o_raw, states = _hgrn_fwd(proj, hgrn_lower_bound, big_blk, HGRN_GROUP)
    o_a = _attn_fwd(proj, attn_sinks)
    (dx2, do_raw, do_a, d_hg, d_agm, gh, ga, mg, dyh, dya, dx2b, sums) = _tail(
        o_raw, o_a, proj, x2d, tgt, wbh, wba, wout, hgrn_norm_w, fnw, row_blk)
    dwbh, dwba, dwout = _wgrad3(gh, dyh, ga, dya, mg, dx2b, big_blk)
    d_aq, d_kv, dsink = _attn_bwd(proj, attn_sinks, do_a)
    d_hgrn, dlb = _hgrn_bwd(proj, hgrn_lower_bound, do_raw, states, big_blk, HGRN_GROUP)
    pieces = (d_hgrn, d_hg, d_aq, d_kv, d_agm)
    dw_pieces = [_inproj_wgrad_piece(xnt, p, CB, "inproj_wgrad_" + n)
                 for p, n in zip(pieces, ("hgrn", "hgate", "aq", "kv", "gates"))]

    dwin_r = jnp.concatenate(dw_pieces, axis=1).reshape(D, NDEV, IN_SHARD).transpose(1, 0, 2).astype(BF16)
    slots = lambda a: a.reshape(NDEV, ROW_SHARD, D).astype(BF16)
    partials = [dwin_r, slots(dwbh), slots(dwba), slots(dwout)]
    got = _exchange_pair(partials)
    core = lax.axis_index("c").astype(jnp.int32).reshape(1)
    pair_sums = [_pair_sum(a, g, core, ROW_SHARD, "pair_sum_" + n)
                 for a, g, n in zip(partials, got, ("w_in", "w_bh", "w_ba", "w_out"))]
    rin, rbh, rba, rout = _exchange_chips(pair_sums)
    grad_x, dnw = _inproj_dgrad(pieces, w_p, x2d, dx2, norm_w, big_blk, after=pair_sums[0])
    rsm = _exchange_small(_pack_small_grads(dnw, dlb, sums, dsink))
    chip = (2 * lax.axis_index("x") + lax.axis_index("y")).astype(jnp.int32).reshape(1)
    s_in, s_bh, s_ba, s_out = pair_sums
    g_in, d_in, nm_in, nv_in = _adamw_shard(rin, s_in, chip, w_in[0], m_w_in[0], v_w_in[0], 128, "adamw_w_in")
    g_bh, d_bh, nm_bh, nv_bh = _adamw_shard(
        rbh, s_bh, chip, w_branch_hgrn[0], m_w_branch_hgrn[0], v_w_branch_hgrn[0], 128, "adamw_w_bh")
    g_ba, d_ba, nm_ba, nv_ba = _adamw_shard(
        rba, s_ba, chip, w_branch_attn[0], m_w_branch_attn[0], v_w_branch_attn[0], 128, "adamw_w_ba")
    g_out, d_out, nm_out, nv_out = _adamw_shard(rout, s_out, chip, w_out[0], m_w_out[0], v_w_out[0], 128, "adamw_w_out")
    sg, sd, sm, sv = _adamw_small(
        rsm,
        (norm_w, hgrn_lower_bound, hgrn_norm_w, fnw, attn_sinks),
        (m_norm_w, m_hgrn_lower_bound, m_hgrn_norm_w, m_final_norm_w.reshape(1, D), m_attn_sinks),
        (v_norm_w, v_hgrn_lower_bound, v_hgrn_norm_w, v_final_norm_w.reshape(1, D), v_attn_sinks))

    loss = lax.psum(sums[2, 0], ("x", "y", "c"))

    def group(s, w_in_v, bh, ba, out):
        nw, lb, hnw, fn, sinks = s
        return (nw, w_in_v[None], lb, hnw, bh[None], sinks, ba[None], out[None], fn.reshape(D))

    return (loss, grad_x.reshape(1, t, D),
            *group(sg, g_in, g_bh, g_ba, g_out), *group(sd, d_in, d_bh, d_ba, d_out),
            *group(sm, nm_in, nm_bh, nm_ba, nm_out), *group(sv, nv_in, nv_bh, nv_ba, nv_out))
```

```python
import functools

import jax
import jax.numpy as jnp
from jax import lax
from jax.experimental import pallas as pl
from jax.experimental.pallas import tpu as pltpu
from jax.experimental.pallas import tpu_sc as plsc

F32 = jnp.float32
BF16 = jnp.bfloat16

D = 1024
DIN = 8704
NDEV = 8
IN_SHARD = DIN // NDEV
PAIR = 2 * IN_SHARD
ROW_SHARD = D // NDEV
HEADS = 8
HD = 128
CH = 64
HGRN_GROUP = 8
QH = 16
AB = 128
EPS = 1e-6
NEG = -1e30
ATT_SCALE = 0.125

C_HGRN = 0
C_HG = 3072
C_AQ = 4096
C_AK = 5120
C_AV = 5376
C_AG = 5632
C_MH = 6656
C_MA = 7680
CB = 512

LR = 0.001
B1 = 0.9
B2 = 0.999
ADAM_EPS = 1e-08
WD = 0.01
STEP = 10

V7X_VMEM_BYTES = 64 * 1024 * 1024
MESH = pl.DeviceIdType.MESH


def _cp(vmem_mb):
    return pltpu.CompilerParams(vmem_limit_bytes=vmem_mb * 1024 * 1024)


def _mm(a, b):
    return jnp.dot(a, b, preferred_element_type=F32)


def _mm_nt(a, b):
    return lax.dot_general(a, b, (((1,), (1,)), ((), ())), preferred_element_type=F32)


def _mm_tn(a, b):
    return lax.dot_general(a, b, (((0,), (0,)), ((), ())), preferred_element_type=F32)


def _tri3(lower):
    r = lax.broadcasted_iota(jnp.int32, (CH, 3 * CH), 0)
    c = lax.broadcasted_iota(jnp.int32, (CH, 3 * CH), 1)
    c = jnp.where(c >= 2 * CH, c - 2 * CH, jnp.where(c >= CH, c - CH, c))
    return ((r >= c) if lower else (c >= r)).astype(BF16)


def _mm_tri_exact(tri3, g):
    g1 = g.astype(BF16)
    r1 = g - g1.astype(F32)
    g2 = r1.astype(BF16)
    g3 = (r1 - g2.astype(F32)).astype(BF16)
    return _mm(tri3, jnp.concatenate([g1, g2, g3], axis=0))


def _sigmoid(v):
    return 0.5 * jnp.tanh(0.5 * v) + 0.5


def _bf(v):
    return v.astype(BF16)


def _place():
    x, y, c = lax.axis_index("x"), lax.axis_index("y"), lax.axis_index("c")
    return (x, y, c), (x, y, 1 - c), [(1 - x, y), (x, 1 - y), (1 - x, 1 - y)]


def _dev_index(px, py, pc):
    return 4 * px + 2 * py + pc


def _gather_in_projection(w_in_s):
    half = D // 2

    def body(win_ref, wp_g, give, take, mine, send_sems, recv_sems, loc_sem, swap_sems):
        (x, y, c), sibling, chips = _place()
        give[...] = win_ref[pl.ds(pl.multiple_of(half * (1 - c), half), half), :].astype(BF16)
        swap = pltpu.make_async_remote_copy(src_ref=give, dst_ref=take, send_sem=swap_sems.at[0], recv_sem=swap_sems.at[1],
                                            device_id=sibling, device_id_type=MESH)
        swap.start()
        swap.wait()
        own = win_ref[pl.ds(pl.multiple_of(half * c, half), half), :]
        other = take[...].astype(F32)
        mine[...] = jnp.where(c == 0, jnp.concatenate([own, other], axis=1),
                              jnp.concatenate([other, own], axis=1)).astype(BF16)

        def place(px, py, pc):
            return wp_g.at[2 * px + py, pl.ds(pl.multiple_of(half * pc, half), half), :]

        def copy(kind, origin, to, src=mine):
            return pltpu.make_async_remote_copy(
                src_ref=src, dst_ref=place(*origin), send_sem=send_sems.at[kind], recv_sem=recv_sems.at[kind],
                device_id=to, device_id_type=MESH)

        me = (x, y, c)
        local = pltpu.make_async_copy(mine, place(*me), loc_sem)
        local.start()
        first = [copy(0, me, sibling)] + [copy(1 + j, me, (*chip, c)) for j, chip in enumerate(chips)]
        for cp in first:
            cp.start()
        passed = []
        for j, chip in enumerate(chips):
            copy(1 + j, (*chip, c), me).wait_recv()
            cp = copy(4 + j, (*chip, c), sibling, src=place(*chip, c))
            cp.start()
            passed.append(cp)
        copy(0, (x, y, 1 - c), me).wait_recv()
        for j, chip in enumerate(chips):
            copy(4 + j, (*chip, 1 - c), me).wait_recv()
        for cp in first + passed:
            cp.wait_send()
        local.wait()

    return pl.pallas_call(
        body,
        out_shape=jax.ShapeDtypeStruct((NDEV // 2, D, PAIR), BF16),
        in_specs=[pl.BlockSpec(memory_space=pltpu.VMEM)],
        out_specs=pl.BlockSpec(memory_space=pl.ANY),
        scratch_shapes=[pltpu.VMEM((half, IN_SHARD), BF16), pltpu.VMEM((half, IN_SHARD), BF16),
                        pltpu.VMEM((half, PAIR), BF16),
                        pltpu.SemaphoreType.DMA((NDEV - 1,)), pltpu.SemaphoreType.DMA((NDEV - 1,)),
                        pltpu.SemaphoreType.DMA, pltpu.SemaphoreType.DMA((2,))],
        name="gather_in_projection", compiler_params=_cp(40),
    )(w_in_s)


def _gather_square(shards, after):
    n = len(shards)

    def launch(*refs):
        ins, outs = refs[:n], refs[n + 1:2 * n + 1]
        send_sems, recv_sems, loc_sems = refs[2 * n + 1:]
        (x, y, c), _, _ = _place()
        me = _dev_index(x, y, c)
        peers = [(1 - x if r & 4 else x, 1 - y if r & 2 else y, 1 - c if r & 1 else c) for r in range(1, NDEV)]
        barrier = pltpu.get_barrier_semaphore()
        for peer in peers:
            pl.semaphore_signal(barrier, inc=1, device_id=peer, device_id_type=MESH)
        pl.semaphore_wait(barrier, NDEV - 1)
        local = [pltpu.make_async_copy(ins[k], outs[k].at[me], loc_sems.at[k]) for k in range(n)]
        copies = [pltpu.make_async_remote_copy(
            src_ref=ins[k], dst_ref=outs[k].at[me], send_sem=send_sems.at[r, k], recv_sem=recv_sems.at[r, k],
            device_id=peer, device_id_type=MESH) for r, peer in enumerate(peers) for k in range(n)]
        for cp in local + copies:
            cp.start()
        for r, peer in enumerate(peers):
            for k in range(n):
                pltpu.make_async_remote_copy(
                    src_ref=ins[k], dst_ref=outs[k].at[_dev_index(*peer)], send_sem=send_sems.at[r, k],
                    recv_sem=recv_sems.at[r, k], device_id=peer, device_id_type=MESH).wait_recv()
        for cp in copies:
            cp.wait_send()
        for cp in local:
            cp.wait()

    return pl.kernel(
        launch, out_type=[jax.ShapeDtypeStruct((NDEV,) + a.shape, a.dtype) for a in shards],
        mesh=plsc.ScalarSubcoreMesh(axis_name="sequencer", num_cores=1), name="gather_square",
        scratch_types=(pltpu.SemaphoreType.DMA((NDEV - 1, n)), pltpu.SemaphoreType.DMA((NDEV - 1, n)),
                       pltpu.SemaphoreType.DMA((n,))),
        compiler_params=pltpu.CompilerParams(collective_id=2),
    )(*shards, after)


def _exchange_pair(arrs):
    n = len(arrs)

    def launch(*refs):
        ins, got = refs[:n], refs[n:2 * n]
        send_sems, recv_sems = refs[2 * n:]
        (x, y, c), sibling, _ = _place()
        barrier = pltpu.get_barrier_semaphore()
        pl.semaphore_signal(barrier, inc=1, device_id=sibling, device_id_type=MESH)
        pl.semaphore_wait(barrier, 1)
        sends = [pltpu.make_async_remote_copy(
            src_ref=ins[k].at[_dev_index(q // 2, q % 2, 1 - c)], dst_ref=got[k].at[q], send_sem=send_sems.at[q, k],
            recv_sem=recv_sems.at[q, k], device_id=sibling, device_id_type=MESH) for q in range(4) for k in range(n)]
        for cp in sends:
            cp.start()
        for cp in sends:
            cp.wait_recv()
        for cp in sends:
            cp.wait_send()

    return pl.kernel(
        launch, out_type=[jax.ShapeDtypeStruct((4,) + a.shape[1:], a.dtype) for a in arrs],
        mesh=plsc.ScalarSubcoreMesh(axis_name="sequencer", num_cores=1), name="exchange_pair",
        scratch_types=(pltpu.SemaphoreType.DMA((4, n)), pltpu.SemaphoreType.DMA((4, n))),
        compiler_params=pltpu.CompilerParams(collective_id=0),
    )(*arrs)


def _pair_sum(full, got, core, rows, name):
    _, nr, nc = got.shape

    def body(core_ref, a_ref, b_ref, o_ref):
        o_ref[...] = (a_ref[...].astype(F32) + b_ref[...].astype(F32)).astype(BF16)

    blk = pl.BlockSpec((1, rows, nc), lambda q, i, core_ref: (q, i, 0))
    return pl.pallas_call(
        body,
        grid_spec=pltpu.PrefetchScalarGridSpec(
            num_scalar_prefetch=1, grid=(4, nr // rows),
            in_specs=[pl.BlockSpec((1, rows, nc), lambda q, i, core_ref: (2 * q + core_ref[0], i, 0)), blk],
            out_specs=blk),
        out_shape=jax.ShapeDtypeStruct(got.shape, BF16), name=name,
    )(core, full, got)


def _exchange_chips(sums):
    n = len(sums)

    def launch(*refs):
        ins, outs = refs[:n], refs[n:2 * n]
        send_sems, recv_sems = refs[2 * n:]
        (x, y, c), _, chips = _place()
        barrier = pltpu.get_barrier_semaphore()
        for px, py in chips:
            pl.semaphore_signal(barrier, inc=1, device_id=(px, py, c), device_id_type=MESH)
        pl.semaphore_wait(barrier, len(chips))
        copies = [pltpu.make_async_remote_copy(
            src_ref=ins[k].at[2 * px + py], dst_ref=outs[k].at[j], send_sem=send_sems.at[j, k],
            recv_sem=recv_sems.at[j, k], device_id=(px, py, c), device_id_type=MESH)
            for j, (px, py) in enumerate(chips) for k in range(n)]
        for cp in copies:
            cp.start()
        for cp in copies:
            cp.wait_recv()
        for cp in copies:
            cp.wait_send()

    return pl.kernel(
        launch, out_type=[jax.ShapeDtypeStruct((3,) + a.shape[1:], a.dtype) for a in sums],
        mesh=plsc.ScalarSubcoreMesh(axis_name="sequencer", num_cores=1), name="exchange_chips",
        scratch_types=(pltpu.SemaphoreType.DMA((3, n)), pltpu.SemaphoreType.DMA((3, n))),
        compiler_params=pltpu.CompilerParams(collective_id=1),
    )(*sums)


def _exchange_small(small):
    def body(sm_ref, out_ref, send_sems, recv_sems):
        (x, y, c), _, _ = _place()
        me = _dev_index(x, y, c)
        peers = [(1 - x if r & 4 else x, 1 - y if r & 2 else y, 1 - c if r & 1 else c) for r in range(1, NDEV)]
        out_ref[me] = sm_ref[...]
        copies = [pltpu.make_async_remote_copy(
            src_ref=sm_ref, dst_ref=out_ref.at[me], send_sem=send_sems.at[r], recv_sem=recv_sems.at[r],
            device_id=peer, device_id_type=MESH) for r, peer in enumerate(peers)]
        for cp in copies:
            cp.start()
        for r, peer in enumerate(peers):
            pltpu.make_async_remote_copy(
                src_ref=sm_ref, dst_ref=out_ref.at[_dev_index(*peer)], send_sem=send_sems.at[r], recv_sem=recv_sems.at[r],
                device_id=peer, device_id_type=MESH).wait_recv()
        for cp in copies:
            cp.wait_send()

    vm = pl.BlockSpec(memory_space=pltpu.VMEM)
    return pl.pallas_call(
        body, out_shape=jax.ShapeDtypeStruct((NDEV,) + small.shape, F32), in_specs=[vm], out_specs=vm,
        scratch_shapes=[pltpu.SemaphoreType.DMA((NDEV - 1,)), pltpu.SemaphoreType.DMA((NDEV - 1,))],
        name="exchange_small",
    )(small)


def _inproj_fwd(x2d, norm_w, w_pairs, tb):
    t = x2d.shape[0]
    nblk, _, nb = w_pairs.shape

    def body(x_ref, nw_ref, w_ref, proj_ref, xnt_ref, xn_s):
        @pl.when(pl.program_id(1) == 0)
        def _():
            xv = x_ref[...]
            r = lax.rsqrt(jnp.mean(xv * xv, axis=-1, keepdims=True) + EPS)
            xn = (xv * r) * nw_ref[...]
            xn_s[...] = xn.astype(BF16)
            xnt_ref[...] = xn.T.astype(BF16)

        proj_ref[...] = _mm(xn_s[...], w_ref[0])

    return pl.pallas_call(
        body, grid=(t // tb, nblk),
        in_specs=[pl.BlockSpec((tb, D), lambda i, j: (i, 0)), pl.BlockSpec((1, D), lambda i, j: (0, 0)),
                  pl.BlockSpec((1, D, nb), lambda i, j: (j, 0, 0))],
        out_specs=[pl.BlockSpec((tb, nb), lambda i, j: (i, j)), pl.BlockSpec((D, tb), lambda i, j: (0, i))],
        out_shape=[jax.ShapeDtypeStruct((t, DIN), F32), jax.ShapeDtypeStruct((D, t), BF16)],
        scratch_shapes=[pltpu.VMEM((tb, D), BF16)],
        name="inproj_fwd", compiler_params=_cp(56),
    )(x2d, norm_w, w_pairs)


def _lower_bound(lb_ref):
    a0 = lb_ref[0:1, :]
    a1 = lb_ref[1:2, :]
    mx = jnp.maximum(a0, a1)
    e0 = jnp.exp(a0 - mx)
    e1 = jnp.exp(a1 - mx)
    return e0 / (e0 + e1)


def _hgrn_chunk_fwd(hq, hf, lb, tril):
    sg = _sigmoid(hf)
    f = lb + (1.0 - lb) * sg
    g = jnp.log(f)
    k = 1.0 - f
    sq = _sigmoid(hq)
    q = hq * sq
    b = _mm_tri_exact(tril, g)
    last_row = lax.broadcasted_iota(jnp.int32, b.shape, 0) == CH - 1
    b_last = jnp.sum(jnp.where(last_row, b, 0.0), axis=0, keepdims=True)
    c = 0.5 * b_last
    eb = jnp.exp(b)
    ea = jnp.exp(b - c)
    ek = jnp.exp(c - b)
    ed = jnp.exp(b_last - b)
    ebl = jnp.exp(b_last)
    return dict(sg=sg, f=f, k=k, sq=sq, q=q, eb=eb, ea=ea, ek=ek, ed=ed, ebl=ebl,
                qe=q * eb, qa=q * ea, ka=k * ek, kd=k * ed)


def _tri(lower):
    r = lax.broadcasted_iota(jnp.int32, (CH, CH), 0)
    c = lax.broadcasted_iota(jnp.int32, (CH, CH), 1)
    return (r >= c) if lower else (c >= r)


def _head_segment(p_ref, rows, j, hg):
    return p_ref[rows, j * HD * hg:(j + 1) * HD * hg]


def _head(a, k):
    return a[:, k * HD:(k + 1) * HD]


def _hgrn_fwd(proj, lbw, rb, hg):
    assert hg == HEADS
    t = proj.shape[0]
    ncb = rb // CH

    def body(p_ref, lb_ref, o_ref, st_ref, s_scr):
        @pl.when(pl.program_id(1) == 0)
        def _():
            s_scr[...] = jnp.zeros_like(s_scr)

        lb = _lower_bound(lb_ref)
        causal = _tri(True)
        tril = _tri3(True)
        heads = range(hg)

        def chunk(cc, carry):
            r0 = pl.multiple_of(cc * CH, CH)
            rows = pl.ds(r0, CH)
            e = _hgrn_chunk_fwd(_head_segment(p_ref, rows, 0, hg), _head_segment(p_ref, rows, 1, hg), lb, tril)
            v = _bf(_head_segment(p_ref, rows, 2, hg))
            sts = [s_scr[k] for k in heads]
            qa, ka, qe, kd = _bf(e["qa"]), _bf(e["ka"]), _bf(e["qe"]), _bf(e["kd"])
            a = [_bf(jnp.where(causal, _mm_nt(_head(qa, k), _head(ka, k)), 0.0)) for k in heads]
            o_inter = [_mm_nt(_head(qe, k), _bf(sts[k])) for k in heads]
            kv = [_mm_tn(_head(v, k), _head(kd, k)) for k in heads]
            o_intra = [_mm(a[k], _head(v, k)) for k in heads]
            for k in heads:
                st_ref[cc, k] = sts[k]
                o_ref[rows, k * HD:(k + 1) * HD] = o_inter[k] + o_intra[k]
                s_scr[k] = sts[k] * _head(e["ebl"], k) + kv[k]
            return carry

        lax.fori_loop(0, ncb, chunk, 0, unroll=2)

    return pl.pallas_call(
        body, grid=(HEADS // hg, t // rb),
        in_specs=[pl.BlockSpec((rb, 3 * HD * hg), lambda h, i: (i, h)), pl.BlockSpec((2, HD * hg), lambda h, i: (0, h))],
        out_specs=[pl.BlockSpec((rb, HD * hg), lambda h, i: (i, h)),
                   pl.BlockSpec((ncb, hg, HD, HD), lambda h, i: (i, h, 0, 0))],
        out_shape=[jax.ShapeDtypeStruct((t, D), F32), jax.ShapeDtypeStruct((t // CH, HEADS, HD, HD), F32)],
        scratch_shapes=[pltpu.VMEM((hg, HD, HD), F32)],
        name="hgrn_fwd", compiler_params=_cp(48),
    )(proj, lbw)


def _hgrn_bwd(proj, lbw, do_raw, states, rb, hg):
    assert hg == HEADS
    t = proj.shape[0]
    nblk = t // rb
    ncb = rb // CH
    wd = HD * hg

    def body(p_ref, lb_ref, do_ref, st_ref, dp_ref, dlb_ref, ds_scr):
        @pl.when(pl.program_id(1) == 0)
        def _():
            ds_scr[...] = jnp.zeros_like(ds_scr)
            dlb_ref[...] = jnp.zeros_like(dlb_ref)

        lb = _lower_bound(lb_ref)
        causal = _tri(True)
        tril = _tri3(True)
        triu = _tri3(False)
        last_row = lax.broadcasted_iota(jnp.int32, (CH, HD * hg), 0) == CH - 1
        row0 = lax.broadcasted_iota(jnp.int32, (8, HD * hg), 0) == 0
        heads = range(hg)
        wide = lambda parts: jnp.concatenate(parts, axis=1)

        def chunk(it, carry):
            cc = ncb - 1 - it
            r0 = pl.multiple_of(cc * CH, CH)
            rows = pl.ds(r0, CH)
            hq = _head_segment(p_ref, rows, 0, hg)
            e = _hgrn_chunk_fwd(hq, _head_segment(p_ref, rows, 1, hg), lb, tril)
            v = _bf(_head_segment(p_ref, rows, 2, hg))
            do = _bf(do_ref[rows, :])
            sts = [st_ref[cc, k] for k in heads]
            dsts = [ds_scr[k] for k in heads]
            dlb_acc = dlb_ref[...]
            qa, ka, qe, kd = _bf(e["qa"]), _bf(e["ka"]), _bf(e["qe"]), _bf(e["kd"])
            a = [_bf(jnp.where(causal, _mm_nt(_head(qa, k), _head(ka, k)), 0.0)) for k in heads]
            da = [_bf(jnp.where(causal, _mm_nt(_head(do, k), _head(v, k)), 0.0)) for k in heads]
            dqe = wide([_mm(_head(do, k), _bf(sts[k])) for k in heads])
            dkd = wide([_mm(_head(v, k), _bf(dsts[k])) for k in heads])
            dv_state = [_mm_nt(_head(kd, k), _bf(dsts[k])) for k in heads]
            ds_new = [_mm_tn(_head(do, k), _head(qe, k)) for k in heads]
            dv_intra = [_mm_tn(a[k], _head(do, k)) for k in heads]
            dqa = wide([_mm(da[k], _head(ka, k)) for k in heads])
            dka = wide([_mm_tn(da[k], _head(qa, k)) for k in heads])
            dv = wide([dv_intra[k] + dv_state[k] for k in heads])
            dbl = e["ebl"] * wide([jnp.sum(sts[k] * dsts[k], axis=0, keepdims=True) for k in heads])
            dq = dqe * e["eb"] + dqa * e["ea"]
            dk = dka * e["ek"] + dkd * e["ed"]
            dkd_kd = dkd * kd.astype(F32)
            db = dqe * qe.astype(F32) + dqa * qa.astype(F32) - dka * ka.astype(F32) - dkd_kd
            db = db + jnp.where(last_row, dbl + jnp.sum(dkd_kd, axis=0, keepdims=True), 0.0)
            dg = _mm_tri_exact(triu, db)
            df = dg / e["f"] - dk
            sg = e["sg"]
            sq = e["sq"]
            dhq = _bf(dq * (sq * (1.0 + hq * (1.0 - sq))))
            dhf = _bf(df * (1.0 - lb) * sg * (1.0 - sg))
            dhi = _bf(dv)
            dlb_new = dlb_acc + jnp.where(row0, jnp.sum(df * (1.0 - sg), axis=0, keepdims=True), 0.0)
            for k in heads:
                ds_scr[k] = ds_new[k] + dsts[k] * _head(e["ebl"], k)
            dp_ref[rows, 0:wd] = dhq
            dp_ref[rows, wd:2 * wd] = dhf
            dp_ref[rows, 2 * wd:3 * wd] = dhi
            dlb_ref[...] = dlb_new
            return carry

        lax.fori_loop(0, ncb, chunk, 0, unroll=2)

    rev = lambda h, i: (nblk - 1 - i, h)
    return pl.pallas_call(
        body, grid=(HEADS // hg, nblk),
        in_specs=[pl.BlockSpec((rb, 3 * HD * hg), rev), pl.BlockSpec((2, HD * hg), lambda h, i: (0, h)),
                  pl.BlockSpec((rb, HD * hg), rev), pl.BlockSpec((ncb, hg, HD, HD), lambda h, i: (nblk - 1 - i, h, 0, 0))],
        out_specs=[pl.BlockSpec((rb, 3 * HD * hg), rev), pl.BlockSpec((8, HD * hg), lambda h, i: (0, h))],
        out_shape=[jax.ShapeDtypeStruct((t, 3 * D), BF16), jax.ShapeDtypeStruct((8, D), F32)],
        scratch_shapes=[pltpu.VMEM((hg, HD, HD), F32)],
        name="hgrn_bwd", compiler_params=_cp(48),
    )(proj, lbw, do_raw, states)


def _kv_variants(tile, odd):
    low = lax.broadcasted_iota(jnp.int32, tile.shape, 1) < 64
    if odd:
        hi = jnp.where(low, 0.0, tile)
        lo = pltpu.roll(hi, 64, 1)
    else:
        lo = jnp.where(low, tile, 0.0)
        hi = pltpu.roll(lo, 64, 1)
    return _bf(lo), _bf(hi)


def _attn_masks(n):
    qi = lax.broadcasted_iota(jnp.int32, (AB, AB), 0)
    kj = lax.broadcasted_iota(jnp.int32, (AB, AB), 1)
    cur = kj <= qi
    return cur, cur | (n > 0), qi <= kj


def _kv_all(prev_ref, cur_ref):
    out = []
    for tl in range(2):
        cols = slice(tl * 128, (tl + 1) * 128)
        tile = jnp.concatenate([prev_ref[:, cols], cur_ref[:, cols]], axis=0)
        out.append(_kv_variants(tile, 0))
        out.append(_kv_variants(tile, 1))
    return out


def _window(a2, cur):
    return jnp.where(cur, a2[:, AB:], a2[:, :AB])


def _attn_softmax(scores, sinks, cur, ok):
    s = [jnp.where(ok, _window(s2, cur) * ATT_SCALE, NEG) for s2 in scores]
    m = [jnp.maximum(jnp.max(si, axis=-1, keepdims=True), sink) for si, sink in zip(s, sinks)]
    p = [jnp.exp(si - mi) for si, mi in zip(s, m)]
    es = [jnp.exp(sink - mi) for sink, mi in zip(sinks, m)]
    inv = [1.0 / (jnp.sum(pi, axis=-1, keepdims=True) + ei) for pi, ei in zip(p, es)]
    return [pi * ii for pi, ii in zip(p, inv)], [ei * ii for ei, ii in zip(es, inv)]


def _spread(pc, cur):
    return jnp.concatenate([jnp.where(cur, 0.0, pc), jnp.where(cur, pc, 0.0)], axis=1)


def _spread_t(pct, cur_t):
    return jnp.concatenate([jnp.where(cur_t, 0.0, pct), jnp.where(cur_t, pct, 0.0)], axis=0)


def _attn_fwd(proj, sinks):
    t = proj.shape[0]
    nb = t // AB

    def body(q_ref, kc_ref, kp_ref, vc_ref, vp_ref, sink_ref, o_ref):
        cur, ok, _ = _attn_masks(pl.program_id(0))
        kvars = _kv_all(kp_ref, kc_ref)
        vvars = _kv_all(vp_ref, vc_ref)
        qps = [_bf(q_ref[:, 128 * j:128 * (j + 1)]) for j in range(8)]
        heads = [(j, ab) for j in range(8) for ab in range(2)]
        scores = [_mm_nt(qps[j], kvars[j // 2][ab]) for j, ab in heads]
        pcs, _ = _attn_softmax(scores, [sink_ref[0, h] for h in range(QH)], cur, ok)
        parts = [_mm(_bf(_spread(pcs[h], cur)), vvars[j // 2][ab]) for h, (j, ab) in enumerate(heads)]
        for j in range(8):
            o_ref[:, 128 * j:128 * (j + 1)] = parts[2 * j] + parts[2 * j + 1]

    prev = lambda n: jnp.maximum(n - 1, 0)
    return pl.pallas_call(
        body, grid=(nb,),
        in_specs=[pl.BlockSpec((AB, D), lambda n: (n, C_AQ // D)),
                  pl.BlockSpec((AB, 256), lambda n: (n, C_AK // 256)),
                  pl.BlockSpec((AB, 256), lambda n: (prev(n), C_AK // 256)),
                  pl.BlockSpec((AB, 256), lambda n: (n, C_AV // 256)),
                  pl.BlockSpec((AB, 256), lambda n: (prev(n), C_AV // 256)),
                  pl.BlockSpec(memory_space=pltpu.SMEM)],
        out_specs=pl.BlockSpec((AB, D), lambda n: (n, 0)),
        out_shape=jax.ShapeDtypeStruct((t, D), F32),
        name="attn_fwd", compiler_params=_cp(32),
    )(proj, proj, proj, proj, proj, sinks)


def _attn_bwd(proj, sinks, do_a):
    t = proj.shape[0]
    nb = t // AB

    def body(q_ref, kc_ref, kp_ref, vc_ref, vp_ref, do_ref, sink_ref, dq_ref, dkv_ref, dsink_ref, carry):
        n = pl.program_id(0)

        @pl.when(n == 0)
        def _():
            dsink_ref[...] = jnp.zeros_like(dsink_ref)
            carry[...] = jnp.zeros_like(carry)

        @pl.when(n < nb)
        def _():
            cur, ok, cur_t = _attn_masks(n)
            low = lax.broadcasted_iota(jnp.int32, (2 * AB, 128), 1) < 64
            lane = lax.broadcasted_iota(jnp.int32, (8, 128), 1)
            row0 = lax.broadcasted_iota(jnp.int32, (8, 128), 0) == 0
            kvars = _kv_all(kp_ref, kc_ref)
            vvars = _kv_all(vp_ref, vc_ref)
            qps = [_bf(q_ref[:, 128 * j:128 * (j + 1)]) for j in range(8)]
            dops = [_bf(do_ref[:, 128 * j:128 * (j + 1)]) for j in range(8)]
            heads = [(j, ab) for j in range(8) for ab in range(2)]
            scores = [_mm_nt(qps[j], kvars[j // 2][ab]) for j, ab in heads]
            dps = [_mm_nt(dops[j], vvars[j // 2][ab]) for j, ab in heads]
            pcs, pss = _attn_softmax(scores, [sink_ref[0, h] for h in range(QH)], cur, ok)
            dpcs = [_window(dp2, cur) for dp2 in dps]
            rss = [jnp.sum(pc * dpc, axis=-1, keepdims=True) for pc, dpc in zip(pcs, dpcs)]
            dscs = [pc * (dpc - rs) for pc, dpc, rs in zip(pcs, dpcs, rss)]
            dsink = jnp.zeros((8, 128), F32)
            for h in range(QH):
                dsink = dsink + jnp.where(row0 & (lane == h), -jnp.sum(pss[h] * rss[h]), 0.0)
            dq_terms = [_mm(_bf(_spread(dscs[h], cur)), kvars[j // 2][ab]) for h, (j, ab) in enumerate(heads)]
            for j in range(8):
                dq_ref[:, 128 * j:128 * (j + 1)] = _bf((dq_terms[2 * j] + dq_terms[2 * j + 1]) * ATT_SCALE)
            dsc_t = [_bf(_spread_t(dsc.T, cur_t)) for dsc in dscs]
            pc_t = [_bf(_spread_t(pc.T, cur_t)) for pc in pcs]
            dk_terms = [_mm(dsc_t[h], qps[j]) for h, (j, ab) in enumerate(heads)]
            dv_terms = [_mm(pc_t[h], dops[j]) for h, (j, ab) in enumerate(heads)]
            dk_ab = [[dk_terms[4 * g + ab] + dk_terms[4 * g + 2 + ab] for ab in range(2)] for g in range(4)]
            dv_ab = [[dv_terms[4 * g + ab] + dv_terms[4 * g + 2 + ab] for ab in range(2)] for g in range(4)]
            for tl in range(2):
                ke, ko = dk_ab[2 * tl], dk_ab[2 * tl + 1]
                ve, vo = dv_ab[2 * tl], dv_ab[2 * tl + 1]
                dkt = (jnp.where(low, ke[0], 0.0) + pltpu.roll(jnp.where(low, 0.0, ke[1]), 64, 1)
                       + jnp.where(low, 0.0, ko[1]) + pltpu.roll(jnp.where(low, ko[0], 0.0), 64, 1)) * ATT_SCALE
                dvt = (jnp.where(low, ve[0], 0.0) + pltpu.roll(jnp.where(low, 0.0, ve[1]), 64, 1)
                       + jnp.where(low, 0.0, vo[1]) + pltpu.roll(jnp.where(low, vo[0], 0.0), 64, 1))
                kcols = slice(tl * 128, (tl + 1) * 128)
                vcols = slice(256 + tl * 128, 256 + (tl + 1) * 128)
                dkv_ref[:, kcols] = _bf(carry[:, kcols] + dkt[0:AB])
                dkv_ref[:, vcols] = _bf(carry[:, vcols] + dvt[0:AB])
                carry[:, kcols] = dkt[AB:2 * AB]
                carry[:, vcols] = dvt[AB:2 * AB]
            dsink_ref[...] += dsink

        @pl.when(n == nb)
        def _():
            dkv_ref[...] = _bf(carry[...])

    cur = lambda n: jnp.minimum(n, nb - 1)
    prev = lambda n: jnp.clip(n - 1, 0, nb - 1)
    return pl.pallas_call(
        body, grid=(nb + 1,),
        in_specs=[pl.BlockSpec((AB, D), lambda n: (cur(n), C_AQ // D)),
                  pl.BlockSpec((AB, 256), lambda n: (cur(n), C_AK // 256)),
                  pl.BlockSpec((AB, 256), lambda n: (prev(n), C_AK // 256)),
                  pl.BlockSpec((AB, 256), lambda n: (cur(n), C_AV // 256)),
                  pl.BlockSpec((AB, 256), lambda n: (prev(n), C_AV // 256)),
                  pl.BlockSpec((AB, D), lambda n: (cur(n), 0)),
                  pl.BlockSpec(memory_space=pltpu.SMEM)],
        out_specs=[pl.BlockSpec((AB, D), lambda n: (cur(n), 0)),
                   pl.BlockSpec((AB, 512), lambda n: (prev(n), 0)),
                   pl.BlockSpec((8, 128), lambda n: (0, 0))],
        out_shape=[jax.ShapeDtypeStruct((t, D), BF16), jax.ShapeDtypeStruct((t, 512), BF16),
                   jax.ShapeDtypeStruct((8, 128), F32)],
        scratch_shapes=[pltpu.VMEM((AB, 512), F32)],
        name="attn_bwd", compiler_params=_cp(32),
    )(proj, proj, proj, proj, proj, do_a, sinks)


def _silu_and_grad(v):
    s = _sigmoid(v)
    return v * s, s * (1.0 + v * (1.0 - s))


def _tail(o_raw, o_a, proj, x2d, tgt, wbh, wba, wout, hnw, fnw, tb):
    t = x2d.shape[0]

    def body(or_ref, oa_ref, hg_ref, ag0, ag1, mh0, mh1, ma0, ma1, x_ref, t_ref, wbh_ref, wba_ref, wout_ref, hnw_ref,
             fnw_ref, dx2_ref, dor_ref, doa_ref, dhg_ref, dagm_ref, gh_ref, ga_ref, mg_ref, dyh_ref, dya_ref, dx2b_ref,
             sums_ref):
        @pl.when(pl.program_id(0) == 0)
        def _():
            sums_ref[...] = jnp.zeros_like(sums_ref)

        halves = lambda a, b: jnp.concatenate([a[...], b[...]], axis=1)
        hnw_v = hnw_ref[...]
        fnw_v = fnw_ref[...]
        o = or_ref[...]
        rs, xhs = [], []
        for h in range(HEADS):
            oh = o[:, h * HD:(h + 1) * HD]
            r = lax.rsqrt(jnp.mean(oh * oh, axis=-1, keepdims=True) + EPS)
            rs.append(r)
            xhs.append(oh * r)
        xh = jnp.concatenate(xhs, axis=1)
        on = xh * hnw_v
        sil_hg, dsil_hg = _silu_and_grad(hg_ref[...])
        gh_b = _bf(on * sil_hg)
        y_h = _mm(gh_b, wbh_ref[...])
        oa = oa_ref[...]
        sil_ag, dsil_ag = _silu_and_grad(halves(ag0, ag1))
        ga_b = _bf(oa * sil_ag)
        y_a = _mm(ga_b, wba_ref[...])
        s_mh = _sigmoid(halves(mh0, mh1))
        s_ma = _sigmoid(halves(ma0, ma1))
        mg_b = _bf(s_mh * y_h + s_ma * y_a)
        x2 = x_ref[...] + _mm(mg_b, wout_ref[...])
        r2 = lax.rsqrt(jnp.mean(x2 * x2, axis=-1, keepdims=True) + EPS)
        xh2 = x2 * r2
        err = xh2 * fnw_v - t_ref[...]
        loss = 0.5 * jnp.sum(jnp.mean(err * err, axis=-1, keepdims=True))
        dy = err * (1.0 / D)
        dfnw = jnp.sum(dy * xh2, axis=0, keepdims=True)
        dxh2 = dy * fnw_v
        dx2 = r2 * (dxh2 - xh2 * jnp.mean(dxh2 * xh2, axis=-1, keepdims=True))
        dx2_ref[...] = dx2
        dx2_b = _bf(dx2)
        dmg = _mm_nt(dx2_b, wout_ref[...])
        dmg_h = dmg * s_mh
        dmg_a = dmg * s_ma
        dyh_b = _bf(dmg_h)
        dya_b = _bf(dmg_a)
        dagm_ref[:, D:2 * D] = _bf(dmg_h * y_h * (1.0 - s_mh))
        dagm_ref[:, 2 * D:3 * D] = _bf(dmg_a * y_a * (1.0 - s_ma))
        dgh = _mm_nt(dyh_b, wbh_ref[...])
        dga = _mm_nt(dya_b, wba_ref[...])
        doa_ref[...] = dga * sil_ag
        dagm_ref[:, 0:D] = _bf(dga * oa * dsil_ag)
        dhg_ref[...] = _bf(dgh * on * dsil_hg)
        don = dgh * sil_hg
        dhnw = jnp.sum(don * xh, axis=0, keepdims=True)
        dxh = don * hnw_v
        dos = []
        for h in range(HEADS):
            sl = slice(h * HD, (h + 1) * HD)
            dos.append(rs[h] * (dxh[:, sl] - xhs[h] * jnp.mean(dxh[:, sl] * xhs[h], axis=-1, keepdims=True)))
        dor_ref[...] = jnp.concatenate(dos, axis=1)
        gh_ref[...] = gh_b
        ga_ref[...] = ga_b
        mg_ref[...] = mg_b
        dyh_ref[...] = dyh_b
        dya_ref[...] = dya_b
        dx2b_ref[...] = dx2_b
        row = lax.broadcasted_iota(jnp.int32, (8, D), 0)
        sums_ref[...] += jnp.where(row == 0, dfnw, 0.0) + jnp.where(row == 1, dhnw, 0.0) + jnp.where(row == 2, loss, 0.0)

    rowblk = lambda c: pl.BlockSpec((tb, D), lambda i: (i, c))
    half = lambda c: pl.BlockSpec((tb, 512), lambda i: (i, c))
    full = lambda shape: pl.BlockSpec(shape, lambda i: (0, 0))
    return pl.pallas_call(
        body, grid=(t // tb,),
        in_specs=[rowblk(0), rowblk(0), rowblk(C_HG // D), half(C_AG // 512), half(C_AG // 512 + 1), half(C_MH // 512),
                  half(C_MH // 512 + 1), half(C_MA // 512), half(C_MA // 512 + 1), rowblk(0), rowblk(0),
                  full((D, D)), full((D, D)), full((D, D)), full((1, D)), full((1, D))],
        out_specs=[rowblk(0), rowblk(0), rowblk(0), rowblk(0), pl.BlockSpec((tb, 3 * D), lambda i: (i, 0))]
        + [rowblk(0)] * 6 + [full((8, D))],
        out_shape=[jax.ShapeDtypeStruct((t, D), F32)] * 3
        + [jax.ShapeDtypeStruct((t, D), BF16), jax.ShapeDtypeStruct((t, 3 * D), BF16)]
        + [jax.ShapeDtypeStruct((t, D), BF16)] * 6 + [jax.ShapeDtypeStruct((8, D), F32)],
        name="tail", compiler_params=_cp(56),
    )(o_raw, o_a, proj, proj, proj, proj, proj, proj, proj, x2d, tgt, wbh, wba, wout, hnw, fnw)


def _wgrad3(gh, dyh, ga, dya, mg, dx2b, tk):
    t = dyh.shape[0]

    def body(a0, b0, a1, b1, a2, b2, o0, o1, o2):
        @pl.when(pl.program_id(0) == 0)
        def _():
            o0[...] = jnp.zeros_like(o0)
            o1[...] = jnp.zeros_like(o1)
            o2[...] = jnp.zeros_like(o2)

        o0[...] += _mm_tn(a0[...], b0[...])
        o1[...] += _mm_tn(a1[...], b1[...])
        o2[...] += _mm_tn(a2[...], b2[...])

    blk = pl.BlockSpec((tk, D), lambda k: (k, 0))
    out = pl.BlockSpec((D, D), lambda k: (0, 0))
    return pl.pallas_call(
        body, grid=(t // tk,), in_specs=[blk] * 6, out_specs=[out] * 3,
        out_shape=[jax.ShapeDtypeStruct((D, D), F32)] * 3,
        name="wgrad3", compiler_params=_cp(48),
    )(gh, dyh, ga, dya, mg, dx2b)


def _inproj_wgrad_piece(xnt, piece, nb, name):
    t = xnt.shape[1]
    width = piece.shape[1]

    def body(xnt_ref, p_ref, o_ref):
        o_ref[...] = _mm(xnt_ref[...], p_ref[...])

    return pl.pallas_call(
        body, grid=(width // nb,),
        in_specs=[pl.BlockSpec((D, t), lambda j: (0, 0), pipeline_mode=pl.Buffered(1)),
                  pl.BlockSpec((t, nb), lambda j: (0, j))],
        out_specs=pl.BlockSpec((D, nb), lambda j: (0, j)),
        out_shape=jax.ShapeDtypeStruct((D, width), F32),
        name=name, compiler_params=_cp(56),
    )(xnt, piece)


def _inproj_dgrad(pieces, w_p, x2d, dx2, norm_w, tb, after):
    t = x2d.shape[0]

    def body(*refs):
        piece_refs = refs[:len(pieces)]
        w_ref, x_ref, dx2_ref, nw_ref, _, gx_ref, dnw_ref = refs[len(pieces):]

        @pl.when(pl.program_id(0) == 0)
        def _():
            dnw_ref[...] = jnp.zeros_like(dnw_ref)

        dxn = None
        off = 0
        for p in piece_refs:
            width = p.shape[1]
            for q in range(w_ref.shape[0]):
                lo, hi = max(off, q * PAIR), min(off + width, (q + 1) * PAIR)
                if lo < hi:
                    term = _mm_nt(p[:, lo - off:hi - off], w_ref[q, :, lo - q * PAIR:hi - q * PAIR])
                    dxn = term if dxn is None else dxn + term
            off += width
        xv = x_ref[...]
        r = lax.rsqrt(jnp.mean(xv * xv, axis=-1, keepdims=True) + EPS)
        xh = xv * r
        dxh = dxn * nw_ref[...]
        gx_ref[...] = dx2_ref[...] + r * (dxh - xh * jnp.mean(dxh * xh, axis=-1, keepdims=True))
        row0 = lax.broadcasted_iota(jnp.int32, (8, D), 0) == 0
        dnw_ref[...] += jnp.where(row0, jnp.sum(dxn * xh, axis=0, keepdims=True), 0.0)

    rowblk = pl.BlockSpec((tb, D), lambda i: (i, 0))
    return pl.pallas_call(
        body, grid=(t // tb,),
        in_specs=[pl.BlockSpec((tb, p.shape[1]), lambda i: (i, 0)) for p in pieces]
        + [pl.BlockSpec(w_p.shape, lambda i: (0, 0, 0), pipeline_mode=pl.Buffered(1)), rowblk, rowblk,
           pl.BlockSpec((1, D), lambda i: (0, 0)), pl.BlockSpec(memory_space=pl.ANY)],
        out_specs=[rowblk, pl.BlockSpec((8, D), lambda i: (0, 0))],
        out_shape=[jax.ShapeDtypeStruct((t, D), F32), jax.ShapeDtypeStruct((8, D), F32)],
        name="inproj_dgrad", compiler_params=_cp(60),
    )(*pieces, w_p, x2d, dx2, norm_w, after)


def _adamw_math(w, g, m, v):
    m = B1 * m + (1.0 - B1) * g
    v = B2 * v + (1.0 - B2) * (g * g)
    m_hat = m / (1.0 - B1 ** STEP)
    v_hat = v / (1.0 - B2 ** STEP)
    delta = -LR * (m_hat / (jnp.sqrt(v_hat) + ADAM_EPS) + WD * w)
    return delta, m, v


def _adamw_shard(recv, sums, chip, w, m, v, rows, name):
    nparts, nr, nc = recv.shape

    def body(chip_ref, own_ref, p_ref, w_ref, m_ref, v_ref, g_ref, d_ref, nm_ref, nv_ref):
        g = own_ref[0].astype(F32)
        for s in range(nparts):
            g = g + p_ref[s].astype(F32)
        d, nm, nv = _adamw_math(w_ref[...], g, m_ref[...], v_ref[...])
        g_ref[...] = g
        d_ref[...] = d
        nm_ref[...] = nm
        nv_ref[...] = nv

    blk = pl.BlockSpec((rows, nc), lambda i, chip_ref: (i, 0))
    return pl.pallas_call(
        body,
        grid_spec=pltpu.PrefetchScalarGridSpec(
            num_scalar_prefetch=1, grid=(nr // rows,),
            in_specs=[pl.BlockSpec((1, rows, nc), lambda i, chip_ref: (chip_ref[0], i, 0)),
                      pl.BlockSpec((nparts, rows, nc), lambda i, chip_ref: (0, i, 0)), blk, blk, blk],
            out_specs=[blk] * 4),
        out_shape=[jax.ShapeDtypeStruct((nr, nc), F32)] * 4,
        name=name, compiler_params=_cp(48),
    )(chip, sums, recv, w, m, v)


SMALL_ROWS = dict(norm_w=0, lower_bound=1, hgrn_norm_w=3, final_norm_w=4, sinks=5, loss=6)


def _pack_small_grads(dnw, dlb, sums, dsink):
    def body(dnw_ref, dlb_ref, sums_ref, dsink_ref, o_ref):
        o_ref[...] = jnp.zeros_like(o_ref)
        o_ref[0:1, :] = dnw_ref[0:1, :]
        o_ref[1:2, :] = dlb_ref[0:1, :]
        o_ref[3:4, :] = sums_ref[1:2, :]
        o_ref[4:5, :] = sums_ref[0:1, :]
        o_ref[5:6, 0:128] = dsink_ref[0:1, :]
        o_ref[6:7, :] = sums_ref[2:3, :]

    return pl.pallas_call(body, out_shape=jax.ShapeDtypeStruct((8, D), F32), name="pack_small_grads")(dnw, dlb, sums, dsink)


def _adamw_small(parts, ws, ms, vs):
    shapes = [a.shape for a in ws]

    def body(p_ref, *refs):
        w, m, v = refs[0:5], refs[5:10], refs[10:15]
        outs = [refs[15 + 5 * i:20 + 5 * i] for i in range(4)]
        loss_ref = refs[35]

        def total(row, width):
            g = p_ref[0, row:row + 1, 0:width]
            for s in range(1, NDEV):
                g = g + p_ref[s, row:row + 1, 0:width]
            return g

        loss_ref[...] = total(6, 128)
        lb = _lower_bound(w[1])
        ga0 = total(1, D) * lb * (1.0 - lb)
        grads = [total(0, D), None, total(3, D), total(4, D), total(5, QH)]
        for i in (0, 2, 3, 4):
            res = (grads[i],) + _adamw_math(w[i][...], grads[i], m[i][...], v[i][...])
            for o, val in zip(outs, res):
                o[i][...] = val
        for r, g in ((0, ga0), (1, -ga0)):
            res = (g,) + _adamw_math(w[1][r:r + 1, :], g, m[1][r:r + 1, :], v[1][r:r + 1, :])
            for o, val in zip(outs, res):
                o[1][r:r + 1, :] = val

    res = pl.pallas_call(
        body, out_shape=[jax.ShapeDtypeStruct(s, F32) for s in shapes] * 4 + [jax.ShapeDtypeStruct((1, 128), F32)],
        name="adamw_small",
    )(parts, *ws, *ms, *vs)
    return [res[5 * i:5 * i + 5] for i in range(4)], res[20][0, 0]


def kernel(x, norm_w, w_in, hgrn_lower_bound, hgrn_norm_w, w_branch_hgrn, attn_sinks, w_branch_attn, w_out, final_norm_w, loss_target, m_norm_w, m_w_in, m_hgrn_lower_bound, m_hgrn_norm_w, m_w_branch_hgrn, m_attn_sinks, m_w_branch_attn, m_w_out, m_final_norm_w, v_norm_w, v_w_in, v_hgrn_lower_bound, v_hgrn_norm_w, v_w_branch_hgrn, v_attn_sinks, v_w_branch_attn, v_w_out, v_final_norm_w):
    t = x.shape[1]
    x2d = x.reshape(t, D)
    tgt = loss_target.reshape(t, D)
    fnw = final_norm_w.reshape(1, D)
    row_blk = min(256, t)
    big_blk = min(512, t)

    w_p = _gather_in_projection(w_in[0])
    wbh, wba, wout = (g.reshape(D, D) for g in _gather_square(
        [w_branch_hgrn[0].astype(BF16), w_branch_attn[0].astype(BF16), w_out[0].astype(BF16)], after=w_p))

    proj, xnt = _inproj_fwd(x2d, norm_w, w_p, min(1024, t))
    o_raw, states = _hgrn_fwd(proj, hgrn_lower_bound, big_blk, HGRN_GROUP)
    o_a = _attn_fwd(proj, attn_sinks)
    (dx2, do_raw, do_a, d_hg, d_agm, gh, ga, mg, dyh, dya, dx2b, sums) = _tail(
        o_raw, o_a, proj, x2d, tgt, wbh, wba, wout, hgrn_norm_w, fnw, row_blk)
    dwbh, dwba, dwout = _wgrad3(gh, dyh, ga, dya, mg, dx2b, big_blk)
    d_aq, d_kv, dsink = _attn_bwd(proj, attn_sinks, do_a)
    d_hgrn, dlb = _hgrn_bwd(proj, hgrn_lower_bound, do_raw, states, big_blk, HGRN_GROUP)
    pieces = (d_hgrn, d_hg, d_aq, d_kv, d_agm)
    dw_pieces = [_inproj_wgrad_piece(xnt, p, CB, "inproj_wgrad_" + n)
                 for p, n in zip(pieces, ("hgrn", "hgate", "aq", "kv", "gates"))]

    dwin_r = jnp.concatenate(dw_pieces, axis=1).reshape(D, NDEV, IN_SHARD).transpose(1, 0, 2).astype(BF16)
    slots = lambda a: a.reshape(NDEV, ROW_SHARD, D).astype(BF16)
    partials = [dwin_r, slots(dwbh), slots(dwba), slots(dwout)]
    got = _exchange_pair(partials)
    core = lax.axis_index("c").astype(jnp.int32).reshape(1)
    pair_sums = [_pair_sum(a, g, core, ROW_SHARD, "pair_sum_" + n)
                 for a, g, n in zip(partials, got, ("w_in", "w_bh", "w_ba", "w_out"))]
    rin, rbh, rba, rout = _exchange_chips(pair_sums)
    grad_x, dnw = _inproj_dgrad(pieces, w_p, x2d, dx2, norm_w, big_blk, after=pair_sums[0])
    rsm = _exchange_small(_pack_small_grads(dnw, dlb, sums, dsink))
    chip = (2 * lax.axis_index("x") + lax.axis_index("y")).astype(jnp.int32).reshape(1)
    s_in, s_bh, s_ba, s_out = pair_sums
    g_in, d_in, nm_in, nv_in = _adamw_shard(rin, s_in, chip, w_in[0], m_w_in[0], v_w_in[0], 128, "adamw_w_in")
    g_bh, d_bh, nm_bh, nv_bh = _adamw_shard(
        rbh, s_bh, chip, w_branch_hgrn[0], m_w_branch_hgrn[0], v_w_branch_hgrn[0], 128, "adamw_w_bh")
    g_ba, d_ba, nm_ba, nv_ba = _adamw_shard(
        rba, s_ba, chip, w_branch_attn[0], m_w_branch_attn[0], v_w_branch_attn[0], 128, "adamw_w_ba")
    g_out, d_out, nm_out, nv_out = _adamw_shard(rout, s_out, chip, w_out[0], m_w_out[0], v_w_out[0], 128, "adamw_w_out")
    (sg, sd, sm, sv), loss = _adamw_small(
        rsm,
        (norm_w, hgrn_lower_bound, hgrn_norm_w, fnw, attn_sinks),
        (m_norm_w, m_hgrn_lower_bound, m_hgrn_norm_w, m_final_norm_w.reshape(1, D), m_attn_sinks),
        (v_norm_w, v_hgrn_lower_bound, v_hgrn_norm_w, v_final_norm_w.reshape(1, D), v_attn_sinks))

    def group(s, w_in_v, bh, ba, out):
        nw, lb, hnw, fn, sinks = s
        return (nw, w_in_v[None], lb, hnw, bh[None], sinks, ba[None], out[None], fn.reshape(D))

    return (loss, grad_x.reshape(1, t, D),
            *group(sg, g_in, g_bh, g_ba, g_out), *group(sd, d_in, d_bh, d_ba, d_out),
            *group(sm, nm_in, nm_bh, nm_ba, nm_out), *group(sv, nv_in, nv_bh, nv_ba, nv_out))
```

```python
import functools

import jax
import jax.numpy as jnp
from jax import lax
from jax.experimental import pallas as pl
from jax.experimental.pallas import tpu as pltpu
from jax.experimental.pallas import tpu_sc as plsc

F32 = jnp.float32
BF16 = jnp.bfloat16

D = 1024
DIN = 8704
NDEV = 8
IN_SHARD = DIN // NDEV
PAIR = 2 * IN_SHARD
ROW_SHARD = D // NDEV
HEADS = 8
HD = 128
CH = 64
HGRN_GROUP = 8
QH = 16
AB = 128
EPS = 1e-6
NEG = -1e30
ATT_SCALE = 0.125

C_HGRN = 0
C_HG = 3072
C_AQ = 4096
C_AK = 5120
C_AV = 5376
C_AG = 5632
C_MH = 6656
C_MA = 7680
CB = 512

LR = 0.001
B1 = 0.9
B2 = 0.999
ADAM_EPS = 1e-08
WD = 0.01
STEP = 10

V7X_VMEM_BYTES = 64 * 1024 * 1024
MESH = pl.DeviceIdType.MESH


def _cp(vmem_mb):
    return pltpu.CompilerParams(vmem_limit_bytes=vmem_mb * 1024 * 1024)


def _mm(a, b):
    return jnp.dot(a, b, preferred_element_type=F32)


def _mm_nt(a, b):
    return lax.dot_general(a, b, (((1,), (1,)), ((), ())), preferred_element_type=F32)


def _mm_tn(a, b):
    return lax.dot_general(a, b, (((0,), (0,)), ((), ())), preferred_element_type=F32)


def _tri3(lower):
    r = lax.broadcasted_iota(jnp.int32, (CH, 3 * CH), 0)
    c = lax.broadcasted_iota(jnp.int32, (CH, 3 * CH), 1)
    c = jnp.where(c >= 2 * CH, c - 2 * CH, jnp.where(c >= CH, c - CH, c))
    return ((r >= c) if lower else (c >= r)).astype(BF16)


def _mm_tri_exact(tri3, g):
    g1 = g.astype(BF16)
    r1 = g - g1.astype(F32)
    g2 = r1.astype(BF16)
    g3 = (r1 - g2.astype(F32)).astype(BF16)
    return _mm(tri3, jnp.concatenate([g1, g2, g3], axis=0))


def _sigmoid(v):
    return 0.5 * jnp.tanh(0.5 * v) + 0.5


def _bf(v):
    return v.astype(BF16)


def _place():
    x, y, c = lax.axis_index("x"), lax.axis_index("y"), lax.axis_index("c")
    return (x, y, c), (x, y, 1 - c), [(1 - x, y), (x, 1 - y), (1 - x, 1 - y)]


def _dev_index(px, py, pc):
    return 4 * px + 2 * py + pc


def _gather_in_projection(w_in_s, x2d, norm_w):
    half = D // 2
    t = x2d.shape[0]
    prep_rows = min(512, t)
    nprep = t // prep_rows

    def body(win_ref, x_hbm, nw_ref, wp_g, xn_hbm, xnt_hbm, give, take, mine, xbuf, xnbuf, xntbuf,
             send_sems, recv_sems, loc_sem, swap_sems, in_sems, out_sems):
        (x, y, c), sibling, chips = _place()
        give[...] = win_ref[pl.ds(pl.multiple_of(half * (1 - c), half), half), :].astype(BF16)
        swap = pltpu.make_async_remote_copy(src_ref=give, dst_ref=take, send_sem=swap_sems.at[0], recv_sem=swap_sems.at[1],
                                            device_id=sibling, device_id_type=MESH)
        swap.start()
        swap.wait()
        own = win_ref[pl.ds(pl.multiple_of(half * c, half), half), :]
        other = take[...].astype(F32)
        mine[...] = jnp.where(c == 0, jnp.concatenate([own, other], axis=1),
                              jnp.concatenate([other, own], axis=1)).astype(BF16)

        def place(px, py, pc):
            return wp_g.at[2 * px + py, pl.ds(pl.multiple_of(half * pc, half), half), :]

        def copy(kind, origin, to, src=mine):
            return pltpu.make_async_remote_copy(
                src_ref=src, dst_ref=place(*origin), send_sem=send_sems.at[kind], recv_sem=recv_sems.at[kind],
                device_id=to, device_id_type=MESH)

        me = (x, y, c)
        local = pltpu.make_async_copy(mine, place(*me), loc_sem)
        local.start()
        first = [copy(0, me, sibling)] + [copy(1 + j, me, (*chip, c)) for j, chip in enumerate(chips)]
        for cp in first:
            cp.start()

        def rows_of(i):
            return pl.ds(pl.multiple_of(i * prep_rows, prep_rows), prep_rows)

        def load(i, slot):
            return pltpu.make_async_copy(x_hbm.at[rows_of(i), :], xbuf.at[slot], in_sems.at[slot])

        def stores(i, slot):
            return (pltpu.make_async_copy(xnbuf.at[slot], xn_hbm.at[rows_of(i), :], out_sems.at[slot, 0]),
                    pltpu.make_async_copy(xntbuf.at[slot], xnt_hbm.at[:, rows_of(i)], out_sems.at[slot, 1]))

        load(0, 0).start()

        def prep(i, carry):
            slot = lax.rem(i, 2)
            load(i, slot).wait()

            @pl.when(i + 1 < nprep)
            def _():
                load(i + 1, 1 - slot).start()

            @pl.when(i >= 2)
            def _():
                for cp in stores(i - 2, slot):
                    cp.wait()

            xv = xbuf[slot]
            xn = (xv * lax.rsqrt(jnp.mean(xv * xv, axis=-1, keepdims=True) + EPS)) * nw_ref[...]
            xnbuf[slot] = xn.astype(BF16)
            xntbuf[slot] = xn.T.astype(BF16)
            for cp in stores(i, slot):
                cp.start()
            return carry

        lax.fori_loop(0, nprep, prep, 0)
        for i in range(max(nprep - 2, 0), nprep):
            for cp in stores(i, i % 2):
                cp.wait()

        passed = []
        for j, chip in enumerate(chips):
            copy(1 + j, (*chip, c), me).wait_recv()
            cp = copy(4 + j, (*chip, c), sibling, src=place(*chip, c))
            cp.start()
            passed.append(cp)
        copy(0, (x, y, 1 - c), me).wait_recv()
        for j, chip in enumerate(chips):
            copy(4 + j, (*chip, 1 - c), me).wait_recv()
        for cp in first + passed:
            cp.wait_send()
        local.wait()

    vm = pl.BlockSpec(memory_space=pltpu.VMEM)
    hbm = pl.BlockSpec(memory_space=pl.ANY)
    return pl.pallas_call(
        body,
        out_shape=[jax.ShapeDtypeStruct((NDEV // 2, D, PAIR), BF16), jax.ShapeDtypeStruct((t, D), BF16),
                   jax.ShapeDtypeStruct((D, t), BF16)],
        in_specs=[vm, hbm, vm],
        out_specs=[hbm, hbm, hbm],
        scratch_shapes=[pltpu.VMEM((half, IN_SHARD), BF16), pltpu.VMEM((half, IN_SHARD), BF16),
                        pltpu.VMEM((half, PAIR), BF16),
                        pltpu.VMEM((2, prep_rows, D), F32), pltpu.VMEM((2, prep_rows, D), BF16),
                        pltpu.VMEM((2, D, prep_rows), BF16),
                        pltpu.SemaphoreType.DMA((NDEV - 1,)), pltpu.SemaphoreType.DMA((NDEV - 1,)),
                        pltpu.SemaphoreType.DMA, pltpu.SemaphoreType.DMA((2,)),
                        pltpu.SemaphoreType.DMA((2,)), pltpu.SemaphoreType.DMA((2, 2))],
        name="gather_in_projection", compiler_params=_cp(48),
    )(w_in_s, x2d, norm_w)


def _gather_square(shards, after):
    n = len(shards)

    def launch(*refs):
        ins, outs = refs[:n], refs[n + 1:2 * n + 1]
        send_sems, recv_sems, loc_sems = refs[2 * n + 1:]
        (x, y, c), _, _ = _place()
        me = _dev_index(x, y, c)
        peers = [(1 - x if r & 4 else x, 1 - y if r & 2 else y, 1 - c if r & 1 else c) for r in range(1, NDEV)]
        barrier = pltpu.get_barrier_semaphore()
        for peer in peers:
            pl.semaphore_signal(barrier, inc=1, device_id=peer, device_id_type=MESH)
        pl.semaphore_wait(barrier, NDEV - 1)
        local = [pltpu.make_async_copy(ins[k], outs[k].at[me], loc_sems.at[k]) for k in range(n)]
        copies = [pltpu.make_async_remote_copy(
            src_ref=ins[k], dst_ref=outs[k].at[me], send_sem=send_sems.at[r, k], recv_sem=recv_sems.at[r, k],
            device_id=peer, device_id_type=MESH) for r, peer in enumerate(peers) for k in range(n)]
        for cp in local + copies:
            cp.start()
        for r, peer in enumerate(peers):
            for k in range(n):
                pltpu.make_async_remote_copy(
                    src_ref=ins[k], dst_ref=outs[k].at[_dev_index(*peer)], send_sem=send_sems.at[r, k],
                    recv_sem=recv_sems.at[r, k], device_id=peer, device_id_type=MESH).wait_recv()
        for cp in copies:
            cp.wait_send()
        for cp in local:
            cp.wait()

    return pl.kernel(
        launch, out_type=[jax.ShapeDtypeStruct((NDEV,) + a.shape, a.dtype) for a in shards],
        mesh=plsc.ScalarSubcoreMesh(axis_name="sequencer", num_cores=1), name="gather_square",
        scratch_types=(pltpu.SemaphoreType.DMA((NDEV - 1, n)), pltpu.SemaphoreType.DMA((NDEV - 1, n)),
                       pltpu.SemaphoreType.DMA((n,))),
        compiler_params=pltpu.CompilerParams(collective_id=2),
    )(*shards, after)


def _exchange_pair(arrs):
    n = len(arrs)

    def launch(*refs):
        ins, got = refs[:n], refs[n:2 * n]
        send_sems, recv_sems = refs[2 * n:]
        (x, y, c), sibling, _ = _place()
        barrier = pltpu.get_barrier_semaphore()
        pl.semaphore_signal(barrier, inc=1, device_id=sibling, device_id_type=MESH)
        pl.semaphore_wait(barrier, 1)
        sends = [pltpu.make_async_remote_copy(
            src_ref=ins[k].at[_dev_index(q // 2, q % 2, 1 - c)], dst_ref=got[k].at[q], send_sem=send_sems.at[q, k],
            recv_sem=recv_sems.at[q, k], device_id=sibling, device_id_type=MESH) for q in range(4) for k in range(n)]
        for cp in sends:
            cp.start()
        for cp in sends:
            cp.wait_recv()
        for cp in sends:
            cp.wait_send()

    return pl.kernel(
        launch, out_type=[jax.ShapeDtypeStruct((4,) + a.shape[1:], a.dtype) for a in arrs],
        mesh=plsc.ScalarSubcoreMesh(axis_name="sequencer", num_cores=1), name="exchange_pair",
        scratch_types=(pltpu.SemaphoreType.DMA((4, n)), pltpu.SemaphoreType.DMA((4, n))),
        compiler_params=pltpu.CompilerParams(collective_id=0),
    )(*arrs)


def _pair_sum(full, got, core, rows, name):
    _, nr, nc = got.shape

    def body(core_ref, a_ref, b_ref, o_ref):
        o_ref[...] = (a_ref[...].astype(F32) + b_ref[...].astype(F32)).astype(BF16)

    blk = pl.BlockSpec((1, rows, nc), lambda q, i, core_ref: (q, i, 0))
    return pl.pallas_call(
        body,
        grid_spec=pltpu.PrefetchScalarGridSpec(
            num_scalar_prefetch=1, grid=(4, nr // rows),
            in_specs=[pl.BlockSpec((1, rows, nc), lambda q, i, core_ref: (2 * q + core_ref[0], i, 0)), blk],
            out_specs=blk),
        out_shape=jax.ShapeDtypeStruct(got.shape, BF16), name=name,
    )(core, full, got)


def _exchange_chips(sums):
    n = len(sums)

    def launch(*refs):
        ins, outs = refs[:n], refs[n:2 * n]
        send_sems, recv_sems = refs[2 * n:]
        (x, y, c), _, chips = _place()
        barrier = pltpu.get_barrier_semaphore()
        for px, py in chips:
            pl.semaphore_signal(barrier, inc=1, device_id=(px, py, c), device_id_type=MESH)
        pl.semaphore_wait(barrier, len(chips))
        copies = [pltpu.make_async_remote_copy(
            src_ref=ins[k].at[2 * px + py], dst_ref=outs[k].at[j], send_sem=send_sems.at[j, k],
            recv_sem=recv_sems.at[j, k], device_id=(px, py, c), device_id_type=MESH)
            for j, (px, py) in enumerate(chips) for k in range(n)]
        for cp in copies:
            cp.start()
        for cp in copies:
            cp.wait_recv()
        for cp in copies:
            cp.wait_send()

    return pl.kernel(
        launch, out_type=[jax.ShapeDtypeStruct((3,) + a.shape[1:], a.dtype) for a in sums],
        mesh=plsc.ScalarSubcoreMesh(axis_name="sequencer", num_cores=1), name="exchange_chips",
        scratch_types=(pltpu.SemaphoreType.DMA((3, n)), pltpu.SemaphoreType.DMA((3, n))),
        compiler_params=pltpu.CompilerParams(collective_id=1),
    )(*sums)


def _exchange_small(small):
    def body(sm_ref, out_ref, send_sems, recv_sems):
        (x, y, c), _, _ = _place()
        me = _dev_index(x, y, c)
        peers = [(1 - x if r & 4 else x, 1 - y if r & 2 else y, 1 - c if r & 1 else c) for r in range(1, NDEV)]
        out_ref[me] = sm_ref[...]
        copies = [pltpu.make_async_remote_copy(
            src_ref=sm_ref, dst_ref=out_ref.at[me], send_sem=send_sems.at[r], recv_sem=recv_sems.at[r],
            device_id=peer, device_id_type=MESH) for r, peer in enumerate(peers)]
        for cp in copies:
            cp.start()
        for r, peer in enumerate(peers):
            pltpu.make_async_remote_copy(
                src_ref=sm_ref, dst_ref=out_ref.at[_dev_index(*peer)], send_sem=send_sems.at[r], recv_sem=recv_sems.at[r],
                device_id=peer, device_id_type=MESH).wait_recv()
        for cp in copies:
            cp.wait_send()

    vm = pl.BlockSpec(memory_space=pltpu.VMEM)
    return pl.pallas_call(
        body, out_shape=jax.ShapeDtypeStruct((NDEV,) + small.shape, F32), in_specs=[vm], out_specs=vm,
        scratch_shapes=[pltpu.SemaphoreType.DMA((NDEV - 1,)), pltpu.SemaphoreType.DMA((NDEV - 1,))],
        name="exchange_small",
    )(small)


def _inproj_fwd(xn, w_pairs, tb):
    t = xn.shape[0]
    nblk, _, nb = w_pairs.shape

    def body(xn_ref, w_ref, proj_ref):
        proj_ref[...] = _mm(xn_ref[...], w_ref[0])

    return pl.pallas_call(
        body, grid=(t // tb, nblk),
        in_specs=[pl.BlockSpec((tb, D), lambda i, j: (i, 0)), pl.BlockSpec((1, D, nb), lambda i, j: (j, 0, 0))],
        out_specs=pl.BlockSpec((tb, nb), lambda i, j: (i, j)),
        out_shape=jax.ShapeDtypeStruct((t, DIN), F32),
        name="inproj_fwd", compiler_params=_cp(56),
    )(xn, w_pairs)


def _lower_bound(lb_ref):
    a0 = lb_ref[0:1, :]
    a1 = lb_ref[1:2, :]
    mx = jnp.maximum(a0, a1)
    e0 = jnp.exp(a0 - mx)
    e1 = jnp.exp(a1 - mx)
    return e0 / (e0 + e1)


def _hgrn_chunk_fwd(hq, hf, lb, tril):
    sg = _sigmoid(hf)
    f = lb + (1.0 - lb) * sg
    g = jnp.log(f)
    k = 1.0 - f
    sq = _sigmoid(hq)
    q = hq * sq
    b = _mm_tri_exact(tril, g)
    last_row = lax.broadcasted_iota(jnp.int32, b.shape, 0) == CH - 1
    b_last = jnp.sum(jnp.where(last_row, b, 0.0), axis=0, keepdims=True)
    c = 0.5 * b_last
    eb = jnp.exp(b)
    ea = jnp.exp(b - c)
    ek = jnp.exp(c - b)
    ed = jnp.exp(b_last - b)
    ebl = jnp.exp(b_last)
    return dict(sg=sg, f=f, k=k, sq=sq, q=q, eb=eb, ea=ea, ek=ek, ed=ed, ebl=ebl,
                qe=q * eb, qa=q * ea, ka=k * ek, kd=k * ed)


def _tri(lower):
    r = lax.broadcasted_iota(jnp.int32, (CH, CH), 0)
    c = lax.broadcasted_iota(jnp.int32, (CH, CH), 1)
    return (r >= c) if lower else (c >= r)


def _head_segment(p_ref, rows, j, hg):
    return p_ref[rows, j * HD * hg:(j + 1) * HD * hg]


def _head(a, k):
    return a[:, k * HD:(k + 1) * HD]


def _hgrn_fwd(proj, lbw, rb, hg):
    assert hg == HEADS
    t = proj.shape[0]
    ncb = rb // CH

    def body(p_ref, lb_ref, o_ref, st_ref, s_scr):
        @pl.when(pl.program_id(1) == 0)
        def _():
            s_scr[...] = jnp.zeros_like(s_scr)

        lb = _lower_bound(lb_ref)
        causal = _tri(True)
        tril = _tri3(True)
        heads = range(hg)

        def chunk(cc, carry):
            r0 = pl.multiple_of(cc * CH, CH)
            rows = pl.ds(r0, CH)
            e = _hgrn_chunk_fwd(_head_segment(p_ref, rows, 0, hg), _head_segment(p_ref, rows, 1, hg), lb, tril)
            v = _bf(_head_segment(p_ref, rows, 2, hg))
            sts = [s_scr[k] for k in heads]
            qa, ka, qe, kd = _bf(e["qa"]), _bf(e["ka"]), _bf(e["qe"]), _bf(e["kd"])
            a = [_bf(jnp.where(causal, _mm_nt(_head(qa, k), _head(ka, k)), 0.0)) for k in heads]
            o_inter = [_mm_nt(_head(qe, k), _bf(sts[k])) for k in heads]
            kv = [_mm_tn(_head(v, k), _head(kd, k)) for k in heads]
            o_intra = [_mm(a[k], _head(v, k)) for k in heads]
            for k in heads:
                st_ref[cc, k] = sts[k]
                o_ref[rows, k * HD:(k + 1) * HD] = o_inter[k] + o_intra[k]
                s_scr[k] = sts[k] * _head(e["ebl"], k) + kv[k]
            return carry

        lax.fori_loop(0, ncb, chunk, 0, unroll=2)

    return pl.pallas_call(
        body, grid=(HEADS // hg, t // rb),
        in_specs=[pl.BlockSpec((rb, 3 * HD * hg), lambda h, i: (i, h)), pl.BlockSpec((2, HD * hg), lambda h, i: (0, h))],
        out_specs=[pl.BlockSpec((rb, HD * hg), lambda h, i: (i, h)),
                   pl.BlockSpec((ncb, hg, HD, HD), lambda h, i: (i, h, 0, 0))],
        out_shape=[jax.ShapeDtypeStruct((t, D), F32), jax.ShapeDtypeStruct((t // CH, HEADS, HD, HD), F32)],
        scratch_shapes=[pltpu.VMEM((hg, HD, HD), F32)],
        name="hgrn_fwd", compiler_params=_cp(48),
    )(proj, lbw)


def _hgrn_bwd(proj, lbw, do_raw, states, rb, hg):
    assert hg == HEADS
    t = proj.shape[0]
    nblk = t // rb
    ncb = rb // CH
    wd = HD * hg

    def body(p_ref, lb_ref, do_ref, st_ref, dp_ref, dlb_ref, ds_scr):
        @pl.when(pl.program_id(1) == 0)
        def _():
            ds_scr[...] = jnp.zeros_like(ds_scr)
            dlb_ref[...] = jnp.zeros_like(dlb_ref)

        lb = _lower_bound(lb_ref)
        causal = _tri(True)
        tril = _tri3(True)
        triu = _tri3(False)
        last_row = lax.broadcasted_iota(jnp.int32, (CH, HD * hg), 0) == CH - 1
        row0 = lax.broadcasted_iota(jnp.int32, (8, HD * hg), 0) == 0
        heads = range(hg)
        wide = lambda parts: jnp.concatenate(parts, axis=1)

        def chunk(it, carry):
            cc = ncb - 1 - it
            r0 = pl.multiple_of(cc * CH, CH)
            rows = pl.ds(r0, CH)
            hq = _head_segment(p_ref, rows, 0, hg)
            e = _hgrn_chunk_fwd(hq, _head_segment(p_ref, rows, 1, hg), lb, tril)
            v = _bf(_head_segment(p_ref, rows, 2, hg))
            do = _bf(do_ref[rows, :])
            sts = [st_ref[cc, k] for k in heads]
            dsts = [ds_scr[k] for k in heads]
            dlb_acc = dlb_ref[...]
            qa, ka, qe, kd = _bf(e["qa"]), _bf(e["ka"]), _bf(e["qe"]), _bf(e["kd"])
            a = [_bf(jnp.where(causal, _mm_nt(_head(qa, k), _head(ka, k)), 0.0)) for k in heads]
            da = [_bf(jnp.where(causal, _mm_nt(_head(do, k), _head(v, k)), 0.0)) for k in heads]
            dqe = wide([_mm(_head(do, k), _bf(sts[k])) for k in heads])
            dkd = wide([_mm(_head(v, k), _bf(dsts[k])) for k in heads])
            dv_state = [_mm_nt(_head(kd, k), _bf(dsts[k])) for k in heads]
            ds_new = [_mm_tn(_head(do, k), _head(qe, k)) for k in heads]
            dv_intra = [_mm_tn(a[k], _head(do, k)) for k in heads]
            dqa = wide([_mm(da[k], _head(ka, k)) for k in heads])
            dka = wide([_mm_tn(da[k], _head(qa, k)) for k in heads])
            dv = wide([dv_intra[k] + dv_state[k] for k in heads])
            dbl = e["ebl"] * wide([jnp.sum(sts[k] * dsts[k], axis=0, keepdims=True) for k in heads])
            dq = dqe * e["eb"] + dqa * e["ea"]
            dk = dka * e["ek"] + dkd * e["ed"]
            dkd_kd = dkd * kd.astype(F32)
            db = dqe * qe.astype(F32) + dqa * qa.astype(F32) - dka * ka.astype(F32) - dkd_kd
            db = db + jnp.where(last_row, dbl + jnp.sum(dkd_kd, axis=0, keepdims=True), 0.0)
            dg = _mm_tri_exact(triu, db)
            df = dg / e["f"] - dk
            sg = e["sg"]
            sq = e["sq"]
            dhq = _bf(dq * (sq * (1.0 + hq * (1.0 - sq))))
            dhf = _bf(df * (1.0 - lb) * sg * (1.0 - sg))
            dhi = _bf(dv)
            dlb_new = dlb_acc + jnp.where(row0, jnp.sum(df * (1.0 - sg), axis=0, keepdims=True), 0.0)
            for k in heads:
                ds_scr[k] = ds_new[k] + dsts[k] * _head(e["ebl"], k)
            dp_ref[rows, 0:wd] = dhq
            dp_ref[rows, wd:2 * wd] = dhf
            dp_ref[rows, 2 * wd:3 * wd] = dhi
            dlb_ref[...] = dlb_new
            return carry

        lax.fori_loop(0, ncb, chunk, 0, unroll=2)

    rev = lambda h, i: (nblk - 1 - i, h)
    return pl.pallas_call(
        body, grid=(HEADS // hg, nblk),
        in_specs=[pl.BlockSpec((rb, 3 * HD * hg), rev), pl.BlockSpec((2, HD * hg), lambda h, i: (0, h)),
                  pl.BlockSpec((rb, HD * hg), rev), pl.BlockSpec((ncb, hg, HD, HD), lambda h, i: (nblk - 1 - i, h, 0, 0))],
        out_specs=[pl.BlockSpec((rb, 3 * HD * hg), rev), pl.BlockSpec((8, HD * hg), lambda h, i: (0, h))],
        out_shape=[jax.ShapeDtypeStruct((t, 3 * D), BF16), jax.ShapeDtypeStruct((8, D), F32)],
        scratch_shapes=[pltpu.VMEM((hg, HD, HD), F32)],
        name="hgrn_bwd", compiler_params=_cp(48),
    )(proj, lbw, do_raw, states)


def _kv_variants(tile, odd):
    low = lax.broadcasted_iota(jnp.int32, tile.shape, 1) < 64
    if odd:
        hi = jnp.where(low, 0.0, tile)
        lo = pltpu.roll(hi, 64, 1)
    else:
        lo = jnp.where(low, tile, 0.0)
        hi = pltpu.roll(lo, 64, 1)
    return _bf(lo), _bf(hi)


def _attn_masks(n):
    qi = lax.broadcasted_iota(jnp.int32, (AB, AB), 0)
    kj = lax.broadcasted_iota(jnp.int32, (AB, AB), 1)
    cur = kj <= qi
    return cur, cur | (n > 0), qi <= kj


def _kv_all(prev_ref, cur_ref):
    out = []
    for tl in range(2):
        cols = slice(tl * 128, (tl + 1) * 128)
        tile = jnp.concatenate([prev_ref[:, cols], cur_ref[:, cols]], axis=0)
        out.append(_kv_variants(tile, 0))
        out.append(_kv_variants(tile, 1))
    return out


def _window(a2, cur):
    return jnp.where(cur, a2[:, AB:], a2[:, :AB])


def _attn_softmax(scores, sinks, cur, ok):
    s = [jnp.where(ok, _window(s2, cur) * ATT_SCALE, NEG) for s2 in scores]
    m = [jnp.maximum(jnp.max(si, axis=-1, keepdims=True), sink) for si, sink in zip(s, sinks)]
    p = [jnp.exp(si - mi) for si, mi in zip(s, m)]
    es = [jnp.exp(sink - mi) for sink, mi in zip(sinks, m)]
    inv = [1.0 / (jnp.sum(pi, axis=-1, keepdims=True) + ei) for pi, ei in zip(p, es)]
    return [pi * ii for pi, ii in zip(p, inv)], [ei * ii for ei, ii in zip(es, inv)]


def _spread(pc, cur):
    return jnp.concatenate([jnp.where(cur, 0.0, pc), jnp.where(cur, pc, 0.0)], axis=1)


def _spread_t(pct, cur_t):
    return jnp.concatenate([jnp.where(cur_t, 0.0, pct), jnp.where(cur_t, pct, 0.0)], axis=0)


def _attn_fwd(proj, sinks):
    t = proj.shape[0]
    nb = t // AB

    def body(q_ref, kc_ref, kp_ref, vc_ref, vp_ref, sink_ref, o_ref):
        cur, ok, _ = _attn_masks(pl.program_id(0))
        kvars = _kv_all(kp_ref, kc_ref)
        vvars = _kv_all(vp_ref, vc_ref)
        qps = [_bf(q_ref[:, 128 * j:128 * (j + 1)]) for j in range(8)]
        heads = [(j, ab) for j in range(8) for ab in range(2)]
        scores = [_mm_nt(qps[j], kvars[j // 2][ab]) for j, ab in heads]
        pcs, _ = _attn_softmax(scores, [sink_ref[0, h] for h in range(QH)], cur, ok)
        parts = [_mm(_bf(_spread(pcs[h], cur)), vvars[j // 2][ab]) for h, (j, ab) in enumerate(heads)]
        for j in range(8):
            o_ref[:, 128 * j:128 * (j + 1)] = parts[2 * j] + parts[2 * j + 1]

    prev = lambda n: jnp.maximum(n - 1, 0)
    return pl.pallas_call(
        body, grid=(nb,),
        in_specs=[pl.BlockSpec((AB, D), lambda n: (n, C_AQ // D)),
                  pl.BlockSpec((AB, 256), lambda n: (n, C_AK // 256)),
                  pl.BlockSpec((AB, 256), lambda n: (prev(n), C_AK // 256)),
                  pl.BlockSpec((AB, 256), lambda n: (n, C_AV // 256)),
                  pl.BlockSpec((AB, 256), lambda n: (prev(n), C_AV // 256)),
                  pl.BlockSpec(memory_space=pltpu.SMEM)],
        out_specs=pl.BlockSpec((AB, D), lambda n: (n, 0)),
        out_shape=jax.ShapeDtypeStruct((t, D), F32),
        name="attn_fwd", compiler_params=_cp(32),
    )(proj, proj, proj, proj, proj, sinks)


def _attn_bwd(proj, sinks, do_a):
    t = proj.shape[0]
    nb = t // AB

    def body(q_ref, kc_ref, kp_ref, vc_ref, vp_ref, do_ref, sink_ref, dq_ref, dkv_ref, dsink_ref, carry):
        n = pl.program_id(0)

        @pl.when(n == 0)
        def _():
            dsink_ref[...] = jnp.zeros_like(dsink_ref)
            carry[...] = jnp.zeros_like(carry)

        @pl.when(n < nb)
        def _():
            cur, ok, cur_t = _attn_masks(n)
            low = lax.broadcasted_iota(jnp.int32, (2 * AB, 128), 1) < 64
            lane = lax.broadcasted_iota(jnp.int32, (8, 128), 1)
            row0 = lax.broadcasted_iota(jnp.int32, (8, 128), 0) == 0
            kvars = _kv_all(kp_ref, kc_ref)
            vvars = _kv_all(vp_ref, vc_ref)
            qps = [_bf(q_ref[:, 128 * j:128 * (j + 1)]) for j in range(8)]
            dops = [_bf(do_ref[:, 128 * j:128 * (j + 1)]) for j in range(8)]
            heads = [(j, ab) for j in range(8) for ab in range(2)]
            scores = [_mm_nt(qps[j], kvars[j // 2][ab]) for j, ab in heads]
            dps = [_mm_nt(dops[j], vvars[j // 2][ab]) for j, ab in heads]
            pcs, pss = _attn_softmax(scores, [sink_ref[0, h] for h in range(QH)], cur, ok)
            dpcs = [_window(dp2, cur) for dp2 in dps]
            rss = [jnp.sum(pc * dpc, axis=-1, keepdims=True) for pc, dpc in zip(pcs, dpcs)]
            dscs = [pc * (dpc - rs) for pc, dpc, rs in zip(pcs, dpcs, rss)]
            dsink = jnp.zeros((8, 128), F32)
            for h in range(QH):
                dsink = dsink + jnp.where(row0 & (lane == h), -jnp.sum(pss[h] * rss[h]), 0.0)
            dq_terms = [_mm(_bf(_spread(dscs[h], cur)), kvars[j // 2][ab]) for h, (j, ab) in enumerate(heads)]
            for j in range(8):
                dq_ref[:, 128 * j:128 * (j + 1)] = _bf((dq_terms[2 * j] + dq_terms[2 * j + 1]) * ATT_SCALE)
            dsc_t = [_bf(_spread_t(dsc.T, cur_t)) for dsc in dscs]
            pc_t = [_bf(_spread_t(pc.T, cur_t)) for pc in pcs]
            dk_terms = [_mm(dsc_t[h], qps[j]) for h, (j, ab) in enumerate(heads)]
            dv_terms = [_mm(pc_t[h], dops[j]) for h, (j, ab) in enumerate(heads)]
            dk_ab = [[dk_terms[4 * g + ab] + dk_terms[4 * g + 2 + ab] for ab in range(2)] for g in range(4)]
            dv_ab = [[dv_terms[4 * g + ab] + dv_terms[4 * g + 2 + ab] for ab in range(2)] for g in range(4)]
            for tl in range(2):
                ke, ko = dk_ab[2 * tl], dk_ab[2 * tl + 1]
                ve, vo = dv_ab[2 * tl], dv_ab[2 * tl + 1]
                dkt = (jnp.where(low, ke[0], 0.0) + pltpu.roll(jnp.where(low, 0.0, ke[1]), 64, 1)
                       + jnp.where(low, 0.0, ko[1]) + pltpu.roll(jnp.where(low, ko[0], 0.0), 64, 1)) * ATT_SCALE
                dvt = (jnp.where(low, ve[0], 0.0) + pltpu.roll(jnp.where(low, 0.0, ve[1]), 64, 1)
                       + jnp.where(low, 0.0, vo[1]) + pltpu.roll(jnp.where(low, vo[0], 0.0), 64, 1))
                kcols = slice(tl * 128, (tl + 1) * 128)
                vcols = slice(256 + tl * 128, 256 + (tl + 1) * 128)
                dkv_ref[:, kcols] = _bf(carry[:, kcols] + dkt[0:AB])
                dkv_ref[:, vcols] = _bf(carry[:, vcols] + dvt[0:AB])
                carry[:, kcols] = dkt[AB:2 * AB]
                carry[:, vcols] = dvt[AB:2 * AB]
            dsink_ref[...] += dsink

        @pl.when(n == nb)
        def _():
            dkv_ref[...] = _bf(carry[...])

    cur = lambda n: jnp.minimum(n, nb - 1)
    prev = lambda n: jnp.clip(n - 1, 0, nb - 1)
    return pl.pallas_call(
        body, grid=(nb + 1,),
        in_specs=[pl.BlockSpec((AB, D), lambda n: (cur(n), C_AQ // D)),
                  pl.BlockSpec((AB, 256), lambda n: (cur(n), C_AK // 256)),
                  pl.BlockSpec((AB, 256), lambda n: (prev(n), C_AK // 256)),
                  pl.BlockSpec((AB, 256), lambda n: (cur(n), C_AV // 256)),
                  pl.BlockSpec((AB, 256), lambda n: (prev(n), C_AV // 256)),
                  pl.BlockSpec((AB, D), lambda n: (cur(n), 0)),
                  pl.BlockSpec(memory_space=pltpu.SMEM)],
        out_specs=[pl.BlockSpec((AB, D), lambda n: (cur(n), 0)),
                   pl.BlockSpec((AB, 512), lambda n: (prev(n), 0)),
                   pl.BlockSpec((8, 128), lambda n: (0, 0))],
        out_shape=[jax.ShapeDtypeStruct((t, D), BF16), jax.ShapeDtypeStruct((t, 512), BF16),
                   jax.ShapeDtypeStruct((8, 128), F32)],
        scratch_shapes=[pltpu.VMEM((AB, 512), F32)],
        name="attn_bwd", compiler_params=_cp(32),
    )(proj, proj, proj, proj, proj, do_a, sinks)


def _silu_and_grad(v):
    s = _sigmoid(v)
    return v * s, s * (1.0 + v * (1.0 - s))


def _tail(o_raw, o_a, proj, x2d, tgt, wbh, wba, wout, hnw, fnw, tb):
    t = x2d.shape[0]

    def body(or_ref, oa_ref, hg_ref, ag0, ag1, mh0, mh1, ma0, ma1, x_ref, t_ref, wbh_ref, wba_ref, wout_ref, hnw_ref,
             fnw_ref, dx2_ref, dor_ref, doa_ref, dhg_ref, dagm_ref, gh_ref, ga_ref, mg_ref, dyh_ref, dya_ref, dx2b_ref,
             sums_ref):
        @pl.when(pl.program_id(0) == 0)
        def _():
            sums_ref[...] = jnp.zeros_like(sums_ref)

        halves = lambda a, b: jnp.concatenate([a[...], b[...]], axis=1)
        hnw_v = hnw_ref[...]
        fnw_v = fnw_ref[...]
        o = or_ref[...]
        rs, xhs = [], []
        for h in range(HEADS):
            oh = o[:, h * HD:(h + 1) * HD]
            r = lax.rsqrt(jnp.mean(oh * oh, axis=-1, keepdims=True) + EPS)
            rs.append(r)
            xhs.append(oh * r)
        xh = jnp.concatenate(xhs, axis=1)
        on = xh * hnw_v
        sil_hg, dsil_hg = _silu_and_grad(hg_ref[...])
        gh_b = _bf(on * sil_hg)
        y_h = _mm(gh_b, wbh_ref[...])
        oa = oa_ref[...]
        sil_ag, dsil_ag = _silu_and_grad(halves(ag0, ag1))
        ga_b = _bf(oa * sil_ag)
        y_a = _mm(ga_b, wba_ref[...])
        s_mh = _sigmoid(halves(mh0, mh1))
        s_ma = _sigmoid(halves(ma0, ma1))
        mg_b = _bf(s_mh * y_h + s_ma * y_a)
        x2 = x_ref[...] + _mm(mg_b, wout_ref[...])
        r2 = lax.rsqrt(jnp.mean(x2 * x2, axis=-1, keepdims=True) + EPS)
        xh2 = x2 * r2
        err = xh2 * fnw_v - t_ref[...]
        loss = 0.5 * jnp.sum(jnp.mean(err * err, axis=-1, keepdims=True))
        dy = err * (1.0 / D)
        dfnw = jnp.sum(dy * xh2, axis=0, keepdims=True)
        dxh2 = dy * fnw_v
        dx2 = r2 * (dxh2 - xh2 * jnp.mean(dxh2 * xh2, axis=-1, keepdims=True))
        dx2_ref[...] = dx2
        dx2_b = _bf(dx2)
        dmg = _mm_nt(dx2_b, wout_ref[...])
        dmg_h = dmg * s_mh
        dmg_a = dmg * s_ma
        dyh_b = _bf(dmg_h)
        dya_b = _bf(dmg_a)
        dagm_ref[:, D:2 * D] = _bf(dmg_h * y_h * (1.0 - s_mh))
        dagm_ref[:, 2 * D:3 * D] = _bf(dmg_a * y_a * (1.0 - s_ma))
        dgh = _mm_nt(dyh_b, wbh_ref[...])
        dga = _mm_nt(dya_b, wba_ref[...])
        doa_ref[...] = dga * sil_ag
        dagm_ref[:, 0:D] = _bf(dga * oa * dsil_ag)
        dhg_ref[...] = _bf(dgh * on * dsil_hg)
        don = dgh * sil_hg
        dhnw = jnp.sum(don * xh, axis=0, keepdims=True)
        dxh = don * hnw_v
        dos = []
        for h in range(HEADS):
            sl = slice(h * HD, (h + 1) * HD)
            dos.append(rs[h] * (dxh[:, sl] - xhs[h] * jnp.mean(dxh[:, sl] * xhs[h], axis=-1, keepdims=True)))
        dor_ref[...] = jnp.concatenate(dos, axis=1)
        gh_ref[...] = gh_b
        ga_ref[...] = ga_b
        mg_ref[...] = mg_b
        dyh_ref[...] = dyh_b
        dya_ref[...] = dya_b
        dx2b_ref[...] = dx2_b
        row = lax.broadcasted_iota(jnp.int32, (8, D), 0)
        sums_ref[...] += jnp.where(row == 0, dfnw, 0.0) + jnp.where(row == 1, dhnw, 0.0) + jnp.where(row == 2, loss, 0.0)

    rowblk = lambda c: pl.BlockSpec((tb, D), lambda i: (i, c))
    half = lambda c: pl.BlockSpec((tb, 512), lambda i: (i, c))
    full = lambda shape: pl.BlockSpec(shape, lambda i: (0, 0))
    return pl.pallas_call(
        body, grid=(t // tb,),
        in_specs=[rowblk(0), rowblk(0), rowblk(C_HG // D), half(C_AG // 512), half(C_AG // 512 + 1), half(C_MH // 512),
                  half(C_MH // 512 + 1), half(C_MA // 512), half(C_MA // 512 + 1), rowblk(0), rowblk(0),
                  full((D, D)), full((D, D)), full((D, D)), full((1, D)), full((1, D))],
        out_specs=[rowblk(0), rowblk(0), rowblk(0), rowblk(0), pl.BlockSpec((tb, 3 * D), lambda i: (i, 0))]
        + [rowblk(0)] * 6 + [full((8, D))],
        out_shape=[jax.ShapeDtypeStruct((t, D), F32)] * 3
        + [jax.ShapeDtypeStruct((t, D), BF16), jax.ShapeDtypeStruct((t, 3 * D), BF16)]
        + [jax.ShapeDtypeStruct((t, D), BF16)] * 6 + [jax.ShapeDtypeStruct((8, D), F32)],
        name="tail", compiler_params=_cp(56),
    )(o_raw, o_a, proj, proj, proj, proj, proj, proj, proj, x2d, tgt, wbh, wba, wout, hnw, fnw)


def _wgrad3(gh, dyh, ga, dya, mg, dx2b, tk):
    t = dyh.shape[0]

    def body(a0, b0, a1, b1, a2, b2, o0, o1, o2):
        @pl.when(pl.program_id(0) == 0)
        def _():
            o0[...] = jnp.zeros_like(o0)
            o1[...] = jnp.zeros_like(o1)
            o2[...] = jnp.zeros_like(o2)

        o0[...] += _mm_tn(a0[...], b0[...])
        o1[...] += _mm_tn(a1[...], b1[...])
        o2[...] += _mm_tn(a2[...], b2[...])

    blk = pl.BlockSpec((tk, D), lambda k: (k, 0))
    out = pl.BlockSpec((D, D), lambda k: (0, 0))
    return pl.pallas_call(
        body, grid=(t // tk,), in_specs=[blk] * 6, out_specs=[out] * 3,
        out_shape=[jax.ShapeDtypeStruct((D, D), F32)] * 3,
        name="wgrad3", compiler_params=_cp(48),
    )(gh, dyh, ga, dya, mg, dx2b)


def _inproj_wgrad_piece(xnt, piece, nb, name):
    t = xnt.shape[1]
    width = piece.shape[1]

    def body(xnt_ref, p_ref, o_ref):
        o_ref[...] = _mm(xnt_ref[...], p_ref[...])

    return pl.pallas_call(
        body, grid=(width // nb,),
        in_specs=[pl.BlockSpec((D, t), lambda j: (0, 0), pipeline_mode=pl.Buffered(1)),
                  pl.BlockSpec((t, nb), lambda j: (0, j))],
        out_specs=pl.BlockSpec((D, nb), lambda j: (0, j)),
        out_shape=jax.ShapeDtypeStruct((D, width), F32),
        name=name, compiler_params=_cp(56),
    )(xnt, piece)


def _inproj_dgrad(pieces, w_p, x2d, dx2, norm_w, tb, after):
    t = x2d.shape[0]

    def body(*refs):
        piece_refs = refs[:len(pieces)]
        w_ref, x_ref, dx2_ref, nw_ref, _, gx_ref, dnw_ref = refs[len(pieces):]

        @pl.when(pl.program_id(0) == 0)
        def _():
            dnw_ref[...] = jnp.zeros_like(dnw_ref)

        dxn = None
        off = 0
        for p in piece_refs:
            width = p.shape[1]
            for q in range(w_ref.shape[0]):
                lo, hi = max(off, q * PAIR), min(off + width, (q + 1) * PAIR)
                if lo < hi:
                    term = _mm_nt(p[:, lo - off:hi - off], w_ref[q, :, lo - q * PAIR:hi - q * PAIR])
                    dxn = term if dxn is None else dxn + term
            off += width
        xv = x_ref[...]
        r = lax.rsqrt(jnp.mean(xv * xv, axis=-1, keepdims=True) + EPS)
        xh = xv * r
        dxh = dxn * nw_ref[...]
        gx_ref[...] = dx2_ref[...] + r * (dxh - xh * jnp.mean(dxh * xh, axis=-1, keepdims=True))
        row0 = lax.broadcasted_iota(jnp.int32, (8, D), 0) == 0
        dnw_ref[...] += jnp.where(row0, jnp.sum(dxn * xh, axis=0, keepdims=True), 0.0)

    rowblk = pl.BlockSpec((tb, D), lambda i: (i, 0))
    return pl.pallas_call(
        body, grid=(t // tb,),
        in_specs=[pl.BlockSpec((tb, p.shape[1]), lambda i: (i, 0)) for p in pieces]
        + [pl.BlockSpec(w_p.shape, lambda i: (0, 0, 0), pipeline_mode=pl.Buffered(1)), rowblk, rowblk,
           pl.BlockSpec((1, D), lambda i: (0, 0)), pl.BlockSpec(memory_space=pl.ANY)],
        out_specs=[rowblk, pl.BlockSpec((8, D), lambda i: (0, 0))],
        out_shape=[jax.ShapeDtypeStruct((t, D), F32), jax.ShapeDtypeStruct((8, D), F32)],
        name="inproj_dgrad", compiler_params=_cp(60),
    )(*pieces, w_p, x2d, dx2, norm_w, after)


def _adamw_math(w, g, m, v):
    m = B1 * m + (1.0 - B1) * g
    v = B2 * v + (1.0 - B2) * (g * g)
    m_hat = m / (1.0 - B1 ** STEP)
    v_hat = v / (1.0 - B2 ** STEP)
    delta = -LR * (m_hat / (jnp.sqrt(v_hat) + ADAM_EPS) + WD * w)
    return delta, m, v


def _adamw_shard(recv, sums, chip, w, m, v, rows, name):
    nparts, nr, nc = recv.shape

    def body(chip_ref, own_ref, p_ref, w_ref, m_ref, v_ref, g_ref, d_ref, nm_ref, nv_ref):
        g = own_ref[0].astype(F32)
        for s in range(nparts):
            g = g + p_ref[s].astype(F32)
        d, nm, nv = _adamw_math(w_ref[...], g, m_ref[...], v_ref[...])
        g_ref[...] = g
        d_ref[...] = d
        nm_ref[...] = nm
        nv_ref[...] = nv

    blk = pl.BlockSpec((rows, nc), lambda i, chip_ref: (i, 0))
    return pl.pallas_call(
        body,
        grid_spec=pltpu.PrefetchScalarGridSpec(
            num_scalar_prefetch=1, grid=(nr // rows,),
            in_specs=[pl.BlockSpec((1, rows, nc), lambda i, chip_ref: (chip_ref[0], i, 0)),
                      pl.BlockSpec((nparts, rows, nc), lambda i, chip_ref: (0, i, 0)), blk, blk, blk],
            out_specs=[blk] * 4),
        out_shape=[jax.ShapeDtypeStruct((nr, nc), F32)] * 4,
        name=name, compiler_params=_cp(48),
    )(chip, sums, recv, w, m, v)


SMALL_ROWS = dict(norm_w=0, lower_bound=1, hgrn_norm_w=3, final_norm_w=4, sinks=5, loss=6)


def _pack_small_grads(dnw, dlb, sums, dsink):
    def body(dnw_ref, dlb_ref, sums_ref, dsink_ref, o_ref):
        o_ref[...] = jnp.zeros_like(o_ref)
        o_ref[0:1, :] = dnw_ref[0:1, :]
        o_ref[1:2, :] = dlb_ref[0:1, :]
        o_ref[3:4, :] = sums_ref[1:2, :]
        o_ref[4:5, :] = sums_ref[0:1, :]
        o_ref[5:6, 0:128] = dsink_ref[0:1, :]
        o_ref[6:7, :] = sums_ref[2:3, :]

    return pl.pallas_call(body, out_shape=jax.ShapeDtypeStruct((8, D), F32), name="pack_small_grads")(dnw, dlb, sums, dsink)


def _adamw_small(parts, ws, ms, vs):
    shapes = [a.shape for a in ws]

    def body(p_ref, *refs):
        w, m, v = refs[0:5], refs[5:10], refs[10:15]
        outs = [refs[15 + 5 * i:20 + 5 * i] for i in range(4)]
        loss_ref = refs[35]

        def total(row, width):
            g = p_ref[0, row:row + 1, 0:width]
            for s in range(1, NDEV):
                g = g + p_ref[s, row:row + 1, 0:width]
            return g

        loss_ref[...] = total(6, 128)
        lb = _lower_bound(w[1])
        ga0 = total(1, D) * lb * (1.0 - lb)
        grads = [total(0, D), None, total(3, D), total(4, D), total(5, QH)]
        for i in (0, 2, 3, 4):
            res = (grads[i],) + _adamw_math(w[i][...], grads[i], m[i][...], v[i][...])
            for o, val in zip(outs, res):
                o[i][...] = val
        for r, g in ((0, ga0), (1, -ga0)):
            res = (g,) + _adamw_math(w[1][r:r + 1, :], g, m[1][r:r + 1, :], v[1][r:r + 1, :])
            for o, val in zip(outs, res):
                o[1][r:r + 1, :] = val

    res = pl.pallas_call(
        body, out_shape=[jax.ShapeDtypeStruct(s, F32) for s in shapes] * 4 + [jax.ShapeDtypeStruct((1, 128), F32)],
        name="adamw_small",
    )(parts, *ws, *ms, *vs)
    return [res[5 * i:5 * i + 5] for i in range(4)], res[20][0, 0]


def kernel(x, norm_w, w_in, hgrn_lower_bound, hgrn_norm_w, w_branch_hgrn, attn_sinks, w_branch_attn, w_out, final_norm_w, loss_target, m_norm_w, m_w_in, m_hgrn_lower_bound, m_hgrn_norm_w, m_w_branch_hgrn, m_attn_sinks, m_w_branch_attn, m_w_out, m_final_norm_w, v_norm_w, v_w_in, v_hgrn_lower_bound, v_hgrn_norm_w, v_w_branch_hgrn, v_attn_sinks, v_w_branch_attn, v_w_out, v_final_norm_w):
    t = x.shape[1]
    x2d = x.reshape(t, D)
    tgt = loss_target.reshape(t, D)
    fnw = final_norm_w.reshape(1, D)
    row_blk = min(256, t)
    big_blk = min(512, t)

    w_p, xn, xnt = _gather_in_projection(w_in[0], x2d, norm_w)
    wbh, wba, wout = (g.reshape(D, D) for g in _gather_square(
        [w_branch_hgrn[0].astype(BF16), w_branch_attn[0].astype(BF16), w_out[0].astype(BF16)], after=w_p))

    proj = _inproj_fwd(xn, w_p, min(1024, t))
    o_raw, states = _hgrn_fwd(proj, hgrn_lower_bound, big_blk, HGRN_GROUP)
    o_a = _attn_fwd(proj, attn_sinks)
    (dx2, do_raw, do_a, d_hg, d_agm, gh, ga, mg, dyh, dya, dx2b, sums) = _tail(
        o_raw, o_a, proj, x2d, tgt, wbh, wba, wout, hgrn_norm_w, fnw, row_blk)
    dwbh, dwba, dwout = _wgrad3(gh, dyh, ga, dya, mg, dx2b, big_blk)
    d_aq, d_kv, dsink = _attn_bwd(proj, attn_sinks, do_a)
    d_hgrn, dlb = _hgrn_bwd(proj, hgrn_lower_bound, do_raw, states, big_blk, HGRN_GROUP)
    pieces = (d_hgrn, d_hg, d_aq, d_kv, d_agm)
    dw_pieces = [_inproj_wgrad_piece(xnt, p, CB, "inproj_wgrad_" + n)
                 for p, n in zip(pieces, ("hgrn", "hgate", "aq", "kv", "gates"))]

    dwin_r = jnp.concatenate(dw_pieces, axis=1).reshape(D, NDEV, IN_SHARD).transpose(1, 0, 2).astype(BF16)
    slots = lambda a: a.reshape(NDEV, ROW_SHARD, D).astype(BF16)
    partials = [dwin_r, slots(dwbh), slots(dwba), slots(dwout)]
    got = _exchange_pair(partials)
    core = lax.axis_index("c").astype(jnp.int32).reshape(1)
    pair_sums = [_pair_sum(a, g, core, ROW_SHARD, "pair_sum_" + n)
                 for a, g, n in zip(partials, got, ("w_in", "w_bh", "w_ba", "w_out"))]
    rin, rbh, rba, rout = _exchange_chips(pair_sums)
    grad_x, dnw = _inproj_dgrad(pieces, w_p, x2d, dx2, norm_w, big_blk, after=pair_sums[0])
    rsm = _exchange_small(_pack_small_grads(dnw, dlb, sums, dsink))
    chip = (2 * lax.axis_index("x") + lax.axis_index("y")).astype(jnp.int32).reshape(1)
    s_in, s_bh, s_ba, s_out = pair_sums
    g_in, d_in, nm_in, nv_in = _adamw_shard(rin, s_in, chip, w_in[0], m_w_in[0], v_w_in[0], 128, "adamw_w_in")
    g_bh, d_bh, nm_bh, nv_bh = _adamw_shard(
        rbh, s_bh, chip, w_branch_hgrn[0], m_w_branch_hgrn[0], v_w_branch_hgrn[0], 128, "adamw_w_bh")
    g_ba, d_ba, nm_ba, nv_ba = _adamw_shard(
        rba, s_ba, chip, w_branch_attn[0], m_w_branch_attn[0], v_w_branch_attn[0], 128, "adamw_w_ba")
    g_out, d_out, nm_out, nv_out = _adamw_shard(rout, s_out, chip, w_out[0], m_w_out[0], v_w_out[0], 128, "adamw_w_out")
    (sg, sd, sm, sv), loss = _adamw_small(
        rsm,
        (norm_w, hgrn_lower_bound, hgrn_norm_w, fnw, attn_sinks),
        (m_norm_w, m_hgrn_lower_bound, m_hgrn_norm_w, m_final_norm_w.reshape(1, D), m_attn_sinks),
        (v_norm_w, v_hgrn_lower_bound, v_hgrn_norm_w, v_final_norm_w.reshape(1, D), v_attn_sinks))

    def group(s, w_in_v, bh, ba, out):
        nw, lb, hnw, fn, sinks = s
        return (nw, w_in_v[None], lb, hnw, bh[None], sinks, ba[None], out[None], fn.reshape(D))

    return (loss, grad_x.reshape(1, t, D),
            *group(sg, g_in, g_bh, g_ba, g_out), *group(sd, d_in, d_bh, d_ba, d_out),
            *group(sm, nm_in, nm_bh, nm_ba, nm_out), *group(sv, nv_in, nv_bh, nv_ba, nv_out))
```

```python
import functools

import jax
import jax.numpy as jnp
from jax import lax
from jax.experimental import pallas as pl
from jax.experimental.pallas import tpu as pltpu
from jax.experimental.pallas import tpu_sc as plsc

F32 = jnp.float32
BF16 = jnp.bfloat16

D = 1024
DIN = 8704
NDEV = 8
IN_SHARD = DIN // NDEV
PAIR = 2 * IN_SHARD
ROW_SHARD = D // NDEV
HEADS = 8
HD = 128
CH = 64
HGRN_GROUP = 8
QH = 16
AB = 128
EPS = 1e-6
NEG = -1e30
ATT_SCALE = 0.125

C_HGRN = 0
C_HG = 3072
C_AQ = 4096
C_AK = 5120
C_AV = 5376
C_AG = 5632
C_MH = 6656
C_MA = 7680
CB = 512

LR = 0.001
B1 = 0.9
B2 = 0.999
ADAM_EPS = 1e-08
WD = 0.01
STEP = 10

V7X_VMEM_BYTES = 64 * 1024 * 1024
MESH = pl.DeviceIdType.MESH


def _cp(vmem_mb):
    return pltpu.CompilerParams(vmem_limit_bytes=vmem_mb * 1024 * 1024)


def _mm(a, b):
    return jnp.dot(a, b, preferred_element_type=F32)


def _mm_nt(a, b):
    return lax.dot_general(a, b, (((1,), (1,)), ((), ())), preferred_element_type=F32)


def _mm_tn(a, b):
    return lax.dot_general(a, b, (((0,), (0,)), ((), ())), preferred_element_type=F32)


def _tri3(lower):
    r = lax.broadcasted_iota(jnp.int32, (CH, 3 * CH), 0)
    c = lax.broadcasted_iota(jnp.int32, (CH, 3 * CH), 1)
    c = jnp.where(c >= 2 * CH, c - 2 * CH, jnp.where(c >= CH, c - CH, c))
    return ((r >= c) if lower else (c >= r)).astype(BF16)


def _mm_tri_exact(tri3, g):
    g1 = g.astype(BF16)
    r1 = g - g1.astype(F32)
    g2 = r1.astype(BF16)
    g3 = (r1 - g2.astype(F32)).astype(BF16)
    return _mm(tri3, jnp.concatenate([g1, g2, g3], axis=0))


def _sigmoid(v):
    return 0.5 * jnp.tanh(0.5 * v) + 0.5


def _bf(v):
    return v.astype(BF16)


def _place():
    x, y, c = lax.axis_index("x"), lax.axis_index("y"), lax.axis_index("c")
    return (x, y, c), (x, y, 1 - c), [(1 - x, y), (x, 1 - y), (1 - x, 1 - y)]


def _dev_index(px, py, pc):
    return 4 * px + 2 * py + pc


def _gather_in_projection(w_in_s, x2d, norm_w):
    half = D // 2
    t = x2d.shape[0]
    prep_rows = min(512, t)
    nprep = t // prep_rows

    def body(win_ref, x_hbm, nw_ref, wp_g, xn_hbm, xnt_hbm, proj_hbm, give, take, mine, xbuf, xnbuf, xntbuf, w_own, pbuf,
             send_sems, recv_sems, loc_sem, swap_sems, in_sems, out_sems, own_sem):
        (x, y, c), sibling, chips = _place()
        give[...] = win_ref[pl.ds(pl.multiple_of(half * (1 - c), half), half), :].astype(BF16)
        swap = pltpu.make_async_remote_copy(src_ref=give, dst_ref=take, send_sem=swap_sems.at[0], recv_sem=swap_sems.at[1],
                                            device_id=sibling, device_id_type=MESH)
        swap.start()
        swap.wait()
        own = win_ref[pl.ds(pl.multiple_of(half * c, half), half), :]
        other = take[...].astype(F32)
        mine[...] = jnp.where(c == 0, jnp.concatenate([own, other], axis=1),
                              jnp.concatenate([other, own], axis=1)).astype(BF16)

        def place(px, py, pc):
            return wp_g.at[2 * px + py, pl.ds(pl.multiple_of(half * pc, half), half), :]

        def copy(kind, origin, to, src=mine):
            return pltpu.make_async_remote_copy(
                src_ref=src, dst_ref=place(*origin), send_sem=send_sems.at[kind], recv_sem=recv_sems.at[kind],
                device_id=to, device_id_type=MESH)

        me = (x, y, c)
        local = pltpu.make_async_copy(mine, place(*me), loc_sem)
        local.start()
        first = [copy(0, me, sibling)] + [copy(1 + j, me, (*chip, c)) for j, chip in enumerate(chips)]
        for cp in first:
            cp.start()

        copy(0, (x, y, 1 - c), me).wait_recv()
        local.wait()
        my_chip = 2 * x + y
        fetch = pltpu.make_async_copy(wp_g.at[my_chip], w_own, own_sem)
        fetch.start()

        def rows_of(i):
            return pl.ds(pl.multiple_of(i * prep_rows, prep_rows), prep_rows)

        def load(i, slot):
            return pltpu.make_async_copy(x_hbm.at[rows_of(i), :], xbuf.at[slot], in_sems.at[slot])

        def stores(i, slot):
            own_cols = pl.ds(pl.multiple_of(my_chip * PAIR, 128), PAIR)
            return (pltpu.make_async_copy(xnbuf.at[slot], xn_hbm.at[rows_of(i), :], out_sems.at[slot, 0]),
                    pltpu.make_async_copy(xntbuf.at[slot], xnt_hbm.at[:, rows_of(i)], out_sems.at[slot, 1]),
                    pltpu.make_async_copy(pbuf.at[slot], proj_hbm.at[rows_of(i), own_cols], out_sems.at[slot, 2]))

        load(0, 0).start()
        fetch.wait()

        def prep(i, carry):
            slot = lax.rem(i, 2)
            load(i, slot).wait()

            @pl.when(i + 1 < nprep)
            def _():
                load(i + 1, 1 - slot).start()

            @pl.when(i >= 2)
            def _():
                for cp in stores(i - 2, slot):
                    cp.wait()

            xv = xbuf[slot]
            xn = (xv * lax.rsqrt(jnp.mean(xv * xv, axis=-1, keepdims=True) + EPS)) * nw_ref[...]
            xn_b = xn.astype(BF16)
            xnbuf[slot] = xn_b
            xntbuf[slot] = xn.T.astype(BF16)
            pbuf[slot] = _mm(xn_b, w_own[...])
            for cp in stores(i, slot):
                cp.start()
            return carry

        lax.fori_loop(0, nprep, prep, 0)
        for i in range(max(nprep - 2, 0), nprep):
            for cp in stores(i, i % 2):
                cp.wait()

        passed = []
        for j, chip in enumerate(chips):
            copy(1 + j, (*chip, c), me).wait_recv()
            cp = copy(4 + j, (*chip, c), sibling, src=place(*chip, c))
            cp.start()
            passed.append(cp)
        for j, chip in enumerate(chips):
            copy(4 + j, (*chip, 1 - c), me).wait_recv()
        for cp in first + passed:
            cp.wait_send()

    vm = pl.BlockSpec(memory_space=pltpu.VMEM)
    hbm = pl.BlockSpec(memory_space=pl.ANY)
    return pl.pallas_call(
        body,
        out_shape=[jax.ShapeDtypeStruct((NDEV // 2, D, PAIR), BF16), jax.ShapeDtypeStruct((t, D), BF16),
                   jax.ShapeDtypeStruct((D, t), BF16), jax.ShapeDtypeStruct((t, DIN), F32)],
        in_specs=[vm, hbm, vm],
        out_specs=[hbm, hbm, hbm, hbm],
        scratch_shapes=[pltpu.VMEM((half, IN_SHARD), BF16), pltpu.VMEM((half, IN_SHARD), BF16),
                        pltpu.VMEM((half, PAIR), BF16),
                        pltpu.VMEM((2, prep_rows, D), F32), pltpu.VMEM((2, prep_rows, D), BF16),
                        pltpu.VMEM((2, D, prep_rows), BF16),
                        pltpu.VMEM((D, PAIR), BF16), pltpu.VMEM((2, prep_rows, PAIR), F32),
                        pltpu.SemaphoreType.DMA((NDEV - 1,)), pltpu.SemaphoreType.DMA((NDEV - 1,)),
                        pltpu.SemaphoreType.DMA, pltpu.SemaphoreType.DMA((2,)),
                        pltpu.SemaphoreType.DMA((2,)), pltpu.SemaphoreType.DMA((2, 3)), pltpu.SemaphoreType.DMA],
        name="gather_in_projection", compiler_params=_cp(56),
    )(w_in_s, x2d, norm_w)


def _gather_square(shards, after):
    n = len(shards)

    def launch(*refs):
        ins, outs = refs[:n], refs[n + 1:2 * n + 1]
        send_sems, recv_sems, loc_sems = refs[2 * n + 1:]
        (x, y, c), _, _ = _place()
        me = _dev_index(x, y, c)
        peers = [(1 - x if r & 4 else x, 1 - y if r & 2 else y, 1 - c if r & 1 else c) for r in range(1, NDEV)]
        barrier = pltpu.get_barrier_semaphore()
        for peer in peers:
            pl.semaphore_signal(barrier, inc=1, device_id=peer, device_id_type=MESH)
        pl.semaphore_wait(barrier, NDEV - 1)
        local = [pltpu.make_async_copy(ins[k], outs[k].at[me], loc_sems.at[k]) for k in range(n)]
        copies = [pltpu.make_async_remote_copy(
            src_ref=ins[k], dst_ref=outs[k].at[me], send_sem=send_sems.at[r, k], recv_sem=recv_sems.at[r, k],
            device_id=peer, device_id_type=MESH) for r, peer in enumerate(peers) for k in range(n)]
        for cp in local + copies:
            cp.start()
        for r, peer in enumerate(peers):
            for k in range(n):
                pltpu.make_async_remote_copy(
                    src_ref=ins[k], dst_ref=outs[k].at[_dev_index(*peer)], send_sem=send_sems.at[r, k],
                    recv_sem=recv_sems.at[r, k], device_id=peer, device_id_type=MESH).wait_recv()
        for cp in copies:
            cp.wait_send()
        for cp in local:
            cp.wait()

    return pl.kernel(
        launch, out_type=[jax.ShapeDtypeStruct((NDEV,) + a.shape, a.dtype) for a in shards],
        mesh=plsc.ScalarSubcoreMesh(axis_name="sequencer", num_cores=1), name="gather_square",
        scratch_types=(pltpu.SemaphoreType.DMA((NDEV - 1, n)), pltpu.SemaphoreType.DMA((NDEV - 1, n)),
                       pltpu.SemaphoreType.DMA((n,))),
        compiler_params=pltpu.CompilerParams(collective_id=2),
    )(*shards, after)


def _exchange_pair(arrs):
    n = len(arrs)

    def launch(*refs):
        ins, got = refs[:n], refs[n:2 * n]
        send_sems, recv_sems = refs[2 * n:]
        (x, y, c), sibling, _ = _place()
        barrier = pltpu.get_barrier_semaphore()
        pl.semaphore_signal(barrier, inc=1, device_id=sibling, device_id_type=MESH)
        pl.semaphore_wait(barrier, 1)
        sends = [pltpu.make_async_remote_copy(
            src_ref=ins[k].at[_dev_index(q // 2, q % 2, 1 - c)], dst_ref=got[k].at[q], send_sem=send_sems.at[q, k],
            recv_sem=recv_sems.at[q, k], device_id=sibling, device_id_type=MESH) for q in range(4) for k in range(n)]
        for cp in sends:
            cp.start()
        for cp in sends:
            cp.wait_recv()
        for cp in sends:
            cp.wait_send()

    return pl.kernel(
        launch, out_type=[jax.ShapeDtypeStruct((4,) + a.shape[1:], a.dtype) for a in arrs],
        mesh=plsc.ScalarSubcoreMesh(axis_name="sequencer", num_cores=1), name="exchange_pair",
        scratch_types=(pltpu.SemaphoreType.DMA((4, n)), pltpu.SemaphoreType.DMA((4, n))),
        compiler_params=pltpu.CompilerParams(collective_id=0),
    )(*arrs)


def _pair_sum(full, got, core, rows, name):
    _, nr, nc = got.shape

    def body(core_ref, a_ref, b_ref, o_ref):
        o_ref[...] = (a_ref[...].astype(F32) + b_ref[...].astype(F32)).astype(BF16)

    blk = pl.BlockSpec((1, rows, nc), lambda q, i, core_ref: (q, i, 0))
    return pl.pallas_call(
        body,
        grid_spec=pltpu.PrefetchScalarGridSpec(
            num_scalar_prefetch=1, grid=(4, nr // rows),
            in_specs=[pl.BlockSpec((1, rows, nc), lambda q, i, core_ref: (2 * q + core_ref[0], i, 0)), blk],
            out_specs=blk),
        out_shape=jax.ShapeDtypeStruct(got.shape, BF16), name=name,
    )(core, full, got)


def _exchange_chips(sums):
    n = len(sums)

    def launch(*refs):
        ins, outs = refs[:n], refs[n:2 * n]
        send_sems, recv_sems = refs[2 * n:]
        (x, y, c), _, chips = _place()
        barrier = pltpu.get_barrier_semaphore()
        for px, py in chips:
            pl.semaphore_signal(barrier, inc=1, device_id=(px, py, c), device_id_type=MESH)
        pl.semaphore_wait(barrier, len(chips))
        copies = [pltpu.make_async_remote_copy(
            src_ref=ins[k].at[2 * px + py], dst_ref=outs[k].at[j], send_sem=send_sems.at[j, k],
            recv_sem=recv_sems.at[j, k], device_id=(px, py, c), device_id_type=MESH)
            for j, (px, py) in enumerate(chips) for k in range(n)]
        for cp in copies:
            cp.start()
        for cp in copies:
            cp.wait_recv()
        for cp in copies:
            cp.wait_send()

    return pl.kernel(
        launch, out_type=[jax.ShapeDtypeStruct((3,) + a.shape[1:], a.dtype) for a in sums],
        mesh=plsc.ScalarSubcoreMesh(axis_name="sequencer", num_cores=1), name="exchange_chips",
        scratch_types=(pltpu.SemaphoreType.DMA((3, n)), pltpu.SemaphoreType.DMA((3, n))),
        compiler_params=pltpu.CompilerParams(collective_id=1),
    )(*sums)


def _exchange_small(small):
    def body(sm_ref, out_ref, send_sems, recv_sems):
        (x, y, c), _, _ = _place()
        me = _dev_index(x, y, c)
        peers = [(1 - x if r & 4 else x, 1 - y if r & 2 else y, 1 - c if r & 1 else c) for r in range(1, NDEV)]
        out_ref[me] = sm_ref[...]
        copies = [pltpu.make_async_remote_copy(
            src_ref=sm_ref, dst_ref=out_ref.at[me], send_sem=send_sems.at[r], recv_sem=recv_sems.at[r],
            device_id=peer, device_id_type=MESH) for r, peer in enumerate(peers)]
        for cp in copies:
            cp.start()
        for r, peer in enumerate(peers):
            pltpu.make_async_remote_copy(
                src_ref=sm_ref, dst_ref=out_ref.at[_dev_index(*peer)], send_sem=send_sems.at[r], recv_sem=recv_sems.at[r],
                device_id=peer, device_id_type=MESH).wait_recv()
        for cp in copies:
            cp.wait_send()

    vm = pl.BlockSpec(memory_space=pltpu.VMEM)
    return pl.pallas_call(
        body, out_shape=jax.ShapeDtypeStruct((NDEV,) + small.shape, F32), in_specs=[vm], out_specs=vm,
        scratch_shapes=[pltpu.SemaphoreType.DMA((NDEV - 1,)), pltpu.SemaphoreType.DMA((NDEV - 1,))],
        name="exchange_small",
    )(small)


def _inproj_fwd(xn, w_pairs, proj, chip, tb):
    t = xn.shape[0]
    nblk, _, nb = w_pairs.shape

    def body(chip_ref, xn_ref, w_ref, proj_in, proj_ref):
        proj_ref[...] = _mm(xn_ref[...], w_ref[0])

    def other(j, chip_ref):
        return j + (j >= chip_ref[0]).astype(jnp.int32)

    return pl.pallas_call(
        body,
        grid_spec=pltpu.PrefetchScalarGridSpec(
            num_scalar_prefetch=1, grid=(t // tb, nblk - 1),
            in_specs=[pl.BlockSpec((tb, D), lambda i, j, chip_ref: (i, 0)),
                      pl.BlockSpec((1, D, nb), lambda i, j, chip_ref: (other(j, chip_ref), 0, 0)),
                      pl.BlockSpec(memory_space=pl.ANY)],
            out_specs=pl.BlockSpec((tb, nb), lambda i, j, chip_ref: (i, other(j, chip_ref)))),
        out_shape=jax.ShapeDtypeStruct((t, DIN), F32),
        input_output_aliases={3: 0},
        name="inproj_fwd", compiler_params=_cp(56),
    )(chip, xn, w_pairs, proj)


def _lower_bound(lb_ref):
    a0 = lb_ref[0:1, :]
    a1 = lb_ref[1:2, :]
    mx = jnp.maximum(a0, a1)
    e0 = jnp.exp(a0 - mx)
    e1 = jnp.exp(a1 - mx)
    return e0 / (e0 + e1)


def _hgrn_chunk_fwd(hq, hf, lb, tril):
    sg = _sigmoid(hf)
    f = lb + (1.0 - lb) * sg
    g = jnp.log(f)
    k = 1.0 - f
    sq = _sigmoid(hq)
    q = hq * sq
    b = _mm_tri_exact(tril, g)
    last_row = lax.broadcasted_iota(jnp.int32, b.shape, 0) == CH - 1
    b_last = jnp.sum(jnp.where(last_row, b, 0.0), axis=0, keepdims=True)
    c = 0.5 * b_last
    eb = jnp.exp(b)
    ea = jnp.exp(b - c)
    ek = jnp.exp(c - b)
    ed = jnp.exp(b_last - b)
    ebl = jnp.exp(b_last)
    return dict(sg=sg, f=f, k=k, sq=sq, q=q, eb=eb, ea=ea, ek=ek, ed=ed, ebl=ebl,
                qe=q * eb, qa=q * ea, ka=k * ek, kd=k * ed)


def _tri(lower):
    r = lax.broadcasted_iota(jnp.int32, (CH, CH), 0)
    c = lax.broadcasted_iota(jnp.int32, (CH, CH), 1)
    return (r >= c) if lower else (c >= r)


def _head_segment(p_ref, rows, j, hg):
    return p_ref[rows, j * HD * hg:(j + 1) * HD * hg]


def _head(a, k):
    return a[:, k * HD:(k + 1) * HD]


def _hgrn_fwd(proj, lbw, rb, hg):
    assert hg == HEADS
    t = proj.shape[0]
    ncb = rb // CH

    def body(p_ref, lb_ref, o_ref, st_ref, s_scr):
        @pl.when(pl.program_id(1) == 0)
        def _():
            s_scr[...] = jnp.zeros_like(s_scr)

        lb = _lower_bound(lb_ref)
        causal = _tri(True)
        tril = _tri3(True)
        heads = range(hg)

        def chunk(cc, carry):
            r0 = pl.multiple_of(cc * CH, CH)
            rows = pl.ds(r0, CH)
            e = _hgrn_chunk_fwd(_head_segment(p_ref, rows, 0, hg), _head_segment(p_ref, rows, 1, hg), lb, tril)
            v = _bf(_head_segment(p_ref, rows, 2, hg))
            sts = [s_scr[k] for k in heads]
            qa, ka, qe, kd = _bf(e["qa"]), _bf(e["ka"]), _bf(e["qe"]), _bf(e["kd"])
            a = [_bf(jnp.where(causal, _mm_nt(_head(qa, k), _head(ka, k)), 0.0)) for k in heads]
            o_inter = [_mm_nt(_head(qe, k), _bf(sts[k])) for k in heads]
            kv = [_mm_tn(_head(v, k), _head(kd, k)) for k in heads]
            o_intra = [_mm(a[k], _head(v, k)) for k in heads]
            for k in heads:
                st_ref[cc, k] = sts[k]
                o_ref[rows, k * HD:(k + 1) * HD] = o_inter[k] + o_intra[k]
                s_scr[k] = sts[k] * _head(e["ebl"], k) + kv[k]
            return carry

        lax.fori_loop(0, ncb, chunk, 0, unroll=2)

    return pl.pallas_call(
        body, grid=(HEADS // hg, t // rb),
        in_specs=[pl.BlockSpec((rb, 3 * HD * hg), lambda h, i: (i, h)), pl.BlockSpec((2, HD * hg), lambda h, i: (0, h))],
        out_specs=[pl.BlockSpec((rb, HD * hg), lambda h, i: (i, h)),
                   pl.BlockSpec((ncb, hg, HD, HD), lambda h, i: (i, h, 0, 0))],
        out_shape=[jax.ShapeDtypeStruct((t, D), F32), jax.ShapeDtypeStruct((t // CH, HEADS, HD, HD), F32)],
        scratch_shapes=[pltpu.VMEM((hg, HD, HD), F32)],
        name="hgrn_fwd", compiler_params=_cp(48),
    )(proj, lbw)


def _hgrn_bwd(proj, lbw, do_raw, states, rb, hg):
    assert hg == HEADS
    t = proj.shape[0]
    nblk = t // rb
    ncb = rb // CH
    wd = HD * hg

    def body(p_ref, lb_ref, do_ref, st_ref, dp_ref, dlb_ref, ds_scr):
        @pl.when(pl.program_id(1) == 0)
        def _():
            ds_scr[...] = jnp.zeros_like(ds_scr)
            dlb_ref[...] = jnp.zeros_like(dlb_ref)

        lb = _lower_bound(lb_ref)
        causal = _tri(True)
        tril = _tri3(True)
        triu = _tri3(False)
        last_row = lax.broadcasted_iota(jnp.int32, (CH, HD * hg), 0) == CH - 1
        row0 = lax.broadcasted_iota(jnp.int32, (8, HD * hg), 0) == 0
        heads = range(hg)
        wide = lambda parts: jnp.concatenate(parts, axis=1)

        def chunk(it, carry):
            cc = ncb - 1 - it
            r0 = pl.multiple_of(cc * CH, CH)
            rows = pl.ds(r0, CH)
            hq = _head_segment(p_ref, rows, 0, hg)
            e = _hgrn_chunk_fwd(hq, _head_segment(p_ref, rows, 1, hg), lb, tril)
            v = _bf(_head_segment(p_ref, rows, 2, hg))
            do = _bf(do_ref[rows, :])
            sts = [st_ref[cc, k] for k in heads]
            dsts = [ds_scr[k] for k in heads]
            dlb_acc = dlb_ref[...]
            qa, ka, qe, kd = _bf(e["qa"]), _bf(e["ka"]), _bf(e["qe"]), _bf(e["kd"])
            a = [_bf(jnp.where(causal, _mm_nt(_head(qa, k), _head(ka, k)), 0.0)) for k in heads]
            da = [_bf(jnp.where(causal, _mm_nt(_head(do, k), _head(v, k)), 0.0)) for k in heads]
            dqe = wide([_mm(_head(do, k), _bf(sts[k])) for k in heads])
            dkd = wide([_mm(_head(v, k), _bf(dsts[k])) for k in heads])
            dv_state = [_mm_nt(_head(kd, k), _bf(dsts[k])) for k in heads]
            ds_new = [_mm_tn(_head(do, k), _head(qe, k)) for k in heads]
            dv_intra = [_mm_tn(a[k], _head(do, k)) for k in heads]
            dqa = wide([_mm(da[k], _head(ka, k)) for k in heads])
            dka = wide([_mm_tn(da[k], _head(qa, k)) for k in heads])
            dv = wide([dv_intra[k] + dv_state[k] for k in heads])
            dbl = e["ebl"] * wide([jnp.sum(sts[k] * dsts[k], axis=0, keepdims=True) for k in heads])
            dq = dqe * e["eb"] + dqa * e["ea"]
            dk = dka * e["ek"] + dkd * e["ed"]
            dkd_kd = dkd * kd.astype(F32)
            db = dqe * qe.astype(F32) + dqa * qa.astype(F32) - dka * ka.astype(F32) - dkd_kd
            db = db + jnp.where(last_row, dbl + jnp.sum(dkd_kd, axis=0, keepdims=True), 0.0)
            dg = _mm_tri_exact(triu, db)
            df = dg / e["f"] - dk
            sg = e["sg"]
            sq = e["sq"]
            dhq = _bf(dq * (sq * (1.0 + hq * (1.0 - sq))))
            dhf = _bf(df * (1.0 - lb) * sg * (1.0 - sg))
            dhi = _bf(dv)
            dlb_new = dlb_acc + jnp.where(row0, jnp.sum(df * (1.0 - sg), axis=0, keepdims=True), 0.0)
            for k in heads:
                ds_scr[k] = ds_new[k] + dsts[k] * _head(e["ebl"], k)
            dp_ref[rows, 0:wd] = dhq
            dp_ref[rows, wd:2 * wd] = dhf
            dp_ref[rows, 2 * wd:3 * wd] = dhi
            dlb_ref[...] = dlb_new
            return carry

        lax.fori_loop(0, ncb, chunk, 0, unroll=2)

    rev = lambda h, i: (nblk - 1 - i, h)
    return pl.pallas_call(
        body, grid=(HEADS // hg, nblk),
        in_specs=[pl.BlockSpec((rb, 3 * HD * hg), rev), pl.BlockSpec((2, HD * hg), lambda h, i: (0, h)),
                  pl.BlockSpec((rb, HD * hg), rev), pl.BlockSpec((ncb, hg, HD, HD), lambda h, i: (nblk - 1 - i, h, 0, 0))],
        out_specs=[pl.BlockSpec((rb, 3 * HD * hg), rev), pl.BlockSpec((8, HD * hg), lambda h, i: (0, h))],
        out_shape=[jax.ShapeDtypeStruct((t, 3 * D), BF16), jax.ShapeDtypeStruct((8, D), F32)],
        scratch_shapes=[pltpu.VMEM((hg, HD, HD), F32)],
        name="hgrn_bwd", compiler_params=_cp(48),
    )(proj, lbw, do_raw, states)


def _kv_variants(tile, odd):
    low = lax.broadcasted_iota(jnp.int32, tile.shape, 1) < 64
    if odd:
        hi = jnp.where(low, 0.0, tile)
        lo = pltpu.roll(hi, 64, 1)
    else:
        lo = jnp.where(low, tile, 0.0)
        hi = pltpu.roll(lo, 64, 1)
    return _bf(lo), _bf(hi)


def _attn_masks(n):
    qi = lax.broadcasted_iota(jnp.int32, (AB, AB), 0)
    kj = lax.broadcasted_iota(jnp.int32, (AB, AB), 1)
    cur = kj <= qi
    return cur, cur | (n > 0), qi <= kj


def _kv_all(prev_ref, cur_ref):
    out = []
    for tl in range(2):
        cols = slice(tl * 128, (tl + 1) * 128)
        tile = jnp.concatenate([prev_ref[:, cols], cur_ref[:, cols]], axis=0)
        out.append(_kv_variants(tile, 0))
        out.append(_kv_variants(tile, 1))
    return out


def _window(a2, cur):
    return jnp.where(cur, a2[:, AB:], a2[:, :AB])


def _attn_softmax(scores, sinks, cur, ok):
    s = [jnp.where(ok, _window(s2, cur) * ATT_SCALE, NEG) for s2 in scores]
    m = [jnp.maximum(jnp.max(si, axis=-1, keepdims=True), sink) for si, sink in zip(s, sinks)]
    p = [jnp.exp(si - mi) for si, mi in zip(s, m)]
    es = [jnp.exp(sink - mi) for sink, mi in zip(sinks, m)]
    inv = [1.0 / (jnp.sum(pi, axis=-1, keepdims=True) + ei) for pi, ei in zip(p, es)]
    return [pi * ii for pi, ii in zip(p, inv)], [ei * ii for ei, ii in zip(es, inv)]


def _spread(pc, cur):
    return jnp.concatenate([jnp.where(cur, 0.0, pc), jnp.where(cur, pc, 0.0)], axis=1)


def _spread_t(pct, cur_t):
    return jnp.concatenate([jnp.where(cur_t, 0.0, pct), jnp.where(cur_t, pct, 0.0)], axis=0)


def _attn_fwd(proj, sinks):
    t = proj.shape[0]
    nb = t // AB

    def body(q_ref, kc_ref, kp_ref, vc_ref, vp_ref, sink_ref, o_ref):
        cur, ok, _ = _attn_masks(pl.program_id(0))
        kvars = _kv_all(kp_ref, kc_ref)
        vvars = _kv_all(vp_ref, vc_ref)
        qps = [_bf(q_ref[:, 128 * j:128 * (j + 1)]) for j in range(8)]
        heads = [(j, ab) for j in range(8) for ab in range(2)]
        scores = [_mm_nt(qps[j], kvars[j // 2][ab]) for j, ab in heads]
        pcs, _ = _attn_softmax(scores, [sink_ref[0, h] for h in range(QH)], cur, ok)
        parts = [_mm(_bf(_spread(pcs[h], cur)), vvars[j // 2][ab]) for h, (j, ab) in enumerate(heads)]
        for j in range(8):
            o_ref[:, 128 * j:128 * (j + 1)] = parts[2 * j] + parts[2 * j + 1]

    prev = lambda n: jnp.maximum(n - 1, 0)
    return pl.pallas_call(
        body, grid=(nb,),
        in_specs=[pl.BlockSpec((AB, D), lambda n: (n, C_AQ // D)),
                  pl.BlockSpec((AB, 256), lambda n: (n, C_AK // 256)),
                  pl.BlockSpec((AB, 256), lambda n: (prev(n), C_AK // 256)),
                  pl.BlockSpec((AB, 256), lambda n: (n, C_AV // 256)),
                  pl.BlockSpec((AB, 256), lambda n: (prev(n), C_AV // 256)),
                  pl.BlockSpec(memory_space=pltpu.SMEM)],
        out_specs=pl.BlockSpec((AB, D), lambda n: (n, 0)),
        out_shape=jax.ShapeDtypeStruct((t, D), F32),
        name="attn_fwd", compiler_params=_cp(32),
    )(proj, proj, proj, proj, proj, sinks)


def _attn_bwd(proj, sinks, do_a):
    t = proj.shape[0]
    nb = t // AB

    def body(q_ref, kc_ref, kp_ref, vc_ref, vp_ref, do_ref, sink_ref, dq_ref, dkv_ref, dsink_ref, carry):
        n = pl.program_id(0)

        @pl.when(n == 0)
        def _():
            dsink_ref[...] = jnp.zeros_like(dsink_ref)
            carry[...] = jnp.zeros_like(carry)

        @pl.when(n < nb)
        def _():
            cur, ok, cur_t = _attn_masks(n)
            low = lax.broadcasted_iota(jnp.int32, (2 * AB, 128), 1) < 64
            lane = lax.broadcasted_iota(jnp.int32, (8, 128), 1)
            row0 = lax.broadcasted_iota(jnp.int32, (8, 128), 0) == 0
            kvars = _kv_all(kp_ref, kc_ref)
            vvars = _kv_all(vp_ref, vc_ref)
            qps = [_bf(q_ref[:, 128 * j:128 * (j + 1)]) for j in range(8)]
            dops = [_bf(do_ref[:, 128 * j:128 * (j + 1)]) for j in range(8)]
            heads = [(j, ab) for j in range(8) for ab in range(2)]
            scores = [_mm_nt(qps[j], kvars[j // 2][ab]) for j, ab in heads]
            dps = [_mm_nt(dops[j], vvars[j // 2][ab]) for j, ab in heads]
            pcs, pss = _attn_softmax(scores, [sink_ref[0, h] for h in range(QH)], cur, ok)
            dpcs = [_window(dp2, cur) for dp2 in dps]
            rss = [jnp.sum(pc * dpc, axis=-1, keepdims=True) for pc, dpc in zip(pcs, dpcs)]
            dscs = [pc * (dpc - rs) for pc, dpc, rs in zip(pcs, dpcs, rss)]
            dsink = jnp.zeros((8, 128), F32)
            for h in range(QH):
                dsink = dsink + jnp.where(row0 & (lane == h), -jnp.sum(pss[h] * rss[h]), 0.0)
            dq_terms = [_mm(_bf(_spread(dscs[h], cur)), kvars[j // 2][ab]) for h, (j, ab) in enumerate(heads)]
            for j in range(8):
                dq_ref[:, 128 * j:128 * (j + 1)] = _bf((dq_terms[2 * j] + dq_terms[2 * j + 1]) * ATT_SCALE)
            dsc_t = [_bf(_spread_t(dsc.T, cur_t)) for dsc in dscs]
            pc_t = [_bf(_spread_t(pc.T, cur_t)) for pc in pcs]
            dk_terms = [_mm(dsc_t[h], qps[j]) for h, (j, ab) in enumerate(heads)]
            dv_terms = [_mm(pc_t[h], dops[j]) for h, (j, ab) in enumerate(heads)]
            dk_ab = [[dk_terms[4 * g + ab] + dk_terms[4 * g + 2 + ab] for ab in range(2)] for g in range(4)]
            dv_ab = [[dv_terms[4 * g + ab] + dv_terms[4 * g + 2 + ab] for ab in range(2)] for g in range(4)]
            for tl in range(2):
                ke, ko = dk_ab[2 * tl], dk_ab[2 * tl + 1]
                ve, vo = dv_ab[2 * tl], dv_ab[2 * tl + 1]
                dkt = (jnp.where(low, ke[0], 0.0) + pltpu.roll(jnp.where(low, 0.0, ke[1]), 64, 1)
                       + jnp.where(low, 0.0, ko[1]) + pltpu.roll(jnp.where(low, ko[0], 0.0), 64, 1)) * ATT_SCALE
                dvt = (jnp.where(low, ve[0], 0.0) + pltpu.roll(jnp.where(low, 0.0, ve[1]), 64, 1)
                       + jnp.where(low, 0.0, vo[1]) + pltpu.roll(jnp.where(low, vo[0], 0.0), 64, 1))
                kcols = slice(tl * 128, (tl + 1) * 128)
                vcols = slice(256 + tl * 128, 256 + (tl + 1) * 128)
                dkv_ref[:, kcols] = _bf(carry[:, kcols] + dkt[0:AB])
                dkv_ref[:, vcols] = _bf(carry[:, vcols] + dvt[0:AB])
                carry[:, kcols] = dkt[AB:2 * AB]
                carry[:, vcols] = dvt[AB:2 * AB]
            dsink_ref[...] += dsink

        @pl.when(n == nb)
        def _():
            dkv_ref[...] = _bf(carry[...])

    cur = lambda n: jnp.minimum(n, nb - 1)
    prev = lambda n: jnp.clip(n - 1, 0, nb - 1)
    return pl.pallas_call(
        body, grid=(nb + 1,),
        in_specs=[pl.BlockSpec((AB, D), lambda n: (cur(n), C_AQ // D)),
                  pl.BlockSpec((AB, 256), lambda n: (cur(n), C_AK // 256)),
                  pl.BlockSpec((AB, 256), lambda n: (prev(n), C_AK // 256)),
                  pl.BlockSpec((AB, 256), lambda n: (cur(n), C_AV // 256)),
                  pl.BlockSpec((AB, 256), lambda n: (prev(n), C_AV // 256)),
                  pl.BlockSpec((AB, D), lambda n: (cur(n), 0)),
                  pl.BlockSpec(memory_space=pltpu.SMEM)],
        out_specs=[pl.BlockSpec((AB, D), lambda n: (cur(n), 0)),
                   pl.BlockSpec((AB, 512), lambda n: (prev(n), 0)),
                   pl.BlockSpec((8, 128), lambda n: (0, 0))],
        out_shape=[jax.ShapeDtypeStruct((t, D), BF16), jax.ShapeDtypeStruct((t, 512), BF16),
                   jax.ShapeDtypeStruct((8, 128), F32)],
        scratch_shapes=[pltpu.VMEM((AB, 512), F32)],
        name="attn_bwd", compiler_params=_cp(32),
    )(proj, proj, proj, proj, proj, do_a, sinks)


def _silu_and_grad(v):
    s = _sigmoid(v)
    return v * s, s * (1.0 + v * (1.0 - s))


def _tail(o_raw, o_a, proj, x2d, tgt, wbh, wba, wout, hnw, fnw, tb):
    t = x2d.shape[0]

    def body(or_ref, oa_ref, hg_ref, ag0, ag1, mh0, mh1, ma0, ma1, x_ref, t_ref, wbh_ref, wba_ref, wout_ref, hnw_ref,
             fnw_ref, dx2_ref, dor_ref, doa_ref, dhg_ref, dagm_ref, gh_ref, ga_ref, mg_ref, dyh_ref, dya_ref, dx2b_ref,
             sums_ref):
        @pl.when(pl.program_id(0) == 0)
        def _():
            sums_ref[...] = jnp.zeros_like(sums_ref)

        halves = lambda a, b: jnp.concatenate([a[...], b[...]], axis=1)
        hnw_v = hnw_ref[...]
        fnw_v = fnw_ref[...]
        o = or_ref[...]
        rs, xhs = [], []
        for h in range(HEADS):
            oh = o[:, h * HD:(h + 1) * HD]
            r = lax.rsqrt(jnp.mean(oh * oh, axis=-1, keepdims=True) + EPS)
            rs.append(r)
            xhs.append(oh * r)
        xh = jnp.concatenate(xhs, axis=1)
        on = xh * hnw_v
        sil_hg, dsil_hg = _silu_and_grad(hg_ref[...])
        gh_b = _bf(on * sil_hg)
        y_h = _mm(gh_b, wbh_ref[...])
        oa = oa_ref[...]
        sil_ag, dsil_ag = _silu_and_grad(halves(ag0, ag1))
        ga_b = _bf(oa * sil_ag)
        y_a = _mm(ga_b, wba_ref[...])
        s_mh = _sigmoid(halves(mh0, mh1))
        s_ma = _sigmoid(halves(ma0, ma1))
        mg_b = _bf(s_mh * y_h + s_ma * y_a)
        x2 = x_ref[...] + _mm(mg_b, wout_ref[...])
        r2 = lax.rsqrt(jnp.mean(x2 * x2, axis=-1, keepdims=True) + EPS)
        xh2 = x2 * r2
        err = xh2 * fnw_v - t_ref[...]
        loss = 0.5 * jnp.sum(jnp.mean(err * err, axis=-1, keepdims=True))
        dy = err * (1.0 / D)
        dfnw = jnp.sum(dy * xh2, axis=0, keepdims=True)
        dxh2 = dy * fnw_v
        dx2 = r2 * (dxh2 - xh2 * jnp.mean(dxh2 * xh2, axis=-1, keepdims=True))
        dx2_ref[...] = dx2
        dx2_b = _bf(dx2)
        dmg = _mm_nt(dx2_b, wout_ref[...])
        dmg_h = dmg * s_mh
        dmg_a = dmg * s_ma
        dyh_b = _bf(dmg_h)
        dya_b = _bf(dmg_a)
        dagm_ref[:, D:2 * D] = _bf(dmg_h * y_h * (1.0 - s_mh))
        dagm_ref[:, 2 * D:3 * D] = _bf(dmg_a * y_a * (1.0 - s_ma))
        dgh = _mm_nt(dyh_b, wbh_ref[...])
        dga = _mm_nt(dya_b, wba_ref[...])
        doa_ref[...] = dga * sil_ag
        dagm_ref[:, 0:D] = _bf(dga * oa * dsil_ag)
        dhg_ref[...] = _bf(dgh * on * dsil_hg)
        don = dgh * sil_hg
        dhnw = jnp.sum(don * xh, axis=0, keepdims=True)
        dxh = don * hnw_v
        dos = []
        for h in range(HEADS):
            sl = slice(h * HD, (h + 1) * HD)
            dos.append(rs[h] * (dxh[:, sl] - xhs[h] * jnp.mean(dxh[:, sl] * xhs[h], axis=-1, keepdims=True)))
        dor_ref[...] = jnp.concatenate(dos, axis=1)
        gh_ref[...] = gh_b
        ga_ref[...] = ga_b
        mg_ref[...] = mg_b
        dyh_ref[...] = dyh_b
        dya_ref[...] = dya_b
        dx2b_ref[...] = dx2_b
        row = lax.broadcasted_iota(jnp.int32, (8, D), 0)
        sums_ref[...] += jnp.where(row == 0, dfnw, 0.0) + jnp.where(row == 1, dhnw, 0.0) + jnp.where(row == 2, loss, 0.0)

    rowblk = lambda c: pl.BlockSpec((tb, D), lambda i: (i, c))
    half = lambda c: pl.BlockSpec((tb, 512), lambda i: (i, c))
    full = lambda shape: pl.BlockSpec(shape, lambda i: (0, 0))
    return pl.pallas_call(
        body, grid=(t // tb,),
        in_specs=[rowblk(0), rowblk(0), rowblk(C_HG // D), half(C_AG // 512), half(C_AG // 512 + 1), half(C_MH // 512),
                  half(C_MH // 512 + 1), half(C_MA // 512), half(C_MA // 512 + 1), rowblk(0), rowblk(0),
                  full((D, D)), full((D, D)), full((D, D)), full((1, D)), full((1, D))],
        out_specs=[rowblk(0), rowblk(0), rowblk(0), rowblk(0), pl.BlockSpec((tb, 3 * D), lambda i: (i, 0))]
        + [rowblk(0)] * 6 + [full((8, D))],
        out_shape=[jax.ShapeDtypeStruct((t, D), F32)] * 3
        + [jax.ShapeDtypeStruct((t, D), BF16), jax.ShapeDtypeStruct((t, 3 * D), BF16)]
        + [jax.ShapeDtypeStruct((t, D), BF16)] * 6 + [jax.ShapeDtypeStruct((8, D), F32)],
        name="tail", compiler_params=_cp(56),
    )(o_raw, o_a, proj, proj, proj, proj, proj, proj, proj, x2d, tgt, wbh, wba, wout, hnw, fnw)


def _wgrad3(gh, dyh, ga, dya, mg, dx2b, tk):
    t = dyh.shape[0]

    def body(a0, b0, a1, b1, a2, b2, o0, o1, o2):
        @pl.when(pl.program_id(0) == 0)
        def _():
            o0[...] = jnp.zeros_like(o0)
            o1[...] = jnp.zeros_like(o1)
            o2[...] = jnp.zeros_like(o2)

        o0[...] += _mm_tn(a0[...], b0[...])
        o1[...] += _mm_tn(a1[...], b1[...])
        o2[...] += _mm_tn(a2[...], b2[...])

    blk = pl.BlockSpec((tk, D), lambda k: (k, 0))
    out = pl.BlockSpec((D, D), lambda k: (0, 0))
    return pl.pallas_call(
        body, grid=(t // tk,), in_specs=[blk] * 6, out_specs=[out] * 3,
        out_shape=[jax.ShapeDtypeStruct((D, D), F32)] * 3,
        name="wgrad3", compiler_params=_cp(48),
    )(gh, dyh, ga, dya, mg, dx2b)


def _inproj_wgrad_piece(xnt, piece, nb, name):
    t = xnt.shape[1]
    width = piece.shape[1]

    def body(xnt_ref, p_ref, o_ref):
        o_ref[...] = _mm(xnt_ref[...], p_ref[...])

    return pl.pallas_call(
        body, grid=(width // nb,),
        in_specs=[pl.BlockSpec((D, t), lambda j: (0, 0), pipeline_mode=pl.Buffered(1)),
                  pl.BlockSpec((t, nb), lambda j: (0, j))],
        out_specs=pl.BlockSpec((D, nb), lambda j: (0, j)),
        out_shape=jax.ShapeDtypeStruct((D, width), F32),
        name=name, compiler_params=_cp(56),
    )(xnt, piece)


def _inproj_dgrad(pieces, w_p, x2d, dx2, norm_w, tb, after):
    t = x2d.shape[0]

    def body(*refs):
        piece_refs = refs[:len(pieces)]
        w_ref, x_ref, dx2_ref, nw_ref, _, gx_ref, dnw_ref = refs[len(pieces):]

        @pl.when(pl.program_id(0) == 0)
        def _():
            dnw_ref[...] = jnp.zeros_like(dnw_ref)

        dxn = None
        off = 0
        for p in piece_refs:
            width = p.shape[1]
            for q in range(w_ref.shape[0]):
                lo, hi = max(off, q * PAIR), min(off + width, (q + 1) * PAIR)
                if lo < hi:
                    term = _mm_nt(p[:, lo - off:hi - off], w_ref[q, :, lo - q * PAIR:hi - q * PAIR])
                    dxn = term if dxn is None else dxn + term
            off += width
        xv = x_ref[...]
        r = lax.rsqrt(jnp.mean(xv * xv, axis=-1, keepdims=True) + EPS)
        xh = xv * r
        dxh = dxn * nw_ref[...]
        gx_ref[...] = dx2_ref[...] + r * (dxh - xh * jnp.mean(dxh * xh, axis=-1, keepdims=True))
        row0 = lax.broadcasted_iota(jnp.int32, (8, D), 0) == 0
        dnw_ref[...] += jnp.where(row0, jnp.sum(dxn * xh, axis=0, keepdims=True), 0.0)

    rowblk = pl.BlockSpec((tb, D), lambda i: (i, 0))
    return pl.pallas_call(
        body, grid=(t // tb,),
        in_specs=[pl.BlockSpec((tb, p.shape[1]), lambda i: (i, 0)) for p in pieces]
        + [pl.BlockSpec(w_p.shape, lambda i: (0, 0, 0), pipeline_mode=pl.Buffered(1)), rowblk, rowblk,
           pl.BlockSpec((1, D), lambda i: (0, 0)), pl.BlockSpec(memory_space=pl.ANY)],
        out_specs=[rowblk, pl.BlockSpec((8, D), lambda i: (0, 0))],
        out_shape=[jax.ShapeDtypeStruct((t, D), F32), jax.ShapeDtypeStruct((8, D), F32)],
        name="inproj_dgrad", compiler_params=_cp(60),
    )(*pieces, w_p, x2d, dx2, norm_w, after)


def _adamw_math(w, g, m, v):
    m = B1 * m + (1.0 - B1) * g
    v = B2 * v + (1.0 - B2) * (g * g)
    m_hat = m / (1.0 - B1 ** STEP)
    v_hat = v / (1.0 - B2 ** STEP)
    delta = -LR * (m_hat / (jnp.sqrt(v_hat) + ADAM_EPS) + WD * w)
    return delta, m, v


def _adamw_shard(recv, sums, chip, w, m, v, rows, name):
    nparts, nr, nc = recv.shape

    def body(chip_ref, own_ref, p_ref, w_ref, m_ref, v_ref, g_ref, d_ref, nm_ref, nv_ref):
        g = own_ref[0].astype(F32)
        for s in range(nparts):
            g = g + p_ref[s].astype(F32)
        d, nm, nv = _adamw_math(w_ref[...], g, m_ref[...], v_ref[...])
        g_ref[...] = g
        d_ref[...] = d
        nm_ref[...] = nm
        nv_ref[...] = nv

    blk = pl.BlockSpec((rows, nc), lambda i, chip_ref: (i, 0))
    return pl.pallas_call(
        body,
        grid_spec=pltpu.PrefetchScalarGridSpec(
            num_scalar_prefetch=1, grid=(nr // rows,),
            in_specs=[pl.BlockSpec((1, rows, nc), lambda i, chip_ref: (chip_ref[0], i, 0)),
                      pl.BlockSpec((nparts, rows, nc), lambda i, chip_ref: (0, i, 0)), blk, blk, blk],
            out_specs=[blk] * 4),
        out_shape=[jax.ShapeDtypeStruct((nr, nc), F32)] * 4,
        name=name, compiler_params=_cp(48),
    )(chip, sums, recv, w, m, v)


SMALL_ROWS = dict(norm_w=0, lower_bound=1, hgrn_norm_w=3, final_norm_w=4, sinks=5, loss=6)


def _pack_small_grads(dnw, dlb, sums, dsink):
    def body(dnw_ref, dlb_ref, sums_ref, dsink_ref, o_ref):
        o_ref[...] = jnp.zeros_like(o_ref)
        o_ref[0:1, :] = dnw_ref[0:1, :]
        o_ref[1:2, :] = dlb_ref[0:1, :]
        o_ref[3:4, :] = sums_ref[1:2, :]
        o_ref[4:5, :] = sums_ref[0:1, :]
        o_ref[5:6, 0:128] = dsink_ref[0:1, :]
        o_ref[6:7, :] = sums_ref[2:3, :]

    return pl.pallas_call(body, out_shape=jax.ShapeDtypeStruct((8, D), F32), name="pack_small_grads")(dnw, dlb, sums, dsink)


def _adamw_small(parts, ws, ms, vs):
    shapes = [a.shape for a in ws]

    def body(p_ref, *refs):
        w, m, v = refs[0:5], refs[5:10], refs[10:15]
        outs = [refs[15 + 5 * i:20 + 5 * i] for i in range(4)]
        loss_ref = refs[35]

        def total(row, width):
            g = p_ref[0, row:row + 1, 0:width]
            for s in range(1, NDEV):
                g = g + p_ref[s, row:row + 1, 0:width]
            return g

        loss_ref[...] = total(6, 128)
        lb = _lower_bound(w[1])
        ga0 = total(1, D) * lb * (1.0 - lb)
        grads = [total(0, D), None, total(3, D), total(4, D), total(5, QH)]
        for i in (0, 2, 3, 4):
            res = (grads[i],) + _adamw_math(w[i][...], grads[i], m[i][...], v[i][...])
            for o, val in zip(outs, res):
                o[i][...] = val
        for r, g in ((0, ga0), (1, -ga0)):
            res = (g,) + _adamw_math(w[1][r:r + 1, :], g, m[1][r:r + 1, :], v[1][r:r + 1, :])
            for o, val in zip(outs, res):
                o[1][r:r + 1, :] = val

    res = pl.pallas_call(
        body, out_shape=[jax.ShapeDtypeStruct(s, F32) for s in shapes] * 4 + [jax.ShapeDtypeStruct((1, 128), F32)],
        name="adamw_small",
    )(parts, *ws, *ms, *vs)
    return [res[5 * i:5 * i + 5] for i in range(4)], res[20][0, 0]


def kernel(x, norm_w, w_in, hgrn_lower_bound, hgrn_norm_w, w_branch_hgrn, attn_sinks, w_branch_attn, w_out, final_norm_w, loss_target, m_norm_w, m_w_in, m_hgrn_lower_bound, m_hgrn_norm_w, m_w_branch_hgrn, m_attn_sinks, m_w_branch_attn, m_w_out, m_final_norm_w, v_norm_w, v_w_in, v_hgrn_lower_bound, v_hgrn_norm_w, v_w_branch_hgrn, v_attn_sinks, v_w_branch_attn, v_w_out, v_final_norm_w):
    t = x.shape[1]
    x2d = x.reshape(t, D)
    tgt = loss_target.reshape(t, D)
    fnw = final_norm_w.reshape(1, D)
    row_blk = min(256, t)
    big_blk = min(512, t)

    chip = (2 * lax.axis_index("x") + lax.axis_index("y")).astype(jnp.int32).reshape(1)
    w_p, xn, xnt, proj_own = _gather_in_projection(w_in[0], x2d, norm_w)
    wbh, wba, wout = (g.reshape(D, D) for g in _gather_square(
        [w_branch_hgrn[0].astype(BF16), w_branch_attn[0].astype(BF16), w_out[0].astype(BF16)], after=w_p))

    proj = _inproj_fwd(xn, w_p, proj_own, chip, min(1024, t))
    o_raw, states = _hgrn_fwd(proj, hgrn_lower_bound, big_blk, HGRN_GROUP)
    o_a = _attn_fwd(proj, attn_sinks)
    (dx2, do_raw, do_a, d_hg, d_agm, gh, ga, mg, dyh, dya, dx2b, sums) = _tail(
        o_raw, o_a, proj, x2d, tgt, wbh, wba, wout, hgrn_norm_w, fnw, row_blk)
    dwbh, dwba, dwout = _wgrad3(gh, dyh, ga, dya, mg, dx2b, big_blk)
    d_aq, d_kv, dsink = _attn_bwd(proj, attn_sinks, do_a)
    d_hgrn, dlb = _hgrn_bwd(proj, hgrn_lower_bound, do_raw, states, big_blk, HGRN_GROUP)
    pieces = (d_hgrn, d_hg, d_aq, d_kv, d_agm)
    dw_pieces = [_inproj_wgrad_piece(xnt, p, CB, "inproj_wgrad_" + n)
                 for p, n in zip(pieces, ("hgrn", "hgate", "aq", "kv", "gates"))]

    dwin_r = jnp.concatenate(dw_pieces, axis=1).reshape(D, NDEV, IN_SHARD).transpose(1, 0, 2).astype(BF16)
    slots = lambda a: a.reshape(NDEV, ROW_SHARD, D).astype(BF16)
    partials = [dwin_r, slots(dwbh), slots(dwba), slots(dwout)]
    got = _exchange_pair(partials)
    core = lax.axis_index("c").astype(jnp.int32).reshape(1)
    pair_sums = [_pair_sum(a, g, core, ROW_SHARD, "pair_sum_" + n)
                 for a, g, n in zip(partials, got, ("w_in", "w_bh", "w_ba", "w_out"))]
    rin, rbh, rba, rout = _exchange_chips(pair_sums)
    grad_x, dnw = _inproj_dgrad(pieces, w_p, x2d, dx2, norm_w, big_blk, after=pair_sums[0])
    rsm = _exchange_small(_pack_small_grads(dnw, dlb, sums, dsink))
    s_in, s_bh, s_ba, s_out = pair_sums
    g_in, d_in, nm_in, nv_in = _adamw_shard(rin, s_in, chip, w_in[0], m_w_in[0], v_w_in[0], 128, "adamw_w_in")
    g_bh, d_bh, nm_bh, nv_bh = _adamw_shard(
        rbh, s_bh, chip, w_branch_hgrn[0], m_w_branch_hgrn[0], v_w_branch_hgrn[0], 128, "adamw_w_bh")
    g_ba, d_ba, nm_ba, nv_ba = _adamw_shard(
        rba, s_ba, chip, w_branch_attn[0], m_w_branch_attn[0], v_w_branch_attn[0], 128, "adamw_w_ba")
    g_out, d_out, nm_out, nv_out = _adamw_shard(rout, s_out, chip, w_out[0], m_w_out[0], v_w_out[0], 128, "adamw_w_out")
    (sg, sd, sm, sv), loss = _adamw_small(
        rsm,
        (norm_w, hgrn_lower_bound, hgrn_norm_w, fnw, attn_sinks),
        (m_norm_w, m_hgrn_lower_bound, m_hgrn_norm_w, m_final_norm_w.reshape(1, D), m_attn_sinks),
        (v_norm_w, v_hgrn_lower_bound, v_hgrn_norm_w, v_final_norm_w.reshape(1, D), v_attn_sinks))

    def group(s, w_in_v, bh, ba, out):
        nw, lb, hnw, fn, sinks = s
        return (nw, w_in_v[None], lb, hnw, bh[None], sinks, ba[None], out[None], fn.reshape(D))

    return (loss, grad_x.reshape(1, t, D),
            *group(sg, g_in, g_bh, g_ba, g_out), *group(sd, d_in, d_bh, d_ba, d_out),
            *group(sm, nm_in, nm_bh, nm_ba, nm_out), *group(sv, nv_in, nv_bh, nv_ba, nv_out))
```

```python
import functools

import jax
import jax.numpy as jnp
from jax import lax
from jax.experimental import pallas as pl
from jax.experimental.pallas import tpu as pltpu
from jax.experimental.pallas import tpu_sc as plsc

F32 = jnp.float32
BF16 = jnp.bfloat16

D = 1024
DIN = 8704
NDEV = 8
IN_SHARD = DIN // NDEV
PAIR = 2 * IN_SHARD
ROW_SHARD = D // NDEV
HEADS = 8
HD = 128
CH = 64
HGRN_GROUP = 8
QH = 16
AB = 128
EPS = 1e-6
NEG = -1e30
ATT_SCALE = 0.125

C_HGRN = 0
C_HG = 3072
C_AQ = 4096
C_AK = 5120
C_AV = 5376
C_AG = 5632
C_MH = 6656
C_MA = 7680
CB = 512

LR = 0.001
B1 = 0.9
B2 = 0.999
ADAM_EPS = 1e-08
WD = 0.01
STEP = 10

V7X_VMEM_BYTES = 64 * 1024 * 1024
MESH = pl.DeviceIdType.MESH


def _cp(vmem_mb):
    return pltpu.CompilerParams(vmem_limit_bytes=vmem_mb * 1024 * 1024)


def _mm(a, b):
    return jnp.dot(a, b, preferred_element_type=F32)


def _mm_nt(a, b):
    return lax.dot_general(a, b, (((1,), (1,)), ((), ())), preferred_element_type=F32)


def _mm_tn(a, b):
    return lax.dot_general(a, b, (((0,), (0,)), ((), ())), preferred_element_type=F32)


def _tri3(lower):
    r = lax.broadcasted_iota(jnp.int32, (CH, 3 * CH), 0)
    c = lax.broadcasted_iota(jnp.int32, (CH, 3 * CH), 1)
    c = jnp.where(c >= 2 * CH, c - 2 * CH, jnp.where(c >= CH, c - CH, c))
    return ((r >= c) if lower else (c >= r)).astype(BF16)


def _mm_tri_exact(tri3, g):
    g1 = g.astype(BF16)
    r1 = g - g1.astype(F32)
    g2 = r1.astype(BF16)
    g3 = (r1 - g2.astype(F32)).astype(BF16)
    return _mm(tri3, jnp.concatenate([g1, g2, g3], axis=0))


def _sigmoid(v):
    return 0.5 * jnp.tanh(0.5 * v) + 0.5


def _bf(v):
    return v.astype(BF16)


def _place():
    x, y, c = lax.axis_index("x"), lax.axis_index("y"), lax.axis_index("c")
    return (x, y, c), (x, y, 1 - c), [(1 - x, y), (x, 1 - y), (1 - x, 1 - y)]


def _dev_index(px, py, pc):
    return 4 * px + 2 * py + pc


def _gather_in_projection(w_in_s, x2d, norm_w):
    half = D // 2
    t = x2d.shape[0]
    prep_rows = min(512, t)
    nprep = t // prep_rows

    def body(win_ref, x_hbm, nw_ref, wp_g, xn_hbm, xnt_hbm, proj_hbm, give, take, mine, xbuf, xnbuf, xntbuf, w_own, pbuf,
             send_sems, recv_sems, loc_sem, swap_sems, in_sems, out_sems, own_sem):
        (x, y, c), sibling, chips = _place()
        give[...] = win_ref[pl.ds(pl.multiple_of(half * (1 - c), half), half), :].astype(BF16)
        swap = pltpu.make_async_remote_copy(src_ref=give, dst_ref=take, send_sem=swap_sems.at[0], recv_sem=swap_sems.at[1],
                                            device_id=sibling, device_id_type=MESH)
        swap.start()
        swap.wait()
        own = win_ref[pl.ds(pl.multiple_of(half * c, half), half), :]
        other = take[...].astype(F32)
        mine[...] = jnp.where(c == 0, jnp.concatenate([own, other], axis=1),
                              jnp.concatenate([other, own], axis=1)).astype(BF16)

        def place(px, py, pc):
            return wp_g.at[2 * px + py, pl.ds(pl.multiple_of(half * pc, half), half), :]

        def copy(kind, origin, to, src=mine):
            return pltpu.make_async_remote_copy(
                src_ref=src, dst_ref=place(*origin), send_sem=send_sems.at[kind], recv_sem=recv_sems.at[kind],
                device_id=to, device_id_type=MESH)

        me = (x, y, c)
        local = pltpu.make_async_copy(mine, place(*me), loc_sem)
        local.start()
        first = [copy(0, me, sibling)] + [copy(1 + j, me, (*chip, c)) for j, chip in enumerate(chips)]
        for cp in first:
            cp.start()

        copy(0, (x, y, 1 - c), me).wait_recv()
        local.wait()
        my_chip = 2 * x + y
        fetch = pltpu.make_async_copy(wp_g.at[my_chip], w_own, own_sem)
        fetch.start()

        def rows_of(i):
            return pl.ds(pl.multiple_of(i * prep_rows, prep_rows), prep_rows)

        def load(i, slot):
            return pltpu.make_async_copy(x_hbm.at[rows_of(i), :], xbuf.at[slot], in_sems.at[slot])

        def stores(i, slot):
            own_cols = pl.ds(pl.multiple_of(my_chip * PAIR, 128), PAIR)
            return (pltpu.make_async_copy(xnbuf.at[slot], xn_hbm.at[rows_of(i), :], out_sems.at[slot, 0]),
                    pltpu.make_async_copy(xntbuf.at[slot], xnt_hbm.at[:, rows_of(i)], out_sems.at[slot, 1]),
                    pltpu.make_async_copy(pbuf.at[slot], proj_hbm.at[rows_of(i), own_cols], out_sems.at[slot, 2]))

        load(0, 0).start()
        fetch.wait()

        def prep(i, carry):
            slot = lax.rem(i, 2)
            load(i, slot).wait()

            @pl.when(i + 1 < nprep)
            def _():
                load(i + 1, 1 - slot).start()

            @pl.when(i >= 2)
            def _():
                for cp in stores(i - 2, slot):
                    cp.wait()

            xv = xbuf[slot]
            xn = (xv * lax.rsqrt(jnp.mean(xv * xv, axis=-1, keepdims=True) + EPS)) * nw_ref[...]
            xn_b = xn.astype(BF16)
            xnbuf[slot] = xn_b
            xntbuf[slot] = xn.T.astype(BF16)
            pbuf[slot] = _mm(xn_b, w_own[...])
            for cp in stores(i, slot):
                cp.start()
            return carry

        lax.fori_loop(0, nprep, prep, 0)
        for i in range(max(nprep - 2, 0), nprep):
            for cp in stores(i, i % 2):
                cp.wait()

        passed = []
        for j, chip in enumerate(chips):
            copy(1 + j, (*chip, c), me).wait_recv()
            cp = copy(4 + j, (*chip, c), sibling, src=place(*chip, c))
            cp.start()
            passed.append(cp)
        for j, chip in enumerate(chips):
            copy(4 + j, (*chip, 1 - c), me).wait_recv()
        for cp in first + passed:
            cp.wait_send()

    vm = pl.BlockSpec(memory_space=pltpu.VMEM)
    hbm = pl.BlockSpec(memory_space=pl.ANY)
    return pl.pallas_call(
        body,
        out_shape=[jax.ShapeDtypeStruct((NDEV // 2, D, PAIR), BF16), jax.ShapeDtypeStruct((t, D), BF16),
                   jax.ShapeDtypeStruct((D, t), BF16), jax.ShapeDtypeStruct((t, DIN), F32)],
        in_specs=[vm, hbm, vm],
        out_specs=[hbm, hbm, hbm, hbm],
        scratch_shapes=[pltpu.VMEM((half, IN_SHARD), BF16), pltpu.VMEM((half, IN_SHARD), BF16),
                        pltpu.VMEM((half, PAIR), BF16),
                        pltpu.VMEM((2, prep_rows, D), F32), pltpu.VMEM((2, prep_rows, D), BF16),
                        pltpu.VMEM((2, D, prep_rows), BF16),
                        pltpu.VMEM((D, PAIR), BF16), pltpu.VMEM((2, prep_rows, PAIR), F32),
                        pltpu.SemaphoreType.DMA((NDEV - 1,)), pltpu.SemaphoreType.DMA((NDEV - 1,)),
                        pltpu.SemaphoreType.DMA, pltpu.SemaphoreType.DMA((2,)),
                        pltpu.SemaphoreType.DMA((2,)), pltpu.SemaphoreType.DMA((2, 3)), pltpu.SemaphoreType.DMA],
        name="gather_in_projection", compiler_params=_cp(56),
    )(w_in_s, x2d, norm_w)


def _gather_square(shards, after):
    n = len(shards)

    def launch(*refs):
        ins, outs = refs[:n], refs[n + 1:2 * n + 1]
        send_sems, recv_sems, loc_sems = refs[2 * n + 1:]
        (x, y, c), _, _ = _place()
        me = _dev_index(x, y, c)
        peers = [(1 - x if r & 4 else x, 1 - y if r & 2 else y, 1 - c if r & 1 else c) for r in range(1, NDEV)]
        barrier = pltpu.get_barrier_semaphore()
        for peer in peers:
            pl.semaphore_signal(barrier, inc=1, device_id=peer, device_id_type=MESH)
        pl.semaphore_wait(barrier, NDEV - 1)
        local = [pltpu.make_async_copy(ins[k], outs[k].at[me], loc_sems.at[k]) for k in range(n)]
        copies = [pltpu.make_async_remote_copy(
            src_ref=ins[k], dst_ref=outs[k].at[me], send_sem=send_sems.at[r, k], recv_sem=recv_sems.at[r, k],
            device_id=peer, device_id_type=MESH) for r, peer in enumerate(peers) for k in range(n)]
        for cp in local + copies:
            cp.start()
        for r, peer in enumerate(peers):
            for k in range(n):
                pltpu.make_async_remote_copy(
                    src_ref=ins[k], dst_ref=outs[k].at[_dev_index(*peer)], send_sem=send_sems.at[r, k],
                    recv_sem=recv_sems.at[r, k], device_id=peer, device_id_type=MESH).wait_recv()
        for cp in copies:
            cp.wait_send()
        for cp in local:
            cp.wait()

    return pl.kernel(
        launch, out_type=[jax.ShapeDtypeStruct((NDEV,) + a.shape, a.dtype) for a in shards],
        mesh=plsc.ScalarSubcoreMesh(axis_name="sequencer", num_cores=1), name="gather_square",
        scratch_types=(pltpu.SemaphoreType.DMA((NDEV - 1, n)), pltpu.SemaphoreType.DMA((NDEV - 1, n)),
                       pltpu.SemaphoreType.DMA((n,))),
        compiler_params=pltpu.CompilerParams(collective_id=2),
    )(*shards, after)


def _exchange_pair(arrs):
    n = len(arrs)

    def launch(*refs):
        ins, got = refs[:n], refs[n:2 * n]
        send_sems, recv_sems = refs[2 * n:]
        (x, y, c), sibling, _ = _place()
        barrier = pltpu.get_barrier_semaphore()
        pl.semaphore_signal(barrier, inc=1, device_id=sibling, device_id_type=MESH)
        pl.semaphore_wait(barrier, 1)
        sends = [pltpu.make_async_remote_copy(
            src_ref=ins[k].at[_dev_index(q // 2, q % 2, 1 - c)], dst_ref=got[k].at[q], send_sem=send_sems.at[q, k],
            recv_sem=recv_sems.at[q, k], device_id=sibling, device_id_type=MESH) for q in range(4) for k in range(n)]
        for cp in sends:
            cp.start()
        for cp in sends:
            cp.wait_recv()
        for cp in sends:
            cp.wait_send()

    return pl.kernel(
        launch, out_type=[jax.ShapeDtypeStruct((4,) + a.shape[1:], a.dtype) for a in arrs],
        mesh=plsc.ScalarSubcoreMesh(axis_name="sequencer", num_cores=1), name="exchange_pair",
        scratch_types=(pltpu.SemaphoreType.DMA((4, n)), pltpu.SemaphoreType.DMA((4, n))),
        compiler_params=pltpu.CompilerParams(collective_id=0),
    )(*arrs)


def _pair_sum(full, got, core, rows, name):
    _, nr, nc = got.shape

    def body(core_ref, a_ref, b_ref, o_ref):
        o_ref[...] = (a_ref[...].astype(F32) + b_ref[...].astype(F32)).astype(BF16)

    blk = pl.BlockSpec((1, rows, nc), lambda q, i, core_ref: (q, i, 0))
    return pl.pallas_call(
        body,
        grid_spec=pltpu.PrefetchScalarGridSpec(
            num_scalar_prefetch=1, grid=(4, nr // rows),
            in_specs=[pl.BlockSpec((1, rows, nc), lambda q, i, core_ref: (2 * q + core_ref[0], i, 0)), blk],
            out_specs=blk),
        out_shape=jax.ShapeDtypeStruct(got.shape, BF16), name=name,
    )(core, full, got)


def _exchange_chips(sums):
    n = len(sums)

    def launch(*refs):
        ins, outs = refs[:n], refs[n:2 * n]
        send_sems, recv_sems = refs[2 * n:]
        (x, y, c), _, chips = _place()
        barrier = pltpu.get_barrier_semaphore()
        for px, py in chips:
            pl.semaphore_signal(barrier, inc=1, device_id=(px, py, c), device_id_type=MESH)
        pl.semaphore_wait(barrier, len(chips))
        copies = [pltpu.make_async_remote_copy(
            src_ref=ins[k].at[2 * px + py], dst_ref=outs[k].at[j], send_sem=send_sems.at[j, k],
            recv_sem=recv_sems.at[j, k], device_id=(px, py, c), device_id_type=MESH)
            for j, (px, py) in enumerate(chips) for k in range(n)]
        for cp in copies:
            cp.start()
        for cp in copies:
            cp.wait_recv()
        for cp in copies:
            cp.wait_send()

    return pl.kernel(
        launch, out_type=[jax.ShapeDtypeStruct((3,) + a.shape[1:], a.dtype) for a in sums],
        mesh=plsc.ScalarSubcoreMesh(axis_name="sequencer", num_cores=1), name="exchange_chips",
        scratch_types=(pltpu.SemaphoreType.DMA((3, n)), pltpu.SemaphoreType.DMA((3, n))),
        compiler_params=pltpu.CompilerParams(collective_id=1),
    )(*sums)


def _exchange_square(partials):
    n = len(partials)

    def launch(*refs):
        ins, outs = refs[:n], refs[n:2 * n]
        send_sems, recv_sems, loc_sems = refs[2 * n:]
        (x, y, c), _, _ = _place()
        me = _dev_index(x, y, c)
        peers = [(1 - x if r & 4 else x, 1 - y if r & 2 else y, 1 - c if r & 1 else c) for r in range(1, NDEV)]
        barrier = pltpu.get_barrier_semaphore()
        for peer in peers:
            pl.semaphore_signal(barrier, inc=1, device_id=peer, device_id_type=MESH)
        pl.semaphore_wait(barrier, NDEV - 1)
        local = [pltpu.make_async_copy(ins[k].at[me], outs[k].at[me], loc_sems.at[k]) for k in range(n)]
        copies = [pltpu.make_async_remote_copy(
            src_ref=ins[k].at[_dev_index(*peer)], dst_ref=outs[k].at[me], send_sem=send_sems.at[r, k],
            recv_sem=recv_sems.at[r, k], device_id=peer, device_id_type=MESH)
            for r, peer in enumerate(peers) for k in range(n)]
        for cp in local + copies:
            cp.start()
        for r, peer in enumerate(peers):
            for k in range(n):
                pltpu.make_async_remote_copy(
                    src_ref=ins[k].at[me], dst_ref=outs[k].at[_dev_index(*peer)], send_sem=send_sems.at[r, k],
                    recv_sem=recv_sems.at[r, k], device_id=peer, device_id_type=MESH).wait_recv()
        for cp in copies:
            cp.wait_send()
        for cp in local:
            cp.wait()

    return pl.kernel(
        launch, out_type=[jax.ShapeDtypeStruct(a.shape, a.dtype) for a in partials],
        mesh=plsc.ScalarSubcoreMesh(axis_name="sequencer", num_cores=1), name="exchange_square",
        scratch_types=(pltpu.SemaphoreType.DMA((NDEV - 1, n)), pltpu.SemaphoreType.DMA((NDEV - 1, n)),
                       pltpu.SemaphoreType.DMA((n,))),
        compiler_params=pltpu.CompilerParams(collective_id=3),
    )(*partials)


def _exchange_small(small):
    def body(sm_ref, out_ref, send_sems, recv_sems):
        (x, y, c), _, _ = _place()
        me = _dev_index(x, y, c)
        peers = [(1 - x if r & 4 else x, 1 - y if r & 2 else y, 1 - c if r & 1 else c) for r in range(1, NDEV)]
        out_ref[me] = sm_ref[...]
        copies = [pltpu.make_async_remote_copy(
            src_ref=sm_ref, dst_ref=out_ref.at[me], send_sem=send_sems.at[r], recv_sem=recv_sems.at[r],
            device_id=peer, device_id_type=MESH) for r, peer in enumerate(peers)]
        for cp in copies:
            cp.start()
        for r, peer in enumerate(peers):
            pltpu.make_async_remote_copy(
                src_ref=sm_ref, dst_ref=out_ref.at[_dev_index(*peer)], send_sem=send_sems.at[r], recv_sem=recv_sems.at[r],
                device_id=peer, device_id_type=MESH).wait_recv()
        for cp in copies:
            cp.wait_send()

    vm = pl.BlockSpec(memory_space=pltpu.VMEM)
    return pl.pallas_call(
        body, out_shape=jax.ShapeDtypeStruct((NDEV,) + small.shape, F32), in_specs=[vm], out_specs=vm,
        scratch_shapes=[pltpu.SemaphoreType.DMA((NDEV - 1,)), pltpu.SemaphoreType.DMA((NDEV - 1,))],
        name="exchange_small",
    )(small)


def _inproj_fwd(xn, w_pairs, proj, chip, tb):
    t = xn.shape[0]
    nblk, _, nb = w_pairs.shape

    def body(chip_ref, xn_ref, w_ref, proj_in, proj_ref):
        proj_ref[...] = _mm(xn_ref[...], w_ref[0])

    def other(j, chip_ref):
        return j + (j >= chip_ref[0]).astype(jnp.int32)

    return pl.pallas_call(
        body,
        grid_spec=pltpu.PrefetchScalarGridSpec(
            num_scalar_prefetch=1, grid=(t // tb, nblk - 1),
            in_specs=[pl.BlockSpec((tb, D), lambda i, j, chip_ref: (i, 0)),
                      pl.BlockSpec((1, D, nb), lambda i, j, chip_ref: (other(j, chip_ref), 0, 0)),
                      pl.BlockSpec(memory_space=pl.ANY)],
            out_specs=pl.BlockSpec((tb, nb), lambda i, j, chip_ref: (i, other(j, chip_ref)))),
        out_shape=jax.ShapeDtypeStruct((t, DIN), F32),
        input_output_aliases={3: 0},
        name="inproj_fwd", compiler_params=_cp(56),
    )(chip, xn, w_pairs, proj)


def _lower_bound(lb_ref):
    a0 = lb_ref[0:1, :]
    a1 = lb_ref[1:2, :]
    mx = jnp.maximum(a0, a1)
    e0 = jnp.exp(a0 - mx)
    e1 = jnp.exp(a1 - mx)
    return e0 / (e0 + e1)


def _hgrn_chunk_fwd(hq, hf, lb, tril):
    sg = _sigmoid(hf)
    f = lb + (1.0 - lb) * sg
    g = jnp.log(f)
    k = 1.0 - f
    sq = _sigmoid(hq)
    q = hq * sq
    b = _mm_tri_exact(tril, g)
    last_row = lax.broadcasted_iota(jnp.int32, b.shape, 0) == CH - 1
    b_last = jnp.sum(jnp.where(last_row, b, 0.0), axis=0, keepdims=True)
    c = 0.5 * b_last
    eb = jnp.exp(b)
    ea = jnp.exp(b - c)
    ek = jnp.exp(c - b)
    ed = jnp.exp(b_last - b)
    ebl = jnp.exp(b_last)
    return dict(sg=sg, f=f, k=k, sq=sq, q=q, eb=eb, ea=ea, ek=ek, ed=ed, ebl=ebl,
                qe=q * eb, qa=q * ea, ka=k * ek, kd=k * ed)


def _tri(lower):
    r = lax.broadcasted_iota(jnp.int32, (CH, CH), 0)
    c = lax.broadcasted_iota(jnp.int32, (CH, CH), 1)
    return (r >= c) if lower else (c >= r)


def _head_segment(p_ref, rows, j, hg):
    return p_ref[rows, j * HD * hg:(j + 1) * HD * hg]


def _head(a, k):
    return a[:, k * HD:(k + 1) * HD]


def _hgrn_fwd(proj, lbw, rb, hg):
    assert hg == HEADS
    t = proj.shape[0]
    ncb = rb // CH

    def body(p_ref, lb_ref, o_ref, st_ref, s_scr):
        @pl.when(pl.program_id(1) == 0)
        def _():
            s_scr[...] = jnp.zeros_like(s_scr)

        lb = _lower_bound(lb_ref)
        causal = _tri(True)
        tril = _tri3(True)
        heads = range(hg)

        def chunk(cc, carry):
            r0 = pl.multiple_of(cc * CH, CH)
            rows = pl.ds(r0, CH)
            e = _hgrn_chunk_fwd(_head_segment(p_ref, rows, 0, hg), _head_segment(p_ref, rows, 1, hg), lb, tril)
            v = _bf(_head_segment(p_ref, rows, 2, hg))
            sts = [s_scr[k] for k in heads]
            qa, ka, qe, kd = _bf(e["qa"]), _bf(e["ka"]), _bf(e["qe"]), _bf(e["kd"])
            a = [_bf(jnp.where(causal, _mm_nt(_head(qa, k), _head(ka, k)), 0.0)) for k in heads]
            o_inter = [_mm_nt(_head(qe, k), _bf(sts[k])) for k in heads]
            kv = [_mm_tn(_head(v, k), _head(kd, k)) for k in heads]
            o_intra = [_mm(a[k], _head(v, k)) for k in heads]
            for k in heads:
                st_ref[cc, k] = sts[k]
                o_ref[rows, k * HD:(k + 1) * HD] = o_inter[k] + o_intra[k]
                s_scr[k] = sts[k] * _head(e["ebl"], k) + kv[k]
            return carry

        lax.fori_loop(0, ncb, chunk, 0, unroll=2)

    return pl.pallas_call(
        body, grid=(HEADS // hg, t // rb),
        in_specs=[pl.BlockSpec((rb, 3 * HD * hg), lambda h, i: (i, h)), pl.BlockSpec((2, HD * hg), lambda h, i: (0, h))],
        out_specs=[pl.BlockSpec((rb, HD * hg), lambda h, i: (i, h)),
                   pl.BlockSpec((ncb, hg, HD, HD), lambda h, i: (i, h, 0, 0))],
        out_shape=[jax.ShapeDtypeStruct((t, D), F32), jax.ShapeDtypeStruct((t // CH, HEADS, HD, HD), F32)],
        scratch_shapes=[pltpu.VMEM((hg, HD, HD), F32)],
        name="hgrn_fwd", compiler_params=_cp(48),
    )(proj, lbw)


def _hgrn_bwd(proj, lbw, do_raw, states, rb, hg):
    assert hg == HEADS
    t = proj.shape[0]
    nblk = t // rb
    ncb = rb // CH
    wd = HD * hg

    def body(p_ref, lb_ref, do_ref, st_ref, dp_ref, dlb_ref, ds_scr):
        @pl.when(pl.program_id(1) == 0)
        def _():
            ds_scr[...] = jnp.zeros_like(ds_scr)
            dlb_ref[...] = jnp.zeros_like(dlb_ref)

        lb = _lower_bound(lb_ref)
        causal = _tri(True)
        tril = _tri3(True)
        triu = _tri3(False)
        last_row = lax.broadcasted_iota(jnp.int32, (CH, HD * hg), 0) == CH - 1
        row0 = lax.broadcasted_iota(jnp.int32, (8, HD * hg), 0) == 0
        heads = range(hg)
        wide = lambda parts: jnp.concatenate(parts, axis=1)

        def chunk(it, carry):
            cc = ncb - 1 - it
            r0 = pl.multiple_of(cc * CH, CH)
            rows = pl.ds(r0, CH)
            hq = _head_segment(p_ref, rows, 0, hg)
            e = _hgrn_chunk_fwd(hq, _head_segment(p_ref, rows, 1, hg), lb, tril)
            v = _bf(_head_segment(p_ref, rows, 2, hg))
            do = _bf(do_ref[rows, :])
            sts = [st_ref[cc, k] for k in heads]
            dsts = [ds_scr[k] for k in heads]
            dlb_acc = dlb_ref[...]
            qa, ka, qe, kd = _bf(e["qa"]), _bf(e["ka"]), _bf(e["qe"]), _bf(e["kd"])
            a = [_bf(jnp.where(causal, _mm_nt(_head(qa, k), _head(ka, k)), 0.0)) for k in heads]
            da = [_bf(jnp.where(causal, _mm_nt(_head(do, k), _head(v, k)), 0.0)) for k in heads]
            dqe = wide([_mm(_head(do, k), _bf(sts[k])) for k in heads])
            dkd = wide([_mm(_head(v, k), _bf(dsts[k])) for k in heads])
            dv_state = [_mm_nt(_head(kd, k), _bf(dsts[k])) for k in heads]
            ds_new = [_mm_tn(_head(do, k), _head(qe, k)) for k in heads]
            dv_intra = [_mm_tn(a[k], _head(do, k)) for k in heads]
            dqa = wide([_mm(da[k], _head(ka, k)) for k in heads])
            dka = wide([_mm_tn(da[k], _head(qa, k)) for k in heads])
            dv = wide([dv_intra[k] + dv_state[k] for k in heads])
            dbl = e["ebl"] * wide([jnp.sum(sts[k] * dsts[k], axis=0, keepdims=True) for k in heads])
            dq = dqe * e["eb"] + dqa * e["ea"]
            dk = dka * e["ek"] + dkd * e["ed"]
            dkd_kd = dkd * kd.astype(F32)
            db = dqe * qe.astype(F32) + dqa * qa.astype(F32) - dka * ka.astype(F32) - dkd_kd
            db = db + jnp.where(last_row, dbl + jnp.sum(dkd_kd, axis=0, keepdims=True), 0.0)
            dg = _mm_tri_exact(triu, db)
            df = dg / e["f"] - dk
            sg = e["sg"]
            sq = e["sq"]
            dhq = _bf(dq * (sq * (1.0 + hq * (1.0 - sq))))
            dhf = _bf(df * (1.0 - lb) * sg * (1.0 - sg))
            dhi = _bf(dv)
            dlb_new = dlb_acc + jnp.where(row0, jnp.sum(df * (1.0 - sg), axis=0, keepdims=True), 0.0)
            for k in heads:
                ds_scr[k] = ds_new[k] + dsts[k] * _head(e["ebl"], k)
            dp_ref[rows, 0:wd] = dhq
            dp_ref[rows, wd:2 * wd] = dhf
            dp_ref[rows, 2 * wd:3 * wd] = dhi
            dlb_ref[...] = dlb_new
            return carry

        lax.fori_loop(0, ncb, chunk, 0, unroll=2)

    rev = lambda h, i: (nblk - 1 - i, h)
    return pl.pallas_call(
        body, grid=(HEADS // hg, nblk),
        in_specs=[pl.BlockSpec((rb, 3 * HD * hg), rev), pl.BlockSpec((2, HD * hg), lambda h, i: (0, h)),
                  pl.BlockSpec((rb, HD * hg), rev), pl.BlockSpec((ncb, hg, HD, HD), lambda h, i: (nblk - 1 - i, h, 0, 0))],
        out_specs=[pl.BlockSpec((rb, 3 * HD * hg), rev), pl.BlockSpec((8, HD * hg), lambda h, i: (0, h))],
        out_shape=[jax.ShapeDtypeStruct((t, 3 * D), BF16), jax.ShapeDtypeStruct((8, D), F32)],
        scratch_shapes=[pltpu.VMEM((hg, HD, HD), F32)],
        name="hgrn_bwd", compiler_params=_cp(48),
    )(proj, lbw, do_raw, states)


def _kv_variants(tile, odd):
    low = lax.broadcasted_iota(jnp.int32, tile.shape, 1) < 64
    if odd:
        hi = jnp.where(low, 0.0, tile)
        lo = pltpu.roll(hi, 64, 1)
    else:
        lo = jnp.where(low, tile, 0.0)
        hi = pltpu.roll(lo, 64, 1)
    return _bf(lo), _bf(hi)


def _attn_masks(n):
    qi = lax.broadcasted_iota(jnp.int32, (AB, AB), 0)
    kj = lax.broadcasted_iota(jnp.int32, (AB, AB), 1)
    cur = kj <= qi
    return cur, cur | (n > 0), qi <= kj


def _kv_all(prev_ref, cur_ref):
    out = []
    for tl in range(2):
        cols = slice(tl * 128, (tl + 1) * 128)
        tile = jnp.concatenate([prev_ref[:, cols], cur_ref[:, cols]], axis=0)
        out.append(_kv_variants(tile, 0))
        out.append(_kv_variants(tile, 1))
    return out


def _window(a2, cur):
    return jnp.where(cur, a2[:, AB:], a2[:, :AB])


def _attn_softmax(scores, sinks, cur, ok):
    s = [jnp.where(ok, _window(s2, cur) * ATT_SCALE, NEG) for s2 in scores]
    m = [jnp.maximum(jnp.max(si, axis=-1, keepdims=True), sink) for si, sink in zip(s, sinks)]
    p = [jnp.exp(si - mi) for si, mi in zip(s, m)]
    es = [jnp.exp(sink - mi) for sink, mi in zip(sinks, m)]
    inv = [1.0 / (jnp.sum(pi, axis=-1, keepdims=True) + ei) for pi, ei in zip(p, es)]
    return [pi * ii for pi, ii in zip(p, inv)], [ei * ii for ei, ii in zip(es, inv)]


def _spread(pc, cur):
    return jnp.concatenate([jnp.where(cur, 0.0, pc), jnp.where(cur, pc, 0.0)], axis=1)


def _spread_t(pct, cur_t):
    return jnp.concatenate([jnp.where(cur_t, 0.0, pct), jnp.where(cur_t, pct, 0.0)], axis=0)


def _attn_fwd(proj, sinks):
    t = proj.shape[0]
    nb = t // AB

    def body(q_ref, kc_ref, kp_ref, vc_ref, vp_ref, sink_ref, o_ref):
        cur, ok, _ = _attn_masks(pl.program_id(0))
        kvars = _kv_all(kp_ref, kc_ref)
        vvars = _kv_all(vp_ref, vc_ref)
        qps = [_bf(q_ref[:, 128 * j:128 * (j + 1)]) for j in range(8)]
        heads = [(j, ab) for j in range(8) for ab in range(2)]
        scores = [_mm_nt(qps[j], kvars[j // 2][ab]) for j, ab in heads]
        pcs, _ = _attn_softmax(scores, [sink_ref[0, h] for h in range(QH)], cur, ok)
        parts = [_mm(_bf(_spread(pcs[h], cur)), vvars[j // 2][ab]) for h, (j, ab) in enumerate(heads)]
        for j in range(8):
            o_ref[:, 128 * j:128 * (j + 1)] = parts[2 * j] + parts[2 * j + 1]

    prev = lambda n: jnp.maximum(n - 1, 0)
    return pl.pallas_call(
        body, grid=(nb,),
        in_specs=[pl.BlockSpec((AB, D), lambda n: (n, C_AQ // D)),
                  pl.BlockSpec((AB, 256), lambda n: (n, C_AK // 256)),
                  pl.BlockSpec((AB, 256), lambda n: (prev(n), C_AK // 256)),
                  pl.BlockSpec((AB, 256), lambda n: (n, C_AV // 256)),
                  pl.BlockSpec((AB, 256), lambda n: (prev(n), C_AV // 256)),
                  pl.BlockSpec(memory_space=pltpu.SMEM)],
        out_specs=pl.BlockSpec((AB, D), lambda n: (n, 0)),
        out_shape=jax.ShapeDtypeStruct((t, D), F32),
        name="attn_fwd", compiler_params=_cp(32),
    )(proj, proj, proj, proj, proj, sinks)


def _attn_bwd(proj, sinks, do_a):
    t = proj.shape[0]
    nb = t // AB

    def body(q_ref, kc_ref, kp_ref, vc_ref, vp_ref, do_ref, sink_ref, dq_ref, dkv_ref, dsink_ref, carry):
        n = pl.program_id(0)

        @pl.when(n == 0)
        def _():
            dsink_ref[...] = jnp.zeros_like(dsink_ref)
            carry[...] = jnp.zeros_like(carry)

        @pl.when(n < nb)
        def _():
            cur, ok, cur_t = _attn_masks(n)
            low = lax.broadcasted_iota(jnp.int32, (2 * AB, 128), 1) < 64
            lane = lax.broadcasted_iota(jnp.int32, (8, 128), 1)
            row0 = lax.broadcasted_iota(jnp.int32, (8, 128), 0) == 0
            kvars = _kv_all(kp_ref, kc_ref)
            vvars = _kv_all(vp_ref, vc_ref)
            qps = [_bf(q_ref[:, 128 * j:128 * (j + 1)]) for j in range(8)]
            dops = [_bf(do_ref[:, 128 * j:128 * (j + 1)]) for j in range(8)]
            heads = [(j, ab) for j in range(8) for ab in range(2)]
            scores = [_mm_nt(qps[j], kvars[j // 2][ab]) for j, ab in heads]
            dps = [_mm_nt(dops[j], vvars[j // 2][ab]) for j, ab in heads]
            pcs, pss = _attn_softmax(scores, [sink_ref[0, h] for h in range(QH)], cur, ok)
            dpcs = [_window(dp2, cur) for dp2 in dps]
            rss = [jnp.sum(pc * dpc, axis=-1, keepdims=True) for pc, dpc in zip(pcs, dpcs)]
            dscs = [pc * (dpc - rs) for pc, dpc, rs in zip(pcs, dpcs, rss)]
            dsink = jnp.zeros((8, 128), F32)
            for h in range(QH):
                dsink = dsink + jnp.where(row0 & (lane == h), -jnp.sum(pss[h] * rss[h]), 0.0)
            dq_terms = [_mm(_bf(_spread(dscs[h], cur)), kvars[j // 2][ab]) for h, (j, ab) in enumerate(heads)]
            for j in range(8):
                dq_ref[:, 128 * j:128 * (j + 1)] = _bf((dq_terms[2 * j] + dq_terms[2 * j + 1]) * ATT_SCALE)
            dsc_t = [_bf(_spread_t(dsc.T, cur_t)) for dsc in dscs]
            pc_t = [_bf(_spread_t(pc.T, cur_t)) for pc in pcs]
            dk_terms = [_mm(dsc_t[h], qps[j]) for h, (j, ab) in enumerate(heads)]
            dv_terms = [_mm(pc_t[h], dops[j]) for h, (j, ab) in enumerate(heads)]
            dk_ab = [[dk_terms[4 * g + ab] + dk_terms[4 * g + 2 + ab] for ab in range(2)] for g in range(4)]
            dv_ab = [[dv_terms[4 * g + ab] + dv_terms[4 * g + 2 + ab] for ab in range(2)] for g in range(4)]
            for tl in range(2):
                ke, ko = dk_ab[2 * tl], dk_ab[2 * tl + 1]
                ve, vo = dv_ab[2 * tl], dv_ab[2 * tl + 1]
                dkt = (jnp.where(low, ke[0], 0.0) + pltpu.roll(jnp.where(low, 0.0, ke[1]), 64, 1)
                       + jnp.where(low, 0.0, ko[1]) + pltpu.roll(jnp.where(low, ko[0], 0.0), 64, 1)) * ATT_SCALE
                dvt = (jnp.where(low, ve[0], 0.0) + pltpu.roll(jnp.where(low, 0.0, ve[1]), 64, 1)
                       + jnp.where(low, 0.0, vo[1]) + pltpu.roll(jnp.where(low, vo[0], 0.0), 64, 1))
                kcols = slice(tl * 128, (tl + 1) * 128)
                vcols = slice(256 + tl * 128, 256 + (tl + 1) * 128)
                dkv_ref[:, kcols] = _bf(carry[:, kcols] + dkt[0:AB])
                dkv_ref[:, vcols] = _bf(carry[:, vcols] + dvt[0:AB])
                carry[:, kcols] = dkt[AB:2 * AB]
                carry[:, vcols] = dvt[AB:2 * AB]
            dsink_ref[...] += dsink

        @pl.when(n == nb)
        def _():
            dkv_ref[...] = _bf(carry[...])

    cur = lambda n: jnp.minimum(n, nb - 1)
    prev = lambda n: jnp.clip(n - 1, 0, nb - 1)
    return pl.pallas_call(
        body, grid=(nb + 1,),
        in_specs=[pl.BlockSpec((AB, D), lambda n: (cur(n), C_AQ // D)),
                  pl.BlockSpec((AB, 256), lambda n: (cur(n), C_AK // 256)),
                  pl.BlockSpec((AB, 256), lambda n: (prev(n), C_AK // 256)),
                  pl.BlockSpec((AB, 256), lambda n: (cur(n), C_AV // 256)),
                  pl.BlockSpec((AB, 256), lambda n: (prev(n), C_AV // 256)),
                  pl.BlockSpec((AB, D), lambda n: (cur(n), 0)),
                  pl.BlockSpec(memory_space=pltpu.SMEM)],
        out_specs=[pl.BlockSpec((AB, D), lambda n: (cur(n), 0)),
                   pl.BlockSpec((AB, 512), lambda n: (prev(n), 0)),
                   pl.BlockSpec((8, 128), lambda n: (0, 0))],
        out_shape=[jax.ShapeDtypeStruct((t, D), BF16), jax.ShapeDtypeStruct((t, 512), BF16),
                   jax.ShapeDtypeStruct((8, 128), F32)],
        scratch_shapes=[pltpu.VMEM((AB, 512), F32)],
        name="attn_bwd", compiler_params=_cp(32),
    )(proj, proj, proj, proj, proj, do_a, sinks)


def _silu_and_grad(v):
    s = _sigmoid(v)
    return v * s, s * (1.0 + v * (1.0 - s))


def _tail(o_raw, o_a, proj, x2d, tgt, wbh, wba, wout, hnw, fnw, tb):
    t = x2d.shape[0]

    def body(or_ref, oa_ref, hg_ref, ag0, ag1, mh0, mh1, ma0, ma1, x_ref, t_ref, wbh_ref, wba_ref, wout_ref, hnw_ref,
             fnw_ref, dx2_ref, dor_ref, doa_ref, dhg_ref, dagm_ref, gh_ref, ga_ref, mg_ref, dyh_ref, dya_ref, dx2b_ref,
             sums_ref):
        @pl.when(pl.program_id(0) == 0)
        def _():
            sums_ref[...] = jnp.zeros_like(sums_ref)

        halves = lambda a, b: jnp.concatenate([a[...], b[...]], axis=1)
        hnw_v = hnw_ref[...]
        fnw_v = fnw_ref[...]
        o = or_ref[...]
        rs, xhs = [], []
        for h in range(HEADS):
            oh = o[:, h * HD:(h + 1) * HD]
            r = lax.rsqrt(jnp.mean(oh * oh, axis=-1, keepdims=True) + EPS)
            rs.append(r)
            xhs.append(oh * r)
        xh = jnp.concatenate(xhs, axis=1)
        on = xh * hnw_v
        sil_hg, dsil_hg = _silu_and_grad(hg_ref[...])
        gh_b = _bf(on * sil_hg)
        y_h = _mm(gh_b, wbh_ref[...])
        oa = oa_ref[...]
        sil_ag, dsil_ag = _silu_and_grad(halves(ag0, ag1))
        ga_b = _bf(oa * sil_ag)
        y_a = _mm(ga_b, wba_ref[...])
        s_mh = _sigmoid(halves(mh0, mh1))
        s_ma = _sigmoid(halves(ma0, ma1))
        mg_b = _bf(s_mh * y_h + s_ma * y_a)
        x2 = x_ref[...] + _mm(mg_b, wout_ref[...])
        r2 = lax.rsqrt(jnp.mean(x2 * x2, axis=-1, keepdims=True) + EPS)
        xh2 = x2 * r2
        err = xh2 * fnw_v - t_ref[...]
        loss = 0.5 * jnp.sum(jnp.mean(err * err, axis=-1, keepdims=True))
        dy = err * (1.0 / D)
        dfnw = jnp.sum(dy * xh2, axis=0, keepdims=True)
        dxh2 = dy * fnw_v
        dx2 = r2 * (dxh2 - xh2 * jnp.mean(dxh2 * xh2, axis=-1, keepdims=True))
        dx2_ref[...] = dx2
        dx2_b = _bf(dx2)
        dmg = _mm_nt(dx2_b, wout_ref[...])
        dmg_h = dmg * s_mh
        dmg_a = dmg * s_ma
        dyh_b = _bf(dmg_h)
        dya_b = _bf(dmg_a)
        dagm_ref[:, D:2 * D] = _bf(dmg_h * y_h * (1.0 - s_mh))
        dagm_ref[:, 2 * D:3 * D] = _bf(dmg_a * y_a * (1.0 - s_ma))
        dgh = _mm_nt(dyh_b, wbh_ref[...])
        dga = _mm_nt(dya_b, wba_ref[...])
        doa_ref[...] = dga * sil_ag
        dagm_ref[:, 0:D] = _bf(dga * oa * dsil_ag)
        dhg_ref[...] = _bf(dgh * on * dsil_hg)
        don = dgh * sil_hg
        dhnw = jnp.sum(don * xh, axis=0, keepdims=True)
        dxh = don * hnw_v
        dos = []
        for h in range(HEADS):
            sl = slice(h * HD, (h + 1) * HD)
            dos.append(rs[h] * (dxh[:, sl] - xhs[h] * jnp.mean(dxh[:, sl] * xhs[h], axis=-1, keepdims=True)))
        dor_ref[...] = jnp.concatenate(dos, axis=1)
        gh_ref[...] = gh_b
        ga_ref[...] = ga_b
        mg_ref[...] = mg_b
        dyh_ref[...] = dyh_b
        dya_ref[...] = dya_b
        dx2b_ref[...] = dx2_b
        row = lax.broadcasted_iota(jnp.int32, (8, D), 0)
        sums_ref[...] += jnp.where(row == 0, dfnw, 0.0) + jnp.where(row == 1, dhnw, 0.0) + jnp.where(row == 2, loss, 0.0)

    rowblk = lambda c: pl.BlockSpec((tb, D), lambda i: (i, c))
    half = lambda c: pl.BlockSpec((tb, 512), lambda i: (i, c))
    full = lambda shape: pl.BlockSpec(shape, lambda i: (0, 0))
    return pl.pallas_call(
        body, grid=(t // tb,),
        in_specs=[rowblk(0), rowblk(0), rowblk(C_HG // D), half(C_AG // 512), half(C_AG // 512 + 1), half(C_MH // 512),
                  half(C_MH // 512 + 1), half(C_MA // 512), half(C_MA // 512 + 1), rowblk(0), rowblk(0),
                  full((D, D)), full((D, D)), full((D, D)), full((1, D)), full((1, D))],
        out_specs=[rowblk(0), rowblk(0), rowblk(0), rowblk(0), pl.BlockSpec((tb, 3 * D), lambda i: (i, 0))]
        + [rowblk(0)] * 6 + [full((8, D))],
        out_shape=[jax.ShapeDtypeStruct((t, D), F32)] * 3
        + [jax.ShapeDtypeStruct((t, D), BF16), jax.ShapeDtypeStruct((t, 3 * D), BF16)]
        + [jax.ShapeDtypeStruct((t, D), BF16)] * 6 + [jax.ShapeDtypeStruct((8, D), F32)],
        name="tail", compiler_params=_cp(56),
    )(o_raw, o_a, proj, proj, proj, proj, proj, proj, proj, x2d, tgt, wbh, wba, wout, hnw, fnw)


def _wgrad3(gh, dyh, ga, dya, mg, dx2b, tk):
    t = dyh.shape[0]

    def body(a0, b0, a1, b1, a2, b2, o0, o1, o2):
        @pl.when(pl.program_id(0) == 0)
        def _():
            o0[...] = jnp.zeros_like(o0)
            o1[...] = jnp.zeros_like(o1)
            o2[...] = jnp.zeros_like(o2)

        o0[...] += _mm_tn(a0[...], b0[...])
        o1[...] += _mm_tn(a1[...], b1[...])
        o2[...] += _mm_tn(a2[...], b2[...])

    blk = pl.BlockSpec((tk, D), lambda k: (k, 0))
    out = pl.BlockSpec((D, D), lambda k: (0, 0))
    return pl.pallas_call(
        body, grid=(t // tk,), in_specs=[blk] * 6, out_specs=[out] * 3,
        out_shape=[jax.ShapeDtypeStruct((D, D), F32)] * 3,
        name="wgrad3", compiler_params=_cp(48),
    )(gh, dyh, ga, dya, mg, dx2b)


def _inproj_wgrad_piece(xnt, piece, nb, name):
    t = xnt.shape[1]
    width = piece.shape[1]

    def body(xnt_ref, p_ref, o_ref):
        o_ref[...] = _mm(xnt_ref[...], p_ref[...])

    return pl.pallas_call(
        body, grid=(width // nb,),
        in_specs=[pl.BlockSpec((D, t), lambda j: (0, 0), pipeline_mode=pl.Buffered(1)),
                  pl.BlockSpec((t, nb), lambda j: (0, j))],
        out_specs=pl.BlockSpec((D, nb), lambda j: (0, j)),
        out_shape=jax.ShapeDtypeStruct((D, width), F32),
        name=name, compiler_params=_cp(56),
    )(xnt, piece)


def _inproj_dgrad(pieces, w_p, x2d, dx2, norm_w, tb, after):
    t = x2d.shape[0]

    def body(*refs):
        piece_refs = refs[:len(pieces)]
        w_ref, x_ref, dx2_ref, nw_ref, _, gx_ref, dnw_ref = refs[len(pieces):]

        @pl.when(pl.program_id(0) == 0)
        def _():
            dnw_ref[...] = jnp.zeros_like(dnw_ref)

        dxn = None
        off = 0
        for p in piece_refs:
            width = p.shape[1]
            for q in range(w_ref.shape[0]):
                lo, hi = max(off, q * PAIR), min(off + width, (q + 1) * PAIR)
                if lo < hi:
                    term = _mm_nt(p[:, lo - off:hi - off], w_ref[q, :, lo - q * PAIR:hi - q * PAIR])
                    dxn = term if dxn is None else dxn + term
            off += width
        xv = x_ref[...]
        r = lax.rsqrt(jnp.mean(xv * xv, axis=-1, keepdims=True) + EPS)
        xh = xv * r
        dxh = dxn * nw_ref[...]
        gx_ref[...] = dx2_ref[...] + r * (dxh - xh * jnp.mean(dxh * xh, axis=-1, keepdims=True))
        row0 = lax.broadcasted_iota(jnp.int32, (8, D), 0) == 0
        dnw_ref[...] += jnp.where(row0, jnp.sum(dxn * xh, axis=0, keepdims=True), 0.0)

    rowblk = pl.BlockSpec((tb, D), lambda i: (i, 0))
    return pl.pallas_call(
        body, grid=(t // tb,),
        in_specs=[pl.BlockSpec((tb, p.shape[1]), lambda i: (i, 0)) for p in pieces]
        + [pl.BlockSpec(w_p.shape, lambda i: (0, 0, 0), pipeline_mode=pl.Buffered(1)), rowblk, rowblk,
           pl.BlockSpec((1, D), lambda i: (0, 0)), pl.BlockSpec(memory_space=pl.ANY)],
        out_specs=[rowblk, pl.BlockSpec((8, D), lambda i: (0, 0))],
        out_shape=[jax.ShapeDtypeStruct((t, D), F32), jax.ShapeDtypeStruct((8, D), F32)],
        name="inproj_dgrad", compiler_params=_cp(60),
    )(*pieces, w_p, x2d, dx2, norm_w, after)


def _adamw_math(w, g, m, v):
    m = B1 * m + (1.0 - B1) * g
    v = B2 * v + (1.0 - B2) * (g * g)
    m_hat = m / (1.0 - B1 ** STEP)
    v_hat = v / (1.0 - B2 ** STEP)
    delta = -LR * (m_hat / (jnp.sqrt(v_hat) + ADAM_EPS) + WD * w)
    return delta, m, v


def _adamw_shard(recv, sums, chip, w, m, v, rows, name):
    nparts, nr, nc = recv.shape

    def body(chip_ref, own_ref, p_ref, w_ref, m_ref, v_ref, g_ref, d_ref, nm_ref, nv_ref):
        g = own_ref[0].astype(F32)
        for s in range(nparts):
            g = g + p_ref[s].astype(F32)
        d, nm, nv = _adamw_math(w_ref[...], g, m_ref[...], v_ref[...])
        g_ref[...] = g
        d_ref[...] = d
        nm_ref[...] = nm
        nv_ref[...] = nv

    blk = pl.BlockSpec((rows, nc), lambda i, chip_ref: (i, 0))
    return pl.pallas_call(
        body,
        grid_spec=pltpu.PrefetchScalarGridSpec(
            num_scalar_prefetch=1, grid=(nr // rows,),
            in_specs=[pl.BlockSpec((1, rows, nc), lambda i, chip_ref: (chip_ref[0], i, 0)),
                      pl.BlockSpec((nparts, rows, nc), lambda i, chip_ref: (0, i, 0)), blk, blk, blk],
            out_specs=[blk] * 4),
        out_shape=[jax.ShapeDtypeStruct((nr, nc), F32)] * 4,
        name=name, compiler_params=_cp(48),
    )(chip, sums, recv, w, m, v)


def _adamw_sum8(parts, w, m, v, name):
    def body(p_ref, w_ref, m_ref, v_ref, g_ref, d_ref, nm_ref, nv_ref):
        g = p_ref[0].astype(F32)
        for s in range(1, NDEV):
            g = g + p_ref[s].astype(F32)
        d, nm, nv = _adamw_math(w_ref[...], g, m_ref[...], v_ref[...])
        g_ref[...] = g
        d_ref[...] = d
        nm_ref[...] = nm
        nv_ref[...] = nv

    return pl.pallas_call(body, out_shape=[jax.ShapeDtypeStruct(w.shape, F32)] * 4, name=name)(parts, w, m, v)


SMALL_ROWS = dict(norm_w=0, lower_bound=1, hgrn_norm_w=3, final_norm_w=4, sinks=5, loss=6)


def _pack_small_grads(dnw, dlb, sums, dsink):
    def body(dnw_ref, dlb_ref, sums_ref, dsink_ref, o_ref):
        o_ref[...] = jnp.zeros_like(o_ref)
        o_ref[0:1, :] = dnw_ref[0:1, :]
        o_ref[1:2, :] = dlb_ref[0:1, :]
        o_ref[3:4, :] = sums_ref[1:2, :]
        o_ref[4:5, :] = sums_ref[0:1, :]
        o_ref[5:6, 0:128] = dsink_ref[0:1, :]
        o_ref[6:7, :] = sums_ref[2:3, :]

    return pl.pallas_call(body, out_shape=jax.ShapeDtypeStruct((8, D), F32), name="pack_small_grads")(dnw, dlb, sums, dsink)


def _adamw_small(parts, ws, ms, vs):
    shapes = [a.shape for a in ws]

    def body(p_ref, *refs):
        w, m, v = refs[0:5], refs[5:10], refs[10:15]
        outs = [refs[15 + 5 * i:20 + 5 * i] for i in range(4)]
        loss_ref = refs[35]

        def total(row, width):
            g = p_ref[0, row:row + 1, 0:width]
            for s in range(1, NDEV):
                g = g + p_ref[s, row:row + 1, 0:width]
            return g

        loss_ref[...] = total(6, 128)
        lb = _lower_bound(w[1])
        ga0 = total(1, D) * lb * (1.0 - lb)
        grads = [total(0, D), None, total(3, D), total(4, D), total(5, QH)]
        for i in (0, 2, 3, 4):
            res = (grads[i],) + _adamw_math(w[i][...], grads[i], m[i][...], v[i][...])
            for o, val in zip(outs, res):
                o[i][...] = val
        for r, g in ((0, ga0), (1, -ga0)):
            res = (g,) + _adamw_math(w[1][r:r + 1, :], g, m[1][r:r + 1, :], v[1][r:r + 1, :])
            for o, val in zip(outs, res):
                o[1][r:r + 1, :] = val

    res = pl.pallas_call(
        body, out_shape=[jax.ShapeDtypeStruct(s, F32) for s in shapes] * 4 + [jax.ShapeDtypeStruct((1, 128), F32)],
        name="adamw_small",
    )(parts, *ws, *ms, *vs)
    return [res[5 * i:5 * i + 5] for i in range(4)], res[20][0, 0]


def kernel(x, norm_w, w_in, hgrn_lower_bound, hgrn_norm_w, w_branch_hgrn, attn_sinks, w_branch_attn, w_out, final_norm_w, loss_target, m_norm_w, m_w_in, m_hgrn_lower_bound, m_hgrn_norm_w, m_w_branch_hgrn, m_attn_sinks, m_w_branch_attn, m_w_out, m_final_norm_w, v_norm_w, v_w_in, v_hgrn_lower_bound, v_hgrn_norm_w, v_w_branch_hgrn, v_attn_sinks, v_w_branch_attn, v_w_out, v_final_norm_w):
    t = x.shape[1]
    x2d = x.reshape(t, D)
    tgt = loss_target.reshape(t, D)
    fnw = final_norm_w.reshape(1, D)
    row_blk = min(256, t)
    big_blk = min(512, t)

    chip = (2 * lax.axis_index("x") + lax.axis_index("y")).astype(jnp.int32).reshape(1)
    w_p, xn, xnt, proj_own = _gather_in_projection(w_in[0], x2d, norm_w)
    wbh, wba, wout = (g.reshape(D, D) for g in _gather_square(
        [w_branch_hgrn[0].astype(BF16), w_branch_attn[0].astype(BF16), w_out[0].astype(BF16)], after=w_p))

    proj = _inproj_fwd(xn, w_p, proj_own, chip, min(1024, t))
    o_raw, states = _hgrn_fwd(proj, hgrn_lower_bound, big_blk, HGRN_GROUP)
    o_a = _attn_fwd(proj, attn_sinks)
    (dx2, do_raw, do_a, d_hg, d_agm, gh, ga, mg, dyh, dya, dx2b, sums) = _tail(
        o_raw, o_a, proj, x2d, tgt, wbh, wba, wout, hgrn_norm_w, fnw, row_blk)
    dwbh, dwba, dwout = _wgrad3(gh, dyh, ga, dya, mg, dx2b, big_blk)
    d_aq, d_kv, dsink = _attn_bwd(proj, attn_sinks, do_a)
    d_hgrn, dlb = _hgrn_bwd(proj, hgrn_lower_bound, do_raw, states, big_blk, HGRN_GROUP)
    pieces = (d_hgrn, d_hg, d_aq, d_kv, d_agm)
    dw_pieces = [_inproj_wgrad_piece(xnt, p, CB, "inproj_wgrad_" + n)
                 for p, n in zip(pieces, ("hgrn", "hgate", "aq", "kv", "gates"))]

    dwin_r = jnp.concatenate(dw_pieces, axis=1).reshape(D, NDEV, IN_SHARD).transpose(1, 0, 2).astype(BF16)
    slots = lambda a: a.reshape(NDEV, ROW_SHARD, D).astype(BF16)
    got = _exchange_pair([dwin_r])
    core = lax.axis_index("c").astype(jnp.int32).reshape(1)
    s_in = _pair_sum(dwin_r, got[0], core, ROW_SHARD, "pair_sum_w_in")
    rin, = _exchange_chips([s_in])
    rbh, rba, rout = _exchange_square([slots(dwbh), slots(dwba), slots(dwout)])
    grad_x, dnw = _inproj_dgrad(pieces, w_p, x2d, dx2, norm_w, big_blk, after=s_in)
    rsm = _exchange_small(_pack_small_grads(dnw, dlb, sums, dsink))
    g_in, d_in, nm_in, nv_in = _adamw_shard(rin, s_in, chip, w_in[0], m_w_in[0], v_w_in[0], 128, "adamw_w_in")
    g_bh, d_bh, nm_bh, nv_bh = _adamw_sum8(rbh, w_branch_hgrn[0], m_w_branch_hgrn[0], v_w_branch_hgrn[0], "adamw_w_bh")
    g_ba, d_ba, nm_ba, nv_ba = _adamw_sum8(rba, w_branch_attn[0], m_w_branch_attn[0], v_w_branch_attn[0], "adamw_w_ba")
    g_out, d_out, nm_out, nv_out = _adamw_sum8(rout, w_out[0], m_w_out[0], v_w_out[0], "adamw_w_out")
    (sg, sd, sm, sv), loss = _adamw_small(
        rsm,
        (norm_w, hgrn_lower_bound, hgrn_norm_w, fnw, attn_sinks),
        (m_norm_w, m_hgrn_lower_bound, m_hgrn_norm_w, m_final_norm_w.reshape(1, D), m_attn_sinks),
        (v_norm_w, v_hgrn_lower_bound, v_hgrn_norm_w, v_final_norm_w.reshape(1, D), v_attn_sinks))

    def group(s, w_in_v, bh, ba, out):
        nw, lb, hnw, fn, sinks = s
        return (nw, w_in_v[None], lb, hnw, bh[None], sinks, ba[None], out[None], fn.reshape(D))

    return (loss, grad_x.reshape(1, t, D),
            *group(sg, g_in, g_bh, g_ba, g_out), *group(sd, d_in, d_bh, d_ba, d_out),
            *group(sm, nm_in, nm_bh, nm_ba, nm_out), *group(sv, nv_in, nv_bh, nv_ba, nv_out))
```

```python
import functools

import jax
import jax.numpy as jnp
from jax import lax
from jax.experimental import pallas as pl
from jax.experimental.pallas import tpu as pltpu
from jax.experimental.pallas import tpu_sc as plsc

F32 = jnp.float32
BF16 = jnp.bfloat16

D = 1024
DIN = 8704
NDEV = 8
IN_SHARD = DIN // NDEV
PAIR = 2 * IN_SHARD
ROW_SHARD = D // NDEV
HEADS = 8
HD = 128
CH = 64
HGRN_GROUP = 8
QH = 16
AB = 128
EPS = 1e-6
NEG = -1e30
ATT_SCALE = 0.125

C_HGRN = 0
C_HG = 3072
C_AQ = 4096
C_AK = 5120
C_AV = 5376
C_AG = 5632
C_MH = 6656
C_MA = 7680
CB = 512

LR = 0.001
B1 = 0.9
B2 = 0.999
ADAM_EPS = 1e-08
WD = 0.01
STEP = 10

V7X_VMEM_BYTES = 64 * 1024 * 1024
MESH = pl.DeviceIdType.MESH


def _cp(vmem_mb):
    return pltpu.CompilerParams(vmem_limit_bytes=vmem_mb * 1024 * 1024)


def _mm(a, b):
    return jnp.dot(a, b, preferred_element_type=F32)


def _mm_nt(a, b):
    return lax.dot_general(a, b, (((1,), (1,)), ((), ())), preferred_element_type=F32)


def _mm_tn(a, b):
    return lax.dot_general(a, b, (((0,), (0,)), ((), ())), preferred_element_type=F32)


def _tri3(lower):
    r = lax.broadcasted_iota(jnp.int32, (CH, 3 * CH), 0)
    c = lax.broadcasted_iota(jnp.int32, (CH, 3 * CH), 1)
    c = jnp.where(c >= 2 * CH, c - 2 * CH, jnp.where(c >= CH, c - CH, c))
    return ((r >= c) if lower else (c >= r)).astype(BF16)


def _mm_tri_exact(tri3, g):
    g1 = g.astype(BF16)
    r1 = g - g1.astype(F32)
    g2 = r1.astype(BF16)
    g3 = (r1 - g2.astype(F32)).astype(BF16)
    return _mm(tri3, jnp.concatenate([g1, g2, g3], axis=0))


def _sigmoid(v):
    return 0.5 * jnp.tanh(0.5 * v) + 0.5


def _bf(v):
    return v.astype(BF16)


def _place():
    x, y, c = lax.axis_index("x"), lax.axis_index("y"), lax.axis_index("c")
    return (x, y, c), (x, y, 1 - c), [(1 - x, y), (x, 1 - y), (1 - x, 1 - y)]


def _dev_index(px, py, pc):
    return 4 * px + 2 * py + pc


def _gather_in_projection(w_in_s, x2d, norm_w):
    half = D // 2
    t = x2d.shape[0]
    prep_rows = min(512, t)
    nprep = t // prep_rows

    def body(win_ref, x_hbm, nw_ref, wp_g, xn_hbm, xnt_hbm, proj_hbm, give, take, mine, xbuf, xnbuf, xntbuf, w_own, pbuf,
             send_sems, recv_sems, loc_sem, swap_sems, in_sems, out_sems, own_sem):
        (x, y, c), sibling, chips = _place()
        give[...] = win_ref[pl.ds(pl.multiple_of(half * (1 - c), half), half), :].astype(BF16)
        swap = pltpu.make_async_remote_copy(src_ref=give, dst_ref=take, send_sem=swap_sems.at[0], recv_sem=swap_sems.at[1],
                                            device_id=sibling, device_id_type=MESH)
        swap.start()
        swap.wait()
        own = win_ref[pl.ds(pl.multiple_of(half * c, half), half), :]
        other = take[...].astype(F32)
        mine[...] = jnp.where(c == 0, jnp.concatenate([own, other], axis=1),
                              jnp.concatenate([other, own], axis=1)).astype(BF16)

        def place(px, py, pc):
            return wp_g.at[2 * px + py, pl.ds(pl.multiple_of(half * pc, half), half), :]

        def copy(kind, origin, to, src=mine):
            return pltpu.make_async_remote_copy(
                src_ref=src, dst_ref=place(*origin), send_sem=send_sems.at[kind], recv_sem=recv_sems.at[kind],
                device_id=to, device_id_type=MESH)

        me = (x, y, c)
        local = pltpu.make_async_copy(mine, place(*me), loc_sem)
        local.start()
        first = [copy(0, me, sibling)] + [copy(1 + j, me, (*chip, c)) for j, chip in enumerate(chips)]
        for cp in first:
            cp.start()

        copy(0, (x, y, 1 - c), me).wait_recv()
        local.wait()
        my_chip = 2 * x + y
        fetch = pltpu.make_async_copy(wp_g.at[my_chip], w_own, own_sem)
        fetch.start()

        def rows_of(i):
            return pl.ds(pl.multiple_of(i * prep_rows, prep_rows), prep_rows)

        def load(i, slot):
            return pltpu.make_async_copy(x_hbm.at[rows_of(i), :], xbuf.at[slot], in_sems.at[slot])

        def stores(i, slot):
            own_cols = pl.ds(pl.multiple_of(my_chip * PAIR, 128), PAIR)
            return (pltpu.make_async_copy(xnbuf.at[slot], xn_hbm.at[rows_of(i), :], out_sems.at[slot, 0]),
                    pltpu.make_async_copy(xntbuf.at[slot], xnt_hbm.at[:, rows_of(i)], out_sems.at[slot, 1]),
                    pltpu.make_async_copy(pbuf.at[slot], proj_hbm.at[rows_of(i), own_cols], out_sems.at[slot, 2]))

        load(0, 0).start()
        fetch.wait()

        def prep(i, carry):
            slot = lax.rem(i, 2)
            load(i, slot).wait()

            @pl.when(i + 1 < nprep)
            def _():
                load(i + 1, 1 - slot).start()

            @pl.when(i >= 2)
            def _():
                for cp in stores(i - 2, slot):
                    cp.wait()

            xv = xbuf[slot]
            xn = (xv * lax.rsqrt(jnp.mean(xv * xv, axis=-1, keepdims=True) + EPS)) * nw_ref[...]
            xn_b = xn.astype(BF16)
            xnbuf[slot] = xn_b
            xntbuf[slot] = xn.T.astype(BF16)
            pbuf[slot] = _mm(xn_b, w_own[...])
            for cp in stores(i, slot):
                cp.start()
            return carry

        lax.fori_loop(0, nprep, prep, 0)
        for i in range(max(nprep - 2, 0), nprep):
            for cp in stores(i, i % 2):
                cp.wait()

        passed = []
        for j, chip in enumerate(chips):
            copy(1 + j, (*chip, c), me).wait_recv()
            cp = copy(4 + j, (*chip, c), sibling, src=place(*chip, c))
            cp.start()
            passed.append(cp)
        for j, chip in enumerate(chips):
            copy(4 + j, (*chip, 1 - c), me).wait_recv()
        for cp in first + passed:
            cp.wait_send()

    vm = pl.BlockSpec(memory_space=pltpu.VMEM)
    hbm = pl.BlockSpec(memory_space=pl.ANY)
    return pl.pallas_call(
        body,
        out_shape=[jax.ShapeDtypeStruct((NDEV // 2, D, PAIR), BF16), jax.ShapeDtypeStruct((t, D), BF16),
                   jax.ShapeDtypeStruct((D, t), BF16), jax.ShapeDtypeStruct((t, DIN), F32)],
        in_specs=[vm, hbm, vm],
        out_specs=[hbm, hbm, hbm, hbm],
        scratch_shapes=[pltpu.VMEM((half, IN_SHARD), BF16), pltpu.VMEM((half, IN_SHARD), BF16),
                        pltpu.VMEM((half, PAIR), BF16),
                        pltpu.VMEM((2, prep_rows, D), F32), pltpu.VMEM((2, prep_rows, D), BF16),
                        pltpu.VMEM((2, D, prep_rows), BF16),
                        pltpu.VMEM((D, PAIR), BF16), pltpu.VMEM((2, prep_rows, PAIR), F32),
                        pltpu.SemaphoreType.DMA((NDEV - 1,)), pltpu.SemaphoreType.DMA((NDEV - 1,)),
                        pltpu.SemaphoreType.DMA, pltpu.SemaphoreType.DMA((2,)),
                        pltpu.SemaphoreType.DMA((2,)), pltpu.SemaphoreType.DMA((2, 3)), pltpu.SemaphoreType.DMA],
        name="gather_in_projection", compiler_params=_cp(56),
    )(w_in_s, x2d, norm_w)


def _gather_square(shards, after):
    n = len(shards)

    def launch(*refs):
        ins, outs = refs[:n], refs[n + 1:2 * n + 1]
        send_sems, recv_sems, loc_sems = refs[2 * n + 1:]
        (x, y, c), _, _ = _place()
        me = _dev_index(x, y, c)
        peers = [(1 - x if r & 4 else x, 1 - y if r & 2 else y, 1 - c if r & 1 else c) for r in range(1, NDEV)]
        barrier = pltpu.get_barrier_semaphore()
        for peer in peers:
            pl.semaphore_signal(barrier, inc=1, device_id=peer, device_id_type=MESH)
        pl.semaphore_wait(barrier, NDEV - 1)
        local = [pltpu.make_async_copy(ins[k], outs[k].at[me], loc_sems.at[k]) for k in range(n)]
        copies = [pltpu.make_async_remote_copy(
            src_ref=ins[k], dst_ref=outs[k].at[me], send_sem=send_sems.at[r, k], recv_sem=recv_sems.at[r, k],
            device_id=peer, device_id_type=MESH) for r, peer in enumerate(peers) for k in range(n)]
        for cp in local + copies:
            cp.start()
        for r, peer in enumerate(peers):
            for k in range(n):
                pltpu.make_async_remote_copy(
                    src_ref=ins[k], dst_ref=outs[k].at[_dev_index(*peer)], send_sem=send_sems.at[r, k],
                    recv_sem=recv_sems.at[r, k], device_id=peer, device_id_type=MESH).wait_recv()
        for cp in copies:
            cp.wait_send()
        for cp in local:
            cp.wait()

    return pl.kernel(
        launch, out_type=[jax.ShapeDtypeStruct((NDEV,) + a.shape, a.dtype) for a in shards],
        mesh=plsc.ScalarSubcoreMesh(axis_name="sequencer", num_cores=1), name="gather_square",
        scratch_types=(pltpu.SemaphoreType.DMA((NDEV - 1, n)), pltpu.SemaphoreType.DMA((NDEV - 1, n)),
                       pltpu.SemaphoreType.DMA((n,))),
        compiler_params=pltpu.CompilerParams(collective_id=2),
    )(*shards, after)


def _exchange_pair(arrs):
    n = len(arrs)

    def launch(*refs):
        ins, got = refs[:n], refs[n:2 * n]
        send_sems, recv_sems = refs[2 * n:]
        (x, y, c), sibling, _ = _place()
        barrier = pltpu.get_barrier_semaphore()
        pl.semaphore_signal(barrier, inc=1, device_id=sibling, device_id_type=MESH)
        pl.semaphore_wait(barrier, 1)
        sends = [pltpu.make_async_remote_copy(
            src_ref=ins[k].at[_dev_index(q // 2, q % 2, 1 - c)], dst_ref=got[k].at[q], send_sem=send_sems.at[q, k],
            recv_sem=recv_sems.at[q, k], device_id=sibling, device_id_type=MESH) for q in range(4) for k in range(n)]
        for cp in sends:
            cp.start()
        for cp in sends:
            cp.wait_recv()
        for cp in sends:
            cp.wait_send()

    return pl.kernel(
        launch, out_type=[jax.ShapeDtypeStruct((4,) + a.shape[1:], a.dtype) for a in arrs],
        mesh=plsc.ScalarSubcoreMesh(axis_name="sequencer", num_cores=1), name="exchange_pair",
        scratch_types=(pltpu.SemaphoreType.DMA((4, n)), pltpu.SemaphoreType.DMA((4, n))),
        compiler_params=pltpu.CompilerParams(collective_id=0),
    )(*arrs)


def _pair_sum(full, got, core, rows, name):
    _, nr, nc = got.shape

    def body(core_ref, a_ref, b_ref, o_ref):
        o_ref[...] = (a_ref[...].astype(F32) + b_ref[...].astype(F32)).astype(BF16)

    blk = pl.BlockSpec((1, rows, nc), lambda q, i, core_ref: (q, i, 0))
    return pl.pallas_call(
        body,
        grid_spec=pltpu.PrefetchScalarGridSpec(
            num_scalar_prefetch=1, grid=(4, nr // rows),
            in_specs=[pl.BlockSpec((1, rows, nc), lambda q, i, core_ref: (2 * q + core_ref[0], i, 0)), blk],
            out_specs=blk),
        out_shape=jax.ShapeDtypeStruct(got.shape, BF16), name=name,
    )(core, full, got)


def _exchange_chips(sums):
    n = len(sums)

    def launch(*refs):
        ins, outs = refs[:n], refs[n:2 * n]
        send_sems, recv_sems = refs[2 * n:]
        (x, y, c), _, chips = _place()
        barrier = pltpu.get_barrier_semaphore()
        for px, py in chips:
            pl.semaphore_signal(barrier, inc=1, device_id=(px, py, c), device_id_type=MESH)
        pl.semaphore_wait(barrier, len(chips))
        copies = [pltpu.make_async_remote_copy(
            src_ref=ins[k].at[2 * px + py], dst_ref=outs[k].at[j], send_sem=send_sems.at[j, k],
            recv_sem=recv_sems.at[j, k], device_id=(px, py, c), device_id_type=MESH)
            for j, (px, py) in enumerate(chips) for k in range(n)]
        for cp in copies:
            cp.start()
        for cp in copies:
            cp.wait_recv()
        for cp in copies:
            cp.wait_send()

    return pl.kernel(
        launch, out_type=[jax.ShapeDtypeStruct((3,) + a.shape[1:], a.dtype) for a in sums],
        mesh=plsc.ScalarSubcoreMesh(axis_name="sequencer", num_cores=1), name="exchange_chips",
        scratch_types=(pltpu.SemaphoreType.DMA((3, n)), pltpu.SemaphoreType.DMA((3, n))),
        compiler_params=pltpu.CompilerParams(collective_id=1),
    )(*sums)


def _exchange_square(partials):
    n = len(partials)

    def launch(*refs):
        ins, outs = refs[:n], refs[n:2 * n]
        send_sems, recv_sems, loc_sems = refs[2 * n:]
        (x, y, c), _, _ = _place()
        me = _dev_index(x, y, c)
        peers = [(1 - x if r & 4 else x, 1 - y if r & 2 else y, 1 - c if r & 1 else c) for r in range(1, NDEV)]
        barrier = pltpu.get_barrier_semaphore()
        for peer in peers:
            pl.semaphore_signal(barrier, inc=1, device_id=peer, device_id_type=MESH)
        pl.semaphore_wait(barrier, NDEV - 1)
        local = [pltpu.make_async_copy(ins[k].at[me], outs[k].at[me], loc_sems.at[k]) for k in range(n)]
        copies = [pltpu.make_async_remote_copy(
            src_ref=ins[k].at[_dev_index(*peer)], dst_ref=outs[k].at[me], send_sem=send_sems.at[r, k],
            recv_sem=recv_sems.at[r, k], device_id=peer, device_id_type=MESH)
            for r, peer in enumerate(peers) for k in range(n)]
        for cp in local + copies:
            cp.start()
        for r, peer in enumerate(peers):
            for k in range(n):
                pltpu.make_async_remote_copy(
                    src_ref=ins[k].at[me], dst_ref=outs[k].at[_dev_index(*peer)], send_sem=send_sems.at[r, k],
                    recv_sem=recv_sems.at[r, k], device_id=peer, device_id_type=MESH).wait_recv()
        for cp in copies:
            cp.wait_send()
        for cp in local:
            cp.wait()

    return pl.kernel(
        launch, out_type=[jax.ShapeDtypeStruct(a.shape, a.dtype) for a in partials],
        mesh=plsc.ScalarSubcoreMesh(axis_name="sequencer", num_cores=1), name="exchange_square",
        scratch_types=(pltpu.SemaphoreType.DMA((NDEV - 1, n)), pltpu.SemaphoreType.DMA((NDEV - 1, n)),
                       pltpu.SemaphoreType.DMA((n,))),
        compiler_params=pltpu.CompilerParams(collective_id=3),
    )(*partials)


def _exchange_small(small):
    def body(sm_ref, out_ref, send_sems, recv_sems):
        (x, y, c), _, _ = _place()
        me = _dev_index(x, y, c)
        peers = [(1 - x if r & 4 else x, 1 - y if r & 2 else y, 1 - c if r & 1 else c) for r in range(1, NDEV)]
        out_ref[me] = sm_ref[...]
        copies = [pltpu.make_async_remote_copy(
            src_ref=sm_ref, dst_ref=out_ref.at[me], send_sem=send_sems.at[r], recv_sem=recv_sems.at[r],
            device_id=peer, device_id_type=MESH) for r, peer in enumerate(peers)]
        for cp in copies:
            cp.start()
        for r, peer in enumerate(peers):
            pltpu.make_async_remote_copy(
                src_ref=sm_ref, dst_ref=out_ref.at[_dev_index(*peer)], send_sem=send_sems.at[r], recv_sem=recv_sems.at[r],
                device_id=peer, device_id_type=MESH).wait_recv()
        for cp in copies:
            cp.wait_send()

    vm = pl.BlockSpec(memory_space=pltpu.VMEM)
    return pl.pallas_call(
        body, out_shape=jax.ShapeDtypeStruct((NDEV,) + small.shape, F32), in_specs=[vm], out_specs=vm,
        scratch_shapes=[pltpu.SemaphoreType.DMA((NDEV - 1,)), pltpu.SemaphoreType.DMA((NDEV - 1,))],
        name="exchange_small",
    )(small)


def _inproj_fwd(xn, w_pairs, proj, chip, tb):
    t = xn.shape[0]
    nblk, _, nb = w_pairs.shape

    def body(chip_ref, xn_ref, w_ref, proj_in, proj_ref):
        proj_ref[...] = _mm(xn_ref[...], w_ref[0])

    def other(j, chip_ref):
        return j + (j >= chip_ref[0]).astype(jnp.int32)

    return pl.pallas_call(
        body,
        grid_spec=pltpu.PrefetchScalarGridSpec(
            num_scalar_prefetch=1, grid=(t // tb, nblk - 1),
            in_specs=[pl.BlockSpec((tb, D), lambda i, j, chip_ref: (i, 0)),
                      pl.BlockSpec((1, D, nb), lambda i, j, chip_ref: (other(j, chip_ref), 0, 0)),
                      pl.BlockSpec(memory_space=pl.ANY)],
            out_specs=pl.BlockSpec((tb, nb), lambda i, j, chip_ref: (i, other(j, chip_ref)))),
        out_shape=jax.ShapeDtypeStruct((t, DIN), F32),
        input_output_aliases={3: 0},
        name="inproj_fwd", compiler_params=_cp(56),
    )(chip, xn, w_pairs, proj)


def _lower_bound(lb_ref):
    a0 = lb_ref[0:1, :]
    a1 = lb_ref[1:2, :]
    mx = jnp.maximum(a0, a1)
    e0 = jnp.exp(a0 - mx)
    e1 = jnp.exp(a1 - mx)
    return e0 / (e0 + e1)


def _hgrn_chunk_fwd(hq, hf, lb, tril):
    sg = _sigmoid(hf)
    f = lb + (1.0 - lb) * sg
    g = jnp.log(f)
    k = 1.0 - f
    sq = _sigmoid(hq)
    q = hq * sq
    b = _mm_tri_exact(tril, g)
    last_row = lax.broadcasted_iota(jnp.int32, b.shape, 0) == CH - 1
    b_last = jnp.sum(jnp.where(last_row, b, 0.0), axis=0, keepdims=True)
    c = 0.5 * b_last
    eb = jnp.exp(b)
    ea = jnp.exp(b - c)
    ek = jnp.exp(c - b)
    ed = jnp.exp(b_last - b)
    ebl = jnp.exp(b_last)
    return dict(sg=sg, f=f, k=k, sq=sq, q=q, eb=eb, ea=ea, ek=ek, ed=ed, ebl=ebl,
                qe=q * eb, qa=q * ea, ka=k * ek, kd=k * ed)


def _tri(lower):
    r = lax.broadcasted_iota(jnp.int32, (CH, CH), 0)
    c = lax.broadcasted_iota(jnp.int32, (CH, CH), 1)
    return (r >= c) if lower else (c >= r)


def _head_segment(p_ref, rows, j, hg):
    return p_ref[rows, j * HD * hg:(j + 1) * HD * hg]


def _head(a, k):
    return a[:, k * HD:(k + 1) * HD]


def _hgrn_fwd(proj, lbw, rb, hg):
    assert hg == HEADS
    t = proj.shape[0]
    ncb = rb // CH

    def body(p_ref, lb_ref, o_ref, st_ref, s_scr):
        @pl.when(pl.program_id(1) == 0)
        def _():
            s_scr[...] = jnp.zeros_like(s_scr)

        lb = _lower_bound(lb_ref)
        causal = _tri(True)
        tril = _tri3(True)
        heads = range(hg)

        def chunk(cc, carry):
            r0 = pl.multiple_of(cc * CH, CH)
            rows = pl.ds(r0, CH)
            e = _hgrn_chunk_fwd(_head_segment(p_ref, rows, 0, hg), _head_segment(p_ref, rows, 1, hg), lb, tril)
            v = _bf(_head_segment(p_ref, rows, 2, hg))
            sts = [s_scr[k] for k in heads]
            qa, ka, qe, kd = _bf(e["qa"]), _bf(e["ka"]), _bf(e["qe"]), _bf(e["kd"])
            a = [_bf(jnp.where(causal, _mm_nt(_head(qa, k), _head(ka, k)), 0.0)) for k in heads]
            o_inter = [_mm_nt(_head(qe, k), _bf(sts[k])) for k in heads]
            kv = [_mm_tn(_head(v, k), _head(kd, k)) for k in heads]
            o_intra = [_mm(a[k], _head(v, k)) for k in heads]
            for k in heads:
                st_ref[cc, k] = sts[k]
                o_ref[rows, k * HD:(k + 1) * HD] = o_inter[k] + o_intra[k]
                s_scr[k] = sts[k] * _head(e["ebl"], k) + kv[k]
            return carry

        lax.fori_loop(0, ncb, chunk, 0, unroll=2)

    return pl.pallas_call(
        body, grid=(HEADS // hg, t // rb),
        in_specs=[pl.BlockSpec((rb, 3 * HD * hg), lambda h, i: (i, h)), pl.BlockSpec((2, HD * hg), lambda h, i: (0, h))],
        out_specs=[pl.BlockSpec((rb, HD * hg), lambda h, i: (i, h)),
                   pl.BlockSpec((ncb, hg, HD, HD), lambda h, i: (i, h, 0, 0))],
        out_shape=[jax.ShapeDtypeStruct((t, D), F32), jax.ShapeDtypeStruct((t // CH, HEADS, HD, HD), F32)],
        scratch_shapes=[pltpu.VMEM((hg, HD, HD), F32)],
        name="hgrn_fwd", compiler_params=_cp(48),
    )(proj, lbw)


def _hgrn_bwd(proj, lbw, do_raw, states, rb, hg):
    assert hg == HEADS
    t = proj.shape[0]
    nblk = t // rb
    ncb = rb // CH
    wd = HD * hg

    def body(p_ref, lb_ref, do_ref, st_ref, dp_ref, dlb_ref, ds_scr):
        @pl.when(pl.program_id(1) == 0)
        def _():
            ds_scr[...] = jnp.zeros_like(ds_scr)
            dlb_ref[...] = jnp.zeros_like(dlb_ref)

        lb = _lower_bound(lb_ref)
        causal = _tri(True)
        tril = _tri3(True)
        triu = _tri3(False)
        last_row = lax.broadcasted_iota(jnp.int32, (CH, HD * hg), 0) == CH - 1
        row0 = lax.broadcasted_iota(jnp.int32, (8, HD * hg), 0) == 0
        heads = range(hg)
        wide = lambda parts: jnp.concatenate(parts, axis=1)

        def chunk(it, carry):
            cc = ncb - 1 - it
            r0 = pl.multiple_of(cc * CH, CH)
            rows = pl.ds(r0, CH)
            hq = _head_segment(p_ref, rows, 0, hg)
            e = _hgrn_chunk_fwd(hq, _head_segment(p_ref, rows, 1, hg), lb, tril)
            v = _bf(_head_segment(p_ref, rows, 2, hg))
            do = _bf(do_ref[rows, :])
            sts = [st_ref[cc, k] for k in heads]
            dsts = [ds_scr[k] for k in heads]
            dlb_acc = dlb_ref[...]
            qa, ka, qe, kd = _bf(e["qa"]), _bf(e["ka"]), _bf(e["qe"]), _bf(e["kd"])
            a = [_bf(jnp.where(causal, _mm_nt(_head(qa, k), _head(ka, k)), 0.0)) for k in heads]
            da = [_bf(jnp.where(causal, _mm_nt(_head(do, k), _head(v, k)), 0.0)) for k in heads]
            dqe = wide([_mm(_head(do, k), _bf(sts[k])) for k in heads])
            dkd = wide([_mm(_head(v, k), _bf(dsts[k])) for k in heads])
            dv_state = [_mm_nt(_head(kd, k), _bf(dsts[k])) for k in heads]
            ds_new = [_mm_tn(_head(do, k), _head(qe, k)) for k in heads]
            dv_intra = [_mm_tn(a[k], _head(do, k)) for k in heads]
            dqa = wide([_mm(da[k], _head(ka, k)) for k in heads])
            dka = wide([_mm_tn(da[k], _head(qa, k)) for k in heads])
            dv = wide([dv_intra[k] + dv_state[k] for k in heads])
            dbl = e["ebl"] * wide([jnp.sum(sts[k] * dsts[k], axis=0, keepdims=True) for k in heads])
            dq = dqe * e["eb"] + dqa * e["ea"]
            dk = dka * e["ek"] + dkd * e["ed"]
            dkd_kd = dkd * kd.astype(F32)
            db = dqe * qe.astype(F32) + dqa * qa.astype(F32) - dka * ka.astype(F32) - dkd_kd
            db = db + jnp.where(last_row, dbl + jnp.sum(dkd_kd, axis=0, keepdims=True), 0.0)
            dg = _mm_tri_exact(triu, db)
            df = dg / e["f"] - dk
            sg = e["sg"]
            sq = e["sq"]
            dhq = _bf(dq * (sq * (1.0 + hq * (1.0 - sq))))
            dhf = _bf(df * (1.0 - lb) * sg * (1.0 - sg))
            dhi = _bf(dv)
            dlb_new = dlb_acc + jnp.where(row0, jnp.sum(df * (1.0 - sg), axis=0, keepdims=True), 0.0)
            for k in heads:
                ds_scr[k] = ds_new[k] + dsts[k] * _head(e["ebl"], k)
            dp_ref[rows, 0:wd] = dhq
            dp_ref[rows, wd:2 * wd] = dhf
            dp_ref[rows, 2 * wd:3 * wd] = dhi
            dlb_ref[...] = dlb_new
            return carry

        lax.fori_loop(0, ncb, chunk, 0, unroll=2)

    rev = lambda h, i: (nblk - 1 - i, h)
    return pl.pallas_call(
        body, grid=(HEADS // hg, nblk),
        in_specs=[pl.BlockSpec((rb, 3 * HD * hg), rev), pl.BlockSpec((2, HD * hg), lambda h, i: (0, h)),
                  pl.BlockSpec((rb, HD * hg), rev), pl.BlockSpec((ncb, hg, HD, HD), lambda h, i: (nblk - 1 - i, h, 0, 0))],
        out_specs=[pl.BlockSpec((rb, 3 * HD * hg), rev), pl.BlockSpec((8, HD * hg), lambda h, i: (0, h))],
        out_shape=[jax.ShapeDtypeStruct((t, 3 * D), BF16), jax.ShapeDtypeStruct((8, D), F32)],
        scratch_shapes=[pltpu.VMEM((hg, HD, HD), F32)],
        name="hgrn_bwd", compiler_params=_cp(48),
    )(proj, lbw, do_raw, states)


def _kv_variants(tile, odd):
    low = lax.broadcasted_iota(jnp.int32, tile.shape, 1) < 64
    if odd:
        hi = jnp.where(low, 0.0, tile)
        lo = pltpu.roll(hi, 64, 1)
    else:
        lo = jnp.where(low, tile, 0.0)
        hi = pltpu.roll(lo, 64, 1)
    return _bf(lo), _bf(hi)


def _attn_masks(n):
    qi = lax.broadcasted_iota(jnp.int32, (AB, AB), 0)
    kj = lax.broadcasted_iota(jnp.int32, (AB, AB), 1)
    cur = kj <= qi
    return cur, cur | (n > 0), qi <= kj


def _kv_all(prev_ref, cur_ref):
    out = []
    for tl in range(2):
        cols = slice(tl * 128, (tl + 1) * 128)
        tile = jnp.concatenate([prev_ref[:, cols], cur_ref[:, cols]], axis=0)
        out.append(_kv_variants(tile, 0))
        out.append(_kv_variants(tile, 1))
    return out


def _window(a2, cur):
    return jnp.where(cur, a2[:, AB:], a2[:, :AB])


def _attn_softmax(scores, sinks, cur, ok):
    s = [jnp.where(ok, _window(s2, cur) * ATT_SCALE, NEG) for s2 in scores]
    m = [jnp.maximum(jnp.max(si, axis=-1, keepdims=True), sink) for si, sink in zip(s, sinks)]
    p = [jnp.exp(si - mi) for si, mi in zip(s, m)]
    es = [jnp.exp(sink - mi) for sink, mi in zip(sinks, m)]
    inv = [1.0 / (jnp.sum(pi, axis=-1, keepdims=True) + ei) for pi, ei in zip(p, es)]
    return [pi * ii for pi, ii in zip(p, inv)], [ei * ii for ei, ii in zip(es, inv)]


def _spread(pc, cur):
    return jnp.concatenate([jnp.where(cur, 0.0, pc), jnp.where(cur, pc, 0.0)], axis=1)


def _spread_t(pct, cur_t):
    return jnp.concatenate([jnp.where(cur_t, 0.0, pct), jnp.where(cur_t, pct, 0.0)], axis=0)


def _attn_fwd(proj, sinks):
    t = proj.shape[0]
    nb = t // AB

    def body(q_ref, kc_ref, kp_ref, vc_ref, vp_ref, sink_ref, o_ref):
        cur, ok, _ = _attn_masks(pl.program_id(0))
        kvars = _kv_all(kp_ref, kc_ref)
        vvars = _kv_all(vp_ref, vc_ref)
        qps = [_bf(q_ref[:, 128 * j:128 * (j + 1)]) for j in range(8)]
        heads = [(j, ab) for j in range(8) for ab in range(2)]
        scores = [_mm_nt(qps[j], kvars[j // 2][ab]) for j, ab in heads]
        pcs, _ = _attn_softmax(scores, [sink_ref[0, h] for h in range(QH)], cur, ok)
        parts = [_mm(_bf(_spread(pcs[h], cur)), vvars[j // 2][ab]) for h, (j, ab) in enumerate(heads)]
        for j in range(8):
            o_ref[:, 128 * j:128 * (j + 1)] = parts[2 * j] + parts[2 * j + 1]

    prev = lambda n: jnp.maximum(n - 1, 0)
    return pl.pallas_call(
        body, grid=(nb,),
        in_specs=[pl.BlockSpec((AB, D), lambda n: (n, C_AQ // D)),
                  pl.BlockSpec((AB, 256), lambda n: (n, C_AK // 256)),
                  pl.BlockSpec((AB, 256), lambda n: (prev(n), C_AK // 256)),
                  pl.BlockSpec((AB, 256), lambda n: (n, C_AV // 256)),
                  pl.BlockSpec((AB, 256), lambda n: (prev(n), C_AV // 256)),
                  pl.BlockSpec(memory_space=pltpu.SMEM)],
        out_specs=pl.BlockSpec((AB, D), lambda n: (n, 0)),
        out_shape=jax.ShapeDtypeStruct((t, D), F32),
        name="attn_fwd", compiler_params=_cp(32),
    )(proj, proj, proj, proj, proj, sinks)


def _attn_bwd(proj, sinks, do_a):
    t = proj.shape[0]
    nb = t // AB

    def body(q_ref, kc_ref, kp_ref, vc_ref, vp_ref, do_ref, sink_ref, dq_ref, dkv_ref, dsink_ref, carry):
        n = pl.program_id(0)

        @pl.when(n == 0)
        def _():
            dsink_ref[...] = jnp.zeros_like(dsink_ref)
            carry[...] = jnp.zeros_like(carry)

        @pl.when(n < nb)
        def _():
            cur, ok, cur_t = _attn_masks(n)
            low = lax.broadcasted_iota(jnp.int32, (2 * AB, 128), 1) < 64
            lane = lax.broadcasted_iota(jnp.int32, (8, 128), 1)
            row0 = lax.broadcasted_iota(jnp.int32, (8, 128), 0) == 0
            kvars = _kv_all(kp_ref, kc_ref)
            vvars = _kv_all(vp_ref, vc_ref)
            qps = [_bf(q_ref[:, 128 * j:128 * (j + 1)]) for j in range(8)]
            dops = [_bf(do_ref[:, 128 * j:128 * (j + 1)]) for j in range(8)]
            heads = [(j, ab) for j in range(8) for ab in range(2)]
            scores = [_mm_nt(qps[j], kvars[j // 2][ab]) for j, ab in heads]
            dps = [_mm_nt(dops[j], vvars[j // 2][ab]) for j, ab in heads]
            pcs, pss = _attn_softmax(scores, [sink_ref[0, h] for h in range(QH)], cur, ok)
            dpcs = [_window(dp2, cur) for dp2 in dps]
            rss = [jnp.sum(pc * dpc, axis=-1, keepdims=True) for pc, dpc in zip(pcs, dpcs)]
            dscs = [pc * (dpc - rs) for pc, dpc, rs in zip(pcs, dpcs, rss)]
            dsink = jnp.zeros((8, 128), F32)
            for h in range(QH):
                dsink = dsink + jnp.where(row0 & (lane == h), -jnp.sum(pss[h] * rss[h]), 0.0)
            dq_terms = [_mm(_bf(_spread(dscs[h], cur)), kvars[j // 2][ab]) for h, (j, ab) in enumerate(heads)]
            for j in range(8):
                dq_ref[:, 128 * j:128 * (j + 1)] = _bf((dq_terms[2 * j] + dq_terms[2 * j + 1]) * ATT_SCALE)
            dsc_t = [_bf(_spread_t(dsc.T, cur_t)) for dsc in dscs]
            pc_t = [_bf(_spread_t(pc.T, cur_t)) for pc in pcs]
            dk_terms = [_mm(dsc_t[h], qps[j]) for h, (j, ab) in enumerate(heads)]
            dv_terms = [_mm(pc_t[h], dops[j]) for h, (j, ab) in enumerate(heads)]
            dk_ab = [[dk_terms[4 * g + ab] + dk_terms[4 * g + 2 + ab] for ab in range(2)] for g in range(4)]
            dv_ab = [[dv_terms[4 * g + ab] + dv_terms[4 * g + 2 + ab] for ab in range(2)] for g in range(4)]
            for tl in range(2):
                ke, ko = dk_ab[2 * tl], dk_ab[2 * tl + 1]
                ve, vo = dv_ab[2 * tl], dv_ab[2 * tl + 1]
                dkt = (jnp.where(low, ke[0], 0.0) + pltpu.roll(jnp.where(low, 0.0, ke[1]), 64, 1)
                       + jnp.where(low, 0.0, ko[1]) + pltpu.roll(jnp.where(low, ko[0], 0.0), 64, 1)) * ATT_SCALE
                dvt = (jnp.where(low, ve[0], 0.0) + pltpu.roll(jnp.where(low, 0.0, ve[1]), 64, 1)
                       + jnp.where(low, 0.0, vo[1]) + pltpu.roll(jnp.where(low, vo[0], 0.0), 64, 1))
                kcols = slice(tl * 128, (tl + 1) * 128)
                vcols = slice(256 + tl * 128, 256 + (tl + 1) * 128)
                dkv_ref[:, kcols] = _bf(carry[:, kcols] + dkt[0:AB])
                dkv_ref[:, vcols] = _bf(carry[:, vcols] + dvt[0:AB])
                carry[:, kcols] = dkt[AB:2 * AB]
                carry[:, vcols] = dvt[AB:2 * AB]
            dsink_ref[...] += dsink

        @pl.when(n == nb)
        def _():
            dkv_ref[...] = _bf(carry[...])

    cur = lambda n: jnp.minimum(n, nb - 1)
    prev = lambda n: jnp.clip(n - 1, 0, nb - 1)
    return pl.pallas_call(
        body, grid=(nb + 1,),
        in_specs=[pl.BlockSpec((AB, D), lambda n: (cur(n), C_AQ // D)),
                  pl.BlockSpec((AB, 256), lambda n: (cur(n), C_AK // 256)),
                  pl.BlockSpec((AB, 256), lambda n: (prev(n), C_AK // 256)),
                  pl.BlockSpec((AB, 256), lambda n: (cur(n), C_AV // 256)),
                  pl.BlockSpec((AB, 256), lambda n: (prev(n), C_AV // 256)),
                  pl.BlockSpec((AB, D), lambda n: (cur(n), 0)),
                  pl.BlockSpec(memory_space=pltpu.SMEM)],
        out_specs=[pl.BlockSpec((AB, D), lambda n: (cur(n), 0)),
                   pl.BlockSpec((AB, 512), lambda n: (prev(n), 0)),
                   pl.BlockSpec((8, 128), lambda n: (0, 0))],
        out_shape=[jax.ShapeDtypeStruct((t, D), BF16), jax.ShapeDtypeStruct((t, 512), BF16),
                   jax.ShapeDtypeStruct((8, 128), F32)],
        scratch_shapes=[pltpu.VMEM((AB, 512), F32)],
        name="attn_bwd", compiler_params=_cp(32),
    )(proj, proj, proj, proj, proj, do_a, sinks)


def _silu_and_grad(v):
    s = _sigmoid(v)
    return v * s, s * (1.0 + v * (1.0 - s))


def _tail(o_raw, o_a, proj, x2d, tgt, wbh, wba, wout, hnw, fnw, tb):
    t = x2d.shape[0]

    def body(or_ref, oa_ref, hg_ref, ag0, ag1, mh0, mh1, ma0, ma1, x_ref, t_ref, wbh_ref, wba_ref, wout_ref, hnw_ref,
             fnw_ref, dx2_ref, dor_ref, doa_ref, dhg_ref, dagm_ref, gh_ref, ga_ref, mg_ref, dyh_ref, dya_ref, dx2b_ref,
             sums_ref):
        @pl.when(pl.program_id(0) == 0)
        def _():
            sums_ref[...] = jnp.zeros_like(sums_ref)

        halves = lambda a, b: jnp.concatenate([a[...], b[...]], axis=1)
        hnw_v = hnw_ref[...]
        fnw_v = fnw_ref[...]
        o = or_ref[...]
        rs, xhs = [], []
        for h in range(HEADS):
            oh = o[:, h * HD:(h + 1) * HD]
            r = lax.rsqrt(jnp.mean(oh * oh, axis=-1, keepdims=True) + EPS)
            rs.append(r)
            xhs.append(oh * r)
        xh = jnp.concatenate(xhs, axis=1)
        on = xh * hnw_v
        sil_hg, dsil_hg = _silu_and_grad(hg_ref[...])
        gh_b = _bf(on * sil_hg)
        y_h = _mm(gh_b, wbh_ref[...])
        oa = oa_ref[...]
        sil_ag, dsil_ag = _silu_and_grad(halves(ag0, ag1))
        ga_b = _bf(oa * sil_ag)
        y_a = _mm(ga_b, wba_ref[...])
        s_mh = _sigmoid(halves(mh0, mh1))
        s_ma = _sigmoid(halves(ma0, ma1))
        mg_b = _bf(s_mh * y_h + s_ma * y_a)
        x2 = x_ref[...] + _mm(mg_b, wout_ref[...])
        r2 = lax.rsqrt(jnp.mean(x2 * x2, axis=-1, keepdims=True) + EPS)
        xh2 = x2 * r2
        err = xh2 * fnw_v - t_ref[...]
        loss = 0.5 * jnp.sum(jnp.mean(err * err, axis=-1, keepdims=True))
        dy = err * (1.0 / D)
        dfnw = jnp.sum(dy * xh2, axis=0, keepdims=True)
        dxh2 = dy * fnw_v
        dx2 = r2 * (dxh2 - xh2 * jnp.mean(dxh2 * xh2, axis=-1, keepdims=True))
        dx2_ref[...] = dx2
        dx2_b = _bf(dx2)
        dmg = _mm_nt(dx2_b, wout_ref[...])
        dmg_h = dmg * s_mh
        dmg_a = dmg * s_ma
        dyh_b = _bf(dmg_h)
        dya_b = _bf(dmg_a)
        dagm_ref[:, D:2 * D] = _bf(dmg_h * y_h * (1.0 - s_mh))
        dagm_ref[:, 2 * D:3 * D] = _bf(dmg_a * y_a * (1.0 - s_ma))
        dgh = _mm_nt(dyh_b, wbh_ref[...])
        dga = _mm_nt(dya_b, wba_ref[...])
        doa_ref[...] = dga * sil_ag
        dagm_ref[:, 0:D] = _bf(dga * oa * dsil_ag)
        dhg_ref[...] = _bf(dgh * on * dsil_hg)
        don = dgh * sil_hg
        dhnw = jnp.sum(don * xh, axis=0, keepdims=True)
        dxh = don * hnw_v
        dos = []
        for h in range(HEADS):
            sl = slice(h * HD, (h + 1) * HD)
            dos.append(rs[h] * (dxh[:, sl] - xhs[h] * jnp.mean(dxh[:, sl] * xhs[h], axis=-1, keepdims=True)))
        dor_ref[...] = jnp.concatenate(dos, axis=1)
        gh_ref[...] = gh_b
        ga_ref[...] = ga_b
        mg_ref[...] = mg_b
        dyh_ref[...] = dyh_b
        dya_ref[...] = dya_b
        dx2b_ref[...] = dx2_b
        row = lax.broadcasted_iota(jnp.int32, (8, D), 0)
        sums_ref[...] += jnp.where(row == 0, dfnw, 0.0) + jnp.where(row == 1, dhnw, 0.0) + jnp.where(row == 2, loss, 0.0)

    rowblk = lambda c: pl.BlockSpec((tb, D), lambda i: (i, c))
    half = lambda c: pl.BlockSpec((tb, 512), lambda i: (i, c))
    full = lambda shape: pl.BlockSpec(shape, lambda i: (0, 0))
    return pl.pallas_call(
        body, grid=(t // tb,),
        in_specs=[rowblk(0), rowblk(0), rowblk(C_HG // D), half(C_AG // 512), half(C_AG // 512 + 1), half(C_MH // 512),
                  half(C_MH // 512 + 1), half(C_MA // 512), half(C_MA // 512 + 1), rowblk(0), rowblk(0),
                  full((D, D)), full((D, D)), full((D, D)), full((1, D)), full((1, D))],
        out_specs=[rowblk(0), rowblk(0), rowblk(0), rowblk(0), pl.BlockSpec((tb, 3 * D), lambda i: (i, 0))]
        + [rowblk(0)] * 6 + [full((8, D))],
        out_shape=[jax.ShapeDtypeStruct((t, D), F32)] * 3
        + [jax.ShapeDtypeStruct((t, D), BF16), jax.ShapeDtypeStruct((t, 3 * D), BF16)]
        + [jax.ShapeDtypeStruct((t, D), BF16)] * 6 + [jax.ShapeDtypeStruct((8, D), F32)],
        name="tail", compiler_params=_cp(56),
    )(o_raw, o_a, proj, proj, proj, proj, proj, proj, proj, x2d, tgt, wbh, wba, wout, hnw, fnw)


def _wgrad3(gh, dyh, ga, dya, mg, dx2b, tk):
    t = dyh.shape[0]

    def body(a0, b0, a1, b1, a2, b2, o0, o1, o2):
        @pl.when(pl.program_id(0) == 0)
        def _():
            o0[...] = jnp.zeros_like(o0)
            o1[...] = jnp.zeros_like(o1)
            o2[...] = jnp.zeros_like(o2)

        o0[...] += _mm_tn(a0[...], b0[...])
        o1[...] += _mm_tn(a1[...], b1[...])
        o2[...] += _mm_tn(a2[...], b2[...])

    blk = pl.BlockSpec((tk, D), lambda k: (k, 0))
    out = pl.BlockSpec((D, D), lambda k: (0, 0))
    return pl.pallas_call(
        body, grid=(t // tk,), in_specs=[blk] * 6, out_specs=[out] * 3,
        out_shape=[jax.ShapeDtypeStruct((D, D), F32)] * 3,
        name="wgrad3", compiler_params=_cp(48),
    )(gh, dyh, ga, dya, mg, dx2b)


def _inproj_wgrad_piece(xnt, piece, nb, name):
    t = xnt.shape[1]
    width = piece.shape[1]

    def body(xnt_ref, p_ref, o_ref):
        o_ref[...] = _mm(xnt_ref[...], p_ref[...])

    return pl.pallas_call(
        body, grid=(width // nb,),
        in_specs=[pl.BlockSpec((D, t), lambda j: (0, 0), pipeline_mode=pl.Buffered(1)),
                  pl.BlockSpec((t, nb), lambda j: (0, j))],
        out_specs=pl.BlockSpec((D, nb), lambda j: (0, j)),
        out_shape=jax.ShapeDtypeStruct((D, width), F32),
        name=name, compiler_params=_cp(56),
    )(xnt, piece)


def _inproj_dgrad(pieces, w_p, x2d, dx2, norm_w, tb, after):
    t = x2d.shape[0]

    def body(*refs):
        piece_refs = refs[:len(pieces)]
        w_ref, x_ref, dx2_ref, nw_ref, _, gx_ref, dnw_ref = refs[len(pieces):]

        @pl.when(pl.program_id(0) == 0)
        def _():
            dnw_ref[...] = jnp.zeros_like(dnw_ref)

        dxn = None
        off = 0
        for p in piece_refs:
            width = p.shape[1]
            for q in range(w_ref.shape[0]):
                lo, hi = max(off, q * PAIR), min(off + width, (q + 1) * PAIR)
                if lo < hi:
                    term = _mm_nt(p[:, lo - off:hi - off], w_ref[q, :, lo - q * PAIR:hi - q * PAIR])
                    dxn = term if dxn is None else dxn + term
            off += width
        xv = x_ref[...]
        r = lax.rsqrt(jnp.mean(xv * xv, axis=-1, keepdims=True) + EPS)
        xh = xv * r
        dxh = dxn * nw_ref[...]
        gx_ref[...] = dx2_ref[...] + r * (dxh - xh * jnp.mean(dxh * xh, axis=-1, keepdims=True))
        row0 = lax.broadcasted_iota(jnp.int32, (8, D), 0) == 0
        dnw_ref[...] += jnp.where(row0, jnp.sum(dxn * xh, axis=0, keepdims=True), 0.0)

    rowblk = pl.BlockSpec((tb, D), lambda i: (i, 0))
    return pl.pallas_call(
        body, grid=(t // tb,),
        in_specs=[pl.BlockSpec((tb, p.shape[1]), lambda i: (i, 0)) for p in pieces]
        + [pl.BlockSpec(w_p.shape, lambda i: (0, 0, 0), pipeline_mode=pl.Buffered(1)), rowblk, rowblk,
           pl.BlockSpec((1, D), lambda i: (0, 0)), pl.BlockSpec(memory_space=pl.ANY)],
        out_specs=[rowblk, pl.BlockSpec((8, D), lambda i: (0, 0))],
        out_shape=[jax.ShapeDtypeStruct((t, D), F32), jax.ShapeDtypeStruct((8, D), F32)],
        name="inproj_dgrad", compiler_params=_cp(60),
    )(*pieces, w_p, x2d, dx2, norm_w, after)


def _adamw_math(w, g, m, v):
    m = B1 * m + (1.0 - B1) * g
    v = B2 * v + (1.0 - B2) * (g * g)
    m_hat = m / (1.0 - B1 ** STEP)
    v_hat = v / (1.0 - B2 ** STEP)
    delta = -LR * (m_hat / (jnp.sqrt(v_hat) + ADAM_EPS) + WD * w)
    return delta, m, v


def _adamw_shard(recv, sums, chip, w, m, v, rows, name):
    nparts, nr, nc = recv.shape

    def body(chip_ref, own_ref, p_ref, w_ref, m_ref, v_ref, g_ref, d_ref, nm_ref, nv_ref):
        g = own_ref[0].astype(F32)
        for s in range(nparts):
            g = g + p_ref[s].astype(F32)
        d, nm, nv = _adamw_math(w_ref[...], g, m_ref[...], v_ref[...])
        g_ref[...] = g
        d_ref[...] = d
        nm_ref[...] = nm
        nv_ref[...] = nv

    blk = pl.BlockSpec((rows, nc), lambda i, chip_ref: (i, 0))
    return pl.pallas_call(
        body,
        grid_spec=pltpu.PrefetchScalarGridSpec(
            num_scalar_prefetch=1, grid=(nr // rows,),
            in_specs=[pl.BlockSpec((1, rows, nc), lambda i, chip_ref: (chip_ref[0], i, 0)),
                      pl.BlockSpec((nparts, rows, nc), lambda i, chip_ref: (0, i, 0)), blk, blk, blk],
            out_specs=[blk] * 4),
        out_shape=[jax.ShapeDtypeStruct((nr, nc), F32)] * 4,
        name=name, compiler_params=_cp(48),
    )(chip, sums, recv, w, m, v)


def _adamw_sum8(parts, w, m, v, after, name):
    def body(p_ref, w_ref, m_ref, v_ref, _, g_ref, d_ref, nm_ref, nv_ref):
        g = p_ref[0].astype(F32)
        for s in range(1, NDEV):
            g = g + p_ref[s].astype(F32)
        d, nm, nv = _adamw_math(w_ref[...], g, m_ref[...], v_ref[...])
        g_ref[...] = g
        d_ref[...] = d
        nm_ref[...] = nm
        nv_ref[...] = nv

    vm = pl.BlockSpec(memory_space=pltpu.VMEM)
    return pl.pallas_call(
        body, out_shape=[jax.ShapeDtypeStruct(w.shape, F32)] * 4,
        in_specs=[vm, vm, vm, vm, pl.BlockSpec(memory_space=pl.ANY)], out_specs=[vm] * 4, name=name,
    )(parts, w, m, v, after)


SMALL_ROWS = dict(norm_w=0, lower_bound=1, hgrn_norm_w=3, final_norm_w=4, sinks=5, loss=6)


def _pack_small_grads(dnw, dlb, sums, dsink):
    def body(dnw_ref, dlb_ref, sums_ref, dsink_ref, o_ref):
        o_ref[...] = jnp.zeros_like(o_ref)
        o_ref[0:1, :] = dnw_ref[0:1, :]
        o_ref[1:2, :] = dlb_ref[0:1, :]
        o_ref[3:4, :] = sums_ref[1:2, :]
        o_ref[4:5, :] = sums_ref[0:1, :]
        o_ref[5:6, 0:128] = dsink_ref[0:1, :]
        o_ref[6:7, :] = sums_ref[2:3, :]

    return pl.pallas_call(body, out_shape=jax.ShapeDtypeStruct((8, D), F32), name="pack_small_grads")(dnw, dlb, sums, dsink)


def _adamw_small(parts, ws, ms, vs):
    shapes = [a.shape for a in ws]

    def body(p_ref, *refs):
        w, m, v = refs[0:5], refs[5:10], refs[10:15]
        outs = [refs[15 + 5 * i:20 + 5 * i] for i in range(4)]
        loss_ref = refs[35]

        def total(row, width):
            g = p_ref[0, row:row + 1, 0:width]
            for s in range(1, NDEV):
                g = g + p_ref[s, row:row + 1, 0:width]
            return g

        loss_ref[...] = total(6, 128)
        lb = _lower_bound(w[1])
        ga0 = total(1, D) * lb * (1.0 - lb)
        grads = [total(0, D), None, total(3, D), total(4, D), total(5, QH)]
        for i in (0, 2, 3, 4):
            res = (grads[i],) + _adamw_math(w[i][...], grads[i], m[i][...], v[i][...])
            for o, val in zip(outs, res):
                o[i][...] = val
        for r, g in ((0, ga0), (1, -ga0)):
            res = (g,) + _adamw_math(w[1][r:r + 1, :], g, m[1][r:r + 1, :], v[1][r:r + 1, :])
            for o, val in zip(outs, res):
                o[1][r:r + 1, :] = val

    res = pl.pallas_call(
        body, out_shape=[jax.ShapeDtypeStruct(s, F32) for s in shapes] * 4 + [jax.ShapeDtypeStruct((1, 128), F32)],
        name="adamw_small",
    )(parts, *ws, *ms, *vs)
    return [res[5 * i:5 * i + 5] for i in range(4)], res[20][0, 0]


def kernel(x, norm_w, w_in, hgrn_lower_bound, hgrn_norm_w, w_branch_hgrn, attn_sinks, w_branch_attn, w_out, final_norm_w, loss_target, m_norm_w, m_w_in, m_hgrn_lower_bound, m_hgrn_norm_w, m_w_branch_hgrn, m_attn_sinks, m_w_branch_attn, m_w_out, m_final_norm_w, v_norm_w, v_w_in, v_hgrn_lower_bound, v_hgrn_norm_w, v_w_branch_hgrn, v_attn_sinks, v_w_branch_attn, v_w_out, v_final_norm_w):
    t = x.shape[1]
    x2d = x.reshape(t, D)
    tgt = loss_target.reshape(t, D)
    fnw = final_norm_w.reshape(1, D)
    row_blk = min(256, t)
    big_blk = min(512, t)

    chip = (2 * lax.axis_index("x") + lax.axis_index("y")).astype(jnp.int32).reshape(1)
    w_p, xn, xnt, proj_own = _gather_in_projection(w_in[0], x2d, norm_w)
    wbh, wba, wout = (g.reshape(D, D) for g in _gather_square(
        [w_branch_hgrn[0].astype(BF16), w_branch_attn[0].astype(BF16), w_out[0].astype(BF16)], after=w_p))

    proj = _inproj_fwd(xn, w_p, proj_own, chip, min(1024, t))
    o_raw, states = _hgrn_fwd(proj, hgrn_lower_bound, big_blk, HGRN_GROUP)
    o_a = _attn_fwd(proj, attn_sinks)
    (dx2, do_raw, do_a, d_hg, d_agm, gh, ga, mg, dyh, dya, dx2b, sums) = _tail(
        o_raw, o_a, proj, x2d, tgt, wbh, wba, wout, hgrn_norm_w, fnw, row_blk)
    dwbh, dwba, dwout = _wgrad3(gh, dyh, ga, dya, mg, dx2b, big_blk)
    d_aq, d_kv, dsink = _attn_bwd(proj, attn_sinks, do_a)
    d_hgrn, dlb = _hgrn_bwd(proj, hgrn_lower_bound, do_raw, states, big_blk, HGRN_GROUP)
    pieces = (d_hgrn, d_hg, d_aq, d_kv, d_agm)
    dw_pieces = [_inproj_wgrad_piece(xnt, p, CB, "inproj_wgrad_" + n)
                 for p, n in zip(pieces, ("hgrn", "hgate", "aq", "kv", "gates"))]

    dwin_r = jnp.concatenate(dw_pieces, axis=1).reshape(D, NDEV, IN_SHARD).transpose(1, 0, 2).astype(BF16)
    slots = lambda a: a.reshape(NDEV, ROW_SHARD, D).astype(BF16)
    got = _exchange_pair([dwin_r])
    core = lax.axis_index("c").astype(jnp.int32).reshape(1)
    s_in = _pair_sum(dwin_r, got[0], core, ROW_SHARD, "pair_sum_w_in")
    rin, = _exchange_chips([s_in])
    rbh, rba, rout = _exchange_square([slots(dwbh), slots(dwba), slots(dwout)])
    grad_x, dnw = _inproj_dgrad(pieces, w_p, x2d, dx2, norm_w, big_blk, after=s_in)
    rsm = _exchange_small(_pack_small_grads(dnw, dlb, sums, dsink))
    g_in, d_in, nm_in, nv_in = _adamw_shard(rin, s_in, chip, w_in[0], m_w_in[0], v_w_in[0], 128, "adamw_w_in")
    g_bh, d_bh, nm_bh, nv_bh = _adamw_sum8(
        rbh, w_branch_hgrn[0], m_w_branch_hgrn[0], v_w_branch_hgrn[0], dnw, "adamw_w_bh")
    g_ba, d_ba, nm_ba, nv_ba = _adamw_sum8(
        rba, w_branch_attn[0], m_w_branch_attn[0], v_w_branch_attn[0], dnw, "adamw_w_ba")
    g_out, d_out, nm_out, nv_out = _adamw_sum8(rout, w_out[0], m_w_out[0], v_w_out[0], dnw, "adamw_w_out")
    (sg, sd, sm, sv), loss = _adamw_small(
        rsm,
        (norm_w, hgrn_lower_bound, hgrn_norm_w, fnw, attn_sinks),
        (m_norm_w, m_hgrn_lower_bound, m_hgrn_norm_w, m_final_norm_w.reshape(1, D), m_attn_sinks),
        (v_norm_w, v_hgrn_lower_bound, v_hgrn_norm_w, v_final_norm_w.reshape(1, D), v_attn_sinks))

    def group(s, w_in_v, bh, ba, out):
        nw, lb, hnw, fn, sinks = s
        return (nw, w_in_v[None], lb, hnw, bh[None], sinks, ba[None], out[None], fn.reshape(D))

    return (loss, grad_x.reshape(1, t, D),
            *group(sg, g_in, g_bh, g_ba, g_out), *group(sd, d_in, d_bh, d_ba, d_out),
            *group(sm, nm_in, nm_bh, nm_ba, nm_out), *group(sv, nv_in, nv_bh, nv_ba, nv_out))
```

```python
import jax
import jax.numpy as jnp
from jax import lax
from jax.experimental import pallas as pl
from jax.experimental.pallas import tpu as pltpu
from jax.experimental.pallas import tpu_sc as plsc

F32 = jnp.float32
BF16 = jnp.bfloat16

D = 1024
DIN = 8704
NDEV = 8
IN_SHARD = DIN // NDEV
PAIR = 2 * IN_SHARD
ROW_SHARD = D // NDEV
HEADS = 8
HD = 128
CH = 64
HGRN_GROUP = 8
QH = 16
AB = 128
EPS = 1e-6
NEG = -1e30
ATT_SCALE = 0.125

C_HG = 3072
C_AQ = 4096
C_AK = 5120
C_AV = 5376
C_AG = 5632
C_MH = 6656
C_MA = 7680
CB = 512

LR = 0.001
B1 = 0.9
B2 = 0.999
ADAM_EPS = 1e-08
WD = 0.01
STEP = 10

MESH = pl.DeviceIdType.MESH


def _cp(vmem_mb):
    return pltpu.CompilerParams(vmem_limit_bytes=vmem_mb * 1024 * 1024)


def _mm(a, b):
    return jnp.dot(a, b, preferred_element_type=F32)


def _mm_nt(a, b):
    return lax.dot_general(a, b, (((1,), (1,)), ((), ())), preferred_element_type=F32)


def _mm_tn(a, b):
    return lax.dot_general(a, b, (((0,), (0,)), ((), ())), preferred_element_type=F32)


def _tri3(lower):
    r = lax.broadcasted_iota(jnp.int32, (CH, 3 * CH), 0)
    c = lax.broadcasted_iota(jnp.int32, (CH, 3 * CH), 1)
    c = jnp.where(c >= 2 * CH, c - 2 * CH, jnp.where(c >= CH, c - CH, c))
    return ((r >= c) if lower else (c >= r)).astype(BF16)


def _mm_tri_exact(tri3, g):
    g1 = g.astype(BF16)
    r1 = g - g1.astype(F32)
    g2 = r1.astype(BF16)
    g3 = (r1 - g2.astype(F32)).astype(BF16)
    return _mm(tri3, jnp.concatenate([g1, g2, g3], axis=0))


def _sigmoid(v):
    return 0.5 * jnp.tanh(0.5 * v) + 0.5


def _bf(v):
    return v.astype(BF16)


def _place():
    x, y, c = lax.axis_index("x"), lax.axis_index("y"), lax.axis_index("c")
    return (x, y, c), (x, y, 1 - c), [(1 - x, y), (x, 1 - y), (1 - x, 1 - y)]


def _dev_index(px, py, pc):
    return 4 * px + 2 * py + pc


def _gather_in_projection(w_in_s, x2d, norm_w):
    half = D // 2
    t = x2d.shape[0]
    prep_rows = min(512, t)
    nprep = t // prep_rows

    def body(win_ref, x_hbm, nw_ref, wp_g, xn_hbm, xnt_hbm, proj_hbm, give, take, mine, xbuf, xnbuf, xntbuf, w_own, pbuf,
             send_sems, recv_sems, loc_sem, swap_sems, in_sems, out_sems, own_sem):
        (x, y, c), sibling, chips = _place()
        give[...] = win_ref[pl.ds(pl.multiple_of(half * (1 - c), half), half), :].astype(BF16)
        swap = pltpu.make_async_remote_copy(src_ref=give, dst_ref=take, send_sem=swap_sems.at[0], recv_sem=swap_sems.at[1],
                                            device_id=sibling, device_id_type=MESH)
        swap.start()
        swap.wait()
        own = win_ref[pl.ds(pl.multiple_of(half * c, half), half), :]
        other = take[...].astype(F32)
        mine[...] = jnp.where(c == 0, jnp.concatenate([own, other], axis=1),
                              jnp.concatenate([other, own], axis=1)).astype(BF16)

        def place(px, py, pc):
            return wp_g.at[2 * px + py, pl.ds(pl.multiple_of(half * pc, half), half), :]

        def copy(kind, origin, to, src=mine):
            return pltpu.make_async_remote_copy(
                src_ref=src, dst_ref=place(*origin), send_sem=send_sems.at[kind], recv_sem=recv_sems.at[kind],
                device_id=to, device_id_type=MESH)

        me = (x, y, c)
        local = pltpu.make_async_copy(mine, place(*me), loc_sem)
        local.start()
        first = [copy(0, me, sibling)] + [copy(1 + j, me, (*chip, c)) for j, chip in enumerate(chips)]
        for cp in first:
            cp.start()

        copy(0, (x, y, 1 - c), me).wait_recv()
        local.wait()
        my_chip = 2 * x + y
        fetch = pltpu.make_async_copy(wp_g.at[my_chip], w_own, own_sem)
        fetch.start()

        def rows_of(i):
            return pl.ds(pl.multiple_of(i * prep_rows, prep_rows), prep_rows)

        def load(i, slot):
            return pltpu.make_async_copy(x_hbm.at[rows_of(i), :], xbuf.at[slot], in_sems.at[slot])

        def stores(i, slot):
            own_cols = pl.ds(pl.multiple_of(my_chip * PAIR, 128), PAIR)
            return (pltpu.make_async_copy(xnbuf.at[slot], xn_hbm.at[rows_of(i), :], out_sems.at[slot, 0]),
                    pltpu.make_async_copy(xntbuf.at[slot], xnt_hbm.at[:, rows_of(i)], out_sems.at[slot, 1]),
                    pltpu.make_async_copy(pbuf.at[slot], proj_hbm.at[rows_of(i), own_cols], out_sems.at[slot, 2]))

        load(0, 0).start()
        fetch.wait()

        def prep(i, carry):
            slot = lax.rem(i, 2)
            load(i, slot).wait()

            @pl.when(i + 1 < nprep)
            def _():
                load(i + 1, 1 - slot).start()

            @pl.when(i >= 2)
            def _():
                for cp in stores(i - 2, slot):
                    cp.wait()

            xv = xbuf[slot]
            xn = (xv * lax.rsqrt(jnp.mean(xv * xv, axis=-1, keepdims=True) + EPS)) * nw_ref[...]
            xn_b = xn.astype(BF16)
            xnbuf[slot] = xn_b
            xntbuf[slot] = xn.T.astype(BF16)
            pbuf[slot] = _mm(xn_b, w_own[...])
            for cp in stores(i, slot):
                cp.start()
            return carry

        lax.fori_loop(0, nprep, prep, 0)
        for i in range(max(nprep - 2, 0), nprep):
            for cp in stores(i, i % 2):
                cp.wait()

        passed = []
        for j, chip in enumerate(chips):
            copy(1 + j, (*chip, c), me).wait_recv()
            cp = copy(4 + j, (*chip, c), sibling, src=place(*chip, c))
            cp.start()
            passed.append(cp)
        for j, chip in enumerate(chips):
            copy(4 + j, (*chip, 1 - c), me).wait_recv()
        for cp in first + passed:
            cp.wait_send()

    vm = pl.BlockSpec(memory_space=pltpu.VMEM)
    hbm = pl.BlockSpec(memory_space=pl.ANY)
    return pl.pallas_call(
        body,
        out_shape=[jax.ShapeDtypeStruct((NDEV // 2, D, PAIR), BF16), jax.ShapeDtypeStruct((t, D), BF16),
                   jax.ShapeDtypeStruct((D, t), BF16), jax.ShapeDtypeStruct((t, DIN), F32)],
        in_specs=[vm, hbm, vm],
        out_specs=[hbm, hbm, hbm, hbm],
        scratch_shapes=[pltpu.VMEM((half, IN_SHARD), BF16), pltpu.VMEM((half, IN_SHARD), BF16),
                        pltpu.VMEM((half, PAIR), BF16),
                        pltpu.VMEM((2, prep_rows, D), F32), pltpu.VMEM((2, prep_rows, D), BF16),
                        pltpu.VMEM((2, D, prep_rows), BF16),
                        pltpu.VMEM((D, PAIR), BF16), pltpu.VMEM((2, prep_rows, PAIR), F32),
                        pltpu.SemaphoreType.DMA((NDEV - 1,)), pltpu.SemaphoreType.DMA((NDEV - 1,)),
                        pltpu.SemaphoreType.DMA, pltpu.SemaphoreType.DMA((2,)),
                        pltpu.SemaphoreType.DMA((2,)), pltpu.SemaphoreType.DMA((2, 3)), pltpu.SemaphoreType.DMA],
        name="gather_in_projection", compiler_params=_cp(56),
    )(w_in_s, x2d, norm_w)


def _gather_square(shards, after):
    n = len(shards)

    def launch(*refs):
        ins, outs = refs[:n], refs[n + 1:2 * n + 1]
        send_sems, recv_sems, loc_sems = refs[2 * n + 1:]
        (x, y, c), _, _ = _place()
        me = _dev_index(x, y, c)
        peers = [(1 - x if r & 4 else x, 1 - y if r & 2 else y, 1 - c if r & 1 else c) for r in range(1, NDEV)]
        barrier = pltpu.get_barrier_semaphore()
        for peer in peers:
            pl.semaphore_signal(barrier, inc=1, device_id=peer, device_id_type=MESH)
        pl.semaphore_wait(barrier, NDEV - 1)
        local = [pltpu.make_async_copy(ins[k], outs[k].at[me], loc_sems.at[k]) for k in range(n)]
        copies = [pltpu.make_async_remote_copy(
            src_ref=ins[k], dst_ref=outs[k].at[me], send_sem=send_sems.at[r, k], recv_sem=recv_sems.at[r, k],
            device_id=peer, device_id_type=MESH) for r, peer in enumerate(peers) for k in range(n)]
        for cp in local + copies:
            cp.start()
        for r, peer in enumerate(peers):
            for k in range(n):
                pltpu.make_async_remote_copy(
                    src_ref=ins[k], dst_ref=outs[k].at[_dev_index(*peer)], send_sem=send_sems.at[r, k],
                    recv_sem=recv_sems.at[r, k], device_id=peer, device_id_type=MESH).wait_recv()
        for cp in copies:
            cp.wait_send()
        for cp in local:
            cp.wait()

    return pl.kernel(
        launch, out_type=[jax.ShapeDtypeStruct((NDEV,) + a.shape, a.dtype) for a in shards],
        mesh=plsc.ScalarSubcoreMesh(axis_name="sequencer", num_cores=1), name="gather_square",
        scratch_types=(pltpu.SemaphoreType.DMA((NDEV - 1, n)), pltpu.SemaphoreType.DMA((NDEV - 1, n)),
                       pltpu.SemaphoreType.DMA((n,))),
        compiler_params=pltpu.CompilerParams(collective_id=2),
    )(*shards, after)


def _exchange_pair(arrs):
    n = len(arrs)

    def launch(*refs):
        ins, got = refs[:n], refs[n:2 * n]
        send_sems, recv_sems = refs[2 * n:]
        (x, y, c), sibling, _ = _place()
        barrier = pltpu.get_barrier_semaphore()
        pl.semaphore_signal(barrier, inc=1, device_id=sibling, device_id_type=MESH)
        pl.semaphore_wait(barrier, 1)
        sends = [pltpu.make_async_remote_copy(
            src_ref=ins[k].at[_dev_index(q // 2, q % 2, 1 - c)], dst_ref=got[k].at[q], send_sem=send_sems.at[q, k],
            recv_sem=recv_sems.at[q, k], device_id=sibling, device_id_type=MESH) for q in range(4) for k in range(n)]
        for cp in sends:
            cp.start()
        for cp in sends:
            cp.wait_recv()
        for cp in sends:
            cp.wait_send()

    return pl.kernel(
        launch, out_type=[jax.ShapeDtypeStruct((4,) + a.shape[1:], a.dtype) for a in arrs],
        mesh=plsc.ScalarSubcoreMesh(axis_name="sequencer", num_cores=1), name="exchange_pair",
        scratch_types=(pltpu.SemaphoreType.DMA((4, n)), pltpu.SemaphoreType.DMA((4, n))),
        compiler_params=pltpu.CompilerParams(collective_id=0),
    )(*arrs)


def _pair_sum(full, got, core, rows, name):
    _, nr, nc = got.shape

    def body(core_ref, a_ref, b_ref, o_ref):
        o_ref[...] = (a_ref[...].astype(F32) + b_ref[...].astype(F32)).astype(BF16)

    blk = pl.BlockSpec((1, rows, nc), lambda q, i, core_ref: (q, i, 0))
    return pl.pallas_call(
        body,
        grid_spec=pltpu.PrefetchScalarGridSpec(
            num_scalar_prefetch=1, grid=(4, nr // rows),
            in_specs=[pl.BlockSpec((1, rows, nc), lambda q, i, core_ref: (2 * q + core_ref[0], i, 0)), blk],
            out_specs=blk),
        out_shape=jax.ShapeDtypeStruct(got.shape, BF16), name=name,
    )(core, full, got)


def _exchange_chips(sums):
    n = len(sums)

    def launch(*refs):
        ins, outs = refs[:n], refs[n:2 * n]
        send_sems, recv_sems = refs[2 * n:]
        (x, y, c), _, chips = _place()
        barrier = pltpu.get_barrier_semaphore()
        for px, py in chips:
            pl.semaphore_signal(barrier, inc=1, device_id=(px, py, c), device_id_type=MESH)
        pl.semaphore_wait(barrier, len(chips))
        copies = [pltpu.make_async_remote_copy(
            src_ref=ins[k].at[2 * px + py], dst_ref=outs[k].at[j], send_sem=send_sems.at[j, k],
            recv_sem=recv_sems.at[j, k], device_id=(px, py, c), device_id_type=MESH)
            for j, (px, py) in enumerate(chips) for k in range(n)]
        for cp in copies:
            cp.start()
        for cp in copies:
            cp.wait_recv()
        for cp in copies:
            cp.wait_send()

    return pl.kernel(
        launch, out_type=[jax.ShapeDtypeStruct((3,) + a.shape[1:], a.dtype) for a in sums],
        mesh=plsc.ScalarSubcoreMesh(axis_name="sequencer", num_cores=1), name="exchange_chips",
        scratch_types=(pltpu.SemaphoreType.DMA((3, n)), pltpu.SemaphoreType.DMA((3, n))),
        compiler_params=pltpu.CompilerParams(collective_id=1),
    )(*sums)


def _exchange_square(partials):
    n = len(partials)

    def launch(*refs):
        ins, outs = refs[:n], refs[n:2 * n]
        send_sems, recv_sems, loc_sems = refs[2 * n:]
        (x, y, c), _, _ = _place()
        me = _dev_index(x, y, c)
        peers = [(1 - x if r & 4 else x, 1 - y if r & 2 else y, 1 - c if r & 1 else c) for r in range(1, NDEV)]
        barrier = pltpu.get_barrier_semaphore()
        for peer in peers:
            pl.semaphore_signal(barrier, inc=1, device_id=peer, device_id_type=MESH)
        pl.semaphore_wait(barrier, NDEV - 1)
        local = [pltpu.make_async_copy(ins[k].at[me], outs[k].at[me], loc_sems.at[k]) for k in range(n)]
        copies = [pltpu.make_async_remote_copy(
            src_ref=ins[k].at[_dev_index(*peer)], dst_ref=outs[k].at[me], send_sem=send_sems.at[r, k],
            recv_sem=recv_sems.at[r, k], device_id=peer, device_id_type=MESH)
            for r, peer in enumerate(peers) for k in range(n)]
        for cp in local + copies:
            cp.start()
        for r, peer in enumerate(peers):
            for k in range(n):
                pltpu.make_async_remote_copy(
                    src_ref=ins[k].at[me], dst_ref=outs[k].at[_dev_index(*peer)], send_sem=send_sems.at[r, k],
                    recv_sem=recv_sems.at[r, k], device_id=peer, device_id_type=MESH).wait_recv()
        for cp in copies:
            cp.wait_send()
        for cp in local:
            cp.wait()

    return pl.kernel(
        launch, out_type=[jax.ShapeDtypeStruct(a.shape, a.dtype) for a in partials],
        mesh=plsc.ScalarSubcoreMesh(axis_name="sequencer", num_cores=1), name="exchange_square",
        scratch_types=(pltpu.SemaphoreType.DMA((NDEV - 1, n)), pltpu.SemaphoreType.DMA((NDEV - 1, n)),
                       pltpu.SemaphoreType.DMA((n,))),
        compiler_params=pltpu.CompilerParams(collective_id=3),
    )(*partials)


def _exchange_small(small):
    def body(sm_ref, out_ref, send_sems, recv_sems):
        (x, y, c), _, _ = _place()
        me = _dev_index(x, y, c)
        peers = [(1 - x if r & 4 else x, 1 - y if r & 2 else y, 1 - c if r & 1 else c) for r in range(1, NDEV)]
        out_ref[me] = sm_ref[...]
        copies = [pltpu.make_async_remote_copy(
            src_ref=sm_ref, dst_ref=out_ref.at[me], send_sem=send_sems.at[r], recv_sem=recv_sems.at[r],
            device_id=peer, device_id_type=MESH) for r, peer in enumerate(peers)]
        for cp in copies:
            cp.start()
        for r, peer in enumerate(peers):
            pltpu.make_async_remote_copy(
                src_ref=sm_ref, dst_ref=out_ref.at[_dev_index(*peer)], send_sem=send_sems.at[r], recv_sem=recv_sems.at[r],
                device_id=peer, device_id_type=MESH).wait_recv()
        for cp in copies:
            cp.wait_send()

    vm = pl.BlockSpec(memory_space=pltpu.VMEM)
    return pl.pallas_call(
        body, out_shape=jax.ShapeDtypeStruct((NDEV,) + small.shape, F32), in_specs=[vm], out_specs=vm,
        scratch_shapes=[pltpu.SemaphoreType.DMA((NDEV - 1,)), pltpu.SemaphoreType.DMA((NDEV - 1,))],
        name="exchange_small",
    )(small)


def _inproj_fwd(xn, w_pairs, proj, chip, tb):
    t = xn.shape[0]
    nblk, _, nb = w_pairs.shape

    def body(chip_ref, xn_ref, w_ref, proj_in, proj_ref):
        proj_ref[...] = _mm(xn_ref[...], w_ref[0])

    def other(j, chip_ref):
        return j + (j >= chip_ref[0]).astype(jnp.int32)

    return pl.pallas_call(
        body,
        grid_spec=pltpu.PrefetchScalarGridSpec(
            num_scalar_prefetch=1, grid=(t // tb, nblk - 1),
            in_specs=[pl.BlockSpec((tb, D), lambda i, j, chip_ref: (i, 0)),
                      pl.BlockSpec((1, D, nb), lambda i, j, chip_ref: (other(j, chip_ref), 0, 0)),
                      pl.BlockSpec(memory_space=pl.ANY)],
            out_specs=pl.BlockSpec((tb, nb), lambda i, j, chip_ref: (i, other(j, chip_ref)))),
        out_shape=jax.ShapeDtypeStruct((t, DIN), F32),
        input_output_aliases={3: 0},
        name="inproj_fwd", compiler_params=_cp(56),
    )(chip, xn, w_pairs, proj)


def _lower_bound(lb_ref):
    a0 = lb_ref[0:1, :]
    a1 = lb_ref[1:2, :]
    mx = jnp.maximum(a0, a1)
    e0 = jnp.exp(a0 - mx)
    e1 = jnp.exp(a1 - mx)
    return e0 / (e0 + e1)


def _hgrn_chunk_fwd(hq, hf, lb, tril):
    sg = _sigmoid(hf)
    f = lb + (1.0 - lb) * sg
    g = jnp.log(f)
    k = 1.0 - f
    sq = _sigmoid(hq)
    q = hq * sq
    b = _mm_tri_exact(tril, g)
    last_row = lax.broadcasted_iota(jnp.int32, b.shape, 0) == CH - 1
    b_last = jnp.sum(jnp.where(last_row, b, 0.0), axis=0, keepdims=True)
    c = 0.5 * b_last
    eb = jnp.exp(b)
    ea = jnp.exp(b - c)
    ek = jnp.exp(c - b)
    ed = jnp.exp(b_last - b)
    ebl = jnp.exp(b_last)
    return dict(sg=sg, f=f, k=k, sq=sq, q=q, eb=eb, ea=ea, ek=ek, ed=ed, ebl=ebl,
                qe=q * eb, qa=q * ea, ka=k * ek, kd=k * ed)


def _tri(lower):
    r = lax.broadcasted_iota(jnp.int32, (CH, CH), 0)
    c = lax.broadcasted_iota(jnp.int32, (CH, CH), 1)
    return (r >= c) if lower else (c >= r)


def _head_segment(p_ref, rows, j, hg):
    return p_ref[rows, j * HD * hg:(j + 1) * HD * hg]


def _head(a, k):
    return a[:, k * HD:(k + 1) * HD]


def _hgrn_fwd(proj, lbw, rb, hg):
    assert hg == HEADS
    t = proj.shape[0]
    ncb = rb // CH

    def body(p_ref, lb_ref, o_ref, st_ref, s_scr):
        @pl.when(pl.program_id(1) == 0)
        def _():
            s_scr[...] = jnp.zeros_like(s_scr)

        lb = _lower_bound(lb_ref)
        causal = _tri(True)
        tril = _tri3(True)
        heads = range(hg)

        def chunk(cc, carry):
            r0 = pl.multiple_of(cc * CH, CH)
            rows = pl.ds(r0, CH)
            e = _hgrn_chunk_fwd(_head_segment(p_ref, rows, 0, hg), _head_segment(p_ref, rows, 1, hg), lb, tril)
            v = _bf(_head_segment(p_ref, rows, 2, hg))
            sts = [s_scr[k] for k in heads]
            qa, ka, qe, kd = _bf(e["qa"]), _bf(e["ka"]), _bf(e["qe"]), _bf(e["kd"])
            a = [_bf(jnp.where(causal, _mm_nt(_head(qa, k), _head(ka, k)), 0.0)) for k in heads]
            o_inter = [_mm_nt(_head(qe, k), _bf(sts[k])) for k in heads]
            kv = [_mm_tn(_head(v, k), _head(kd, k)) for k in heads]
            o_intra = [_mm(a[k], _head(v, k)) for k in heads]
            for k in heads:
                st_ref[cc, k] = sts[k]
                o_ref[rows, k * HD:(k + 1) * HD] = o_inter[k] + o_intra[k]
                s_scr[k] = sts[k] * _head(e["ebl"], k) + kv[k]
            return carry

        lax.fori_loop(0, ncb, chunk, 0, unroll=4)

    return pl.pallas_call(
        body, grid=(HEADS // hg, t // rb),
        in_specs=[pl.BlockSpec((rb, 3 * HD * hg), lambda h, i: (i, h)), pl.BlockSpec((2, HD * hg), lambda h, i: (0, h))],
        out_specs=[pl.BlockSpec((rb, HD * hg), lambda h, i: (i, h)),
                   pl.BlockSpec((ncb, hg, HD, HD), lambda h, i: (i, h, 0, 0))],
        out_shape=[jax.ShapeDtypeStruct((t, D), F32), jax.ShapeDtypeStruct((t // CH, HEADS, HD, HD), F32)],
        scratch_shapes=[pltpu.VMEM((hg, HD, HD), F32)],
        name="hgrn_fwd", compiler_params=_cp(48),
    )(proj, lbw)


def _hgrn_bwd(proj, lbw, do_raw, states, rb, hg):
    assert hg == HEADS
    t = proj.shape[0]
    nblk = t // rb
    ncb = rb // CH
    wd = HD * hg

    def body(p_ref, lb_ref, do_ref, st_ref, dp_ref, dlb_ref, ds_scr):
        @pl.when(pl.program_id(1) == 0)
        def _():
            ds_scr[...] = jnp.zeros_like(ds_scr)
            dlb_ref[...] = jnp.zeros_like(dlb_ref)

        lb = _lower_bound(lb_ref)
        causal = _tri(True)
        tril = _tri3(True)
        triu = _tri3(False)
        last_row = lax.broadcasted_iota(jnp.int32, (CH, HD * hg), 0) == CH - 1
        row0 = lax.broadcasted_iota(jnp.int32, (8, HD * hg), 0) == 0
        heads = range(hg)
        wide = lambda parts: jnp.concatenate(parts, axis=1)

        def chunk(it, carry):
            cc = ncb - 1 - it
            r0 = pl.multiple_of(cc * CH, CH)
            rows = pl.ds(r0, CH)
            hq = _head_segment(p_ref, rows, 0, hg)
            e = _hgrn_chunk_fwd(hq, _head_segment(p_ref, rows, 1, hg), lb, tril)
            v = _bf(_head_segment(p_ref, rows, 2, hg))
            do = _bf(do_ref[rows, :])
            sts = [st_ref[cc, k] for k in heads]
            dsts = [ds_scr[k] for k in heads]
            dlb_acc = dlb_ref[...]
            qa, ka, qe, kd = _bf(e["qa"]), _bf(e["ka"]), _bf(e["qe"]), _bf(e["kd"])
            a = [_bf(jnp.where(causal, _mm_nt(_head(qa, k), _head(ka, k)), 0.0)) for k in heads]
            da = [_bf(jnp.where(causal, _mm_nt(_head(do, k), _head(v, k)), 0.0)) for k in heads]
            dqe = wide([_mm(_head(do, k), _bf(sts[k])) for k in heads])
            dkd = wide([_mm(_head(v, k), _bf(dsts[k])) for k in heads])
            dv_state = [_mm_nt(_head(kd, k), _bf(dsts[k])) for k in heads]
            ds_new = [_mm_tn(_head(do, k), _head(qe, k)) for k in heads]
            dv_intra = [_mm_tn(a[k], _head(do, k)) for k in heads]
            dqa = wide([_mm(da[k], _head(ka, k)) for k in heads])
            dka = wide([_mm_tn(da[k], _head(qa, k)) for k in heads])
            dv = wide([dv_intra[k] + dv_state[k] for k in heads])
            dbl = e["ebl"] * wide([jnp.sum(sts[k] * dsts[k], axis=0, keepdims=True) for k in heads])
            dq = dqe * e["eb"] + dqa * e["ea"]
            dk = dka * e["ek"] + dkd * e["ed"]
            dkd_kd = dkd * kd.astype(F32)
            db = dqe * qe.astype(F32) + dqa * qa.astype(F32) - dka * ka.astype(F32) - dkd_kd
            db = db + jnp.where(last_row, dbl + jnp.sum(dkd_kd, axis=0, keepdims=True), 0.0)
            dg = _mm_tri_exact(triu, db)
            df = dg / e["f"] - dk
            sg = e["sg"]
            sq = e["sq"]
            dhq = _bf(dq * (sq * (1.0 + hq * (1.0 - sq))))
            dhf = _bf(df * (1.0 - lb) * sg * (1.0 - sg))
            dhi = _bf(dv)
            dlb_new = dlb_acc + jnp.where(row0, jnp.sum(df * (1.0 - sg), axis=0, keepdims=True), 0.0)
            for k in heads:
                ds_scr[k] = ds_new[k] + dsts[k] * _head(e["ebl"], k)
            dp_ref[rows, 0:wd] = dhq
            dp_ref[rows, wd:2 * wd] = dhf
            dp_ref[rows, 2 * wd:3 * wd] = dhi
            dlb_ref[...] = dlb_new
            return carry

        lax.fori_loop(0, ncb, chunk, 0, unroll=2)

    rev = lambda h, i: (nblk - 1 - i, h)
    return pl.pallas_call(
        body, grid=(HEADS // hg, nblk),
        in_specs=[pl.BlockSpec((rb, 3 * HD * hg), rev), pl.BlockSpec((2, HD * hg), lambda h, i: (0, h)),
                  pl.BlockSpec((rb, HD * hg), rev), pl.BlockSpec((ncb, hg, HD, HD), lambda h, i: (nblk - 1 - i, h, 0, 0))],
        out_specs=[pl.BlockSpec((rb, 3 * HD * hg), rev), pl.BlockSpec((8, HD * hg), lambda h, i: (0, h))],
        out_shape=[jax.ShapeDtypeStruct((t, 3 * D), BF16), jax.ShapeDtypeStruct((8, D), F32)],
        scratch_shapes=[pltpu.VMEM((hg, HD, HD), F32)],
        name="hgrn_bwd", compiler_params=_cp(48),
    )(proj, lbw, do_raw, states)


def _kv_variants(tile, odd):
    low = lax.broadcasted_iota(jnp.int32, tile.shape, 1) < 64
    if odd:
        hi = jnp.where(low, 0.0, tile)
        lo = pltpu.roll(hi, 64, 1)
    else:
        lo = jnp.where(low, tile, 0.0)
        hi = pltpu.roll(lo, 64, 1)
    return _bf(lo), _bf(hi)


def _attn_masks(n):
    qi = lax.broadcasted_iota(jnp.int32, (AB, AB), 0)
    kj = lax.broadcasted_iota(jnp.int32, (AB, AB), 1)
    cur = kj <= qi
    return cur, cur | (n > 0), qi <= kj


def _kv_all(prev_ref, cur_ref):
    out = []
    for tl in range(2):
        cols = slice(tl * 128, (tl + 1) * 128)
        tile = jnp.concatenate([prev_ref[:, cols], cur_ref[:, cols]], axis=0)
        out.append(_kv_variants(tile, 0))
        out.append(_kv_variants(tile, 1))
    return out


def _window(a2, cur):
    return jnp.where(cur, a2[:, AB:], a2[:, :AB])


def _attn_softmax(scores, sinks, cur, ok):
    s = [jnp.where(ok, _window(s2, cur) * ATT_SCALE, NEG) for s2 in scores]
    m = [jnp.maximum(jnp.max(si, axis=-1, keepdims=True), sink) for si, sink in zip(s, sinks)]
    p = [jnp.exp(si - mi) for si, mi in zip(s, m)]
    es = [jnp.exp(sink - mi) for sink, mi in zip(sinks, m)]
    inv = [1.0 / (jnp.sum(pi, axis=-1, keepdims=True) + ei) for pi, ei in zip(p, es)]
    return [pi * ii for pi, ii in zip(p, inv)], [ei * ii for ei, ii in zip(es, inv)]


def _spread(pc, cur):
    return jnp.concatenate([jnp.where(cur, 0.0, pc), jnp.where(cur, pc, 0.0)], axis=1)


def _spread_t(pct, cur_t):
    return jnp.concatenate([jnp.where(cur_t, 0.0, pct), jnp.where(cur_t, pct, 0.0)], axis=0)


def _attn_fwd(proj, sinks):
    t = proj.shape[0]
    nb = t // AB

    def body(q_ref, kc_ref, kp_ref, vc_ref, vp_ref, sink_ref, o_ref):
        cur, ok, _ = _attn_masks(pl.program_id(0))
        kvars = _kv_all(kp_ref, kc_ref)
        vvars = _kv_all(vp_ref, vc_ref)
        qps = [_bf(q_ref[:, 128 * j:128 * (j + 1)]) for j in range(8)]
        heads = [(j, ab) for j in range(8) for ab in range(2)]
        scores = [_mm_nt(qps[j], kvars[j // 2][ab]) for j, ab in heads]
        pcs, _ = _attn_softmax(scores, [sink_ref[0, h] for h in range(QH)], cur, ok)
        parts = [_mm(_bf(_spread(pcs[h], cur)), vvars[j // 2][ab]) for h, (j, ab) in enumerate(heads)]
        for j in range(8):
            o_ref[:, 128 * j:128 * (j + 1)] = parts[2 * j] + parts[2 * j + 1]

    prev = lambda n: jnp.maximum(n - 1, 0)
    return pl.pallas_call(
        body, grid=(nb,),
        in_specs=[pl.BlockSpec((AB, D), lambda n: (n, C_AQ // D)),
                  pl.BlockSpec((AB, 256), lambda n: (n, C_AK // 256)),
                  pl.BlockSpec((AB, 256), lambda n: (prev(n), C_AK // 256)),
                  pl.BlockSpec((AB, 256), lambda n: (n, C_AV // 256)),
                  pl.BlockSpec((AB, 256), lambda n: (prev(n), C_AV // 256)),
                  pl.BlockSpec(memory_space=pltpu.SMEM)],
        out_specs=pl.BlockSpec((AB, D), lambda n: (n, 0)),
        out_shape=jax.ShapeDtypeStruct((t, D), F32),
        name="attn_fwd", compiler_params=_cp(32),
    )(proj, proj, proj, proj, proj, sinks)


def _attn_bwd(proj, sinks, do_a):
    t = proj.shape[0]
    nb = t // AB

    def body(q_ref, kc_ref, kp_ref, vc_ref, vp_ref, do_ref, sink_ref, dq_ref, dkv_ref, dsink_ref, carry):
        n = pl.program_id(0)

        @pl.when(n == 0)
        def _():
            dsink_ref[...] = jnp.zeros_like(dsink_ref)
            carry[...] = jnp.zeros_like(carry)

        @pl.when(n < nb)
        def _():
            cur, ok, cur_t = _attn_masks(n)
            low = lax.broadcasted_iota(jnp.int32, (2 * AB, 128), 1) < 64
            lane = lax.broadcasted_iota(jnp.int32, (8, 128), 1)
            row0 = lax.broadcasted_iota(jnp.int32, (8, 128), 0) == 0
            kvars = _kv_all(kp_ref, kc_ref)
            vvars = _kv_all(vp_ref, vc_ref)
            qps = [_bf(q_ref[:, 128 * j:128 * (j + 1)]) for j in range(8)]
            dops = [_bf(do_ref[:, 128 * j:128 * (j + 1)]) for j in range(8)]
            heads = [(j, ab) for j in range(8) for ab in range(2)]
            scores = [_mm_nt(qps[j], kvars[j // 2][ab]) for j, ab in heads]
            dps = [_mm_nt(dops[j], vvars[j // 2][ab]) for j, ab in heads]
            pcs, pss = _attn_softmax(scores, [sink_ref[0, h] for h in range(QH)], cur, ok)
            dpcs = [_window(dp2, cur) for dp2 in dps]
            rss = [jnp.sum(pc * dpc, axis=-1, keepdims=True) for pc, dpc in zip(pcs, dpcs)]
            dscs = [pc * (dpc - rs) for pc, dpc, rs in zip(pcs, dpcs, rss)]
            dsink = jnp.zeros((8, 128), F32)
            for h in range(QH):
                dsink = dsink + jnp.where(row0 & (lane == h), -jnp.sum(pss[h] * rss[h]), 0.0)
            dq_terms = [_mm(_bf(_spread(dscs[h], cur)), kvars[j // 2][ab]) for h, (j, ab) in enumerate(heads)]
            for j in range(8):
                dq_ref[:, 128 * j:128 * (j + 1)] = _bf((dq_terms[2 * j] + dq_terms[2 * j + 1]) * ATT_SCALE)
            dsc_t = [_bf(_spread_t(dsc.T, cur_t)) for dsc in dscs]
            pc_t = [_bf(_spread_t(pc.T, cur_t)) for pc in pcs]
            dk_terms = [_mm(dsc_t[h], qps[j]) for h, (j, ab) in enumerate(heads)]
            dv_terms = [_mm(pc_t[h], dops[j]) for h, (j, ab) in enumerate(heads)]
            dk_ab = [[dk_terms[4 * g + ab] + dk_terms[4 * g + 2 + ab] for ab in range(2)] for g in range(4)]
            dv_ab = [[dv_terms[4 * g + ab] + dv_terms[4 * g + 2 + ab] for ab in range(2)] for g in range(4)]
            for tl in range(2):
                ke, ko = dk_ab[2 * tl], dk_ab[2 * tl + 1]
                ve, vo = dv_ab[2 * tl], dv_ab[2 * tl + 1]
                dkt = (jnp.where(low, ke[0], 0.0) + pltpu.roll(jnp.where(low, 0.0, ke[1]), 64, 1)
                       + jnp.where(low, 0.0, ko[1]) + pltpu.roll(jnp.where(low, ko[0], 0.0), 64, 1)) * ATT_SCALE
                dvt = (jnp.where(low, ve[0], 0.0) + pltpu.roll(jnp.where(low, 0.0, ve[1]), 64, 1)
                       + jnp.where(low, 0.0, vo[1]) + pltpu.roll(jnp.where(low, vo[0], 0.0), 64, 1))
                kcols = slice(tl * 128, (tl + 1) * 128)
                vcols = slice(256 + tl * 128, 256 + (tl + 1) * 128)
                dkv_ref[:, kcols] = _bf(carry[:, kcols] + dkt[0:AB])
                dkv_ref[:, vcols] = _bf(carry[:, vcols] + dvt[0:AB])
                carry[:, kcols] = dkt[AB:2 * AB]
                carry[:, vcols] = dvt[AB:2 * AB]
            dsink_ref[...] += dsink

        @pl.when(n == nb)
        def _():
            dkv_ref[...] = _bf(carry[...])

    cur = lambda n: jnp.minimum(n, nb - 1)
    prev = lambda n: jnp.clip(n - 1, 0, nb - 1)
    return pl.pallas_call(
        body, grid=(nb + 1,),
        in_specs=[pl.BlockSpec((AB, D), lambda n: (cur(n), C_AQ // D)),
                  pl.BlockSpec((AB, 256), lambda n: (cur(n), C_AK // 256)),
                  pl.BlockSpec((AB, 256), lambda n: (prev(n), C_AK // 256)),
                  pl.BlockSpec((AB, 256), lambda n: (cur(n), C_AV // 256)),
                  pl.BlockSpec((AB, 256), lambda n: (prev(n), C_AV // 256)),
                  pl.BlockSpec((AB, D), lambda n: (cur(n), 0)),
                  pl.BlockSpec(memory_space=pltpu.SMEM)],
        out_specs=[pl.BlockSpec((AB, D), lambda n: (cur(n), 0)),
                   pl.BlockSpec((AB, 512), lambda n: (prev(n), 0)),
                   pl.BlockSpec((8, 128), lambda n: (0, 0))],
        out_shape=[jax.ShapeDtypeStruct((t, D), BF16), jax.ShapeDtypeStruct((t, 512), BF16),
                   jax.ShapeDtypeStruct((8, 128), F32)],
        scratch_shapes=[pltpu.VMEM((AB, 512), F32)],
        name="attn_bwd", compiler_params=_cp(32),
    )(proj, proj, proj, proj, proj, do_a, sinks)


def _silu_and_grad(v):
    s = _sigmoid(v)
    return v * s, s * (1.0 + v * (1.0 - s))


def _tail(o_raw, o_a, proj, x2d, tgt, wbh, wba, wout, hnw, fnw, tb):
    t = x2d.shape[0]

    def body(or_ref, oa_ref, hg_ref, ag0, ag1, mh0, mh1, ma0, ma1, x_ref, t_ref, wbh_ref, wba_ref, wout_ref, hnw_ref,
             fnw_ref, dx2_ref, dor_ref, doa_ref, dhg_ref, dagm_ref, gh_ref, ga_ref, mg_ref, dyh_ref, dya_ref, dx2b_ref,
             sums_ref):
        @pl.when(pl.program_id(0) == 0)
        def _():
            sums_ref[...] = jnp.zeros_like(sums_ref)

        halves = lambda a, b: jnp.concatenate([a[...], b[...]], axis=1)
        hnw_v = hnw_ref[...]
        fnw_v = fnw_ref[...]
        o = or_ref[...]
        rs, xhs = [], []
        for h in range(HEADS):
            oh = o[:, h * HD:(h + 1) * HD]
            r = lax.rsqrt(jnp.mean(oh * oh, axis=-1, keepdims=True) + EPS)
            rs.append(r)
            xhs.append(oh * r)
        xh = jnp.concatenate(xhs, axis=1)
        on = xh * hnw_v
        sil_hg, dsil_hg = _silu_and_grad(hg_ref[...])
        gh_b = _bf(on * sil_hg)
        y_h = _mm(gh_b, wbh_ref[...])
        oa = oa_ref[...]
        sil_ag, dsil_ag = _silu_and_grad(halves(ag0, ag1))
        ga_b = _bf(oa * sil_ag)
        y_a = _mm(ga_b, wba_ref[...])
        s_mh = _sigmoid(halves(mh0, mh1))
        s_ma = _sigmoid(halves(ma0, ma1))
        mg_b = _bf(s_mh * y_h + s_ma * y_a)
        x2 = x_ref[...] + _mm(mg_b, wout_ref[...])
        r2 = lax.rsqrt(jnp.mean(x2 * x2, axis=-1, keepdims=True) + EPS)
        xh2 = x2 * r2
        err = xh2 * fnw_v - t_ref[...]
        loss = 0.5 * jnp.sum(jnp.mean(err * err, axis=-1, keepdims=True))
        dy = err * (1.0 / D)
        dfnw = jnp.sum(dy * xh2, axis=0, keepdims=True)
        dxh2 = dy * fnw_v
        dx2 = r2 * (dxh2 - xh2 * jnp.mean(dxh2 * xh2, axis=-1, keepdims=True))
        dx2_ref[...] = dx2
        dx2_b = _bf(dx2)
        dmg = _mm_nt(dx2_b, wout_ref[...])
        dmg_h = dmg * s_mh
        dmg_a = dmg * s_ma
        dyh_b = _bf(dmg_h)
        dya_b = _bf(dmg_a)
        dagm_ref[:, D:2 * D] = _bf(dmg_h * y_h * (1.0 - s_mh))
        dagm_ref[:, 2 * D:3 * D] = _bf(dmg_a * y_a * (1.0 - s_ma))
        dgh = _mm_nt(dyh_b, wbh_ref[...])
        dga = _mm_nt(dya_b, wba_ref[...])
        doa_ref[...] = dga * sil_ag
        dagm_ref[:, 0:D] = _bf(dga * oa * dsil_ag)
        dhg_ref[...] = _bf(dgh * on * dsil_hg)
        don = dgh * sil_hg
        dhnw = jnp.sum(don * xh, axis=0, keepdims=True)
        dxh = don * hnw_v
        dos = []
        for h in range(HEADS):
            sl = slice(h * HD, (h + 1) * HD)
            dos.append(rs[h] * (dxh[:, sl] - xhs[h] * jnp.mean(dxh[:, sl] * xhs[h], axis=-1, keepdims=True)))
        dor_ref[...] = jnp.concatenate(dos, axis=1)
        gh_ref[...] = gh_b
        ga_ref[...] = ga_b
        mg_ref[...] = mg_b
        dyh_ref[...] = dyh_b
        dya_ref[...] = dya_b
        dx2b_ref[...] = dx2_b
        row = lax.broadcasted_iota(jnp.int32, (8, D), 0)
        sums_ref[...] += jnp.where(row == 0, dfnw, 0.0) + jnp.where(row == 1, dhnw, 0.0) + jnp.where(row == 2, loss, 0.0)

    rowblk = lambda c: pl.BlockSpec((tb, D), lambda i: (i, c))
    half = lambda c: pl.BlockSpec((tb, 512), lambda i: (i, c))
    full = lambda shape: pl.BlockSpec(shape, lambda i: (0, 0))
    return pl.pallas_call(
        body, grid=(t // tb,),
        in_specs=[rowblk(0), rowblk(0), rowblk(C_HG // D), half(C_AG // 512), half(C_AG // 512 + 1), half(C_MH // 512),
                  half(C_MH // 512 + 1), half(C_MA // 512), half(C_MA // 512 + 1), rowblk(0), rowblk(0),
                  full((D, D)), full((D, D)), full((D, D)), full((1, D)), full((1, D))],
        out_specs=[rowblk(0), rowblk(0), rowblk(0), rowblk(0), pl.BlockSpec((tb, 3 * D), lambda i: (i, 0))]
        + [rowblk(0)] * 6 + [full((8, D))],
        out_shape=[jax.ShapeDtypeStruct((t, D), F32)] * 3
        + [jax.ShapeDtypeStruct((t, D), BF16), jax.ShapeDtypeStruct((t, 3 * D), BF16)]
        + [jax.ShapeDtypeStruct((t, D), BF16)] * 6 + [jax.ShapeDtypeStruct((8, D), F32)],
        name="tail", compiler_params=_cp(56),
    )(o_raw, o_a, proj, proj, proj, proj, proj, proj, proj, x2d, tgt, wbh, wba, wout, hnw, fnw)


def _wgrad3(gh, dyh, ga, dya, mg, dx2b, tk):
    t = dyh.shape[0]

    def body(a0, b0, a1, b1, a2, b2, o0, o1, o2):
        @pl.when(pl.program_id(0) == 0)
        def _():
            o0[...] = jnp.zeros_like(o0)
            o1[...] = jnp.zeros_like(o1)
            o2[...] = jnp.zeros_like(o2)

        o0[...] += _mm_tn(a0[...], b0[...])
        o1[...] += _mm_tn(a1[...], b1[...])
        o2[...] += _mm_tn(a2[...], b2[...])

    blk = pl.BlockSpec((tk, D), lambda k: (k, 0))
    out = pl.BlockSpec((D, D), lambda k: (0, 0))
    return pl.pallas_call(
        body, grid=(t // tk,), in_specs=[blk] * 6, out_specs=[out] * 3,
        out_shape=[jax.ShapeDtypeStruct((D, D), F32)] * 3,
        name="wgrad3", compiler_params=_cp(48),
    )(gh, dyh, ga, dya, mg, dx2b)


def _inproj_wgrad_piece(xnt, piece, nb, name):
    t = xnt.shape[1]
    width = piece.shape[1]

    def body(xnt_ref, p_ref, o_ref):
        o_ref[...] = _mm(xnt_ref[...], p_ref[...])

    return pl.pallas_call(
        body, grid=(width // nb,),
        in_specs=[pl.BlockSpec((D, t), lambda j: (0, 0), pipeline_mode=pl.Buffered(1)),
                  pl.BlockSpec((t, nb), lambda j: (0, j))],
        out_specs=pl.BlockSpec((D, nb), lambda j: (0, j)),
        out_shape=jax.ShapeDtypeStruct((D, width), F32),
        name=name, compiler_params=_cp(56),
    )(xnt, piece)


def _inproj_dgrad(pieces, w_p, x2d, dx2, norm_w, tb, after):
    t = x2d.shape[0]

    def body(*refs):
        piece_refs = refs[:len(pieces)]
        w_ref, x_ref, dx2_ref, nw_ref, _, gx_ref, dnw_ref = refs[len(pieces):]

        @pl.when(pl.program_id(0) == 0)
        def _():
            dnw_ref[...] = jnp.zeros_like(dnw_ref)

        dxn = None
        off = 0
        for p in piece_refs:
            width = p.shape[1]
            for q in range(w_ref.shape[0]):
                lo, hi = max(off, q * PAIR), min(off + width, (q + 1) * PAIR)
                if lo < hi:
                    term = _mm_nt(p[:, lo - off:hi - off], w_ref[q, :, lo - q * PAIR:hi - q * PAIR])
                    dxn = term if dxn is None else dxn + term
            off += width
        xv = x_ref[...]
        r = lax.rsqrt(jnp.mean(xv * xv, axis=-1, keepdims=True) + EPS)
        xh = xv * r
        dxh = dxn * nw_ref[...]
        gx_ref[...] = dx2_ref[...] + r * (dxh - xh * jnp.mean(dxh * xh, axis=-1, keepdims=True))
        row0 = lax.broadcasted_iota(jnp.int32, (8, D), 0) == 0
        dnw_ref[...] += jnp.where(row0, jnp.sum(dxn * xh, axis=0, keepdims=True), 0.0)

    rowblk = pl.BlockSpec((tb, D), lambda i: (i, 0))
    return pl.pallas_call(
        body, grid=(t // tb,),
        in_specs=[pl.BlockSpec((tb, p.shape[1]), lambda i: (i, 0)) for p in pieces]
        + [pl.BlockSpec(w_p.shape, lambda i: (0, 0, 0), pipeline_mode=pl.Buffered(1)), rowblk, rowblk,
           pl.BlockSpec((1, D), lambda i: (0, 0)), pl.BlockSpec(memory_space=pl.ANY)],
        out_specs=[rowblk, pl.BlockSpec((8, D), lambda i: (0, 0))],
        out_shape=[jax.ShapeDtypeStruct((t, D), F32), jax.ShapeDtypeStruct((8, D), F32)],
        name="inproj_dgrad", compiler_params=_cp(60),
    )(*pieces, w_p, x2d, dx2, norm_w, after)


def _adamw_math(w, g, m, v):
    m = B1 * m + (1.0 - B1) * g
    v = B2 * v + (1.0 - B2) * (g * g)
    m_hat = m / (1.0 - B1 ** STEP)
    v_hat = v / (1.0 - B2 ** STEP)
    delta = -LR * (m_hat / (jnp.sqrt(v_hat) + ADAM_EPS) + WD * w)
    return delta, m, v


def _adamw_shard(recv, sums, chip, w, m, v, rows, name):
    nparts, nr, nc = recv.shape

    def body(chip_ref, own_ref, p_ref, w_ref, m_ref, v_ref, g_ref, d_ref, nm_ref, nv_ref):
        g = own_ref[0].astype(F32)
        for s in range(nparts):
            g = g + p_ref[s].astype(F32)
        d, nm, nv = _adamw_math(w_ref[...], g, m_ref[...], v_ref[...])
        g_ref[...] = g
        d_ref[...] = d
        nm_ref[...] = nm
        nv_ref[...] = nv

    blk = pl.BlockSpec((rows, nc), lambda i, chip_ref: (i, 0))
    return pl.pallas_call(
        body,
        grid_spec=pltpu.PrefetchScalarGridSpec(
            num_scalar_prefetch=1, grid=(nr // rows,),
            in_specs=[pl.BlockSpec((1, rows, nc), lambda i, chip_ref: (chip_ref[0], i, 0)),
                      pl.BlockSpec((nparts, rows, nc), lambda i, chip_ref: (0, i, 0)), blk, blk, blk],
            out_specs=[blk] * 4),
        out_shape=[jax.ShapeDtypeStruct((nr, nc), F32)] * 4,
        name=name, compiler_params=_cp(48),
    )(chip, sums, recv, w, m, v)


def _adamw_sum8(parts, w, m, v, after, name):
    def body(p_ref, w_ref, m_ref, v_ref, _, g_ref, d_ref, nm_ref, nv_ref):
        g = p_ref[0].astype(F32)
        for s in range(1, NDEV):
            g = g + p_ref[s].astype(F32)
        d, nm, nv = _adamw_math(w_ref[...], g, m_ref[...], v_ref[...])
        g_ref[...] = g
        d_ref[...] = d
        nm_ref[...] = nm
        nv_ref[...] = nv

    vm = pl.BlockSpec(memory_space=pltpu.VMEM)
    return pl.pallas_call(
        body, out_shape=[jax.ShapeDtypeStruct(w.shape, F32)] * 4,
        in_specs=[vm, vm, vm, vm, pl.BlockSpec(memory_space=pl.ANY)], out_specs=[vm] * 4, name=name,
    )(parts, w, m, v, after)


SMALL_ROWS = dict(norm_w=0, lower_bound=1, hgrn_norm_w=3, final_norm_w=4, sinks=5, loss=6)


def _pack_small_grads(dnw, dlb, sums, dsink):
    def body(dnw_ref, dlb_ref, sums_ref, dsink_ref, o_ref):
        o_ref[...] = jnp.zeros_like(o_ref)
        o_ref[0:1, :] = dnw_ref[0:1, :]
        o_ref[1:2, :] = dlb_ref[0:1, :]
        o_ref[3:4, :] = sums_ref[1:2, :]
        o_ref[4:5, :] = sums_ref[0:1, :]
        o_ref[5:6, 0:128] = dsink_ref[0:1, :]
        o_ref[6:7, :] = sums_ref[2:3, :]

    return pl.pallas_call(body, out_shape=jax.ShapeDtypeStruct((8, D), F32), name="pack_small_grads")(dnw, dlb, sums, dsink)


def _adamw_small(parts, ws, ms, vs):
    shapes = [a.shape for a in ws]

    def body(p_ref, *refs):
        w, m, v = refs[0:5], refs[5:10], refs[10:15]
        outs = [refs[15 + 5 * i:20 + 5 * i] for i in range(4)]
        loss_ref = refs[35]

        def total(row, width):
            g = p_ref[0, row:row + 1, 0:width]
            for s in range(1, NDEV):
                g = g + p_ref[s, row:row + 1, 0:width]
            return g

        loss_ref[...] = total(6, 128)
        lb = _lower_bound(w[1])
        ga0 = total(1, D) * lb * (1.0 - lb)
        grads = [total(0, D), None, total(3, D), total(4, D), total(5, QH)]
        for i in (0, 2, 3, 4):
            res = (grads[i],) + _adamw_math(w[i][...], grads[i], m[i][...], v[i][...])
            for o, val in zip(outs, res):
                o[i][...] = val
        for r, g in ((0, ga0), (1, -ga0)):
            res = (g,) + _adamw_math(w[1][r:r + 1, :], g, m[1][r:r + 1, :], v[1][r:r + 1, :])
            for o, val in zip(outs, res):
                o[1][r:r + 1, :] = val

    res = pl.pallas_call(
        body, out_shape=[jax.ShapeDtypeStruct(s, F32) for s in shapes] * 4 + [jax.ShapeDtypeStruct((1, 128), F32)],
        name="adamw_small",
    )(parts, *ws, *ms, *vs)
    return [res[5 * i:5 * i + 5] for i in range(4)], res[20][0, 0]


def kernel(x, norm_w, w_in, hgrn_lower_bound, hgrn_norm_w, w_branch_hgrn, attn_sinks, w_branch_attn, w_out, final_norm_w, loss_target, m_norm_w, m_w_in, m_hgrn_lower_bound, m_hgrn_norm_w, m_w_branch_hgrn, m_attn_sinks, m_w_branch_attn, m_w_out, m_final_norm_w, v_norm_w, v_w_in, v_hgrn_lower_bound, v_hgrn_norm_w, v_w_branch_hgrn, v_attn_sinks, v_w_branch_attn, v_w_out, v_final_norm_w):
    t = x.shape[1]
    x2d = x.reshape(t, D)
    tgt = loss_target.reshape(t, D)
    fnw = final_norm_w.reshape(1, D)
    row_blk = min(256, t)
    big_blk = min(512, t)

    chip = (2 * lax.axis_index("x") + lax.axis_index("y")).astype(jnp.int32).reshape(1)
    w_p, xn, xnt, proj_own = _gather_in_projection(w_in[0], x2d, norm_w)
    wbh, wba, wout = (g.reshape(D, D) for g in _gather_square(
        [w_branch_hgrn[0].astype(BF16), w_branch_attn[0].astype(BF16), w_out[0].astype(BF16)], after=w_p))

    proj = _inproj_fwd(xn, w_p, proj_own, chip, min(1024, t))
    o_raw, states = _hgrn_fwd(proj, hgrn_lower_bound, big_blk, HGRN_GROUP)
    o_a = _attn_fwd(proj, attn_sinks)
    (dx2, do_raw, do_a, d_hg, d_agm, gh, ga, mg, dyh, dya, dx2b, sums) = _tail(
        o_raw, o_a, proj, x2d, tgt, wbh, wba, wout, hgrn_norm_w, fnw, row_blk)
    dwbh, dwba, dwout = _wgrad3(gh, dyh, ga, dya, mg, dx2b, big_blk)
    d_aq, d_kv, dsink = _attn_bwd(proj, attn_sinks, do_a)
    d_hgrn, dlb = _hgrn_bwd(proj, hgrn_lower_bound, do_raw, states, big_blk, HGRN_GROUP)
    pieces = (d_hgrn, d_hg, d_aq, d_kv, d_agm)
    dw_pieces = [_inproj_wgrad_piece(xnt, p, CB, "inproj_wgrad_" + n)
                 for p, n in zip(pieces, ("hgrn", "hgate", "aq", "kv", "gates"))]

    dwin_r = jnp.concatenate(dw_pieces, axis=1).reshape(D, NDEV, IN_SHARD).transpose(1, 0, 2).astype(BF16)
    slots = lambda a: a.reshape(NDEV, ROW_SHARD, D).astype(BF16)
    got = _exchange_pair([dwin_r])
    core = lax.axis_index("c").astype(jnp.int32).reshape(1)
    s_in = _pair_sum(dwin_r, got[0], core, ROW_SHARD, "pair_sum_w_in")
    rin, = _exchange_chips([s_in])
    rbh, rba, rout = _exchange_square([slots(dwbh), slots(dwba), slots(dwout)])
    grad_x, dnw = _inproj_dgrad(pieces, w_p, x2d, dx2, norm_w, big_blk, after=s_in)
    rsm = _exchange_small(_pack_small_grads(dnw, dlb, sums, dsink))
    g_in, d_in, nm_in, nv_in = _adamw_shard(rin, s_in, chip, w_in[0], m_w_in[0], v_w_in[0], 128, "adamw_w_in")
    g_bh, d_bh, nm_bh, nv_bh = _adamw_sum8(
        rbh, w_branch_hgrn[0], m_w_branch_hgrn[0], v_w_branch_hgrn[0], dnw, "adamw_w_bh")
    g_ba, d_ba, nm_ba, nv_ba = _adamw_sum8(
        rba, w_branch_attn[0], m_w_branch_attn[0], v_w_branch_attn[0], dnw, "adamw_w_ba")
    g_out, d_out, nm_out, nv_out = _adamw_sum8(rout, w_out[0], m_w_out[0], v_w_out[0], dnw, "adamw_w_out")
    (sg, sd, sm, sv), loss = _adamw_small(
        rsm,
        (norm_w, hgrn_lower_bound, hgrn_norm_w, fnw, attn_sinks),
        (m_norm_w, m_hgrn_lower_bound, m_hgrn_norm_w, m_final_norm_w.reshape(1, D), m_attn_sinks),
        (v_norm_w, v_hgrn_lower_bound, v_hgrn_norm_w, v_final_norm_w.reshape(1, D), v_attn_sinks))

    def group(s, w_in_v, bh, ba, out):
        nw, lb, hnw, fn, sinks = s
        return (nw, w_in_v[None], lb, hnw, bh[None], sinks, ba[None], out[None], fn.reshape(D))

    return (loss, grad_x.reshape(1, t, D),
            *group(sg, g_in, g_bh, g_ba, g_out), *group(sd, d_in, d_bh, d_ba, d_out),
            *group(sm, nm_in, nm_bh, nm_ba, nm_out), *group(sv, nv_in, nv_bh, nv_ba, nv_out))
```

```python
import jax
import jax.numpy as jnp
from jax import lax
from jax.experimental import pallas as pl
from jax.experimental.pallas import tpu as pltpu
from jax.experimental.pallas import tpu_sc as plsc

F32 = jnp.float32
BF16 = jnp.bfloat16

D = 1024
DIN = 8704
NDEV = 8
IN_SHARD = DIN // NDEV
PAIR = 2 * IN_SHARD
ROW_SHARD = D // NDEV
HEADS = 8
HD = 128
CH = 64
HGRN_GROUP = 8
QH = 16
AB = 128
EPS = 1e-6
NEG = -1e30
ATT_SCALE = 0.125

C_HG = 3072
C_AQ = 4096
C_AK = 5120
C_AV = 5376
C_AG = 5632
C_MH = 6656
C_MA = 7680
CB = 512

LR = 0.001
B1 = 0.9
B2 = 0.999
ADAM_EPS = 1e-08
WD = 0.01
STEP = 10

MESH = pl.DeviceIdType.MESH


def _cp(vmem_mb):
    return pltpu.CompilerParams(vmem_limit_bytes=vmem_mb * 1024 * 1024)


def _mm(a, b):
    return jnp.dot(a, b, preferred_element_type=F32)


def _mm_nt(a, b):
    return lax.dot_general(a, b, (((1,), (1,)), ((), ())), preferred_element_type=F32)


def _mm_tn(a, b):
    return lax.dot_general(a, b, (((0,), (0,)), ((), ())), preferred_element_type=F32)


def _tri3(lower):
    r = lax.broadcasted_iota(jnp.int32, (CH, 3 * CH), 0)
    c = lax.broadcasted_iota(jnp.int32, (CH, 3 * CH), 1)
    c = jnp.where(c >= 2 * CH, c - 2 * CH, jnp.where(c >= CH, c - CH, c))
    return ((r >= c) if lower else (c >= r)).astype(BF16)


def _mm_tri_exact(tri3, g):
    g1 = g.astype(BF16)
    r1 = g - g1.astype(F32)
    g2 = r1.astype(BF16)
    g3 = (r1 - g2.astype(F32)).astype(BF16)
    return _mm(tri3, jnp.concatenate([g1, g2, g3], axis=0))


def _sigmoid(v):
    return 0.5 * jnp.tanh(0.5 * v) + 0.5


def _bf(v):
    return v.astype(BF16)


def _place():
    x, y, c = lax.axis_index("x"), lax.axis_index("y"), lax.axis_index("c")
    return (x, y, c), (x, y, 1 - c), [(1 - x, y), (x, 1 - y), (1 - x, 1 - y)]


def _dev_index(px, py, pc):
    return 4 * px + 2 * py + pc


def _gather_in_projection(w_in_s, x2d, norm_w):
    half = D // 2
    t = x2d.shape[0]
    prep_rows = min(512, t)
    nprep = t // prep_rows

    def body(win_ref, x_hbm, nw_ref, wp_g, xn_hbm, xnt_hbm, proj_hbm, give, take, mine, xbuf, xnbuf, xntbuf, w_own, pbuf,
             send_sems, recv_sems, loc_sem, swap_sems, in_sems, out_sems, own_sem):
        (x, y, c), sibling, chips = _place()
        give[...] = win_ref[pl.ds(pl.multiple_of(half * (1 - c), half), half), :].astype(BF16)
        swap = pltpu.make_async_remote_copy(src_ref=give, dst_ref=take, send_sem=swap_sems.at[0], recv_sem=swap_sems.at[1],
                                            device_id=sibling, device_id_type=MESH)
        swap.start()
        swap.wait()
        own = win_ref[pl.ds(pl.multiple_of(half * c, half), half), :]
        other = take[...].astype(F32)
        mine[...] = jnp.where(c == 0, jnp.concatenate([own, other], axis=1),
                              jnp.concatenate([other, own], axis=1)).astype(BF16)

        def place(px, py, pc):
            return wp_g.at[2 * px + py, pl.ds(pl.multiple_of(half * pc, half), half), :]

        def copy(kind, origin, to, src=mine):
            return pltpu.make_async_remote_copy(
                src_ref=src, dst_ref=place(*origin), send_sem=send_sems.at[kind], recv_sem=recv_sems.at[kind],
                device_id=to, device_id_type=MESH)

        me = (x, y, c)
        local = pltpu.make_async_copy(mine, place(*me), loc_sem)
        local.start()
        first = [copy(0, me, sibling)] + [copy(1 + j, me, (*chip, c)) for j, chip in enumerate(chips)]
        for cp in first:
            cp.start()

        copy(0, (x, y, 1 - c), me).wait_recv()
        local.wait()
        my_chip = 2 * x + y
        fetch = pltpu.make_async_copy(wp_g.at[my_chip], w_own, own_sem)
        fetch.start()

        def rows_of(i):
            return pl.ds(pl.multiple_of(i * prep_rows, prep_rows), prep_rows)

        def load(i, slot):
            return pltpu.make_async_copy(x_hbm.at[rows_of(i), :], xbuf.at[slot], in_sems.at[slot])

        def stores(i, slot):
            own_cols = pl.ds(pl.multiple_of(my_chip * PAIR, 128), PAIR)
            return (pltpu.make_async_copy(xnbuf.at[slot], xn_hbm.at[rows_of(i), :], out_sems.at[slot, 0]),
                    pltpu.make_async_copy(xntbuf.at[slot], xnt_hbm.at[:, rows_of(i)], out_sems.at[slot, 1]),
                    pltpu.make_async_copy(pbuf.at[slot], proj_hbm.at[rows_of(i), own_cols], out_sems.at[slot, 2]))

        load(0, 0).start()
        fetch.wait()

        def prep(i, carry):
            slot = lax.rem(i, 2)
            load(i, slot).wait()

            @pl.when(i + 1 < nprep)
            def _():
                load(i + 1, 1 - slot).start()

            @pl.when(i >= 2)
            def _():
                for cp in stores(i - 2, slot):
                    cp.wait()

            xv = xbuf[slot]
            xn = (xv * lax.rsqrt(jnp.mean(xv * xv, axis=-1, keepdims=True) + EPS)) * nw_ref[...]
            xn_b = xn.astype(BF16)
            xnbuf[slot] = xn_b
            xntbuf[slot] = xn.T.astype(BF16)
            pbuf[slot] = _mm(xn_b, w_own[...])
            for cp in stores(i, slot):
                cp.start()
            return carry

        lax.fori_loop(0, nprep, prep, 0)
        for i in range(max(nprep - 2, 0), nprep):
            for cp in stores(i, i % 2):
                cp.wait()

        passed = []
        for j, chip in enumerate(chips):
            copy(1 + j, (*chip, c), me).wait_recv()
            cp = copy(4 + j, (*chip, c), sibling, src=place(*chip, c))
            cp.start()
            passed.append(cp)
        for j, chip in enumerate(chips):
            copy(4 + j, (*chip, 1 - c), me).wait_recv()
        for cp in first + passed:
            cp.wait_send()

    vm = pl.BlockSpec(memory_space=pltpu.VMEM)
    hbm = pl.BlockSpec(memory_space=pl.ANY)
    return pl.pallas_call(
        body,
        out_shape=[jax.ShapeDtypeStruct((NDEV // 2, D, PAIR), BF16), jax.ShapeDtypeStruct((t, D), BF16),
                   jax.ShapeDtypeStruct((D, t), BF16), jax.ShapeDtypeStruct((t, DIN), F32)],
        in_specs=[vm, hbm, vm],
        out_specs=[hbm, hbm, hbm, hbm],
        scratch_shapes=[pltpu.VMEM((half, IN_SHARD), BF16), pltpu.VMEM((half, IN_SHARD), BF16),
                        pltpu.VMEM((half, PAIR), BF16),
                        pltpu.VMEM((2, prep_rows, D), F32), pltpu.VMEM((2, prep_rows, D), BF16),
                        pltpu.VMEM((2, D, prep_rows), BF16),
                        pltpu.VMEM((D, PAIR), BF16), pltpu.VMEM((2, prep_rows, PAIR), F32),
                        pltpu.SemaphoreType.DMA((NDEV - 1,)), pltpu.SemaphoreType.DMA((NDEV - 1,)),
                        pltpu.SemaphoreType.DMA, pltpu.SemaphoreType.DMA((2,)),
                        pltpu.SemaphoreType.DMA((2,)), pltpu.SemaphoreType.DMA((2, 3)), pltpu.SemaphoreType.DMA],
        name="gather_in_projection", compiler_params=_cp(56),
    )(w_in_s, x2d, norm_w)


def _gather_square(shards, after):
    n = len(shards)

    def launch(*refs):
        ins, outs = refs[:n], refs[n + 1:2 * n + 1]
        send_sems, recv_sems, loc_sems = refs[2 * n + 1:]
        (x, y, c), _, _ = _place()
        me = _dev_index(x, y, c)
        peers = [(1 - x if r & 4 else x, 1 - y if r & 2 else y, 1 - c if r & 1 else c) for r in range(1, NDEV)]
        barrier = pltpu.get_barrier_semaphore()
        for peer in peers:
            pl.semaphore_signal(barrier, inc=1, device_id=peer, device_id_type=MESH)
        pl.semaphore_wait(barrier, NDEV - 1)
        local = [pltpu.make_async_copy(ins[k], outs[k].at[me], loc_sems.at[k]) for k in range(n)]
        copies = [pltpu.make_async_remote_copy(
            src_ref=ins[k], dst_ref=outs[k].at[me], send_sem=send_sems.at[r, k], recv_sem=recv_sems.at[r, k],
            device_id=peer, device_id_type=MESH) for r, peer in enumerate(peers) for k in range(n)]
        for cp in local + copies:
            cp.start()
        for r, peer in enumerate(peers):
            for k in range(n):
                pltpu.make_async_remote_copy(
                    src_ref=ins[k], dst_ref=outs[k].at[_dev_index(*peer)], send_sem=send_sems.at[r, k],
                    recv_sem=recv_sems.at[r, k], device_id=peer, device_id_type=MESH).wait_recv()
        for cp in copies:
            cp.wait_send()
        for cp in local:
            cp.wait()

    return pl.kernel(
        launch, out_type=[jax.ShapeDtypeStruct((NDEV,) + a.shape, a.dtype) for a in shards],
        mesh=plsc.ScalarSubcoreMesh(axis_name="sequencer", num_cores=1), name="gather_square",
        scratch_types=(pltpu.SemaphoreType.DMA((NDEV - 1, n)), pltpu.SemaphoreType.DMA((NDEV - 1, n)),
                       pltpu.SemaphoreType.DMA((n,))),
        compiler_params=pltpu.CompilerParams(collective_id=2),
    )(*shards, after)


def _exchange_pair(arrs):
    n = len(arrs)

    def launch(*refs):
        ins, got = refs[:n], refs[n:2 * n]
        send_sems, recv_sems = refs[2 * n:]
        (x, y, c), sibling, _ = _place()
        barrier = pltpu.get_barrier_semaphore()
        pl.semaphore_signal(barrier, inc=1, device_id=sibling, device_id_type=MESH)
        pl.semaphore_wait(barrier, 1)
        sends = [pltpu.make_async_remote_copy(
            src_ref=ins[k].at[_dev_index(q // 2, q % 2, 1 - c)], dst_ref=got[k].at[q], send_sem=send_sems.at[q, k],
            recv_sem=recv_sems.at[q, k], device_id=sibling, device_id_type=MESH) for q in range(4) for k in range(n)]
        for cp in sends:
            cp.start()
        for cp in sends:
            cp.wait_recv()
        for cp in sends:
            cp.wait_send()

    return pl.kernel(
        launch, out_type=[jax.ShapeDtypeStruct((4,) + a.shape[1:], a.dtype) for a in arrs],
        mesh=plsc.ScalarSubcoreMesh(axis_name="sequencer", num_cores=1), name="exchange_pair",
        scratch_types=(pltpu.SemaphoreType.DMA((4, n)), pltpu.SemaphoreType.DMA((4, n))),
        compiler_params=pltpu.CompilerParams(collective_id=0),
    )(*arrs)


def _pair_sum(full, got, core, rows, name):
    _, nr, nc = got.shape

    def body(core_ref, a_ref, b_ref, o_ref):
        o_ref[...] = (a_ref[...].astype(F32) + b_ref[...].astype(F32)).astype(BF16)

    blk = pl.BlockSpec((1, rows, nc), lambda q, i, core_ref: (q, i, 0))
    return pl.pallas_call(
        body,
        grid_spec=pltpu.PrefetchScalarGridSpec(
            num_scalar_prefetch=1, grid=(4, nr // rows),
            in_specs=[pl.BlockSpec((1, rows, nc), lambda q, i, core_ref: (2 * q + core_ref[0], i, 0)), blk],
            out_specs=blk),
        out_shape=jax.ShapeDtypeStruct(got.shape, BF16), name=name,
    )(core, full, got)


def _exchange_chips(sums):
    n = len(sums)

    def launch(*refs):
        ins, outs = refs[:n], refs[n:2 * n]
        send_sems, recv_sems = refs[2 * n:]
        (x, y, c), _, chips = _place()
        barrier = pltpu.get_barrier_semaphore()
        for px, py in chips:
            pl.semaphore_signal(barrier, inc=1, device_id=(px, py, c), device_id_type=MESH)
        pl.semaphore_wait(barrier, len(chips))
        copies = [pltpu.make_async_remote_copy(
            src_ref=ins[k].at[2 * px + py], dst_ref=outs[k].at[j], send_sem=send_sems.at[j, k],
            recv_sem=recv_sems.at[j, k], device_id=(px, py, c), device_id_type=MESH)
            for j, (px, py) in enumerate(chips) for k in range(n)]
        for cp in copies:
            cp.start()
        for cp in copies:
            cp.wait_recv()
        for cp in copies:
            cp.wait_send()

    return pl.kernel(
        launch, out_type=[jax.ShapeDtypeStruct((3,) + a.shape[1:], a.dtype) for a in sums],
        mesh=plsc.ScalarSubcoreMesh(axis_name="sequencer", num_cores=1), name="exchange_chips",
        scratch_types=(pltpu.SemaphoreType.DMA((3, n)), pltpu.SemaphoreType.DMA((3, n))),
        compiler_params=pltpu.CompilerParams(collective_id=1),
    )(*sums)


def _exchange_square(partials):
    n = len(partials)

    def launch(*refs):
        ins, outs = refs[:n], refs[n:2 * n]
        send_sems, recv_sems, loc_sems = refs[2 * n:]
        (x, y, c), _, _ = _place()
        me = _dev_index(x, y, c)
        peers = [(1 - x if r & 4 else x, 1 - y if r & 2 else y, 1 - c if r & 1 else c) for r in range(1, NDEV)]
        barrier = pltpu.get_barrier_semaphore()
        for peer in peers:
            pl.semaphore_signal(barrier, inc=1, device_id=peer, device_id_type=MESH)
        pl.semaphore_wait(barrier, NDEV - 1)
        local = [pltpu.make_async_copy(ins[k].at[me], outs[k].at[me], loc_sems.at[k]) for k in range(n)]
        copies = [pltpu.make_async_remote_copy(
            src_ref=ins[k].at[_dev_index(*peer)], dst_ref=outs[k].at[me], send_sem=send_sems.at[r, k],
            recv_sem=recv_sems.at[r, k], device_id=peer, device_id_type=MESH)
            for r, peer in enumerate(peers) for k in range(n)]
        for cp in local + copies:
            cp.start()
        for r, peer in enumerate(peers):
            for k in range(n):
                pltpu.make_async_remote_copy(
                    src_ref=ins[k].at[me], dst_ref=outs[k].at[_dev_index(*peer)], send_sem=send_sems.at[r, k],
                    recv_sem=recv_sems.at[r, k], device_id=peer, device_id_type=MESH).wait_recv()
        for cp in copies:
            cp.wait_send()
        for cp in local:
            cp.wait()

    return pl.kernel(
        launch, out_type=[jax.ShapeDtypeStruct(a.shape, a.dtype) for a in partials],
        mesh=plsc.ScalarSubcoreMesh(axis_name="sequencer", num_cores=1), name="exchange_square",
        scratch_types=(pltpu.SemaphoreType.DMA((NDEV - 1, n)), pltpu.SemaphoreType.DMA((NDEV - 1, n)),
                       pltpu.SemaphoreType.DMA((n,))),
        compiler_params=pltpu.CompilerParams(collective_id=3),
    )(*partials)


def _exchange_small(small):
    def body(sm_ref, out_ref, send_sems, recv_sems):
        (x, y, c), _, _ = _place()
        me = _dev_index(x, y, c)
        peers = [(1 - x if r & 4 else x, 1 - y if r & 2 else y, 1 - c if r & 1 else c) for r in range(1, NDEV)]
        out_ref[me] = sm_ref[...]
        copies = [pltpu.make_async_remote_copy(
            src_ref=sm_ref, dst_ref=out_ref.at[me], send_sem=send_sems.at[r], recv_sem=recv_sems.at[r],
            device_id=peer, device_id_type=MESH) for r, peer in enumerate(peers)]
        for cp in copies:
            cp.start()
        for r, peer in enumerate(peers):
            pltpu.make_async_remote_copy(
                src_ref=sm_ref, dst_ref=out_ref.at[_dev_index(*peer)], send_sem=send_sems.at[r], recv_sem=recv_sems.at[r],
                device_id=peer, device_id_type=MESH).wait_recv()
        for cp in copies:
            cp.wait_send()

    vm = pl.BlockSpec(memory_space=pltpu.VMEM)
    return pl.pallas_call(
        body, out_shape=jax.ShapeDtypeStruct((NDEV,) + small.shape, F32), in_specs=[vm], out_specs=vm,
        scratch_shapes=[pltpu.SemaphoreType.DMA((NDEV - 1,)), pltpu.SemaphoreType.DMA((NDEV - 1,))],
        name="exchange_small",
    )(small)


def _inproj_fwd(xn, w_pairs, proj, chip, tb):
    t = xn.shape[0]
    nblk, _, nb = w_pairs.shape

    def body(chip_ref, xn_ref, w_ref, proj_in, proj_ref):
        proj_ref[...] = _mm(xn_ref[...], w_ref[0])

    def other(j, chip_ref):
        return j + (j >= chip_ref[0]).astype(jnp.int32)

    return pl.pallas_call(
        body,
        grid_spec=pltpu.PrefetchScalarGridSpec(
            num_scalar_prefetch=1, grid=(t // tb, nblk - 1),
            in_specs=[pl.BlockSpec((tb, D), lambda i, j, chip_ref: (i, 0)),
                      pl.BlockSpec((1, D, nb), lambda i, j, chip_ref: (other(j, chip_ref), 0, 0)),
                      pl.BlockSpec(memory_space=pl.ANY)],
            out_specs=pl.BlockSpec((tb, nb), lambda i, j, chip_ref: (i, other(j, chip_ref)))),
        out_shape=jax.ShapeDtypeStruct((t, DIN), F32),
        input_output_aliases={3: 0},
        name="inproj_fwd", compiler_params=_cp(56),
    )(chip, xn, w_pairs, proj)


def _lower_bound(lb_ref):
    a0 = lb_ref[0:1, :]
    a1 = lb_ref[1:2, :]
    mx = jnp.maximum(a0, a1)
    e0 = jnp.exp(a0 - mx)
    e1 = jnp.exp(a1 - mx)
    return e0 / (e0 + e1)


def _hgrn_chunk_fwd(hq, hf, lb, tril):
    sg = _sigmoid(hf)
    f = lb + (1.0 - lb) * sg
    g = jnp.log(f)
    k = 1.0 - f
    sq = _sigmoid(hq)
    q = hq * sq
    b = _mm_tri_exact(tril, g)
    last_row = lax.broadcasted_iota(jnp.int32, b.shape, 0) == CH - 1
    b_last = jnp.sum(jnp.where(last_row, b, 0.0), axis=0, keepdims=True)
    c = 0.5 * b_last
    eb = jnp.exp(b)
    ea = jnp.exp(b - c)
    ek = jnp.exp(c - b)
    ed = jnp.exp(b_last - b)
    ebl = jnp.exp(b_last)
    return dict(sg=sg, f=f, k=k, sq=sq, q=q, eb=eb, ea=ea, ek=ek, ed=ed, ebl=ebl,
                qe=q * eb, qa=q * ea, ka=k * ek, kd=k * ed)


def _tri(lower):
    r = lax.broadcasted_iota(jnp.int32, (CH, CH), 0)
    c = lax.broadcasted_iota(jnp.int32, (CH, CH), 1)
    return (r >= c) if lower else (c >= r)


def _head_segment(p_ref, rows, j, hg):
    return p_ref[rows, j * HD * hg:(j + 1) * HD * hg]


def _head(a, k):
    return a[:, k * HD:(k + 1) * HD]


def _hgrn_fwd(proj, lbw, rb, hg):
    assert hg == HEADS
    t = proj.shape[0]
    ncb = rb // CH

    def body(p_ref, lb_ref, o_ref, st_ref, s_scr):
        @pl.when(pl.program_id(1) == 0)
        def _():
            s_scr[...] = jnp.zeros_like(s_scr)

        lb = _lower_bound(lb_ref)
        causal = _tri(True)
        tril = _tri3(True)
        heads = range(hg)

        def chunk(cc, carry):
            r0 = pl.multiple_of(cc * CH, CH)
            rows = pl.ds(r0, CH)
            e = _hgrn_chunk_fwd(_head_segment(p_ref, rows, 0, hg), _head_segment(p_ref, rows, 1, hg), lb, tril)
            v = _bf(_head_segment(p_ref, rows, 2, hg))
            sts = [s_scr[k] for k in heads]
            qa, ka, qe, kd = _bf(e["qa"]), _bf(e["ka"]), _bf(e["qe"]), _bf(e["kd"])
            a = [_bf(jnp.where(causal, _mm_nt(_head(qa, k), _head(ka, k)), 0.0)) for k in heads]
            o_inter = [_mm_nt(_head(qe, k), _bf(sts[k])) for k in heads]
            kv = [_mm_tn(_head(v, k), _head(kd, k)) for k in heads]
            o_intra = [_mm(a[k], _head(v, k)) for k in heads]
            for k in heads:
                st_ref[cc, k] = sts[k]
                o_ref[rows, k * HD:(k + 1) * HD] = o_inter[k] + o_intra[k]
                s_scr[k] = sts[k] * _head(e["ebl"], k) + kv[k]
            return carry

        lax.fori_loop(0, ncb, chunk, 0, unroll=4)

    return pl.pallas_call(
        body, grid=(HEADS // hg, t // rb),
        in_specs=[pl.BlockSpec((rb, 3 * HD * hg), lambda h, i: (i, h)), pl.BlockSpec((2, HD * hg), lambda h, i: (0, h))],
        out_specs=[pl.BlockSpec((rb, HD * hg), lambda h, i: (i, h)),
                   pl.BlockSpec((ncb, hg, HD, HD), lambda h, i: (i, h, 0, 0))],
        out_shape=[jax.ShapeDtypeStruct((t, D), F32), jax.ShapeDtypeStruct((t // CH, HEADS, HD, HD), F32)],
        scratch_shapes=[pltpu.VMEM((hg, HD, HD), F32)],
        name="hgrn_fwd", compiler_params=_cp(48),
    )(proj, lbw)


def _hgrn_bwd(proj, lbw, do_raw, states, rb, hg):
    assert hg == HEADS
    t = proj.shape[0]
    nblk = t // rb
    ncb = rb // CH
    wd = HD * hg

    def body(p_ref, lb_ref, do_ref, st_ref, dp_ref, dlb_ref, ds_scr):
        @pl.when(pl.program_id(1) == 0)
        def _():
            ds_scr[...] = jnp.zeros_like(ds_scr)
            dlb_ref[...] = jnp.zeros_like(dlb_ref)

        lb = _lower_bound(lb_ref)
        causal = _tri(True)
        tril = _tri3(True)
        triu = _tri3(False)
        last_row = lax.broadcasted_iota(jnp.int32, (CH, HD * hg), 0) == CH - 1
        row0 = lax.broadcasted_iota(jnp.int32, (8, HD * hg), 0) == 0
        heads = range(hg)
        wide = lambda parts: jnp.concatenate(parts, axis=1)

        def chunk(it, carry):
            cc = ncb - 1 - it
            r0 = pl.multiple_of(cc * CH, CH)
            rows = pl.ds(r0, CH)
            hq = _head_segment(p_ref, rows, 0, hg)
            e = _hgrn_chunk_fwd(hq, _head_segment(p_ref, rows, 1, hg), lb, tril)
            v = _bf(_head_segment(p_ref, rows, 2, hg))
            do = _bf(do_ref[rows, :])
            sts = [st_ref[cc, k] for k in heads]
            dsts = [ds_scr[k] for k in heads]
            dlb_acc = dlb_ref[...]
            qa, ka, qe, kd = _bf(e["qa"]), _bf(e["ka"]), _bf(e["qe"]), _bf(e["kd"])
            a = [_bf(jnp.where(causal, _mm_nt(_head(qa, k), _head(ka, k)), 0.0)) for k in heads]
            da = [_bf(jnp.where(causal, _mm_nt(_head(do, k), _head(v, k)), 0.0)) for k in heads]
            dqe = wide([_mm(_head(do, k), _bf(sts[k])) for k in heads])
            dkd = wide([_mm(_head(v, k), _bf(dsts[k])) for k in heads])
            dv_state = [_mm_nt(_head(kd, k), _bf(dsts[k])) for k in heads]
            ds_new = [_mm_tn(_head(do, k), _head(qe, k)) for k in heads]
            dv_intra = [_mm_tn(a[k], _head(do, k)) for k in heads]
            dqa = wide([_mm(da[k], _head(ka, k)) for k in heads])
            dka = wide([_mm_tn(da[k], _head(qa, k)) for k in heads])
            dv = wide([dv_intra[k] + dv_state[k] for k in heads])
            dbl = e["ebl"] * wide([jnp.sum(sts[k] * dsts[k], axis=0, keepdims=True) for k in heads])
            dq = dqe * e["eb"] + dqa * e["ea"]
            dk = dka * e["ek"] + dkd * e["ed"]
            dkd_kd = dkd * kd.astype(F32)
            db = dqe * qe.astype(F32) + dqa * qa.astype(F32) - dka * ka.astype(F32) - dkd_kd
            db = db + jnp.where(last_row, dbl + jnp.sum(dkd_kd, axis=0, keepdims=True), 0.0)
            dg = _mm_tri_exact(triu, db)
            df = dg / e["f"] - dk
            sg = e["sg"]
            sq = e["sq"]
            dhq = _bf(dq * (sq * (1.0 + hq * (1.0 - sq))))
            dhf = _bf(df * (1.0 - lb) * sg * (1.0 - sg))
            dhi = _bf(dv)
            dlb_new = dlb_acc + jnp.where(row0, jnp.sum(df * (1.0 - sg), axis=0, keepdims=True), 0.0)
            for k in heads:
                ds_scr[k] = ds_new[k] + dsts[k] * _head(e["ebl"], k)
            dp_ref[rows, 0:wd] = dhq
            dp_ref[rows, wd:2 * wd] = dhf
            dp_ref[rows, 2 * wd:3 * wd] = dhi
            dlb_ref[...] = dlb_new
            return carry

        lax.fori_loop(0, ncb, chunk, 0, unroll=2)

    rev = lambda h, i: (nblk - 1 - i, h)
    return pl.pallas_call(
        body, grid=(HEADS // hg, nblk),
        in_specs=[pl.BlockSpec((rb, 3 * HD * hg), rev), pl.BlockSpec((2, HD * hg), lambda h, i: (0, h)),
                  pl.BlockSpec((rb, HD * hg), rev), pl.BlockSpec((ncb, hg, HD, HD), lambda h, i: (nblk - 1 - i, h, 0, 0))],
        out_specs=[pl.BlockSpec((rb, 3 * HD * hg), rev), pl.BlockSpec((8, HD * hg), lambda h, i: (0, h))],
        out_shape=[jax.ShapeDtypeStruct((t, 3 * D), BF16), jax.ShapeDtypeStruct((8, D), F32)],
        scratch_shapes=[pltpu.VMEM((hg, HD, HD), F32)],
        name="hgrn_bwd", compiler_params=_cp(48),
    )(proj, lbw, do_raw, states)


def _kv_variants(tile, odd):
    low = lax.broadcasted_iota(jnp.int32, tile.shape, 1) < 64
    if odd:
        hi = jnp.where(low, 0.0, tile)
        lo = pltpu.roll(hi, 64, 1)
    else:
        lo = jnp.where(low, tile, 0.0)
        hi = pltpu.roll(lo, 64, 1)
    return _bf(lo), _bf(hi)


def _attn_masks(n):
    qi = lax.broadcasted_iota(jnp.int32, (AB, AB), 0)
    kj = lax.broadcasted_iota(jnp.int32, (AB, AB), 1)
    cur = kj <= qi
    return cur, cur | (n > 0), qi <= kj


def _kv_all(prev_ref, cur_ref):
    out = []
    for tl in range(2):
        cols = slice(tl * 128, (tl + 1) * 128)
        tile = jnp.concatenate([prev_ref[:, cols], cur_ref[:, cols]], axis=0)
        out.append(_kv_variants(tile, 0))
        out.append(_kv_variants(tile, 1))
    return out


def _window(a2, cur):
    return jnp.where(cur, a2[:, AB:], a2[:, :AB])


def _attn_softmax(scores, sinks, cur, ok):
    s = [jnp.where(ok, _window(s2, cur) * ATT_SCALE, NEG) for s2 in scores]
    m = [jnp.maximum(jnp.max(si, axis=-1, keepdims=True), sink) for si, sink in zip(s, sinks)]
    p = [jnp.exp(si - mi) for si, mi in zip(s, m)]
    es = [jnp.exp(sink - mi) for sink, mi in zip(sinks, m)]
    inv = [1.0 / (jnp.sum(pi, axis=-1, keepdims=True) + ei) for pi, ei in zip(p, es)]
    return [pi * ii for pi, ii in zip(p, inv)], [ei * ii for ei, ii in zip(es, inv)]


def _spread(pc, cur):
    return jnp.concatenate([jnp.where(cur, 0.0, pc), jnp.where(cur, pc, 0.0)], axis=1)


def _spread_t(pct, cur_t):
    return jnp.concatenate([jnp.where(cur_t, 0.0, pct), jnp.where(cur_t, pct, 0.0)], axis=0)


def _attn_fwd(proj, sinks):
    t = proj.shape[0]
    nb = t // AB

    nsub = 2
    assert nb % nsub == 0

    def body(q_ref, kc_ref, kp_ref, vc_ref, vp_ref, sink_ref, o_ref):
        sinks_v = [sink_ref[0, h] for h in range(QH)]
        heads = [(j, ab) for j in range(8) for ab in range(2)]
        for sb in range(nsub):
            rows = pl.ds(AB * sb, AB)
            before = pl.ds(AB * (sb - 1), AB)
            cur, ok, _ = _attn_masks(nsub * pl.program_id(0) + sb)
            kvars = _kv_all(kp_ref if sb == 0 else kc_ref.at[before, :], kc_ref.at[rows, :])
            vvars = _kv_all(vp_ref if sb == 0 else vc_ref.at[before, :], vc_ref.at[rows, :])
            qps = [_bf(q_ref[rows, 128 * j:128 * (j + 1)]) for j in range(8)]
            scores = [_mm_nt(qps[j], kvars[j // 2][ab]) for j, ab in heads]
            pcs, _ = _attn_softmax(scores, sinks_v, cur, ok)
            parts = [_mm(_bf(_spread(pcs[h], cur)), vvars[j // 2][ab]) for h, (j, ab) in enumerate(heads)]
            for j in range(8):
                o_ref[rows, 128 * j:128 * (j + 1)] = parts[2 * j] + parts[2 * j + 1]

    prev = lambda n: jnp.maximum(nsub * n - 1, 0)
    step = nsub * AB
    return pl.pallas_call(
        body, grid=(nb // nsub,),
        in_specs=[pl.BlockSpec((step, D), lambda n: (n, C_AQ // D)),
                  pl.BlockSpec((step, 256), lambda n: (n, C_AK // 256)),
                  pl.BlockSpec((AB, 256), lambda n: (prev(n), C_AK // 256)),
                  pl.BlockSpec((step, 256), lambda n: (n, C_AV // 256)),
                  pl.BlockSpec((AB, 256), lambda n: (prev(n), C_AV // 256)),
                  pl.BlockSpec(memory_space=pltpu.SMEM)],
        out_specs=pl.BlockSpec((step, D), lambda n: (n, 0)),
        out_shape=jax.ShapeDtypeStruct((t, D), F32),
        name="attn_fwd", compiler_params=_cp(32),
    )(proj, proj, proj, proj, proj, sinks)


def _attn_bwd(proj, sinks, do_a):
    t = proj.shape[0]
    nb = t // AB

    def body(q_ref, kc_ref, kp_ref, vc_ref, vp_ref, do_ref, sink_ref, dq_ref, dkv_ref, dsink_ref, carry):
        n = pl.program_id(0)

        @pl.when(n == 0)
        def _():
            dsink_ref[...] = jnp.zeros_like(dsink_ref)
            carry[...] = jnp.zeros_like(carry)

        @pl.when(n < nb)
        def _():
            cur, ok, cur_t = _attn_masks(n)
            low = lax.broadcasted_iota(jnp.int32, (2 * AB, 128), 1) < 64
            lane = lax.broadcasted_iota(jnp.int32, (8, 128), 1)
            row0 = lax.broadcasted_iota(jnp.int32, (8, 128), 0) == 0
            kvars = _kv_all(kp_ref, kc_ref)
            vvars = _kv_all(vp_ref, vc_ref)
            qps = [_bf(q_ref[:, 128 * j:128 * (j + 1)]) for j in range(8)]
            dops = [_bf(do_ref[:, 128 * j:128 * (j + 1)]) for j in range(8)]
            heads = [(j, ab) for j in range(8) for ab in range(2)]
            scores = [_mm_nt(qps[j], kvars[j // 2][ab]) for j, ab in heads]
            dps = [_mm_nt(dops[j], vvars[j // 2][ab]) for j, ab in heads]
            pcs, pss = _attn_softmax(scores, [sink_ref[0, h] for h in range(QH)], cur, ok)
            dpcs = [_window(dp2, cur) for dp2 in dps]
            rss = [jnp.sum(pc * dpc, axis=-1, keepdims=True) for pc, dpc in zip(pcs, dpcs)]
            dscs = [pc * (dpc - rs) for pc, dpc, rs in zip(pcs, dpcs, rss)]
            dsink = jnp.zeros((8, 128), F32)
            for h in range(QH):
                dsink = dsink + jnp.where(row0 & (lane == h), -jnp.sum(pss[h] * rss[h]), 0.0)
            dq_terms = [_mm(_bf(_spread(dscs[h], cur)), kvars[j // 2][ab]) for h, (j, ab) in enumerate(heads)]
            for j in range(8):
                dq_ref[:, 128 * j:128 * (j + 1)] = _bf((dq_terms[2 * j] + dq_terms[2 * j + 1]) * ATT_SCALE)
            dsc_t = [_bf(_spread_t(dsc.T, cur_t)) for dsc in dscs]
            pc_t = [_bf(_spread_t(pc.T, cur_t)) for pc in pcs]
            dk_terms = [_mm(dsc_t[h], qps[j]) for h, (j, ab) in enumerate(heads)]
            dv_terms = [_mm(pc_t[h], dops[j]) for h, (j, ab) in enumerate(heads)]
            dk_ab = [[dk_terms[4 * g + ab] + dk_terms[4 * g + 2 + ab] for ab in range(2)] for g in range(4)]
            dv_ab = [[dv_terms[4 * g + ab] + dv_terms[4 * g + 2 + ab] for ab in range(2)] for g in range(4)]
            for tl in range(2):
                ke, ko = dk_ab[2 * tl], dk_ab[2 * tl + 1]
                ve, vo = dv_ab[2 * tl], dv_ab[2 * tl + 1]
                dkt = (jnp.where(low, ke[0], 0.0) + pltpu.roll(jnp.where(low, 0.0, ke[1]), 64, 1)
                       + jnp.where(low, 0.0, ko[1]) + pltpu.roll(jnp.where(low, ko[0], 0.0), 64, 1)) * ATT_SCALE
                dvt = (jnp.where(low, ve[0], 0.0) + pltpu.roll(jnp.where(low, 0.0, ve[1]), 64, 1)
                       + jnp.where(low, 0.0, vo[1]) + pltpu.roll(jnp.where(low, vo[0], 0.0), 64, 1))
                kcols = slice(tl * 128, (tl + 1) * 128)
                vcols = slice(256 + tl * 128, 256 + (tl + 1) * 128)
                dkv_ref[:, kcols] = _bf(carry[:, kcols] + dkt[0:AB])
                dkv_ref[:, vcols] = _bf(carry[:, vcols] + dvt[0:AB])
                carry[:, kcols] = dkt[AB:2 * AB]
                carry[:, vcols] = dvt[AB:2 * AB]
            dsink_ref[...] += dsink

        @pl.when(n == nb)
        def _():
            dkv_ref[...] = _bf(carry[...])

    cur = lambda n: jnp.minimum(n, nb - 1)
    prev = lambda n: jnp.clip(n - 1, 0, nb - 1)
    return pl.pallas_call(
        body, grid=(nb + 1,),
        in_specs=[pl.BlockSpec((AB, D), lambda n: (cur(n), C_AQ // D)),
                  pl.BlockSpec((AB, 256), lambda n: (cur(n), C_AK // 256)),
                  pl.BlockSpec((AB, 256), lambda n: (prev(n), C_AK // 256)),
                  pl.BlockSpec((AB, 256), lambda n: (cur(n), C_AV // 256)),
                  pl.BlockSpec((AB, 256), lambda n: (prev(n), C_AV // 256)),
                  pl.BlockSpec((AB, D), lambda n: (cur(n), 0)),
                  pl.BlockSpec(memory_space=pltpu.SMEM)],
        out_specs=[pl.BlockSpec((AB, D), lambda n: (cur(n), 0)),
                   pl.BlockSpec((AB, 512), lambda n: (prev(n), 0)),
                   pl.BlockSpec((8, 128), lambda n: (0, 0))],
        out_shape=[jax.ShapeDtypeStruct((t, D), BF16), jax.ShapeDtypeStruct((t, 512), BF16),
                   jax.ShapeDtypeStruct((8, 128), F32)],
        scratch_shapes=[pltpu.VMEM((AB, 512), F32)],
        name="attn_bwd", compiler_params=_cp(32),
    )(proj, proj, proj, proj, proj, do_a, sinks)


def _silu_and_grad(v):
    s = _sigmoid(v)
    return v * s, s * (1.0 + v * (1.0 - s))


def _tail(o_raw, o_a, proj, x2d, tgt, wbh, wba, wout, hnw, fnw, tb):
    t = x2d.shape[0]

    def body(or_ref, oa_ref, hg_ref, ag0, ag1, mh0, mh1, ma0, ma1, x_ref, t_ref, wbh_ref, wba_ref, wout_ref, hnw_ref,
             fnw_ref, dx2_ref, dor_ref, doa_ref, dhg_ref, dagm_ref, gh_ref, ga_ref, mg_ref, dyh_ref, dya_ref, dx2b_ref,
             sums_ref):
        @pl.when(pl.program_id(0) == 0)
        def _():
            sums_ref[...] = jnp.zeros_like(sums_ref)

        halves = lambda a, b: jnp.concatenate([a[...], b[...]], axis=1)
        hnw_v = hnw_ref[...]
        fnw_v = fnw_ref[...]
        o = or_ref[...]
        rs, xhs = [], []
        for h in range(HEADS):
            oh = o[:, h * HD:(h + 1) * HD]
            r = lax.rsqrt(jnp.mean(oh * oh, axis=-1, keepdims=True) + EPS)
            rs.append(r)
            xhs.append(oh * r)
        xh = jnp.concatenate(xhs, axis=1)
        on = xh * hnw_v
        sil_hg, dsil_hg = _silu_and_grad(hg_ref[...])
        gh_b = _bf(on * sil_hg)
        y_h = _mm(gh_b, wbh_ref[...])
        oa = oa_ref[...]
        sil_ag, dsil_ag = _silu_and_grad(halves(ag0, ag1))
        ga_b = _bf(oa * sil_ag)
        y_a = _mm(ga_b, wba_ref[...])
        s_mh = _sigmoid(halves(mh0, mh1))
        s_ma = _sigmoid(halves(ma0, ma1))
        mg_b = _bf(s_mh * y_h + s_ma * y_a)
        x2 = x_ref[...] + _mm(mg_b, wout_ref[...])
        r2 = lax.rsqrt(jnp.mean(x2 * x2, axis=-1, keepdims=True) + EPS)
        xh2 = x2 * r2
        err = xh2 * fnw_v - t_ref[...]
        loss = 0.5 * jnp.sum(jnp.mean(err * err, axis=-1, keepdims=True))
        dy = err * (1.0 / D)
        dfnw = jnp.sum(dy * xh2, axis=0, keepdims=True)
        dxh2 = dy * fnw_v
        dx2 = r2 * (dxh2 - xh2 * jnp.mean(dxh2 * xh2, axis=-1, keepdims=True))
        dx2_ref[...] = dx2
        dx2_b = _bf(dx2)
        dmg = _mm_nt(dx2_b, wout_ref[...])
        dmg_h = dmg * s_mh
        dmg_a = dmg * s_ma
        dyh_b = _bf(dmg_h)
        dya_b = _bf(dmg_a)
        dagm_ref[:, D:2 * D] = _bf(dmg_h * y_h * (1.0 - s_mh))
        dagm_ref[:, 2 * D:3 * D] = _bf(dmg_a * y_a * (1.0 - s_ma))
        dgh = _mm_nt(dyh_b, wbh_ref[...])
        dga = _mm_nt(dya_b, wba_ref[...])
        doa_ref[...] = dga * sil_ag
        dagm_ref[:, 0:D] = _bf(dga * oa * dsil_ag)
        dhg_ref[...] = _bf(dgh * on * dsil_hg)
        don = dgh * sil_hg
        dhnw = jnp.sum(don * xh, axis=0, keepdims=True)
        dxh = don * hnw_v
        dos = []
        for h in range(HEADS):
            sl = slice(h * HD, (h + 1) * HD)
            dos.append(rs[h] * (dxh[:, sl] - xhs[h] * jnp.mean(dxh[:, sl] * xhs[h], axis=-1, keepdims=True)))
        dor_ref[...] = jnp.concatenate(dos, axis=1)
        gh_ref[...] = gh_b
        ga_ref[...] = ga_b
        mg_ref[...] = mg_b
        dyh_ref[...] = dyh_b
        dya_ref[...] = dya_b
        dx2b_ref[...] = dx2_b
        row = lax.broadcasted_iota(jnp.int32, (8, D), 0)
        sums_ref[...] += jnp.where(row == 0, dfnw, 0.0) + jnp.where(row == 1, dhnw, 0.0) + jnp.where(row == 2, loss, 0.0)

    rowblk = lambda c: pl.BlockSpec((tb, D), lambda i: (i, c))
    half = lambda c: pl.BlockSpec((tb, 512), lambda i: (i, c))
    full = lambda shape: pl.BlockSpec(shape, lambda i: (0, 0))
    return pl.pallas_call(
        body, grid=(t // tb,),
        in_specs=[rowblk(0), rowblk(0), rowblk(C_HG // D), half(C_AG // 512), half(C_AG // 512 + 1), half(C_MH // 512),
                  half(C_MH // 512 + 1), half(C_MA // 512), half(C_MA // 512 + 1), rowblk(0), rowblk(0),
                  full((D, D)), full((D, D)), full((D, D)), full((1, D)), full((1, D))],
        out_specs=[rowblk(0), rowblk(0), rowblk(0), rowblk(0), pl.BlockSpec((tb, 3 * D), lambda i: (i, 0))]
        + [rowblk(0)] * 6 + [full((8, D))],
        out_shape=[jax.ShapeDtypeStruct((t, D), F32)] * 3
        + [jax.ShapeDtypeStruct((t, D), BF16), jax.ShapeDtypeStruct((t, 3 * D), BF16)]
        + [jax.ShapeDtypeStruct((t, D), BF16)] * 6 + [jax.ShapeDtypeStruct((8, D), F32)],
        name="tail", compiler_params=_cp(56),
    )(o_raw, o_a, proj, proj, proj, proj, proj, proj, proj, x2d, tgt, wbh, wba, wout, hnw, fnw)


def _wgrad3(gh, dyh, ga, dya, mg, dx2b, tk):
    t = dyh.shape[0]

    def body(a0, b0, a1, b1, a2, b2, o0, o1, o2):
        @pl.when(pl.program_id(0) == 0)
        def _():
            o0[...] = jnp.zeros_like(o0)
            o1[...] = jnp.zeros_like(o1)
            o2[...] = jnp.zeros_like(o2)

        o0[...] += _mm_tn(a0[...], b0[...])
        o1[...] += _mm_tn(a1[...], b1[...])
        o2[...] += _mm_tn(a2[...], b2[...])

    blk = pl.BlockSpec((tk, D), lambda k: (k, 0))
    out = pl.BlockSpec((D, D), lambda k: (0, 0))
    return pl.pallas_call(
        body, grid=(t // tk,), in_specs=[blk] * 6, out_specs=[out] * 3,
        out_shape=[jax.ShapeDtypeStruct((D, D), F32)] * 3,
        name="wgrad3", compiler_params=_cp(48),
    )(gh, dyh, ga, dya, mg, dx2b)


def _inproj_wgrad_piece(xnt, piece, nb, name):
    t = xnt.shape[1]
    width = piece.shape[1]

    def body(xnt_ref, p_ref, o_ref):
        o_ref[...] = _mm(xnt_ref[...], p_ref[...])

    return pl.pallas_call(
        body, grid=(width // nb,),
        in_specs=[pl.BlockSpec((D, t), lambda j: (0, 0), pipeline_mode=pl.Buffered(1)),
                  pl.BlockSpec((t, nb), lambda j: (0, j))],
        out_specs=pl.BlockSpec((D, nb), lambda j: (0, j)),
        out_shape=jax.ShapeDtypeStruct((D, width), F32),
        name=name, compiler_params=_cp(56),
    )(xnt, piece)


def _inproj_dgrad(pieces, w_p, x2d, dx2, norm_w, tb, after):
    t = x2d.shape[0]

    def body(*refs):
        piece_refs = refs[:len(pieces)]
        w_ref, x_ref, dx2_ref, nw_ref, _, gx_ref, dnw_ref = refs[len(pieces):]

        @pl.when(pl.program_id(0) == 0)
        def _():
            dnw_ref[...] = jnp.zeros_like(dnw_ref)

        dxn = None
        off = 0
        for p in piece_refs:
            width = p.shape[1]
            for q in range(w_ref.shape[0]):
                lo, hi = max(off, q * PAIR), min(off + width, (q + 1) * PAIR)
                if lo < hi:
                    term = _mm_nt(p[:, lo - off:hi - off], w_ref[q, :, lo - q * PAIR:hi - q * PAIR])
                    dxn = term if dxn is None else dxn + term
            off += width
        xv = x_ref[...]
        r = lax.rsqrt(jnp.mean(xv * xv, axis=-1, keepdims=True) + EPS)
        xh = xv * r
        dxh = dxn * nw_ref[...]
        gx_ref[...] = dx2_ref[...] + r * (dxh - xh * jnp.mean(dxh * xh, axis=-1, keepdims=True))
        row0 = lax.broadcasted_iota(jnp.int32, (8, D), 0) == 0
        dnw_ref[...] += jnp.where(row0, jnp.sum(dxn * xh, axis=0, keepdims=True), 0.0)

    rowblk = pl.BlockSpec((tb, D), lambda i: (i, 0))
    return pl.pallas_call(
        body, grid=(t // tb,),
        in_specs=[pl.BlockSpec((tb, p.shape[1]), lambda i: (i, 0)) for p in pieces]
        + [pl.BlockSpec(w_p.shape, lambda i: (0, 0, 0), pipeline_mode=pl.Buffered(1)), rowblk, rowblk,
           pl.BlockSpec((1, D), lambda i: (0, 0)), pl.BlockSpec(memory_space=pl.ANY)],
        out_specs=[rowblk, pl.BlockSpec((8, D), lambda i: (0, 0))],
        out_shape=[jax.ShapeDtypeStruct((t, D), F32), jax.ShapeDtypeStruct((8, D), F32)],
        name="inproj_dgrad", compiler_params=_cp(60),
    )(*pieces, w_p, x2d, dx2, norm_w, after)


def _adamw_math(w, g, m, v):
    m = B1 * m + (1.0 - B1) * g
    v = B2 * v + (1.0 - B2) * (g * g)
    m_hat = m / (1.0 - B1 ** STEP)
    v_hat = v / (1.0 - B2 ** STEP)
    delta = -LR * (m_hat / (jnp.sqrt(v_hat) + ADAM_EPS) + WD * w)
    return delta, m, v


def _adamw_shard(recv, sums, chip, w, m, v, rows, name):
    nparts, nr, nc = recv.shape

    def body(chip_ref, own_ref, p_ref, w_ref, m_ref, v_ref, g_ref, d_ref, nm_ref, nv_ref):
        g = own_ref[0].astype(F32)
        for s in range(nparts):
            g = g + p_ref[s].astype(F32)
        d, nm, nv = _adamw_math(w_ref[...], g, m_ref[...], v_ref[...])
        g_ref[...] = g
        d_ref[...] = d
        nm_ref[...] = nm
        nv_ref[...] = nv

    blk = pl.BlockSpec((rows, nc), lambda i, chip_ref: (i, 0))
    return pl.pallas_call(
        body,
        grid_spec=pltpu.PrefetchScalarGridSpec(
            num_scalar_prefetch=1, grid=(nr // rows,),
            in_specs=[pl.BlockSpec((1, rows, nc), lambda i, chip_ref: (chip_ref[0], i, 0)),
                      pl.BlockSpec((nparts, rows, nc), lambda i, chip_ref: (0, i, 0)), blk, blk, blk],
            out_specs=[blk] * 4),
        out_shape=[jax.ShapeDtypeStruct((nr, nc), F32)] * 4,
        name=name, compiler_params=_cp(48),
    )(chip, sums, recv, w, m, v)


def _adamw_sum8(parts, w, m, v, after, name):
    def body(p_ref, w_ref, m_ref, v_ref, _, g_ref, d_ref, nm_ref, nv_ref):
        g = p_ref[0].astype(F32)
        for s in range(1, NDEV):
            g = g + p_ref[s].astype(F32)
        d, nm, nv = _adamw_math(w_ref[...], g, m_ref[...], v_ref[...])
        g_ref[...] = g
        d_ref[...] = d
        nm_ref[...] = nm
        nv_ref[...] = nv

    vm = pl.BlockSpec(memory_space=pltpu.VMEM)
    return pl.pallas_call(
        body, out_shape=[jax.ShapeDtypeStruct(w.shape, F32)] * 4,
        in_specs=[vm, vm, vm, vm, pl.BlockSpec(memory_space=pl.ANY)], out_specs=[vm] * 4, name=name,
    )(parts, w, m, v, after)


SMALL_ROWS = dict(norm_w=0, lower_bound=1, hgrn_norm_w=3, final_norm_w=4, sinks=5, loss=6)


def _pack_small_grads(dnw, dlb, sums, dsink):
    def body(dnw_ref, dlb_ref, sums_ref, dsink_ref, o_ref):
        o_ref[...] = jnp.zeros_like(o_ref)
        o_ref[0:1, :] = dnw_ref[0:1, :]
        o_ref[1:2, :] = dlb_ref[0:1, :]
        o_ref[3:4, :] = sums_ref[1:2, :]
        o_ref[4:5, :] = sums_ref[0:1, :]
        o_ref[5:6, 0:128] = dsink_ref[0:1, :]
        o_ref[6:7, :] = sums_ref[2:3, :]

    return pl.pallas_call(body, out_shape=jax.ShapeDtypeStruct((8, D), F32), name="pack_small_grads")(dnw, dlb, sums, dsink)


def _adamw_small(parts, ws, ms, vs):
    shapes = [a.shape for a in ws]

    def body(p_ref, *refs):
        w, m, v = refs[0:5], refs[5:10], refs[10:15]
        outs = [refs[15 + 5 * i:20 + 5 * i] for i in range(4)]
        loss_ref = refs[35]

        def total(row, width):
            g = p_ref[0, row:row + 1, 0:width]
            for s in range(1, NDEV):
                g = g + p_ref[s, row:row + 1, 0:width]
            return g

        loss_ref[...] = total(6, 128)
        lb = _lower_bound(w[1])
        ga0 = total(1, D) * lb * (1.0 - lb)
        grads = [total(0, D), None, total(3, D), total(4, D), total(5, QH)]
        for i in (0, 2, 3, 4):
            res = (grads[i],) + _adamw_math(w[i][...], grads[i], m[i][...], v[i][...])
            for o, val in zip(outs, res):
                o[i][...] = val
        for r, g in ((0, ga0), (1, -ga0)):
            res = (g,) + _adamw_math(w[1][r:r + 1, :], g, m[1][r:r + 1, :], v[1][r:r + 1, :])
            for o, val in zip(outs, res):
                o[1][r:r + 1, :] = val

    res = pl.pallas_call(
        body, out_shape=[jax.ShapeDtypeStruct(s, F32) for s in shapes] * 4 + [jax.ShapeDtypeStruct((1, 128), F32)],
        name="adamw_small",
    )(parts, *ws, *ms, *vs)
    return [res[5 * i:5 * i + 5] for i in range(4)], res[20][0, 0]


def kernel(x, norm_w, w_in, hgrn_lower_bound, hgrn_norm_w, w_branch_hgrn, attn_sinks, w_branch_attn, w_out, final_norm_w, loss_target, m_norm_w, m_w_in, m_hgrn_lower_bound, m_hgrn_norm_w, m_w_branch_hgrn, m_attn_sinks, m_w_branch_attn, m_w_out, m_final_norm_w, v_norm_w, v_w_in, v_hgrn_lower_bound, v_hgrn_norm_w, v_w_branch_hgrn, v_attn_sinks, v_w_branch_attn, v_w_out, v_final_norm_w):
    t = x.shape[1]
    x2d = x.reshape(t, D)
    tgt = loss_target.reshape(t, D)
    fnw = final_norm_w.reshape(1, D)
    row_blk = min(256, t)
    big_blk = min(512, t)

    chip = (2 * lax.axis_index("x") + lax.axis_index("y")).astype(jnp.int32).reshape(1)
    w_p, xn, xnt, proj_own = _gather_in_projection(w_in[0], x2d, norm_w)
    wbh, wba, wout = (g.reshape(D, D) for g in _gather_square(
        [w_branch_hgrn[0].astype(BF16), w_branch_attn[0].astype(BF16), w_out[0].astype(BF16)], after=w_p))

    proj = _inproj_fwd(xn, w_p, proj_own, chip, min(1024, t))
    o_raw, states = _hgrn_fwd(proj, hgrn_lower_bound, big_blk, HGRN_GROUP)
    o_a = _attn_fwd(proj, attn_sinks)
    (dx2, do_raw, do_a, d_hg, d_agm, gh, ga, mg, dyh, dya, dx2b, sums) = _tail(
        o_raw, o_a, proj, x2d, tgt, wbh, wba, wout, hgrn_norm_w, fnw, row_blk)
    dwbh, dwba, dwout = _wgrad3(gh, dyh, ga, dya, mg, dx2b, big_blk)
    d_aq, d_kv, dsink = _attn_bwd(proj, attn_sinks, do_a)
    d_hgrn, dlb = _hgrn_bwd(proj, hgrn_lower_bound, do_raw, states, big_blk, HGRN_GROUP)
    pieces = (d_hgrn, d_hg, d_aq, d_kv, d_agm)
    dw_pieces = [_inproj_wgrad_piece(xnt, p, CB, "inproj_wgrad_" + n)
                 for p, n in zip(pieces, ("hgrn", "hgate", "aq", "kv", "gates"))]

    dwin_r = jnp.concatenate(dw_pieces, axis=1).reshape(D, NDEV, IN_SHARD).transpose(1, 0, 2).astype(BF16)
    slots = lambda a: a.reshape(NDEV, ROW_SHARD, D).astype(BF16)
    got = _exchange_pair([dwin_r])
    core = lax.axis_index("c").astype(jnp.int32).reshape(1)
    s_in = _pair_sum(dwin_r, got[0], core, 4 * ROW_SHARD, "pair_sum_w_in")
    rin, = _exchange_chips([s_in])
    rbh, rba, rout = _exchange_square([slots(dwbh), slots(dwba), slots(dwout)])
    grad_x, dnw = _inproj_dgrad(pieces, w_p, x2d, dx2, norm_w, big_blk, after=s_in)
    rsm = _exchange_small(_pack_small_grads(dnw, dlb, sums, dsink))
    g_in, d_in, nm_in, nv_in = _adamw_shard(rin, s_in, chip, w_in[0], m_w_in[0], v_w_in[0], 128, "adamw_w_in")
    g_bh, d_bh, nm_bh, nv_bh = _adamw_sum8(
        rbh, w_branch_hgrn[0], m_w_branch_hgrn[0], v_w_branch_hgrn[0], dnw, "adamw_w_bh")
    g_ba, d_ba, nm_ba, nv_ba = _adamw_sum8(
        rba, w_branch_attn[0], m_w_branch_attn[0], v_w_branch_attn[0], dnw, "adamw_w_ba")
    g_out, d_out, nm_out, nv_out = _adamw_sum8(rout, w_out[0], m_w_out[0], v_w_out[0], dnw, "adamw_w_out")
    (sg, sd, sm, sv), loss = _adamw_small(
        rsm,
        (norm_w, hgrn_lower_bound, hgrn_norm_w, fnw, attn_sinks),
        (m_norm_w, m_hgrn_lower_bound, m_hgrn_norm_w, m_final_norm_w.reshape(1, D), m_attn_sinks),
        (v_norm_w, v_hgrn_lower_bound, v_hgrn_norm_w, v_final_norm_w.reshape(1, D), v_attn_sinks))

    def group(s, w_in_v, bh, ba, out):
        nw, lb, hnw, fn, sinks = s
        return (nw, w_in_v[None], lb, hnw, bh[None], sinks, ba[None], out[None], fn.reshape(D))

    return (loss, grad_x.reshape(1, t, D),
            *group(sg, g_in, g_bh, g_ba, g_out), *group(sd, d_in, d_bh, d_ba, d_out),
            *group(sm, nm_in, nm_bh, nm_ba, nm_out), *group(sv, nv_in, nv_bh, nv_ba, nv_out))
```

```python
import jax
import jax.numpy as jnp
from jax import lax
from jax.experimental import pallas as pl
from jax.experimental.pallas import tpu as pltpu
from jax.experimental.pallas import tpu_sc as plsc

F32 = jnp.float32
BF16 = jnp.bfloat16

D = 1024
DIN = 8704
NDEV = 8
IN_SHARD = DIN // NDEV
PAIR = 2 * IN_SHARD
ROW_SHARD = D // NDEV
HEADS = 8
HD = 128
CH = 64
HGRN_GROUP = 8
QH = 16
AB = 128
EPS = 1e-6
NEG = -1e30
ATT_SCALE = 0.125

C_HG = 3072
C_AQ = 4096
C_AK = 5120
C_AV = 5376
C_AG = 5632
C_MH = 6656
C_MA = 7680
CB = 512

LR = 0.001
B1 = 0.9
B2 = 0.999
ADAM_EPS = 1e-08
WD = 0.01
STEP = 10

MESH = pl.DeviceIdType.MESH


def _cp(vmem_mb):
    return pltpu.CompilerParams(vmem_limit_bytes=vmem_mb * 1024 * 1024)


def _mm(a, b):
    return jnp.dot(a, b, preferred_element_type=F32)


def _mm_nt(a, b):
    return lax.dot_general(a, b, (((1,), (1,)), ((), ())), preferred_element_type=F32)


def _mm_tn(a, b):
    return lax.dot_general(a, b, (((0,), (0,)), ((), ())), preferred_element_type=F32)


def _tri3(lower):
    r = lax.broadcasted_iota(jnp.int32, (CH, 3 * CH), 0)
    c = lax.broadcasted_iota(jnp.int32, (CH, 3 * CH), 1)
    c = jnp.where(c >= 2 * CH, c - 2 * CH, jnp.where(c >= CH, c - CH, c))
    return ((r >= c) if lower else (c >= r)).astype(BF16)


def _mm_tri_exact(tri3, g):
    g1 = g.astype(BF16)
    r1 = g - g1.astype(F32)
    g2 = r1.astype(BF16)
    g3 = (r1 - g2.astype(F32)).astype(BF16)
    return _mm(tri3, jnp.concatenate([g1, g2, g3], axis=0))


def _sigmoid(v):
    return 0.5 * jnp.tanh(0.5 * v) + 0.5


def _bf(v):
    return v.astype(BF16)


def _place():
    x, y, c = lax.axis_index("x"), lax.axis_index("y"), lax.axis_index("c")
    return (x, y, c), (x, y, 1 - c), [(1 - x, y), (x, 1 - y), (1 - x, 1 - y)]


def _dev_index(px, py, pc):
    return 4 * px + 2 * py + pc


def _gather_in_projection(w_in_s, x2d, norm_w):
    half = D // 2
    t = x2d.shape[0]
    prep_rows = min(512, t)
    nprep = t // prep_rows

    def body(win_ref, x_hbm, nw_ref, wp_g, xn_hbm, xnt_hbm, proj_hbm, give, take, mine, xbuf, xnbuf, xntbuf, w_own, pbuf,
             send_sems, recv_sems, loc_sem, swap_sems, in_sems, out_sems, own_sem):
        (x, y, c), sibling, chips = _place()
        give[...] = win_ref[pl.ds(pl.multiple_of(half * (1 - c), half), half), :].astype(BF16)
        swap = pltpu.make_async_remote_copy(src_ref=give, dst_ref=take, send_sem=swap_sems.at[0], recv_sem=swap_sems.at[1],
                                            device_id=sibling, device_id_type=MESH)
        swap.start()
        swap.wait()
        own = win_ref[pl.ds(pl.multiple_of(half * c, half), half), :]
        other = take[...].astype(F32)
        mine[...] = jnp.where(c == 0, jnp.concatenate([own, other], axis=1),
                              jnp.concatenate([other, own], axis=1)).astype(BF16)

        def place(px, py, pc):
            return wp_g.at[2 * px + py, pl.ds(pl.multiple_of(half * pc, half), half), :]

        def copy(kind, origin, to, src=mine):
            return pltpu.make_async_remote_copy(
                src_ref=src, dst_ref=place(*origin), send_sem=send_sems.at[kind], recv_sem=recv_sems.at[kind],
                device_id=to, device_id_type=MESH)

        me = (x, y, c)
        local = pltpu.make_async_copy(mine, place(*me), loc_sem)
        local.start()
        first = [copy(0, me, sibling)] + [copy(1 + j, me, (*chip, c)) for j, chip in enumerate(chips)]
        for cp in first:
            cp.start()

        copy(0, (x, y, 1 - c), me).wait_recv()
        local.wait()
        my_chip = 2 * x + y
        fetch = pltpu.make_async_copy(wp_g.at[my_chip], w_own, own_sem)
        fetch.start()

        def rows_of(i):
            return pl.ds(pl.multiple_of(i * prep_rows, prep_rows), prep_rows)

        def load(i, slot):
            return pltpu.make_async_copy(x_hbm.at[rows_of(i), :], xbuf.at[slot], in_sems.at[slot])

        def stores(i, slot):
            own_cols = pl.ds(pl.multiple_of(my_chip * PAIR, 128), PAIR)
            return (pltpu.make_async_copy(xnbuf.at[slot], xn_hbm.at[rows_of(i), :], out_sems.at[slot, 0]),
                    pltpu.make_async_copy(xntbuf.at[slot], xnt_hbm.at[:, rows_of(i)], out_sems.at[slot, 1]),
                    pltpu.make_async_copy(pbuf.at[slot], proj_hbm.at[rows_of(i), own_cols], out_sems.at[slot, 2]))

        load(0, 0).start()
        fetch.wait()

        def prep(i, carry):
            slot = lax.rem(i, 2)
            load(i, slot).wait()

            @pl.when(i + 1 < nprep)
            def _():
                load(i + 1, 1 - slot).start()

            @pl.when(i >= 2)
            def _():
                for cp in stores(i - 2, slot):
                    cp.wait()

            xv = xbuf[slot]
            xn = (xv * lax.rsqrt(jnp.mean(xv * xv, axis=-1, keepdims=True) + EPS)) * nw_ref[...]
            xn_b = xn.astype(BF16)
            xnbuf[slot] = xn_b
            xntbuf[slot] = xn.T.astype(BF16)
            pbuf[slot] = _mm(xn_b, w_own[...])
            for cp in stores(i, slot):
                cp.start()
            return carry

        lax.fori_loop(0, nprep, prep, 0)
        for i in range(max(nprep - 2, 0), nprep):
            for cp in stores(i, i % 2):
                cp.wait()

        passed = []
        for j, chip in enumerate(chips):
            copy(1 + j, (*chip, c), me).wait_recv()
            cp = copy(4 + j, (*chip, c), sibling, src=place(*chip, c))
            cp.start()
            passed.append(cp)
        for j, chip in enumerate(chips):
            copy(4 + j, (*chip, 1 - c), me).wait_recv()
        for cp in first + passed:
            cp.wait_send()

    vm = pl.BlockSpec(memory_space=pltpu.VMEM)
    hbm = pl.BlockSpec(memory_space=pl.ANY)
    return pl.pallas_call(
        body,
        out_shape=[jax.ShapeDtypeStruct((NDEV // 2, D, PAIR), BF16), jax.ShapeDtypeStruct((t, D), BF16),
                   jax.ShapeDtypeStruct((D, t), BF16), jax.ShapeDtypeStruct((t, DIN), F32)],
        in_specs=[vm, hbm, vm],
        out_specs=[hbm, hbm, hbm, hbm],
        scratch_shapes=[pltpu.VMEM((half, IN_SHARD), BF16), pltpu.VMEM((half, IN_SHARD), BF16),
                        pltpu.VMEM((half, PAIR), BF16),
                        pltpu.VMEM((2, prep_rows, D), F32), pltpu.VMEM((2, prep_rows, D), BF16),
                        pltpu.VMEM((2, D, prep_rows), BF16),
                        pltpu.VMEM((D, PAIR), BF16), pltpu.VMEM((2, prep_rows, PAIR), F32),
                        pltpu.SemaphoreType.DMA((NDEV - 1,)), pltpu.SemaphoreType.DMA((NDEV - 1,)),
                        pltpu.SemaphoreType.DMA, pltpu.SemaphoreType.DMA((2,)),
                        pltpu.SemaphoreType.DMA((2,)), pltpu.SemaphoreType.DMA((2, 3)), pltpu.SemaphoreType.DMA],
        name="gather_in_projection", compiler_params=_cp(56),
    )(w_in_s, x2d, norm_w)


def _gather_square(shards, after):
    n = len(shards)

    def launch(*refs):
        ins, outs = refs[:n], refs[n + 1:2 * n + 1]
        send_sems, recv_sems, loc_sems = refs[2 * n + 1:]
        (x, y, c), _, _ = _place()
        me = _dev_index(x, y, c)
        peers = [(1 - x if r & 4 else x, 1 - y if r & 2 else y, 1 - c if r & 1 else c) for r in range(1, NDEV)]
        barrier = pltpu.get_barrier_semaphore()
        for peer in peers:
            pl.semaphore_signal(barrier, inc=1, device_id=peer, device_id_type=MESH)
        pl.semaphore_wait(barrier, NDEV - 1)
        local = [pltpu.make_async_copy(ins[k], outs[k].at[me], loc_sems.at[k]) for k in range(n)]
        copies = [pltpu.make_async_remote_copy(
            src_ref=ins[k], dst_ref=outs[k].at[me], send_sem=send_sems.at[r, k], recv_sem=recv_sems.at[r, k],
            device_id=peer, device_id_type=MESH) for r, peer in enumerate(peers) for k in range(n)]
        for cp in local + copies:
            cp.start()
        for r, peer in enumerate(peers):
            for k in range(n):
                pltpu.make_async_remote_copy(
                    src_ref=ins[k], dst_ref=outs[k].at[_dev_index(*peer)], send_sem=send_sems.at[r, k],
                    recv_sem=recv_sems.at[r, k], device_id=peer, device_id_type=MESH).wait_recv()
        for cp in copies:
            cp.wait_send()
        for cp in local:
            cp.wait()

    return pl.kernel(
        launch, out_type=[jax.ShapeDtypeStruct((NDEV,) + a.shape, a.dtype) for a in shards],
        mesh=plsc.ScalarSubcoreMesh(axis_name="sequencer", num_cores=1), name="gather_square",
        scratch_types=(pltpu.SemaphoreType.DMA((NDEV - 1, n)), pltpu.SemaphoreType.DMA((NDEV - 1, n)),
                       pltpu.SemaphoreType.DMA((n,))),
        compiler_params=pltpu.CompilerParams(collective_id=2),
    )(*shards, after)


def _exchange_pair(arrs):
    n = len(arrs)

    def launch(*refs):
        ins, got = refs[:n], refs[n:2 * n]
        send_sems, recv_sems = refs[2 * n:]
        (x, y, c), sibling, _ = _place()
        barrier = pltpu.get_barrier_semaphore()
        pl.semaphore_signal(barrier, inc=1, device_id=sibling, device_id_type=MESH)
        pl.semaphore_wait(barrier, 1)
        sends = [pltpu.make_async_remote_copy(
            src_ref=ins[k].at[_dev_index(q // 2, q % 2, 1 - c)], dst_ref=got[k].at[q], send_sem=send_sems.at[q, k],
            recv_sem=recv_sems.at[q, k], device_id=sibling, device_id_type=MESH) for q in range(4) for k in range(n)]
        for cp in sends:
            cp.start()
        for cp in sends:
            cp.wait_recv()
        for cp in sends:
            cp.wait_send()

    return pl.kernel(
        launch, out_type=[jax.ShapeDtypeStruct((4,) + a.shape[1:], a.dtype) for a in arrs],
        mesh=plsc.ScalarSubcoreMesh(axis_name="sequencer", num_cores=1), name="exchange_pair",
        scratch_types=(pltpu.SemaphoreType.DMA((4, n)), pltpu.SemaphoreType.DMA((4, n))),
        compiler_params=pltpu.CompilerParams(collective_id=0),
    )(*arrs)


def _pair_sum(full, got, core, rows, name):
    _, nr, nc = got.shape

    def body(core_ref, a_ref, b_ref, o_ref):
        o_ref[...] = (a_ref[...].astype(F32) + b_ref[...].astype(F32)).astype(BF16)

    blk = pl.BlockSpec((1, rows, nc), lambda q, i, core_ref: (q, i, 0))
    return pl.pallas_call(
        body,
        grid_spec=pltpu.PrefetchScalarGridSpec(
            num_scalar_prefetch=1, grid=(4, nr // rows),
            in_specs=[pl.BlockSpec((1, rows, nc), lambda q, i, core_ref: (2 * q + core_ref[0], i, 0)), blk],
            out_specs=blk),
        out_shape=jax.ShapeDtypeStruct(got.shape, BF16), name=name,
    )(core, full, got)


def _exchange_chips(sums):
    n = len(sums)

    def launch(*refs):
        ins, outs = refs[:n], refs[n:2 * n]
        send_sems, recv_sems = refs[2 * n:]
        (x, y, c), _, chips = _place()
        barrier = pltpu.get_barrier_semaphore()
        for px, py in chips:
            pl.semaphore_signal(barrier, inc=1, device_id=(px, py, c), device_id_type=MESH)
        pl.semaphore_wait(barrier, len(chips))
        copies = [pltpu.make_async_remote_copy(
            src_ref=ins[k].at[2 * px + py], dst_ref=outs[k].at[j], send_sem=send_sems.at[j, k],
            recv_sem=recv_sems.at[j, k], device_id=(px, py, c), device_id_type=MESH)
            for j, (px, py) in enumerate(chips) for k in range(n)]
        for cp in copies:
            cp.start()
        for cp in copies:
            cp.wait_recv()
        for cp in copies:
            cp.wait_send()

    return pl.kernel(
        launch, out_type=[jax.ShapeDtypeStruct((3,) + a.shape[1:], a.dtype) for a in sums],
        mesh=plsc.ScalarSubcoreMesh(axis_name="sequencer", num_cores=1), name="exchange_chips",
        scratch_types=(pltpu.SemaphoreType.DMA((3, n)), pltpu.SemaphoreType.DMA((3, n))),
        compiler_params=pltpu.CompilerParams(collective_id=1),
    )(*sums)


def _exchange_square(partials):
    n = len(partials)

    def launch(*refs):
        ins, outs = refs[:n], refs[n:2 * n]
        send_sems, recv_sems, loc_sems = refs[2 * n:]
        (x, y, c), _, _ = _place()
        me = _dev_index(x, y, c)
        peers = [(1 - x if r & 4 else x, 1 - y if r & 2 else y, 1 - c if r & 1 else c) for r in range(1, NDEV)]
        barrier = pltpu.get_barrier_semaphore()
        for peer in peers:
            pl.semaphore_signal(barrier, inc=1, device_id=peer, device_id_type=MESH)
        pl.semaphore_wait(barrier, NDEV - 1)
        local = [pltpu.make_async_copy(ins[k].at[me], outs[k].at[me], loc_sems.at[k]) for k in range(n)]
        copies = [pltpu.make_async_remote_copy(
            src_ref=ins[k].at[_dev_index(*peer)], dst_ref=outs[k].at[me], send_sem=send_sems.at[r, k],
            recv_sem=recv_sems.at[r, k], device_id=peer, device_id_type=MESH)
            for r, peer in enumerate(peers) for k in range(n)]
        for cp in local + copies:
            cp.start()
        for r, peer in enumerate(peers):
            for k in range(n):
                pltpu.make_async_remote_copy(
                    src_ref=ins[k].at[me], dst_ref=outs[k].at[_dev_index(*peer)], send_sem=send_sems.at[r, k],
                    recv_sem=recv_sems.at[r, k], device_id=peer, device_id_type=MESH).wait_recv()
        for cp in copies:
            cp.wait_send()
        for cp in local:
            cp.wait()

    return pl.kernel(
        launch, out_type=[jax.ShapeDtypeStruct(a.shape, a.dtype) for a in partials],
        mesh=plsc.ScalarSubcoreMesh(axis_name="sequencer", num_cores=1), name="exchange_square",
        scratch_types=(pltpu.SemaphoreType.DMA((NDEV - 1, n)), pltpu.SemaphoreType.DMA((NDEV - 1, n)),
                       pltpu.SemaphoreType.DMA((n,))),
        compiler_params=pltpu.CompilerParams(collective_id=3),
    )(*partials)


def _exchange_small(small):
    def body(sm_ref, out_ref, send_sems, recv_sems):
        (x, y, c), _, _ = _place()
        me = _dev_index(x, y, c)
        peers = [(1 - x if r & 4 else x, 1 - y if r & 2 else y, 1 - c if r & 1 else c) for r in range(1, NDEV)]
        out_ref[me] = sm_ref[...]
        copies = [pltpu.make_async_remote_copy(
            src_ref=sm_ref, dst_ref=out_ref.at[me], send_sem=send_sems.at[r], recv_sem=recv_sems.at[r],
            device_id=peer, device_id_type=MESH) for r, peer in enumerate(peers)]
        for cp in copies:
            cp.start()
        for r, peer in enumerate(peers):
            pltpu.make_async_remote_copy(
                src_ref=sm_ref, dst_ref=out_ref.at[_dev_index(*peer)], send_sem=send_sems.at[r], recv_sem=recv_sems.at[r],
                device_id=peer, device_id_type=MESH).wait_recv()
        for cp in copies:
            cp.wait_send()

    vm = pl.BlockSpec(memory_space=pltpu.VMEM)
    return pl.pallas_call(
        body, out_shape=jax.ShapeDtypeStruct((NDEV,) + small.shape, F32), in_specs=[vm], out_specs=vm,
        scratch_shapes=[pltpu.SemaphoreType.DMA((NDEV - 1,)), pltpu.SemaphoreType.DMA((NDEV - 1,))],
        name="exchange_small",
    )(small)


def _inproj_fwd(xn, w_pairs, proj, chip, tb):
    t = xn.shape[0]
    nblk, _, nb = w_pairs.shape

    def body(chip_ref, xn_ref, w_ref, proj_in, proj_ref):
        proj_ref[...] = _mm(xn_ref[...], w_ref[0])

    def other(j, chip_ref):
        return j + (j >= chip_ref[0]).astype(jnp.int32)

    return pl.pallas_call(
        body,
        grid_spec=pltpu.PrefetchScalarGridSpec(
            num_scalar_prefetch=1, grid=(t // tb, nblk - 1),
            in_specs=[pl.BlockSpec((tb, D), lambda i, j, chip_ref: (i, 0)),
                      pl.BlockSpec((1, D, nb), lambda i, j, chip_ref: (other(j, chip_ref), 0, 0)),
                      pl.BlockSpec(memory_space=pl.ANY)],
            out_specs=pl.BlockSpec((tb, nb), lambda i, j, chip_ref: (i, other(j, chip_ref)))),
        out_shape=jax.ShapeDtypeStruct((t, DIN), F32),
        input_output_aliases={3: 0},
        name="inproj_fwd", compiler_params=_cp(56),
    )(chip, xn, w_pairs, proj)


def _lower_bound(lb_ref):
    a0 = lb_ref[0:1, :]
    a1 = lb_ref[1:2, :]
    mx = jnp.maximum(a0, a1)
    e0 = jnp.exp(a0 - mx)
    e1 = jnp.exp(a1 - mx)
    return e0 / (e0 + e1)


def _hgrn_chunk_fwd(hq, hf, lb, tril):
    sg = _sigmoid(hf)
    f = lb + (1.0 - lb) * sg
    g = jnp.log(f)
    k = 1.0 - f
    sq = _sigmoid(hq)
    q = hq * sq
    b = _mm_tri_exact(tril, g)
    last_row = lax.broadcasted_iota(jnp.int32, b.shape, 0) == CH - 1
    b_last = jnp.sum(jnp.where(last_row, b, 0.0), axis=0, keepdims=True)
    c = 0.5 * b_last
    eb = jnp.exp(b)
    ea = jnp.exp(b - c)
    ek = jnp.exp(c - b)
    ed = jnp.exp(b_last - b)
    ebl = jnp.exp(b_last)
    return dict(sg=sg, f=f, k=k, sq=sq, q=q, eb=eb, ea=ea, ek=ek, ed=ed, ebl=ebl,
                qe=q * eb, qa=q * ea, ka=k * ek, kd=k * ed)


def _tri(lower):
    r = lax.broadcasted_iota(jnp.int32, (CH, CH), 0)
    c = lax.broadcasted_iota(jnp.int32, (CH, CH), 1)
    return (r >= c) if lower else (c >= r)


def _head_segment(p_ref, rows, j, hg):
    return p_ref[rows, j * HD * hg:(j + 1) * HD * hg]


def _head(a, k):
    return a[:, k * HD:(k + 1) * HD]


def _hgrn_fwd(proj, lbw, rb, hg):
    assert hg == HEADS
    t = proj.shape[0]
    ncb = rb // CH

    def body(p_ref, lb_ref, o_ref, st_ref, s_scr):
        @pl.when(pl.program_id(1) == 0)
        def _():
            s_scr[...] = jnp.zeros_like(s_scr)

        lb = _lower_bound(lb_ref)
        causal = _tri(True)
        tril = _tri3(True)
        heads = range(hg)

        def chunk(cc, carry):
            r0 = pl.multiple_of(cc * CH, CH)
            rows = pl.ds(r0, CH)
            e = _hgrn_chunk_fwd(_head_segment(p_ref, rows, 0, hg), _head_segment(p_ref, rows, 1, hg), lb, tril)
            v = _bf(_head_segment(p_ref, rows, 2, hg))
            sts = [s_scr[k] for k in heads]
            qa, ka, qe, kd = _bf(e["qa"]), _bf(e["ka"]), _bf(e["qe"]), _bf(e["kd"])
            a = [_bf(jnp.where(causal, _mm_nt(_head(qa, k), _head(ka, k)), 0.0)) for k in heads]
            o_inter = [_mm_nt(_head(qe, k), _bf(sts[k])) for k in heads]
            kv = [_mm_tn(_head(v, k), _head(kd, k)) for k in heads]
            o_intra = [_mm(a[k], _head(v, k)) for k in heads]
            for k in heads:
                st_ref[cc, k] = sts[k]
                o_ref[rows, k * HD:(k + 1) * HD] = o_inter[k] + o_intra[k]
                s_scr[k] = sts[k] * _head(e["ebl"], k) + kv[k]
            return carry

        lax.fori_loop(0, ncb, chunk, 0, unroll=4)

    return pl.pallas_call(
        body, grid=(HEADS // hg, t // rb),
        in_specs=[pl.BlockSpec((rb, 3 * HD * hg), lambda h, i: (i, h)), pl.BlockSpec((2, HD * hg), lambda h, i: (0, h))],
        out_specs=[pl.BlockSpec((rb, HD * hg), lambda h, i: (i, h)),
                   pl.BlockSpec((ncb, hg, HD, HD), lambda h, i: (i, h, 0, 0))],
        out_shape=[jax.ShapeDtypeStruct((t, D), F32), jax.ShapeDtypeStruct((t // CH, HEADS, HD, HD), F32)],
        scratch_shapes=[pltpu.VMEM((hg, HD, HD), F32)],
        name="hgrn_fwd", compiler_params=_cp(48),
    )(proj, lbw)


def _hgrn_bwd(proj, lbw, do_raw, states, rb, hg):
    assert hg == HEADS
    t = proj.shape[0]
    nblk = t // rb
    ncb = rb // CH
    wd = HD * hg

    def body(p_ref, lb_ref, do_ref, st_ref, dp_ref, dlb_ref, ds_scr):
        @pl.when(pl.program_id(1) == 0)
        def _():
            ds_scr[...] = jnp.zeros_like(ds_scr)
            dlb_ref[...] = jnp.zeros_like(dlb_ref)

        lb = _lower_bound(lb_ref)
        causal = _tri(True)
        tril = _tri3(True)
        triu = _tri3(False)
        last_row = lax.broadcasted_iota(jnp.int32, (CH, HD * hg), 0) == CH - 1
        row0 = lax.broadcasted_iota(jnp.int32, (8, HD * hg), 0) == 0
        heads = range(hg)
        wide = lambda parts: jnp.concatenate(parts, axis=1)

        def chunk(it, carry):
            cc = ncb - 1 - it
            r0 = pl.multiple_of(cc * CH, CH)
            rows = pl.ds(r0, CH)
            hq = _head_segment(p_ref, rows, 0, hg)
            e = _hgrn_chunk_fwd(hq, _head_segment(p_ref, rows, 1, hg), lb, tril)
            v = _bf(_head_segment(p_ref, rows, 2, hg))
            do = _bf(do_ref[rows, :])
            sts = [st_ref[cc, k] for k in heads]
            dsts = [ds_scr[k] for k in heads]
            dlb_acc = dlb_ref[...]
            qa, ka, qe, kd = _bf(e["qa"]), _bf(e["ka"]), _bf(e["qe"]), _bf(e["kd"])
            a = [_bf(jnp.where(causal, _mm_nt(_head(qa, k), _head(ka, k)), 0.0)) for k in heads]
            da = [_bf(jnp.where(causal, _mm_nt(_head(do, k), _head(v, k)), 0.0)) for k in heads]
            dqe = wide([_mm(_head(do, k), _bf(sts[k])) for k in heads])
            dkd = wide([_mm(_head(v, k), _bf(dsts[k])) for k in heads])
            dv_state = [_mm_nt(_head(kd, k), _bf(dsts[k])) for k in heads]
            ds_new = [_mm_tn(_head(do, k), _head(qe, k)) for k in heads]
            dv_intra = [_mm_tn(a[k], _head(do, k)) for k in heads]
            dqa = wide([_mm(da[k], _head(ka, k)) for k in heads])
            dka = wide([_mm_tn(da[k], _head(qa, k)) for k in heads])
            dv = wide([dv_intra[k] + dv_state[k] for k in heads])
            dbl = e["ebl"] * wide([jnp.sum(sts[k] * dsts[k], axis=0, keepdims=True) for k in heads])
            dq = dqe * e["eb"] + dqa * e["ea"]
            dk = dka * e["ek"] + dkd * e["ed"]
            dkd_kd = dkd * kd.astype(F32)
            db = dqe * qe.astype(F32) + dqa * qa.astype(F32) - dka * ka.astype(F32) - dkd_kd
            db = db + jnp.where(last_row, dbl + jnp.sum(dkd_kd, axis=0, keepdims=True), 0.0)
            dg = _mm_tri_exact(triu, db)
            df = dg / e["f"] - dk
            sg = e["sg"]
            sq = e["sq"]
            dhq = _bf(dq * (sq * (1.0 + hq * (1.0 - sq))))
            dhf = _bf(df * (1.0 - lb) * sg * (1.0 - sg))
            dhi = _bf(dv)
            dlb_new = dlb_acc + jnp.where(row0, jnp.sum(df * (1.0 - sg), axis=0, keepdims=True), 0.0)
            for k in heads:
                ds_scr[k] = ds_new[k] + dsts[k] * _head(e["ebl"], k)
            dp_ref[rows, 0:wd] = dhq
            dp_ref[rows, wd:2 * wd] = dhf
            dp_ref[rows, 2 * wd:3 * wd] = dhi
            dlb_ref[...] = dlb_new
            return carry

        lax.fori_loop(0, ncb, chunk, 0, unroll=2)

    rev = lambda h, i: (nblk - 1 - i, h)
    return pl.pallas_call(
        body, grid=(HEADS // hg, nblk),
        in_specs=[pl.BlockSpec((rb, 3 * HD * hg), rev), pl.BlockSpec((2, HD * hg), lambda h, i: (0, h)),
                  pl.BlockSpec((rb, HD * hg), rev), pl.BlockSpec((ncb, hg, HD, HD), lambda h, i: (nblk - 1 - i, h, 0, 0))],
        out_specs=[pl.BlockSpec((rb, 3 * HD * hg), rev), pl.BlockSpec((8, HD * hg), lambda h, i: (0, h))],
        out_shape=[jax.ShapeDtypeStruct((t, 3 * D), BF16), jax.ShapeDtypeStruct((8, D), F32)],
        scratch_shapes=[pltpu.VMEM((hg, HD, HD), F32)],
        name="hgrn_bwd", compiler_params=_cp(48),
    )(proj, lbw, do_raw, states)


def _kv_variants(tile, odd):
    low = lax.broadcasted_iota(jnp.int32, tile.shape, 1) < 64
    if odd:
        hi = jnp.where(low, 0.0, tile)
        lo = pltpu.roll(hi, 64, 1)
    else:
        lo = jnp.where(low, tile, 0.0)
        hi = pltpu.roll(lo, 64, 1)
    return _bf(lo), _bf(hi)


def _attn_masks(n):
    qi = lax.broadcasted_iota(jnp.int32, (AB, AB), 0)
    kj = lax.broadcasted_iota(jnp.int32, (AB, AB), 1)
    cur = kj <= qi
    return cur, cur | (n > 0), qi <= kj


def _kv_all(prev_ref, cur_ref):
    out = []
    for tl in range(2):
        cols = slice(tl * 128, (tl + 1) * 128)
        tile = jnp.concatenate([prev_ref[:, cols], cur_ref[:, cols]], axis=0)
        out.append(_kv_variants(tile, 0))
        out.append(_kv_variants(tile, 1))
    return out


def _window(a2, cur):
    return jnp.where(cur, a2[:, AB:], a2[:, :AB])


def _attn_softmax(scores, sinks, cur, ok):
    s = [jnp.where(ok, _window(s2, cur) * ATT_SCALE, NEG) for s2 in scores]
    m = [jnp.maximum(jnp.max(si, axis=-1, keepdims=True), sink) for si, sink in zip(s, sinks)]
    p = [jnp.exp(si - mi) for si, mi in zip(s, m)]
    es = [jnp.exp(sink - mi) for sink, mi in zip(sinks, m)]
    inv = [1.0 / (jnp.sum(pi, axis=-1, keepdims=True) + ei) for pi, ei in zip(p, es)]
    return [pi * ii for pi, ii in zip(p, inv)], [ei * ii for ei, ii in zip(es, inv)]


def _spread(pc, cur):
    return jnp.concatenate([jnp.where(cur, 0.0, pc), jnp.where(cur, pc, 0.0)], axis=1)


def _spread_t(pct, cur_t):
    return jnp.concatenate([jnp.where(cur_t, 0.0, pct), jnp.where(cur_t, pct, 0.0)], axis=0)


def _attn_fwd(proj, sinks):
    t = proj.shape[0]
    nb = t // AB

    nsub = 2
    assert nb % nsub == 0

    def body(q_ref, kc_ref, kp_ref, vc_ref, vp_ref, sink_ref, o_ref):
        sinks_v = [sink_ref[0, h] for h in range(QH)]
        heads = [(j, ab) for j in range(8) for ab in range(2)]
        for sb in range(nsub):
            rows = pl.ds(AB * sb, AB)
            before = pl.ds(AB * (sb - 1), AB)
            cur, ok, _ = _attn_masks(nsub * pl.program_id(0) + sb)
            kvars = _kv_all(kp_ref if sb == 0 else kc_ref.at[before, :], kc_ref.at[rows, :])
            vvars = _kv_all(vp_ref if sb == 0 else vc_ref.at[before, :], vc_ref.at[rows, :])
            qps = [_bf(q_ref[rows, 128 * j:128 * (j + 1)]) for j in range(8)]
            scores = [_mm_nt(qps[j], kvars[j // 2][ab]) for j, ab in heads]
            pcs, _ = _attn_softmax(scores, sinks_v, cur, ok)
            parts = [_mm(_bf(_spread(pcs[h], cur)), vvars[j // 2][ab]) for h, (j, ab) in enumerate(heads)]
            for j in range(8):
                o_ref[rows, 128 * j:128 * (j + 1)] = parts[2 * j] + parts[2 * j + 1]

    prev = lambda n: jnp.maximum(nsub * n - 1, 0)
    step = nsub * AB
    return pl.pallas_call(
        body, grid=(nb // nsub,),
        in_specs=[pl.BlockSpec((step, D), lambda n: (n, C_AQ // D)),
                  pl.BlockSpec((step, 256), lambda n: (n, C_AK // 256)),
                  pl.BlockSpec((AB, 256), lambda n: (prev(n), C_AK // 256)),
                  pl.BlockSpec((step, 256), lambda n: (n, C_AV // 256)),
                  pl.BlockSpec((AB, 256), lambda n: (prev(n), C_AV // 256)),
                  pl.BlockSpec(memory_space=pltpu.SMEM)],
        out_specs=pl.BlockSpec((step, D), lambda n: (n, 0)),
        out_shape=jax.ShapeDtypeStruct((t, D), F32),
        name="attn_fwd", compiler_params=_cp(32),
    )(proj, proj, proj, proj, proj, sinks)


def _attn_bwd(proj, sinks, do_a):
    t = proj.shape[0]
    nb = t // AB
    nsub = 2
    assert nb % nsub == 0
    steps = nb // nsub
    step = nsub * AB

    def body(q_ref, kc_ref, kp_ref, vc_ref, vp_ref, do_ref, sink_ref, dq_ref, dkv_ref, dsink_ref, carry):
        n = pl.program_id(0)

        @pl.when(n == 0)
        def _():
            dsink_ref[...] = jnp.zeros_like(dsink_ref)
            carry[...] = jnp.zeros_like(carry)

        def one_block(sb):
            rows = pl.ds(AB * sb, AB)
            before = pl.ds(AB * (sb - 1), AB)
            cur, ok, cur_t = _attn_masks(nsub * n + sb)
            low = lax.broadcasted_iota(jnp.int32, (2 * AB, 128), 1) < 64
            lane = lax.broadcasted_iota(jnp.int32, (8, 128), 1)
            row0 = lax.broadcasted_iota(jnp.int32, (8, 128), 0) == 0
            kvars = _kv_all(kp_ref if sb == 0 else kc_ref.at[before, :], kc_ref.at[rows, :])
            vvars = _kv_all(vp_ref if sb == 0 else vc_ref.at[before, :], vc_ref.at[rows, :])
            qps = [_bf(q_ref[rows, 128 * j:128 * (j + 1)]) for j in range(8)]
            dops = [_bf(do_ref[rows, 128 * j:128 * (j + 1)]) for j in range(8)]
            heads = [(j, ab) for j in range(8) for ab in range(2)]
            scores = [_mm_nt(qps[j], kvars[j // 2][ab]) for j, ab in heads]
            dps = [_mm_nt(dops[j], vvars[j // 2][ab]) for j, ab in heads]
            pcs, pss = _attn_softmax(scores, [sink_ref[0, h] for h in range(QH)], cur, ok)
            dpcs = [_window(dp2, cur) for dp2 in dps]
            rss = [jnp.sum(pc * dpc, axis=-1, keepdims=True) for pc, dpc in zip(pcs, dpcs)]
            dscs = [pc * (dpc - rs) for pc, dpc, rs in zip(pcs, dpcs, rss)]
            dsink = jnp.zeros((8, 128), F32)
            for h in range(QH):
                dsink = dsink + jnp.where(row0 & (lane == h), -jnp.sum(pss[h] * rss[h]), 0.0)
            dq_terms = [_mm(_bf(_spread(dscs[h], cur)), kvars[j // 2][ab]) for h, (j, ab) in enumerate(heads)]
            for j in range(8):
                dq_ref[rows, 128 * j:128 * (j + 1)] = _bf((dq_terms[2 * j] + dq_terms[2 * j + 1]) * ATT_SCALE)
            dsc_t = [_bf(_spread_t(dsc.T, cur_t)) for dsc in dscs]
            pc_t = [_bf(_spread_t(pc.T, cur_t)) for pc in pcs]
            dk_terms = [_mm(dsc_t[h], qps[j]) for h, (j, ab) in enumerate(heads)]
            dv_terms = [_mm(pc_t[h], dops[j]) for h, (j, ab) in enumerate(heads)]
            dk_ab = [[dk_terms[4 * g + ab] + dk_terms[4 * g + 2 + ab] for ab in range(2)] for g in range(4)]
            dv_ab = [[dv_terms[4 * g + ab] + dv_terms[4 * g + 2 + ab] for ab in range(2)] for g in range(4)]
            dkts, dvts = [], []
            for tl in range(2):
                ke, ko = dk_ab[2 * tl], dk_ab[2 * tl + 1]
                ve, vo = dv_ab[2 * tl], dv_ab[2 * tl + 1]
                dkts.append((jnp.where(low, ke[0], 0.0) + pltpu.roll(jnp.where(low, 0.0, ke[1]), 64, 1)
                             + jnp.where(low, 0.0, ko[1]) + pltpu.roll(jnp.where(low, ko[0], 0.0), 64, 1)) * ATT_SCALE)
                dvts.append(jnp.where(low, ve[0], 0.0) + pltpu.roll(jnp.where(low, 0.0, ve[1]), 64, 1)
                            + jnp.where(low, 0.0, vo[1]) + pltpu.roll(jnp.where(low, vo[0], 0.0), 64, 1))
            return dkts, dvts, dsink

        @pl.when(n < steps)
        def _():
            (dk0, dv0, ds0), (dk1, dv1, ds1) = one_block(0), one_block(1)
            first, second = slice(0, AB), slice(AB, 2 * AB)
            for tl in range(2):
                for cols, g0, g1 in ((slice(tl * 128, (tl + 1) * 128), dk0[tl], dk1[tl]),
                                     (slice(256 + tl * 128, 256 + (tl + 1) * 128), dv0[tl], dv1[tl])):
                    dkv_ref[first, cols] = _bf(carry[first, cols])
                    dkv_ref[second, cols] = _bf(carry[second, cols] + g0[first])
                    carry[first, cols] = g0[second] + g1[first]
                    carry[second, cols] = g1[second]
            dsink_ref[...] += ds0 + ds1

        @pl.when(n == steps)
        def _():
            dkv_ref[...] = _bf(carry[...])

    cur = lambda n: jnp.minimum(n, steps - 1)
    last = lambda n: jnp.clip(n - 1, 0, steps - 1)
    prev = lambda n: jnp.clip(nsub * n - 1, 0, nb - 1)
    return pl.pallas_call(
        body, grid=(steps + 1,),
        in_specs=[pl.BlockSpec((step, D), lambda n: (cur(n), C_AQ // D)),
                  pl.BlockSpec((step, 256), lambda n: (cur(n), C_AK // 256)),
                  pl.BlockSpec((AB, 256), lambda n: (prev(n), C_AK // 256)),
                  pl.BlockSpec((step, 256), lambda n: (cur(n), C_AV // 256)),
                  pl.BlockSpec((AB, 256), lambda n: (prev(n), C_AV // 256)),
                  pl.BlockSpec((step, D), lambda n: (cur(n), 0)),
                  pl.BlockSpec(memory_space=pltpu.SMEM)],
        out_specs=[pl.BlockSpec((step, D), lambda n: (cur(n), 0)),
                   pl.BlockSpec((step, 512), lambda n: (last(n), 0)),
                   pl.BlockSpec((8, 128), lambda n: (0, 0))],
        out_shape=[jax.ShapeDtypeStruct((t, D), BF16), jax.ShapeDtypeStruct((t, 512), BF16),
                   jax.ShapeDtypeStruct((8, 128), F32)],
        scratch_shapes=[pltpu.VMEM((step, 512), F32)],
        name="attn_bwd", compiler_params=_cp(40),
    )(proj, proj, proj, proj, proj, do_a, sinks)


def _silu_and_grad(v):
    s = _sigmoid(v)
    return v * s, s * (1.0 + v * (1.0 - s))


def _tail(o_raw, o_a, proj, x2d, tgt, wbh, wba, wout, hnw, fnw, tb):
    t = x2d.shape[0]

    def body(or_ref, oa_ref, hg_ref, ag0, ag1, mh0, mh1, ma0, ma1, x_ref, t_ref, wbh_ref, wba_ref, wout_ref, hnw_ref,
             fnw_ref, dx2_ref, dor_ref, doa_ref, dhg_ref, dagm_ref, gh_ref, ga_ref, mg_ref, dyh_ref, dya_ref, dx2b_ref,
             sums_ref):
        @pl.when(pl.program_id(0) == 0)
        def _():
            sums_ref[...] = jnp.zeros_like(sums_ref)

        halves = lambda a, b: jnp.concatenate([a[...], b[...]], axis=1)
        hnw_v = hnw_ref[...]
        fnw_v = fnw_ref[...]
        o = or_ref[...]
        rs, xhs = [], []
        for h in range(HEADS):
            oh = o[:, h * HD:(h + 1) * HD]
            r = lax.rsqrt(jnp.mean(oh * oh, axis=-1, keepdims=True) + EPS)
            rs.append(r)
            xhs.append(oh * r)
        xh = jnp.concatenate(xhs, axis=1)
        on = xh * hnw_v
        sil_hg, dsil_hg = _silu_and_grad(hg_ref[...])
        gh_b = _bf(on * sil_hg)
        y_h = _mm(gh_b, wbh_ref[...])
        oa = oa_ref[...]
        sil_ag, dsil_ag = _silu_and_grad(halves(ag0, ag1))
        ga_b = _bf(oa * sil_ag)
        y_a = _mm(ga_b, wba_ref[...])
        s_mh = _sigmoid(halves(mh0, mh1))
        s_ma = _sigmoid(halves(ma0, ma1))
        mg_b = _bf(s_mh * y_h + s_ma * y_a)
        x2 = x_ref[...] + _mm(mg_b, wout_ref[...])
        r2 = lax.rsqrt(jnp.mean(x2 * x2, axis=-1, keepdims=True) + EPS)
        xh2 = x2 * r2
        err = xh2 * fnw_v - t_ref[...]
        loss = 0.5 * jnp.sum(jnp.mean(err * err, axis=-1, keepdims=True))
        dy = err * (1.0 / D)
        dfnw = jnp.sum(dy * xh2, axis=0, keepdims=True)
        dxh2 = dy * fnw_v
        dx2 = r2 * (dxh2 - xh2 * jnp.mean(dxh2 * xh2, axis=-1, keepdims=True))
        dx2_ref[...] = dx2
        dx2_b = _bf(dx2)
        dmg = _mm_nt(dx2_b, wout_ref[...])
        dmg_h = dmg * s_mh
        dmg_a = dmg * s_ma
        dyh_b = _bf(dmg_h)
        dya_b = _bf(dmg_a)
        dagm_ref[:, D:2 * D] = _bf(dmg_h * y_h * (1.0 - s_mh))
        dagm_ref[:, 2 * D:3 * D] = _bf(dmg_a * y_a * (1.0 - s_ma))
        dgh = _mm_nt(dyh_b, wbh_ref[...])
        dga = _mm_nt(dya_b, wba_ref[...])
        doa_ref[...] = dga * sil_ag
        dagm_ref[:, 0:D] = _bf(dga * oa * dsil_ag)
        dhg_ref[...] = _bf(dgh * on * dsil_hg)
        don = dgh * sil_hg
        dhnw = jnp.sum(don * xh, axis=0, keepdims=True)
        dxh = don * hnw_v
        dos = []
        for h in range(HEADS):
            sl = slice(h * HD, (h + 1) * HD)
            dos.append(rs[h] * (dxh[:, sl] - xhs[h] * jnp.mean(dxh[:, sl] * xhs[h], axis=-1, keepdims=True)))
        dor_ref[...] = jnp.concatenate(dos, axis=1)
        gh_ref[...] = gh_b
        ga_ref[...] = ga_b
        mg_ref[...] = mg_b
        dyh_ref[...] = dyh_b
        dya_ref[...] = dya_b
        dx2b_ref[...] = dx2_b
        row = lax.broadcasted_iota(jnp.int32, (8, D), 0)
        sums_ref[...] += jnp.where(row == 0, dfnw, 0.0) + jnp.where(row == 1, dhnw, 0.0) + jnp.where(row == 2, loss, 0.0)

    rowblk = lambda c: pl.BlockSpec((tb, D), lambda i: (i, c))
    half = lambda c: pl.BlockSpec((tb, 512), lambda i: (i, c))
    full = lambda shape: pl.BlockSpec(shape, lambda i: (0, 0))
    return pl.pallas_call(
        body, grid=(t // tb,),
        in_specs=[rowblk(0), rowblk(0), rowblk(C_HG // D), half(C_AG // 512), half(C_AG // 512 + 1), half(C_MH // 512),
                  half(C_MH // 512 + 1), half(C_MA // 512), half(C_MA // 512 + 1), rowblk(0), rowblk(0),
                  full((D, D)), full((D, D)), full((D, D)), full((1, D)), full((1, D))],
        out_specs=[rowblk(0), rowblk(0), rowblk(0), rowblk(0), pl.BlockSpec((tb, 3 * D), lambda i: (i, 0))]
        + [rowblk(0)] * 6 + [full((8, D))],
        out_shape=[jax.ShapeDtypeStruct((t, D), F32)] * 3
        + [jax.ShapeDtypeStruct((t, D), BF16), jax.ShapeDtypeStruct((t, 3 * D), BF16)]
        + [jax.ShapeDtypeStruct((t, D), BF16)] * 6 + [jax.ShapeDtypeStruct((8, D), F32)],
        name="tail", compiler_params=_cp(56),
    )(o_raw, o_a, proj, proj, proj, proj, proj, proj, proj, x2d, tgt, wbh, wba, wout, hnw, fnw)


def _wgrad3(gh, dyh, ga, dya, mg, dx2b, tk):
    t = dyh.shape[0]

    def body(a0, b0, a1, b1, a2, b2, o0, o1, o2):
        @pl.when(pl.program_id(0) == 0)
        def _():
            o0[...] = jnp.zeros_like(o0)
            o1[...] = jnp.zeros_like(o1)
            o2[...] = jnp.zeros_like(o2)

        o0[...] += _mm_tn(a0[...], b0[...])
        o1[...] += _mm_tn(a1[...], b1[...])
        o2[...] += _mm_tn(a2[...], b2[...])

    blk = pl.BlockSpec((tk, D), lambda k: (k, 0))
    out = pl.BlockSpec((D, D), lambda k: (0, 0))
    return pl.pallas_call(
        body, grid=(t // tk,), in_specs=[blk] * 6, out_specs=[out] * 3,
        out_shape=[jax.ShapeDtypeStruct((D, D), F32)] * 3,
        name="wgrad3", compiler_params=_cp(48),
    )(gh, dyh, ga, dya, mg, dx2b)


def _inproj_wgrad_piece(xnt, piece, nb, name):
    t = xnt.shape[1]
    width = piece.shape[1]

    def body(xnt_ref, p_ref, o_ref):
        o_ref[...] = _mm(xnt_ref[...], p_ref[...])

    return pl.pallas_call(
        body, grid=(width // nb,),
        in_specs=[pl.BlockSpec((D, t), lambda j: (0, 0), pipeline_mode=pl.Buffered(1)),
                  pl.BlockSpec((t, nb), lambda j: (0, j))],
        out_specs=pl.BlockSpec((D, nb), lambda j: (0, j)),
        out_shape=jax.ShapeDtypeStruct((D, width), F32),
        name=name, compiler_params=_cp(56),
    )(xnt, piece)


def _inproj_dgrad(pieces, w_p, x2d, dx2, norm_w, tb, after):
    t = x2d.shape[0]

    def body(*refs):
        piece_refs = refs[:len(pieces)]
        w_ref, x_ref, dx2_ref, nw_ref, _, gx_ref, dnw_ref = refs[len(pieces):]

        @pl.when(pl.program_id(0) == 0)
        def _():
            dnw_ref[...] = jnp.zeros_like(dnw_ref)

        dxn = None
        off = 0
        for p in piece_refs:
            width = p.shape[1]
            for q in range(w_ref.shape[0]):
                lo, hi = max(off, q * PAIR), min(off + width, (q + 1) * PAIR)
                if lo < hi:
                    term = _mm_nt(p[:, lo - off:hi - off], w_ref[q, :, lo - q * PAIR:hi - q * PAIR])
                    dxn = term if dxn is None else dxn + term
            off += width
        xv = x_ref[...]
        r = lax.rsqrt(jnp.mean(xv * xv, axis=-1, keepdims=True) + EPS)
        xh = xv * r
        dxh = dxn * nw_ref[...]
        gx_ref[...] = dx2_ref[...] + r * (dxh - xh * jnp.mean(dxh * xh, axis=-1, keepdims=True))
        row0 = lax.broadcasted_iota(jnp.int32, (8, D), 0) == 0
        dnw_ref[...] += jnp.where(row0, jnp.sum(dxn * xh, axis=0, keepdims=True), 0.0)

    rowblk = pl.BlockSpec((tb, D), lambda i: (i, 0))
    return pl.pallas_call(
        body, grid=(t // tb,),
        in_specs=[pl.BlockSpec((tb, p.shape[1]), lambda i: (i, 0)) for p in pieces]
        + [pl.BlockSpec(w_p.shape, lambda i: (0, 0, 0), pipeline_mode=pl.Buffered(1)), rowblk, rowblk,
           pl.BlockSpec((1, D), lambda i: (0, 0)), pl.BlockSpec(memory_space=pl.ANY)],
        out_specs=[rowblk, pl.BlockSpec((8, D), lambda i: (0, 0))],
        out_shape=[jax.ShapeDtypeStruct((t, D), F32), jax.ShapeDtypeStruct((8, D), F32)],
        name="inproj_dgrad", compiler_params=_cp(60),
    )(*pieces, w_p, x2d, dx2, norm_w, after)


def _adamw_math(w, g, m, v):
    m = B1 * m + (1.0 - B1) * g
    v = B2 * v + (1.0 - B2) * (g * g)
    m_hat = m / (1.0 - B1 ** STEP)
    v_hat = v / (1.0 - B2 ** STEP)
    delta = -LR * (m_hat / (jnp.sqrt(v_hat) + ADAM_EPS) + WD * w)
    return delta, m, v


def _adamw_shard(recv, sums, chip, w, m, v, rows, name):
    nparts, nr, nc = recv.shape

    def body(chip_ref, own_ref, p_ref, w_ref, m_ref, v_ref, g_ref, d_ref, nm_ref, nv_ref):
        g = own_ref[0].astype(F32)
        for s in range(nparts):
            g = g + p_ref[s].astype(F32)
        d, nm, nv = _adamw_math(w_ref[...], g, m_ref[...], v_ref[...])
        g_ref[...] = g
        d_ref[...] = d
        nm_ref[...] = nm
        nv_ref[...] = nv

    blk = pl.BlockSpec((rows, nc), lambda i, chip_ref: (i, 0))
    return pl.pallas_call(
        body,
        grid_spec=pltpu.PrefetchScalarGridSpec(
            num_scalar_prefetch=1, grid=(nr // rows,),
            in_specs=[pl.BlockSpec((1, rows, nc), lambda i, chip_ref: (chip_ref[0], i, 0)),
                      pl.BlockSpec((nparts, rows, nc), lambda i, chip_ref: (0, i, 0)), blk, blk, blk],
            out_specs=[blk] * 4),
        out_shape=[jax.ShapeDtypeStruct((nr, nc), F32)] * 4,
        name=name, compiler_params=_cp(48),
    )(chip, sums, recv, w, m, v)


def _adamw_sum8(parts, w, m, v, after, name):
    def body(p_ref, w_ref, m_ref, v_ref, _, g_ref, d_ref, nm_ref, nv_ref):
        g = p_ref[0].astype(F32)
        for s in range(1, NDEV):
            g = g + p_ref[s].astype(F32)
        d, nm, nv = _adamw_math(w_ref[...], g, m_ref[...], v_ref[...])
        g_ref[...] = g
        d_ref[...] = d
        nm_ref[...] = nm
        nv_ref[...] = nv

    vm = pl.BlockSpec(memory_space=pltpu.VMEM)
    return pl.pallas_call(
        body, out_shape=[jax.ShapeDtypeStruct(w.shape, F32)] * 4,
        in_specs=[vm, vm, vm, vm, pl.BlockSpec(memory_space=pl.ANY)], out_specs=[vm] * 4, name=name,
    )(parts, w, m, v, after)


SMALL_ROWS = dict(norm_w=0, lower_bound=1, hgrn_norm_w=3, final_norm_w=4, sinks=5, loss=6)


def _pack_small_grads(dnw, dlb, sums, dsink):
    def body(dnw_ref, dlb_ref, sums_ref, dsink_ref, o_ref):
        o_ref[...] = jnp.zeros_like(o_ref)
        o_ref[0:1, :] = dnw_ref[0:1, :]
        o_ref[1:2, :] = dlb_ref[0:1, :]
        o_ref[3:4, :] = sums_ref[1:2, :]
        o_ref[4:5, :] = sums_ref[0:1, :]
        o_ref[5:6, 0:128] = dsink_ref[0:1, :]
        o_ref[6:7, :] = sums_ref[2:3, :]

    return pl.pallas_call(body, out_shape=jax.ShapeDtypeStruct((8, D), F32), name="pack_small_grads")(dnw, dlb, sums, dsink)


def _adamw_small(parts, ws, ms, vs):
    shapes = [a.shape for a in ws]

    def body(p_ref, *refs):
        w, m, v = refs[0:5], refs[5:10], refs[10:15]
        outs = [refs[15 + 5 * i:20 + 5 * i] for i in range(4)]
        loss_ref = refs[35]

        def total(row, width):
            g = p_ref[0, row:row + 1, 0:width]
            for s in range(1, NDEV):
                g = g + p_ref[s, row:row + 1, 0:width]
            return g

        loss_ref[...] = total(6, 128)
        lb = _lower_bound(w[1])
        ga0 = total(1, D) * lb * (1.0 - lb)
        grads = [total(0, D), None, total(3, D), total(4, D), total(5, QH)]
        for i in (0, 2, 3, 4):
            res = (grads[i],) + _adamw_math(w[i][...], grads[i], m[i][...], v[i][...])
            for o, val in zip(outs, res):
                o[i][...] = val
        for r, g in ((0, ga0), (1, -ga0)):
            res = (g,) + _adamw_math(w[1][r:r + 1, :], g, m[1][r:r + 1, :], v[1][r:r + 1, :])
            for o, val in zip(outs, res):
                o[1][r:r + 1, :] = val

    res = pl.pallas_call(
        body, out_shape=[jax.ShapeDtypeStruct(s, F32) for s in shapes] * 4 + [jax.ShapeDtypeStruct((1, 128), F32)],
        name="adamw_small",
    )(parts, *ws, *ms, *vs)
    return [res[5 * i:5 * i + 5] for i in range(4)], res[20][0, 0]


def kernel(x, norm_w, w_in, hgrn_lower_bound, hgrn_norm_w, w_branch_hgrn, attn_sinks, w_branch_attn, w_out, final_norm_w, loss_target, m_norm_w, m_w_in, m_hgrn_lower_bound, m_hgrn_norm_w, m_w_branch_hgrn, m_attn_sinks, m_w_branch_attn, m_w_out, m_final_norm_w, v_norm_w, v_w_in, v_hgrn_lower_bound, v_hgrn_norm_w, v_w_branch_hgrn, v_attn_sinks, v_w_branch_attn, v_w_out, v_final_norm_w):
    t = x.shape[1]
    x2d = x.reshape(t, D)
    tgt = loss_target.reshape(t, D)
    fnw = final_norm_w.reshape(1, D)
    row_blk = min(256, t)
    big_blk = min(512, t)

    chip = (2 * lax.axis_index("x") + lax.axis_index("y")).astype(jnp.int32).reshape(1)
    w_p, xn, xnt, proj_own = _gather_in_projection(w_in[0], x2d, norm_w)
    wbh, wba, wout = (g.reshape(D, D) for g in _gather_square(
        [w_branch_hgrn[0].astype(BF16), w_branch_attn[0].astype(BF16), w_out[0].astype(BF16)], after=w_p))

    proj = _inproj_fwd(xn, w_p, proj_own, chip, min(1024, t))
    o_raw, states = _hgrn_fwd(proj, hgrn_lower_bound, big_blk, HGRN_GROUP)
    o_a = _attn_fwd(proj, attn_sinks)
    (dx2, do_raw, do_a, d_hg, d_agm, gh, ga, mg, dyh, dya, dx2b, sums) = _tail(
        o_raw, o_a, proj, x2d, tgt, wbh, wba, wout, hgrn_norm_w, fnw, row_blk)
    dwbh, dwba, dwout = _wgrad3(gh, dyh, ga, dya, mg, dx2b, big_blk)
    d_aq, d_kv, dsink = _attn_bwd(proj, attn_sinks, do_a)
    d_hgrn, dlb = _hgrn_bwd(proj, hgrn_lower_bound, do_raw, states, big_blk, HGRN_GROUP)
    pieces = (d_hgrn, d_hg, d_aq, d_kv, d_agm)
    dw_pieces = [_inproj_wgrad_piece(xnt, p, CB, "inproj_wgrad_" + n)
                 for p, n in zip(pieces, ("hgrn", "hgate", "aq", "kv", "gates"))]

    dwin_r = jnp.concatenate(dw_pieces, axis=1).reshape(D, NDEV, IN_SHARD).transpose(1, 0, 2).astype(BF16)
    slots = lambda a: a.reshape(NDEV, ROW_SHARD, D).astype(BF16)
    got = _exchange_pair([dwin_r])
    core = lax.axis_index("c").astype(jnp.int32).reshape(1)
    s_in = _pair_sum(dwin_r, got[0], core, 4 * ROW_SHARD, "pair_sum_w_in")
    rin, = _exchange_chips([s_in])
    rbh, rba, rout = _exchange_square([slots(dwbh), slots(dwba), slots(dwout)])
    grad_x, dnw = _inproj_dgrad(pieces, w_p, x2d, dx2, norm_w, big_blk, after=s_in)
    rsm = _exchange_small(_pack_small_grads(dnw, dlb, sums, dsink))
    g_in, d_in, nm_in, nv_in = _adamw_shard(rin, s_in, chip, w_in[0], m_w_in[0], v_w_in[0], 128, "adamw_w_in")
    g_bh, d_bh, nm_bh, nv_bh = _adamw_sum8(
        rbh, w_branch_hgrn[0], m_w_branch_hgrn[0], v_w_branch_hgrn[0], dnw, "adamw_w_bh")
    g_ba, d_ba, nm_ba, nv_ba = _adamw_sum8(
        rba, w_branch_attn[0], m_w_branch_attn[0], v_w_branch_attn[0], dnw, "adamw_w_ba")
    g_out, d_out, nm_out, nv_out = _adamw_sum8(rout, w_out[0], m_w_out[0], v_w_out[0], dnw, "adamw_w_out")
    (sg, sd, sm, sv), loss = _adamw_small(
        rsm,
        (norm_w, hgrn_lower_bound, hgrn_norm_w, fnw, attn_sinks),
        (m_norm_w, m_hgrn_lower_bound, m_hgrn_norm_w, m_final_norm_w.reshape(1, D), m_attn_sinks),
        (v_norm_w, v_hgrn_lower_bound, v_hgrn_norm_w, v_final_norm_w.reshape(1, D), v_attn_sinks))

    def group(s, w_in_v, bh, ba, out):
        nw, lb, hnw, fn, sinks = s
        return (nw, w_in_v[None], lb, hnw, bh[None], sinks, ba[None], out[None], fn.reshape(D))

    return (loss, grad_x.reshape(1, t, D),
            *group(sg, g_in, g_bh, g_ba, g_out), *group(sd, d_in, d_bh, d_ba, d_out),
            *group(sm, nm_in, nm_bh, nm_ba, nm_out), *group(sv, nv_in, nv_bh, nv_ba, nv_out))
```

```python
import jax
import jax.numpy as jnp
from jax import lax
from jax.experimental import pallas as pl
from jax.experimental.pallas import tpu as pltpu
from jax.experimental.pallas import tpu_sc as plsc

F32 = jnp.float32
BF16 = jnp.bfloat16

D = 1024
DIN = 8704
NDEV = 8
IN_SHARD = DIN // NDEV
PAIR = 2 * IN_SHARD
ROW_SHARD = D // NDEV
HEADS = 8
HD = 128
CH = 64
HGRN_GROUP = 8
QH = 16
AB = 128
EPS = 1e-6
NEG = -1e30
ATT_SCALE = 0.125

C_HG = 3072
C_AQ = 4096
C_AK = 5120
C_AV = 5376
C_AG = 5632
C_MH = 6656
C_MA = 7680
CB = 512

LR = 0.001
B1 = 0.9
B2 = 0.999
ADAM_EPS = 1e-08
WD = 0.01
STEP = 10

MESH = pl.DeviceIdType.MESH


def _cp(vmem_mb):
    return pltpu.CompilerParams(vmem_limit_bytes=vmem_mb * 1024 * 1024)


def _mm(a, b):
    return jnp.dot(a, b, preferred_element_type=F32)


def _mm_nt(a, b):
    return lax.dot_general(a, b, (((1,), (1,)), ((), ())), preferred_element_type=F32)


def _mm_tn(a, b):
    return lax.dot_general(a, b, (((0,), (0,)), ((), ())), preferred_element_type=F32)


def _tri3(lower):
    r = lax.broadcasted_iota(jnp.int32, (CH, 3 * CH), 0)
    c = lax.broadcasted_iota(jnp.int32, (CH, 3 * CH), 1)
    c = jnp.where(c >= 2 * CH, c - 2 * CH, jnp.where(c >= CH, c - CH, c))
    return ((r >= c) if lower else (c >= r)).astype(BF16)


def _mm_tri_exact(tri3, g):
    g1 = g.astype(BF16)
    r1 = g - g1.astype(F32)
    g2 = r1.astype(BF16)
    g3 = (r1 - g2.astype(F32)).astype(BF16)
    return _mm(tri3, jnp.concatenate([g1, g2, g3], axis=0))


def _sigmoid(v):
    return 0.5 * jnp.tanh(0.5 * v) + 0.5


def _bf(v):
    return v.astype(BF16)


def _place():
    x, y, c = lax.axis_index("x"), lax.axis_index("y"), lax.axis_index("c")
    return (x, y, c), (x, y, 1 - c), [(1 - x, y), (x, 1 - y), (1 - x, 1 - y)]


def _dev_index(px, py, pc):
    return 4 * px + 2 * py + pc


def _gather_in_projection(w_in_s, x2d, norm_w):
    half = D // 2
    t = x2d.shape[0]
    prep_rows = min(512, t)
    nprep = t // prep_rows

    def body(win_ref, x_hbm, nw_ref, wp_g, xn_hbm, xnt_hbm, proj_hbm, give, take, mine, xbuf, xnbuf, xntbuf, w_own, pbuf,
             send_sems, recv_sems, loc_sem, swap_sems, in_sems, out_sems, own_sem):
        (x, y, c), sibling, chips = _place()
        give[...] = win_ref[pl.ds(pl.multiple_of(half * (1 - c), half), half), :].astype(BF16)
        swap = pltpu.make_async_remote_copy(src_ref=give, dst_ref=take, send_sem=swap_sems.at[0], recv_sem=swap_sems.at[1],
                                            device_id=sibling, device_id_type=MESH)
        swap.start()
        swap.wait()
        own = win_ref[pl.ds(pl.multiple_of(half * c, half), half), :]
        other = take[...].astype(F32)
        mine[...] = jnp.where(c == 0, jnp.concatenate([own, other], axis=1),
                              jnp.concatenate([other, own], axis=1)).astype(BF16)

        def place(px, py, pc):
            return wp_g.at[2 * px + py, pl.ds(pl.multiple_of(half * pc, half), half), :]

        def copy(kind, origin, to, src=mine):
            return pltpu.make_async_remote_copy(
                src_ref=src, dst_ref=place(*origin), send_sem=send_sems.at[kind], recv_sem=recv_sems.at[kind],
                device_id=to, device_id_type=MESH)

        me = (x, y, c)
        local = pltpu.make_async_copy(mine, place(*me), loc_sem)
        local.start()
        first = [copy(0, me, sibling)] + [copy(1 + j, me, (*chip, c)) for j, chip in enumerate(chips)]
        for cp in first:
            cp.start()

        copy(0, (x, y, 1 - c), me).wait_recv()
        local.wait()
        my_chip = 2 * x + y
        fetch = pltpu.make_async_copy(wp_g.at[my_chip], w_own, own_sem)
        fetch.start()

        def rows_of(i):
            return pl.ds(pl.multiple_of(i * prep_rows, prep_rows), prep_rows)

        def load(i, slot):
            return pltpu.make_async_copy(x_hbm.at[rows_of(i), :], xbuf.at[slot], in_sems.at[slot])

        def stores(i, slot):
            own_cols = pl.ds(pl.multiple_of(my_chip * PAIR, 128), PAIR)
            return (pltpu.make_async_copy(xnbuf.at[slot], xn_hbm.at[rows_of(i), :], out_sems.at[slot, 0]),
                    pltpu.make_async_copy(xntbuf.at[slot], xnt_hbm.at[:, rows_of(i)], out_sems.at[slot, 1]),
                    pltpu.make_async_copy(pbuf.at[slot], proj_hbm.at[rows_of(i), own_cols], out_sems.at[slot, 2]))

        load(0, 0).start()
        fetch.wait()

        def prep(i, carry):
            slot = lax.rem(i, 2)
            load(i, slot).wait()

            @pl.when(i + 1 < nprep)
            def _():
                load(i + 1, 1 - slot).start()

            @pl.when(i >= 2)
            def _():
                for cp in stores(i - 2, slot):
                    cp.wait()

            xv = xbuf[slot]
            xn = (xv * lax.rsqrt(jnp.mean(xv * xv, axis=-1, keepdims=True) + EPS)) * nw_ref[...]
            xn_b = xn.astype(BF16)
            xnbuf[slot] = xn_b
            xntbuf[slot] = xn.T.astype(BF16)
            pbuf[slot] = _mm(xn_b, w_own[...])
            for cp in stores(i, slot):
                cp.start()
            return carry

        lax.fori_loop(0, nprep, prep, 0)
        for i in range(max(nprep - 2, 0), nprep):
            for cp in stores(i, i % 2):
                cp.wait()

        passed = []
        for j, chip in enumerate(chips):
            copy(1 + j, (*chip, c), me).wait_recv()
            cp = copy(4 + j, (*chip, c), sibling, src=place(*chip, c))
            cp.start()
            passed.append(cp)
        for j, chip in enumerate(chips):
            copy(4 + j, (*chip, 1 - c), me).wait_recv()
        for cp in first + passed:
            cp.wait_send()

    vm = pl.BlockSpec(memory_space=pltpu.VMEM)
    hbm = pl.BlockSpec(memory_space=pl.ANY)
    return pl.pallas_call(
        body,
        out_shape=[jax.ShapeDtypeStruct((NDEV // 2, D, PAIR), BF16), jax.ShapeDtypeStruct((t, D), BF16),
                   jax.ShapeDtypeStruct((D, t), BF16), jax.ShapeDtypeStruct((t, DIN), F32)],
        in_specs=[vm, hbm, vm],
        out_specs=[hbm, hbm, hbm, hbm],
        scratch_shapes=[pltpu.VMEM((half, IN_SHARD), BF16), pltpu.VMEM((half, IN_SHARD), BF16),
                        pltpu.VMEM((half, PAIR), BF16),
                        pltpu.VMEM((2, prep_rows, D), F32), pltpu.VMEM((2, prep_rows, D), BF16),
                        pltpu.VMEM((2, D, prep_rows), BF16),
                        pltpu.VMEM((D, PAIR), BF16), pltpu.VMEM((2, prep_rows, PAIR), F32),
                        pltpu.SemaphoreType.DMA((NDEV - 1,)), pltpu.SemaphoreType.DMA((NDEV - 1,)),
                        pltpu.SemaphoreType.DMA, pltpu.SemaphoreType.DMA((2,)),
                        pltpu.SemaphoreType.DMA((2,)), pltpu.SemaphoreType.DMA((2, 3)), pltpu.SemaphoreType.DMA],
        name="gather_in_projection", compiler_params=_cp(56),
    )(w_in_s, x2d, norm_w)


def _gather_square(shards, after):
    n = len(shards)

    def launch(*refs):
        ins, outs = refs[:n], refs[n + 1:2 * n + 1]
        send_sems, recv_sems, loc_sems = refs[2 * n + 1:]
        (x, y, c), _, _ = _place()
        me = _dev_index(x, y, c)
        peers = [(1 - x if r & 4 else x, 1 - y if r & 2 else y, 1 - c if r & 1 else c) for r in range(1, NDEV)]
        barrier = pltpu.get_barrier_semaphore()
        for peer in peers:
            pl.semaphore_signal(barrier, inc=1, device_id=peer, device_id_type=MESH)
        pl.semaphore_wait(barrier, NDEV - 1)
        local = [pltpu.make_async_copy(ins[k], outs[k].at[me], loc_sems.at[k]) for k in range(n)]
        copies = [pltpu.make_async_remote_copy(
            src_ref=ins[k], dst_ref=outs[k].at[me], send_sem=send_sems.at[r, k], recv_sem=recv_sems.at[r, k],
            device_id=peer, device_id_type=MESH) for r, peer in enumerate(peers) for k in range(n)]
        for cp in local + copies:
            cp.start()
        for r, peer in enumerate(peers):
            for k in range(n):
                pltpu.make_async_remote_copy(
                    src_ref=ins[k], dst_ref=outs[k].at[_dev_index(*peer)], send_sem=send_sems.at[r, k],
                    recv_sem=recv_sems.at[r, k], device_id=peer, device_id_type=MESH).wait_recv()
        for cp in copies:
            cp.wait_send()
        for cp in local:
            cp.wait()

    return pl.kernel(
        launch, out_type=[jax.ShapeDtypeStruct((NDEV,) + a.shape, a.dtype) for a in shards],
        mesh=plsc.ScalarSubcoreMesh(axis_name="sequencer", num_cores=1), name="gather_square",
        scratch_types=(pltpu.SemaphoreType.DMA((NDEV - 1, n)), pltpu.SemaphoreType.DMA((NDEV - 1, n)),
                       pltpu.SemaphoreType.DMA((n,))),
        compiler_params=pltpu.CompilerParams(collective_id=2),
    )(*shards, after)


def _exchange_pair(arrs):
    n = len(arrs)

    def launch(*refs):
        ins, got = refs[:n], refs[n:2 * n]
        send_sems, recv_sems = refs[2 * n:]
        (x, y, c), sibling, _ = _place()
        barrier = pltpu.get_barrier_semaphore()
        pl.semaphore_signal(barrier, inc=1, device_id=sibling, device_id_type=MESH)
        pl.semaphore_wait(barrier, 1)
        sends = [pltpu.make_async_remote_copy(
            src_ref=ins[k].at[_dev_index(q // 2, q % 2, 1 - c)], dst_ref=got[k].at[q], send_sem=send_sems.at[q, k],
            recv_sem=recv_sems.at[q, k], device_id=sibling, device_id_type=MESH) for q in range(4) for k in range(n)]
        for cp in sends:
            cp.start()
        for cp in sends:
            cp.wait_recv()
        for cp in sends:
            cp.wait_send()

    return pl.kernel(
        launch, out_type=[jax.ShapeDtypeStruct((4,) + a.shape[1:], a.dtype) for a in arrs],
        mesh=plsc.ScalarSubcoreMesh(axis_name="sequencer", num_cores=1), name="exchange_pair",
        scratch_types=(pltpu.SemaphoreType.DMA((4, n)), pltpu.SemaphoreType.DMA((4, n))),
        compiler_params=pltpu.CompilerParams(collective_id=0),
    )(*arrs)


def _pair_sum(full, got, core, rows, name):
    _, nr, nc = got.shape

    def body(core_ref, a_ref, b_ref, o_ref):
        o_ref[...] = (a_ref[...].astype(F32) + b_ref[...].astype(F32)).astype(BF16)

    blk = pl.BlockSpec((1, rows, nc), lambda q, i, core_ref: (q, i, 0))
    return pl.pallas_call(
        body,
        grid_spec=pltpu.PrefetchScalarGridSpec(
            num_scalar_prefetch=1, grid=(4, nr // rows),
            in_specs=[pl.BlockSpec((1, rows, nc), lambda q, i, core_ref: (2 * q + core_ref[0], i, 0)), blk],
            out_specs=blk),
        out_shape=jax.ShapeDtypeStruct(got.shape, BF16), name=name,
    )(core, full, got)


def _exchange_chips(sums):
    n = len(sums)

    def launch(*refs):
        ins, outs = refs[:n], refs[n:2 * n]
        send_sems, recv_sems = refs[2 * n:]
        (x, y, c), _, chips = _place()
        barrier = pltpu.get_barrier_semaphore()
        for px, py in chips:
            pl.semaphore_signal(barrier, inc=1, device_id=(px, py, c), device_id_type=MESH)
        pl.semaphore_wait(barrier, len(chips))
        copies = [pltpu.make_async_remote_copy(
            src_ref=ins[k].at[2 * px + py], dst_ref=outs[k].at[j], send_sem=send_sems.at[j, k],
            recv_sem=recv_sems.at[j, k], device_id=(px, py, c), device_id_type=MESH)
            for j, (px, py) in enumerate(chips) for k in range(n)]
        for cp in copies:
            cp.start()
        for cp in copies:
            cp.wait_recv()
        for cp in copies:
            cp.wait_send()

    return pl.kernel(
        launch, out_type=[jax.ShapeDtypeStruct((3,) + a.shape[1:], a.dtype) for a in sums],
        mesh=plsc.ScalarSubcoreMesh(axis_name="sequencer", num_cores=1), name="exchange_chips",
        scratch_types=(pltpu.SemaphoreType.DMA((3, n)), pltpu.SemaphoreType.DMA((3, n))),
        compiler_params=pltpu.CompilerParams(collective_id=1),
    )(*sums)


def _exchange_square(partials):
    n = len(partials)

    def launch(*refs):
        ins, outs = refs[:n], refs[n:2 * n]
        send_sems, recv_sems, loc_sems = refs[2 * n:]
        (x, y, c), _, _ = _place()
        me = _dev_index(x, y, c)
        peers = [(1 - x if r & 4 else x, 1 - y if r & 2 else y, 1 - c if r & 1 else c) for r in range(1, NDEV)]
        barrier = pltpu.get_barrier_semaphore()
        for peer in peers:
            pl.semaphore_signal(barrier, inc=1, device_id=peer, device_id_type=MESH)
        pl.semaphore_wait(barrier, NDEV - 1)
        local = [pltpu.make_async_copy(ins[k].at[me], outs[k].at[me], loc_sems.at[k]) for k in range(n)]
        copies = [pltpu.make_async_remote_copy(
            src_ref=ins[k].at[_dev_index(*peer)], dst_ref=outs[k].at[me], send_sem=send_sems.at[r, k],
            recv_sem=recv_sems.at[r, k], device_id=peer, device_id_type=MESH)
            for r, peer in enumerate(peers) for k in range(n)]
        for cp in local + copies:
            cp.start()
        for r, peer in enumerate(peers):
            for k in range(n):
                pltpu.make_async_remote_copy(
                    src_ref=ins[k].at[me], dst_ref=outs[k].at[_dev_index(*peer)], send_sem=send_sems.at[r, k],
                    recv_sem=recv_sems.at[r, k], device_id=peer, device_id_type=MESH).wait_recv()
        for cp in copies:
            cp.wait_send()
        for cp in local:
            cp.wait()

    return pl.kernel(
        launch, out_type=[jax.ShapeDtypeStruct(a.shape, a.dtype) for a in partials],
        mesh=plsc.ScalarSubcoreMesh(axis_name="sequencer", num_cores=1), name="exchange_square",
        scratch_types=(pltpu.SemaphoreType.DMA((NDEV - 1, n)), pltpu.SemaphoreType.DMA((NDEV - 1, n)),
                       pltpu.SemaphoreType.DMA((n,))),
        compiler_params=pltpu.CompilerParams(collective_id=3),
    )(*partials)


def _exchange_small(small):
    def body(sm_ref, out_ref, send_sems, recv_sems):
        (x, y, c), _, _ = _place()
        me = _dev_index(x, y, c)
        peers = [(1 - x if r & 4 else x, 1 - y if r & 2 else y, 1 - c if r & 1 else c) for r in range(1, NDEV)]
        out_ref[me] = sm_ref[...]
        copies = [pltpu.make_async_remote_copy(
            src_ref=sm_ref, dst_ref=out_ref.at[me], send_sem=send_sems.at[r], recv_sem=recv_sems.at[r],
            device_id=peer, device_id_type=MESH) for r, peer in enumerate(peers)]
        for cp in copies:
            cp.start()
        for r, peer in enumerate(peers):
            pltpu.make_async_remote_copy(
                src_ref=sm_ref, dst_ref=out_ref.at[_dev_index(*peer)], send_sem=send_sems.at[r], recv_sem=recv_sems.at[r],
                device_id=peer, device_id_type=MESH).wait_recv()
        for cp in copies:
            cp.wait_send()

    vm = pl.BlockSpec(memory_space=pltpu.VMEM)
    return pl.pallas_call(
        body, out_shape=jax.ShapeDtypeStruct((NDEV,) + small.shape, F32), in_specs=[vm], out_specs=vm,
        scratch_shapes=[pltpu.SemaphoreType.DMA((NDEV - 1,)), pltpu.SemaphoreType.DMA((NDEV - 1,))],
        name="exchange_small",
    )(small)


def _inproj_fwd(xn, w_pairs, proj, chip, tb):
    t = xn.shape[0]
    nblk, _, nb = w_pairs.shape

    def body(chip_ref, xn_ref, w_ref, proj_in, proj_ref):
        proj_ref[...] = _mm(xn_ref[...], w_ref[0])

    def other(j, chip_ref):
        return j + (j >= chip_ref[0]).astype(jnp.int32)

    return pl.pallas_call(
        body,
        grid_spec=pltpu.PrefetchScalarGridSpec(
            num_scalar_prefetch=1, grid=(t // tb, nblk - 1),
            in_specs=[pl.BlockSpec((tb, D), lambda i, j, chip_ref: (i, 0)),
                      pl.BlockSpec((1, D, nb), lambda i, j, chip_ref: (other(j, chip_ref), 0, 0)),
                      pl.BlockSpec(memory_space=pl.ANY)],
            out_specs=pl.BlockSpec((tb, nb), lambda i, j, chip_ref: (i, other(j, chip_ref)))),
        out_shape=jax.ShapeDtypeStruct((t, DIN), F32),
        input_output_aliases={3: 0},
        name="inproj_fwd", compiler_params=_cp(56),
    )(chip, xn, w_pairs, proj)


def _lower_bound(lb_ref):
    a0 = lb_ref[0:1, :]
    a1 = lb_ref[1:2, :]
    mx = jnp.maximum(a0, a1)
    e0 = jnp.exp(a0 - mx)
    e1 = jnp.exp(a1 - mx)
    return e0 / (e0 + e1)


def _hgrn_chunk_fwd(hq, hf, lb, tril):
    sg = _sigmoid(hf)
    f = lb + (1.0 - lb) * sg
    g = jnp.log(f)
    k = 1.0 - f
    sq = _sigmoid(hq)
    q = hq * sq
    b = _mm_tri_exact(tril, g)
    last_row = lax.broadcasted_iota(jnp.int32, b.shape, 0) == CH - 1
    b_last = jnp.sum(jnp.where(last_row, b, 0.0), axis=0, keepdims=True)
    c = 0.5 * b_last
    eb = jnp.exp(b)
    ea = jnp.exp(b - c)
    ek = jnp.exp(c - b)
    ed = jnp.exp(b_last - b)
    ebl = jnp.exp(b_last)
    return dict(sg=sg, f=f, k=k, sq=sq, q=q, eb=eb, ea=ea, ek=ek, ed=ed, ebl=ebl,
                qe=q * eb, qa=q * ea, ka=k * ek, kd=k * ed)


def _tri(lower):
    r = lax.broadcasted_iota(jnp.int32, (CH, CH), 0)
    c = lax.broadcasted_iota(jnp.int32, (CH, CH), 1)
    return (r >= c) if lower else (c >= r)


def _head_segment(p_ref, rows, j, hg):
    return p_ref[rows, j * HD * hg:(j + 1) * HD * hg]


def _head(a, k):
    return a[:, k * HD:(k + 1) * HD]


def _hgrn_fwd(proj, lbw, rb, hg):
    assert hg == HEADS
    t = proj.shape[0]
    ncb = rb // CH

    def body(p_ref, lb_ref, o_ref, st_ref, s_scr):
        @pl.when(pl.program_id(1) == 0)
        def _():
            s_scr[...] = jnp.zeros_like(s_scr)

        lb = _lower_bound(lb_ref)
        causal = _tri(True)
        tril = _tri3(True)
        heads = range(hg)

        def chunk(cc, carry):
            r0 = pl.multiple_of(cc * CH, CH)
            rows = pl.ds(r0, CH)
            e = _hgrn_chunk_fwd(_head_segment(p_ref, rows, 0, hg), _head_segment(p_ref, rows, 1, hg), lb, tril)
            v = _bf(_head_segment(p_ref, rows, 2, hg))
            sts = [s_scr[k] for k in heads]
            qa, ka, qe, kd = _bf(e["qa"]), _bf(e["ka"]), _bf(e["qe"]), _bf(e["kd"])
            a = [_bf(jnp.where(causal, _mm_nt(_head(qa, k), _head(ka, k)), 0.0)) for k in heads]
            o_inter = [_mm_nt(_head(qe, k), _bf(sts[k])) for k in heads]
            kv = [_mm_tn(_head(v, k), _head(kd, k)) for k in heads]
            o_intra = [_mm(a[k], _head(v, k)) for k in heads]
            for k in heads:
                st_ref[cc, k] = sts[k]
                o_ref[rows, k * HD:(k + 1) * HD] = o_inter[k] + o_intra[k]
                s_scr[k] = sts[k] * _head(e["ebl"], k) + kv[k]
            return carry

        lax.fori_loop(0, ncb, chunk, 0, unroll=4)

    return pl.pallas_call(
        body, grid=(HEADS // hg, t // rb),
        in_specs=[pl.BlockSpec((rb, 3 * HD * hg), lambda h, i: (i, h)), pl.BlockSpec((2, HD * hg), lambda h, i: (0, h))],
        out_specs=[pl.BlockSpec((rb, HD * hg), lambda h, i: (i, h)),
                   pl.BlockSpec((ncb, hg, HD, HD), lambda h, i: (i, h, 0, 0))],
        out_shape=[jax.ShapeDtypeStruct((t, D), F32), jax.ShapeDtypeStruct((t // CH, HEADS, HD, HD), F32)],
        scratch_shapes=[pltpu.VMEM((hg, HD, HD), F32)],
        name="hgrn_fwd", compiler_params=_cp(48),
    )(proj, lbw)


def _hgrn_bwd(proj, lbw, do_raw, states, rb, hg):
    assert hg == HEADS
    t = proj.shape[0]
    nblk = t // rb
    ncb = rb // CH
    wd = HD * hg

    def body(p_ref, lb_ref, do_ref, st_ref, dp_ref, dlb_ref, ds_scr):
        @pl.when(pl.program_id(1) == 0)
        def _():
            ds_scr[...] = jnp.zeros_like(ds_scr)
            dlb_ref[...] = jnp.zeros_like(dlb_ref)

        lb = _lower_bound(lb_ref)
        causal = _tri(True)
        tril = _tri3(True)
        triu = _tri3(False)
        last_row = lax.broadcasted_iota(jnp.int32, (CH, HD * hg), 0) == CH - 1
        row0 = lax.broadcasted_iota(jnp.int32, (8, HD * hg), 0) == 0
        heads = range(hg)
        wide = lambda parts: jnp.concatenate(parts, axis=1)

        def chunk(it, carry):
            cc = ncb - 1 - it
            r0 = pl.multiple_of(cc * CH, CH)
            rows = pl.ds(r0, CH)
            hq = _head_segment(p_ref, rows, 0, hg)
            e = _hgrn_chunk_fwd(hq, _head_segment(p_ref, rows, 1, hg), lb, tril)
            v = _bf(_head_segment(p_ref, rows, 2, hg))
            do = _bf(do_ref[rows, :])
            sts = [st_ref[cc, k] for k in heads]
            dsts = [ds_scr[k] for k in heads]
            dlb_acc = dlb_ref[...]
            qa, ka, qe, kd = _bf(e["qa"]), _bf(e["ka"]), _bf(e["qe"]), _bf(e["kd"])
            a = [_bf(jnp.where(causal, _mm_nt(_head(qa, k), _head(ka, k)), 0.0)) for k in heads]
            da = [_bf(jnp.where(causal, _mm_nt(_head(do, k), _head(v, k)), 0.0)) for k in heads]
            dqe = wide([_mm(_head(do, k), _bf(sts[k])) for k in heads])
            dkd = wide([_mm(_head(v, k), _bf(dsts[k])) for k in heads])
            dv_state = [_mm_nt(_head(kd, k), _bf(dsts[k])) for k in heads]
            ds_new = [_mm_tn(_head(do, k), _head(qe, k)) for k in heads]
            dv_intra = [_mm_tn(a[k], _head(do, k)) for k in heads]
            dqa = wide([_mm(da[k], _head(ka, k)) for k in heads])
            dka = wide([_mm_tn(da[k], _head(qa, k)) for k in heads])
            dv = wide([dv_intra[k] + dv_state[k] for k in heads])
            dbl = e["ebl"] * wide([jnp.sum(sts[k] * dsts[k], axis=0, keepdims=True) for k in heads])
            dq = dqe * e["eb"] + dqa * e["ea"]
            dk = dka * e["ek"] + dkd * e["ed"]
            dkd_kd = dkd * kd.astype(F32)
            db = dqe * qe.astype(F32) + dqa * qa.astype(F32) - dka * ka.astype(F32) - dkd_kd
            db = db + jnp.where(last_row, dbl + jnp.sum(dkd_kd, axis=0, keepdims=True), 0.0)
            dg = _mm_tri_exact(triu, db)
            df = dg / e["f"] - dk
            sg = e["sg"]
            sq = e["sq"]
            dhq = _bf(dq * (sq * (1.0 + hq * (1.0 - sq))))
            dhf = _bf(df * (1.0 - lb) * sg * (1.0 - sg))
            dhi = _bf(dv)
            dlb_new = dlb_acc + jnp.where(row0, jnp.sum(df * (1.0 - sg), axis=0, keepdims=True), 0.0)
            for k in heads:
                ds_scr[k] = ds_new[k] + dsts[k] * _head(e["ebl"], k)
            dp_ref[rows, 0:wd] = dhq
            dp_ref[rows, wd:2 * wd] = dhf
            dp_ref[rows, 2 * wd:3 * wd] = dhi
            dlb_ref[...] = dlb_new
            return carry

        lax.fori_loop(0, ncb, chunk, 0, unroll=2)

    rev = lambda h, i: (nblk - 1 - i, h)
    return pl.pallas_call(
        body, grid=(HEADS // hg, nblk),
        in_specs=[pl.BlockSpec((rb, 3 * HD * hg), rev), pl.BlockSpec((2, HD * hg), lambda h, i: (0, h)),
                  pl.BlockSpec((rb, HD * hg), rev), pl.BlockSpec((ncb, hg, HD, HD), lambda h, i: (nblk - 1 - i, h, 0, 0))],
        out_specs=[pl.BlockSpec((rb, 3 * HD * hg), rev), pl.BlockSpec((8, HD * hg), lambda h, i: (0, h))],
        out_shape=[jax.ShapeDtypeStruct((t, 3 * D), BF16), jax.ShapeDtypeStruct((8, D), F32)],
        scratch_shapes=[pltpu.VMEM((hg, HD, HD), F32)],
        name="hgrn_bwd", compiler_params=_cp(48),
    )(proj, lbw, do_raw, states)


def _kv_variants(tile, odd):
    low = lax.broadcasted_iota(jnp.int32, tile.shape, 1) < 64
    if odd:
        hi = jnp.where(low, 0.0, tile)
        lo = pltpu.roll(hi, 64, 1)
    else:
        lo = jnp.where(low, tile, 0.0)
        hi = pltpu.roll(lo, 64, 1)
    return _bf(lo), _bf(hi)


def _attn_masks(n):
    qi = lax.broadcasted_iota(jnp.int32, (AB, AB), 0)
    kj = lax.broadcasted_iota(jnp.int32, (AB, AB), 1)
    cur = kj <= qi
    return cur, cur | (n > 0), qi <= kj


def _kv_all(prev_ref, cur_ref):
    out = []
    for tl in range(2):
        cols = slice(tl * 128, (tl + 1) * 128)
        tile = jnp.concatenate([prev_ref[:, cols], cur_ref[:, cols]], axis=0)
        out.append(_kv_variants(tile, 0))
        out.append(_kv_variants(tile, 1))
    return out


def _window(a2, cur):
    return jnp.where(cur, a2[:, AB:], a2[:, :AB])


def _attn_softmax(scores, sinks, cur, ok):
    s = [jnp.where(ok, _window(s2, cur) * ATT_SCALE, NEG) for s2 in scores]
    m = [jnp.maximum(jnp.max(si, axis=-1, keepdims=True), sink) for si, sink in zip(s, sinks)]
    p = [jnp.exp(si - mi) for si, mi in zip(s, m)]
    es = [jnp.exp(sink - mi) for sink, mi in zip(sinks, m)]
    inv = [1.0 / (jnp.sum(pi, axis=-1, keepdims=True) + ei) for pi, ei in zip(p, es)]
    return [pi * ii for pi, ii in zip(p, inv)], [ei * ii for ei, ii in zip(es, inv)]


def _spread(pc, cur):
    return jnp.concatenate([jnp.where(cur, 0.0, pc), jnp.where(cur, pc, 0.0)], axis=1)


def _spread_t(pct, cur_t):
    return jnp.concatenate([jnp.where(cur_t, 0.0, pct), jnp.where(cur_t, pct, 0.0)], axis=0)


def _attn_fwd(proj, sinks):
    t = proj.shape[0]
    nb = t // AB

    nsub = 2
    assert nb % nsub == 0

    def body(q_ref, kc_ref, kp_ref, vc_ref, vp_ref, sink_ref, o_ref):
        sinks_v = [sink_ref[0, h] for h in range(QH)]
        heads = [(j, ab) for j in range(8) for ab in range(2)]
        for sb in range(nsub):
            rows = pl.ds(AB * sb, AB)
            before = pl.ds(AB * (sb - 1), AB)
            cur, ok, _ = _attn_masks(nsub * pl.program_id(0) + sb)
            kvars = _kv_all(kp_ref if sb == 0 else kc_ref.at[before, :], kc_ref.at[rows, :])
            vvars = _kv_all(vp_ref if sb == 0 else vc_ref.at[before, :], vc_ref.at[rows, :])
            qps = [_bf(q_ref[rows, 128 * j:128 * (j + 1)]) for j in range(8)]
            scores = [_mm_nt(qps[j], kvars[j // 2][ab]) for j, ab in heads]
            pcs, _ = _attn_softmax(scores, sinks_v, cur, ok)
            parts = [_mm(_bf(_spread(pcs[h], cur)), vvars[j // 2][ab]) for h, (j, ab) in enumerate(heads)]
            for j in range(8):
                o_ref[rows, 128 * j:128 * (j + 1)] = parts[2 * j] + parts[2 * j + 1]

    prev = lambda n: jnp.maximum(nsub * n - 1, 0)
    step = nsub * AB
    return pl.pallas_call(
        body, grid=(nb // nsub,),
        in_specs=[pl.BlockSpec((step, D), lambda n: (n, C_AQ // D)),
                  pl.BlockSpec((step, 256), lambda n: (n, C_AK // 256)),
                  pl.BlockSpec((AB, 256), lambda n: (prev(n), C_AK // 256)),
                  pl.BlockSpec((step, 256), lambda n: (n, C_AV // 256)),
                  pl.BlockSpec((AB, 256), lambda n: (prev(n), C_AV // 256)),
                  pl.BlockSpec(memory_space=pltpu.SMEM)],
        out_specs=pl.BlockSpec((step, D), lambda n: (n, 0)),
        out_shape=jax.ShapeDtypeStruct((t, D), F32),
        name="attn_fwd", compiler_params=_cp(32),
    )(proj, proj, proj, proj, proj, sinks)


def _attn_bwd(proj, sinks, do_a):
    t = proj.shape[0]
    nb = t // AB
    nsub = 2
    assert nb % nsub == 0
    steps = nb // nsub
    step = nsub * AB

    def body(q_ref, kc_ref, kp_ref, vc_ref, vp_ref, do_ref, sink_ref, dq_ref, dkv_ref, dsink_ref, carry):
        n = pl.program_id(0)

        @pl.when(n == 0)
        def _():
            dsink_ref[...] = jnp.zeros_like(dsink_ref)
            carry[...] = jnp.zeros_like(carry)

        def one_block(sb):
            rows = pl.ds(AB * sb, AB)
            before = pl.ds(AB * (sb - 1), AB)
            cur, ok, cur_t = _attn_masks(nsub * n + sb)
            low = lax.broadcasted_iota(jnp.int32, (2 * AB, 128), 1) < 64
            lane = lax.broadcasted_iota(jnp.int32, (8, 128), 1)
            row0 = lax.broadcasted_iota(jnp.int32, (8, 128), 0) == 0
            kvars = _kv_all(kp_ref if sb == 0 else kc_ref.at[before, :], kc_ref.at[rows, :])
            vvars = _kv_all(vp_ref if sb == 0 else vc_ref.at[before, :], vc_ref.at[rows, :])
            qps = [_bf(q_ref[rows, 128 * j:128 * (j + 1)]) for j in range(8)]
            dops = [_bf(do_ref[rows, 128 * j:128 * (j + 1)]) for j in range(8)]
            heads = [(j, ab) for j in range(8) for ab in range(2)]
            scores = [_mm_nt(qps[j], kvars[j // 2][ab]) for j, ab in heads]
            dps = [_mm_nt(dops[j], vvars[j // 2][ab]) for j, ab in heads]
            pcs, pss = _attn_softmax(scores, [sink_ref[0, h] for h in range(QH)], cur, ok)
            dpcs = [_window(dp2, cur) for dp2 in dps]
            rss = [jnp.sum(pc * dpc, axis=-1, keepdims=True) for pc, dpc in zip(pcs, dpcs)]
            dscs = [pc * (dpc - rs) for pc, dpc, rs in zip(pcs, dpcs, rss)]
            dsink = jnp.zeros((8, 128), F32)
            for h in range(QH):
                dsink = dsink + jnp.where(row0 & (lane == h), -jnp.sum(pss[h] * rss[h]), 0.0)
            dq_terms = [_mm(_bf(_spread(dscs[h], cur)), kvars[j // 2][ab]) for h, (j, ab) in enumerate(heads)]
            for j in range(8):
                dq_ref[rows, 128 * j:128 * (j + 1)] = _bf((dq_terms[2 * j] + dq_terms[2 * j + 1]) * ATT_SCALE)
            dsc_t = [_bf(_spread_t(dsc.T, cur_t)) for dsc in dscs]
            pc_t = [_bf(_spread_t(pc.T, cur_t)) for pc in pcs]
            dk_terms = [_mm(dsc_t[h], qps[j]) for h, (j, ab) in enumerate(heads)]
            dv_terms = [_mm(pc_t[h], dops[j]) for h, (j, ab) in enumerate(heads)]
            dk_ab = [[dk_terms[4 * g + ab] + dk_terms[4 * g + 2 + ab] for ab in range(2)] for g in range(4)]
            dv_ab = [[dv_terms[4 * g + ab] + dv_terms[4 * g + 2 + ab] for ab in range(2)] for g in range(4)]
            dkts, dvts = [], []
            for tl in range(2):
                ke, ko = dk_ab[2 * tl], dk_ab[2 * tl + 1]
                ve, vo = dv_ab[2 * tl], dv_ab[2 * tl + 1]
                dkts.append((jnp.where(low, ke[0], 0.0) + pltpu.roll(jnp.where(low, 0.0, ke[1]), 64, 1)
                             + jnp.where(low, 0.0, ko[1]) + pltpu.roll(jnp.where(low, ko[0], 0.0), 64, 1)) * ATT_SCALE)
                dvts.append(jnp.where(low, ve[0], 0.0) + pltpu.roll(jnp.where(low, 0.0, ve[1]), 64, 1)
                            + jnp.where(low, 0.0, vo[1]) + pltpu.roll(jnp.where(low, vo[0], 0.0), 64, 1))
            return dkts, dvts, dsink

        @pl.when(n < steps)
        def _():
            (dk0, dv0, ds0), (dk1, dv1, ds1) = one_block(0), one_block(1)
            first, second = slice(0, AB), slice(AB, 2 * AB)
            for tl in range(2):
                for cols, g0, g1 in ((slice(tl * 128, (tl + 1) * 128), dk0[tl], dk1[tl]),
                                     (slice(256 + tl * 128, 256 + (tl + 1) * 128), dv0[tl], dv1[tl])):
                    dkv_ref[first, cols] = _bf(carry[first, cols])
                    dkv_ref[second, cols] = _bf(carry[second, cols] + g0[first])
                    carry[first, cols] = g0[second] + g1[first]
                    carry[second, cols] = g1[second]
            dsink_ref[...] += ds0 + ds1

        @pl.when(n == steps)
        def _():
            dkv_ref[...] = _bf(carry[...])

    cur = lambda n: jnp.minimum(n, steps - 1)
    last = lambda n: jnp.clip(n - 1, 0, steps - 1)
    prev = lambda n: jnp.clip(nsub * n - 1, 0, nb - 1)
    return pl.pallas_call(
        body, grid=(steps + 1,),
        in_specs=[pl.BlockSpec((step, D), lambda n: (cur(n), C_AQ // D)),
                  pl.BlockSpec((step, 256), lambda n: (cur(n), C_AK // 256)),
                  pl.BlockSpec((AB, 256), lambda n: (prev(n), C_AK // 256)),
                  pl.BlockSpec((step, 256), lambda n: (cur(n), C_AV // 256)),
                  pl.BlockSpec((AB, 256), lambda n: (prev(n), C_AV // 256)),
                  pl.BlockSpec((step, D), lambda n: (cur(n), 0)),
                  pl.BlockSpec(memory_space=pltpu.SMEM)],
        out_specs=[pl.BlockSpec((step, D), lambda n: (cur(n), 0)),
                   pl.BlockSpec((step, 512), lambda n: (last(n), 0)),
                   pl.BlockSpec((8, 128), lambda n: (0, 0))],
        out_shape=[jax.ShapeDtypeStruct((t, D), BF16), jax.ShapeDtypeStruct((t, 512), BF16),
                   jax.ShapeDtypeStruct((8, 128), F32)],
        scratch_shapes=[pltpu.VMEM((step, 512), F32)],
        name="attn_bwd", compiler_params=_cp(40),
    )(proj, proj, proj, proj, proj, do_a, sinks)


def _silu_and_grad(v):
    s = _sigmoid(v)
    return v * s, s * (1.0 + v * (1.0 - s))


def _tail(o_raw, o_a, proj, x2d, tgt, wbh, wba, wout, hnw, fnw, tb):
    t = x2d.shape[0]

    def body(or_ref, oa_ref, hg_ref, ag0, ag1, mh0, mh1, ma0, ma1, x_ref, t_ref, wbh_ref, wba_ref, wout_ref, hnw_ref,
             fnw_ref, dx2_ref, dor_ref, doa_ref, dhg_ref, dagm_ref, gh_ref, ga_ref, mg_ref, dyh_ref, dya_ref, dx2b_ref,
             sums_ref):
        @pl.when(pl.program_id(0) == 0)
        def _():
            sums_ref[...] = jnp.zeros_like(sums_ref)

        halves = lambda a, b: jnp.concatenate([a[...], b[...]], axis=1)
        hnw_v = hnw_ref[...]
        fnw_v = fnw_ref[...]
        o = or_ref[...]
        rs, xhs = [], []
        for h in range(HEADS):
            oh = o[:, h * HD:(h + 1) * HD]
            r = lax.rsqrt(jnp.mean(oh * oh, axis=-1, keepdims=True) + EPS)
            rs.append(r)
            xhs.append(oh * r)
        xh = jnp.concatenate(xhs, axis=1)
        on = xh * hnw_v
        sil_hg, dsil_hg = _silu_and_grad(hg_ref[...])
        gh_b = _bf(on * sil_hg)
        y_h = _mm(gh_b, wbh_ref[...])
        oa = oa_ref[...]
        sil_ag, dsil_ag = _silu_and_grad(halves(ag0, ag1))
        ga_b = _bf(oa * sil_ag)
        y_a = _mm(ga_b, wba_ref[...])
        s_mh = _sigmoid(halves(mh0, mh1))
        s_ma = _sigmoid(halves(ma0, ma1))
        mg_b = _bf(s_mh * y_h + s_ma * y_a)
        x2 = x_ref[...] + _mm(mg_b, wout_ref[...])
        r2 = lax.rsqrt(jnp.mean(x2 * x2, axis=-1, keepdims=True) + EPS)
        xh2 = x2 * r2
        err = xh2 * fnw_v - t_ref[...]
        loss = 0.5 * jnp.sum(jnp.mean(err * err, axis=-1, keepdims=True))
        dy = err * (1.0 / D)
        dfnw = jnp.sum(dy * xh2, axis=0, keepdims=True)
        dxh2 = dy * fnw_v
        dx2 = r2 * (dxh2 - xh2 * jnp.mean(dxh2 * xh2, axis=-1, keepdims=True))
        dx2_ref[...] = dx2
        dx2_b = _bf(dx2)
        dmg = _mm_nt(dx2_b, wout_ref[...])
        dmg_h = dmg * s_mh
        dmg_a = dmg * s_ma
        dyh_b = _bf(dmg_h)
        dya_b = _bf(dmg_a)
        dagm_ref[:, D:2 * D] = _bf(dmg_h * y_h * (1.0 - s_mh))
        dagm_ref[:, 2 * D:3 * D] = _bf(dmg_a * y_a * (1.0 - s_ma))
        dgh = _mm_nt(dyh_b, wbh_ref[...])
        dga = _mm_nt(dya_b, wba_ref[...])
        doa_ref[...] = dga * sil_ag
        dagm_ref[:, 0:D] = _bf(dga * oa * dsil_ag)
        dhg_ref[...] = _bf(dgh * on * dsil_hg)
        don = dgh * sil_hg
        dhnw = jnp.sum(don * xh, axis=0, keepdims=True)
        dxh = don * hnw_v
        dos = []
        for h in range(HEADS):
            sl = slice(h * HD, (h + 1) * HD)
            dos.append(rs[h] * (dxh[:, sl] - xhs[h] * jnp.mean(dxh[:, sl] * xhs[h], axis=-1, keepdims=True)))
        dor_ref[...] = jnp.concatenate(dos, axis=1)
        gh_ref[...] = gh_b
        ga_ref[...] = ga_b
        mg_ref[...] = mg_b
        dyh_ref[...] = dyh_b
        dya_ref[...] = dya_b
        dx2b_ref[...] = dx2_b
        row = lax.broadcasted_iota(jnp.int32, (8, D), 0)
        sums_ref[...] += jnp.where(row == 0, dfnw, 0.0) + jnp.where(row == 1, dhnw, 0.0) + jnp.where(row == 2, loss, 0.0)

    rowblk = lambda c: pl.BlockSpec((tb, D), lambda i: (i, c))
    half = lambda c: pl.BlockSpec((tb, 512), lambda i: (i, c))
    full = lambda shape: pl.BlockSpec(shape, lambda i: (0, 0))
    return pl.pallas_call(
        body, grid=(t // tb,),
        in_specs=[rowblk(0), rowblk(0), rowblk(C_HG // D), half(C_AG // 512), half(C_AG // 512 + 1), half(C_MH // 512),
                  half(C_MH // 512 + 1), half(C_MA // 512), half(C_MA // 512 + 1), rowblk(0), rowblk(0),
                  full((D, D)), full((D, D)), full((D, D)), full((1, D)), full((1, D))],
        out_specs=[rowblk(0), rowblk(0), rowblk(0), rowblk(0), pl.BlockSpec((tb, 3 * D), lambda i: (i, 0))]
        + [rowblk(0)] * 6 + [full((8, D))],
        out_shape=[jax.ShapeDtypeStruct((t, D), F32)] * 3
        + [jax.ShapeDtypeStruct((t, D), BF16), jax.ShapeDtypeStruct((t, 3 * D), BF16)]
        + [jax.ShapeDtypeStruct((t, D), BF16)] * 6 + [jax.ShapeDtypeStruct((8, D), F32)],
        name="tail", compiler_params=_cp(56),
    )(o_raw, o_a, proj, proj, proj, proj, proj, proj, proj, x2d, tgt, wbh, wba, wout, hnw, fnw)


def _wgrad3(gh, dyh, ga, dya, mg, dx2b, tk):
    t = dyh.shape[0]

    def body(a0, b0, a1, b1, a2, b2, o0, o1, o2):
        @pl.when(pl.program_id(0) == 0)
        def _():
            o0[...] = jnp.zeros_like(o0)
            o1[...] = jnp.zeros_like(o1)
            o2[...] = jnp.zeros_like(o2)

        o0[...] += _mm_tn(a0[...], b0[...])
        o1[...] += _mm_tn(a1[...], b1[...])
        o2[...] += _mm_tn(a2[...], b2[...])

    blk = pl.BlockSpec((tk, D), lambda k: (k, 0))
    out = pl.BlockSpec((D, D), lambda k: (0, 0))
    return pl.pallas_call(
        body, grid=(t // tk,), in_specs=[blk] * 6, out_specs=[out] * 3,
        out_shape=[jax.ShapeDtypeStruct((D, D), F32)] * 3,
        name="wgrad3", compiler_params=_cp(48),
    )(gh, dyh, ga, dya, mg, dx2b)


def _inproj_wgrad_piece(xnt, piece, nb, name):
    t = xnt.shape[1]
    width = piece.shape[1]

    def body(xnt_ref, p_ref, o_ref):
        o_ref[...] = _mm(xnt_ref[...], p_ref[...])

    return pl.pallas_call(
        body, grid=(width // nb,),
        in_specs=[pl.BlockSpec((D, t), lambda j: (0, 0), pipeline_mode=pl.Buffered(1)),
                  pl.BlockSpec((t, nb), lambda j: (0, j))],
        out_specs=pl.BlockSpec((D, nb), lambda j: (0, j)),
        out_shape=jax.ShapeDtypeStruct((D, width), F32),
        name=name, compiler_params=_cp(56),
    )(xnt, piece)


def _inproj_dgrad(pieces, w_p, x2d, dx2, norm_w, tb, after):
    t = x2d.shape[0]

    def body(*refs):
        piece_refs = refs[:len(pieces)]
        w_ref, x_ref, dx2_ref, nw_ref, _, gx_ref, dnw_ref = refs[len(pieces):]

        @pl.when(pl.program_id(0) == 0)
        def _():
            dnw_ref[...] = jnp.zeros_like(dnw_ref)

        dxn = None
        off = 0
        for p in piece_refs:
            width = p.shape[1]
            for q in range(w_ref.shape[0]):
                lo, hi = max(off, q * PAIR), min(off + width, (q + 1) * PAIR)
                if lo < hi:
                    term = _mm_nt(p[:, lo - off:hi - off], w_ref[q, :, lo - q * PAIR:hi - q * PAIR])
                    dxn = term if dxn is None else dxn + term
            off += width
        xv = x_ref[...]
        r = lax.rsqrt(jnp.mean(xv * xv, axis=-1, keepdims=True) + EPS)
        xh = xv * r
        dxh = dxn * nw_ref[...]
        gx_ref[...] = dx2_ref[...] + r * (dxh - xh * jnp.mean(dxh * xh, axis=-1, keepdims=True))
        row0 = lax.broadcasted_iota(jnp.int32, (8, D), 0) == 0
        dnw_ref[...] += jnp.where(row0, jnp.sum(dxn * xh, axis=0, keepdims=True), 0.0)

    rowblk = pl.BlockSpec((tb, D), lambda i: (i, 0))
    return pl.pallas_call(
        body, grid=(t // tb,),
        in_specs=[pl.BlockSpec((tb, p.shape[1]), lambda i: (i, 0)) for p in pieces]
        + [pl.BlockSpec(w_p.shape, lambda i: (0, 0, 0), pipeline_mode=pl.Buffered(1)), rowblk, rowblk,
           pl.BlockSpec((1, D), lambda i: (0, 0)), pl.BlockSpec(memory_space=pl.ANY)],
        out_specs=[rowblk, pl.BlockSpec((8, D), lambda i: (0, 0))],
        out_shape=[jax.ShapeDtypeStruct((t, D), F32), jax.ShapeDtypeStruct((8, D), F32)],
        name="inproj_dgrad", compiler_params=_cp(60),
    )(*pieces, w_p, x2d, dx2, norm_w, after)


def _adamw_math(w, g, m, v):
    m = B1 * m + (1.0 - B1) * g
    v = B2 * v + (1.0 - B2) * (g * g)
    m_hat = m / (1.0 - B1 ** STEP)
    v_hat = v / (1.0 - B2 ** STEP)
    delta = -LR * (m_hat / (jnp.sqrt(v_hat) + ADAM_EPS) + WD * w)
    return delta, m, v


def _adamw_shard(recv, sums, chip, w, m, v, rows, name):
    nparts, nr, nc = recv.shape

    def body(chip_ref, own_ref, p_ref, w_ref, m_ref, v_ref, g_ref, d_ref, nm_ref, nv_ref):
        g = own_ref[0].astype(F32)
        for s in range(nparts):
            g = g + p_ref[s].astype(F32)
        d, nm, nv = _adamw_math(w_ref[0], g, m_ref[0], v_ref[0])
        g_ref[0] = g
        d_ref[0] = d
        nm_ref[0] = nm
        nv_ref[0] = nv

    blk = pl.BlockSpec((1, rows, nc), lambda i, chip_ref: (0, i, 0))
    return pl.pallas_call(
        body,
        grid_spec=pltpu.PrefetchScalarGridSpec(
            num_scalar_prefetch=1, grid=(nr // rows,),
            in_specs=[pl.BlockSpec((1, rows, nc), lambda i, chip_ref: (chip_ref[0], i, 0)),
                      pl.BlockSpec((nparts, rows, nc), lambda i, chip_ref: (0, i, 0)), blk, blk, blk],
            out_specs=[blk] * 4),
        out_shape=[jax.ShapeDtypeStruct((1, nr, nc), F32)] * 4,
        name=name, compiler_params=_cp(48),
    )(chip, sums, recv, w, m, v)


def _adamw_sum8(parts, w, m, v, after, name):
    def body(p_ref, w_ref, m_ref, v_ref, _, g_ref, d_ref, nm_ref, nv_ref):
        g = p_ref[0].astype(F32)
        for s in range(1, NDEV):
            g = g + p_ref[s].astype(F32)
        d, nm, nv = _adamw_math(w_ref[0], g, m_ref[0], v_ref[0])
        g_ref[0] = g
        d_ref[0] = d
        nm_ref[0] = nm
        nv_ref[0] = nv

    vm = pl.BlockSpec(memory_space=pltpu.VMEM)
    return pl.pallas_call(
        body, out_shape=[jax.ShapeDtypeStruct(w.shape, F32)] * 4,
        in_specs=[vm, vm, vm, vm, pl.BlockSpec(memory_space=pl.ANY)], out_specs=[vm] * 4, name=name,
    )(parts, w, m, v, after)


SMALL_ROWS = dict(norm_w=0, lower_bound=1, hgrn_norm_w=3, final_norm_w=4, sinks=5, loss=6)


def _pack_small_grads(dnw, dlb, sums, dsink):
    def body(dnw_ref, dlb_ref, sums_ref, dsink_ref, o_ref):
        o_ref[...] = jnp.zeros_like(o_ref)
        o_ref[0:1, :] = dnw_ref[0:1, :]
        o_ref[1:2, :] = dlb_ref[0:1, :]
        o_ref[3:4, :] = sums_ref[1:2, :]
        o_ref[4:5, :] = sums_ref[0:1, :]
        o_ref[5:6, 0:128] = dsink_ref[0:1, :]
        o_ref[6:7, :] = sums_ref[2:3, :]

    return pl.pallas_call(body, out_shape=jax.ShapeDtypeStruct((8, D), F32), name="pack_small_grads")(dnw, dlb, sums, dsink)


def _adamw_small(parts, ws, ms, vs):
    shapes = [a.shape for a in ws]

    def body(p_ref, *refs):
        w, m, v = refs[0:5], refs[5:10], refs[10:15]
        outs = [refs[15 + 5 * i:20 + 5 * i] for i in range(4)]
        loss_ref = refs[35]

        def total(row, width):
            g = p_ref[0, row:row + 1, 0:width]
            for s in range(1, NDEV):
                g = g + p_ref[s, row:row + 1, 0:width]
            return g

        loss_ref[...] = total(6, 128)
        lb = _lower_bound(w[1])
        ga0 = total(1, D) * lb * (1.0 - lb)
        grads = [total(0, D), None, total(3, D), total(4, D), total(5, QH)]
        for i in (0, 2, 3, 4):
            res = (grads[i],) + _adamw_math(w[i][...], grads[i], m[i][...], v[i][...])
            for o, val in zip(outs, res):
                o[i][...] = val
        for r, g in ((0, ga0), (1, -ga0)):
            res = (g,) + _adamw_math(w[1][r:r + 1, :], g, m[1][r:r + 1, :], v[1][r:r + 1, :])
            for o, val in zip(outs, res):
                o[1][r:r + 1, :] = val

    res = pl.pallas_call(
        body, out_shape=[jax.ShapeDtypeStruct(s, F32) for s in shapes] * 4 + [jax.ShapeDtypeStruct((1, 128), F32)],
        name="adamw_small",
    )(parts, *ws, *ms, *vs)
    return [res[5 * i:5 * i + 5] for i in range(4)], res[20][0, 0]


def kernel(x, norm_w, w_in, hgrn_lower_bound, hgrn_norm_w, w_branch_hgrn, attn_sinks, w_branch_attn, w_out, final_norm_w, loss_target, m_norm_w, m_w_in, m_hgrn_lower_bound, m_hgrn_norm_w, m_w_branch_hgrn, m_attn_sinks, m_w_branch_attn, m_w_out, m_final_norm_w, v_norm_w, v_w_in, v_hgrn_lower_bound, v_hgrn_norm_w, v_w_branch_hgrn, v_attn_sinks, v_w_branch_attn, v_w_out, v_final_norm_w):
    t = x.shape[1]
    x2d = x.reshape(t, D)
    tgt = loss_target.reshape(t, D)
    fnw = final_norm_w.reshape(1, D)
    row_blk = min(256, t)
    big_blk = min(512, t)

    chip = (2 * lax.axis_index("x") + lax.axis_index("y")).astype(jnp.int32).reshape(1)
    w_p, xn, xnt, proj_own = _gather_in_projection(w_in[0], x2d, norm_w)
    wbh, wba, wout = (g.reshape(D, D) for g in _gather_square(
        [w_branch_hgrn[0].astype(BF16), w_branch_attn[0].astype(BF16), w_out[0].astype(BF16)], after=w_p))

    proj = _inproj_fwd(xn, w_p, proj_own, chip, min(1024, t))
    o_raw, states = _hgrn_fwd(proj, hgrn_lower_bound, big_blk, HGRN_GROUP)
    o_a = _attn_fwd(proj, attn_sinks)
    (dx2, do_raw, do_a, d_hg, d_agm, gh, ga, mg, dyh, dya, dx2b, sums) = _tail(
        o_raw, o_a, proj, x2d, tgt, wbh, wba, wout, hgrn_norm_w, fnw, row_blk)
    dwbh, dwba, dwout = _wgrad3(gh, dyh, ga, dya, mg, dx2b, big_blk)
    d_aq, d_kv, dsink = _attn_bwd(proj, attn_sinks, do_a)
    d_hgrn, dlb = _hgrn_bwd(proj, hgrn_lower_bound, do_raw, states, big_blk, HGRN_GROUP)
    pieces = (d_hgrn, d_hg, d_aq, d_kv, d_agm)
    dw_pieces = [_inproj_wgrad_piece(xnt, p, CB, "inproj_wgrad_" + n)
                 for p, n in zip(pieces, ("hgrn", "hgate", "aq", "kv", "gates"))]

    dwin_r = jnp.concatenate(dw_pieces, axis=1).reshape(D, NDEV, IN_SHARD).transpose(1, 0, 2).astype(BF16)
    slots = lambda a: a.reshape(NDEV, ROW_SHARD, D).astype(BF16)
    got = _exchange_pair([dwin_r])
    core = lax.axis_index("c").astype(jnp.int32).reshape(1)
    s_in = _pair_sum(dwin_r, got[0], core, 4 * ROW_SHARD, "pair_sum_w_in")
    rin, = _exchange_chips([s_in])
    rbh, rba, rout = _exchange_square([slots(dwbh), slots(dwba), slots(dwout)])
    grad_x, dnw = _inproj_dgrad(pieces, w_p, x2d, dx2, norm_w, big_blk, after=s_in)
    rsm = _exchange_small(_pack_small_grads(dnw, dlb, sums, dsink))
    g_in, d_in, nm_in, nv_in = _adamw_shard(rin, s_in, chip, w_in, m_w_in, v_w_in, 128, "adamw_w_in")
    g_bh, d_bh, nm_bh, nv_bh = _adamw_sum8(rbh, w_branch_hgrn, m_w_branch_hgrn, v_w_branch_hgrn, dnw, "adamw_w_bh")
    g_ba, d_ba, nm_ba, nv_ba = _adamw_sum8(rba, w_branch_attn, m_w_branch_attn, v_w_branch_attn, dnw, "adamw_w_ba")
    g_out, d_out, nm_out, nv_out = _adamw_sum8(rout, w_out, m_w_out, v_w_out, dnw, "adamw_w_out")
    (sg, sd, sm, sv), loss = _adamw_small(
        rsm,
        (norm_w, hgrn_lower_bound, hgrn_norm_w, fnw, attn_sinks),
        (m_norm_w, m_hgrn_lower_bound, m_hgrn_norm_w, m_final_norm_w.reshape(1, D), m_attn_sinks),
        (v_norm_w, v_hgrn_lower_bound, v_hgrn_norm_w, v_final_norm_w.reshape(1, D), v_attn_sinks))

    def group(s, w_in_v, bh, ba, out):
        nw, lb, hnw, fn, sinks = s
        return (nw, w_in_v, lb, hnw, bh, sinks, ba, out, fn.reshape(D))

    return (loss, grad_x.reshape(1, t, D),
            *group(sg, g_in, g_bh, g_ba, g_out), *group(sd, d_in, d_bh, d_ba, d_out),
            *group(sm, nm_in, nm_bh, nm_ba, nm_out), *group(sv, nv_in, nv_bh, nv_ba, nv_out))
```

```python
import jax
import jax.numpy as jnp
from jax import lax
from jax.experimental import pallas as pl
from jax.experimental.pallas import tpu as pltpu
from jax.experimental.pallas import tpu_sc as plsc

F32 = jnp.float32
BF16 = jnp.bfloat16

D = 1024
DIN = 8704
NDEV = 8
IN_SHARD = DIN // NDEV
PAIR = 2 * IN_SHARD
ROW_SHARD = D // NDEV
HEADS = 8
HD = 128
CH = 64
HGRN_GROUP = 8
QH = 16
AB = 128
EPS = 1e-6
NEG = -1e30
ATT_SCALE = 0.125

C_HG = 3072
C_AQ = 4096
C_AK = 5120
C_AV = 5376
C_AG = 5632
C_MH = 6656
C_MA = 7680
CB = 512

LR = 0.001
B1 = 0.9
B2 = 0.999
ADAM_EPS = 1e-08
WD = 0.01
STEP = 10

MESH = pl.DeviceIdType.MESH


def _cp(vmem_mb):
    return pltpu.CompilerParams(vmem_limit_bytes=vmem_mb * 1024 * 1024)


def _mm(a, b):
    return jnp.dot(a, b, preferred_element_type=F32)


def _mm_nt(a, b):
    return lax.dot_general(a, b, (((1,), (1,)), ((), ())), preferred_element_type=F32)


def _mm_tn(a, b):
    return lax.dot_general(a, b, (((0,), (0,)), ((), ())), preferred_element_type=F32)


def _tri3(lower):
    r = lax.broadcasted_iota(jnp.int32, (CH, 3 * CH), 0)
    c = lax.broadcasted_iota(jnp.int32, (CH, 3 * CH), 1)
    c = jnp.where(c >= 2 * CH, c - 2 * CH, jnp.where(c >= CH, c - CH, c))
    return ((r >= c) if lower else (c >= r)).astype(BF16)


def _mm_tri_exact(tri3, g):
    g1 = g.astype(BF16)
    r1 = g - g1.astype(F32)
    g2 = r1.astype(BF16)
    g3 = (r1 - g2.astype(F32)).astype(BF16)
    return _mm(tri3, jnp.concatenate([g1, g2, g3], axis=0))


def _sigmoid(v):
    return 0.5 * jnp.tanh(0.5 * v) + 0.5


def _bf(v):
    return v.astype(BF16)


def _place():
    x, y, c = lax.axis_index("x"), lax.axis_index("y"), lax.axis_index("c")
    return (x, y, c), (x, y, 1 - c), [(1 - x, y), (x, 1 - y), (1 - x, 1 - y)]


def _dev_index(px, py, pc):
    return 4 * px + 2 * py + pc


def _gather_in_projection(w_in_s, x2d, norm_w):
    half = D // 2
    t = x2d.shape[0]
    prep_rows = min(512, t)
    nprep = t // prep_rows

    def body(win_ref, x_hbm, nw_ref, wp_g, xn_hbm, xnt_hbm, proj_hbm, give, take, mine, xbuf, xnbuf, xntbuf, w_own, pbuf,
             send_sems, recv_sems, loc_sem, swap_sems, in_sems, out_sems, own_sem):
        (x, y, c), sibling, chips = _place()
        give[...] = win_ref[pl.ds(pl.multiple_of(half * (1 - c), half), half), :].astype(BF16)
        swap = pltpu.make_async_remote_copy(src_ref=give, dst_ref=take, send_sem=swap_sems.at[0], recv_sem=swap_sems.at[1],
                                            device_id=sibling, device_id_type=MESH)
        swap.start()
        swap.wait()
        own = win_ref[pl.ds(pl.multiple_of(half * c, half), half), :]
        other = take[...].astype(F32)
        mine[...] = jnp.where(c == 0, jnp.concatenate([own, other], axis=1),
                              jnp.concatenate([other, own], axis=1)).astype(BF16)

        def place(px, py, pc):
            return wp_g.at[2 * px + py, pl.ds(pl.multiple_of(half * pc, half), half), :]

        def copy(kind, origin, to, src=mine):
            return pltpu.make_async_remote_copy(
                src_ref=src, dst_ref=place(*origin), send_sem=send_sems.at[kind], recv_sem=recv_sems.at[kind],
                device_id=to, device_id_type=MESH)

        me = (x, y, c)
        local = pltpu.make_async_copy(mine, place(*me), loc_sem)
        local.start()
        first = [copy(0, me, sibling)] + [copy(1 + j, me, (*chip, c)) for j, chip in enumerate(chips)]
        for cp in first:
            cp.start()

        copy(0, (x, y, 1 - c), me).wait_recv()
        local.wait()
        my_chip = 2 * x + y
        fetch = pltpu.make_async_copy(wp_g.at[my_chip], w_own, own_sem)
        fetch.start()

        def rows_of(i):
            return pl.ds(pl.multiple_of(i * prep_rows, prep_rows), prep_rows)

        def load(i, slot):
            return pltpu.make_async_copy(x_hbm.at[rows_of(i), :], xbuf.at[slot], in_sems.at[slot])

        def stores(i, slot):
            own_cols = pl.ds(pl.multiple_of(my_chip * PAIR, 128), PAIR)
            return (pltpu.make_async_copy(xnbuf.at[slot], xn_hbm.at[rows_of(i), :], out_sems.at[slot, 0]),
                    pltpu.make_async_copy(xntbuf.at[slot], xnt_hbm.at[:, rows_of(i)], out_sems.at[slot, 1]),
                    pltpu.make_async_copy(pbuf.at[slot], proj_hbm.at[rows_of(i), own_cols], out_sems.at[slot, 2]))

        load(0, 0).start()
        fetch.wait()

        def prep(i, carry):
            slot = lax.rem(i, 2)
            load(i, slot).wait()

            @pl.when(i + 1 < nprep)
            def _():
                load(i + 1, 1 - slot).start()

            @pl.when(i >= 2)
            def _():
                for cp in stores(i - 2, slot):
                    cp.wait()

            xv = xbuf[slot]
            xn = (xv * lax.rsqrt(jnp.mean(xv * xv, axis=-1, keepdims=True) + EPS)) * nw_ref[...]
            xn_b = xn.astype(BF16)
            xnbuf[slot] = xn_b
            xntbuf[slot] = xn.T.astype(BF16)
            pbuf[slot] = _mm(xn_b, w_own[...])
            for cp in stores(i, slot):
                cp.start()
            return carry

        lax.fori_loop(0, nprep, prep, 0)
        for i in range(max(nprep - 2, 0), nprep):
            for cp in stores(i, i % 2):
                cp.wait()

        passed = []
        for j, chip in enumerate(chips):
            copy(1 + j, (*chip, c), me).wait_recv()
            cp = copy(4 + j, (*chip, c), sibling, src=place(*chip, c))
            cp.start()
            passed.append(cp)
        for j, chip in enumerate(chips):
            copy(4 + j, (*chip, 1 - c), me).wait_recv()
        for cp in first + passed:
            cp.wait_send()

    vm = pl.BlockSpec(memory_space=pltpu.VMEM)
    hbm = pl.BlockSpec(memory_space=pl.ANY)
    return pl.pallas_call(
        body,
        out_shape=[jax.ShapeDtypeStruct((NDEV // 2, D, PAIR), BF16), jax.ShapeDtypeStruct((t, D), BF16),
                   jax.ShapeDtypeStruct((D, t), BF16), jax.ShapeDtypeStruct((t, DIN), F32)],
        in_specs=[vm, hbm, vm],
        out_specs=[hbm, hbm, hbm, hbm],
        scratch_shapes=[pltpu.VMEM((half, IN_SHARD), BF16), pltpu.VMEM((half, IN_SHARD), BF16),
                        pltpu.VMEM((half, PAIR), BF16),
                        pltpu.VMEM((2, prep_rows, D), F32), pltpu.VMEM((2, prep_rows, D), BF16),
                        pltpu.VMEM((2, D, prep_rows), BF16),
                        pltpu.VMEM((D, PAIR), BF16), pltpu.VMEM((2, prep_rows, PAIR), F32),
                        pltpu.SemaphoreType.DMA((NDEV - 1,)), pltpu.SemaphoreType.DMA((NDEV - 1,)),
                        pltpu.SemaphoreType.DMA, pltpu.SemaphoreType.DMA((2,)),
                        pltpu.SemaphoreType.DMA((2,)), pltpu.SemaphoreType.DMA((2, 3)), pltpu.SemaphoreType.DMA],
        name="gather_in_projection", compiler_params=_cp(56),
    )(w_in_s, x2d, norm_w)


def _gather_square(shards, after):
    n = len(shards)

    def launch(*refs):
        ins, outs = refs[:n], refs[n + 1:2 * n + 1]
        send_sems, recv_sems, loc_sems = refs[2 * n + 1:]
        (x, y, c), _, _ = _place()
        me = _dev_index(x, y, c)
        peers = [(1 - x if r & 4 else x, 1 - y if r & 2 else y, 1 - c if r & 1 else c) for r in range(1, NDEV)]
        barrier = pltpu.get_barrier_semaphore()
        for peer in peers:
            pl.semaphore_signal(barrier, inc=1, device_id=peer, device_id_type=MESH)
        pl.semaphore_wait(barrier, NDEV - 1)
        local = [pltpu.make_async_copy(ins[k], outs[k].at[me], loc_sems.at[k]) for k in range(n)]
        copies = [pltpu.make_async_remote_copy(
            src_ref=ins[k], dst_ref=outs[k].at[me], send_sem=send_sems.at[r, k], recv_sem=recv_sems.at[r, k],
            device_id=peer, device_id_type=MESH) for r, peer in enumerate(peers) for k in range(n)]
        for cp in local + copies:
            cp.start()
        for r, peer in enumerate(peers):
            for k in range(n):
                pltpu.make_async_remote_copy(
                    src_ref=ins[k], dst_ref=outs[k].at[_dev_index(*peer)], send_sem=send_sems.at[r, k],
                    recv_sem=recv_sems.at[r, k], device_id=peer, device_id_type=MESH).wait_recv()
        for cp in copies:
            cp.wait_send()
        for cp in local:
            cp.wait()

    return pl.kernel(
        launch, out_type=[jax.ShapeDtypeStruct((NDEV,) + a.shape, a.dtype) for a in shards],
        mesh=plsc.ScalarSubcoreMesh(axis_name="sequencer", num_cores=1), name="gather_square",
        scratch_types=(pltpu.SemaphoreType.DMA((NDEV - 1, n)), pltpu.SemaphoreType.DMA((NDEV - 1, n)),
                       pltpu.SemaphoreType.DMA((n,))),
        compiler_params=pltpu.CompilerParams(collective_id=2),
    )(*shards, after)


def _exchange_pair(arrs):
    n = len(arrs)

    def launch(*refs):
        ins, got = refs[:n], refs[n:2 * n]
        send_sems, recv_sems = refs[2 * n:]
        (x, y, c), sibling, _ = _place()
        barrier = pltpu.get_barrier_semaphore()
        pl.semaphore_signal(barrier, inc=1, device_id=sibling, device_id_type=MESH)
        pl.semaphore_wait(barrier, 1)
        sends = [pltpu.make_async_remote_copy(
            src_ref=ins[k].at[_dev_index(q // 2, q % 2, 1 - c)], dst_ref=got[k].at[q], send_sem=send_sems.at[q, k],
            recv_sem=recv_sems.at[q, k], device_id=sibling, device_id_type=MESH) for q in range(4) for k in range(n)]
        for cp in sends:
            cp.start()
        for cp in sends:
            cp.wait_recv()
        for cp in sends:
            cp.wait_send()

    return pl.kernel(
        launch, out_type=[jax.ShapeDtypeStruct((4,) + a.shape[1:], a.dtype) for a in arrs],
        mesh=plsc.ScalarSubcoreMesh(axis_name="sequencer", num_cores=1), name="exchange_pair",
        scratch_types=(pltpu.SemaphoreType.DMA((4, n)), pltpu.SemaphoreType.DMA((4, n))),
        compiler_params=pltpu.CompilerParams(collective_id=0),
    )(*arrs)


def _pair_sum(full, got, core, rows, name):
    _, nr, nc = got.shape

    def body(core_ref, a_ref, b_ref, o_ref):
        o_ref[...] = (a_ref[...].astype(F32) + b_ref[...].astype(F32)).astype(BF16)

    blk = pl.BlockSpec((1, rows, nc), lambda q, i, core_ref: (q, i, 0))
    return pl.pallas_call(
        body,
        grid_spec=pltpu.PrefetchScalarGridSpec(
            num_scalar_prefetch=1, grid=(4, nr // rows),
            in_specs=[pl.BlockSpec((1, rows, nc), lambda q, i, core_ref: (2 * q + core_ref[0], i, 0)), blk],
            out_specs=blk),
        out_shape=jax.ShapeDtypeStruct(got.shape, BF16), name=name,
    )(core, full, got)


def _exchange_chips(sums):
    n = len(sums)

    def launch(*refs):
        ins, outs = refs[:n], refs[n:2 * n]
        send_sems, recv_sems = refs[2 * n:]
        (x, y, c), _, chips = _place()
        barrier = pltpu.get_barrier_semaphore()
        for px, py in chips:
            pl.semaphore_signal(barrier, inc=1, device_id=(px, py, c), device_id_type=MESH)
        pl.semaphore_wait(barrier, len(chips))
        copies = [pltpu.make_async_remote_copy(
            src_ref=ins[k].at[2 * px + py], dst_ref=outs[k].at[j], send_sem=send_sems.at[j, k],
            recv_sem=recv_sems.at[j, k], device_id=(px, py, c), device_id_type=MESH)
            for j, (px, py) in enumerate(chips) for k in range(n)]
        for cp in copies:
            cp.start()
        for cp in copies:
            cp.wait_recv()
        for cp in copies:
            cp.wait_send()

    return pl.kernel(
        launch, out_type=[jax.ShapeDtypeStruct((3,) + a.shape[1:], a.dtype) for a in sums],
        mesh=plsc.ScalarSubcoreMesh(axis_name="sequencer", num_cores=1), name="exchange_chips",
        scratch_types=(pltpu.SemaphoreType.DMA((3, n)), pltpu.SemaphoreType.DMA((3, n))),
        compiler_params=pltpu.CompilerParams(collective_id=1),
    )(*sums)


def _exchange_square(partials):
    n = len(partials)

    def launch(*refs):
        ins, outs = refs[:n], refs[n:2 * n]
        send_sems, recv_sems, loc_sems = refs[2 * n:]
        (x, y, c), _, _ = _place()
        me = _dev_index(x, y, c)
        peers = [(1 - x if r & 4 else x, 1 - y if r & 2 else y, 1 - c if r & 1 else c) for r in range(1, NDEV)]
        barrier = pltpu.get_barrier_semaphore()
        for peer in peers:
            pl.semaphore_signal(barrier, inc=1, device_id=peer, device_id_type=MESH)
        pl.semaphore_wait(barrier, NDEV - 1)
        local = [pltpu.make_async_copy(ins[k].at[me], outs[k].at[me], loc_sems.at[k]) for k in range(n)]
        copies = [pltpu.make_async_remote_copy(
            src_ref=ins[k].at[_dev_index(*peer)], dst_ref=outs[k].at[me], send_sem=send_sems.at[r, k],
            recv_sem=recv_sems.at[r, k], device_id=peer, device_id_type=MESH)
            for r, peer in enumerate(peers) for k in range(n)]
        for cp in local + copies:
            cp.start()
        for r, peer in enumerate(peers):
            for k in range(n):
                pltpu.make_async_remote_copy(
                    src_ref=ins[k].at[me], dst_ref=outs[k].at[_dev_index(*peer)], send_sem=send_sems.at[r, k],
                    recv_sem=recv_sems.at[r, k], device_id=peer, device_id_type=MESH).wait_recv()
        for cp in copies:
            cp.wait_send()
        for cp in local:
            cp.wait()

    return pl.kernel(
        launch, out_type=[jax.ShapeDtypeStruct(a.shape, a.dtype) for a in partials],
        mesh=plsc.ScalarSubcoreMesh(axis_name="sequencer", num_cores=1), name="exchange_square",
        scratch_types=(pltpu.SemaphoreType.DMA((NDEV - 1, n)), pltpu.SemaphoreType.DMA((NDEV - 1, n)),
                       pltpu.SemaphoreType.DMA((n,))),
        compiler_params=pltpu.CompilerParams(collective_id=3),
    )(*partials)


def _exchange_small(small):
    def body(sm_ref, out_ref, send_sems, recv_sems):
        (x, y, c), _, _ = _place()
        me = _dev_index(x, y, c)
        peers = [(1 - x if r & 4 else x, 1 - y if r & 2 else y, 1 - c if r & 1 else c) for r in range(1, NDEV)]
        out_ref[me] = sm_ref[...]
        copies = [pltpu.make_async_remote_copy(
            src_ref=sm_ref, dst_ref=out_ref.at[me], send_sem=send_sems.at[r], recv_sem=recv_sems.at[r],
            device_id=peer, device_id_type=MESH) for r, peer in enumerate(peers)]
        for cp in copies:
            cp.start()
        for r, peer in enumerate(peers):
            pltpu.make_async_remote_copy(
                src_ref=sm_ref, dst_ref=out_ref.at[_dev_index(*peer)], send_sem=send_sems.at[r], recv_sem=recv_sems.at[r],
                device_id=peer, device_id_type=MESH).wait_recv()
        for cp in copies:
            cp.wait_send()

    vm = pl.BlockSpec(memory_space=pltpu.VMEM)
    return pl.pallas_call(
        body, out_shape=jax.ShapeDtypeStruct((NDEV,) + small.shape, F32), in_specs=[vm], out_specs=vm,
        scratch_shapes=[pltpu.SemaphoreType.DMA((NDEV - 1,)), pltpu.SemaphoreType.DMA((NDEV - 1,))],
        name="exchange_small",
    )(small)


def _inproj_fwd(xn, w_pairs, proj, chip, tb):
    t = xn.shape[0]
    nblk, _, nb = w_pairs.shape

    def body(chip_ref, xn_ref, w_ref, proj_in, proj_ref):
        proj_ref[...] = _mm(xn_ref[...], w_ref[0])

    def other(j, chip_ref):
        return j + (j >= chip_ref[0]).astype(jnp.int32)

    return pl.pallas_call(
        body,
        grid_spec=pltpu.PrefetchScalarGridSpec(
            num_scalar_prefetch=1, grid=(t // tb, nblk - 1),
            in_specs=[pl.BlockSpec((tb, D), lambda i, j, chip_ref: (i, 0)),
                      pl.BlockSpec((1, D, nb), lambda i, j, chip_ref: (other(j, chip_ref), 0, 0)),
                      pl.BlockSpec(memory_space=pl.ANY)],
            out_specs=pl.BlockSpec((tb, nb), lambda i, j, chip_ref: (i, other(j, chip_ref)))),
        out_shape=jax.ShapeDtypeStruct((t, DIN), F32),
        input_output_aliases={3: 0},
        name="inproj_fwd", compiler_params=_cp(56),
    )(chip, xn, w_pairs, proj)


def _lower_bound(lb_ref):
    a0 = lb_ref[0:1, :]
    a1 = lb_ref[1:2, :]
    mx = jnp.maximum(a0, a1)
    e0 = jnp.exp(a0 - mx)
    e1 = jnp.exp(a1 - mx)
    return e0 / (e0 + e1)


def _hgrn_chunk_fwd(hq, hf, lb, tril):
    sg = _sigmoid(hf)
    f = lb + (1.0 - lb) * sg
    g = jnp.log(f)
    k = 1.0 - f
    sq = _sigmoid(hq)
    q = hq * sq
    b = _mm_tri_exact(tril, g)
    last_row = lax.broadcasted_iota(jnp.int32, b.shape, 0) == CH - 1
    b_last = jnp.sum(jnp.where(last_row, b, 0.0), axis=0, keepdims=True)
    c = 0.5 * b_last
    eb = jnp.exp(b)
    ea = jnp.exp(b - c)
    ek = jnp.exp(c - b)
    ed = jnp.exp(b_last - b)
    ebl = jnp.exp(b_last)
    return dict(sg=sg, f=f, k=k, sq=sq, q=q, eb=eb, ea=ea, ek=ek, ed=ed, ebl=ebl,
                qe=q * eb, qa=q * ea, ka=k * ek, kd=k * ed)


def _tri(lower):
    r = lax.broadcasted_iota(jnp.int32, (CH, CH), 0)
    c = lax.broadcasted_iota(jnp.int32, (CH, CH), 1)
    return (r >= c) if lower else (c >= r)


def _head_segment(p_ref, rows, j, hg):
    return p_ref[rows, j * HD * hg:(j + 1) * HD * hg]


def _head(a, k):
    return a[:, k * HD:(k + 1) * HD]


def _hgrn_fwd(proj, lbw, rb, hg):
    assert hg == HEADS
    t = proj.shape[0]
    ncb = rb // CH

    def body(p_ref, lb_ref, o_ref, st_ref, s_scr):
        @pl.when(pl.program_id(1) == 0)
        def _():
            s_scr[...] = jnp.zeros_like(s_scr)

        lb = _lower_bound(lb_ref)
        causal = _tri(True)
        tril = _tri3(True)
        heads = range(hg)

        def chunk(cc, carry):
            r0 = pl.multiple_of(cc * CH, CH)
            rows = pl.ds(r0, CH)
            e = _hgrn_chunk_fwd(_head_segment(p_ref, rows, 0, hg), _head_segment(p_ref, rows, 1, hg), lb, tril)
            v = _bf(_head_segment(p_ref, rows, 2, hg))
            sts = [s_scr[k] for k in heads]
            qa, ka, qe, kd = _bf(e["qa"]), _bf(e["ka"]), _bf(e["qe"]), _bf(e["kd"])
            a = [_bf(jnp.where(causal, _mm_nt(_head(qa, k), _head(ka, k)), 0.0)) for k in heads]
            o_inter = [_mm_nt(_head(qe, k), _bf(sts[k])) for k in heads]
            kv = [_mm_tn(_head(v, k), _head(kd, k)) for k in heads]
            o_intra = [_mm(a[k], _head(v, k)) for k in heads]
            for k in heads:
                st_ref[cc, k] = sts[k]
                o_ref[rows, k * HD:(k + 1) * HD] = o_inter[k] + o_intra[k]
                s_scr[k] = sts[k] * _head(e["ebl"], k) + kv[k]
            return carry

        lax.fori_loop(0, ncb, chunk, 0, unroll=4)

    return pl.pallas_call(
        body, grid=(HEADS // hg, t // rb),
        in_specs=[pl.BlockSpec((rb, 3 * HD * hg), lambda h, i: (i, h)), pl.BlockSpec((2, HD * hg), lambda h, i: (0, h))],
        out_specs=[pl.BlockSpec((rb, HD * hg), lambda h, i: (i, h)),
                   pl.BlockSpec((ncb, hg, HD, HD), lambda h, i: (i, h, 0, 0))],
        out_shape=[jax.ShapeDtypeStruct((t, D), F32), jax.ShapeDtypeStruct((t // CH, HEADS, HD, HD), F32)],
        scratch_shapes=[pltpu.VMEM((hg, HD, HD), F32)],
        name="hgrn_fwd", compiler_params=_cp(48),
    )(proj, lbw)


def _hgrn_bwd(proj, lbw, do_raw, states, rb, hg):
    assert hg == HEADS
    t = proj.shape[0]
    nblk = t // rb
    ncb = rb // CH
    wd = HD * hg

    def body(p_ref, lb_ref, do_ref, st_ref, dp_ref, dlb_ref, ds_scr):
        @pl.when(pl.program_id(1) == 0)
        def _():
            ds_scr[...] = jnp.zeros_like(ds_scr)
            dlb_ref[...] = jnp.zeros_like(dlb_ref)

        lb = _lower_bound(lb_ref)
        causal = _tri(True)
        tril = _tri3(True)
        triu = _tri3(False)
        last_row = lax.broadcasted_iota(jnp.int32, (CH, HD * hg), 0) == CH - 1
        row0 = lax.broadcasted_iota(jnp.int32, (8, HD * hg), 0) == 0
        heads = range(hg)
        wide = lambda parts: jnp.concatenate(parts, axis=1)

        def chunk(it, carry):
            cc = ncb - 1 - it
            r0 = pl.multiple_of(cc * CH, CH)
            rows = pl.ds(r0, CH)
            hq = _head_segment(p_ref, rows, 0, hg)
            e = _hgrn_chunk_fwd(hq, _head_segment(p_ref, rows, 1, hg), lb, tril)
            v = _bf(_head_segment(p_ref, rows, 2, hg))
            do = _bf(do_ref[rows, :])
            sts = [st_ref[cc, k] for k in heads]
            dsts = [ds_scr[k] for k in heads]
            dlb_acc = dlb_ref[...]
            qa, ka, qe, kd = _bf(e["qa"]), _bf(e["ka"]), _bf(e["qe"]), _bf(e["kd"])
            a = [_bf(jnp.where(causal, _mm_nt(_head(qa, k), _head(ka, k)), 0.0)) for k in heads]
            da = [_bf(jnp.where(causal, _mm_nt(_head(do, k), _head(v, k)), 0.0)) for k in heads]
            dqe = wide([_mm(_head(do, k), _bf(sts[k])) for k in heads])
            dkd = wide([_mm(_head(v, k), _bf(dsts[k])) for k in heads])
            dv_state = [_mm_nt(_head(kd, k), _bf(dsts[k])) for k in heads]
            ds_new = [_mm_tn(_head(do, k), _head(qe, k)) for k in heads]
            dv_intra = [_mm_tn(a[k], _head(do, k)) for k in heads]
            dqa = wide([_mm(da[k], _head(ka, k)) for k in heads])
            dka = wide([_mm_tn(da[k], _head(qa, k)) for k in heads])
            dv = wide([dv_intra[k] + dv_state[k] for k in heads])
            dbl = e["ebl"] * wide([jnp.sum(sts[k] * dsts[k], axis=0, keepdims=True) for k in heads])
            dq = dqe * e["eb"] + dqa * e["ea"]
            dk = dka * e["ek"] + dkd * e["ed"]
            dkd_kd = dkd * kd.astype(F32)
            db = dqe * qe.astype(F32) + dqa * qa.astype(F32) - dka * ka.astype(F32) - dkd_kd
            db = db + jnp.where(last_row, dbl + jnp.sum(dkd_kd, axis=0, keepdims=True), 0.0)
            dg = _mm_tri_exact(triu, db)
            df = dg / e["f"] - dk
            sg = e["sg"]
            sq = e["sq"]
            dhq = _bf(dq * (sq * (1.0 + hq * (1.0 - sq))))
            dhf = _bf(df * (1.0 - lb) * sg * (1.0 - sg))
            dhi = _bf(dv)
            dlb_new = dlb_acc + jnp.where(row0, jnp.sum(df * (1.0 - sg), axis=0, keepdims=True), 0.0)
            for k in heads:
                ds_scr[k] = ds_new[k] + dsts[k] * _head(e["ebl"], k)
            dp_ref[rows, 0:wd] = dhq
            dp_ref[rows, wd:2 * wd] = dhf
            dp_ref[rows, 2 * wd:3 * wd] = dhi
            dlb_ref[...] = dlb_new
            return carry

        lax.fori_loop(0, ncb, chunk, 0, unroll=2)

    rev = lambda h, i: (nblk - 1 - i, h)
    return pl.pallas_call(
        body, grid=(HEADS // hg, nblk),
        in_specs=[pl.BlockSpec((rb, 3 * HD * hg), rev), pl.BlockSpec((2, HD * hg), lambda h, i: (0, h)),
                  pl.BlockSpec((rb, HD * hg), rev), pl.BlockSpec((ncb, hg, HD, HD), lambda h, i: (nblk - 1 - i, h, 0, 0))],
        out_specs=[pl.BlockSpec((rb, 3 * HD * hg), rev), pl.BlockSpec((8, HD * hg), lambda h, i: (0, h))],
        out_shape=[jax.ShapeDtypeStruct((t, 3 * D), BF16), jax.ShapeDtypeStruct((8, D), F32)],
        scratch_shapes=[pltpu.VMEM((hg, HD, HD), F32)],
        name="hgrn_bwd", compiler_params=_cp(48),
    )(proj, lbw, do_raw, states)


def _kv_variants(tile, odd):
    low = lax.broadcasted_iota(jnp.int32, tile.shape, 1) < 64
    if odd:
        hi = jnp.where(low, 0.0, tile)
        lo = pltpu.roll(hi, 64, 1)
    else:
        lo = jnp.where(low, tile, 0.0)
        hi = pltpu.roll(lo, 64, 1)
    return _bf(lo), _bf(hi)


def _attn_masks(n):
    qi = lax.broadcasted_iota(jnp.int32, (AB, AB), 0)
    kj = lax.broadcasted_iota(jnp.int32, (AB, AB), 1)
    cur = kj <= qi
    return cur, cur | (n > 0), qi <= kj


def _kv_all(prev_ref, cur_ref):
    out = []
    for tl in range(2):
        cols = slice(tl * 128, (tl + 1) * 128)
        tile = jnp.concatenate([prev_ref[:, cols], cur_ref[:, cols]], axis=0)
        out.append(_kv_variants(tile, 0))
        out.append(_kv_variants(tile, 1))
    return out


def _window(a2, cur):
    return jnp.where(cur, a2[:, AB:], a2[:, :AB])


def _attn_softmax(scores, sinks, cur, ok):
    s = [jnp.where(ok, _window(s2, cur) * ATT_SCALE, NEG) for s2 in scores]
    m = [jnp.maximum(jnp.max(si, axis=-1, keepdims=True), sink) for si, sink in zip(s, sinks)]
    p = [jnp.exp(si - mi) for si, mi in zip(s, m)]
    es = [jnp.exp(sink - mi) for sink, mi in zip(sinks, m)]
    inv = [1.0 / (jnp.sum(pi, axis=-1, keepdims=True) + ei) for pi, ei in zip(p, es)]
    return [pi * ii for pi, ii in zip(p, inv)], [ei * ii for ei, ii in zip(es, inv)]


def _spread(pc, cur):
    return jnp.concatenate([jnp.where(cur, 0.0, pc), jnp.where(cur, pc, 0.0)], axis=1)


def _spread_t(pct, cur_t):
    return jnp.concatenate([jnp.where(cur_t, 0.0, pct), jnp.where(cur_t, pct, 0.0)], axis=0)


def _attn_fwd(proj, sinks):
    t = proj.shape[0]
    nb = t // AB

    nsub = 2
    assert nb % nsub == 0

    def body(q_ref, kc_ref, kp_ref, vc_ref, vp_ref, sink_ref, o_ref):
        sinks_v = [sink_ref[0, h] for h in range(QH)]
        heads = [(j, ab) for j in range(8) for ab in range(2)]
        for sb in range(nsub):
            rows = pl.ds(AB * sb, AB)
            before = pl.ds(AB * (sb - 1), AB)
            cur, ok, _ = _attn_masks(nsub * pl.program_id(0) + sb)
            kvars = _kv_all(kp_ref if sb == 0 else kc_ref.at[before, :], kc_ref.at[rows, :])
            vvars = _kv_all(vp_ref if sb == 0 else vc_ref.at[before, :], vc_ref.at[rows, :])
            qps = [_bf(q_ref[rows, 128 * j:128 * (j + 1)]) for j in range(8)]
            scores = [_mm_nt(qps[j], kvars[j // 2][ab]) for j, ab in heads]
            pcs, _ = _attn_softmax(scores, sinks_v, cur, ok)
            parts = [_mm(_bf(_spread(pcs[h], cur)), vvars[j // 2][ab]) for h, (j, ab) in enumerate(heads)]
            for j in range(8):
                o_ref[rows, 128 * j:128 * (j + 1)] = parts[2 * j] + parts[2 * j + 1]

    prev = lambda n: jnp.maximum(nsub * n - 1, 0)
    step = nsub * AB
    return pl.pallas_call(
        body, grid=(nb // nsub,),
        in_specs=[pl.BlockSpec((step, D), lambda n: (n, C_AQ // D)),
                  pl.BlockSpec((step, 256), lambda n: (n, C_AK // 256)),
                  pl.BlockSpec((AB, 256), lambda n: (prev(n), C_AK // 256)),
                  pl.BlockSpec((step, 256), lambda n: (n, C_AV // 256)),
                  pl.BlockSpec((AB, 256), lambda n: (prev(n), C_AV // 256)),
                  pl.BlockSpec(memory_space=pltpu.SMEM)],
        out_specs=pl.BlockSpec((step, D), lambda n: (n, 0)),
        out_shape=jax.ShapeDtypeStruct((t, D), F32),
        name="attn_fwd", compiler_params=_cp(32),
    )(proj, proj, proj, proj, proj, sinks)


def _attn_bwd(proj, sinks, do_a):
    t = proj.shape[0]
    nb = t // AB
    nsub = 2
    assert nb % nsub == 0
    steps = nb // nsub
    step = nsub * AB

    def body(q_ref, kc_ref, kp_ref, vc_ref, vp_ref, do_ref, sink_ref, dq_ref, dkv_ref, dsink_ref, carry):
        n = pl.program_id(0)

        @pl.when(n == 0)
        def _():
            dsink_ref[...] = jnp.zeros_like(dsink_ref)
            carry[...] = jnp.zeros_like(carry)

        def one_block(sb):
            rows = pl.ds(AB * sb, AB)
            before = pl.ds(AB * (sb - 1), AB)
            cur, ok, cur_t = _attn_masks(nsub * n + sb)
            low = lax.broadcasted_iota(jnp.int32, (2 * AB, 128), 1) < 64
            lane = lax.broadcasted_iota(jnp.int32, (8, 128), 1)
            row0 = lax.broadcasted_iota(jnp.int32, (8, 128), 0) == 0
            kvars = _kv_all(kp_ref if sb == 0 else kc_ref.at[before, :], kc_ref.at[rows, :])
            vvars = _kv_all(vp_ref if sb == 0 else vc_ref.at[before, :], vc_ref.at[rows, :])
            qps = [_bf(q_ref[rows, 128 * j:128 * (j + 1)]) for j in range(8)]
            dops = [_bf(do_ref[rows, 128 * j:128 * (j + 1)]) for j in range(8)]
            heads = [(j, ab) for j in range(8) for ab in range(2)]
            scores = [_mm_nt(qps[j], kvars[j // 2][ab]) for j, ab in heads]
            dps = [_mm_nt(dops[j], vvars[j // 2][ab]) for j, ab in heads]
            pcs, pss = _attn_softmax(scores, [sink_ref[0, h] for h in range(QH)], cur, ok)
            dpcs = [_window(dp2, cur) for dp2 in dps]
            rss = [jnp.sum(pc * dpc, axis=-1, keepdims=True) for pc, dpc in zip(pcs, dpcs)]
            dscs = [pc * (dpc - rs) for pc, dpc, rs in zip(pcs, dpcs, rss)]
            dsink = jnp.zeros((8, 128), F32)
            for h in range(QH):
                dsink = dsink + jnp.where(row0 & (lane == h), -jnp.sum(pss[h] * rss[h]), 0.0)
            dq_terms = [_mm(_bf(_spread(dscs[h], cur)), kvars[j // 2][ab]) for h, (j, ab) in enumerate(heads)]
            for j in range(8):
                dq_ref[rows, 128 * j:128 * (j + 1)] = _bf((dq_terms[2 * j] + dq_terms[2 * j + 1]) * ATT_SCALE)
            dsc_t = [_bf(_spread_t(dsc.T, cur_t)) for dsc in dscs]
            pc_t = [_bf(_spread_t(pc.T, cur_t)) for pc in pcs]
            dk_terms = [_mm(dsc_t[h], qps[j]) for h, (j, ab) in enumerate(heads)]
            dv_terms = [_mm(pc_t[h], dops[j]) for h, (j, ab) in enumerate(heads)]
            dk_ab = [[dk_terms[4 * g + ab] + dk_terms[4 * g + 2 + ab] for ab in range(2)] for g in range(4)]
            dv_ab = [[dv_terms[4 * g + ab] + dv_terms[4 * g + 2 + ab] for ab in range(2)] for g in range(4)]
            dkts, dvts = [], []
            for tl in range(2):
                ke, ko = dk_ab[2 * tl], dk_ab[2 * tl + 1]
                ve, vo = dv_ab[2 * tl], dv_ab[2 * tl + 1]
                dkts.append((jnp.where(low, ke[0], 0.0) + pltpu.roll(jnp.where(low, 0.0, ke[1]), 64, 1)
                             + jnp.where(low, 0.0, ko[1]) + pltpu.roll(jnp.where(low, ko[0], 0.0), 64, 1)) * ATT_SCALE)
                dvts.append(jnp.where(low, ve[0], 0.0) + pltpu.roll(jnp.where(low, 0.0, ve[1]), 64, 1)
                            + jnp.where(low, 0.0, vo[1]) + pltpu.roll(jnp.where(low, vo[0], 0.0), 64, 1))
            return dkts, dvts, dsink

        @pl.when(n < steps)
        def _():
            (dk0, dv0, ds0), (dk1, dv1, ds1) = one_block(0), one_block(1)
            first, second = slice(0, AB), slice(AB, 2 * AB)
            for tl in range(2):
                for cols, g0, g1 in ((slice(tl * 128, (tl + 1) * 128), dk0[tl], dk1[tl]),
                                     (slice(256 + tl * 128, 256 + (tl + 1) * 128), dv0[tl], dv1[tl])):
                    dkv_ref[first, cols] = _bf(carry[first, cols])
                    dkv_ref[second, cols] = _bf(carry[second, cols] + g0[first])
                    carry[first, cols] = g0[second] + g1[first]
                    carry[second, cols] = g1[second]
            dsink_ref[...] += ds0 + ds1

        @pl.when(n == steps)
        def _():
            dkv_ref[...] = _bf(carry[...])

    cur = lambda n: jnp.minimum(n, steps - 1)
    last = lambda n: jnp.clip(n - 1, 0, steps - 1)
    prev = lambda n: jnp.clip(nsub * n - 1, 0, nb - 1)
    return pl.pallas_call(
        body, grid=(steps + 1,),
        in_specs=[pl.BlockSpec((step, D), lambda n: (cur(n), C_AQ // D)),
                  pl.BlockSpec((step, 256), lambda n: (cur(n), C_AK // 256)),
                  pl.BlockSpec((AB, 256), lambda n: (prev(n), C_AK // 256)),
                  pl.BlockSpec((step, 256), lambda n: (cur(n), C_AV // 256)),
                  pl.BlockSpec((AB, 256), lambda n: (prev(n), C_AV // 256)),
                  pl.BlockSpec((step, D), lambda n: (cur(n), 0)),
                  pl.BlockSpec(memory_space=pltpu.SMEM)],
        out_specs=[pl.BlockSpec((step, D), lambda n: (cur(n), 0)),
                   pl.BlockSpec((step, 512), lambda n: (last(n), 0)),
                   pl.BlockSpec((8, 128), lambda n: (0, 0))],
        out_shape=[jax.ShapeDtypeStruct((t, D), BF16), jax.ShapeDtypeStruct((t, 512), BF16),
                   jax.ShapeDtypeStruct((8, 128), F32)],
        scratch_shapes=[pltpu.VMEM((step, 512), F32)],
        name="attn_bwd", compiler_params=_cp(40),
    )(proj, proj, proj, proj, proj, do_a, sinks)


def _silu_and_grad(v):
    s = _sigmoid(v)
    return v * s, s * (1.0 + v * (1.0 - s))


def _tail(o_raw, o_a, proj, x2d, tgt, wbh, wba, wout, hnw, fnw, tb):
    t = x2d.shape[0]

    def body(or_ref, oa_ref, hg_ref, ag0, ag1, mh0, mh1, ma0, ma1, x_ref, t_ref, wbh_ref, wba_ref, wout_ref, hnw_ref,
             fnw_ref, dx2_ref, dor_ref, doa_ref, dhg_ref, dagm_ref, gh_ref, ga_ref, mg_ref, dyh_ref, dya_ref, dx2b_ref,
             sums_ref):
        @pl.when(pl.program_id(0) == 0)
        def _():
            sums_ref[...] = jnp.zeros_like(sums_ref)

        halves = lambda a, b: jnp.concatenate([a[...], b[...]], axis=1)
        hnw_v = hnw_ref[...]
        fnw_v = fnw_ref[...]
        o = or_ref[...]
        rs, xhs = [], []
        for h in range(HEADS):
            oh = o[:, h * HD:(h + 1) * HD]
            r = lax.rsqrt(jnp.mean(oh * oh, axis=-1, keepdims=True) + EPS)
            rs.append(r)
            xhs.append(oh * r)
        xh = jnp.concatenate(xhs, axis=1)
        on = xh * hnw_v
        sil_hg, dsil_hg = _silu_and_grad(hg_ref[...])
        gh_b = _bf(on * sil_hg)
        y_h = _mm(gh_b, wbh_ref[...])
        oa = oa_ref[...]
        sil_ag, dsil_ag = _silu_and_grad(halves(ag0, ag1))
        ga_b = _bf(oa * sil_ag)
        y_a = _mm(ga_b, wba_ref[...])
        s_mh = _sigmoid(halves(mh0, mh1))
        s_ma = _sigmoid(halves(ma0, ma1))
        mg_b = _bf(s_mh * y_h + s_ma * y_a)
        x2 = x_ref[...] + _mm(mg_b, wout_ref[...])
        r2 = lax.rsqrt(jnp.mean(x2 * x2, axis=-1, keepdims=True) + EPS)
        xh2 = x2 * r2
        err = xh2 * fnw_v - t_ref[...]
        loss = 0.5 * jnp.sum(jnp.mean(err * err, axis=-1, keepdims=True))
        dy = err * (1.0 / D)
        dfnw = jnp.sum(dy * xh2, axis=0, keepdims=True)
        dxh2 = dy * fnw_v
        dx2 = r2 * (dxh2 - xh2 * jnp.mean(dxh2 * xh2, axis=-1, keepdims=True))
        dx2_ref[...] = dx2
        dx2_b = _bf(dx2)
        dmg = _mm_nt(dx2_b, wout_ref[...])
        dmg_h = dmg * s_mh
        dmg_a = dmg * s_ma
        dyh_b = _bf(dmg_h)
        dya_b = _bf(dmg_a)
        dagm_ref[:, D:2 * D] = _bf(dmg_h * y_h * (1.0 - s_mh))
        dagm_ref[:, 2 * D:3 * D] = _bf(dmg_a * y_a * (1.0 - s_ma))
        dgh = _mm_nt(dyh_b, wbh_ref[...])
        dga = _mm_nt(dya_b, wba_ref[...])
        doa_ref[...] = dga * sil_ag
        dagm_ref[:, 0:D] = _bf(dga * oa * dsil_ag)
        dhg_ref[...] = _bf(dgh * on * dsil_hg)
        don = dgh * sil_hg
        dhnw = jnp.sum(don * xh, axis=0, keepdims=True)
        dxh = don * hnw_v
        dos = []
        for h in range(HEADS):
            sl = slice(h * HD, (h + 1) * HD)
            dos.append(rs[h] * (dxh[:, sl] - xhs[h] * jnp.mean(dxh[:, sl] * xhs[h], axis=-1, keepdims=True)))
        dor_ref[...] = jnp.concatenate(dos, axis=1)
        gh_ref[...] = gh_b
        ga_ref[...] = ga_b
        mg_ref[...] = mg_b
        dyh_ref[...] = dyh_b
        dya_ref[...] = dya_b
        dx2b_ref[...] = dx2_b
        row = lax.broadcasted_iota(jnp.int32, (8, D), 0)
        sums_ref[...] += jnp.where(row == 0, dfnw, 0.0) + jnp.where(row == 1, dhnw, 0.0) + jnp.where(row == 2, loss, 0.0)

    rowblk = lambda c: pl.BlockSpec((tb, D), lambda i: (i, c))
    half = lambda c: pl.BlockSpec((tb, 512), lambda i: (i, c))
    full = lambda shape: pl.BlockSpec(shape, lambda i: (0, 0))
    return pl.pallas_call(
        body, grid=(t // tb,),
        in_specs=[rowblk(0), rowblk(0), rowblk(C_HG // D), half(C_AG // 512), half(C_AG // 512 + 1), half(C_MH // 512),
                  half(C_MH // 512 + 1), half(C_MA // 512), half(C_MA // 512 + 1), rowblk(0), rowblk(0),
                  full((D, D)), full((D, D)), full((D, D)), full((1, D)), full((1, D))],
        out_specs=[rowblk(0), rowblk(0), rowblk(0), rowblk(0), pl.BlockSpec((tb, 3 * D), lambda i: (i, 0))]
        + [rowblk(0)] * 6 + [full((8, D))],
        out_shape=[jax.ShapeDtypeStruct((t, D), F32)] * 3
        + [jax.ShapeDtypeStruct((t, D), BF16), jax.ShapeDtypeStruct((t, 3 * D), BF16)]
        + [jax.ShapeDtypeStruct((t, D), BF16)] * 6 + [jax.ShapeDtypeStruct((8, D), F32)],
        name="tail", compiler_params=_cp(56),
    )(o_raw, o_a, proj, proj, proj, proj, proj, proj, proj, x2d, tgt, wbh, wba, wout, hnw, fnw)


def _wgrad3(gh, dyh, ga, dya, mg, dx2b, tk):
    t = dyh.shape[0]

    def body(a0, b0, a1, b1, a2, b2, o0, o1, o2):
        @pl.when(pl.program_id(0) == 0)
        def _():
            o0[...] = jnp.zeros_like(o0)
            o1[...] = jnp.zeros_like(o1)
            o2[...] = jnp.zeros_like(o2)

        o0[...] += _mm_tn(a0[...], b0[...])
        o1[...] += _mm_tn(a1[...], b1[...])
        o2[...] += _mm_tn(a2[...], b2[...])

    blk = pl.BlockSpec((tk, D), lambda k: (k, 0))
    out = pl.BlockSpec((D, D), lambda k: (0, 0))
    return pl.pallas_call(
        body, grid=(t // tk,), in_specs=[blk] * 6, out_specs=[out] * 3,
        out_shape=[jax.ShapeDtypeStruct((D, D), F32)] * 3,
        name="wgrad3", compiler_params=_cp(48),
    )(gh, dyh, ga, dya, mg, dx2b)


def _inproj_wgrad(xnt, pieces, nb, name):
    t = xnt.shape[1]
    counts = [p.shape[1] // nb for p in pieces]
    firsts = [sum(counts[:k]) for k in range(len(pieces))]

    def body(xnt_ref, *refs):
        o_ref = refs[-1]
        j = pl.program_id(0)
        for first, count, p_ref in zip(firsts, counts, refs[:-1]):
            @pl.when((j >= first) & (j < first + count))
            def _(p_ref=p_ref):
                o_ref[...] = _mm(xnt_ref[...], p_ref[...])

    def piece_spec(first, count):
        return pl.BlockSpec((t, nb), lambda j: (0, jnp.clip(j - first, 0, count - 1)))

    return pl.pallas_call(
        body, grid=(sum(counts),),
        in_specs=[pl.BlockSpec((D, t), lambda j: (0, 0), pipeline_mode=pl.Buffered(1))]
        + [piece_spec(f, c) for f, c in zip(firsts, counts)],
        out_specs=pl.BlockSpec((D, nb), lambda j: (0, j)),
        out_shape=jax.ShapeDtypeStruct((D, sum(counts) * nb), F32),
        name=name, compiler_params=_cp(56),
    )(xnt, *pieces)


def _inproj_dgrad(pieces, w_p, x2d, dx2, norm_w, tb, after):
    t = x2d.shape[0]

    def body(*refs):
        piece_refs = refs[:len(pieces)]
        w_ref, x_ref, dx2_ref, nw_ref, _, gx_ref, dnw_ref = refs[len(pieces):]

        @pl.when(pl.program_id(0) == 0)
        def _():
            dnw_ref[...] = jnp.zeros_like(dnw_ref)

        dxn = None
        off = 0
        for p in piece_refs:
            width = p.shape[1]
            for q in range(w_ref.shape[0]):
                lo, hi = max(off, q * PAIR), min(off + width, (q + 1) * PAIR)
                if lo < hi:
                    term = _mm_nt(p[:, lo - off:hi - off], w_ref[q, :, lo - q * PAIR:hi - q * PAIR])
                    dxn = term if dxn is None else dxn + term
            off += width
        xv = x_ref[...]
        r = lax.rsqrt(jnp.mean(xv * xv, axis=-1, keepdims=True) + EPS)
        xh = xv * r
        dxh = dxn * nw_ref[...]
        gx_ref[...] = dx2_ref[...] + r * (dxh - xh * jnp.mean(dxh * xh, axis=-1, keepdims=True))
        row0 = lax.broadcasted_iota(jnp.int32, (8, D), 0) == 0
        dnw_ref[...] += jnp.where(row0, jnp.sum(dxn * xh, axis=0, keepdims=True), 0.0)

    rowblk = pl.BlockSpec((tb, D), lambda i: (i, 0))
    return pl.pallas_call(
        body, grid=(t // tb,),
        in_specs=[pl.BlockSpec((tb, p.shape[1]), lambda i: (i, 0)) for p in pieces]
        + [pl.BlockSpec(w_p.shape, lambda i: (0, 0, 0), pipeline_mode=pl.Buffered(1)), rowblk, rowblk,
           pl.BlockSpec((1, D), lambda i: (0, 0)), pl.BlockSpec(memory_space=pl.ANY)],
        out_specs=[rowblk, pl.BlockSpec((8, D), lambda i: (0, 0))],
        out_shape=[jax.ShapeDtypeStruct((t, D), F32), jax.ShapeDtypeStruct((8, D), F32)],
        name="inproj_dgrad", compiler_params=_cp(60),
    )(*pieces, w_p, x2d, dx2, norm_w, after)


def _adamw_math(w, g, m, v):
    m = B1 * m + (1.0 - B1) * g
    v = B2 * v + (1.0 - B2) * (g * g)
    m_hat = m / (1.0 - B1 ** STEP)
    v_hat = v / (1.0 - B2 ** STEP)
    delta = -LR * (m_hat / (jnp.sqrt(v_hat) + ADAM_EPS) + WD * w)
    return delta, m, v


def _adamw_shard(recv, sums, chip, w, m, v, rows, name):
    nparts, nr, nc = recv.shape

    def body(chip_ref, own_ref, p_ref, w_ref, m_ref, v_ref, g_ref, d_ref, nm_ref, nv_ref):
        g = own_ref[0].astype(F32)
        for s in range(nparts):
            g = g + p_ref[s].astype(F32)
        d, nm, nv = _adamw_math(w_ref[...], g, m_ref[...], v_ref[...])
        g_ref[...] = g
        d_ref[...] = d
        nm_ref[...] = nm
        nv_ref[...] = nv

    blk = pl.BlockSpec((rows, nc), lambda i, chip_ref: (i, 0))
    return pl.pallas_call(
        body,
        grid_spec=pltpu.PrefetchScalarGridSpec(
            num_scalar_prefetch=1, grid=(nr // rows,),
            in_specs=[pl.BlockSpec((1, rows, nc), lambda i, chip_ref: (chip_ref[0], i, 0)),
                      pl.BlockSpec((nparts, rows, nc), lambda i, chip_ref: (0, i, 0)), blk, blk, blk],
            out_specs=[blk] * 4),
        out_shape=[jax.ShapeDtypeStruct((nr, nc), F32)] * 4,
        name=name, compiler_params=_cp(48),
    )(chip, sums, recv, w, m, v)


def _adamw_sum8(parts, w, m, v, after, name):
    def body(p_ref, w_ref, m_ref, v_ref, _, g_ref, d_ref, nm_ref, nv_ref):
        g = p_ref[0].astype(F32)
        for s in range(1, NDEV):
            g = g + p_ref[s].astype(F32)
        d, nm, nv = _adamw_math(w_ref[...], g, m_ref[...], v_ref[...])
        g_ref[...] = g
        d_ref[...] = d
        nm_ref[...] = nm
        nv_ref[...] = nv

    vm = pl.BlockSpec(memory_space=pltpu.VMEM)
    return pl.pallas_call(
        body, out_shape=[jax.ShapeDtypeStruct(w.shape, F32)] * 4,
        in_specs=[vm, vm, vm, vm, pl.BlockSpec(memory_space=pl.ANY)], out_specs=[vm] * 4, name=name,
    )(parts, w, m, v, after)


SMALL_ROWS = dict(norm_w=0, lower_bound=1, hgrn_norm_w=3, final_norm_w=4, sinks=5, loss=6)


def _pack_small_grads(dnw, dlb, sums, dsink):
    def body(dnw_ref, dlb_ref, sums_ref, dsink_ref, o_ref):
        o_ref[...] = jnp.zeros_like(o_ref)
        o_ref[0:1, :] = dnw_ref[0:1, :]
        o_ref[1:2, :] = dlb_ref[0:1, :]
        o_ref[3:4, :] = sums_ref[1:2, :]
        o_ref[4:5, :] = sums_ref[0:1, :]
        o_ref[5:6, 0:128] = dsink_ref[0:1, :]
        o_ref[6:7, :] = sums_ref[2:3, :]

    return pl.pallas_call(body, out_shape=jax.ShapeDtypeStruct((8, D), F32), name="pack_small_grads")(dnw, dlb, sums, dsink)


def _adamw_small(parts, ws, ms, vs):
    shapes = [a.shape for a in ws]

    def body(p_ref, *refs):
        w, m, v = refs[0:5], refs[5:10], refs[10:15]
        outs = [refs[15 + 5 * i:20 + 5 * i] for i in range(4)]
        loss_ref = refs[35]

        def total(row, width):
            g = p_ref[0, row:row + 1, 0:width]
            for s in range(1, NDEV):
                g = g + p_ref[s, row:row + 1, 0:width]
            return g

        loss_ref[...] = total(6, 128)
        lb = _lower_bound(w[1])
        ga0 = total(1, D) * lb * (1.0 - lb)
        grads = [total(0, D), None, total(3, D), total(4, D), total(5, QH)]
        for i in (0, 2, 3, 4):
            res = (grads[i],) + _adamw_math(w[i][...], grads[i], m[i][...], v[i][...])
            for o, val in zip(outs, res):
                o[i][...] = val
        for r, g in ((0, ga0), (1, -ga0)):
            res = (g,) + _adamw_math(w[1][r:r + 1, :], g, m[1][r:r + 1, :], v[1][r:r + 1, :])
            for o, val in zip(outs, res):
                o[1][r:r + 1, :] = val

    res = pl.pallas_call(
        body, out_shape=[jax.ShapeDtypeStruct(s, F32) for s in shapes] * 4 + [jax.ShapeDtypeStruct((1, 128), F32)],
        name="adamw_small",
    )(parts, *ws, *ms, *vs)
    return [res[5 * i:5 * i + 5] for i in range(4)], res[20][0, 0]


def kernel(x, norm_w, w_in, hgrn_lower_bound, hgrn_norm_w, w_branch_hgrn, attn_sinks, w_branch_attn, w_out, final_norm_w, loss_target, m_norm_w, m_w_in, m_hgrn_lower_bound, m_hgrn_norm_w, m_w_branch_hgrn, m_attn_sinks, m_w_branch_attn, m_w_out, m_final_norm_w, v_norm_w, v_w_in, v_hgrn_lower_bound, v_hgrn_norm_w, v_w_branch_hgrn, v_attn_sinks, v_w_branch_attn, v_w_out, v_final_norm_w):
    t = x.shape[1]
    x2d = x.reshape(t, D)
    tgt = loss_target.reshape(t, D)
    fnw = final_norm_w.reshape(1, D)
    row_blk = min(256, t)
    big_blk = min(512, t)

    chip = (2 * lax.axis_index("x") + lax.axis_index("y")).astype(jnp.int32).reshape(1)
    w_p, xn, xnt, proj_own = _gather_in_projection(w_in[0], x2d, norm_w)
    wbh, wba, wout = (g.reshape(D, D) for g in _gather_square(
        [w_branch_hgrn[0].astype(BF16), w_branch_attn[0].astype(BF16), w_out[0].astype(BF16)], after=w_p))

    proj = _inproj_fwd(xn, w_p, proj_own, chip, min(1024, t))
    o_raw, states = _hgrn_fwd(proj, hgrn_lower_bound, big_blk, HGRN_GROUP)
    o_a = _attn_fwd(proj, attn_sinks)
    (dx2, do_raw, do_a, d_hg, d_agm, gh, ga, mg, dyh, dya, dx2b, sums) = _tail(
        o_raw, o_a, proj, x2d, tgt, wbh, wba, wout, hgrn_norm_w, fnw, row_blk)
    dwbh, dwba, dwout = _wgrad3(gh, dyh, ga, dya, mg, dx2b, big_blk)
    d_aq, d_kv, dsink = _attn_bwd(proj, attn_sinks, do_a)
    d_hgrn, dlb = _hgrn_bwd(proj, hgrn_lower_bound, do_raw, states, big_blk, HGRN_GROUP)
    pieces = (d_hgrn, d_hg, d_aq, d_kv, d_agm)
    dw_pieces = [_inproj_wgrad(xnt, [d_hgrn], CB, "inproj_wgrad_hgrn"),
                 _inproj_wgrad(xnt, [d_hg, d_aq, d_kv], CB // 2, "inproj_wgrad_mid"),
                 _inproj_wgrad(xnt, [d_agm], CB, "inproj_wgrad_gates")]

    dwin_r = jnp.concatenate(dw_pieces, axis=1).reshape(D, NDEV, IN_SHARD).transpose(1, 0, 2).astype(BF16)
    slots = lambda a: a.reshape(NDEV, ROW_SHARD, D).astype(BF16)
    got = _exchange_pair([dwin_r])
    core = lax.axis_index("c").astype(jnp.int32).reshape(1)
    s_in = _pair_sum(dwin_r, got[0], core, 4 * ROW_SHARD, "pair_sum_w_in")
    rin, = _exchange_chips([s_in])
    rbh, rba, rout = _exchange_square([slots(dwbh), slots(dwba), slots(dwout)])
    grad_x, dnw = _inproj_dgrad(pieces, w_p, x2d, dx2, norm_w, big_blk, after=s_in)
    rsm = _exchange_small(_pack_small_grads(dnw, dlb, sums, dsink))
    g_in, d_in, nm_in, nv_in = _adamw_shard(rin, s_in, chip, w_in[0], m_w_in[0], v_w_in[0], 128, "adamw_w_in")
    g_bh, d_bh, nm_bh, nv_bh = _adamw_sum8(
        rbh, w_branch_hgrn[0], m_w_branch_hgrn[0], v_w_branch_hgrn[0], dnw, "adamw_w_bh")
    g_ba, d_ba, nm_ba, nv_ba = _adamw_sum8(
        rba, w_branch_attn[0], m_w_branch_attn[0], v_w_branch_attn[0], dnw, "adamw_w_ba")
    g_out, d_out, nm_out, nv_out = _adamw_sum8(rout, w_out[0], m_w_out[0], v_w_out[0], dnw, "adamw_w_out")
    (sg, sd, sm, sv), loss = _adamw_small(
        rsm,
        (norm_w, hgrn_lower_bound, hgrn_norm_w, fnw, attn_sinks),
        (m_norm_w, m_hgrn_lower_bound, m_hgrn_norm_w, m_final_norm_w.reshape(1, D), m_attn_sinks),
        (v_norm_w, v_hgrn_lower_bound, v_hgrn_norm_w, v_final_norm_w.reshape(1, D), v_attn_sinks))

    def group(s, w_in_v, bh, ba, out):
        nw, lb, hnw, fn, sinks = s
        return (nw, w_in_v[None], lb, hnw, bh[None], sinks, ba[None], out[None], fn.reshape(D))

    return (loss, grad_x.reshape(1, t, D),
            *group(sg, g_in, g_bh, g_ba, g_out), *group(sd, d_in, d_bh, d_ba, d_out),
            *group(sm, nm_in, nm_bh, nm_ba, nm_out), *group(sv, nv_in, nv_bh, nv_ba, nv_out))
```

```python
import jax
import jax.numpy as jnp
from jax import lax
from jax.experimental import pallas as pl
from jax.experimental.pallas import tpu as pltpu
from jax.experimental.pallas import tpu_sc as plsc

F32 = jnp.float32
BF16 = jnp.bfloat16

D = 1024
DIN = 8704
NDEV = 8
IN_SHARD = DIN // NDEV
PAIR = 2 * IN_SHARD
ROW_SHARD = D // NDEV
HEADS = 8
HD = 128
CH = 64
HGRN_GROUP = 8
QH = 16
AB = 128
EPS = 1e-6
NEG = -1e30
ATT_SCALE = 0.125

C_HG = 3072
C_AQ = 4096
C_AK = 5120
C_AV = 5376
C_AG = 5632
C_MH = 6656
C_MA = 7680
CB = 512

LR = 0.001
B1 = 0.9
B2 = 0.999
ADAM_EPS = 1e-08
WD = 0.01
STEP = 10

MESH = pl.DeviceIdType.MESH


def _cp(vmem_mb):
    return pltpu.CompilerParams(vmem_limit_bytes=vmem_mb * 1024 * 1024)


def _mm(a, b):
    return jnp.dot(a, b, preferred_element_type=F32)


def _mm_nt(a, b):
    return lax.dot_general(a, b, (((1,), (1,)), ((), ())), preferred_element_type=F32)


def _mm_tn(a, b):
    return lax.dot_general(a, b, (((0,), (0,)), ((), ())), preferred_element_type=F32)


def _tri3(lower):
    r = lax.broadcasted_iota(jnp.int32, (CH, 3 * CH), 0)
    c = lax.broadcasted_iota(jnp.int32, (CH, 3 * CH), 1)
    c = jnp.where(c >= 2 * CH, c - 2 * CH, jnp.where(c >= CH, c - CH, c))
    return ((r >= c) if lower else (c >= r)).astype(BF16)


def _mm_tri_exact(tri3, g):
    g1 = g.astype(BF16)
    r1 = g - g1.astype(F32)
    g2 = r1.astype(BF16)
    g3 = (r1 - g2.astype(F32)).astype(BF16)
    return _mm(tri3, jnp.concatenate([g1, g2, g3], axis=0))


def _sigmoid(v):
    return 0.5 * jnp.tanh(0.5 * v) + 0.5


def _bf(v):
    return v.astype(BF16)


def _place():
    x, y, c = lax.axis_index("x"), lax.axis_index("y"), lax.axis_index("c")
    return (x, y, c), (x, y, 1 - c), [(1 - x, y), (x, 1 - y), (1 - x, 1 - y)]


def _dev_index(px, py, pc):
    return 4 * px + 2 * py + pc


def _gather_in_projection(w_in_s, x2d, norm_w):
    half = D // 2
    t = x2d.shape[0]
    prep_rows = min(512, t)
    nprep = t // prep_rows

    def body(win_ref, x_hbm, nw_ref, wp_g, xn_hbm, xnt_hbm, proj_hbm, give, take, mine, xbuf, xnbuf, xntbuf, w_own, pbuf,
             send_sems, recv_sems, loc_sem, swap_sems, in_sems, out_sems, own_sem):
        (x, y, c), sibling, chips = _place()
        give[...] = win_ref[pl.ds(pl.multiple_of(half * (1 - c), half), half), :].astype(BF16)
        swap = pltpu.make_async_remote_copy(src_ref=give, dst_ref=take, send_sem=swap_sems.at[0], recv_sem=swap_sems.at[1],
                                            device_id=sibling, device_id_type=MESH)
        swap.start()
        swap.wait()
        own = win_ref[pl.ds(pl.multiple_of(half * c, half), half), :]
        other = take[...].astype(F32)
        mine[...] = jnp.where(c == 0, jnp.concatenate([own, other], axis=1),
                              jnp.concatenate([other, own], axis=1)).astype(BF16)

        def place(px, py, pc):
            return wp_g.at[2 * px + py, pl.ds(pl.multiple_of(half * pc, half), half), :]

        def copy(kind, origin, to, src=mine):
            return pltpu.make_async_remote_copy(
                src_ref=src, dst_ref=place(*origin), send_sem=send_sems.at[kind], recv_sem=recv_sems.at[kind],
                device_id=to, device_id_type=MESH)

        me = (x, y, c)
        local = pltpu.make_async_copy(mine, place(*me), loc_sem)
        local.start()
        first = [copy(0, me, sibling)] + [copy(1 + j, me, (*chip, c)) for j, chip in enumerate(chips)]
        for cp in first:
            cp.start()

        copy(0, (x, y, 1 - c), me).wait_recv()
        local.wait()
        my_chip = 2 * x + y
        fetch = pltpu.make_async_copy(wp_g.at[my_chip], w_own, own_sem)
        fetch.start()

        def rows_of(i):
            return pl.ds(pl.multiple_of(i * prep_rows, prep_rows), prep_rows)

        def load(i, slot):
            return pltpu.make_async_copy(x_hbm.at[rows_of(i), :], xbuf.at[slot], in_sems.at[slot])

        def stores(i, slot):
            own_cols = pl.ds(pl.multiple_of(my_chip * PAIR, 128), PAIR)
            return (pltpu.make_async_copy(xnbuf.at[slot], xn_hbm.at[rows_of(i), :], out_sems.at[slot, 0]),
                    pltpu.make_async_copy(xntbuf.at[slot], xnt_hbm.at[:, rows_of(i)], out_sems.at[slot, 1]),
                    pltpu.make_async_copy(pbuf.at[slot], proj_hbm.at[rows_of(i), own_cols], out_sems.at[slot, 2]))

        load(0, 0).start()
        fetch.wait()

        def prep(i, carry):
            slot = lax.rem(i, 2)
            load(i, slot).wait()

            @pl.when(i + 1 < nprep)
            def _():
                load(i + 1, 1 - slot).start()

            @pl.when(i >= 2)
            def _():
                for cp in stores(i - 2, slot):
                    cp.wait()

            xv = xbuf[slot]
            xn = (xv * lax.rsqrt(jnp.mean(xv * xv, axis=-1, keepdims=True) + EPS)) * nw_ref[...]
            xn_b = xn.astype(BF16)
            xnbuf[slot] = xn_b
            xntbuf[slot] = xn.T.astype(BF16)
            pbuf[slot] = _mm(xn_b, w_own[...])
            for cp in stores(i, slot):
                cp.start()
            return carry

        lax.fori_loop(0, nprep, prep, 0)
        for i in range(max(nprep - 2, 0), nprep):
            for cp in stores(i, i % 2):
                cp.wait()

        passed = []
        for j, chip in enumerate(chips):
            copy(1 + j, (*chip, c), me).wait_recv()
            cp = copy(4 + j, (*chip, c), sibling, src=place(*chip, c))
            cp.start()
            passed.append(cp)
        for j, chip in enumerate(chips):
            copy(4 + j, (*chip, 1 - c), me).wait_recv()
        for cp in first + passed:
            cp.wait_send()

    vm = pl.BlockSpec(memory_space=pltpu.VMEM)
    hbm = pl.BlockSpec(memory_space=pl.ANY)
    return pl.pallas_call(
        body,
        out_shape=[jax.ShapeDtypeStruct((NDEV // 2, D, PAIR), BF16), jax.ShapeDtypeStruct((t, D), BF16),
                   jax.ShapeDtypeStruct((D, t), BF16), jax.ShapeDtypeStruct((t, DIN), F32)],
        in_specs=[vm, hbm, vm],
        out_specs=[hbm, hbm, hbm, hbm],
        scratch_shapes=[pltpu.VMEM((half, IN_SHARD), BF16), pltpu.VMEM((half, IN_SHARD), BF16),
                        pltpu.VMEM((half, PAIR), BF16),
                        pltpu.VMEM((2, prep_rows, D), F32), pltpu.VMEM((2, prep_rows, D), BF16),
                        pltpu.VMEM((2, D, prep_rows), BF16),
                        pltpu.VMEM((D, PAIR), BF16), pltpu.VMEM((2, prep_rows, PAIR), F32),
                        pltpu.SemaphoreType.DMA((NDEV - 1,)), pltpu.SemaphoreType.DMA((NDEV - 1,)),
                        pltpu.SemaphoreType.DMA, pltpu.SemaphoreType.DMA((2,)),
                        pltpu.SemaphoreType.DMA((2,)), pltpu.SemaphoreType.DMA((2, 3)), pltpu.SemaphoreType.DMA],
        name="gather_in_projection", compiler_params=_cp(56),
    )(w_in_s, x2d, norm_w)


def _gather_square(shards, after):
    n = len(shards)

    def launch(*refs):
        ins, outs = refs[:n], refs[n + 1:2 * n + 1]
        send_sems, recv_sems, loc_sems = refs[2 * n + 1:]
        (x, y, c), _, _ = _place()
        me = _dev_index(x, y, c)
        peers = [(1 - x if r & 4 else x, 1 - y if r & 2 else y, 1 - c if r & 1 else c) for r in range(1, NDEV)]
        barrier = pltpu.get_barrier_semaphore()
        for peer in peers:
            pl.semaphore_signal(barrier, inc=1, device_id=peer, device_id_type=MESH)
        pl.semaphore_wait(barrier, NDEV - 1)
        local = [pltpu.make_async_copy(ins[k], outs[k].at[me], loc_sems.at[k]) for k in range(n)]
        copies = [pltpu.make_async_remote_copy(
            src_ref=ins[k], dst_ref=outs[k].at[me], send_sem=send_sems.at[r, k], recv_sem=recv_sems.at[r, k],
            device_id=peer, device_id_type=MESH) for r, peer in enumerate(peers) for k in range(n)]
        for cp in local + copies:
            cp.start()
        for r, peer in enumerate(peers):
            for k in range(n):
                pltpu.make_async_remote_copy(
                    src_ref=ins[k], dst_ref=outs[k].at[_dev_index(*peer)], send_sem=send_sems.at[r, k],
                    recv_sem=recv_sems.at[r, k], device_id=peer, device_id_type=MESH).wait_recv()
        for cp in copies:
            cp.wait_send()
        for cp in local:
            cp.wait()

    return pl.kernel(
        launch, out_type=[jax.ShapeDtypeStruct((NDEV,) + a.shape, a.dtype) for a in shards],
        mesh=plsc.ScalarSubcoreMesh(axis_name="sequencer", num_cores=1), name="gather_square",
        scratch_types=(pltpu.SemaphoreType.DMA((NDEV - 1, n)), pltpu.SemaphoreType.DMA((NDEV - 1, n)),
                       pltpu.SemaphoreType.DMA((n,))),
        compiler_params=pltpu.CompilerParams(collective_id=2),
    )(*shards, after)


def _exchange_pair(arrs, after):
    n = len(arrs)

    def launch(*refs):
        ins, got = refs[:n], refs[n + 1:2 * n + 1]
        send_sems, recv_sems = refs[2 * n + 1:]
        (x, y, c), sibling, _ = _place()
        barrier = pltpu.get_barrier_semaphore()
        pl.semaphore_signal(barrier, inc=1, device_id=sibling, device_id_type=MESH)
        pl.semaphore_wait(barrier, 1)
        sends = [pltpu.make_async_remote_copy(
            src_ref=ins[k].at[_dev_index(q // 2, q % 2, 1 - c)], dst_ref=got[k].at[q], send_sem=send_sems.at[q, k],
            recv_sem=recv_sems.at[q, k], device_id=sibling, device_id_type=MESH) for q in range(4) for k in range(n)]
        for cp in sends:
            cp.start()
        for cp in sends:
            cp.wait_recv()
        for cp in sends:
            cp.wait_send()

    return pl.kernel(
        launch, out_type=[jax.ShapeDtypeStruct((4,) + a.shape[1:], a.dtype) for a in arrs],
        mesh=plsc.ScalarSubcoreMesh(axis_name="sequencer", num_cores=1), name="exchange_pair",
        scratch_types=(pltpu.SemaphoreType.DMA((4, n)), pltpu.SemaphoreType.DMA((4, n))),
        compiler_params=pltpu.CompilerParams(collective_id=0),
    )(*arrs, after)


def _pair_sum(full, got, core, rows, name):
    _, nr, nc = got.shape

    def body(core_ref, a_ref, b_ref, o_ref):
        o_ref[...] = (a_ref[...].astype(F32) + b_ref[...].astype(F32)).astype(BF16)

    blk = pl.BlockSpec((1, rows, nc), lambda q, i, core_ref: (q, i, 0))
    return pl.pallas_call(
        body,
        grid_spec=pltpu.PrefetchScalarGridSpec(
            num_scalar_prefetch=1, grid=(4, nr // rows),
            in_specs=[pl.BlockSpec((1, rows, nc), lambda q, i, core_ref: (2 * q + core_ref[0], i, 0)), blk],
            out_specs=blk),
        out_shape=jax.ShapeDtypeStruct(got.shape, BF16), name=name,
    )(core, full, got)


def _exchange_chips(sums):
    n = len(sums)

    def launch(*refs):
        ins, outs = refs[:n], refs[n:2 * n]
        send_sems, recv_sems = refs[2 * n:]
        (x, y, c), _, chips = _place()
        barrier = pltpu.get_barrier_semaphore()
        for px, py in chips:
            pl.semaphore_signal(barrier, inc=1, device_id=(px, py, c), device_id_type=MESH)
        pl.semaphore_wait(barrier, len(chips))
        copies = [pltpu.make_async_remote_copy(
            src_ref=ins[k].at[2 * px + py], dst_ref=outs[k].at[j], send_sem=send_sems.at[j, k],
            recv_sem=recv_sems.at[j, k], device_id=(px, py, c), device_id_type=MESH)
            for j, (px, py) in enumerate(chips) for k in range(n)]
        for cp in copies:
            cp.start()
        for cp in copies:
            cp.wait_recv()
        for cp in copies:
            cp.wait_send()

    return pl.kernel(
        launch, out_type=[jax.ShapeDtypeStruct((3,) + a.shape[1:], a.dtype) for a in sums],
        mesh=plsc.ScalarSubcoreMesh(axis_name="sequencer", num_cores=1), name="exchange_chips",
        scratch_types=(pltpu.SemaphoreType.DMA((3, n)), pltpu.SemaphoreType.DMA((3, n))),
        compiler_params=pltpu.CompilerParams(collective_id=1),
    )(*sums)


def _exchange_square(partials):
    n = len(partials)

    def launch(*refs):
        ins, outs = refs[:n], refs[n:2 * n]
        send_sems, recv_sems, loc_sems = refs[2 * n:]
        (x, y, c), _, _ = _place()
        me = _dev_index(x, y, c)
        peers = [(1 - x if r & 4 else x, 1 - y if r & 2 else y, 1 - c if r & 1 else c) for r in range(1, NDEV)]
        barrier = pltpu.get_barrier_semaphore()
        for peer in peers:
            pl.semaphore_signal(barrier, inc=1, device_id=peer, device_id_type=MESH)
        pl.semaphore_wait(barrier, NDEV - 1)
        local = [pltpu.make_async_copy(ins[k].at[me], outs[k].at[me], loc_sems.at[k]) for k in range(n)]
        copies = [pltpu.make_async_remote_copy(
            src_ref=ins[k].at[_dev_index(*peer)], dst_ref=outs[k].at[me], send_sem=send_sems.at[r, k],
            recv_sem=recv_sems.at[r, k], device_id=peer, device_id_type=MESH)
            for r, peer in enumerate(peers) for k in range(n)]
        for cp in local + copies:
            cp.start()
        for r, peer in enumerate(peers):
            for k in range(n):
                pltpu.make_async_remote_copy(
                    src_ref=ins[k].at[me], dst_ref=outs[k].at[_dev_index(*peer)], send_sem=send_sems.at[r, k],
                    recv_sem=recv_sems.at[r, k], device_id=peer, device_id_type=MESH).wait_recv()
        for cp in copies:
            cp.wait_send()
        for cp in local:
            cp.wait()

    return pl.kernel(
        launch, out_type=[jax.ShapeDtypeStruct(a.shape, a.dtype) for a in partials],
        mesh=plsc.ScalarSubcoreMesh(axis_name="sequencer", num_cores=1), name="exchange_square",
        scratch_types=(pltpu.SemaphoreType.DMA((NDEV - 1, n)), pltpu.SemaphoreType.DMA((NDEV - 1, n)),
                       pltpu.SemaphoreType.DMA((n,))),
        compiler_params=pltpu.CompilerParams(collective_id=3),
    )(*partials)


def _exchange_small(small):
    def body(sm_ref, out_ref, send_sems, recv_sems):
        (x, y, c), _, _ = _place()
        me = _dev_index(x, y, c)
        peers = [(1 - x if r & 4 else x, 1 - y if r & 2 else y, 1 - c if r & 1 else c) for r in range(1, NDEV)]
        out_ref[me] = sm_ref[...]
        copies = [pltpu.make_async_remote_copy(
            src_ref=sm_ref, dst_ref=out_ref.at[me], send_sem=send_sems.at[r], recv_sem=recv_sems.at[r],
            device_id=peer, device_id_type=MESH) for r, peer in enumerate(peers)]
        for cp in copies:
            cp.start()
        for r, peer in enumerate(peers):
            pltpu.make_async_remote_copy(
                src_ref=sm_ref, dst_ref=out_ref.at[_dev_index(*peer)], send_sem=send_sems.at[r], recv_sem=recv_sems.at[r],
                device_id=peer, device_id_type=MESH).wait_recv()
        for cp in copies:
            cp.wait_send()

    vm = pl.BlockSpec(memory_space=pltpu.VMEM)
    return pl.pallas_call(
        body, out_shape=jax.ShapeDtypeStruct((NDEV,) + small.shape, F32), in_specs=[vm], out_specs=vm,
        scratch_shapes=[pltpu.SemaphoreType.DMA((NDEV - 1,)), pltpu.SemaphoreType.DMA((NDEV - 1,))],
        name="exchange_small",
    )(small)


def _inproj_fwd(xn, w_pairs, proj, chip, tb):
    t = xn.shape[0]
    nblk, _, nb = w_pairs.shape

    def body(chip_ref, xn_ref, w_ref, proj_in, proj_ref):
        proj_ref[...] = _mm(xn_ref[...], w_ref[0])

    def other(j, chip_ref):
        return j + (j >= chip_ref[0]).astype(jnp.int32)

    return pl.pallas_call(
        body,
        grid_spec=pltpu.PrefetchScalarGridSpec(
            num_scalar_prefetch=1, grid=(t // tb, nblk - 1),
            in_specs=[pl.BlockSpec((tb, D), lambda i, j, chip_ref: (i, 0)),
                      pl.BlockSpec((1, D, nb), lambda i, j, chip_ref: (other(j, chip_ref), 0, 0)),
                      pl.BlockSpec(memory_space=pl.ANY)],
            out_specs=pl.BlockSpec((tb, nb), lambda i, j, chip_ref: (i, other(j, chip_ref)))),
        out_shape=jax.ShapeDtypeStruct((t, DIN), F32),
        input_output_aliases={3: 0},
        name="inproj_fwd", compiler_params=_cp(56),
    )(chip, xn, w_pairs, proj)


def _lower_bound(lb_ref):
    a0 = lb_ref[0:1, :]
    a1 = lb_ref[1:2, :]
    mx = jnp.maximum(a0, a1)
    e0 = jnp.exp(a0 - mx)
    e1 = jnp.exp(a1 - mx)
    return e0 / (e0 + e1)


def _hgrn_chunk_fwd(hq, hf, lb, tril):
    sg = _sigmoid(hf)
    f = lb + (1.0 - lb) * sg
    g = jnp.log(f)
    k = 1.0 - f
    sq = _sigmoid(hq)
    q = hq * sq
    b = _mm_tri_exact(tril, g)
    last_row = lax.broadcasted_iota(jnp.int32, b.shape, 0) == CH - 1
    b_last = jnp.sum(jnp.where(last_row, b, 0.0), axis=0, keepdims=True)
    c = 0.5 * b_last
    eb = jnp.exp(b)
    ea = jnp.exp(b - c)
    ek = jnp.exp(c - b)
    ed = jnp.exp(b_last - b)
    ebl = jnp.exp(b_last)
    return dict(sg=sg, f=f, k=k, sq=sq, q=q, eb=eb, ea=ea, ek=ek, ed=ed, ebl=ebl,
                qe=q * eb, qa=q * ea, ka=k * ek, kd=k * ed)


def _tri(lower):
    r = lax.broadcasted_iota(jnp.int32, (CH, CH), 0)
    c = lax.broadcasted_iota(jnp.int32, (CH, CH), 1)
    return (r >= c) if lower else (c >= r)


def _head_segment(p_ref, rows, j, hg):
    return p_ref[rows, j * HD * hg:(j + 1) * HD * hg]


def _head(a, k):
    return a[:, k * HD:(k + 1) * HD]


def _hgrn_fwd(proj, lbw, rb, hg):
    assert hg == HEADS
    t = proj.shape[0]
    ncb = rb // CH

    def body(p_ref, lb_ref, o_ref, st_ref, s_scr):
        @pl.when(pl.program_id(1) == 0)
        def _():
            s_scr[...] = jnp.zeros_like(s_scr)

        lb = _lower_bound(lb_ref)
        causal = _tri(True)
        tril = _tri3(True)
        heads = range(hg)

        def chunk(cc, carry):
            r0 = pl.multiple_of(cc * CH, CH)
            rows = pl.ds(r0, CH)
            e = _hgrn_chunk_fwd(_head_segment(p_ref, rows, 0, hg), _head_segment(p_ref, rows, 1, hg), lb, tril)
            v = _bf(_head_segment(p_ref, rows, 2, hg))
            sts = [s_scr[k] for k in heads]
            qa, ka, qe, kd = _bf(e["qa"]), _bf(e["ka"]), _bf(e["qe"]), _bf(e["kd"])
            a = [_bf(jnp.where(causal, _mm_nt(_head(qa, k), _head(ka, k)), 0.0)) for k in heads]
            o_inter = [_mm_nt(_head(qe, k), _bf(sts[k])) for k in heads]
            kv = [_mm_tn(_head(v, k), _head(kd, k)) for k in heads]
            o_intra = [_mm(a[k], _head(v, k)) for k in heads]
            for k in heads:
                st_ref[cc, k] = sts[k]
                o_ref[rows, k * HD:(k + 1) * HD] = o_inter[k] + o_intra[k]
                s_scr[k] = sts[k] * _head(e["ebl"], k) + kv[k]
            return carry

        lax.fori_loop(0, ncb, chunk, 0, unroll=4)

    return pl.pallas_call(
        body, grid=(HEADS // hg, t // rb),
        in_specs=[pl.BlockSpec((rb, 3 * HD * hg), lambda h, i: (i, h)), pl.BlockSpec((2, HD * hg), lambda h, i: (0, h))],
        out_specs=[pl.BlockSpec((rb, HD * hg), lambda h, i: (i, h)),
                   pl.BlockSpec((ncb, hg, HD, HD), lambda h, i: (i, h, 0, 0))],
        out_shape=[jax.ShapeDtypeStruct((t, D), F32), jax.ShapeDtypeStruct((t // CH, HEADS, HD, HD), F32)],
        scratch_shapes=[pltpu.VMEM((hg, HD, HD), F32)],
        name="hgrn_fwd", compiler_params=_cp(48),
    )(proj, lbw)


def _hgrn_bwd(proj, lbw, do_raw, states, rb, hg):
    assert hg == HEADS
    t = proj.shape[0]
    nblk = t // rb
    ncb = rb // CH
    wd = HD * hg

    def body(p_ref, lb_ref, do_ref, st_ref, dp_ref, dlb_ref, ds_scr):
        @pl.when(pl.program_id(1) == 0)
        def _():
            ds_scr[...] = jnp.zeros_like(ds_scr)
            dlb_ref[...] = jnp.zeros_like(dlb_ref)

        lb = _lower_bound(lb_ref)
        causal = _tri(True)
        tril = _tri3(True)
        triu = _tri3(False)
        last_row = lax.broadcasted_iota(jnp.int32, (CH, HD * hg), 0) == CH - 1
        row0 = lax.broadcasted_iota(jnp.int32, (8, HD * hg), 0) == 0
        heads = range(hg)
        wide = lambda parts: jnp.concatenate(parts, axis=1)

        def chunk(it, carry):
            cc = ncb - 1 - it
            r0 = pl.multiple_of(cc * CH, CH)
            rows = pl.ds(r0, CH)
            hq = _head_segment(p_ref, rows, 0, hg)
            e = _hgrn_chunk_fwd(hq, _head_segment(p_ref, rows, 1, hg), lb, tril)
            v = _bf(_head_segment(p_ref, rows, 2, hg))
            do = _bf(do_ref[rows, :])
            sts = [st_ref[cc, k] for k in heads]
            dsts = [ds_scr[k] for k in heads]
            dlb_acc = dlb_ref[...]
            qa, ka, qe, kd = _bf(e["qa"]), _bf(e["ka"]), _bf(e["qe"]), _bf(e["kd"])
            a = [_bf(jnp.where(causal, _mm_nt(_head(qa, k), _head(ka, k)), 0.0)) for k in heads]
            da = [_bf(jnp.where(causal, _mm_nt(_head(do, k), _head(v, k)), 0.0)) for k in heads]
            dqe = wide([_mm(_head(do, k), _bf(sts[k])) for k in heads])
            dkd = wide([_mm(_head(v, k), _bf(dsts[k])) for k in heads])
            dv_state = [_mm_nt(_head(kd, k), _bf(dsts[k])) for k in heads]
            ds_new = [_mm_tn(_head(do, k), _head(qe, k)) for k in heads]
            dv_intra = [_mm_tn(a[k], _head(do, k)) for k in heads]
            dqa = wide([_mm(da[k], _head(ka, k)) for k in heads])
            dka = wide([_mm_tn(da[k], _head(qa, k)) for k in heads])
            dv = wide([dv_intra[k] + dv_state[k] for k in heads])
            dbl = e["ebl"] * wide([jnp.sum(sts[k] * dsts[k], axis=0, keepdims=True) for k in heads])
            dq = dqe * e["eb"] + dqa * e["ea"]
            dk = dka * e["ek"] + dkd * e["ed"]
            dkd_kd = dkd * kd.astype(F32)
            db = dqe * qe.astype(F32) + dqa * qa.astype(F32) - dka * ka.astype(F32) - dkd_kd
            db = db + jnp.where(last_row, dbl + jnp.sum(dkd_kd, axis=0, keepdims=True), 0.0)
            dg = _mm_tri_exact(triu, db)
            df = dg / e["f"] - dk
            sg = e["sg"]
            sq = e["sq"]
            dhq = _bf(dq * (sq * (1.0 + hq * (1.0 - sq))))
            dhf = _bf(df * (1.0 - lb) * sg * (1.0 - sg))
            dhi = _bf(dv)
            dlb_new = dlb_acc + jnp.where(row0, jnp.sum(df * (1.0 - sg), axis=0, keepdims=True), 0.0)
            for k in heads:
                ds_scr[k] = ds_new[k] + dsts[k] * _head(e["ebl"], k)
            dp_ref[rows, 0:wd] = dhq
            dp_ref[rows, wd:2 * wd] = dhf
            dp_ref[rows, 2 * wd:3 * wd] = dhi
            dlb_ref[...] = dlb_new
            return carry

        lax.fori_loop(0, ncb, chunk, 0, unroll=2)

    rev = lambda h, i: (nblk - 1 - i, h)
    return pl.pallas_call(
        body, grid=(HEADS // hg, nblk),
        in_specs=[pl.BlockSpec((rb, 3 * HD * hg), rev), pl.BlockSpec((2, HD * hg), lambda h, i: (0, h)),
                  pl.BlockSpec((rb, HD * hg), rev), pl.BlockSpec((ncb, hg, HD, HD), lambda h, i: (nblk - 1 - i, h, 0, 0))],
        out_specs=[pl.BlockSpec((rb, 3 * HD * hg), rev), pl.BlockSpec((8, HD * hg), lambda h, i: (0, h))],
        out_shape=[jax.ShapeDtypeStruct((t, 3 * D), BF16), jax.ShapeDtypeStruct((8, D), F32)],
        scratch_shapes=[pltpu.VMEM((hg, HD, HD), F32)],
        name="hgrn_bwd", compiler_params=_cp(48),
    )(proj, lbw, do_raw, states)


def _kv_variants(tile, odd):
    low = lax.broadcasted_iota(jnp.int32, tile.shape, 1) < 64
    if odd:
        hi = jnp.where(low, 0.0, tile)
        lo = pltpu.roll(hi, 64, 1)
    else:
        lo = jnp.where(low, tile, 0.0)
        hi = pltpu.roll(lo, 64, 1)
    return _bf(lo), _bf(hi)


def _attn_masks(n):
    qi = lax.broadcasted_iota(jnp.int32, (AB, AB), 0)
    kj = lax.broadcasted_iota(jnp.int32, (AB, AB), 1)
    cur = kj <= qi
    return cur, cur | (n > 0), qi <= kj


def _kv_all(prev_ref, cur_ref):
    out = []
    for tl in range(2):
        cols = slice(tl * 128, (tl + 1) * 128)
        tile = jnp.concatenate([prev_ref[:, cols], cur_ref[:, cols]], axis=0)
        out.append(_kv_variants(tile, 0))
        out.append(_kv_variants(tile, 1))
    return out


def _window(a2, cur):
    return jnp.where(cur, a2[:, AB:], a2[:, :AB])


def _attn_softmax(scores, sinks, cur, ok):
    s = [jnp.where(ok, _window(s2, cur) * ATT_SCALE, NEG) for s2 in scores]
    m = [jnp.maximum(jnp.max(si, axis=-1, keepdims=True), sink) for si, sink in zip(s, sinks)]
    p = [jnp.exp(si - mi) for si, mi in zip(s, m)]
    es = [jnp.exp(sink - mi) for sink, mi in zip(sinks, m)]
    inv = [1.0 / (jnp.sum(pi, axis=-1, keepdims=True) + ei) for pi, ei in zip(p, es)]
    return [pi * ii for pi, ii in zip(p, inv)], [ei * ii for ei, ii in zip(es, inv)]


def _spread(pc, cur):
    return jnp.concatenate([jnp.where(cur, 0.0, pc), jnp.where(cur, pc, 0.0)], axis=1)


def _spread_t(pct, cur_t):
    return jnp.concatenate([jnp.where(cur_t, 0.0, pct), jnp.where(cur_t, pct, 0.0)], axis=0)


def _attn_fwd(proj, sinks):
    t = proj.shape[0]
    nb = t // AB

    nsub = 2
    assert nb % nsub == 0

    def body(q_ref, kc_ref, kp_ref, vc_ref, vp_ref, sink_ref, o_ref):
        sinks_v = [sink_ref[0, h] for h in range(QH)]
        heads = [(j, ab) for j in range(8) for ab in range(2)]
        for sb in range(nsub):
            rows = pl.ds(AB * sb, AB)
            before = pl.ds(AB * (sb - 1), AB)
            cur, ok, _ = _attn_masks(nsub * pl.program_id(0) + sb)
            kvars = _kv_all(kp_ref if sb == 0 else kc_ref.at[before, :], kc_ref.at[rows, :])
            vvars = _kv_all(vp_ref if sb == 0 else vc_ref.at[before, :], vc_ref.at[rows, :])
            qps = [_bf(q_ref[rows, 128 * j:128 * (j + 1)]) for j in range(8)]
            scores = [_mm_nt(qps[j], kvars[j // 2][ab]) for j, ab in heads]
            pcs, _ = _attn_softmax(scores, sinks_v, cur, ok)
            parts = [_mm(_bf(_spread(pcs[h], cur)), vvars[j // 2][ab]) for h, (j, ab) in enumerate(heads)]
            for j in range(8):
                o_ref[rows, 128 * j:128 * (j + 1)] = parts[2 * j] + parts[2 * j + 1]

    prev = lambda n: jnp.maximum(nsub * n - 1, 0)
    step = nsub * AB
    return pl.pallas_call(
        body, grid=(nb // nsub,),
        in_specs=[pl.BlockSpec((step, D), lambda n: (n, C_AQ // D)),
                  pl.BlockSpec((step, 256), lambda n: (n, C_AK // 256)),
                  pl.BlockSpec((AB, 256), lambda n: (prev(n), C_AK // 256)),
                  pl.BlockSpec((step, 256), lambda n: (n, C_AV // 256)),
                  pl.BlockSpec((AB, 256), lambda n: (prev(n), C_AV // 256)),
                  pl.BlockSpec(memory_space=pltpu.SMEM)],
        out_specs=pl.BlockSpec((step, D), lambda n: (n, 0)),
        out_shape=jax.ShapeDtypeStruct((t, D), F32),
        name="attn_fwd", compiler_params=_cp(32),
    )(proj, proj, proj, proj, proj, sinks)


def _attn_bwd(proj, sinks, do_a):
    t = proj.shape[0]
    nb = t // AB
    nsub = 2
    assert nb % nsub == 0
    steps = nb // nsub
    step = nsub * AB

    def body(q_ref, kc_ref, kp_ref, vc_ref, vp_ref, do_ref, sink_ref, dq_ref, dkv_ref, dsink_ref, carry):
        n = pl.program_id(0)

        @pl.when(n == 0)
        def _():
            dsink_ref[...] = jnp.zeros_like(dsink_ref)
            carry[...] = jnp.zeros_like(carry)

        def one_block(sb):
            rows = pl.ds(AB * sb, AB)
            before = pl.ds(AB * (sb - 1), AB)
            cur, ok, cur_t = _attn_masks(nsub * n + sb)
            low = lax.broadcasted_iota(jnp.int32, (2 * AB, 128), 1) < 64
            lane = lax.broadcasted_iota(jnp.int32, (8, 128), 1)
            row0 = lax.broadcasted_iota(jnp.int32, (8, 128), 0) == 0
            kvars = _kv_all(kp_ref if sb == 0 else kc_ref.at[before, :], kc_ref.at[rows, :])
            vvars = _kv_all(vp_ref if sb == 0 else vc_ref.at[before, :], vc_ref.at[rows, :])
            qps = [_bf(q_ref[rows, 128 * j:128 * (j + 1)]) for j in range(8)]
            dops = [_bf(do_ref[rows, 128 * j:128 * (j + 1)]) for j in range(8)]
            heads = [(j, ab) for j in range(8) for ab in range(2)]
            scores = [_mm_nt(qps[j], kvars[j // 2][ab]) for j, ab in heads]
            dps = [_mm_nt(dops[j], vvars[j // 2][ab]) for j, ab in heads]
            pcs, pss = _attn_softmax(scores, [sink_ref[0, h] for h in range(QH)], cur, ok)
            dpcs = [_window(dp2, cur) for dp2 in dps]
            rss = [jnp.sum(pc * dpc, axis=-1, keepdims=True) for pc, dpc in zip(pcs, dpcs)]
            dscs = [pc * (dpc - rs) for pc, dpc, rs in zip(pcs, dpcs, rss)]
            dsink = jnp.zeros((8, 128), F32)
            for h in range(QH):
                dsink = dsink + jnp.where(row0 & (lane == h), -jnp.sum(pss[h] * rss[h]), 0.0)
            dq_terms = [_mm(_bf(_spread(dscs[h], cur)), kvars[j // 2][ab]) for h, (j, ab) in enumerate(heads)]
            for j in range(8):
                dq_ref[rows, 128 * j:128 * (j + 1)] = _bf((dq_terms[2 * j] + dq_terms[2 * j + 1]) * ATT_SCALE)
            dsc_t = [_bf(_spread_t(dsc.T, cur_t)) for dsc in dscs]
            pc_t = [_bf(_spread_t(pc.T, cur_t)) for pc in pcs]
            dk_terms = [_mm(dsc_t[h], qps[j]) for h, (j, ab) in enumerate(heads)]
            dv_terms = [_mm(pc_t[h], dops[j]) for h, (j, ab) in enumerate(heads)]
            dk_ab = [[dk_terms[4 * g + ab] + dk_terms[4 * g + 2 + ab] for ab in range(2)] for g in range(4)]
            dv_ab = [[dv_terms[4 * g + ab] + dv_terms[4 * g + 2 + ab] for ab in range(2)] for g in range(4)]
            dkts, dvts = [], []
            for tl in range(2):
                ke, ko = dk_ab[2 * tl], dk_ab[2 * tl + 1]
                ve, vo = dv_ab[2 * tl], dv_ab[2 * tl + 1]
                dkts.append((jnp.where(low, ke[0], 0.0) + pltpu.roll(jnp.where(low, 0.0, ke[1]), 64, 1)
                             + jnp.where(low, 0.0, ko[1]) + pltpu.roll(jnp.where(low, ko[0], 0.0), 64, 1)) * ATT_SCALE)
                dvts.append(jnp.where(low, ve[0], 0.0) + pltpu.roll(jnp.where(low, 0.0, ve[1]), 64, 1)
                            + jnp.where(low, 0.0, vo[1]) + pltpu.roll(jnp.where(low, vo[0], 0.0), 64, 1))
            return dkts, dvts, dsink

        @pl.when(n < steps)
        def _():
            (dk0, dv0, ds0), (dk1, dv1, ds1) = one_block(0), one_block(1)
            first, second = slice(0, AB), slice(AB, 2 * AB)
            for tl in range(2):
                for cols, g0, g1 in ((slice(tl * 128, (tl + 1) * 128), dk0[tl], dk1[tl]),
                                     (slice(256 + tl * 128, 256 + (tl + 1) * 128), dv0[tl], dv1[tl])):
                    dkv_ref[first, cols] = _bf(carry[first, cols])
                    dkv_ref[second, cols] = _bf(carry[second, cols] + g0[first])
                    carry[first, cols] = g0[second] + g1[first]
                    carry[second, cols] = g1[second]
            dsink_ref[...] += ds0 + ds1

        @pl.when(n == steps)
        def _():
            dkv_ref[...] = _bf(carry[...])

    cur = lambda n: jnp.minimum(n, steps - 1)
    last = lambda n: jnp.clip(n - 1, 0, steps - 1)
    prev = lambda n: jnp.clip(nsub * n - 1, 0, nb - 1)
    return pl.pallas_call(
        body, grid=(steps + 1,),
        in_specs=[pl.BlockSpec((step, D), lambda n: (cur(n), C_AQ // D)),
                  pl.BlockSpec((step, 256), lambda n: (cur(n), C_AK // 256)),
                  pl.BlockSpec((AB, 256), lambda n: (prev(n), C_AK // 256)),
                  pl.BlockSpec((step, 256), lambda n: (cur(n), C_AV // 256)),
                  pl.BlockSpec((AB, 256), lambda n: (prev(n), C_AV // 256)),
                  pl.BlockSpec((step, D), lambda n: (cur(n), 0)),
                  pl.BlockSpec(memory_space=pltpu.SMEM)],
        out_specs=[pl.BlockSpec((step, D), lambda n: (cur(n), 0)),
                   pl.BlockSpec((step, 512), lambda n: (last(n), 0)),
                   pl.BlockSpec((8, 128), lambda n: (0, 0))],
        out_shape=[jax.ShapeDtypeStruct((t, D), BF16), jax.ShapeDtypeStruct((t, 512), BF16),
                   jax.ShapeDtypeStruct((8, 128), F32)],
        scratch_shapes=[pltpu.VMEM((step, 512), F32)],
        name="attn_bwd", compiler_params=_cp(40),
    )(proj, proj, proj, proj, proj, do_a, sinks)


def _silu_and_grad(v):
    s = _sigmoid(v)
    return v * s, s * (1.0 + v * (1.0 - s))


def _tail(o_raw, o_a, proj, x2d, tgt, wbh, wba, wout, hnw, fnw, tb):
    t = x2d.shape[0]

    def body(or_ref, oa_ref, hg_ref, ag0, ag1, mh0, mh1, ma0, ma1, x_ref, t_ref, wbh_ref, wba_ref, wout_ref, hnw_ref,
             fnw_ref, dx2_ref, dor_ref, doa_ref, dhg_ref, dagm_ref, gh_ref, ga_ref, mg_ref, dyh_ref, dya_ref, dx2b_ref,
             sums_ref):
        @pl.when(pl.program_id(0) == 0)
        def _():
            sums_ref[...] = jnp.zeros_like(sums_ref)

        halves = lambda a, b: jnp.concatenate([a[...], b[...]], axis=1)
        hnw_v = hnw_ref[...]
        fnw_v = fnw_ref[...]
        o = or_ref[...]
        rs, xhs = [], []
        for h in range(HEADS):
            oh = o[:, h * HD:(h + 1) * HD]
            r = lax.rsqrt(jnp.mean(oh * oh, axis=-1, keepdims=True) + EPS)
            rs.append(r)
            xhs.append(oh * r)
        xh = jnp.concatenate(xhs, axis=1)
        on = xh * hnw_v
        sil_hg, dsil_hg = _silu_and_grad(hg_ref[...])
        gh_b = _bf(on * sil_hg)
        y_h = _mm(gh_b, wbh_ref[...])
        oa = oa_ref[...]
        sil_ag, dsil_ag = _silu_and_grad(halves(ag0, ag1))
        ga_b = _bf(oa * sil_ag)
        y_a = _mm(ga_b, wba_ref[...])
        s_mh = _sigmoid(halves(mh0, mh1))
        s_ma = _sigmoid(halves(ma0, ma1))
        mg_b = _bf(s_mh * y_h + s_ma * y_a)
        x2 = x_ref[...] + _mm(mg_b, wout_ref[...])
        r2 = lax.rsqrt(jnp.mean(x2 * x2, axis=-1, keepdims=True) + EPS)
        xh2 = x2 * r2
        err = xh2 * fnw_v - t_ref[...]
        loss = 0.5 * jnp.sum(jnp.mean(err * err, axis=-1, keepdims=True))
        dy = err * (1.0 / D)
        dfnw = jnp.sum(dy * xh2, axis=0, keepdims=True)
        dxh2 = dy * fnw_v
        dx2 = r2 * (dxh2 - xh2 * jnp.mean(dxh2 * xh2, axis=-1, keepdims=True))
        dx2_ref[...] = dx2
        dx2_b = _bf(dx2)
        dmg = _mm_nt(dx2_b, wout_ref[...])
        dmg_h = dmg * s_mh
        dmg_a = dmg * s_ma
        dyh_b = _bf(dmg_h)
        dya_b = _bf(dmg_a)
        dagm_ref[:, D:2 * D] = _bf(dmg_h * y_h * (1.0 - s_mh))
        dagm_ref[:, 2 * D:3 * D] = _bf(dmg_a * y_a * (1.0 - s_ma))
        dgh = _mm_nt(dyh_b, wbh_ref[...])
        dga = _mm_nt(dya_b, wba_ref[...])
        doa_ref[...] = dga * sil_ag
        dagm_ref[:, 0:D] = _bf(dga * oa * dsil_ag)
        dhg_ref[...] = _bf(dgh * on * dsil_hg)
        don = dgh * sil_hg
        dhnw = jnp.sum(don * xh, axis=0, keepdims=True)
        dxh = don * hnw_v
        dos = []
        for h in range(HEADS):
            sl = slice(h * HD, (h + 1) * HD)
            dos.append(rs[h] * (dxh[:, sl] - xhs[h] * jnp.mean(dxh[:, sl] * xhs[h], axis=-1, keepdims=True)))
        dor_ref[...] = jnp.concatenate(dos, axis=1)
        gh_ref[...] = gh_b
        ga_ref[...] = ga_b
        mg_ref[...] = mg_b
        dyh_ref[...] = dyh_b
        dya_ref[...] = dya_b
        dx2b_ref[...] = dx2_b
        row = lax.broadcasted_iota(jnp.int32, (8, D), 0)
        sums_ref[...] += jnp.where(row == 0, dfnw, 0.0) + jnp.where(row == 1, dhnw, 0.0) + jnp.where(row == 2, loss, 0.0)

    rowblk = lambda c: pl.BlockSpec((tb, D), lambda i: (i, c))
    half = lambda c: pl.BlockSpec((tb, 512), lambda i: (i, c))
    full = lambda shape: pl.BlockSpec(shape, lambda i: (0, 0))
    return pl.pallas_call(
        body, grid=(t // tb,),
        in_specs=[rowblk(0), rowblk(0), rowblk(C_HG // D), half(C_AG // 512), half(C_AG // 512 + 1), half(C_MH // 512),
                  half(C_MH // 512 + 1), half(C_MA // 512), half(C_MA // 512 + 1), rowblk(0), rowblk(0),
                  full((D, D)), full((D, D)), full((D, D)), full((1, D)), full((1, D))],
        out_specs=[rowblk(0), rowblk(0), rowblk(0), rowblk(0), pl.BlockSpec((tb, 3 * D), lambda i: (i, 0))]
        + [rowblk(0)] * 6 + [full((8, D))],
        out_shape=[jax.ShapeDtypeStruct((t, D), F32)] * 3
        + [jax.ShapeDtypeStruct((t, D), BF16), jax.ShapeDtypeStruct((t, 3 * D), BF16)]
        + [jax.ShapeDtypeStruct((t, D), BF16)] * 6 + [jax.ShapeDtypeStruct((8, D), F32)],
        name="tail", compiler_params=_cp(56),
    )(o_raw, o_a, proj, proj, proj, proj, proj, proj, proj, x2d, tgt, wbh, wba, wout, hnw, fnw)


def _wgrad3(gh, dyh, ga, dya, mg, dx2b, tk, after):
    t = dyh.shape[0]

    def body(a0, b0, a1, b1, a2, b2, _, o0, o1, o2):
        @pl.when(pl.program_id(0) == 0)
        def _():
            o0[...] = jnp.zeros_like(o0)
            o1[...] = jnp.zeros_like(o1)
            o2[...] = jnp.zeros_like(o2)

        o0[...] += _mm_tn(a0[...], b0[...])
        o1[...] += _mm_tn(a1[...], b1[...])
        o2[...] += _mm_tn(a2[...], b2[...])

    blk = pl.BlockSpec((tk, D), lambda k: (k, 0))
    out = pl.BlockSpec((D, D), lambda k: (0, 0))
    return pl.pallas_call(
        body, grid=(t // tk,), in_specs=[blk] * 6 + [pl.BlockSpec(memory_space=pl.ANY)], out_specs=[out] * 3,
        out_shape=[jax.ShapeDtypeStruct((D, D), F32)] * 3,
        name="wgrad3", compiler_params=_cp(48),
    )(gh, dyh, ga, dya, mg, dx2b, after)


def _inproj_wgrad(xnt, pieces, nb, name):
    t = xnt.shape[1]
    counts = [p.shape[1] // nb for p in pieces]
    firsts = [sum(counts[:k]) for k in range(len(pieces))]

    def body(xnt_ref, *refs):
        o_ref = refs[-1]
        j = pl.program_id(0)
        for first, count, p_ref in zip(firsts, counts, refs[:-1]):
            @pl.when((j >= first) & (j < first + count))
            def _(p_ref=p_ref):
                o_ref[...] = _mm(xnt_ref[...], p_ref[...])

    def piece_spec(first, count):
        return pl.BlockSpec((t, nb), lambda j: (0, jnp.clip(j - first, 0, count - 1)))

    return pl.pallas_call(
        body, grid=(sum(counts),),
        in_specs=[pl.BlockSpec((D, t), lambda j: (0, 0), pipeline_mode=pl.Buffered(1))]
        + [piece_spec(f, c) for f, c in zip(firsts, counts)],
        out_specs=pl.BlockSpec((D, nb), lambda j: (0, j)),
        out_shape=jax.ShapeDtypeStruct((D, sum(counts) * nb), F32),
        name=name, compiler_params=_cp(56),
    )(xnt, *pieces)


def _inproj_dgrad(pieces, w_p, x2d, dx2, norm_w, tb, after):
    t = x2d.shape[0]

    def body(*refs):
        piece_refs = refs[:len(pieces)]
        w_ref, x_ref, dx2_ref, nw_ref, _, gx_ref, dnw_ref = refs[len(pieces):]

        @pl.when(pl.program_id(0) == 0)
        def _():
            dnw_ref[...] = jnp.zeros_like(dnw_ref)

        dxn = None
        off = 0
        for p in piece_refs:
            width = p.shape[1]
            for q in range(w_ref.shape[0]):
                lo, hi = max(off, q * PAIR), min(off + width, (q + 1) * PAIR)
                if lo < hi:
                    term = _mm_nt(p[:, lo - off:hi - off], w_ref[q, :, lo - q * PAIR:hi - q * PAIR])
                    dxn = term if dxn is None else dxn + term
            off += width
        xv = x_ref[...]
        r = lax.rsqrt(jnp.mean(xv * xv, axis=-1, keepdims=True) + EPS)
        xh = xv * r
        dxh = dxn * nw_ref[...]
        gx_ref[...] = dx2_ref[...] + r * (dxh - xh * jnp.mean(dxh * xh, axis=-1, keepdims=True))
        row0 = lax.broadcasted_iota(jnp.int32, (8, D), 0) == 0
        dnw_ref[...] += jnp.where(row0, jnp.sum(dxn * xh, axis=0, keepdims=True), 0.0)

    rowblk = pl.BlockSpec((tb, D), lambda i: (i, 0))
    return pl.pallas_call(
        body, grid=(t // tb,),
        in_specs=[pl.BlockSpec((tb, p.shape[1]), lambda i: (i, 0)) for p in pieces]
        + [pl.BlockSpec(w_p.shape, lambda i: (0, 0, 0), pipeline_mode=pl.Buffered(1)), rowblk, rowblk,
           pl.BlockSpec((1, D), lambda i: (0, 0)), pl.BlockSpec(memory_space=pl.ANY)],
        out_specs=[rowblk, pl.BlockSpec((8, D), lambda i: (0, 0))],
        out_shape=[jax.ShapeDtypeStruct((t, D), F32), jax.ShapeDtypeStruct((8, D), F32)],
        name="inproj_dgrad", compiler_params=_cp(60),
    )(*pieces, w_p, x2d, dx2, norm_w, after)


def _adamw_math(w, g, m, v):
    m = B1 * m + (1.0 - B1) * g
    v = B2 * v + (1.0 - B2) * (g * g)
    m_hat = m / (1.0 - B1 ** STEP)
    v_hat = v / (1.0 - B2 ** STEP)
    delta = -LR * (m_hat / (jnp.sqrt(v_hat) + ADAM_EPS) + WD * w)
    return delta, m, v


def _adamw_shard(recv, sums, chip, w, m, v, rows, name):
    nparts, nr, nc = recv.shape

    def body(chip_ref, own_ref, p_ref, w_ref, m_ref, v_ref, g_ref, d_ref, nm_ref, nv_ref):
        g = own_ref[0].astype(F32)
        for s in range(nparts):
            g = g + p_ref[s].astype(F32)
        d, nm, nv = _adamw_math(w_ref[...], g, m_ref[...], v_ref[...])
        g_ref[...] = g
        d_ref[...] = d
        nm_ref[...] = nm
        nv_ref[...] = nv

    blk = pl.BlockSpec((rows, nc), lambda i, chip_ref: (i, 0))
    return pl.pallas_call(
        body,
        grid_spec=pltpu.PrefetchScalarGridSpec(
            num_scalar_prefetch=1, grid=(nr // rows,),
            in_specs=[pl.BlockSpec((1, rows, nc), lambda i, chip_ref: (chip_ref[0], i, 0)),
                      pl.BlockSpec((nparts, rows, nc), lambda i, chip_ref: (0, i, 0)), blk, blk, blk],
            out_specs=[blk] * 4),
        out_shape=[jax.ShapeDtypeStruct((nr, nc), F32)] * 4,
        name=name, compiler_params=_cp(48),
    )(chip, sums, recv, w, m, v)


def _adamw_sum8(parts, w, m, v, after, name):
    def body(p_ref, w_ref, m_ref, v_ref, _, g_ref, d_ref, nm_ref, nv_ref):
        g = p_ref[0].astype(F32)
        for s in range(1, NDEV):
            g = g + p_ref[s].astype(F32)
        d, nm, nv = _adamw_math(w_ref[...], g, m_ref[...], v_ref[...])
        g_ref[...] = g
        d_ref[...] = d
        nm_ref[...] = nm
        nv_ref[...] = nv

    vm = pl.BlockSpec(memory_space=pltpu.VMEM)
    return pl.pallas_call(
        body, out_shape=[jax.ShapeDtypeStruct(w.shape, F32)] * 4,
        in_specs=[vm, vm, vm, vm, pl.BlockSpec(memory_space=pl.ANY)], out_specs=[vm] * 4, name=name,
    )(parts, w, m, v, after)


SMALL_ROWS = dict(norm_w=0, lower_bound=1, hgrn_norm_w=3, final_norm_w=4, sinks=5, loss=6)


def _pack_small_grads(dnw, dlb, sums, dsink):
    def body(dnw_ref, dlb_ref, sums_ref, dsink_ref, o_ref):
        o_ref[...] = jnp.zeros_like(o_ref)
        o_ref[0:1, :] = dnw_ref[0:1, :]
        o_ref[1:2, :] = dlb_ref[0:1, :]
        o_ref[3:4, :] = sums_ref[1:2, :]
        o_ref[4:5, :] = sums_ref[0:1, :]
        o_ref[5:6, 0:128] = dsink_ref[0:1, :]
        o_ref[6:7, :] = sums_ref[2:3, :]

    return pl.pallas_call(body, out_shape=jax.ShapeDtypeStruct((8, D), F32), name="pack_small_grads")(dnw, dlb, sums, dsink)


def _adamw_small(parts, ws, ms, vs):
    shapes = [a.shape for a in ws]

    def body(p_ref, *refs):
        w, m, v = refs[0:5], refs[5:10], refs[10:15]
        outs = [refs[15 + 5 * i:20 + 5 * i] for i in range(4)]
        loss_ref = refs[35]

        def total(row, width):
            g = p_ref[0, row:row + 1, 0:width]
            for s in range(1, NDEV):
                g = g + p_ref[s, row:row + 1, 0:width]
            return g

        loss_ref[...] = total(6, 128)
        lb = _lower_bound(w[1])
        ga0 = total(1, D) * lb * (1.0 - lb)
        grads = [total(0, D), None, total(3, D), total(4, D), total(5, QH)]
        for i in (0, 2, 3, 4):
            res = (grads[i],) + _adamw_math(w[i][...], grads[i], m[i][...], v[i][...])
            for o, val in zip(outs, res):
                o[i][...] = val
        for r, g in ((0, ga0), (1, -ga0)):
            res = (g,) + _adamw_math(w[1][r:r + 1, :], g, m[1][r:r + 1, :], v[1][r:r + 1, :])
            for o, val in zip(outs, res):
                o[1][r:r + 1, :] = val

    res = pl.pallas_call(
        body, out_shape=[jax.ShapeDtypeStruct(s, F32) for s in shapes] * 4 + [jax.ShapeDtypeStruct((1, 128), F32)],
        name="adamw_small",
    )(parts, *ws, *ms, *vs)
    return [res[5 * i:5 * i + 5] for i in range(4)], res[20][0, 0]


def kernel(x, norm_w, w_in, hgrn_lower_bound, hgrn_norm_w, w_branch_hgrn, attn_sinks, w_branch_attn, w_out, final_norm_w, loss_target, m_norm_w, m_w_in, m_hgrn_lower_bound, m_hgrn_norm_w, m_w_branch_hgrn, m_attn_sinks, m_w_branch_attn, m_w_out, m_final_norm_w, v_norm_w, v_w_in, v_hgrn_lower_bound, v_hgrn_norm_w, v_w_branch_hgrn, v_attn_sinks, v_w_branch_attn, v_w_out, v_final_norm_w):
    t = x.shape[1]
    x2d = x.reshape(t, D)
    tgt = loss_target.reshape(t, D)
    fnw = final_norm_w.reshape(1, D)
    row_blk = min(256, t)
    big_blk = min(512, t)

    chip = (2 * lax.axis_index("x") + lax.axis_index("y")).astype(jnp.int32).reshape(1)
    w_p, xn, xnt, proj_own = _gather_in_projection(w_in[0], x2d, norm_w)
    wbh, wba, wout = (g.reshape(D, D) for g in _gather_square(
        [w_branch_hgrn[0].astype(BF16), w_branch_attn[0].astype(BF16), w_out[0].astype(BF16)], after=w_p))

    proj = _inproj_fwd(xn, w_p, proj_own, chip, min(1024, t))
    o_raw, states = _hgrn_fwd(proj, hgrn_lower_bound, big_blk, HGRN_GROUP)
    o_a = _attn_fwd(proj, attn_sinks)
    (dx2, do_raw, do_a, d_hg, d_agm, gh, ga, mg, dyh, dya, dx2b, sums) = _tail(
        o_raw, o_a, proj, x2d, tgt, wbh, wba, wout, hgrn_norm_w, fnw, row_blk)
    d_aq, d_kv, dsink = _attn_bwd(proj, attn_sinks, do_a)
    d_hgrn, dlb = _hgrn_bwd(proj, hgrn_lower_bound, do_raw, states, big_blk, HGRN_GROUP)
    pieces = (d_hgrn, d_hg, d_aq, d_kv, d_agm)
    dw_pieces = [_inproj_wgrad(xnt, [d_hgrn], CB, "inproj_wgrad_hgrn"),
                 _inproj_wgrad(xnt, [d_hg, d_aq, d_kv], CB // 2, "inproj_wgrad_mid"),
                 _inproj_wgrad(xnt, [d_agm], CB, "inproj_wgrad_gates")]

    dw_cat = jnp.concatenate(dw_pieces, axis=1).astype(BF16)
    dwin_r = dw_cat.reshape(D, NDEV, IN_SHARD).transpose(1, 0, 2)
    dwbh, dwba, dwout = _wgrad3(gh, dyh, ga, dya, mg, dx2b, big_blk, after=dw_cat)
    slots = lambda a: a.reshape(NDEV, ROW_SHARD, D).astype(BF16)
    got = _exchange_pair([dwin_r], after=dwbh)
    core = lax.axis_index("c").astype(jnp.int32).reshape(1)
    s_in = _pair_sum(dwin_r, got[0], core, 4 * ROW_SHARD, "pair_sum_w_in")
    rin, = _exchange_chips([s_in])
    rbh, rba, rout = _exchange_square([slots(dwbh), slots(dwba), slots(dwout)])
    grad_x, dnw = _inproj_dgrad(pieces, w_p, x2d, dx2, norm_w, big_blk, after=s_in)
    rsm = _exchange_small(_pack_small_grads(dnw, dlb, sums, dsink))
    g_in, d_in, nm_in, nv_in = _adamw_shard(rin, s_in, chip, w_in[0], m_w_in[0], v_w_in[0], 128, "adamw_w_in")
    g_bh, d_bh, nm_bh, nv_bh = _adamw_sum8(
        rbh, w_branch_hgrn[0], m_w_branch_hgrn[0], v_w_branch_hgrn[0], dnw, "adamw_w_bh")
    g_ba, d_ba, nm_ba, nv_ba = _adamw_sum8(
        rba, w_branch_attn[0], m_w_branch_attn[0], v_w_branch_attn[0], dnw, "adamw_w_ba")
    g_out, d_out, nm_out, nv_out = _adamw_sum8(rout, w_out[0], m_w_out[0], v_w_out[0], dnw, "adamw_w_out")
    (sg, sd, sm, sv), loss = _adamw_small(
        rsm,
        (norm_w, hgrn_lower_bound, hgrn_norm_w, fnw, attn_sinks),
        (m_norm_w, m_hgrn_lower_bound, m_hgrn_norm_w, m_final_norm_w.reshape(1, D), m_attn_sinks),
        (v_norm_w, v_hgrn_lower_bound, v_hgrn_norm_w, v_final_norm_w.reshape(1, D), v_attn_sinks))

    def group(s, w_in_v, bh, ba, out):
        nw, lb, hnw, fn, sinks = s
        return (nw, w_in_v[None], lb, hnw, bh[None], sinks, ba[None], out[None], fn.reshape(D))

    return (loss, grad_x.reshape(1, t, D),
            *group(sg, g_in, g_bh, g_ba, g_out), *group(sd, d_in, d_bh, d_ba, d_out),
            *group(sm, nm_in, nm_bh, nm_ba, nm_out), *group(sv, nv_in, nv_bh, nv_ba, nv_out))
```

```python
import jax
import jax.numpy as jnp
from jax import lax
from jax.experimental import pallas as pl
from jax.experimental.pallas import tpu as pltpu
from jax.experimental.pallas import tpu_sc as plsc

F32 = jnp.float32
BF16 = jnp.bfloat16

D = 1024
DIN = 8704
NDEV = 8
IN_SHARD = DIN // NDEV
PAIR = 2 * IN_SHARD
ROW_SHARD = D // NDEV
HEADS = 8
HD = 128
CH = 64
HGRN_GROUP = 8
QH = 16
AB = 128
EPS = 1e-6
NEG = -1e30
ATT_SCALE = 0.125

C_HG = 3072
C_AQ = 4096
C_AK = 5120
C_AV = 5376
C_AG = 5632
C_MH = 6656
C_MA = 7680
CB = 512

LR = 0.001
B1 = 0.9
B2 = 0.999
ADAM_EPS = 1e-08
WD = 0.01
STEP = 10

MESH = pl.DeviceIdType.MESH


def _cp(vmem_mb):
    return pltpu.CompilerParams(vmem_limit_bytes=vmem_mb * 1024 * 1024)


def _mm(a, b):
    return jnp.dot(a, b, preferred_element_type=F32)


def _mm_nt(a, b):
    return lax.dot_general(a, b, (((1,), (1,)), ((), ())), preferred_element_type=F32)


def _mm_tn(a, b):
    return lax.dot_general(a, b, (((0,), (0,)), ((), ())), preferred_element_type=F32)


def _tri3(lower):
    r = lax.broadcasted_iota(jnp.int32, (CH, 3 * CH), 0)
    c = lax.broadcasted_iota(jnp.int32, (CH, 3 * CH), 1)
    c = jnp.where(c >= 2 * CH, c - 2 * CH, jnp.where(c >= CH, c - CH, c))
    return ((r >= c) if lower else (c >= r)).astype(BF16)


def _mm_tri_exact(tri3, g):
    g1 = g.astype(BF16)
    r1 = g - g1.astype(F32)
    g2 = r1.astype(BF16)
    g3 = (r1 - g2.astype(F32)).astype(BF16)
    return _mm(tri3, jnp.concatenate([g1, g2, g3], axis=0))


def _sigmoid(v):
    return 0.5 * jnp.tanh(0.5 * v) + 0.5


def _bf(v):
    return v.astype(BF16)


def _place():
    x, y, c = lax.axis_index("x"), lax.axis_index("y"), lax.axis_index("c")
    return (x, y, c), (x, y, 1 - c), [(1 - x, y), (x, 1 - y), (1 - x, 1 - y)]


def _dev_index(px, py, pc):
    return 4 * px + 2 * py + pc


def _gather_in_projection(w_in_s, x2d, norm_w):
    half = D // 2
    t = x2d.shape[0]
    prep_rows = min(512, t)
    nprep = t // prep_rows

    def body(win_ref, x_hbm, nw_ref, wp_g, xn_hbm, xnt_hbm, proj_hbm, give, take, mine, xbuf, xnbuf, xntbuf, w_own, pbuf,
             send_sems, recv_sems, loc_sem, swap_sems, in_sems, out_sems, own_sem):
        (x, y, c), sibling, chips = _place()
        give[...] = win_ref[pl.ds(pl.multiple_of(half * (1 - c), half), half), :].astype(BF16)
        swap = pltpu.make_async_remote_copy(src_ref=give, dst_ref=take, send_sem=swap_sems.at[0], recv_sem=swap_sems.at[1],
                                            device_id=sibling, device_id_type=MESH)
        swap.start()
        swap.wait()
        own = win_ref[pl.ds(pl.multiple_of(half * c, half), half), :]
        other = take[...].astype(F32)
        mine[...] = jnp.where(c == 0, jnp.concatenate([own, other], axis=1),
                              jnp.concatenate([other, own], axis=1)).astype(BF16)

        def place(px, py, pc):
            return wp_g.at[2 * px + py, pl.ds(pl.multiple_of(half * pc, half), half), :]

        def copy(kind, origin, to, src=mine):
            return pltpu.make_async_remote_copy(
                src_ref=src, dst_ref=place(*origin), send_sem=send_sems.at[kind], recv_sem=recv_sems.at[kind],
                device_id=to, device_id_type=MESH)

        me = (x, y, c)
        local = pltpu.make_async_copy(mine, place(*me), loc_sem)
        local.start()
        first = [copy(0, me, sibling)] + [copy(1 + j, me, (*chip, c)) for j, chip in enumerate(chips)]
        for cp in first:
            cp.start()

        copy(0, (x, y, 1 - c), me).wait_recv()
        local.wait()
        my_chip = 2 * x + y
        fetch = pltpu.make_async_copy(wp_g.at[my_chip], w_own, own_sem)
        fetch.start()

        def rows_of(i):
            return pl.ds(pl.multiple_of(i * prep_rows, prep_rows), prep_rows)

        def load(i, slot):
            return pltpu.make_async_copy(x_hbm.at[rows_of(i), :], xbuf.at[slot], in_sems.at[slot])

        def stores(i, slot):
            own_cols = pl.ds(pl.multiple_of(my_chip * PAIR, 128), PAIR)
            return (pltpu.make_async_copy(xnbuf.at[slot], xn_hbm.at[rows_of(i), :], out_sems.at[slot, 0]),
                    pltpu.make_async_copy(xntbuf.at[slot], xnt_hbm.at[:, rows_of(i)], out_sems.at[slot, 1]),
                    pltpu.make_async_copy(pbuf.at[slot], proj_hbm.at[rows_of(i), own_cols], out_sems.at[slot, 2]))

        load(0, 0).start()
        fetch.wait()

        def prep(i, carry):
            slot = lax.rem(i, 2)
            load(i, slot).wait()

            @pl.when(i + 1 < nprep)
            def _():
                load(i + 1, 1 - slot).start()

            @pl.when(i >= 2)
            def _():
                for cp in stores(i - 2, slot):
                    cp.wait()

            xv = xbuf[slot]
            xn = (xv * lax.rsqrt(jnp.mean(xv * xv, axis=-1, keepdims=True) + EPS)) * nw_ref[...]
            xn_b = xn.astype(BF16)
            xnbuf[slot] = xn_b
            xntbuf[slot] = xn.T.astype(BF16)
            pbuf[slot] = _mm(xn_b, w_own[...])
            for cp in stores(i, slot):
                cp.start()
            return carry

        lax.fori_loop(0, nprep, prep, 0)
        for i in range(max(nprep - 2, 0), nprep):
            for cp in stores(i, i % 2):
                cp.wait()

        passed = []
        for j, chip in enumerate(chips):
            copy(1 + j, (*chip, c), me).wait_recv()
            cp = copy(4 + j, (*chip, c), sibling, src=place(*chip, c))
            cp.start()
            passed.append(cp)
        for j, chip in enumerate(chips):
            copy(4 + j, (*chip, 1 - c), me).wait_recv()
        for cp in first + passed:
            cp.wait_send()

    vm = pl.BlockSpec(memory_space=pltpu.VMEM)
    hbm = pl.BlockSpec(memory_space=pl.ANY)
    return pl.pallas_call(
        body,
        out_shape=[jax.ShapeDtypeStruct((NDEV // 2, D, PAIR), BF16), jax.ShapeDtypeStruct((t, D), BF16),
                   jax.ShapeDtypeStruct((D, t), BF16), jax.ShapeDtypeStruct((t, DIN), F32)],
        in_specs=[vm, hbm, vm],
        out_specs=[hbm, hbm, hbm, hbm],
        scratch_shapes=[pltpu.VMEM((half, IN_SHARD), BF16), pltpu.VMEM((half, IN_SHARD), BF16),
                        pltpu.VMEM((half, PAIR), BF16),
                        pltpu.VMEM((2, prep_rows, D), F32), pltpu.VMEM((2, prep_rows, D), BF16),
                        pltpu.VMEM((2, D, prep_rows), BF16),
                        pltpu.VMEM((D, PAIR), BF16), pltpu.VMEM((2, prep_rows, PAIR), F32),
                        pltpu.SemaphoreType.DMA((NDEV - 1,)), pltpu.SemaphoreType.DMA((NDEV - 1,)),
                        pltpu.SemaphoreType.DMA, pltpu.SemaphoreType.DMA((2,)),
                        pltpu.SemaphoreType.DMA((2,)), pltpu.SemaphoreType.DMA((2, 3)), pltpu.SemaphoreType.DMA],
        name="gather_in_projection", compiler_params=_cp(56),
    )(w_in_s, x2d, norm_w)


def _gather_square(shards, after):
    n = len(shards)

    def launch(*refs):
        ins, outs = refs[:n], refs[n + 1:2 * n + 1]
        send_sems, recv_sems, loc_sems = refs[2 * n + 1:]
        (x, y, c), _, _ = _place()
        me = _dev_index(x, y, c)
        peers = [(1 - x if r & 4 else x, 1 - y if r & 2 else y, 1 - c if r & 1 else c) for r in range(1, NDEV)]
        barrier = pltpu.get_barrier_semaphore()
        for peer in peers:
            pl.semaphore_signal(barrier, inc=1, device_id=peer, device_id_type=MESH)
        pl.semaphore_wait(barrier, NDEV - 1)
        local = [pltpu.make_async_copy(ins[k], outs[k].at[me], loc_sems.at[k]) for k in range(n)]
        copies = [pltpu.make_async_remote_copy(
            src_ref=ins[k], dst_ref=outs[k].at[me], send_sem=send_sems.at[r, k], recv_sem=recv_sems.at[r, k],
            device_id=peer, device_id_type=MESH) for r, peer in enumerate(peers) for k in range(n)]
        for cp in local + copies:
            cp.start()
        for r, peer in enumerate(peers):
            for k in range(n):
                pltpu.make_async_remote_copy(
                    src_ref=ins[k], dst_ref=outs[k].at[_dev_index(*peer)], send_sem=send_sems.at[r, k],
                    recv_sem=recv_sems.at[r, k], device_id=peer, device_id_type=MESH).wait_recv()
        for cp in copies:
            cp.wait_send()
        for cp in local:
            cp.wait()

    return pl.kernel(
        launch, out_type=[jax.ShapeDtypeStruct((NDEV,) + a.shape, a.dtype) for a in shards],
        mesh=plsc.ScalarSubcoreMesh(axis_name="sequencer", num_cores=1), name="gather_square",
        scratch_types=(pltpu.SemaphoreType.DMA((NDEV - 1, n)), pltpu.SemaphoreType.DMA((NDEV - 1, n)),
                       pltpu.SemaphoreType.DMA((n,))),
        compiler_params=pltpu.CompilerParams(collective_id=2),
    )(*shards, after)


def _exchange_pair(arrs, after):
    n = len(arrs)

    def launch(*refs):
        ins, got = refs[:n], refs[n + 1:2 * n + 1]
        send_sems, recv_sems = refs[2 * n + 1:]
        (x, y, c), sibling, _ = _place()
        barrier = pltpu.get_barrier_semaphore()
        pl.semaphore_signal(barrier, inc=1, device_id=sibling, device_id_type=MESH)
        pl.semaphore_wait(barrier, 1)
        sends = [pltpu.make_async_remote_copy(
            src_ref=ins[k].at[_dev_index(q // 2, q % 2, 1 - c)], dst_ref=got[k].at[q], send_sem=send_sems.at[q, k],
            recv_sem=recv_sems.at[q, k], device_id=sibling, device_id_type=MESH) for q in range(4) for k in range(n)]
        for cp in sends:
            cp.start()
        for cp in sends:
            cp.wait_recv()
        for cp in sends:
            cp.wait_send()

    return pl.kernel(
        launch, out_type=[jax.ShapeDtypeStruct((4,) + a.shape[1:], a.dtype) for a in arrs],
        mesh=plsc.ScalarSubcoreMesh(axis_name="sequencer", num_cores=1), name="exchange_pair",
        scratch_types=(pltpu.SemaphoreType.DMA((4, n)), pltpu.SemaphoreType.DMA((4, n))),
        compiler_params=pltpu.CompilerParams(collective_id=0),
    )(*arrs, after)


def _pair_sum(full, got, core, rows, name):
    _, nr, nc = got.shape

    def body(core_ref, a_ref, b_ref, o_ref):
        o_ref[...] = (a_ref[...].astype(F32) + b_ref[...].astype(F32)).astype(BF16)

    blk = pl.BlockSpec((1, rows, nc), lambda q, i, core_ref: (q, i, 0))
    return pl.pallas_call(
        body,
        grid_spec=pltpu.PrefetchScalarGridSpec(
            num_scalar_prefetch=1, grid=(4, nr // rows),
            in_specs=[pl.BlockSpec((1, rows, nc), lambda q, i, core_ref: (2 * q + core_ref[0], i, 0)), blk],
            out_specs=blk),
        out_shape=jax.ShapeDtypeStruct(got.shape, BF16), name=name,
    )(core, full, got)


def _exchange_chips(sums):
    n = len(sums)

    def launch(*refs):
        ins, outs = refs[:n], refs[n:2 * n]
        send_sems, recv_sems = refs[2 * n:]
        (x, y, c), _, chips = _place()
        barrier = pltpu.get_barrier_semaphore()
        for px, py in chips:
            pl.semaphore_signal(barrier, inc=1, device_id=(px, py, c), device_id_type=MESH)
        pl.semaphore_wait(barrier, len(chips))
        copies = [pltpu.make_async_remote_copy(
            src_ref=ins[k].at[2 * px + py], dst_ref=outs[k].at[j], send_sem=send_sems.at[j, k],
            recv_sem=recv_sems.at[j, k], device_id=(px, py, c), device_id_type=MESH)
            for j, (px, py) in enumerate(chips) for k in range(n)]
        for cp in copies:
            cp.start()
        for cp in copies:
            cp.wait_recv()
        for cp in copies:
            cp.wait_send()

    return pl.kernel(
        launch, out_type=[jax.ShapeDtypeStruct((3,) + a.shape[1:], a.dtype) for a in sums],
        mesh=plsc.ScalarSubcoreMesh(axis_name="sequencer", num_cores=1), name="exchange_chips",
        scratch_types=(pltpu.SemaphoreType.DMA((3, n)), pltpu.SemaphoreType.DMA((3, n))),
        compiler_params=pltpu.CompilerParams(collective_id=1),
    )(*sums)


def _exchange_square(partials):
    n = len(partials)

    def launch(*refs):
        ins, outs = refs[:n], refs[n:2 * n]
        send_sems, recv_sems, loc_sems = refs[2 * n:]
        (x, y, c), _, _ = _place()
        me = _dev_index(x, y, c)
        peers = [(1 - x if r & 4 else x, 1 - y if r & 2 else y, 1 - c if r & 1 else c) for r in range(1, NDEV)]
        barrier = pltpu.get_barrier_semaphore()
        for peer in peers:
            pl.semaphore_signal(barrier, inc=1, device_id=peer, device_id_type=MESH)
        pl.semaphore_wait(barrier, NDEV - 1)
        local = [pltpu.make_async_copy(ins[k].at[me], outs[k].at[me], loc_sems.at[k]) for k in range(n)]
        copies = [pltpu.make_async_remote_copy(
            src_ref=ins[k].at[_dev_index(*peer)], dst_ref=outs[k].at[me], send_sem=send_sems.at[r, k],
            recv_sem=recv_sems.at[r, k], device_id=peer, device_id_type=MESH)
            for r, peer in enumerate(peers) for k in range(n)]
        for cp in local + copies:
            cp.start()
        for r, peer in enumerate(peers):
            for k in range(n):
                pltpu.make_async_remote_copy(
                    src_ref=ins[k].at[me], dst_ref=outs[k].at[_dev_index(*peer)], send_sem=send_sems.at[r, k],
                    recv_sem=recv_sems.at[r, k], device_id=peer, device_id_type=MESH).wait_recv()
        for cp in copies:
            cp.wait_send()
        for cp in local:
            cp.wait()

    return pl.kernel(
        launch, out_type=[jax.ShapeDtypeStruct(a.shape, a.dtype) for a in partials],
        mesh=plsc.ScalarSubcoreMesh(axis_name="sequencer", num_cores=1), name="exchange_square",
        scratch_types=(pltpu.SemaphoreType.DMA((NDEV - 1, n)), pltpu.SemaphoreType.DMA((NDEV - 1, n)),
                       pltpu.SemaphoreType.DMA((n,))),
        compiler_params=pltpu.CompilerParams(collective_id=3),
    )(*partials)


def _exchange_small(small):
    def body(sm_ref, out_ref, send_sems, recv_sems):
        (x, y, c), _, _ = _place()
        me = _dev_index(x, y, c)
        peers = [(1 - x if r & 4 else x, 1 - y if r & 2 else y, 1 - c if r & 1 else c) for r in range(1, NDEV)]
        out_ref[me] = sm_ref[...]
        copies = [pltpu.make_async_remote_copy(
            src_ref=sm_ref, dst_ref=out_ref.at[me], send_sem=send_sems.at[r], recv_sem=recv_sems.at[r],
            device_id=peer, device_id_type=MESH) for r, peer in enumerate(peers)]
        for cp in copies:
            cp.start()
        for r, peer in enumerate(peers):
            pltpu.make_async_remote_copy(
                src_ref=sm_ref, dst_ref=out_ref.at[_dev_index(*peer)], send_sem=send_sems.at[r], recv_sem=recv_sems.at[r],
                device_id=peer, device_id_type=MESH).wait_recv()
        for cp in copies:
            cp.wait_send()

    vm = pl.BlockSpec(memory_space=pltpu.VMEM)
    return pl.pallas_call(
        body, out_shape=jax.ShapeDtypeStruct((NDEV,) + small.shape, F32), in_specs=[vm], out_specs=vm,
        scratch_shapes=[pltpu.SemaphoreType.DMA((NDEV - 1,)), pltpu.SemaphoreType.DMA((NDEV - 1,))],
        name="exchange_small",
    )(small)


def _inproj_fwd(xn, w_pairs, proj, chip, tb):
    t = xn.shape[0]
    nblk, _, nb = w_pairs.shape

    def body(chip_ref, xn_ref, w_ref, proj_in, proj_ref):
        proj_ref[...] = _mm(xn_ref[...], w_ref[0])

    def other(j, chip_ref):
        return j + (j >= chip_ref[0]).astype(jnp.int32)

    return pl.pallas_call(
        body,
        grid_spec=pltpu.PrefetchScalarGridSpec(
            num_scalar_prefetch=1, grid=(t // tb, nblk - 1),
            in_specs=[pl.BlockSpec((tb, D), lambda i, j, chip_ref: (i, 0)),
                      pl.BlockSpec((1, D, nb), lambda i, j, chip_ref: (other(j, chip_ref), 0, 0)),
                      pl.BlockSpec(memory_space=pl.ANY)],
            out_specs=pl.BlockSpec((tb, nb), lambda i, j, chip_ref: (i, other(j, chip_ref)))),
        out_shape=jax.ShapeDtypeStruct((t, DIN), F32),
        input_output_aliases={3: 0},
        name="inproj_fwd", compiler_params=_cp(56),
    )(chip, xn, w_pairs, proj)


def _lower_bound(lb_ref):
    a0 = lb_ref[0:1, :]
    a1 = lb_ref[1:2, :]
    mx = jnp.maximum(a0, a1)
    e0 = jnp.exp(a0 - mx)
    e1 = jnp.exp(a1 - mx)
    return e0 / (e0 + e1)


def _hgrn_chunk_fwd(hq, hf, lb, tril):
    sg = _sigmoid(hf)
    f = lb + (1.0 - lb) * sg
    g = jnp.log(f)
    k = 1.0 - f
    sq = _sigmoid(hq)
    q = hq * sq
    b = _mm_tri_exact(tril, g)
    last_row = lax.broadcasted_iota(jnp.int32, b.shape, 0) == CH - 1
    b_last = jnp.sum(jnp.where(last_row, b, 0.0), axis=0, keepdims=True)
    c = 0.5 * b_last
    eb = jnp.exp(b)
    ea = jnp.exp(b - c)
    ek = jnp.exp(c - b)
    ed = jnp.exp(b_last - b)
    ebl = jnp.exp(b_last)
    return dict(sg=sg, f=f, k=k, sq=sq, q=q, eb=eb, ea=ea, ek=ek, ed=ed, ebl=ebl,
                qe=q * eb, qa=q * ea, ka=k * ek, kd=k * ed)


def _tri(lower):
    r = lax.broadcasted_iota(jnp.int32, (CH, CH), 0)
    c = lax.broadcasted_iota(jnp.int32, (CH, CH), 1)
    return (r >= c) if lower else (c >= r)


def _head_segment(p_ref, rows, j, hg):
    return p_ref[rows, j * HD * hg:(j + 1) * HD * hg]


def _head(a, k):
    return a[:, k * HD:(k + 1) * HD]


def _hgrn_fwd(proj, lbw, rb, hg):
    assert hg == HEADS
    t = proj.shape[0]
    ncb = rb // CH

    def body(p_ref, lb_ref, o_ref, st_ref, s_scr):
        @pl.when(pl.program_id(1) == 0)
        def _():
            s_scr[...] = jnp.zeros_like(s_scr)

        lb = _lower_bound(lb_ref)
        causal = _tri(True)
        tril = _tri3(True)
        heads = range(hg)

        def chunk(cc, carry):
            r0 = pl.multiple_of(cc * CH, CH)
            rows = pl.ds(r0, CH)
            e = _hgrn_chunk_fwd(_head_segment(p_ref, rows, 0, hg), _head_segment(p_ref, rows, 1, hg), lb, tril)
            v = _bf(_head_segment(p_ref, rows, 2, hg))
            sts = [s_scr[k] for k in heads]
            qa, ka, qe, kd = _bf(e["qa"]), _bf(e["ka"]), _bf(e["qe"]), _bf(e["kd"])
            a = [_bf(jnp.where(causal, _mm_nt(_head(qa, k), _head(ka, k)), 0.0)) for k in heads]
            o_inter = [_mm_nt(_head(qe, k), _bf(sts[k])) for k in heads]
            kv = [_mm_tn(_head(v, k), _head(kd, k)) for k in heads]
            o_intra = [_mm(a[k], _head(v, k)) for k in heads]
            for k in heads:
                st_ref[cc, k] = sts[k]
                o_ref[rows, k * HD:(k + 1) * HD] = o_inter[k] + o_intra[k]
                s_scr[k] = sts[k] * _head(e["ebl"], k) + kv[k]
            return carry

        lax.fori_loop(0, ncb, chunk, 0, unroll=4)

    return pl.pallas_call(
        body, grid=(HEADS // hg, t // rb),
        in_specs=[pl.BlockSpec((rb, 3 * HD * hg), lambda h, i: (i, h)), pl.BlockSpec((2, HD * hg), lambda h, i: (0, h))],
        out_specs=[pl.BlockSpec((rb, HD * hg), lambda h, i: (i, h)),
                   pl.BlockSpec((ncb, hg, HD, HD), lambda h, i: (i, h, 0, 0))],
        out_shape=[jax.ShapeDtypeStruct((t, D), F32), jax.ShapeDtypeStruct((t // CH, HEADS, HD, HD), F32)],
        scratch_shapes=[pltpu.VMEM((hg, HD, HD), F32)],
        name="hgrn_fwd", compiler_params=_cp(48),
    )(proj, lbw)


def _hgrn_bwd(proj, lbw, do_raw, states, rb, hg):
    assert hg == HEADS
    t = proj.shape[0]
    nblk = t // rb
    ncb = rb // CH
    wd = HD * hg

    def body(p_ref, lb_ref, do_ref, st_ref, dp_ref, dlb_ref, ds_scr):
        @pl.when(pl.program_id(1) == 0)
        def _():
            ds_scr[...] = jnp.zeros_like(ds_scr)
            dlb_ref[...] = jnp.zeros_like(dlb_ref)

        lb = _lower_bound(lb_ref)
        causal = _tri(True)
        tril = _tri3(True)
        triu = _tri3(False)
        last_row = lax.broadcasted_iota(jnp.int32, (CH, HD * hg), 0) == CH - 1
        row0 = lax.broadcasted_iota(jnp.int32, (8, HD * hg), 0) == 0
        heads = range(hg)
        wide = lambda parts: jnp.concatenate(parts, axis=1)

        def chunk(it, carry):
            cc = ncb - 1 - it
            r0 = pl.multiple_of(cc * CH, CH)
            rows = pl.ds(r0, CH)
            hq = _head_segment(p_ref, rows, 0, hg)
            e = _hgrn_chunk_fwd(hq, _head_segment(p_ref, rows, 1, hg), lb, tril)
            v = _bf(_head_segment(p_ref, rows, 2, hg))
            do = _bf(do_ref[rows, :])
            sts = [st_ref[cc, k] for k in heads]
            dsts = [ds_scr[k] for k in heads]
            dlb_acc = dlb_ref[...]
            qa, ka, qe, kd = _bf(e["qa"]), _bf(e["ka"]), _bf(e["qe"]), _bf(e["kd"])
            a = [_bf(jnp.where(causal, _mm_nt(_head(qa, k), _head(ka, k)), 0.0)) for k in heads]
            da = [_bf(jnp.where(causal, _mm_nt(_head(do, k), _head(v, k)), 0.0)) for k in heads]
            dqe = wide([_mm(_head(do, k), _bf(sts[k])) for k in heads])
            dkd = wide([_mm(_head(v, k), _bf(dsts[k])) for k in heads])
            dv_state = [_mm_nt(_head(kd, k), _bf(dsts[k])) for k in heads]
            ds_new = [_mm_tn(_head(do, k), _head(qe, k)) for k in heads]
            dv_intra = [_mm_tn(a[k], _head(do, k)) for k in heads]
            dqa = wide([_mm(da[k], _head(ka, k)) for k in heads])
            dka = wide([_mm_tn(da[k], _head(qa, k)) for k in heads])
            dv = wide([dv_intra[k] + dv_state[k] for k in heads])
            dbl = e["ebl"] * wide([jnp.sum(sts[k] * dsts[k], axis=0, keepdims=True) for k in heads])
            dq = dqe * e["eb"] + dqa * e["ea"]
            dk = dka * e["ek"] + dkd * e["ed"]
            dkd_kd = dkd * kd.astype(F32)
            db = dqe * qe.astype(F32) + dqa * qa.astype(F32) - dka * ka.astype(F32) - dkd_kd
            db = db + jnp.where(last_row, dbl + jnp.sum(dkd_kd, axis=0, keepdims=True), 0.0)
            dg = _mm_tri_exact(triu, db)
            df = dg / e["f"] - dk
            sg = e["sg"]
            sq = e["sq"]
            dhq = _bf(dq * (sq * (1.0 + hq * (1.0 - sq))))
            dhf = _bf(df * (1.0 - lb) * sg * (1.0 - sg))
            dhi = _bf(dv)
            dlb_new = dlb_acc + jnp.where(row0, jnp.sum(df * (1.0 - sg), axis=0, keepdims=True), 0.0)
            for k in heads:
                ds_scr[k] = ds_new[k] + dsts[k] * _head(e["ebl"], k)
            dp_ref[rows, 0:wd] = dhq
            dp_ref[rows, wd:2 * wd] = dhf
            dp_ref[rows, 2 * wd:3 * wd] = dhi
            dlb_ref[...] = dlb_new
            return carry

        lax.fori_loop(0, ncb, chunk, 0, unroll=2)

    rev = lambda h, i: (nblk - 1 - i, h)
    return pl.pallas_call(
        body, grid=(HEADS // hg, nblk),
        in_specs=[pl.BlockSpec((rb, 3 * HD * hg), rev), pl.BlockSpec((2, HD * hg), lambda h, i: (0, h)),
                  pl.BlockSpec((rb, HD * hg), rev), pl.BlockSpec((ncb, hg, HD, HD), lambda h, i: (nblk - 1 - i, h, 0, 0))],
        out_specs=[pl.BlockSpec((rb, 3 * HD * hg), rev), pl.BlockSpec((8, HD * hg), lambda h, i: (0, h))],
        out_shape=[jax.ShapeDtypeStruct((t, 3 * D), BF16), jax.ShapeDtypeStruct((8, D), F32)],
        scratch_shapes=[pltpu.VMEM((hg, HD, HD), F32)],
        name="hgrn_bwd", compiler_params=_cp(48),
    )(proj, lbw, do_raw, states)


def _kv_variants(tile, odd):
    low = lax.broadcasted_iota(jnp.int32, tile.shape, 1) < 64
    if odd:
        hi = jnp.where(low, 0.0, tile)
        lo = pltpu.roll(hi, 64, 1)
    else:
        lo = jnp.where(low, tile, 0.0)
        hi = pltpu.roll(lo, 64, 1)
    return _bf(lo), _bf(hi)


def _attn_masks(n):
    qi = lax.broadcasted_iota(jnp.int32, (AB, AB), 0)
    kj = lax.broadcasted_iota(jnp.int32, (AB, AB), 1)
    cur = kj <= qi
    return cur, cur | (n > 0), qi <= kj


def _kv_all(prev_ref, cur_ref):
    out = []
    for tl in range(2):
        cols = slice(tl * 128, (tl + 1) * 128)
        tile = jnp.concatenate([prev_ref[:, cols], cur_ref[:, cols]], axis=0)
        out.append(_kv_variants(tile, 0))
        out.append(_kv_variants(tile, 1))
    return out


def _window(a2, cur):
    return jnp.where(cur, a2[:, AB:], a2[:, :AB])


def _attn_softmax(scores, sinks, cur, ok):
    s = [jnp.where(ok, _window(s2, cur) * ATT_SCALE, NEG) for s2 in scores]
    m = [jnp.maximum(jnp.max(si, axis=-1, keepdims=True), sink) for si, sink in zip(s, sinks)]
    p = [jnp.exp(si - mi) for si, mi in zip(s, m)]
    es = [jnp.exp(sink - mi) for sink, mi in zip(sinks, m)]
    inv = [1.0 / (jnp.sum(pi, axis=-1, keepdims=True) + ei) for pi, ei in zip(p, es)]
    return [pi * ii for pi, ii in zip(p, inv)], [ei * ii for ei, ii in zip(es, inv)]


def _spread(pc, cur):
    return jnp.concatenate([jnp.where(cur, 0.0, pc), jnp.where(cur, pc, 0.0)], axis=1)


def _spread_t(pct, cur_t):
    return jnp.concatenate([jnp.where(cur_t, 0.0, pct), jnp.where(cur_t, pct, 0.0)], axis=0)


def _attn_fwd(proj, sinks):
    t = proj.shape[0]
    nb = t // AB

    nsub = 2
    assert nb % nsub == 0

    def body(q_ref, kc_ref, kp_ref, vc_ref, vp_ref, sink_ref, o_ref):
        sinks_v = [sink_ref[0, h] for h in range(QH)]
        heads = [(j, ab) for j in range(8) for ab in range(2)]
        for sb in range(nsub):
            rows = pl.ds(AB * sb, AB)
            before = pl.ds(AB * (sb - 1), AB)
            cur, ok, _ = _attn_masks(nsub * pl.program_id(0) + sb)
            kvars = _kv_all(kp_ref if sb == 0 else kc_ref.at[before, :], kc_ref.at[rows, :])
            vvars = _kv_all(vp_ref if sb == 0 else vc_ref.at[before, :], vc_ref.at[rows, :])
            qps = [_bf(q_ref[rows, 128 * j:128 * (j + 1)]) for j in range(8)]
            scores = [_mm_nt(qps[j], kvars[j // 2][ab]) for j, ab in heads]
            pcs, _ = _attn_softmax(scores, sinks_v, cur, ok)
            parts = [_mm(_bf(_spread(pcs[h], cur)), vvars[j // 2][ab]) for h, (j, ab) in enumerate(heads)]
            for j in range(8):
                o_ref[rows, 128 * j:128 * (j + 1)] = parts[2 * j] + parts[2 * j + 1]

    prev = lambda n: jnp.maximum(nsub * n - 1, 0)
    step = nsub * AB
    return pl.pallas_call(
        body, grid=(nb // nsub,),
        in_specs=[pl.BlockSpec((step, D), lambda n: (n, C_AQ // D)),
                  pl.BlockSpec((step, 256), lambda n: (n, C_AK // 256)),
                  pl.BlockSpec((AB, 256), lambda n: (prev(n), C_AK // 256)),
                  pl.BlockSpec((step, 256), lambda n: (n, C_AV // 256)),
                  pl.BlockSpec((AB, 256), lambda n: (prev(n), C_AV // 256)),
                  pl.BlockSpec(memory_space=pltpu.SMEM)],
        out_specs=pl.BlockSpec((step, D), lambda n: (n, 0)),
        out_shape=jax.ShapeDtypeStruct((t, D), F32),
        name="attn_fwd", compiler_params=_cp(32),
    )(proj, proj, proj, proj, proj, sinks)


def _attn_bwd(proj, sinks, do_a):
    t = proj.shape[0]
    nb = t // AB
    nsub = 2
    assert nb % nsub == 0
    steps = nb // nsub
    step = nsub * AB

    def body(q_ref, kc_ref, kp_ref, vc_ref, vp_ref, do_ref, sink_ref, dq_ref, dkv_ref, dsink_ref, carry):
        n = pl.program_id(0)

        @pl.when(n == 0)
        def _():
            dsink_ref[...] = jnp.zeros_like(dsink_ref)
            carry[...] = jnp.zeros_like(carry)

        def one_block(sb):
            rows = pl.ds(AB * sb, AB)
            before = pl.ds(AB * (sb - 1), AB)
            cur, ok, cur_t = _attn_masks(nsub * n + sb)
            low = lax.broadcasted_iota(jnp.int32, (2 * AB, 128), 1) < 64
            lane = lax.broadcasted_iota(jnp.int32, (8, 128), 1)
            row0 = lax.broadcasted_iota(jnp.int32, (8, 128), 0) == 0
            kvars = _kv_all(kp_ref if sb == 0 else kc_ref.at[before, :], kc_ref.at[rows, :])
            vvars = _kv_all(vp_ref if sb == 0 else vc_ref.at[before, :], vc_ref.at[rows, :])
            qps = [_bf(q_ref[rows, 128 * j:128 * (j + 1)]) for j in range(8)]
            dops = [_bf(do_ref[rows, 128 * j:128 * (j + 1)]) for j in range(8)]
            heads = [(j, ab) for j in range(8) for ab in range(2)]
            scores = [_mm_nt(qps[j], kvars[j // 2][ab]) for j, ab in heads]
            dps = [_mm_nt(dops[j], vvars[j // 2][ab]) for j, ab in heads]
            pcs, pss = _attn_softmax(scores, [sink_ref[0, h] for h in range(QH)], cur, ok)
            dpcs = [_window(dp2, cur) for dp2 in dps]
            rss = [jnp.sum(pc * dpc, axis=-1, keepdims=True) for pc, dpc in zip(pcs, dpcs)]
            dscs = [pc * (dpc - rs) for pc, dpc, rs in zip(pcs, dpcs, rss)]
            dsink = jnp.zeros((8, 128), F32)
            for h in range(QH):
                dsink = dsink + jnp.where(row0 & (lane == h), -jnp.sum(pss[h] * rss[h]), 0.0)
            dq_terms = [_mm(_bf(_spread(dscs[h], cur)), kvars[j // 2][ab]) for h, (j, ab) in enumerate(heads)]
            for j in range(8):
                dq_ref[rows, 128 * j:128 * (j + 1)] = _bf((dq_terms[2 * j] + dq_terms[2 * j + 1]) * ATT_SCALE)
            dsc_t = [_bf(_spread_t(dsc.T, cur_t)) for dsc in dscs]
            pc_t = [_bf(_spread_t(pc.T, cur_t)) for pc in pcs]
            dk_terms = [_mm(dsc_t[h], qps[j]) for h, (j, ab) in enumerate(heads)]
            dv_terms = [_mm(pc_t[h], dops[j]) for h, (j, ab) in enumerate(heads)]
            dk_ab = [[dk_terms[4 * g + ab] + dk_terms[4 * g + 2 + ab] for ab in range(2)] for g in range(4)]
            dv_ab = [[dv_terms[4 * g + ab] + dv_terms[4 * g + 2 + ab] for ab in range(2)] for g in range(4)]
            dkts, dvts = [], []
            for tl in range(2):
                ke, ko = dk_ab[2 * tl], dk_ab[2 * tl + 1]
                ve, vo = dv_ab[2 * tl], dv_ab[2 * tl + 1]
                dkts.append((jnp.where(low, ke[0], 0.0) + pltpu.roll(jnp.where(low, 0.0, ke[1]), 64, 1)
                             + jnp.where(low, 0.0, ko[1]) + pltpu.roll(jnp.where(low, ko[0], 0.0), 64, 1)) * ATT_SCALE)
                dvts.append(jnp.where(low, ve[0], 0.0) + pltpu.roll(jnp.where(low, 0.0, ve[1]), 64, 1)
                            + jnp.where(low, 0.0, vo[1]) + pltpu.roll(jnp.where(low, vo[0], 0.0), 64, 1))
            return dkts, dvts, dsink

        @pl.when(n < steps)
        def _():
            (dk0, dv0, ds0), (dk1, dv1, ds1) = one_block(0), one_block(1)
            first, second = slice(0, AB), slice(AB, 2 * AB)
            for tl in range(2):
                for cols, g0, g1 in ((slice(tl * 128, (tl + 1) * 128), dk0[tl], dk1[tl]),
                                     (slice(256 + tl * 128, 256 + (tl + 1) * 128), dv0[tl], dv1[tl])):
                    dkv_ref[first, cols] = _bf(carry[first, cols])
                    dkv_ref[second, cols] = _bf(carry[second, cols] + g0[first])
                    carry[first, cols] = g0[second] + g1[first]
                    carry[second, cols] = g1[second]
            dsink_ref[...] += ds0 + ds1

        @pl.when(n == steps)
        def _():
            dkv_ref[...] = _bf(carry[...])

    cur = lambda n: jnp.minimum(n, steps - 1)
    last = lambda n: jnp.clip(n - 1, 0, steps - 1)
    prev = lambda n: jnp.clip(nsub * n - 1, 0, nb - 1)
    return pl.pallas_call(
        body, grid=(steps + 1,),
        in_specs=[pl.BlockSpec((step, D), lambda n: (cur(n), C_AQ // D)),
                  pl.BlockSpec((step, 256), lambda n: (cur(n), C_AK // 256)),
                  pl.BlockSpec((AB, 256), lambda n: (prev(n), C_AK // 256)),
                  pl.BlockSpec((step, 256), lambda n: (cur(n), C_AV // 256)),
                  pl.BlockSpec((AB, 256), lambda n: (prev(n), C_AV // 256)),
                  pl.BlockSpec((step, D), lambda n: (cur(n), 0)),
                  pl.BlockSpec(memory_space=pltpu.SMEM)],
        out_specs=[pl.BlockSpec((step, D), lambda n: (cur(n), 0)),
                   pl.BlockSpec((step, 512), lambda n: (last(n), 0)),
                   pl.BlockSpec((8, 128), lambda n: (0, 0))],
        out_shape=[jax.ShapeDtypeStruct((t, D), BF16), jax.ShapeDtypeStruct((t, 512), BF16),
                   jax.ShapeDtypeStruct((8, 128), F32)],
        scratch_shapes=[pltpu.VMEM((step, 512), F32)],
        name="attn_bwd", compiler_params=_cp(40),
    )(proj, proj, proj, proj, proj, do_a, sinks)


def _silu_and_grad(v):
    s = _sigmoid(v)
    return v * s, s * (1.0 + v * (1.0 - s))


def _tail(o_raw, o_a, proj, x2d, tgt, wbh, wba, wout, hnw, fnw, tb):
    t = x2d.shape[0]

    def body(or_ref, oa_ref, hg_ref, ag0, ag1, mh0, mh1, ma0, ma1, x_ref, t_ref, wbh_ref, wba_ref, wout_ref, hnw_ref,
             fnw_ref, dx2_ref, dor_ref, doa_ref, dhg_ref, dagm_ref, gh_ref, ga_ref, mg_ref, dyh_ref, dya_ref, dx2b_ref,
             sums_ref):
        @pl.when(pl.program_id(0) == 0)
        def _():
            sums_ref[...] = jnp.zeros_like(sums_ref)

        halves = lambda a, b: jnp.concatenate([a[...], b[...]], axis=1)
        hnw_v = hnw_ref[...]
        fnw_v = fnw_ref[...]
        o = or_ref[...]
        rs, xhs = [], []
        for h in range(HEADS):
            oh = o[:, h * HD:(h + 1) * HD]
            r = lax.rsqrt(jnp.mean(oh * oh, axis=-1, keepdims=True) + EPS)
            rs.append(r)
            xhs.append(oh * r)
        xh = jnp.concatenate(xhs, axis=1)
        on = xh * hnw_v
        sil_hg, dsil_hg = _silu_and_grad(hg_ref[...])
        gh_b = _bf(on * sil_hg)
        y_h = _mm(gh_b, wbh_ref[...])
        oa = oa_ref[...]
        sil_ag, dsil_ag = _silu_and_grad(halves(ag0, ag1))
        ga_b = _bf(oa * sil_ag)
        y_a = _mm(ga_b, wba_ref[...])
        s_mh = _sigmoid(halves(mh0, mh1))
        s_ma = _sigmoid(halves(ma0, ma1))
        mg_b = _bf(s_mh * y_h + s_ma * y_a)
        x2 = x_ref[...] + _mm(mg_b, wout_ref[...])
        r2 = lax.rsqrt(jnp.mean(x2 * x2, axis=-1, keepdims=True) + EPS)
        xh2 = x2 * r2
        err = xh2 * fnw_v - t_ref[...]
        loss = 0.5 * jnp.sum(jnp.mean(err * err, axis=-1, keepdims=True))
        dy = err * (1.0 / D)
        dfnw = jnp.sum(dy * xh2, axis=0, keepdims=True)
        dxh2 = dy * fnw_v
        dx2 = r2 * (dxh2 - xh2 * jnp.mean(dxh2 * xh2, axis=-1, keepdims=True))
        dx2_ref[...] = dx2
        dx2_b = _bf(dx2)
        dmg = _mm_nt(dx2_b, wout_ref[...])
        dmg_h = dmg * s_mh
        dmg_a = dmg * s_ma
        dyh_b = _bf(dmg_h)
        dya_b = _bf(dmg_a)
        dagm_ref[:, D:2 * D] = _bf(dmg_h * y_h * (1.0 - s_mh))
        dagm_ref[:, 2 * D:3 * D] = _bf(dmg_a * y_a * (1.0 - s_ma))
        dgh = _mm_nt(dyh_b, wbh_ref[...])
        dga = _mm_nt(dya_b, wba_ref[...])
        doa_ref[...] = dga * sil_ag
        dagm_ref[:, 0:D] = _bf(dga * oa * dsil_ag)
        dhg_ref[...] = _bf(dgh * on * dsil_hg)
        don = dgh * sil_hg
        dhnw = jnp.sum(don * xh, axis=0, keepdims=True)
        dxh = don * hnw_v
        dos = []
        for h in range(HEADS):
            sl = slice(h * HD, (h + 1) * HD)
            dos.append(rs[h] * (dxh[:, sl] - xhs[h] * jnp.mean(dxh[:, sl] * xhs[h], axis=-1, keepdims=True)))
        dor_ref[...] = jnp.concatenate(dos, axis=1)
        gh_ref[...] = gh_b
        ga_ref[...] = ga_b
        mg_ref[...] = mg_b
        dyh_ref[...] = dyh_b
        dya_ref[...] = dya_b
        dx2b_ref[...] = dx2_b
        row = lax.broadcasted_iota(jnp.int32, (8, D), 0)
        sums_ref[...] += jnp.where(row == 0, dfnw, 0.0) + jnp.where(row == 1, dhnw, 0.0) + jnp.where(row == 2, loss, 0.0)

    rowblk = lambda c: pl.BlockSpec((tb, D), lambda i: (i, c))
    half = lambda c: pl.BlockSpec((tb, 512), lambda i: (i, c))
    full = lambda shape: pl.BlockSpec(shape, lambda i: (0, 0))
    return pl.pallas_call(
        body, grid=(t // tb,),
        in_specs=[rowblk(0), rowblk(0), rowblk(C_HG // D), half(C_AG // 512), half(C_AG // 512 + 1), half(C_MH // 512),
                  half(C_MH // 512 + 1), half(C_MA // 512), half(C_MA // 512 + 1), rowblk(0), rowblk(0),
                  full((D, D)), full((D, D)), full((D, D)), full((1, D)), full((1, D))],
        out_specs=[rowblk(0), rowblk(0), rowblk(0), rowblk(0), pl.BlockSpec((tb, 3 * D), lambda i: (i, 0))]
        + [rowblk(0)] * 6 + [full((8, D))],
        out_shape=[jax.ShapeDtypeStruct((t, D), F32)] * 3
        + [jax.ShapeDtypeStruct((t, D), BF16), jax.ShapeDtypeStruct((t, 3 * D), BF16)]
        + [jax.ShapeDtypeStruct((t, D), BF16)] * 6 + [jax.ShapeDtypeStruct((8, D), F32)],
        name="tail", compiler_params=_cp(56),
    )(o_raw, o_a, proj, proj, proj, proj, proj, proj, proj, x2d, tgt, wbh, wba, wout, hnw, fnw)


def _wgrad3(gh, dyh, ga, dya, mg, dx2b, tk, after):
    t = dyh.shape[0]

    def body(a0, b0, a1, b1, a2, b2, _, o0, o1, o2):
        @pl.when(pl.program_id(0) == 0)
        def _():
            o0[...] = jnp.zeros_like(o0)
            o1[...] = jnp.zeros_like(o1)
            o2[...] = jnp.zeros_like(o2)

        o0[...] += _mm_tn(a0[...], b0[...])
        o1[...] += _mm_tn(a1[...], b1[...])
        o2[...] += _mm_tn(a2[...], b2[...])

    blk = pl.BlockSpec((tk, D), lambda k: (k, 0))
    out = pl.BlockSpec((D, D), lambda k: (0, 0))
    return pl.pallas_call(
        body, grid=(t // tk,), in_specs=[blk] * 6 + [pl.BlockSpec(memory_space=pl.ANY)], out_specs=[out] * 3,
        out_shape=[jax.ShapeDtypeStruct((D, D), F32)] * 3,
        name="wgrad3", compiler_params=_cp(48),
    )(gh, dyh, ga, dya, mg, dx2b, after)


def _inproj_wgrad(xnt, pieces, nb, col0, dw, name):
    t = xnt.shape[1]
    counts = [p.shape[1] // nb for p in pieces]
    firsts = [sum(counts[:k]) for k in range(len(pieces))]
    assert col0 % nb == 0

    def body(xnt_ref, *refs):
        o_ref = refs[-1]
        j = pl.program_id(0)
        for first, count, p_ref in zip(firsts, counts, refs):
            @pl.when((j >= first) & (j < first + count))
            def _(p_ref=p_ref):
                o_ref[...] = _mm(xnt_ref[...], p_ref[...]).astype(BF16)

    def piece_spec(first, count):
        return pl.BlockSpec((t, nb), lambda j: (0, jnp.clip(j - first, 0, count - 1)))

    carried = [] if dw is None else [dw]
    return pl.pallas_call(
        body, grid=(sum(counts),),
        in_specs=[pl.BlockSpec((D, t), lambda j: (0, 0), pipeline_mode=pl.Buffered(1))]
        + [piece_spec(f, c) for f, c in zip(firsts, counts)] + [pl.BlockSpec(memory_space=pl.ANY)] * len(carried),
        out_specs=pl.BlockSpec((D, nb), lambda j: (0, j + col0 // nb)),
        out_shape=jax.ShapeDtypeStruct((D, DIN), BF16),
        input_output_aliases={1 + len(pieces): 0} if carried else {},
        name=name, compiler_params=_cp(56),
    )(xnt, *pieces, *carried)


def _inproj_dgrad(pieces, w_p, x2d, dx2, norm_w, tb, after):
    t = x2d.shape[0]

    def body(*refs):
        piece_refs = refs[:len(pieces)]
        w_ref, x_ref, dx2_ref, nw_ref, _, gx_ref, dnw_ref = refs[len(pieces):]

        @pl.when(pl.program_id(0) == 0)
        def _():
            dnw_ref[...] = jnp.zeros_like(dnw_ref)

        dxn = None
        off = 0
        for p in piece_refs:
            width = p.shape[1]
            for q in range(w_ref.shape[0]):
                lo, hi = max(off, q * PAIR), min(off + width, (q + 1) * PAIR)
                if lo < hi:
                    term = _mm_nt(p[:, lo - off:hi - off], w_ref[q, :, lo - q * PAIR:hi - q * PAIR])
                    dxn = term if dxn is None else dxn + term
            off += width
        xv = x_ref[...]
        r = lax.rsqrt(jnp.mean(xv * xv, axis=-1, keepdims=True) + EPS)
        xh = xv * r
        dxh = dxn * nw_ref[...]
        gx_ref[...] = dx2_ref[...] + r * (dxh - xh * jnp.mean(dxh * xh, axis=-1, keepdims=True))
        row0 = lax.broadcasted_iota(jnp.int32, (8, D), 0) == 0
        dnw_ref[...] += jnp.where(row0, jnp.sum(dxn * xh, axis=0, keepdims=True), 0.0)

    rowblk = pl.BlockSpec((tb, D), lambda i: (i, 0))
    return pl.pallas_call(
        body, grid=(t // tb,),
        in_specs=[pl.BlockSpec((tb, p.shape[1]), lambda i: (i, 0)) for p in pieces]
        + [pl.BlockSpec(w_p.shape, lambda i: (0, 0, 0), pipeline_mode=pl.Buffered(1)), rowblk, rowblk,
           pl.BlockSpec((1, D), lambda i: (0, 0)), pl.BlockSpec(memory_space=pl.ANY)],
        out_specs=[rowblk, pl.BlockSpec((8, D), lambda i: (0, 0))],
        out_shape=[jax.ShapeDtypeStruct((t, D), F32), jax.ShapeDtypeStruct((8, D), F32)],
        name="inproj_dgrad", compiler_params=_cp(60),
    )(*pieces, w_p, x2d, dx2, norm_w, after)


def _adamw_math(w, g, m, v):
    m = B1 * m + (1.0 - B1) * g
    v = B2 * v + (1.0 - B2) * (g * g)
    m_hat = m / (1.0 - B1 ** STEP)
    v_hat = v / (1.0 - B2 ** STEP)
    delta = -LR * (m_hat / (jnp.sqrt(v_hat) + ADAM_EPS) + WD * w)
    return delta, m, v


def _adamw_shard(recv, sums, chip, w, m, v, rows, name):
    nparts, nr, nc = recv.shape

    def body(chip_ref, own_ref, p_ref, w_ref, m_ref, v_ref, g_ref, d_ref, nm_ref, nv_ref):
        g = own_ref[0].astype(F32)
        for s in range(nparts):
            g = g + p_ref[s].astype(F32)
        d, nm, nv = _adamw_math(w_ref[...], g, m_ref[...], v_ref[...])
        g_ref[...] = g
        d_ref[...] = d
        nm_ref[...] = nm
        nv_ref[...] = nv

    blk = pl.BlockSpec((rows, nc), lambda i, chip_ref: (i, 0))
    return pl.pallas_call(
        body,
        grid_spec=pltpu.PrefetchScalarGridSpec(
            num_scalar_prefetch=1, grid=(nr // rows,),
            in_specs=[pl.BlockSpec((1, rows, nc), lambda i, chip_ref: (chip_ref[0], i, 0)),
                      pl.BlockSpec((nparts, rows, nc), lambda i, chip_ref: (0, i, 0)), blk, blk, blk],
            out_specs=[blk] * 4),
        out_shape=[jax.ShapeDtypeStruct((nr, nc), F32)] * 4,
        name=name, compiler_params=_cp(48),
    )(chip, sums, recv, w, m, v)


def _adamw_sum8(parts, w, m, v, after, name):
    def body(p_ref, w_ref, m_ref, v_ref, _, g_ref, d_ref, nm_ref, nv_ref):
        g = p_ref[0].astype(F32)
        for s in range(1, NDEV):
            g = g + p_ref[s].astype(F32)
        d, nm, nv = _adamw_math(w_ref[...], g, m_ref[...], v_ref[...])
        g_ref[...] = g
        d_ref[...] = d
        nm_ref[...] = nm
        nv_ref[...] = nv

    vm = pl.BlockSpec(memory_space=pltpu.VMEM)
    return pl.pallas_call(
        body, out_shape=[jax.ShapeDtypeStruct(w.shape, F32)] * 4,
        in_specs=[vm, vm, vm, vm, pl.BlockSpec(memory_space=pl.ANY)], out_specs=[vm] * 4, name=name,
    )(parts, w, m, v, after)


SMALL_ROWS = dict(norm_w=0, lower_bound=1, hgrn_norm_w=3, final_norm_w=4, sinks=5, loss=6)


def _pack_small_grads(dnw, dlb, sums, dsink):
    def body(dnw_ref, dlb_ref, sums_ref, dsink_ref, o_ref):
        o_ref[...] = jnp.zeros_like(o_ref)
        o_ref[0:1, :] = dnw_ref[0:1, :]
        o_ref[1:2, :] = dlb_ref[0:1, :]
        o_ref[3:4, :] = sums_ref[1:2, :]
        o_ref[4:5, :] = sums_ref[0:1, :]
        o_ref[5:6, 0:128] = dsink_ref[0:1, :]
        o_ref[6:7, :] = sums_ref[2:3, :]

    return pl.pallas_call(body, out_shape=jax.ShapeDtypeStruct((8, D), F32), name="pack_small_grads")(dnw, dlb, sums, dsink)


def _adamw_small(parts, ws, ms, vs):
    shapes = [a.shape for a in ws]

    def body(p_ref, *refs):
        w, m, v = refs[0:5], refs[5:10], refs[10:15]
        outs = [refs[15 + 5 * i:20 + 5 * i] for i in range(4)]
        loss_ref = refs[35]

        def total(row, width):
            g = p_ref[0, row:row + 1, 0:width]
            for s in range(1, NDEV):
                g = g + p_ref[s, row:row + 1, 0:width]
            return g

        loss_ref[...] = total(6, 128)
        lb = _lower_bound(w[1])
        ga0 = total(1, D) * lb * (1.0 - lb)
        grads = [total(0, D), None, total(3, D), total(4, D), total(5, QH)]
        for i in (0, 2, 3, 4):
            res = (grads[i],) + _adamw_math(w[i][...], grads[i], m[i][...], v[i][...])
            for o, val in zip(outs, res):
                o[i][...] = val
        for r, g in ((0, ga0), (1, -ga0)):
            res = (g,) + _adamw_math(w[1][r:r + 1, :], g, m[1][r:r + 1, :], v[1][r:r + 1, :])
            for o, val in zip(outs, res):
                o[1][r:r + 1, :] = val

    res = pl.pallas_call(
        body, out_shape=[jax.ShapeDtypeStruct(s, F32) for s in shapes] * 4 + [jax.ShapeDtypeStruct((1, 128), F32)],
        name="adamw_small",
    )(parts, *ws, *ms, *vs)
    return [res[5 * i:5 * i + 5] for i in range(4)], res[20][0, 0]


def kernel(x, norm_w, w_in, hgrn_lower_bound, hgrn_norm_w, w_branch_hgrn, attn_sinks, w_branch_attn, w_out, final_norm_w, loss_target, m_norm_w, m_w_in, m_hgrn_lower_bound, m_hgrn_norm_w, m_w_branch_hgrn, m_attn_sinks, m_w_branch_attn, m_w_out, m_final_norm_w, v_norm_w, v_w_in, v_hgrn_lower_bound, v_hgrn_norm_w, v_w_branch_hgrn, v_attn_sinks, v_w_branch_attn, v_w_out, v_final_norm_w):
    t = x.shape[1]
    x2d = x.reshape(t, D)
    tgt = loss_target.reshape(t, D)
    fnw = final_norm_w.reshape(1, D)
    row_blk = min(256, t)
    big_blk = min(512, t)

    chip = (2 * lax.axis_index("x") + lax.axis_index("y")).astype(jnp.int32).reshape(1)
    w_p, xn, xnt, proj_own = _gather_in_projection(w_in[0], x2d, norm_w)
    wbh, wba, wout = (g.reshape(D, D) for g in _gather_square(
        [w_branch_hgrn[0].astype(BF16), w_branch_attn[0].astype(BF16), w_out[0].astype(BF16)], after=w_p))

    proj = _inproj_fwd(xn, w_p, proj_own, chip, min(1024, t))
    o_raw, states = _hgrn_fwd(proj, hgrn_lower_bound, big_blk, HGRN_GROUP)
    o_a = _attn_fwd(proj, attn_sinks)
    (dx2, do_raw, do_a, d_hg, d_agm, gh, ga, mg, dyh, dya, dx2b, sums) = _tail(
        o_raw, o_a, proj, x2d, tgt, wbh, wba, wout, hgrn_norm_w, fnw, row_blk)
    d_aq, d_kv, dsink = _attn_bwd(proj, attn_sinks, do_a)
    d_hgrn, dlb = _hgrn_bwd(proj, hgrn_lower_bound, do_raw, states, big_blk, HGRN_GROUP)
    pieces = (d_hgrn, d_hg, d_aq, d_kv, d_agm)
    dw_cat = _inproj_wgrad(xnt, [d_hgrn], CB, 0, None, "inproj_wgrad_hgrn")
    dw_cat = _inproj_wgrad(xnt, [d_hg, d_aq, d_kv], CB // 2, C_HG, dw_cat, "inproj_wgrad_mid")
    dw_cat = _inproj_wgrad(xnt, [d_agm], CB, C_AG, dw_cat, "inproj_wgrad_gates")

    dwin_r = dw_cat.reshape(D, NDEV, IN_SHARD).transpose(1, 0, 2)
    dwbh, dwba, dwout = _wgrad3(gh, dyh, ga, dya, mg, dx2b, big_blk, after=dw_cat)
    slots = lambda a: a.reshape(NDEV, ROW_SHARD, D).astype(BF16)
    got = _exchange_pair([dwin_r], after=dwbh)
    core = lax.axis_index("c").astype(jnp.int32).reshape(1)
    s_in = _pair_sum(dwin_r, got[0], core, 4 * ROW_SHARD, "pair_sum_w_in")
    rin, = _exchange_chips([s_in])
    rbh, rba, rout = _exchange_square([slots(dwbh), slots(dwba), slots(dwout)])
    grad_x, dnw = _inproj_dgrad(pieces, w_p, x2d, dx2, norm_w, big_blk, after=s_in)
    rsm = _exchange_small(_pack_small_grads(dnw, dlb, sums, dsink))
    g_in, d_in, nm_in, nv_in = _adamw_shard(rin, s_in, chip, w_in[0], m_w_in[0], v_w_in[0], 128, "adamw_w_in")
    g_bh, d_bh, nm_bh, nv_bh = _adamw_sum8(
        rbh, w_branch_hgrn[0], m_w_branch_hgrn[0], v_w_branch_hgrn[0], dnw, "adamw_w_bh")
    g_ba, d_ba, nm_ba, nv_ba = _adamw_sum8(
        rba, w_branch_attn[0], m_w_branch_attn[0], v_w_branch_attn[0], dnw, "adamw_w_ba")
    g_out, d_out, nm_out, nv_out = _adamw_sum8(rout, w_out[0], m_w_out[0], v_w_out[0], dnw, "adamw_w_out")
    (sg, sd, sm, sv), loss = _adamw_small(
        rsm,
        (norm_w, hgrn_lower_bound, hgrn_norm_w, fnw, attn_sinks),
        (m_norm_w, m_hgrn_lower_bound, m_hgrn_norm_w, m_final_norm_w.reshape(1, D), m_attn_sinks),
        (v_norm_w, v_hgrn_lower_bound, v_hgrn_norm_w, v_final_norm_w.reshape(1, D), v_attn_sinks))

    def group(s, w_in_v, bh, ba, out):
        nw, lb, hnw, fn, sinks = s
        return (nw, w_in_v[None], lb, hnw, bh[None], sinks, ba[None], out[None], fn.reshape(D))

    return (loss, grad_x.reshape(1, t, D),
            *group(sg, g_in, g_bh, g_ba, g_out), *group(sd, d_in, d_bh, d_ba, d_out),
            *group(sm, nm_in, nm_bh, nm_ba, nm_out), *group(sv, nv_in, nv_bh, nv_ba, nv_out))
```

```python
import jax
import jax.numpy as jnp
from jax import lax
from jax.experimental import pallas as pl
from jax.experimental.pallas import tpu as pltpu
from jax.experimental.pallas import tpu_sc as plsc

F32 = jnp.float32
BF16 = jnp.bfloat16

D = 1024
DIN = 8704
NDEV = 8
IN_SHARD = DIN // NDEV
PAIR = 2 * IN_SHARD
ROW_SHARD = D // NDEV
HEADS = 8
HD = 128
CH = 64
HGRN_GROUP = 8
QH = 16
AB = 128
EPS = 1e-6
NEG = -1e30
ATT_SCALE = 0.125

C_HG = 3072
C_AQ = 4096
C_AK = 5120
C_AV = 5376
C_AG = 5632
C_MH = 6656
C_MA = 7680
CB = 512

LR = 0.001
B1 = 0.9
B2 = 0.999
ADAM_EPS = 1e-08
WD = 0.01
STEP = 10

MESH = pl.DeviceIdType.MESH


def _cp(vmem_mb):
    return pltpu.CompilerParams(vmem_limit_bytes=vmem_mb * 1024 * 1024)


def _mm(a, b):
    return jnp.dot(a, b, preferred_element_type=F32)


def _mm_nt(a, b):
    return lax.dot_general(a, b, (((1,), (1,)), ((), ())), preferred_element_type=F32)


def _mm_tn(a, b):
    return lax.dot_general(a, b, (((0,), (0,)), ((), ())), preferred_element_type=F32)


def _tri3(lower):
    r = lax.broadcasted_iota(jnp.int32, (CH, 3 * CH), 0)
    c = lax.broadcasted_iota(jnp.int32, (CH, 3 * CH), 1)
    c = jnp.where(c >= 2 * CH, c - 2 * CH, jnp.where(c >= CH, c - CH, c))
    return ((r >= c) if lower else (c >= r)).astype(BF16)


def _mm_tri_exact(tri3, g):
    g1 = g.astype(BF16)
    r1 = g - g1.astype(F32)
    g2 = r1.astype(BF16)
    g3 = (r1 - g2.astype(F32)).astype(BF16)
    return _mm(tri3, jnp.concatenate([g1, g2, g3], axis=0))


def _sigmoid(v):
    return 0.5 * jnp.tanh(0.5 * v) + 0.5


def _bf(v):
    return v.astype(BF16)


def _place():
    x, y, c = lax.axis_index("x"), lax.axis_index("y"), lax.axis_index("c")
    return (x, y, c), (x, y, 1 - c), [(1 - x, y), (x, 1 - y), (1 - x, 1 - y)]


def _dev_index(px, py, pc):
    return 4 * px + 2 * py + pc


def _gather_in_projection(w_in_s, x2d, norm_w):
    half = D // 2
    t = x2d.shape[0]
    prep_rows = min(512, t)
    nprep = t // prep_rows

    def body(win_ref, x_hbm, nw_ref, wp_g, xn_hbm, xnt_hbm, proj_hbm, give, take, mine, xbuf, xnbuf, xntbuf, w_own, pbuf,
             send_sems, recv_sems, loc_sem, swap_sems, in_sems, out_sems, own_sem):
        (x, y, c), sibling, chips = _place()
        give[...] = win_ref[pl.ds(pl.multiple_of(half * (1 - c), half), half), :].astype(BF16)
        swap = pltpu.make_async_remote_copy(src_ref=give, dst_ref=take, send_sem=swap_sems.at[0], recv_sem=swap_sems.at[1],
                                            device_id=sibling, device_id_type=MESH)
        swap.start()
        swap.wait()
        own = win_ref[pl.ds(pl.multiple_of(half * c, half), half), :]
        other = take[...].astype(F32)
        mine[...] = jnp.where(c == 0, jnp.concatenate([own, other], axis=1),
                              jnp.concatenate([other, own], axis=1)).astype(BF16)

        def place(px, py, pc):
            return wp_g.at[2 * px + py, pl.ds(pl.multiple_of(half * pc, half), half), :]

        def copy(kind, origin, to, src=mine):
            return pltpu.make_async_remote_copy(
                src_ref=src, dst_ref=place(*origin), send_sem=send_sems.at[kind], recv_sem=recv_sems.at[kind],
                device_id=to, device_id_type=MESH)

        me = (x, y, c)
        local = pltpu.make_async_copy(mine, place(*me), loc_sem)
        local.start()
        first = [copy(0, me, sibling)] + [copy(1 + j, me, (*chip, c)) for j, chip in enumerate(chips)]
        for cp in first:
            cp.start()

        copy(0, (x, y, 1 - c), me).wait_recv()
        local.wait()
        my_chip = 2 * x + y
        fetch = pltpu.make_async_copy(wp_g.at[my_chip], w_own, own_sem)
        fetch.start()

        def rows_of(i):
            return pl.ds(pl.multiple_of(i * prep_rows, prep_rows), prep_rows)

        def load(i, slot):
            return pltpu.make_async_copy(x_hbm.at[rows_of(i), :], xbuf.at[slot], in_sems.at[slot])

        def stores(i, slot):
            own_cols = pl.ds(pl.multiple_of(my_chip * PAIR, 128), PAIR)
            return (pltpu.make_async_copy(xnbuf.at[slot], xn_hbm.at[rows_of(i), :], out_sems.at[slot, 0]),
                    pltpu.make_async_copy(xntbuf.at[slot], xnt_hbm.at[:, rows_of(i)], out_sems.at[slot, 1]),
                    pltpu.make_async_copy(pbuf.at[slot], proj_hbm.at[rows_of(i), own_cols], out_sems.at[slot, 2]))

        load(0, 0).start()
        fetch.wait()

        def prep(i, carry):
            slot = lax.rem(i, 2)
            load(i, slot).wait()

            @pl.when(i + 1 < nprep)
            def _():
                load(i + 1, 1 - slot).start()

            @pl.when(i >= 2)
            def _():
                for cp in stores(i - 2, slot):
                    cp.wait()

            xv = xbuf[slot]
            xn = (xv * lax.rsqrt(jnp.mean(xv * xv, axis=-1, keepdims=True) + EPS)) * nw_ref[...]
            xn_b = xn.astype(BF16)
            xnbuf[slot] = xn_b
            xntbuf[slot] = xn.T.astype(BF16)
            pbuf[slot] = _mm(xn_b, w_own[...])
            for cp in stores(i, slot):
                cp.start()
            return carry

        lax.fori_loop(0, nprep, prep, 0)
        for i in range(max(nprep - 2, 0), nprep):
            for cp in stores(i, i % 2):
                cp.wait()

        passed = []
        for j, chip in enumerate(chips):
            copy(1 + j, (*chip, c), me).wait_recv()
            cp = copy(4 + j, (*chip, c), sibling, src=place(*chip, c))
            cp.start()
            passed.append(cp)
        for j, chip in enumerate(chips):
            copy(4 + j, (*chip, 1 - c), me).wait_recv()
        for cp in first + passed:
            cp.wait_send()

    vm = pl.BlockSpec(memory_space=pltpu.VMEM)
    hbm = pl.BlockSpec(memory_space=pl.ANY)
    return pl.pallas_call(
        body,
        out_shape=[jax.ShapeDtypeStruct((NDEV // 2, D, PAIR), BF16), jax.ShapeDtypeStruct((t, D), BF16),
                   jax.ShapeDtypeStruct((D, t), BF16), jax.ShapeDtypeStruct((t, DIN), F32)],
        in_specs=[vm, hbm, vm],
        out_specs=[hbm, hbm, hbm, hbm],
        scratch_shapes=[pltpu.VMEM((half, IN_SHARD), BF16), pltpu.VMEM((half, IN_SHARD), BF16),
                        pltpu.VMEM((half, PAIR), BF16),
                        pltpu.VMEM((2, prep_rows, D), F32), pltpu.VMEM((2, prep_rows, D), BF16),
                        pltpu.VMEM((2, D, prep_rows), BF16),
                        pltpu.VMEM((D, PAIR), BF16), pltpu.VMEM((2, prep_rows, PAIR), F32),
                        pltpu.SemaphoreType.DMA((NDEV - 1,)), pltpu.SemaphoreType.DMA((NDEV - 1,)),
                        pltpu.SemaphoreType.DMA, pltpu.SemaphoreType.DMA((2,)),
                        pltpu.SemaphoreType.DMA((2,)), pltpu.SemaphoreType.DMA((2, 3)), pltpu.SemaphoreType.DMA],
        name="gather_in_projection", compiler_params=_cp(56),
    )(w_in_s, x2d, norm_w)


def _gather_square(shards, after):
    n = len(shards)

    def launch(*refs):
        ins, outs = refs[:n], refs[n + 1:2 * n + 1]
        send_sems, recv_sems, loc_sems = refs[2 * n + 1:]
        (x, y, c), _, _ = _place()
        me = _dev_index(x, y, c)
        peers = [(1 - x if r & 4 else x, 1 - y if r & 2 else y, 1 - c if r & 1 else c) for r in range(1, NDEV)]
        barrier = pltpu.get_barrier_semaphore()
        for peer in peers:
            pl.semaphore_signal(barrier, inc=1, device_id=peer, device_id_type=MESH)
        pl.semaphore_wait(barrier, NDEV - 1)
        local = [pltpu.make_async_copy(ins[k], outs[k].at[me], loc_sems.at[k]) for k in range(n)]
        copies = [pltpu.make_async_remote_copy(
            src_ref=ins[k], dst_ref=outs[k].at[me], send_sem=send_sems.at[r, k], recv_sem=recv_sems.at[r, k],
            device_id=peer, device_id_type=MESH) for r, peer in enumerate(peers) for k in range(n)]
        for cp in local + copies:
            cp.start()
        for r, peer in enumerate(peers):
            for k in range(n):
                pltpu.make_async_remote_copy(
                    src_ref=ins[k], dst_ref=outs[k].at[_dev_index(*peer)], send_sem=send_sems.at[r, k],
                    recv_sem=recv_sems.at[r, k], device_id=peer, device_id_type=MESH).wait_recv()
        for cp in copies:
            cp.wait_send()
        for cp in local:
            cp.wait()

    return pl.kernel(
        launch, out_type=[jax.ShapeDtypeStruct((NDEV,) + a.shape, a.dtype) for a in shards],
        mesh=plsc.ScalarSubcoreMesh(axis_name="sequencer", num_cores=1), name="gather_square",
        scratch_types=(pltpu.SemaphoreType.DMA((NDEV - 1, n)), pltpu.SemaphoreType.DMA((NDEV - 1, n)),
                       pltpu.SemaphoreType.DMA((n,))),
        compiler_params=pltpu.CompilerParams(collective_id=2),
    )(*shards, after)


def _exchange_pair(arrs, after):
    n = len(arrs)

    def launch(*refs):
        ins, got = refs[:n], refs[n + 1:2 * n + 1]
        send_sems, recv_sems = refs[2 * n + 1:]
        (x, y, c), sibling, _ = _place()
        barrier = pltpu.get_barrier_semaphore()
        pl.semaphore_signal(barrier, inc=1, device_id=sibling, device_id_type=MESH)
        pl.semaphore_wait(barrier, 1)
        sends = [pltpu.make_async_remote_copy(
            src_ref=ins[k].at[_dev_index(q // 2, q % 2, 1 - c)], dst_ref=got[k].at[q], send_sem=send_sems.at[q, k],
            recv_sem=recv_sems.at[q, k], device_id=sibling, device_id_type=MESH) for q in range(4) for k in range(n)]
        for cp in sends:
            cp.start()
        for cp in sends:
            cp.wait_recv()
        for cp in sends:
            cp.wait_send()

    return pl.kernel(
        launch, out_type=[jax.ShapeDtypeStruct((4,) + a.shape[1:], a.dtype) for a in arrs],
        mesh=plsc.ScalarSubcoreMesh(axis_name="sequencer", num_cores=1), name="exchange_pair",
        scratch_types=(pltpu.SemaphoreType.DMA((4, n)), pltpu.SemaphoreType.DMA((4, n))),
        compiler_params=pltpu.CompilerParams(collective_id=0),
    )(*arrs, after)


def _pair_sum(full, got, core, rows, name):
    _, nr, nc = got.shape

    def body(core_ref, a_ref, b_ref, o_ref):
        o_ref[...] = (a_ref[...].astype(F32) + b_ref[...].astype(F32)).astype(BF16)

    blk = pl.BlockSpec((1, rows, nc), lambda q, i, core_ref: (q, i, 0))
    return pl.pallas_call(
        body,
        grid_spec=pltpu.PrefetchScalarGridSpec(
            num_scalar_prefetch=1, grid=(4, nr // rows),
            in_specs=[pl.BlockSpec((1, rows, nc), lambda q, i, core_ref: (2 * q + core_ref[0], i, 0)), blk],
            out_specs=blk),
        out_shape=jax.ShapeDtypeStruct(got.shape, BF16), name=name,
    )(core, full, got)


def _exchange_chips(sums):
    n = len(sums)

    def launch(*refs):
        ins, outs = refs[:n], refs[n:2 * n]
        send_sems, recv_sems = refs[2 * n:]
        (x, y, c), _, chips = _place()
        barrier = pltpu.get_barrier_semaphore()
        for px, py in chips:
            pl.semaphore_signal(barrier, inc=1, device_id=(px, py, c), device_id_type=MESH)
        pl.semaphore_wait(barrier, len(chips))
        copies = [pltpu.make_async_remote_copy(
            src_ref=ins[k].at[2 * px + py], dst_ref=outs[k].at[j], send_sem=send_sems.at[j, k],
            recv_sem=recv_sems.at[j, k], device_id=(px, py, c), device_id_type=MESH)
            for j, (px, py) in enumerate(chips) for k in range(n)]
        for cp in copies:
            cp.start()
        for cp in copies:
            cp.wait_recv()
        for cp in copies:
            cp.wait_send()

    return pl.kernel(
        launch, out_type=[jax.ShapeDtypeStruct((3,) + a.shape[1:], a.dtype) for a in sums],
        mesh=plsc.ScalarSubcoreMesh(axis_name="sequencer", num_cores=1), name="exchange_chips",
        scratch_types=(pltpu.SemaphoreType.DMA((3, n)), pltpu.SemaphoreType.DMA((3, n))),
        compiler_params=pltpu.CompilerParams(collective_id=1),
    )(*sums)


def _exchange_square(partials):
    n = len(partials)

    def launch(*refs):
        ins, outs = refs[:n], refs[n:2 * n]
        send_sems, recv_sems, loc_sems = refs[2 * n:]
        (x, y, c), _, _ = _place()
        me = _dev_index(x, y, c)
        peers = [(1 - x if r & 4 else x, 1 - y if r & 2 else y, 1 - c if r & 1 else c) for r in range(1, NDEV)]
        barrier = pltpu.get_barrier_semaphore()
        for peer in peers:
            pl.semaphore_signal(barrier, inc=1, device_id=peer, device_id_type=MESH)
        pl.semaphore_wait(barrier, NDEV - 1)
        local = [pltpu.make_async_copy(ins[k].at[me], outs[k].at[me], loc_sems.at[k]) for k in range(n)]
        copies = [pltpu.make_async_remote_copy(
            src_ref=ins[k].at[_dev_index(*peer)], dst_ref=outs[k].at[me], send_sem=send_sems.at[r, k],
            recv_sem=recv_sems.at[r, k], device_id=peer, device_id_type=MESH)
            for r, peer in enumerate(peers) for k in range(n)]
        for cp in local + copies:
            cp.start()
        for r, peer in enumerate(peers):
            for k in range(n):
                pltpu.make_async_remote_copy(
                    src_ref=ins[k].at[me], dst_ref=outs[k].at[_dev_index(*peer)], send_sem=send_sems.at[r, k],
                    recv_sem=recv_sems.at[r, k], device_id=peer, device_id_type=MESH).wait_recv()
        for cp in copies:
            cp.wait_send()
        for cp in local:
            cp.wait()

    return pl.kernel(
        launch, out_type=[jax.ShapeDtypeStruct(a.shape, a.dtype) for a in partials],
        mesh=plsc.ScalarSubcoreMesh(axis_name="sequencer", num_cores=1), name="exchange_square",
        scratch_types=(pltpu.SemaphoreType.DMA((NDEV - 1, n)), pltpu.SemaphoreType.DMA((NDEV - 1, n)),
                       pltpu.SemaphoreType.DMA((n,))),
        compiler_params=pltpu.CompilerParams(collective_id=3),
    )(*partials)


def _exchange_small(small):
    def body(sm_ref, out_ref, send_sems, recv_sems):
        (x, y, c), _, _ = _place()
        me = _dev_index(x, y, c)
        peers = [(1 - x if r & 4 else x, 1 - y if r & 2 else y, 1 - c if r & 1 else c) for r in range(1, NDEV)]
        out_ref[me] = sm_ref[...]
        copies = [pltpu.make_async_remote_copy(
            src_ref=sm_ref, dst_ref=out_ref.at[me], send_sem=send_sems.at[r], recv_sem=recv_sems.at[r],
            device_id=peer, device_id_type=MESH) for r, peer in enumerate(peers)]
        for cp in copies:
            cp.start()
        for r, peer in enumerate(peers):
            pltpu.make_async_remote_copy(
                src_ref=sm_ref, dst_ref=out_ref.at[_dev_index(*peer)], send_sem=send_sems.at[r], recv_sem=recv_sems.at[r],
                device_id=peer, device_id_type=MESH).wait_recv()
        for cp in copies:
            cp.wait_send()

    vm = pl.BlockSpec(memory_space=pltpu.VMEM)
    return pl.pallas_call(
        body, out_shape=jax.ShapeDtypeStruct((NDEV,) + small.shape, F32), in_specs=[vm], out_specs=vm,
        scratch_shapes=[pltpu.SemaphoreType.DMA((NDEV - 1,)), pltpu.SemaphoreType.DMA((NDEV - 1,))],
        name="exchange_small",
    )(small)


def _inproj_fwd(xn, w_pairs, proj, chip, tb):
    t = xn.shape[0]
    nblk, _, nb = w_pairs.shape

    def body(chip_ref, xn_ref, w_ref, proj_in, proj_ref):
        proj_ref[...] = _mm(xn_ref[...], w_ref[0])

    def other(j, chip_ref):
        return j + (j >= chip_ref[0]).astype(jnp.int32)

    return pl.pallas_call(
        body,
        grid_spec=pltpu.PrefetchScalarGridSpec(
            num_scalar_prefetch=1, grid=(t // tb, nblk - 1),
            in_specs=[pl.BlockSpec((tb, D), lambda i, j, chip_ref: (i, 0)),
                      pl.BlockSpec((1, D, nb), lambda i, j, chip_ref: (other(j, chip_ref), 0, 0)),
                      pl.BlockSpec(memory_space=pl.ANY)],
            out_specs=pl.BlockSpec((tb, nb), lambda i, j, chip_ref: (i, other(j, chip_ref)))),
        out_shape=jax.ShapeDtypeStruct((t, DIN), F32),
        input_output_aliases={3: 0},
        name="inproj_fwd", compiler_params=_cp(56),
    )(chip, xn, w_pairs, proj)


def _lower_bound(lb_ref):
    a0 = lb_ref[0:1, :]
    a1 = lb_ref[1:2, :]
    mx = jnp.maximum(a0, a1)
    e0 = jnp.exp(a0 - mx)
    e1 = jnp.exp(a1 - mx)
    return e0 / (e0 + e1)


def _hgrn_chunk_fwd(hq, hf, lb, tril):
    sg = _sigmoid(hf)
    f = lb + (1.0 - lb) * sg
    g = jnp.log(f)
    k = 1.0 - f
    sq = _sigmoid(hq)
    q = hq * sq
    b = _mm_tri_exact(tril, g)
    last_row = lax.broadcasted_iota(jnp.int32, b.shape, 0) == CH - 1
    b_last = jnp.sum(jnp.where(last_row, b, 0.0), axis=0, keepdims=True)
    c = 0.5 * b_last
    eb = jnp.exp(b)
    ea = jnp.exp(b - c)
    ek = jnp.exp(c - b)
    ed = jnp.exp(b_last - b)
    ebl = jnp.exp(b_last)
    return dict(sg=sg, f=f, k=k, sq=sq, q=q, eb=eb, ea=ea, ek=ek, ed=ed, ebl=ebl,
                qe=q * eb, qa=q * ea, ka=k * ek, kd=k * ed)


def _tri(lower):
    r = lax.broadcasted_iota(jnp.int32, (CH, CH), 0)
    c = lax.broadcasted_iota(jnp.int32, (CH, CH), 1)
    return (r >= c) if lower else (c >= r)


def _head_segment(p_ref, rows, j, hg):
    return p_ref[rows, j * HD * hg:(j + 1) * HD * hg]


def _head(a, k):
    return a[:, k * HD:(k + 1) * HD]


def _hgrn_fwd(proj, lbw, rb, hg):
    assert hg == HEADS
    t = proj.shape[0]
    ncb = rb // CH

    def body(p_ref, lb_ref, o_ref, st_ref, s_scr):
        @pl.when(pl.program_id(1) == 0)
        def _():
            s_scr[...] = jnp.zeros_like(s_scr)

        lb = _lower_bound(lb_ref)
        causal = _tri(True)
        tril = _tri3(True)
        heads = range(hg)

        def chunk(cc, carry):
            r0 = pl.multiple_of(cc * CH, CH)
            rows = pl.ds(r0, CH)
            e = _hgrn_chunk_fwd(_head_segment(p_ref, rows, 0, hg), _head_segment(p_ref, rows, 1, hg), lb, tril)
            v = _bf(_head_segment(p_ref, rows, 2, hg))
            sts = [s_scr[k] for k in heads]
            qa, ka, qe, kd = _bf(e["qa"]), _bf(e["ka"]), _bf(e["qe"]), _bf(e["kd"])
            a = [_bf(jnp.where(causal, _mm_nt(_head(qa, k), _head(ka, k)), 0.0)) for k in heads]
            o_inter = [_mm_nt(_head(qe, k), _bf(sts[k])) for k in heads]
            kv = [_mm_tn(_head(v, k), _head(kd, k)) for k in heads]
            o_intra = [_mm(a[k], _head(v, k)) for k in heads]
            for k in heads:
                st_ref[cc, k] = sts[k]
                o_ref[rows, k * HD:(k + 1) * HD] = o_inter[k] + o_intra[k]
                s_scr[k] = sts[k] * _head(e["ebl"], k) + kv[k]
            return carry

        lax.fori_loop(0, ncb, chunk, 0, unroll=4)

    return pl.pallas_call(
        body, grid=(HEADS // hg, t // rb),
        in_specs=[pl.BlockSpec((rb, 3 * HD * hg), lambda h, i: (i, h)), pl.BlockSpec((2, HD * hg), lambda h, i: (0, h))],
        out_specs=[pl.BlockSpec((rb, HD * hg), lambda h, i: (i, h)),
                   pl.BlockSpec((ncb, hg, HD, HD), lambda h, i: (i, h, 0, 0))],
        out_shape=[jax.ShapeDtypeStruct((t, D), F32), jax.ShapeDtypeStruct((t // CH, HEADS, HD, HD), F32)],
        scratch_shapes=[pltpu.VMEM((hg, HD, HD), F32)],
        name="hgrn_fwd", compiler_params=_cp(48),
    )(proj, lbw)


def _hgrn_bwd(proj, lbw, do_raw, states, rb, hg):
    assert hg == HEADS
    t = proj.shape[0]
    nblk = t // rb
    ncb = rb // CH
    wd = HD * hg

    def body(p_ref, lb_ref, do_ref, st_ref, dp_ref, dlb_ref, ds_scr):
        @pl.when(pl.program_id(1) == 0)
        def _():
            ds_scr[...] = jnp.zeros_like(ds_scr)
            dlb_ref[...] = jnp.zeros_like(dlb_ref)

        lb = _lower_bound(lb_ref)
        causal = _tri(True)
        tril = _tri3(True)
        triu = _tri3(False)
        last_row = lax.broadcasted_iota(jnp.int32, (CH, HD * hg), 0) == CH - 1
        row0 = lax.broadcasted_iota(jnp.int32, (8, HD * hg), 0) == 0
        heads = range(hg)
        wide = lambda parts: jnp.concatenate(parts, axis=1)

        def chunk(it, carry):
            cc = ncb - 1 - it
            r0 = pl.multiple_of(cc * CH, CH)
            rows = pl.ds(r0, CH)
            hq = _head_segment(p_ref, rows, 0, hg)
            e = _hgrn_chunk_fwd(hq, _head_segment(p_ref, rows, 1, hg), lb, tril)
            v = _bf(_head_segment(p_ref, rows, 2, hg))
            do = _bf(do_ref[rows, :])
            sts = [st_ref[cc, k] for k in heads]
            dsts = [ds_scr[k] for k in heads]
            dlb_acc = dlb_ref[...]
            qa, ka, qe, kd = _bf(e["qa"]), _bf(e["ka"]), _bf(e["qe"]), _bf(e["kd"])
            a = [_bf(jnp.where(causal, _mm_nt(_head(qa, k), _head(ka, k)), 0.0)) for k in heads]
            da = [_bf(jnp.where(causal, _mm_nt(_head(do, k), _head(v, k)), 0.0)) for k in heads]
            dqe = wide([_mm(_head(do, k), _bf(sts[k])) for k in heads])
            dkd = wide([_mm(_head(v, k), _bf(dsts[k])) for k in heads])
            dv_state = [_mm_nt(_head(kd, k), _bf(dsts[k])) for k in heads]
            ds_new = [_mm_tn(_head(do, k), _head(qe, k)) for k in heads]
            dv_intra = [_mm_tn(a[k], _head(do, k)) for k in heads]
            dqa = wide([_mm(da[k], _head(ka, k)) for k in heads])
            dka = wide([_mm_tn(da[k], _head(qa, k)) for k in heads])
            dv = wide([dv_intra[k] + dv_state[k] for k in heads])
            dbl = e["ebl"] * wide([jnp.sum(sts[k] * dsts[k], axis=0, keepdims=True) for k in heads])
            dq = dqe * e["eb"] + dqa * e["ea"]
            dk = dka * e["ek"] + dkd * e["ed"]
            dkd_kd = dkd * kd.astype(F32)
            db = dqe * qe.astype(F32) + dqa * qa.astype(F32) - dka * ka.astype(F32) - dkd_kd
            db = db + jnp.where(last_row, dbl + jnp.sum(dkd_kd, axis=0, keepdims=True), 0.0)
            dg = _mm_tri_exact(triu, db)
            df = dg / e["f"] - dk
            sg = e["sg"]
            sq = e["sq"]
            dhq = _bf(dq * (sq * (1.0 + hq * (1.0 - sq))))
            dhf = _bf(df * (1.0 - lb) * sg * (1.0 - sg))
            dhi = _bf(dv)
            dlb_new = dlb_acc + jnp.where(row0, jnp.sum(df * (1.0 - sg), axis=0, keepdims=True), 0.0)
            for k in heads:
                ds_scr[k] = ds_new[k] + dsts[k] * _head(e["ebl"], k)
            dp_ref[rows, 0:wd] = dhq
            dp_ref[rows, wd:2 * wd] = dhf
            dp_ref[rows, 2 * wd:3 * wd] = dhi
            dlb_ref[...] = dlb_new
            return carry

        lax.fori_loop(0, ncb, chunk, 0, unroll=2)

    rev = lambda h, i: (nblk - 1 - i, h)
    return pl.pallas_call(
        body, grid=(HEADS // hg, nblk),
        in_specs=[pl.BlockSpec((rb, 3 * HD * hg), rev), pl.BlockSpec((2, HD * hg), lambda h, i: (0, h)),
                  pl.BlockSpec((rb, HD * hg), rev), pl.BlockSpec((ncb, hg, HD, HD), lambda h, i: (nblk - 1 - i, h, 0, 0))],
        out_specs=[pl.BlockSpec((rb, 3 * HD * hg), rev), pl.BlockSpec((8, HD * hg), lambda h, i: (0, h))],
        out_shape=[jax.ShapeDtypeStruct((t, 3 * D), BF16), jax.ShapeDtypeStruct((8, D), F32)],
        scratch_shapes=[pltpu.VMEM((hg, HD, HD), F32)],
        name="hgrn_bwd", compiler_params=_cp(48),
    )(proj, lbw, do_raw, states)


def _kv_variants(tile, odd):
    low = lax.broadcasted_iota(jnp.int32, tile.shape, 1) < 64
    if odd:
        hi = jnp.where(low, 0.0, tile)
        lo = pltpu.roll(hi, 64, 1)
    else:
        lo = jnp.where(low, tile, 0.0)
        hi = pltpu.roll(lo, 64, 1)
    return _bf(lo), _bf(hi)


def _attn_masks(n):
    qi = lax.broadcasted_iota(jnp.int32, (AB, AB), 0)
    kj = lax.broadcasted_iota(jnp.int32, (AB, AB), 1)
    cur = kj <= qi
    return cur, cur | (n > 0), qi <= kj


def _kv_all(prev_ref, cur_ref):
    out = []
    for tl in range(2):
        cols = slice(tl * 128, (tl + 1) * 128)
        tile = jnp.concatenate([prev_ref[:, cols], cur_ref[:, cols]], axis=0)
        out.append(_kv_variants(tile, 0))
        out.append(_kv_variants(tile, 1))
    return out


def _window(a2, cur):
    return jnp.where(cur, a2[:, AB:], a2[:, :AB])


def _attn_softmax(scores, sinks, cur, ok):
    s = [jnp.where(ok, _window(s2, cur) * ATT_SCALE, NEG) for s2 in scores]
    m = [jnp.maximum(jnp.max(si, axis=-1, keepdims=True), sink) for si, sink in zip(s, sinks)]
    p = [jnp.exp(si - mi) for si, mi in zip(s, m)]
    es = [jnp.exp(sink - mi) for sink, mi in zip(sinks, m)]
    inv = [1.0 / (jnp.sum(pi, axis=-1, keepdims=True) + ei) for pi, ei in zip(p, es)]
    return [pi * ii for pi, ii in zip(p, inv)], [ei * ii for ei, ii in zip(es, inv)]


def _spread(pc, cur):
    return jnp.concatenate([jnp.where(cur, 0.0, pc), jnp.where(cur, pc, 0.0)], axis=1)


def _spread_t(pct, cur_t):
    return jnp.concatenate([jnp.where(cur_t, 0.0, pct), jnp.where(cur_t, pct, 0.0)], axis=0)


def _attn_fwd(proj, sinks):
    t = proj.shape[0]
    nb = t // AB

    nsub = 2
    assert nb % nsub == 0

    def body(q_ref, kc_ref, kp_ref, vc_ref, vp_ref, sink_ref, o_ref):
        sinks_v = [sink_ref[0, h] for h in range(QH)]
        heads = [(j, ab) for j in range(8) for ab in range(2)]
        for sb in range(nsub):
            rows = pl.ds(AB * sb, AB)
            before = pl.ds(AB * (sb - 1), AB)
            cur, ok, _ = _attn_masks(nsub * pl.program_id(0) + sb)
            kvars = _kv_all(kp_ref if sb == 0 else kc_ref.at[before, :], kc_ref.at[rows, :])
            vvars = _kv_all(vp_ref if sb == 0 else vc_ref.at[before, :], vc_ref.at[rows, :])
            qps = [_bf(q_ref[rows, 128 * j:128 * (j + 1)]) for j in range(8)]
            scores = [_mm_nt(qps[j], kvars[j // 2][ab]) for j, ab in heads]
            pcs, _ = _attn_softmax(scores, sinks_v, cur, ok)
            parts = [_mm(_bf(_spread(pcs[h], cur)), vvars[j // 2][ab]) for h, (j, ab) in enumerate(heads)]
            for j in range(8):
                o_ref[rows, 128 * j:128 * (j + 1)] = parts[2 * j] + parts[2 * j + 1]

    prev = lambda n: jnp.maximum(nsub * n - 1, 0)
    step = nsub * AB
    return pl.pallas_call(
        body, grid=(nb // nsub,),
        in_specs=[pl.BlockSpec((step, D), lambda n: (n, C_AQ // D)),
                  pl.BlockSpec((step, 256), lambda n: (n, C_AK // 256)),
                  pl.BlockSpec((AB, 256), lambda n: (prev(n), C_AK // 256)),
                  pl.BlockSpec((step, 256), lambda n: (n, C_AV // 256)),
                  pl.BlockSpec((AB, 256), lambda n: (prev(n), C_AV // 256)),
                  pl.BlockSpec(memory_space=pltpu.SMEM)],
        out_specs=pl.BlockSpec((step, D), lambda n: (n, 0)),
        out_shape=jax.ShapeDtypeStruct((t, D), F32),
        name="attn_fwd", compiler_params=_cp(32),
    )(proj, proj, proj, proj, proj, sinks)


def _attn_bwd(proj, sinks, do_a):
    t = proj.shape[0]
    nb = t // AB
    nsub = 2
    assert nb % nsub == 0
    steps = nb // nsub
    step = nsub * AB

    def body(q_ref, kc_ref, kp_ref, vc_ref, vp_ref, do_ref, sink_ref, dq_ref, dkv_ref, dsink_ref, carry):
        n = pl.program_id(0)

        @pl.when(n == 0)
        def _():
            dsink_ref[...] = jnp.zeros_like(dsink_ref)
            carry[...] = jnp.zeros_like(carry)

        def one_block(sb):
            rows = pl.ds(AB * sb, AB)
            before = pl.ds(AB * (sb - 1), AB)
            cur, ok, cur_t = _attn_masks(nsub * n + sb)
            low = lax.broadcasted_iota(jnp.int32, (2 * AB, 128), 1) < 64
            lane = lax.broadcasted_iota(jnp.int32, (8, 128), 1)
            row0 = lax.broadcasted_iota(jnp.int32, (8, 128), 0) == 0
            kvars = _kv_all(kp_ref if sb == 0 else kc_ref.at[before, :], kc_ref.at[rows, :])
            vvars = _kv_all(vp_ref if sb == 0 else vc_ref.at[before, :], vc_ref.at[rows, :])
            qps = [_bf(q_ref[rows, 128 * j:128 * (j + 1)]) for j in range(8)]
            dops = [_bf(do_ref[rows, 128 * j:128 * (j + 1)]) for j in range(8)]
            heads = [(j, ab) for j in range(8) for ab in range(2)]
            scores = [_mm_nt(qps[j], kvars[j // 2][ab]) for j, ab in heads]
            dps = [_mm_nt(dops[j], vvars[j // 2][ab]) for j, ab in heads]
            pcs, pss = _attn_softmax(scores, [sink_ref[0, h] for h in range(QH)], cur, ok)
            dpcs = [_window(dp2, cur) for dp2 in dps]
            rss = [jnp.sum(pc * dpc, axis=-1, keepdims=True) for pc, dpc in zip(pcs, dpcs)]
            dscs = [pc * (dpc - rs) for pc, dpc, rs in zip(pcs, dpcs, rss)]
            dsink = jnp.zeros((8, 128), F32)
            for h in range(QH):
                dsink = dsink + jnp.where(row0 & (lane == h), -jnp.sum(pss[h] * rss[h]), 0.0)
            dq_terms = [_mm(_bf(_spread(dscs[h], cur)), kvars[j // 2][ab]) for h, (j, ab) in enumerate(heads)]
            for j in range(8):
                dq_ref[rows, 128 * j:128 * (j + 1)] = _bf((dq_terms[2 * j] + dq_terms[2 * j + 1]) * ATT_SCALE)
            dsc_t = [_bf(_spread_t(dsc.T, cur_t)) for dsc in dscs]
            pc_t = [_bf(_spread_t(pc.T, cur_t)) for pc in pcs]
            dk_terms = [_mm(dsc_t[h], qps[j]) for h, (j, ab) in enumerate(heads)]
            dv_terms = [_mm(pc_t[h], dops[j]) for h, (j, ab) in enumerate(heads)]
            dk_ab = [[dk_terms[4 * g + ab] + dk_terms[4 * g + 2 + ab] for ab in range(2)] for g in range(4)]
            dv_ab = [[dv_terms[4 * g + ab] + dv_terms[4 * g + 2 + ab] for ab in range(2)] for g in range(4)]
            dkts, dvts = [], []
            for tl in range(2):
                ke, ko = dk_ab[2 * tl], dk_ab[2 * tl + 1]
                ve, vo = dv_ab[2 * tl], dv_ab[2 * tl + 1]
                dkts.append((jnp.where(low, ke[0], 0.0) + pltpu.roll(jnp.where(low, 0.0, ke[1]), 64, 1)
                             + jnp.where(low, 0.0, ko[1]) + pltpu.roll(jnp.where(low, ko[0], 0.0), 64, 1)) * ATT_SCALE)
                dvts.append(jnp.where(low, ve[0], 0.0) + pltpu.roll(jnp.where(low, 0.0, ve[1]), 64, 1)
                            + jnp.where(low, 0.0, vo[1]) + pltpu.roll(jnp.where(low, vo[0], 0.0), 64, 1))
            return dkts, dvts, dsink

        @pl.when(n < steps)
        def _():
            (dk0, dv0, ds0), (dk1, dv1, ds1) = one_block(0), one_block(1)
            first, second = slice(0, AB), slice(AB, 2 * AB)
            for tl in range(2):
                for cols, g0, g1 in ((slice(tl * 128, (tl + 1) * 128), dk0[tl], dk1[tl]),
                                     (slice(256 + tl * 128, 256 + (tl + 1) * 128), dv0[tl], dv1[tl])):
                    dkv_ref[first, cols] = _bf(carry[first, cols])
                    dkv_ref[second, cols] = _bf(carry[second, cols] + g0[first])
                    carry[first, cols] = g0[second] + g1[first]
                    carry[second, cols] = g1[second]
            dsink_ref[...] += ds0 + ds1

        @pl.when(n == steps)
        def _():
            dkv_ref[...] = _bf(carry[...])

    cur = lambda n: jnp.minimum(n, steps - 1)
    last = lambda n: jnp.clip(n - 1, 0, steps - 1)
    prev = lambda n: jnp.clip(nsub * n - 1, 0, nb - 1)
    return pl.pallas_call(
        body, grid=(steps + 1,),
        in_specs=[pl.BlockSpec((step, D), lambda n: (cur(n), C_AQ // D)),
                  pl.BlockSpec((step, 256), lambda n: (cur(n), C_AK // 256)),
                  pl.BlockSpec((AB, 256), lambda n: (prev(n), C_AK // 256)),
                  pl.BlockSpec((step, 256), lambda n: (cur(n), C_AV // 256)),
                  pl.BlockSpec((AB, 256), lambda n: (prev(n), C_AV // 256)),
                  pl.BlockSpec((step, D), lambda n: (cur(n), 0)),
                  pl.BlockSpec(memory_space=pltpu.SMEM)],
        out_specs=[pl.BlockSpec((step, D), lambda n: (cur(n), 0)),
                   pl.BlockSpec((step, 512), lambda n: (last(n), 0)),
                   pl.BlockSpec((8, 128), lambda n: (0, 0))],
        out_shape=[jax.ShapeDtypeStruct((t, D), BF16), jax.ShapeDtypeStruct((t, 512), BF16),
                   jax.ShapeDtypeStruct((8, 128), F32)],
        scratch_shapes=[pltpu.VMEM((step, 512), F32)],
        name="attn_bwd", compiler_params=_cp(40),
    )(proj, proj, proj, proj, proj, do_a, sinks)


def _silu_and_grad(v):
    s = _sigmoid(v)
    return v * s, s * (1.0 + v * (1.0 - s))


def _tail(o_raw, o_a, proj, x2d, tgt, wbh, wba, wout, hnw, fnw, tb):
    t = x2d.shape[0]

    def body(or_ref, oa_ref, hg_ref, ag0, ag1, mh0, mh1, ma0, ma1, x_ref, t_ref, wbh_ref, wba_ref, wout_ref, hnw_ref,
             fnw_ref, dx2_ref, dor_ref, doa_ref, dhg_ref, dagm_ref, gh_ref, ga_ref, mg_ref, dyh_ref, dya_ref, dx2b_ref,
             sums_ref):
        @pl.when(pl.program_id(0) == 0)
        def _():
            sums_ref[...] = jnp.zeros_like(sums_ref)

        halves = lambda a, b: jnp.concatenate([a[...], b[...]], axis=1)
        hnw_v = hnw_ref[...]
        fnw_v = fnw_ref[...]
        o = or_ref[...]
        rs, xhs = [], []
        for h in range(HEADS):
            oh = o[:, h * HD:(h + 1) * HD]
            r = lax.rsqrt(jnp.mean(oh * oh, axis=-1, keepdims=True) + EPS)
            rs.append(r)
            xhs.append(oh * r)
        xh = jnp.concatenate(xhs, axis=1)
        on = xh * hnw_v
        sil_hg, dsil_hg = _silu_and_grad(hg_ref[...])
        gh_b = _bf(on * sil_hg)
        y_h = _mm(gh_b, wbh_ref[...])
        oa = oa_ref[...]
        sil_ag, dsil_ag = _silu_and_grad(halves(ag0, ag1))
        ga_b = _bf(oa * sil_ag)
        y_a = _mm(ga_b, wba_ref[...])
        s_mh = _sigmoid(halves(mh0, mh1))
        s_ma = _sigmoid(halves(ma0, ma1))
        mg_b = _bf(s_mh * y_h + s_ma * y_a)
        x2 = x_ref[...] + _mm(mg_b, wout_ref[...])
        r2 = lax.rsqrt(jnp.mean(x2 * x2, axis=-1, keepdims=True) + EPS)
        xh2 = x2 * r2
        err = xh2 * fnw_v - t_ref[...]
        loss = 0.5 * jnp.sum(jnp.mean(err * err, axis=-1, keepdims=True))
        dy = err * (1.0 / D)
        dfnw = jnp.sum(dy * xh2, axis=0, keepdims=True)
        dxh2 = dy * fnw_v
        dx2 = r2 * (dxh2 - xh2 * jnp.mean(dxh2 * xh2, axis=-1, keepdims=True))
        dx2_ref[...] = dx2
        dx2_b = _bf(dx2)
        dmg = _mm_nt(dx2_b, wout_ref[...])
        dmg_h = dmg * s_mh
        dmg_a = dmg * s_ma
        dyh_b = _bf(dmg_h)
        dya_b = _bf(dmg_a)
        dagm_ref[:, D:2 * D] = _bf(dmg_h * y_h * (1.0 - s_mh))
        dagm_ref[:, 2 * D:3 * D] = _bf(dmg_a * y_a * (1.0 - s_ma))
        dgh = _mm_nt(dyh_b, wbh_ref[...])
        dga = _mm_nt(dya_b, wba_ref[...])
        doa_ref[...] = dga * sil_ag
        dagm_ref[:, 0:D] = _bf(dga * oa * dsil_ag)
        dhg_ref[...] = _bf(dgh * on * dsil_hg)
        don = dgh * sil_hg
        dhnw = jnp.sum(don * xh, axis=0, keepdims=True)
        dxh = don * hnw_v
        dos = []
        for h in range(HEADS):
            sl = slice(h * HD, (h + 1) * HD)
            dos.append(rs[h] * (dxh[:, sl] - xhs[h] * jnp.mean(dxh[:, sl] * xhs[h], axis=-1, keepdims=True)))
        dor_ref[...] = jnp.concatenate(dos, axis=1)
        gh_ref[...] = gh_b
        ga_ref[...] = ga_b
        mg_ref[...] = mg_b
        dyh_ref[...] = dyh_b
        dya_ref[...] = dya_b
        dx2b_ref[...] = dx2_b
        row = lax.broadcasted_iota(jnp.int32, (8, D), 0)
        sums_ref[...] += jnp.where(row == 0, dfnw, 0.0) + jnp.where(row == 1, dhnw, 0.0) + jnp.where(row == 2, loss, 0.0)

    rowblk = lambda c: pl.BlockSpec((tb, D), lambda i: (i, c))
    half = lambda c: pl.BlockSpec((tb, 512), lambda i: (i, c))
    full = lambda shape: pl.BlockSpec(shape, lambda i: (0, 0))
    return pl.pallas_call(
        body, grid=(t // tb,),
        in_specs=[rowblk(0), rowblk(0), rowblk(C_HG // D), half(C_AG // 512), half(C_AG // 512 + 1), half(C_MH // 512),
                  half(C_MH // 512 + 1), half(C_MA // 512), half(C_MA // 512 + 1), rowblk(0), rowblk(0),
                  full((D, D)), full((D, D)), full((D, D)), full((1, D)), full((1, D))],
        out_specs=[rowblk(0), rowblk(0), rowblk(0), rowblk(0), pl.BlockSpec((tb, 3 * D), lambda i: (i, 0))]
        + [rowblk(0)] * 6 + [full((8, D))],
        out_shape=[jax.ShapeDtypeStruct((t, D), F32)] * 3
        + [jax.ShapeDtypeStruct((t, D), BF16), jax.ShapeDtypeStruct((t, 3 * D), BF16)]
        + [jax.ShapeDtypeStruct((t, D), BF16)] * 6 + [jax.ShapeDtypeStruct((8, D), F32)],
        name="tail", compiler_params=_cp(56),
    )(o_raw, o_a, proj, proj, proj, proj, proj, proj, proj, x2d, tgt, wbh, wba, wout, hnw, fnw)


def _wgrad3(gh, dyh, ga, dya, mg, dx2b, tk, after):
    t = dyh.shape[0]
    nk = t // tk

    def body(a0, b0, a1, b1, a2, b2, _, o0, o1, o2, acc):
        k = pl.program_id(0)

        @pl.when(k == 0)
        def _():
            acc[...] = jnp.zeros_like(acc)

        acc[0] += _mm_tn(a0[...], b0[...])
        acc[1] += _mm_tn(a1[...], b1[...])
        acc[2] += _mm_tn(a2[...], b2[...])

        @pl.when(k == nk - 1)
        def _():
            o0[...] = acc[0].astype(BF16)
            o1[...] = acc[1].astype(BF16)
            o2[...] = acc[2].astype(BF16)

    blk = pl.BlockSpec((tk, D), lambda k: (k, 0))
    out = pl.BlockSpec((D, D), lambda k: (0, 0))
    return pl.pallas_call(
        body, grid=(nk,), in_specs=[blk] * 6 + [pl.BlockSpec(memory_space=pl.ANY)], out_specs=[out] * 3,
        out_shape=[jax.ShapeDtypeStruct((D, D), BF16)] * 3,
        scratch_shapes=[pltpu.VMEM((3, D, D), F32)],
        name="wgrad3", compiler_params=_cp(48),
    )(gh, dyh, ga, dya, mg, dx2b, after)


def _inproj_wgrad(xnt, pieces, nb, col0, dw, name):
    t = xnt.shape[1]
    counts = [p.shape[1] // nb for p in pieces]
    firsts = [sum(counts[:k]) for k in range(len(pieces))]
    assert col0 % nb == 0

    def body(xnt_ref, *refs):
        o_ref = refs[-1]
        j = pl.program_id(0)
        for first, count, p_ref in zip(firsts, counts, refs):
            @pl.when((j >= first) & (j < first + count))
            def _(p_ref=p_ref):
                o_ref[...] = _mm(xnt_ref[...], p_ref[...]).astype(BF16)

    def piece_spec(first, count):
        return pl.BlockSpec((t, nb), lambda j: (0, jnp.clip(j - first, 0, count - 1)))

    carried = [] if dw is None else [dw]
    return pl.pallas_call(
        body, grid=(sum(counts),),
        in_specs=[pl.BlockSpec((D, t), lambda j: (0, 0), pipeline_mode=pl.Buffered(1))]
        + [piece_spec(f, c) for f, c in zip(firsts, counts)] + [pl.BlockSpec(memory_space=pl.ANY)] * len(carried),
        out_specs=pl.BlockSpec((D, nb), lambda j: (0, j + col0 // nb)),
        out_shape=jax.ShapeDtypeStruct((D, DIN), BF16),
        input_output_aliases={1 + len(pieces): 0} if carried else {},
        name=name, compiler_params=_cp(56),
    )(xnt, *pieces, *carried)


def _inproj_dgrad(pieces, w_p, x2d, dx2, norm_w, tb, after):
    t = x2d.shape[0]

    def body(*refs):
        piece_refs = refs[:len(pieces)]
        w_ref, x_ref, dx2_ref, nw_ref, _, gx_ref, dnw_ref = refs[len(pieces):]

        @pl.when(pl.program_id(0) == 0)
        def _():
            dnw_ref[...] = jnp.zeros_like(dnw_ref)

        dxn = None
        off = 0
        for p in piece_refs:
            width = p.shape[1]
            for q in range(w_ref.shape[0]):
                lo, hi = max(off, q * PAIR), min(off + width, (q + 1) * PAIR)
                if lo < hi:
                    term = _mm_nt(p[:, lo - off:hi - off], w_ref[q, :, lo - q * PAIR:hi - q * PAIR])
                    dxn = term if dxn is None else dxn + term
            off += width
        xv = x_ref[...]
        r = lax.rsqrt(jnp.mean(xv * xv, axis=-1, keepdims=True) + EPS)
        xh = xv * r
        dxh = dxn * nw_ref[...]
        gx_ref[...] = dx2_ref[...] + r * (dxh - xh * jnp.mean(dxh * xh, axis=-1, keepdims=True))
        row0 = lax.broadcasted_iota(jnp.int32, (8, D), 0) == 0
        dnw_ref[...] += jnp.where(row0, jnp.sum(dxn * xh, axis=0, keepdims=True), 0.0)

    rowblk = pl.BlockSpec((tb, D), lambda i: (i, 0))
    return pl.pallas_call(
        body, grid=(t // tb,),
        in_specs=[pl.BlockSpec((tb, p.shape[1]), lambda i: (i, 0)) for p in pieces]
        + [pl.BlockSpec(w_p.shape, lambda i: (0, 0, 0), pipeline_mode=pl.Buffered(1)), rowblk, rowblk,
           pl.BlockSpec((1, D), lambda i: (0, 0)), pl.BlockSpec(memory_space=pl.ANY)],
        out_specs=[rowblk, pl.BlockSpec((8, D), lambda i: (0, 0))],
        out_shape=[jax.ShapeDtypeStruct((t, D), F32), jax.ShapeDtypeStruct((8, D), F32)],
        name="inproj_dgrad", compiler_params=_cp(60),
    )(*pieces, w_p, x2d, dx2, norm_w, after)


def _adamw_math(w, g, m, v):
    m = B1 * m + (1.0 - B1) * g
    v = B2 * v + (1.0 - B2) * (g * g)
    m_hat = m / (1.0 - B1 ** STEP)
    v_hat = v / (1.0 - B2 ** STEP)
    delta = -LR * (m_hat / (jnp.sqrt(v_hat) + ADAM_EPS) + WD * w)
    return delta, m, v


def _adamw_shard(recv, sums, chip, w, m, v, rows, name):
    nparts, nr, nc = recv.shape

    def body(chip_ref, own_ref, p_ref, w_ref, m_ref, v_ref, g_ref, d_ref, nm_ref, nv_ref):
        g = own_ref[0].astype(F32)
        for s in range(nparts):
            g = g + p_ref[s].astype(F32)
        d, nm, nv = _adamw_math(w_ref[...], g, m_ref[...], v_ref[...])
        g_ref[...] = g
        d_ref[...] = d
        nm_ref[...] = nm
        nv_ref[...] = nv

    blk = pl.BlockSpec((rows, nc), lambda i, chip_ref: (i, 0))
    return pl.pallas_call(
        body,
        grid_spec=pltpu.PrefetchScalarGridSpec(
            num_scalar_prefetch=1, grid=(nr // rows,),
            in_specs=[pl.BlockSpec((1, rows, nc), lambda i, chip_ref: (chip_ref[0], i, 0)),
                      pl.BlockSpec((nparts, rows, nc), lambda i, chip_ref: (0, i, 0)), blk, blk, blk],
            out_specs=[blk] * 4),
        out_shape=[jax.ShapeDtypeStruct((nr, nc), F32)] * 4,
        name=name, compiler_params=_cp(48),
    )(chip, sums, recv, w, m, v)


def _adamw_sum8(parts, w, m, v, after, name):
    def body(p_ref, w_ref, m_ref, v_ref, _, g_ref, d_ref, nm_ref, nv_ref):
        g = p_ref[0].astype(F32)
        for s in range(1, NDEV):
            g = g + p_ref[s].astype(F32)
        d, nm, nv = _adamw_math(w_ref[...], g, m_ref[...], v_ref[...])
        g_ref[...] = g
        d_ref[...] = d
        nm_ref[...] = nm
        nv_ref[...] = nv

    vm = pl.BlockSpec(memory_space=pltpu.VMEM)
    return pl.pallas_call(
        body, out_shape=[jax.ShapeDtypeStruct(w.shape, F32)] * 4,
        in_specs=[vm, vm, vm, vm, pl.BlockSpec(memory_space=pl.ANY)], out_specs=[vm] * 4, name=name,
    )(parts, w, m, v, after)


SMALL_ROWS = dict(norm_w=0, lower_bound=1, hgrn_norm_w=3, final_norm_w=4, sinks=5, loss=6)


def _pack_small_grads(dnw, dlb, sums, dsink):
    def body(dnw_ref, dlb_ref, sums_ref, dsink_ref, o_ref):
        o_ref[...] = jnp.zeros_like(o_ref)
        o_ref[0:1, :] = dnw_ref[0:1, :]
        o_ref[1:2, :] = dlb_ref[0:1, :]
        o_ref[3:4, :] = sums_ref[1:2, :]
        o_ref[4:5, :] = sums_ref[0:1, :]
        o_ref[5:6, 0:128] = dsink_ref[0:1, :]
        o_ref[6:7, :] = sums_ref[2:3, :]

    return pl.pallas_call(body, out_shape=jax.ShapeDtypeStruct((8, D), F32), name="pack_small_grads")(dnw, dlb, sums, dsink)


def _adamw_small(parts, ws, ms, vs):
    shapes = [a.shape for a in ws]

    def body(p_ref, *refs):
        w, m, v = refs[0:5], refs[5:10], refs[10:15]
        outs = [refs[15 + 5 * i:20 + 5 * i] for i in range(4)]
        loss_ref = refs[35]

        def total(row, width):
            g = p_ref[0, row:row + 1, 0:width]
            for s in range(1, NDEV):
                g = g + p_ref[s, row:row + 1, 0:width]
            return g

        loss_ref[...] = total(6, 128)
        lb = _lower_bound(w[1])
        ga0 = total(1, D) * lb * (1.0 - lb)
        grads = [total(0, D), None, total(3, D), total(4, D), total(5, QH)]
        for i in (0, 2, 3, 4):
            res = (grads[i],) + _adamw_math(w[i][...], grads[i], m[i][...], v[i][...])
            for o, val in zip(outs, res):
                o[i][...] = val
        for r, g in ((0, ga0), (1, -ga0)):
            res = (g,) + _adamw_math(w[1][r:r + 1, :], g, m[1][r:r + 1, :], v[1][r:r + 1, :])
            for o, val in zip(outs, res):
                o[1][r:r + 1, :] = val

    res = pl.pallas_call(
        body, out_shape=[jax.ShapeDtypeStruct(s, F32) for s in shapes] * 4 + [jax.ShapeDtypeStruct((1, 128), F32)],
        name="adamw_small",
    )(parts, *ws, *ms, *vs)
    return [res[5 * i:5 * i + 5] for i in range(4)], res[20][0, 0]


def kernel(x, norm_w, w_in, hgrn_lower_bound, hgrn_norm_w, w_branch_hgrn, attn_sinks, w_branch_attn, w_out, final_norm_w, loss_target, m_norm_w, m_w_in, m_hgrn_lower_bound, m_hgrn_norm_w, m_w_branch_hgrn, m_attn_sinks, m_w_branch_attn, m_w_out, m_final_norm_w, v_norm_w, v_w_in, v_hgrn_lower_bound, v_hgrn_norm_w, v_w_branch_hgrn, v_attn_sinks, v_w_branch_attn, v_w_out, v_final_norm_w):
    t = x.shape[1]
    x2d = x.reshape(t, D)
    tgt = loss_target.reshape(t, D)
    fnw = final_norm_w.reshape(1, D)
    row_blk = min(256, t)
    big_blk = min(512, t)

    chip = (2 * lax.axis_index("x") + lax.axis_index("y")).astype(jnp.int32).reshape(1)
    w_p, xn, xnt, proj_own = _gather_in_projection(w_in[0], x2d, norm_w)
    wbh, wba, wout = (g.reshape(D, D) for g in _gather_square(
        [w_branch_hgrn[0].astype(BF16), w_branch_attn[0].astype(BF16), w_out[0].astype(BF16)], after=w_p))

    proj = _inproj_fwd(xn, w_p, proj_own, chip, min(1024, t))
    o_raw, states = _hgrn_fwd(proj, hgrn_lower_bound, big_blk, HGRN_GROUP)
    o_a = _attn_fwd(proj, attn_sinks)
    (dx2, do_raw, do_a, d_hg, d_agm, gh, ga, mg, dyh, dya, dx2b, sums) = _tail(
        o_raw, o_a, proj, x2d, tgt, wbh, wba, wout, hgrn_norm_w, fnw, row_blk)
    d_aq, d_kv, dsink = _attn_bwd(proj, attn_sinks, do_a)
    d_hgrn, dlb = _hgrn_bwd(proj, hgrn_lower_bound, do_raw, states, big_blk, HGRN_GROUP)
    pieces = (d_hgrn, d_hg, d_aq, d_kv, d_agm)
    dw_cat = _inproj_wgrad(xnt, [d_hgrn], CB, 0, None, "inproj_wgrad_hgrn")
    dw_cat = _inproj_wgrad(xnt, [d_hg, d_aq, d_kv], CB // 2, C_HG, dw_cat, "inproj_wgrad_mid")
    dw_cat = _inproj_wgrad(xnt, [d_agm], CB, C_AG, dw_cat, "inproj_wgrad_gates")

    dwin_r = dw_cat.reshape(D, NDEV, IN_SHARD).transpose(1, 0, 2)
    dwbh, dwba, dwout = _wgrad3(gh, dyh, ga, dya, mg, dx2b, big_blk, after=dw_cat)
    slots = lambda a: a.reshape(NDEV, ROW_SHARD, D)
    got = _exchange_pair([dwin_r], after=dwbh)
    core = lax.axis_index("c").astype(jnp.int32).reshape(1)
    s_in = _pair_sum(dwin_r, got[0], core, 4 * ROW_SHARD, "pair_sum_w_in")
    rin, = _exchange_chips([s_in])
    rbh, rba, rout = _exchange_square([slots(dwbh), slots(dwba), slots(dwout)])
    grad_x, dnw = _inproj_dgrad(pieces, w_p, x2d, dx2, norm_w, big_blk, after=s_in)
    rsm = _exchange_small(_pack_small_grads(dnw, dlb, sums, dsink))
    g_in, d_in, nm_in, nv_in = _adamw_shard(rin, s_in, chip, w_in[0], m_w_in[0], v_w_in[0], 128, "adamw_w_in")
    g_bh, d_bh, nm_bh, nv_bh = _adamw_sum8(
        rbh, w_branch_hgrn[0], m_w_branch_hgrn[0], v_w_branch_hgrn[0], dnw, "adamw_w_bh")
    g_ba, d_ba, nm_ba, nv_ba = _adamw_sum8(
        rba, w_branch_attn[0], m_w_branch_attn[0], v_w_branch_attn[0], dnw, "adamw_w_ba")
    g_out, d_out, nm_out, nv_out = _adamw_sum8(rout, w_out[0], m_w_out[0], v_w_out[0], dnw, "adamw_w_out")
    (sg, sd, sm, sv), loss = _adamw_small(
        rsm,
        (norm_w, hgrn_lower_bound, hgrn_norm_w, fnw, attn_sinks),
        (m_norm_w, m_hgrn_lower_bound, m_hgrn_norm_w, m_final_norm_w.reshape(1, D), m_attn_sinks),
        (v_norm_w, v_hgrn_lower_bound, v_hgrn_norm_w, v_final_norm_w.reshape(1, D), v_attn_sinks))

    def group(s, w_in_v, bh, ba, out):
        nw, lb, hnw, fn, sinks = s
        return (nw, w_in_v[None], lb, hnw, bh[None], sinks, ba[None], out[None], fn.reshape(D))

    return (loss, grad_x.reshape(1, t, D),
            *group(sg, g_in, g_bh, g_ba, g_out), *group(sd, d_in, d_bh, d_ba, d_out),
            *group(sm, nm_in, nm_bh, nm_ba, nm_out), *group(sv, nv_in, nv_bh, nv_ba, nv_out))
```

```python
import jax
import jax.numpy as jnp
from jax import lax
from jax.experimental import pallas as pl
from jax.experimental.pallas import tpu as pltpu
from jax.experimental.pallas import tpu_sc as plsc

F32 = jnp.float32
BF16 = jnp.bfloat16

D = 1024
DIN = 8704
NDEV = 8
IN_SHARD = DIN // NDEV
PAIR = 2 * IN_SHARD
ROW_SHARD = D // NDEV
HEADS = 8
HD = 128
CH = 64
HGRN_GROUP = 8
QH = 16
AB = 128
EPS = 1e-6
NEG = -1e30
ATT_SCALE = 0.125

C_HG = 3072
C_AQ = 4096
C_AK = 5120
C_AV = 5376
C_AG = 5632
C_MH = 6656
C_MA = 7680
CB = 512

LR = 0.001
B1 = 0.9
B2 = 0.999
ADAM_EPS = 1e-08
WD = 0.01
STEP = 10

MESH = pl.DeviceIdType.MESH


def _cp(vmem_mb):
    return pltpu.CompilerParams(vmem_limit_bytes=vmem_mb * 1024 * 1024)


def _mm(a, b):
    return jnp.dot(a, b, preferred_element_type=F32)


def _mm_nt(a, b):
    return lax.dot_general(a, b, (((1,), (1,)), ((), ())), preferred_element_type=F32)


def _mm_tn(a, b):
    return lax.dot_general(a, b, (((0,), (0,)), ((), ())), preferred_element_type=F32)


def _tri3(lower):
    r = lax.broadcasted_iota(jnp.int32, (CH, 3 * CH), 0)
    c = lax.broadcasted_iota(jnp.int32, (CH, 3 * CH), 1)
    c = jnp.where(c >= 2 * CH, c - 2 * CH, jnp.where(c >= CH, c - CH, c))
    return ((r >= c) if lower else (c >= r)).astype(BF16)


def _mm_tri_exact(tri3, g):
    g1 = g.astype(BF16)
    r1 = g - g1.astype(F32)
    g2 = r1.astype(BF16)
    g3 = (r1 - g2.astype(F32)).astype(BF16)
    return _mm(tri3, jnp.concatenate([g1, g2, g3], axis=0))


def _sigmoid(v):
    return 0.5 * jnp.tanh(0.5 * v) + 0.5


def _bf(v):
    return v.astype(BF16)


def _place():
    x, y, c = lax.axis_index("x"), lax.axis_index("y"), lax.axis_index("c")
    return (x, y, c), (x, y, 1 - c), [(1 - x, y), (x, 1 - y), (1 - x, 1 - y)]


def _dev_index(px, py, pc):
    return 4 * px + 2 * py + pc


def _gather_in_projection(w_in_s, x2d, norm_w):
    half = D // 2
    t = x2d.shape[0]
    prep_rows = min(512, t)
    nprep = t // prep_rows

    def body(win_ref, x_hbm, nw_ref, wp_g, xn_hbm, xnt_hbm, proj_hbm, give, take, mine, xbuf, xnbuf, xntbuf, w_own, pbuf,
             send_sems, recv_sems, loc_sem, swap_sems, in_sems, out_sems, own_sem):
        (x, y, c), sibling, chips = _place()
        give[...] = win_ref[pl.ds(pl.multiple_of(half * (1 - c), half), half), :].astype(BF16)
        swap = pltpu.make_async_remote_copy(src_ref=give, dst_ref=take, send_sem=swap_sems.at[0], recv_sem=swap_sems.at[1],
                                            device_id=sibling, device_id_type=MESH)
        swap.start()
        swap.wait()
        own = win_ref[pl.ds(pl.multiple_of(half * c, half), half), :]
        other = take[...].astype(F32)
        mine[...] = jnp.where(c == 0, jnp.concatenate([own, other], axis=1),
                              jnp.concatenate([other, own], axis=1)).astype(BF16)

        def place(px, py, pc):
            return wp_g.at[2 * px + py, pl.ds(pl.multiple_of(half * pc, half), half), :]

        def copy(kind, origin, to, src=mine):
            return pltpu.make_async_remote_copy(
                src_ref=src, dst_ref=place(*origin), send_sem=send_sems.at[kind], recv_sem=recv_sems.at[kind],
                device_id=to, device_id_type=MESH)

        me = (x, y, c)
        local = pltpu.make_async_copy(mine, place(*me), loc_sem)
        local.start()
        first = [copy(0, me, sibling)] + [copy(1 + j, me, (*chip, c)) for j, chip in enumerate(chips)]
        for cp in first:
            cp.start()

        copy(0, (x, y, 1 - c), me).wait_recv()
        local.wait()
        my_chip = 2 * x + y
        fetch = pltpu.make_async_copy(wp_g.at[my_chip], w_own, own_sem)
        fetch.start()

        def rows_of(i):
            return pl.ds(pl.multiple_of(i * prep_rows, prep_rows), prep_rows)

        def load(i, slot):
            return pltpu.make_async_copy(x_hbm.at[rows_of(i), :], xbuf.at[slot], in_sems.at[slot])

        def stores(i, slot):
            own_cols = pl.ds(pl.multiple_of(my_chip * PAIR, 128), PAIR)
            return (pltpu.make_async_copy(xnbuf.at[slot], xn_hbm.at[rows_of(i), :], out_sems.at[slot, 0]),
                    pltpu.make_async_copy(xntbuf.at[slot], xnt_hbm.at[:, rows_of(i)], out_sems.at[slot, 1]),
                    pltpu.make_async_copy(pbuf.at[slot], proj_hbm.at[rows_of(i), own_cols], out_sems.at[slot, 2]))

        load(0, 0).start()
        fetch.wait()

        def prep(i, carry):
            slot = lax.rem(i, 2)
            load(i, slot).wait()

            @pl.when(i + 1 < nprep)
            def _():
                load(i + 1, 1 - slot).start()

            @pl.when(i >= 2)
            def _():
                for cp in stores(i - 2, slot):
                    cp.wait()

            xv = xbuf[slot]
            xn = (xv * lax.rsqrt(jnp.mean(xv * xv, axis=-1, keepdims=True) + EPS)) * nw_ref[...]
            xn_b = xn.astype(BF16)
            xnbuf[slot] = xn_b
            xntbuf[slot] = xn.T.astype(BF16)
            pbuf[slot] = _mm(xn_b, w_own[...])
            for cp in stores(i, slot):
                cp.start()
            return carry

        lax.fori_loop(0, nprep, prep, 0)
        for i in range(max(nprep - 2, 0), nprep):
            for cp in stores(i, i % 2):
                cp.wait()

        passed = []
        for j, chip in enumerate(chips):
            copy(1 + j, (*chip, c), me).wait_recv()
            cp = copy(4 + j, (*chip, c), sibling, src=place(*chip, c))
            cp.start()
            passed.append(cp)
        for j, chip in enumerate(chips):
            copy(4 + j, (*chip, 1 - c), me).wait_recv()
        for cp in first + passed:
            cp.wait_send()

    vm = pl.BlockSpec(memory_space=pltpu.VMEM)
    hbm = pl.BlockSpec(memory_space=pl.ANY)
    return pl.pallas_call(
        body,
        out_shape=[jax.ShapeDtypeStruct((NDEV // 2, D, PAIR), BF16), jax.ShapeDtypeStruct((t, D), BF16),
                   jax.ShapeDtypeStruct((D, t), BF16), jax.ShapeDtypeStruct((t, DIN), F32)],
        in_specs=[vm, hbm, vm],
        out_specs=[hbm, hbm, hbm, hbm],
        scratch_shapes=[pltpu.VMEM((half, IN_SHARD), BF16), pltpu.VMEM((half, IN_SHARD), BF16),
                        pltpu.VMEM((half, PAIR), BF16),
                        pltpu.VMEM((2, prep_rows, D), F32), pltpu.VMEM((2, prep_rows, D), BF16),
                        pltpu.VMEM((2, D, prep_rows), BF16),
                        pltpu.VMEM((D, PAIR), BF16), pltpu.VMEM((2, prep_rows, PAIR), F32),
                        pltpu.SemaphoreType.DMA((NDEV - 1,)), pltpu.SemaphoreType.DMA((NDEV - 1,)),
                        pltpu.SemaphoreType.DMA, pltpu.SemaphoreType.DMA((2,)),
                        pltpu.SemaphoreType.DMA((2,)), pltpu.SemaphoreType.DMA((2, 3)), pltpu.SemaphoreType.DMA],
        name="gather_in_projection", compiler_params=_cp(56),
    )(w_in_s, x2d, norm_w)


def _gather_square(shards, after):
    n = len(shards)

    def launch(*refs):
        ins, outs = refs[:n], refs[n + 1:2 * n + 1]
        send_sems, recv_sems, loc_sems = refs[2 * n + 1:]
        (x, y, c), _, _ = _place()
        me = _dev_index(x, y, c)
        peers = [(1 - x if r & 4 else x, 1 - y if r & 2 else y, 1 - c if r & 1 else c) for r in range(1, NDEV)]
        barrier = pltpu.get_barrier_semaphore()
        for peer in peers:
            pl.semaphore_signal(barrier, inc=1, device_id=peer, device_id_type=MESH)
        pl.semaphore_wait(barrier, NDEV - 1)
        local = [pltpu.make_async_copy(ins[k], outs[k].at[me], loc_sems.at[k]) for k in range(n)]
        copies = [pltpu.make_async_remote_copy(
            src_ref=ins[k], dst_ref=outs[k].at[me], send_sem=send_sems.at[r, k], recv_sem=recv_sems.at[r, k],
            device_id=peer, device_id_type=MESH) for r, peer in enumerate(peers) for k in range(n)]
        for cp in local + copies:
            cp.start()
        for r, peer in enumerate(peers):
            for k in range(n):
                pltpu.make_async_remote_copy(
                    src_ref=ins[k], dst_ref=outs[k].at[_dev_index(*peer)], send_sem=send_sems.at[r, k],
                    recv_sem=recv_sems.at[r, k], device_id=peer, device_id_type=MESH).wait_recv()
        for cp in copies:
            cp.wait_send()
        for cp in local:
            cp.wait()

    return pl.kernel(
        launch, out_type=[jax.ShapeDtypeStruct((NDEV,) + a.shape, a.dtype) for a in shards],
        mesh=plsc.ScalarSubcoreMesh(axis_name="sequencer", num_cores=1), name="gather_square",
        scratch_types=(pltpu.SemaphoreType.DMA((NDEV - 1, n)), pltpu.SemaphoreType.DMA((NDEV - 1, n)),
                       pltpu.SemaphoreType.DMA((n,))),
        compiler_params=pltpu.CompilerParams(collective_id=2),
    )(*shards, after)


def _exchange_pair(arrs, after):
    n = len(arrs)

    def launch(*refs):
        ins, got = refs[:n], refs[n + 1:2 * n + 1]
        send_sems, recv_sems = refs[2 * n + 1:]
        (x, y, c), sibling, _ = _place()
        barrier = pltpu.get_barrier_semaphore()
        pl.semaphore_signal(barrier, inc=1, device_id=sibling, device_id_type=MESH)
        pl.semaphore_wait(barrier, 1)
        sends = [pltpu.make_async_remote_copy(
            src_ref=ins[k].at[_dev_index(q // 2, q % 2, 1 - c)], dst_ref=got[k].at[q], send_sem=send_sems.at[q, k],
            recv_sem=recv_sems.at[q, k], device_id=sibling, device_id_type=MESH) for q in range(4) for k in range(n)]
        for cp in sends:
            cp.start()
        for cp in sends:
            cp.wait_recv()
        for cp in sends:
            cp.wait_send()

    return pl.kernel(
        launch, out_type=[jax.ShapeDtypeStruct((4,) + a.shape[1:], a.dtype) for a in arrs],
        mesh=plsc.ScalarSubcoreMesh(axis_name="sequencer", num_cores=1), name="exchange_pair",
        scratch_types=(pltpu.SemaphoreType.DMA((4, n)), pltpu.SemaphoreType.DMA((4, n))),
        compiler_params=pltpu.CompilerParams(collective_id=0),
    )(*arrs, after)


def _pair_sum(full, got, core, rows, name):
    _, nr, nc = got.shape

    def body(core_ref, a_ref, b_ref, o_ref):
        o_ref[...] = (a_ref[...].astype(F32) + b_ref[...].astype(F32)).astype(BF16)

    blk = pl.BlockSpec((1, rows, nc), lambda q, i, core_ref: (q, i, 0))
    return pl.pallas_call(
        body,
        grid_spec=pltpu.PrefetchScalarGridSpec(
            num_scalar_prefetch=1, grid=(4, nr // rows),
            in_specs=[pl.BlockSpec((1, rows, nc), lambda q, i, core_ref: (2 * q + core_ref[0], i, 0)), blk],
            out_specs=blk),
        out_shape=jax.ShapeDtypeStruct(got.shape, BF16), name=name,
    )(core, full, got)


def _exchange_chips(sums):
    n = len(sums)

    def launch(*refs):
        ins, outs = refs[:n], refs[n:2 * n]
        send_sems, recv_sems = refs[2 * n:]
        (x, y, c), _, chips = _place()
        barrier = pltpu.get_barrier_semaphore()
        for px, py in chips:
            pl.semaphore_signal(barrier, inc=1, device_id=(px, py, c), device_id_type=MESH)
        pl.semaphore_wait(barrier, len(chips))
        copies = [pltpu.make_async_remote_copy(
            src_ref=ins[k].at[2 * px + py], dst_ref=outs[k].at[j], send_sem=send_sems.at[j, k],
            recv_sem=recv_sems.at[j, k], device_id=(px, py, c), device_id_type=MESH)
            for j, (px, py) in enumerate(chips) for k in range(n)]
        for cp in copies:
            cp.start()
        for cp in copies:
            cp.wait_recv()
        for cp in copies:
            cp.wait_send()

    return pl.kernel(
        launch, out_type=[jax.ShapeDtypeStruct((3,) + a.shape[1:], a.dtype) for a in sums],
        mesh=plsc.ScalarSubcoreMesh(axis_name="sequencer", num_cores=1), name="exchange_chips",
        scratch_types=(pltpu.SemaphoreType.DMA((3, n)), pltpu.SemaphoreType.DMA((3, n))),
        compiler_params=pltpu.CompilerParams(collective_id=1),
    )(*sums)


def _exchange_square(partials):
    n = len(partials)

    def launch(*refs):
        ins, outs = refs[:n], refs[n:2 * n]
        send_sems, recv_sems, loc_sems = refs[2 * n:]
        (x, y, c), _, _ = _place()
        me = _dev_index(x, y, c)
        peers = [(1 - x if r & 4 else x, 1 - y if r & 2 else y, 1 - c if r & 1 else c) for r in range(1, NDEV)]
        barrier = pltpu.get_barrier_semaphore()
        for peer in peers:
            pl.semaphore_signal(barrier, inc=1, device_id=peer, device_id_type=MESH)
        pl.semaphore_wait(barrier, NDEV - 1)
        local = [pltpu.make_async_copy(ins[k].at[me], outs[k].at[me], loc_sems.at[k]) for k in range(n)]
        copies = [pltpu.make_async_remote_copy(
            src_ref=ins[k].at[_dev_index(*peer)], dst_ref=outs[k].at[me], send_sem=send_sems.at[r, k],
            recv_sem=recv_sems.at[r, k], device_id=peer, device_id_type=MESH)
            for r, peer in enumerate(peers) for k in range(n)]
        for cp in local + copies:
            cp.start()
        for r, peer in enumerate(peers):
            for k in range(n):
                pltpu.make_async_remote_copy(
                    src_ref=ins[k].at[me], dst_ref=outs[k].at[_dev_index(*peer)], send_sem=send_sems.at[r, k],
                    recv_sem=recv_sems.at[r, k], device_id=peer, device_id_type=MESH).wait_recv()
        for cp in copies:
            cp.wait_send()
        for cp in local:
            cp.wait()

    return pl.kernel(
        launch, out_type=[jax.ShapeDtypeStruct(a.shape, a.dtype) for a in partials],
        mesh=plsc.ScalarSubcoreMesh(axis_name="sequencer", num_cores=1), name="exchange_square",
        scratch_types=(pltpu.SemaphoreType.DMA((NDEV - 1, n)), pltpu.SemaphoreType.DMA((NDEV - 1, n)),
                       pltpu.SemaphoreType.DMA((n,))),
        compiler_params=pltpu.CompilerParams(collective_id=3),
    )(*partials)


def _exchange_small(small):
    def launch(sm_ref, out_ref, send_sems, recv_sems, loc_sem):
        (x, y, c), _, _ = _place()
        me = _dev_index(x, y, c)
        peers = [(1 - x if r & 4 else x, 1 - y if r & 2 else y, 1 - c if r & 1 else c) for r in range(1, NDEV)]
        barrier = pltpu.get_barrier_semaphore()
        for peer in peers:
            pl.semaphore_signal(barrier, inc=1, device_id=peer, device_id_type=MESH)
        pl.semaphore_wait(barrier, NDEV - 1)
        local = pltpu.make_async_copy(sm_ref, out_ref.at[me], loc_sem)
        local.start()
        copies = [pltpu.make_async_remote_copy(
            src_ref=sm_ref, dst_ref=out_ref.at[me], send_sem=send_sems.at[r], recv_sem=recv_sems.at[r],
            device_id=peer, device_id_type=MESH) for r, peer in enumerate(peers)]
        for cp in copies:
            cp.start()
        for r, peer in enumerate(peers):
            pltpu.make_async_remote_copy(
                src_ref=sm_ref, dst_ref=out_ref.at[_dev_index(*peer)], send_sem=send_sems.at[r], recv_sem=recv_sems.at[r],
                device_id=peer, device_id_type=MESH).wait_recv()
        for cp in copies:
            cp.wait_send()
        local.wait()

    return pl.kernel(
        launch, out_type=jax.ShapeDtypeStruct((NDEV,) + small.shape, F32),
        mesh=plsc.ScalarSubcoreMesh(axis_name="sequencer", num_cores=1), name="exchange_small",
        scratch_types=(pltpu.SemaphoreType.DMA((NDEV - 1,)), pltpu.SemaphoreType.DMA((NDEV - 1,)), pltpu.SemaphoreType.DMA),
        compiler_params=pltpu.CompilerParams(collective_id=4),
    )(small)


def _inproj_fwd(xn, w_pairs, proj, chip, tb):
    t = xn.shape[0]
    nblk, _, nb = w_pairs.shape

    def body(chip_ref, xn_ref, w_ref, proj_in, proj_ref):
        proj_ref[...] = _mm(xn_ref[...], w_ref[0])

    def other(j, chip_ref):
        return j + (j >= chip_ref[0]).astype(jnp.int32)

    return pl.pallas_call(
        body,
        grid_spec=pltpu.PrefetchScalarGridSpec(
            num_scalar_prefetch=1, grid=(t // tb, nblk - 1),
            in_specs=[pl.BlockSpec((tb, D), lambda i, j, chip_ref: (i, 0)),
                      pl.BlockSpec((1, D, nb), lambda i, j, chip_ref: (other(j, chip_ref), 0, 0)),
                      pl.BlockSpec(memory_space=pl.ANY)],
            out_specs=pl.BlockSpec((tb, nb), lambda i, j, chip_ref: (i, other(j, chip_ref)))),
        out_shape=jax.ShapeDtypeStruct((t, DIN), F32),
        input_output_aliases={3: 0},
        name="inproj_fwd", compiler_params=_cp(56),
    )(chip, xn, w_pairs, proj)


def _lower_bound(lb_ref):
    a0 = lb_ref[0:1, :]
    a1 = lb_ref[1:2, :]
    mx = jnp.maximum(a0, a1)
    e0 = jnp.exp(a0 - mx)
    e1 = jnp.exp(a1 - mx)
    return e0 / (e0 + e1)


def _hgrn_chunk_fwd(hq, hf, lb, tril):
    sg = _sigmoid(hf)
    f = lb + (1.0 - lb) * sg
    g = jnp.log(f)
    k = 1.0 - f
    sq = _sigmoid(hq)
    q = hq * sq
    b = _mm_tri_exact(tril, g)
    last_row = lax.broadcasted_iota(jnp.int32, b.shape, 0) == CH - 1
    b_last = jnp.sum(jnp.where(last_row, b, 0.0), axis=0, keepdims=True)
    c = 0.5 * b_last
    eb = jnp.exp(b)
    ea = jnp.exp(b - c)
    ek = jnp.exp(c - b)
    ed = jnp.exp(b_last - b)
    ebl = jnp.exp(b_last)
    return dict(sg=sg, f=f, k=k, sq=sq, q=q, eb=eb, ea=ea, ek=ek, ed=ed, ebl=ebl,
                qe=q * eb, qa=q * ea, ka=k * ek, kd=k * ed)


def _tri(lower):
    r = lax.broadcasted_iota(jnp.int32, (CH, CH), 0)
    c = lax.broadcasted_iota(jnp.int32, (CH, CH), 1)
    return (r >= c) if lower else (c >= r)


def _head_segment(p_ref, rows, j, hg):
    return p_ref[rows, j * HD * hg:(j + 1) * HD * hg]


def _head(a, k):
    return a[:, k * HD:(k + 1) * HD]


def _hgrn_fwd(proj, lbw, rb, hg):
    assert hg == HEADS
    t = proj.shape[0]
    ncb = rb // CH

    def body(p_ref, lb_ref, o_ref, st_ref, s_scr):
        @pl.when(pl.program_id(1) == 0)
        def _():
            s_scr[...] = jnp.zeros_like(s_scr)

        lb = _lower_bound(lb_ref)
        causal = _tri(True)
        tril = _tri3(True)
        heads = range(hg)

        def chunk(cc, carry):
            r0 = pl.multiple_of(cc * CH, CH)
            rows = pl.ds(r0, CH)
            e = _hgrn_chunk_fwd(_head_segment(p_ref, rows, 0, hg), _head_segment(p_ref, rows, 1, hg), lb, tril)
            v = _bf(_head_segment(p_ref, rows, 2, hg))
            sts = [s_scr[k] for k in heads]
            qa, ka, qe, kd = _bf(e["qa"]), _bf(e["ka"]), _bf(e["qe"]), _bf(e["kd"])
            a = [_bf(jnp.where(causal, _mm_nt(_head(qa, k), _head(ka, k)), 0.0)) for k in heads]
            o_inter = [_mm_nt(_head(qe, k), _bf(sts[k])) for k in heads]
            kv = [_mm_tn(_head(v, k), _head(kd, k)) for k in heads]
            o_intra = [_mm(a[k], _head(v, k)) for k in heads]
            for k in heads:
                st_ref[cc, k] = sts[k]
                o_ref[rows, k * HD:(k + 1) * HD] = o_inter[k] + o_intra[k]
                s_scr[k] = sts[k] * _head(e["ebl"], k) + kv[k]
            return carry

        lax.fori_loop(0, ncb, chunk, 0, unroll=4)

    return pl.pallas_call(
        body, grid=(HEADS // hg, t // rb),
        in_specs=[pl.BlockSpec((rb, 3 * HD * hg), lambda h, i: (i, h)), pl.BlockSpec((2, HD * hg), lambda h, i: (0, h))],
        out_specs=[pl.BlockSpec((rb, HD * hg), lambda h, i: (i, h)),
                   pl.BlockSpec((ncb, hg, HD, HD), lambda h, i: (i, h, 0, 0))],
        out_shape=[jax.ShapeDtypeStruct((t, D), F32), jax.ShapeDtypeStruct((t // CH, HEADS, HD, HD), F32)],
        scratch_shapes=[pltpu.VMEM((hg, HD, HD), F32)],
        name="hgrn_fwd", compiler_params=_cp(48),
    )(proj, lbw)


def _hgrn_bwd(proj, lbw, do_raw, states, rb, hg):
    assert hg == HEADS
    t = proj.shape[0]
    nblk = t // rb
    ncb = rb // CH
    wd = HD * hg

    def body(p_ref, lb_ref, do_ref, st_ref, dp_ref, dlb_ref, ds_scr):
        @pl.when(pl.program_id(1) == 0)
        def _():
            ds_scr[...] = jnp.zeros_like(ds_scr)
            dlb_ref[...] = jnp.zeros_like(dlb_ref)

        lb = _lower_bound(lb_ref)
        causal = _tri(True)
        tril = _tri3(True)
        triu = _tri3(False)
        last_row = lax.broadcasted_iota(jnp.int32, (CH, HD * hg), 0) == CH - 1
        row0 = lax.broadcasted_iota(jnp.int32, (8, HD * hg), 0) == 0
        heads = range(hg)
        wide = lambda parts: jnp.concatenate(parts, axis=1)

        def chunk(it, carry):
            cc = ncb - 1 - it
            r0 = pl.multiple_of(cc * CH, CH)
            rows = pl.ds(r0, CH)
            hq = _head_segment(p_ref, rows, 0, hg)
            e = _hgrn_chunk_fwd(hq, _head_segment(p_ref, rows, 1, hg), lb, tril)
            v = _bf(_head_segment(p_ref, rows, 2, hg))
            do = _bf(do_ref[rows, :])
            sts = [st_ref[cc, k] for k in heads]
            dsts = [ds_scr[k] for k in heads]
            dlb_acc = dlb_ref[...]
            qa, ka, qe, kd = _bf(e["qa"]), _bf(e["ka"]), _bf(e["qe"]), _bf(e["kd"])
            a = [_bf(jnp.where(causal, _mm_nt(_head(qa, k), _head(ka, k)), 0.0)) for k in heads]
            da = [_bf(jnp.where(causal, _mm_nt(_head(do, k), _head(v, k)), 0.0)) for k in heads]
            dqe = wide([_mm(_head(do, k), _bf(sts[k])) for k in heads])
            dkd = wide([_mm(_head(v, k), _bf(dsts[k])) for k in heads])
            dv_state = [_mm_nt(_head(kd, k), _bf(dsts[k])) for k in heads]
            ds_new = [_mm_tn(_head(do, k), _head(qe, k)) for k in heads]
            dv_intra = [_mm_tn(a[k], _head(do, k)) for k in heads]
            dqa = wide([_mm(da[k], _head(ka, k)) for k in heads])
            dka = wide([_mm_tn(da[k], _head(qa, k)) for k in heads])
            dv = wide([dv_intra[k] + dv_state[k] for k in heads])
            dbl = e["ebl"] * wide([jnp.sum(sts[k] * dsts[k], axis=0, keepdims=True) for k in heads])
            dq = dqe * e["eb"] + dqa * e["ea"]
            dk = dka * e["ek"] + dkd * e["ed"]
            dkd_kd = dkd * kd.astype(F32)
            db = dqe * qe.astype(F32) + dqa * qa.astype(F32) - dka * ka.astype(F32) - dkd_kd
            db = db + jnp.where(last_row, dbl + jnp.sum(dkd_kd, axis=0, keepdims=True), 0.0)
            dg = _mm_tri_exact(triu, db)
            df = dg / e["f"] - dk
            sg = e["sg"]
            sq = e["sq"]
            dhq = _bf(dq * (sq * (1.0 + hq * (1.0 - sq))))
            dhf = _bf(df * (1.0 - lb) * sg * (1.0 - sg))
            dhi = _bf(dv)
            dlb_new = dlb_acc + jnp.where(row0, jnp.sum(df * (1.0 - sg), axis=0, keepdims=True), 0.0)
            for k in heads:
                ds_scr[k] = ds_new[k] + dsts[k] * _head(e["ebl"], k)
            dp_ref[rows, 0:wd] = dhq
            dp_ref[rows, wd:2 * wd] = dhf
            dp_ref[rows, 2 * wd:3 * wd] = dhi
            dlb_ref[...] = dlb_new
            return carry

        lax.fori_loop(0, ncb, chunk, 0, unroll=2)

    rev = lambda h, i: (nblk - 1 - i, h)
    return pl.pallas_call(
        body, grid=(HEADS // hg, nblk),
        in_specs=[pl.BlockSpec((rb, 3 * HD * hg), rev), pl.BlockSpec((2, HD * hg), lambda h, i: (0, h)),
                  pl.BlockSpec((rb, HD * hg), rev), pl.BlockSpec((ncb, hg, HD, HD), lambda h, i: (nblk - 1 - i, h, 0, 0))],
        out_specs=[pl.BlockSpec((rb, 3 * HD * hg), rev), pl.BlockSpec((8, HD * hg), lambda h, i: (0, h))],
        out_shape=[jax.ShapeDtypeStruct((t, 3 * D), BF16), jax.ShapeDtypeStruct((8, D), F32)],
        scratch_shapes=[pltpu.VMEM((hg, HD, HD), F32)],
        name="hgrn_bwd", compiler_params=_cp(48),
    )(proj, lbw, do_raw, states)


def _kv_variants(tile, odd):
    low = lax.broadcasted_iota(jnp.int32, tile.shape, 1) < 64
    if odd:
        hi = jnp.where(low, 0.0, tile)
        lo = pltpu.roll(hi, 64, 1)
    else:
        lo = jnp.where(low, tile, 0.0)
        hi = pltpu.roll(lo, 64, 1)
    return _bf(lo), _bf(hi)


def _attn_masks(n):
    qi = lax.broadcasted_iota(jnp.int32, (AB, AB), 0)
    kj = lax.broadcasted_iota(jnp.int32, (AB, AB), 1)
    cur = kj <= qi
    return cur, cur | (n > 0), qi <= kj


def _kv_all(prev_ref, cur_ref):
    out = []
    for tl in range(2):
        cols = slice(tl * 128, (tl + 1) * 128)
        tile = jnp.concatenate([prev_ref[:, cols], cur_ref[:, cols]], axis=0)
        out.append(_kv_variants(tile, 0))
        out.append(_kv_variants(tile, 1))
    return out


def _window(a2, cur):
    return jnp.where(cur, a2[:, AB:], a2[:, :AB])


def _attn_softmax(scores, sinks, cur, ok):
    s = [jnp.where(ok, _window(s2, cur) * ATT_SCALE, NEG) for s2 in scores]
    m = [jnp.maximum(jnp.max(si, axis=-1, keepdims=True), sink) for si, sink in zip(s, sinks)]
    p = [jnp.exp(si - mi) for si, mi in zip(s, m)]
    es = [jnp.exp(sink - mi) for sink, mi in zip(sinks, m)]
    inv = [1.0 / (jnp.sum(pi, axis=-1, keepdims=True) + ei) for pi, ei in zip(p, es)]
    return [pi * ii for pi, ii in zip(p, inv)], [ei * ii for ei, ii in zip(es, inv)]


def _spread(pc, cur):
    return jnp.concatenate([jnp.where(cur, 0.0, pc), jnp.where(cur, pc, 0.0)], axis=1)


def _spread_t(pct, cur_t):
    return jnp.concatenate([jnp.where(cur_t, 0.0, pct), jnp.where(cur_t, pct, 0.0)], axis=0)


def _attn_fwd(proj, sinks):
    t = proj.shape[0]
    nb = t // AB

    nsub = 2
    assert nb % nsub == 0

    def body(q_ref, kc_ref, kp_ref, vc_ref, vp_ref, sink_ref, o_ref):
        sinks_v = [sink_ref[0, h] for h in range(QH)]
        heads = [(j, ab) for j in range(8) for ab in range(2)]
        for sb in range(nsub):
            rows = pl.ds(AB * sb, AB)
            before = pl.ds(AB * (sb - 1), AB)
            cur, ok, _ = _attn_masks(nsub * pl.program_id(0) + sb)
            kvars = _kv_all(kp_ref if sb == 0 else kc_ref.at[before, :], kc_ref.at[rows, :])
            vvars = _kv_all(vp_ref if sb == 0 else vc_ref.at[before, :], vc_ref.at[rows, :])
            qps = [_bf(q_ref[rows, 128 * j:128 * (j + 1)]) for j in range(8)]
            scores = [_mm_nt(qps[j], kvars[j // 2][ab]) for j, ab in heads]
            pcs, _ = _attn_softmax(scores, sinks_v, cur, ok)
            parts = [_mm(_bf(_spread(pcs[h], cur)), vvars[j // 2][ab]) for h, (j, ab) in enumerate(heads)]
            for j in range(8):
                o_ref[rows, 128 * j:128 * (j + 1)] = parts[2 * j] + parts[2 * j + 1]

    prev = lambda n: jnp.maximum(nsub * n - 1, 0)
    step = nsub * AB
    return pl.pallas_call(
        body, grid=(nb // nsub,),
        in_specs=[pl.BlockSpec((step, D), lambda n: (n, C_AQ // D)),
                  pl.BlockSpec((step, 256), lambda n: (n, C_AK // 256)),
                  pl.BlockSpec((AB, 256), lambda n: (prev(n), C_AK // 256)),
                  pl.BlockSpec((step, 256), lambda n: (n, C_AV // 256)),
                  pl.BlockSpec((AB, 256), lambda n: (prev(n), C_AV // 256)),
                  pl.BlockSpec(memory_space=pltpu.SMEM)],
        out_specs=pl.BlockSpec((step, D), lambda n: (n, 0)),
        out_shape=jax.ShapeDtypeStruct((t, D), F32),
        name="attn_fwd", compiler_params=_cp(32),
    )(proj, proj, proj, proj, proj, sinks)


def _attn_bwd(proj, sinks, do_a):
    t = proj.shape[0]
    nb = t // AB
    nsub = 2
    assert nb % nsub == 0
    steps = nb // nsub
    step = nsub * AB

    def body(q_ref, kc_ref, kp_ref, vc_ref, vp_ref, do_ref, sink_ref, dq_ref, dkv_ref, dsink_ref, carry):
        n = pl.program_id(0)

        @pl.when(n == 0)
        def _():
            dsink_ref[...] = jnp.zeros_like(dsink_ref)
            carry[...] = jnp.zeros_like(carry)

        def one_block(sb):
            rows = pl.ds(AB * sb, AB)
            before = pl.ds(AB * (sb - 1), AB)
            cur, ok, cur_t = _attn_masks(nsub * n + sb)
            low = lax.broadcasted_iota(jnp.int32, (2 * AB, 128), 1) < 64
            lane = lax.broadcasted_iota(jnp.int32, (8, 128), 1)
            row0 = lax.broadcasted_iota(jnp.int32, (8, 128), 0) == 0
            kvars = _kv_all(kp_ref if sb == 0 else kc_ref.at[before, :], kc_ref.at[rows, :])
            vvars = _kv_all(vp_ref if sb == 0 else vc_ref.at[before, :], vc_ref.at[rows, :])
            qps = [_bf(q_ref[rows, 128 * j:128 * (j + 1)]) for j in range(8)]
            dops = [_bf(do_ref[rows, 128 * j:128 * (j + 1)]) for j in range(8)]
            heads = [(j, ab) for j in range(8) for ab in range(2)]
            scores = [_mm_nt(qps[j], kvars[j // 2][ab]) for j, ab in heads]
            dps = [_mm_nt(dops[j], vvars[j // 2][ab]) for j, ab in heads]
            pcs, pss = _attn_softmax(scores, [sink_ref[0, h] for h in range(QH)], cur, ok)
            dpcs = [_window(dp2, cur) for dp2 in dps]
            rss = [jnp.sum(pc * dpc, axis=-1, keepdims=True) for pc, dpc in zip(pcs, dpcs)]
            dscs = [pc * (dpc - rs) for pc, dpc, rs in zip(pcs, dpcs, rss)]
            dsink = jnp.zeros((8, 128), F32)
            for h in range(QH):
                dsink = dsink + jnp.where(row0 & (lane == h), -jnp.sum(pss[h] * rss[h]), 0.0)
            dq_terms = [_mm(_bf(_spread(dscs[h], cur)), kvars[j // 2][ab]) for h, (j, ab) in enumerate(heads)]
            for j in range(8):
                dq_ref[rows, 128 * j:128 * (j + 1)] = _bf((dq_terms[2 * j] + dq_terms[2 * j + 1]) * ATT_SCALE)
            dsc_t = [_bf(_spread_t(dsc.T, cur_t)) for dsc in dscs]
            pc_t = [_bf(_spread_t(pc.T, cur_t)) for pc in pcs]
            dk_terms = [_mm(dsc_t[h], qps[j]) for h, (j, ab) in enumerate(heads)]
            dv_terms = [_mm(pc_t[h], dops[j]) for h, (j, ab) in enumerate(heads)]
            dk_ab = [[dk_terms[4 * g + ab] + dk_terms[4 * g + 2 + ab] for ab in range(2)] for g in range(4)]
            dv_ab = [[dv_terms[4 * g + ab] + dv_terms[4 * g + 2 + ab] for ab in range(2)] for g in range(4)]
            dkts, dvts = [], []
            for tl in range(2):
                ke, ko = dk_ab[2 * tl], dk_ab[2 * tl + 1]
                ve, vo = dv_ab[2 * tl], dv_ab[2 * tl + 1]
                dkts.append((jnp.where(low, ke[0], 0.0) + pltpu.roll(jnp.where(low, 0.0, ke[1]), 64, 1)
                             + jnp.where(low, 0.0, ko[1]) + pltpu.roll(jnp.where(low, ko[0], 0.0), 64, 1)) * ATT_SCALE)
                dvts.append(jnp.where(low, ve[0], 0.0) + pltpu.roll(jnp.where(low, 0.0, ve[1]), 64, 1)
                            + jnp.where(low, 0.0, vo[1]) + pltpu.roll(jnp.where(low, vo[0], 0.0), 64, 1))
            return dkts, dvts, dsink

        @pl.when(n < steps)
        def _():
            (dk0, dv0, ds0), (dk1, dv1, ds1) = one_block(0), one_block(1)
            first, second = slice(0, AB), slice(AB, 2 * AB)
            for tl in range(2):
                for cols, g0, g1 in ((slice(tl * 128, (tl + 1) * 128), dk0[tl], dk1[tl]),
                                     (slice(256 + tl * 128, 256 + (tl + 1) * 128), dv0[tl], dv1[tl])):
                    dkv_ref[first, cols] = _bf(carry[first, cols])
                    dkv_ref[second, cols] = _bf(carry[second, cols] + g0[first])
                    carry[first, cols] = g0[second] + g1[first]
                    carry[second, cols] = g1[second]
            dsink_ref[...] += ds0 + ds1

        @pl.when(n == steps)
        def _():
            dkv_ref[...] = _bf(carry[...])

    cur = lambda n: jnp.minimum(n, steps - 1)
    last = lambda n: jnp.clip(n - 1, 0, steps - 1)
    prev = lambda n: jnp.clip(nsub * n - 1, 0, nb - 1)
    return pl.pallas_call(
        body, grid=(steps + 1,),
        in_specs=[pl.BlockSpec((step, D), lambda n: (cur(n), C_AQ // D)),
                  pl.BlockSpec((step, 256), lambda n: (cur(n), C_AK // 256)),
                  pl.BlockSpec((AB, 256), lambda n: (prev(n), C_AK // 256)),
                  pl.BlockSpec((step, 256), lambda n: (cur(n), C_AV // 256)),
                  pl.BlockSpec((AB, 256), lambda n: (prev(n), C_AV // 256)),
                  pl.BlockSpec((step, D), lambda n: (cur(n), 0)),
                  pl.BlockSpec(memory_space=pltpu.SMEM)],
        out_specs=[pl.BlockSpec((step, D), lambda n: (cur(n), 0)),
                   pl.BlockSpec((step, 512), lambda n: (last(n), 0)),
                   pl.BlockSpec((8, 128), lambda n: (0, 0))],
        out_shape=[jax.ShapeDtypeStruct((t, D), BF16), jax.ShapeDtypeStruct((t, 512), BF16),
                   jax.ShapeDtypeStruct((8, 128), F32)],
        scratch_shapes=[pltpu.VMEM((step, 512), F32)],
        name="attn_bwd", compiler_params=_cp(40),
    )(proj, proj, proj, proj, proj, do_a, sinks)


def _silu_and_grad(v):
    s = _sigmoid(v)
    return v * s, s * (1.0 + v * (1.0 - s))


def _tail(o_raw, o_a, proj, x2d, tgt, wbh, wba, wout, hnw, fnw, tb):
    t = x2d.shape[0]

    def body(or_ref, oa_ref, hg_ref, ag0, ag1, mh0, mh1, ma0, ma1, x_ref, t_ref, wbh_ref, wba_ref, wout_ref, hnw_ref,
             fnw_ref, dx2_ref, dor_ref, doa_ref, dhg_ref, dagm_ref, gh_ref, ga_ref, mg_ref, dyh_ref, dya_ref, dx2b_ref,
             sums_ref):
        @pl.when(pl.program_id(0) == 0)
        def _():
            sums_ref[...] = jnp.zeros_like(sums_ref)

        halves = lambda a, b: jnp.concatenate([a[...], b[...]], axis=1)
        hnw_v = hnw_ref[...]
        fnw_v = fnw_ref[...]
        o = or_ref[...]
        rs, xhs = [], []
        for h in range(HEADS):
            oh = o[:, h * HD:(h + 1) * HD]
            r = lax.rsqrt(jnp.mean(oh * oh, axis=-1, keepdims=True) + EPS)
            rs.append(r)
            xhs.append(oh * r)
        xh = jnp.concatenate(xhs, axis=1)
        on = xh * hnw_v
        sil_hg, dsil_hg = _silu_and_grad(hg_ref[...])
        gh_b = _bf(on * sil_hg)
        y_h = _mm(gh_b, wbh_ref[...])
        oa = oa_ref[...]
        sil_ag, dsil_ag = _silu_and_grad(halves(ag0, ag1))
        ga_b = _bf(oa * sil_ag)
        y_a = _mm(ga_b, wba_ref[...])
        s_mh = _sigmoid(halves(mh0, mh1))
        s_ma = _sigmoid(halves(ma0, ma1))
        mg_b = _bf(s_mh * y_h + s_ma * y_a)
        x2 = x_ref[...] + _mm(mg_b, wout_ref[...])
        r2 = lax.rsqrt(jnp.mean(x2 * x2, axis=-1, keepdims=True) + EPS)
        xh2 = x2 * r2
        err = xh2 * fnw_v - t_ref[...]
        loss = 0.5 * jnp.sum(jnp.mean(err * err, axis=-1, keepdims=True))
        dy = err * (1.0 / D)
        dfnw = jnp.sum(dy * xh2, axis=0, keepdims=True)
        dxh2 = dy * fnw_v
        dx2 = r2 * (dxh2 - xh2 * jnp.mean(dxh2 * xh2, axis=-1, keepdims=True))
        dx2_ref[...] = dx2
        dx2_b = _bf(dx2)
        dmg = _mm_nt(dx2_b, wout_ref[...])
        dmg_h = dmg * s_mh
        dmg_a = dmg * s_ma
        dyh_b = _bf(dmg_h)
        dya_b = _bf(dmg_a)
        dagm_ref[:, D:2 * D] = _bf(dmg_h * y_h * (1.0 - s_mh))
        dagm_ref[:, 2 * D:3 * D] = _bf(dmg_a * y_a * (1.0 - s_ma))
        dgh = _mm_nt(dyh_b, wbh_ref[...])
        dga = _mm_nt(dya_b, wba_ref[...])
        doa_ref[...] = dga * sil_ag
        dagm_ref[:, 0:D] = _bf(dga * oa * dsil_ag)
        dhg_ref[...] = _bf(dgh * on * dsil_hg)
        don = dgh * sil_hg
        dhnw = jnp.sum(don * xh, axis=0, keepdims=True)
        dxh = don * hnw_v
        dos = []
        for h in range(HEADS):
            sl = slice(h * HD, (h + 1) * HD)
            dos.append(rs[h] * (dxh[:, sl] - xhs[h] * jnp.mean(dxh[:, sl] * xhs[h], axis=-1, keepdims=True)))
        dor_ref[...] = jnp.concatenate(dos, axis=1)
        gh_ref[...] = gh_b
        ga_ref[...] = ga_b
        mg_ref[...] = mg_b
        dyh_ref[...] = dyh_b
        dya_ref[...] = dya_b
        dx2b_ref[...] = dx2_b
        row = lax.broadcasted_iota(jnp.int32, (8, D), 0)
        sums_ref[...] += jnp.where(row == 0, dfnw, 0.0) + jnp.where(row == 1, dhnw, 0.0) + jnp.where(row == 2, loss, 0.0)

    rowblk = lambda c: pl.BlockSpec((tb, D), lambda i: (i, c))
    half = lambda c: pl.BlockSpec((tb, 512), lambda i: (i, c))
    full = lambda shape: pl.BlockSpec(shape, lambda i: (0, 0))
    return pl.pallas_call(
        body, grid=(t // tb,),
        in_specs=[rowblk(0), rowblk(0), rowblk(C_HG // D), half(C_AG // 512), half(C_AG // 512 + 1), half(C_MH // 512),
                  half(C_MH // 512 + 1), half(C_MA // 512), half(C_MA // 512 + 1), rowblk(0), rowblk(0),
                  full((D, D)), full((D, D)), full((D, D)), full((1, D)), full((1, D))],
        out_specs=[rowblk(0), rowblk(0), rowblk(0), rowblk(0), pl.BlockSpec((tb, 3 * D), lambda i: (i, 0))]
        + [rowblk(0)] * 6 + [full((8, D))],
        out_shape=[jax.ShapeDtypeStruct((t, D), F32)] * 3
        + [jax.ShapeDtypeStruct((t, D), BF16), jax.ShapeDtypeStruct((t, 3 * D), BF16)]
        + [jax.ShapeDtypeStruct((t, D), BF16)] * 6 + [jax.ShapeDtypeStruct((8, D), F32)],
        name="tail", compiler_params=_cp(56),
    )(o_raw, o_a, proj, proj, proj, proj, proj, proj, proj, x2d, tgt, wbh, wba, wout, hnw, fnw)


def _wgrad3(gh, dyh, ga, dya, mg, dx2b, tk, after):
    t = dyh.shape[0]

    def body(a0, b0, a1, b1, a2, b2, _, o0, o1, o2):
        @pl.when(pl.program_id(0) == 0)
        def _():
            o0[...] = jnp.zeros_like(o0)
            o1[...] = jnp.zeros_like(o1)
            o2[...] = jnp.zeros_like(o2)

        o0[...] += _mm_tn(a0[...], b0[...])
        o1[...] += _mm_tn(a1[...], b1[...])
        o2[...] += _mm_tn(a2[...], b2[...])

    blk = pl.BlockSpec((tk, D), lambda k: (k, 0))
    out = pl.BlockSpec((D, D), lambda k: (0, 0))
    return pl.pallas_call(
        body, grid=(t // tk,), in_specs=[blk] * 6 + [pl.BlockSpec(memory_space=pl.ANY)], out_specs=[out] * 3,
        out_shape=[jax.ShapeDtypeStruct((D, D), F32)] * 3,
        name="wgrad3", compiler_params=_cp(48),
    )(gh, dyh, ga, dya, mg, dx2b, after)


def _inproj_wgrad(xnt, pieces, nb, col0, dw, name):
    t = xnt.shape[1]
    counts = [p.shape[1] // nb for p in pieces]
    firsts = [sum(counts[:k]) for k in range(len(pieces))]
    assert col0 % nb == 0

    def body(xnt_ref, *refs):
        o_ref = refs[-1]
        j = pl.program_id(0)
        for first, count, p_ref in zip(firsts, counts, refs):
            @pl.when((j >= first) & (j < first + count))
            def _(p_ref=p_ref):
                o_ref[...] = _mm(xnt_ref[...], p_ref[...]).astype(BF16)

    def piece_spec(first, count):
        return pl.BlockSpec((t, nb), lambda j: (0, jnp.clip(j - first, 0, count - 1)))

    carried = [] if dw is None else [dw]
    return pl.pallas_call(
        body, grid=(sum(counts),),
        in_specs=[pl.BlockSpec((D, t), lambda j: (0, 0), pipeline_mode=pl.Buffered(1))]
        + [piece_spec(f, c) for f, c in zip(firsts, counts)] + [pl.BlockSpec(memory_space=pl.ANY)] * len(carried),
        out_specs=pl.BlockSpec((D, nb), lambda j: (0, j + col0 // nb)),
        out_shape=jax.ShapeDtypeStruct((D, DIN), BF16),
        input_output_aliases={1 + len(pieces): 0} if carried else {},
        name=name, compiler_params=_cp(56),
    )(xnt, *pieces, *carried)


def _inproj_dgrad(pieces, w_p, x2d, dx2, norm_w, tb, after):
    t = x2d.shape[0]

    def body(*refs):
        piece_refs = refs[:len(pieces)]
        w_ref, x_ref, dx2_ref, nw_ref, _, gx_ref, dnw_ref = refs[len(pieces):]

        @pl.when(pl.program_id(0) == 0)
        def _():
            dnw_ref[...] = jnp.zeros_like(dnw_ref)

        dxn = None
        off = 0
        for p in piece_refs:
            width = p.shape[1]
            for q in range(w_ref.shape[0]):
                lo, hi = max(off, q * PAIR), min(off + width, (q + 1) * PAIR)
                if lo < hi:
                    term = _mm_nt(p[:, lo - off:hi - off], w_ref[q, :, lo - q * PAIR:hi - q * PAIR])
                    dxn = term if dxn is None else dxn + term
            off += width
        xv = x_ref[...]
        r = lax.rsqrt(jnp.mean(xv * xv, axis=-1, keepdims=True) + EPS)
        xh = xv * r
        dxh = dxn * nw_ref[...]
        gx_ref[...] = dx2_ref[...] + r * (dxh - xh * jnp.mean(dxh * xh, axis=-1, keepdims=True))
        row0 = lax.broadcasted_iota(jnp.int32, (8, D), 0) == 0
        dnw_ref[...] += jnp.where(row0, jnp.sum(dxn * xh, axis=0, keepdims=True), 0.0)

    rowblk = pl.BlockSpec((tb, D), lambda i: (i, 0))
    return pl.pallas_call(
        body, grid=(t // tb,),
        in_specs=[pl.BlockSpec((tb, p.shape[1]), lambda i: (i, 0)) for p in pieces]
        + [pl.BlockSpec(w_p.shape, lambda i: (0, 0, 0), pipeline_mode=pl.Buffered(1)), rowblk, rowblk,
           pl.BlockSpec((1, D), lambda i: (0, 0)), pl.BlockSpec(memory_space=pl.ANY)],
        out_specs=[rowblk, pl.BlockSpec((8, D), lambda i: (0, 0))],
        out_shape=[jax.ShapeDtypeStruct((t, D), F32), jax.ShapeDtypeStruct((8, D), F32)],
        name="inproj_dgrad", compiler_params=_cp(60),
    )(*pieces, w_p, x2d, dx2, norm_w, after)


def _adamw_math(w, g, m, v):
    m = B1 * m + (1.0 - B1) * g
    v = B2 * v + (1.0 - B2) * (g * g)
    m_hat = m / (1.0 - B1 ** STEP)
    v_hat = v / (1.0 - B2 ** STEP)
    delta = -LR * (m_hat / (jnp.sqrt(v_hat) + ADAM_EPS) + WD * w)
    return delta, m, v


def _adamw_shard(recv, sums, chip, w, m, v, rows, name):
    nparts, nr, nc = recv.shape

    def body(chip_ref, own_ref, p_ref, w_ref, m_ref, v_ref, g_ref, d_ref, nm_ref, nv_ref):
        g = own_ref[0].astype(F32)
        for s in range(nparts):
            g = g + p_ref[s].astype(F32)
        d, nm, nv = _adamw_math(w_ref[...], g, m_ref[...], v_ref[...])
        g_ref[...] = g
        d_ref[...] = d
        nm_ref[...] = nm
        nv_ref[...] = nv

    blk = pl.BlockSpec((rows, nc), lambda i, chip_ref: (i, 0))
    return pl.pallas_call(
        body,
        grid_spec=pltpu.PrefetchScalarGridSpec(
            num_scalar_prefetch=1, grid=(nr // rows,),
            in_specs=[pl.BlockSpec((1, rows, nc), lambda i, chip_ref: (chip_ref[0], i, 0)),
                      pl.BlockSpec((nparts, rows, nc), lambda i, chip_ref: (0, i, 0)), blk, blk, blk],
            out_specs=[blk] * 4),
        out_shape=[jax.ShapeDtypeStruct((nr, nc), F32)] * 4,
        name=name, compiler_params=_cp(48),
    )(chip, sums, recv, w, m, v)


def _adamw_sum8(parts, w, m, v, after, name):
    def body(p_ref, w_ref, m_ref, v_ref, _, g_ref, d_ref, nm_ref, nv_ref):
        g = p_ref[0].astype(F32)
        for s in range(1, NDEV):
            g = g + p_ref[s].astype(F32)
        d, nm, nv = _adamw_math(w_ref[...], g, m_ref[...], v_ref[...])
        g_ref[...] = g
        d_ref[...] = d
        nm_ref[...] = nm
        nv_ref[...] = nv

    vm = pl.BlockSpec(memory_space=pltpu.VMEM)
    return pl.pallas_call(
        body, out_shape=[jax.ShapeDtypeStruct(w.shape, F32)] * 4,
        in_specs=[vm, vm, vm, vm, pl.BlockSpec(memory_space=pl.ANY)], out_specs=[vm] * 4, name=name,
    )(parts, w, m, v, after)


SMALL_ROWS = dict(norm_w=0, lower_bound=1, hgrn_norm_w=3, final_norm_w=4, sinks=5, loss=6)


def _pack_small_grads(dnw, dlb, sums, dsink):
    def body(dnw_ref, dlb_ref, sums_ref, dsink_ref, o_ref):
        o_ref[...] = jnp.zeros_like(o_ref)
        o_ref[0:1, :] = dnw_ref[0:1, :]
        o_ref[1:2, :] = dlb_ref[0:1, :]
        o_ref[3:4, :] = sums_ref[1:2, :]
        o_ref[4:5, :] = sums_ref[0:1, :]
        o_ref[5:6, 0:128] = dsink_ref[0:1, :]
        o_ref[6:7, :] = sums_ref[2:3, :]

    return pl.pallas_call(body, out_shape=jax.ShapeDtypeStruct((8, D), F32), name="pack_small_grads")(dnw, dlb, sums, dsink)


def _adamw_small(parts, ws, ms, vs):
    shapes = [a.shape for a in ws]

    def body(p_ref, *refs):
        w, m, v = refs[0:5], refs[5:10], refs[10:15]
        outs = [refs[15 + 5 * i:20 + 5 * i] for i in range(4)]
        loss_ref = refs[35]

        def total(row, width):
            g = p_ref[0, row:row + 1, 0:width]
            for s in range(1, NDEV):
                g = g + p_ref[s, row:row + 1, 0:width]
            return g

        loss_ref[...] = total(6, 128)
        lb = _lower_bound(w[1])
        ga0 = total(1, D) * lb * (1.0 - lb)
        grads = [total(0, D), None, total(3, D), total(4, D), total(5, QH)]
        for i in (0, 2, 3, 4):
            res = (grads[i],) + _adamw_math(w[i][...], grads[i], m[i][...], v[i][...])
            for o, val in zip(outs, res):
                o[i][...] = val
        for r, g in ((0, ga0), (1, -ga0)):
            res = (g,) + _adamw_math(w[1][r:r + 1, :], g, m[1][r:r + 1, :], v[1][r:r + 1, :])
            for o, val in zip(outs, res):
                o[1][r:r + 1, :] = val

    res = pl.pallas_call(
        body, out_shape=[jax.ShapeDtypeStruct(s, F32) for s in shapes] * 4 + [jax.ShapeDtypeStruct((1, 128), F32)],
        name="adamw_small",
    )(parts, *ws, *ms, *vs)
    return [res[5 * i:5 * i + 5] for i in range(4)], res[20][0, 0]


def kernel(x, norm_w, w_in, hgrn_lower_bound, hgrn_norm_w, w_branch_hgrn, attn_sinks, w_branch_attn, w_out, final_norm_w, loss_target, m_norm_w, m_w_in, m_hgrn_lower_bound, m_hgrn_norm_w, m_w_branch_hgrn, m_attn_sinks, m_w_branch_attn, m_w_out, m_final_norm_w, v_norm_w, v_w_in, v_hgrn_lower_bound, v_hgrn_norm_w, v_w_branch_hgrn, v_attn_sinks, v_w_branch_attn, v_w_out, v_final_norm_w):
    t = x.shape[1]
    x2d = x.reshape(t, D)
    tgt = loss_target.reshape(t, D)
    fnw = final_norm_w.reshape(1, D)
    row_blk = min(256, t)
    big_blk = min(512, t)

    chip = (2 * lax.axis_index("x") + lax.axis_index("y")).astype(jnp.int32).reshape(1)
    w_p, xn, xnt, proj_own = _gather_in_projection(w_in[0], x2d, norm_w)
    wbh, wba, wout = (g.reshape(D, D) for g in _gather_square(
        [w_branch_hgrn[0].astype(BF16), w_branch_attn[0].astype(BF16), w_out[0].astype(BF16)], after=w_p))

    proj = _inproj_fwd(xn, w_p, proj_own, chip, min(1024, t))
    o_raw, states = _hgrn_fwd(proj, hgrn_lower_bound, big_blk, HGRN_GROUP)
    o_a = _attn_fwd(proj, attn_sinks)
    (dx2, do_raw, do_a, d_hg, d_agm, gh, ga, mg, dyh, dya, dx2b, sums) = _tail(
        o_raw, o_a, proj, x2d, tgt, wbh, wba, wout, hgrn_norm_w, fnw, row_blk)
    d_aq, d_kv, dsink = _attn_bwd(proj, attn_sinks, do_a)
    d_hgrn, dlb = _hgrn_bwd(proj, hgrn_lower_bound, do_raw, states, big_blk, HGRN_GROUP)
    pieces = (d_hgrn, d_hg, d_aq, d_kv, d_agm)
    dw_cat = _inproj_wgrad(xnt, [d_hgrn], CB, 0, None, "inproj_wgrad_hgrn")
    dw_cat = _inproj_wgrad(xnt, [d_hg, d_aq, d_kv], CB // 2, C_HG, dw_cat, "inproj_wgrad_mid")
    dw_cat = _inproj_wgrad(xnt, [d_agm], CB, C_AG, dw_cat, "inproj_wgrad_gates")

    dwin_r = dw_cat.reshape(D, NDEV, IN_SHARD).transpose(1, 0, 2)
    dwbh, dwba, dwout = _wgrad3(gh, dyh, ga, dya, mg, dx2b, big_blk, after=dw_cat)
    slots = lambda a: a.reshape(NDEV, ROW_SHARD, D).astype(BF16)
    got = _exchange_pair([dwin_r], after=dwbh)
    core = lax.axis_index("c").astype(jnp.int32).reshape(1)
    s_in = _pair_sum(dwin_r, got[0], core, 4 * ROW_SHARD, "pair_sum_w_in")
    rin, = _exchange_chips([s_in])
    rbh, rba, rout = _exchange_square([slots(dwbh), slots(dwba), slots(dwout)])
    grad_x, dnw = _inproj_dgrad(pieces, w_p, x2d, dx2, norm_w, big_blk, after=s_in)
    rsm = _exchange_small(_pack_small_grads(dnw, dlb, sums, dsink))
    g_in, d_in, nm_in, nv_in = _adamw_shard(rin, s_in, chip, w_in[0], m_w_in[0], v_w_in[0], 128, "adamw_w_in")
    g_bh, d_bh, nm_bh, nv_bh = _adamw_sum8(
        rbh, w_branch_hgrn[0], m_w_branch_hgrn[0], v_w_branch_hgrn[0], dnw, "adamw_w_bh")
    g_ba, d_ba, nm_ba, nv_ba = _adamw_sum8(
        rba, w_branch_attn[0], m_w_branch_attn[0], v_w_branch_attn[0], dnw, "adamw_w_ba")
    g_out, d_out, nm_out, nv_out = _adamw_sum8(rout, w_out[0], m_w_out[0], v_w_out[0], dnw, "adamw_w_out")
    (sg, sd, sm, sv), loss = _adamw_small(
        rsm,
        (norm_w, hgrn_lower_bound, hgrn_norm_w, fnw, attn_sinks),
        (m_norm_w, m_hgrn_lower_bound, m_hgrn_norm_w, m_final_norm_w.reshape(1, D), m_attn_sinks),
        (v_norm_w, v_hgrn_lower_bound, v_hgrn_norm_w, v_final_norm_w.reshape(1, D), v_attn_sinks))

    def group(s, w_in_v, bh, ba, out):
        nw, lb, hnw, fn, sinks = s
        return (nw, w_in_v[None], lb, hnw, bh[None], sinks, ba[None], out[None], fn.reshape(D))

    return (loss, grad_x.reshape(1, t, D),
            *group(sg, g_in, g_bh, g_ba, g_out), *group(sd, d_in, d_bh, d_ba, d_out),
            *group(sm, nm_in, nm_bh, nm_ba, nm_out), *group(sv, nv_in, nv_bh, nv_ba, nv_out))
```

```python
import jax
import jax.numpy as jnp
from jax import lax
from jax.experimental import pallas as pl
from jax.experimental.pallas import tpu as pltpu
from jax.experimental.pallas import tpu_sc as plsc

F32 = jnp.float32
BF16 = jnp.bfloat16

D = 1024
DIN = 8704
NDEV = 8
IN_SHARD = DIN // NDEV
PAIR = 2 * IN_SHARD
ROW_SHARD = D // NDEV
HEADS = 8
HD = 128
CH = 64
HGRN_GROUP = 8
QH = 16
AB = 128
EPS = 1e-6
NEG = -1e30
ATT_SCALE = 0.125

C_HG = 3072
C_AQ = 4096
C_AK = 5120
C_AV = 5376
C_AG = 5632
C_MH = 6656
C_MA = 7680
CB = 512

LR = 0.001
B1 = 0.9
B2 = 0.999
ADAM_EPS = 1e-08
WD = 0.01
STEP = 10

MESH = pl.DeviceIdType.MESH


def _cp(vmem_mb):
    return pltpu.CompilerParams(vmem_limit_bytes=vmem_mb * 1024 * 1024)


def _mm(a, b):
    return jnp.dot(a, b, preferred_element_type=F32)


def _mm_nt(a, b):
    return lax.dot_general(a, b, (((1,), (1,)), ((), ())), preferred_element_type=F32)


def _mm_tn(a, b):
    return lax.dot_general(a, b, (((0,), (0,)), ((), ())), preferred_element_type=F32)


def _tri3(lower):
    r = lax.broadcasted_iota(jnp.int32, (CH, 3 * CH), 0)
    c = lax.broadcasted_iota(jnp.int32, (CH, 3 * CH), 1)
    c = jnp.where(c >= 2 * CH, c - 2 * CH, jnp.where(c >= CH, c - CH, c))
    return ((r >= c) if lower else (c >= r)).astype(BF16)


def _mm_tri_exact(tri3, g):
    g1 = g.astype(BF16)
    r1 = g - g1.astype(F32)
    g2 = r1.astype(BF16)
    g3 = (r1 - g2.astype(F32)).astype(BF16)
    return _mm(tri3, jnp.concatenate([g1, g2, g3], axis=0))


def _sigmoid(v):
    return 0.5 * jnp.tanh(0.5 * v) + 0.5


def _bf(v):
    return v.astype(BF16)


def _place():
    x, y, c = lax.axis_index("x"), lax.axis_index("y"), lax.axis_index("c")
    return (x, y, c), (x, y, 1 - c), [(1 - x, y), (x, 1 - y), (1 - x, 1 - y)]


def _dev_index(px, py, pc):
    return 4 * px + 2 * py + pc


def _gather_in_projection(w_in_s, x2d, norm_w):
    half = D // 2
    t = x2d.shape[0]
    prep_rows = min(512, t)
    nprep = t // prep_rows

    def body(win_ref, x_hbm, nw_ref, wp_g, xn_hbm, xnt_hbm, proj_hbm, give, take, mine, xbuf, xnbuf, xntbuf, w_own, pbuf,
             send_sems, recv_sems, loc_sem, swap_sems, in_sems, out_sems, own_sem):
        (x, y, c), sibling, chips = _place()
        give[...] = win_ref[pl.ds(pl.multiple_of(half * (1 - c), half), half), :].astype(BF16)
        swap = pltpu.make_async_remote_copy(src_ref=give, dst_ref=take, send_sem=swap_sems.at[0], recv_sem=swap_sems.at[1],
                                            device_id=sibling, device_id_type=MESH)
        swap.start()
        swap.wait()
        own = win_ref[pl.ds(pl.multiple_of(half * c, half), half), :]
        other = take[...].astype(F32)
        mine[...] = jnp.where(c == 0, jnp.concatenate([own, other], axis=1),
                              jnp.concatenate([other, own], axis=1)).astype(BF16)

        def place(px, py, pc):
            return wp_g.at[2 * px + py, pl.ds(pl.multiple_of(half * pc, half), half), :]

        def copy(kind, origin, to, src=mine):
            return pltpu.make_async_remote_copy(
                src_ref=src, dst_ref=place(*origin), send_sem=send_sems.at[kind], recv_sem=recv_sems.at[kind],
                device_id=to, device_id_type=MESH)

        me = (x, y, c)
        local = pltpu.make_async_copy(mine, place(*me), loc_sem)
        local.start()
        first = [copy(0, me, sibling)] + [copy(1 + j, me, (*chip, c)) for j, chip in enumerate(chips)]
        for cp in first:
            cp.start()

        copy(0, (x, y, 1 - c), me).wait_recv()
        local.wait()
        my_chip = 2 * x + y
        fetch = pltpu.make_async_copy(wp_g.at[my_chip], w_own, own_sem)
        fetch.start()

        def rows_of(i):
            return pl.ds(pl.multiple_of(i * prep_rows, prep_rows), prep_rows)

        def load(i, slot):
            return pltpu.make_async_copy(x_hbm.at[rows_of(i), :], xbuf.at[slot], in_sems.at[slot])

        def stores(i, slot):
            own_cols = pl.ds(pl.multiple_of(my_chip * PAIR, 128), PAIR)
            return (pltpu.make_async_copy(xnbuf.at[slot], xn_hbm.at[rows_of(i), :], out_sems.at[slot, 0]),
                    pltpu.make_async_copy(xntbuf.at[slot], xnt_hbm.at[:, rows_of(i)], out_sems.at[slot, 1]),
                    pltpu.make_async_copy(pbuf.at[slot], proj_hbm.at[rows_of(i), own_cols], out_sems.at[slot, 2]))

        load(0, 0).start()
        fetch.wait()

        def prep(i, carry):
            slot = lax.rem(i, 2)
            load(i, slot).wait()

            @pl.when(i + 1 < nprep)
            def _():
                load(i + 1, 1 - slot).start()

            @pl.when(i >= 2)
            def _():
                for cp in stores(i - 2, slot):
                    cp.wait()

            xv = xbuf[slot]
            xn = (xv * lax.rsqrt(jnp.mean(xv * xv, axis=-1, keepdims=True) + EPS)) * nw_ref[...]
            xn_b = xn.astype(BF16)
            xnbuf[slot] = xn_b
            xntbuf[slot] = xn.T.astype(BF16)
            pbuf[slot] = _mm(xn_b, w_own[...])
            for cp in stores(i, slot):
                cp.start()
            return carry

        lax.fori_loop(0, nprep, prep, 0)
        for i in range(max(nprep - 2, 0), nprep):
            for cp in stores(i, i % 2):
                cp.wait()

        passed = []
        for j, chip in enumerate(chips):
            copy(1 + j, (*chip, c), me).wait_recv()
            cp = copy(4 + j, (*chip, c), sibling, src=place(*chip, c))
            cp.start()
            passed.append(cp)
        for j, chip in enumerate(chips):
            copy(4 + j, (*chip, 1 - c), me).wait_recv()
        for cp in first + passed:
            cp.wait_send()

    vm = pl.BlockSpec(memory_space=pltpu.VMEM)
    hbm = pl.BlockSpec(memory_space=pl.ANY)
    return pl.pallas_call(
        body,
        out_shape=[jax.ShapeDtypeStruct((NDEV // 2, D, PAIR), BF16), jax.ShapeDtypeStruct((t, D), BF16),
                   jax.ShapeDtypeStruct((D, t), BF16), jax.ShapeDtypeStruct((t, DIN), F32)],
        in_specs=[vm, hbm, vm],
        out_specs=[hbm, hbm, hbm, hbm],
        scratch_shapes=[pltpu.VMEM((half, IN_SHARD), BF16), pltpu.VMEM((half, IN_SHARD), BF16),
                        pltpu.VMEM((half, PAIR), BF16),
                        pltpu.VMEM((2, prep_rows, D), F32), pltpu.VMEM((2, prep_rows, D), BF16),
                        pltpu.VMEM((2, D, prep_rows), BF16),
                        pltpu.VMEM((D, PAIR), BF16), pltpu.VMEM((2, prep_rows, PAIR), F32),
                        pltpu.SemaphoreType.DMA((NDEV - 1,)), pltpu.SemaphoreType.DMA((NDEV - 1,)),
                        pltpu.SemaphoreType.DMA, pltpu.SemaphoreType.DMA((2,)),
                        pltpu.SemaphoreType.DMA((2,)), pltpu.SemaphoreType.DMA((2, 3)), pltpu.SemaphoreType.DMA],
        name="gather_in_projection", compiler_params=_cp(56),
    )(w_in_s, x2d, norm_w)


def _gather_square(shards, after):
    n = len(shards)

    def launch(*refs):
        ins, outs = refs[:n], refs[n + 1:2 * n + 1]
        send_sems, recv_sems, loc_sems = refs[2 * n + 1:]
        (x, y, c), _, _ = _place()
        me = _dev_index(x, y, c)
        peers = [(1 - x if r & 4 else x, 1 - y if r & 2 else y, 1 - c if r & 1 else c) for r in range(1, NDEV)]
        barrier = pltpu.get_barrier_semaphore()
        for peer in peers:
            pl.semaphore_signal(barrier, inc=1, device_id=peer, device_id_type=MESH)
        pl.semaphore_wait(barrier, NDEV - 1)
        local = [pltpu.make_async_copy(ins[k], outs[k].at[me], loc_sems.at[k]) for k in range(n)]
        copies = [pltpu.make_async_remote_copy(
            src_ref=ins[k], dst_ref=outs[k].at[me], send_sem=send_sems.at[r, k], recv_sem=recv_sems.at[r, k],
            device_id=peer, device_id_type=MESH) for r, peer in enumerate(peers) for k in range(n)]
        for cp in local + copies:
            cp.start()
        for r, peer in enumerate(peers):
            for k in range(n):
                pltpu.make_async_remote_copy(
                    src_ref=ins[k], dst_ref=outs[k].at[_dev_index(*peer)], send_sem=send_sems.at[r, k],
                    recv_sem=recv_sems.at[r, k], device_id=peer, device_id_type=MESH).wait_recv()
        for cp in copies:
            cp.wait_send()
        for cp in local:
            cp.wait()

    return pl.kernel(
        launch, out_type=[jax.ShapeDtypeStruct((NDEV,) + a.shape, a.dtype) for a in shards],
        mesh=plsc.ScalarSubcoreMesh(axis_name="sequencer", num_cores=1), name="gather_square",
        scratch_types=(pltpu.SemaphoreType.DMA((NDEV - 1, n)), pltpu.SemaphoreType.DMA((NDEV - 1, n)),
                       pltpu.SemaphoreType.DMA((n,))),
        compiler_params=pltpu.CompilerParams(collective_id=2),
    )(*shards, after)


def _exchange_pair(arrs, after):
    n = len(arrs)

    def launch(*refs):
        ins, got = refs[:n], refs[n + 1:2 * n + 1]
        send_sems, recv_sems = refs[2 * n + 1:]
        (x, y, c), sibling, _ = _place()
        barrier = pltpu.get_barrier_semaphore()
        pl.semaphore_signal(barrier, inc=1, device_id=sibling, device_id_type=MESH)
        pl.semaphore_wait(barrier, 1)
        sends = [pltpu.make_async_remote_copy(
            src_ref=ins[k].at[_dev_index(q // 2, q % 2, 1 - c)], dst_ref=got[k].at[q], send_sem=send_sems.at[q, k],
            recv_sem=recv_sems.at[q, k], device_id=sibling, device_id_type=MESH) for q in range(4) for k in range(n)]
        for cp in sends:
            cp.start()
        for cp in sends:
            cp.wait_recv()
        for cp in sends:
            cp.wait_send()

    return pl.kernel(
        launch, out_type=[jax.ShapeDtypeStruct((4,) + a.shape[1:], a.dtype) for a in arrs],
        mesh=plsc.ScalarSubcoreMesh(axis_name="sequencer", num_cores=1), name="exchange_pair",
        scratch_types=(pltpu.SemaphoreType.DMA((4, n)), pltpu.SemaphoreType.DMA((4, n))),
        compiler_params=pltpu.CompilerParams(collective_id=0),
    )(*arrs, after)


def _pair_sum(full, got, core, rows, name):
    _, nr, nc = got.shape

    def body(core_ref, a_ref, b_ref, o_ref):
        o_ref[...] = (a_ref[...].astype(F32) + b_ref[...].astype(F32)).astype(BF16)

    blk = pl.BlockSpec((1, rows, nc), lambda q, i, core_ref: (q, i, 0))
    return pl.pallas_call(
        body,
        grid_spec=pltpu.PrefetchScalarGridSpec(
            num_scalar_prefetch=1, grid=(4, nr // rows),
            in_specs=[pl.BlockSpec((1, rows, nc), lambda q, i, core_ref: (2 * q + core_ref[0], i, 0)), blk],
            out_specs=blk),
        out_shape=jax.ShapeDtypeStruct(got.shape, BF16), name=name,
    )(core, full, got)


def _exchange_chips(sums):
    n = len(sums)

    def launch(*refs):
        ins, outs = refs[:n], refs[n:2 * n]
        send_sems, recv_sems = refs[2 * n:]
        (x, y, c), _, chips = _place()
        barrier = pltpu.get_barrier_semaphore()
        for px, py in chips:
            pl.semaphore_signal(barrier, inc=1, device_id=(px, py, c), device_id_type=MESH)
        pl.semaphore_wait(barrier, len(chips))
        copies = [pltpu.make_async_remote_copy(
            src_ref=ins[k].at[2 * px + py], dst_ref=outs[k].at[j], send_sem=send_sems.at[j, k],
            recv_sem=recv_sems.at[j, k], device_id=(px, py, c), device_id_type=MESH)
            for j, (px, py) in enumerate(chips) for k in range(n)]
        for cp in copies:
            cp.start()
        for cp in copies:
            cp.wait_recv()
        for cp in copies:
            cp.wait_send()

    return pl.kernel(
        launch, out_type=[jax.ShapeDtypeStruct((3,) + a.shape[1:], a.dtype) for a in sums],
        mesh=plsc.ScalarSubcoreMesh(axis_name="sequencer", num_cores=1), name="exchange_chips",
        scratch_types=(pltpu.SemaphoreType.DMA((3, n)), pltpu.SemaphoreType.DMA((3, n))),
        compiler_params=pltpu.CompilerParams(collective_id=1),
    )(*sums)


def _exchange_square(partials):
    n = len(partials)

    def launch(*refs):
        ins, outs = refs[:n], refs[n:2 * n]
        send_sems, recv_sems, loc_sems = refs[2 * n:]
        (x, y, c), _, _ = _place()
        me = _dev_index(x, y, c)
        peers = [(1 - x if r & 4 else x, 1 - y if r & 2 else y, 1 - c if r & 1 else c) for r in range(1, NDEV)]
        barrier = pltpu.get_barrier_semaphore()
        for peer in peers:
            pl.semaphore_signal(barrier, inc=1, device_id=peer, device_id_type=MESH)
        pl.semaphore_wait(barrier, NDEV - 1)
        local = [pltpu.make_async_copy(ins[k].at[me], outs[k].at[me], loc_sems.at[k]) for k in range(n)]
        copies = [pltpu.make_async_remote_copy(
            src_ref=ins[k].at[_dev_index(*peer)], dst_ref=outs[k].at[me], send_sem=send_sems.at[r, k],
            recv_sem=recv_sems.at[r, k], device_id=peer, device_id_type=MESH)
            for r, peer in enumerate(peers) for k in range(n)]
        for cp in local + copies:
            cp.start()
        for r, peer in enumerate(peers):
            for k in range(n):
                pltpu.make_async_remote_copy(
                    src_ref=ins[k].at[me], dst_ref=outs[k].at[_dev_index(*peer)], send_sem=send_sems.at[r, k],
                    recv_sem=recv_sems.at[r, k], device_id=peer, device_id_type=MESH).wait_recv()
        for cp in copies:
            cp.wait_send()
        for cp in local:
            cp.wait()

    return pl.kernel(
        launch, out_type=[jax.ShapeDtypeStruct(a.shape, a.dtype) for a in partials],
        mesh=plsc.ScalarSubcoreMesh(axis_name="sequencer", num_cores=1), name="exchange_square",
        scratch_types=(pltpu.SemaphoreType.DMA((NDEV - 1, n)), pltpu.SemaphoreType.DMA((NDEV - 1, n)),
                       pltpu.SemaphoreType.DMA((n,))),
        compiler_params=pltpu.CompilerParams(collective_id=3),
    )(*partials)


def _exchange_small(small):
    def launch(sm_ref, out_ref, send_sems, recv_sems, loc_sem):
        (x, y, c), _, _ = _place()
        me = _dev_index(x, y, c)
        peers = [(1 - x if r & 4 else x, 1 - y if r & 2 else y, 1 - c if r & 1 else c) for r in range(1, NDEV)]
        barrier = pltpu.get_barrier_semaphore()
        for peer in peers:
            pl.semaphore_signal(barrier, inc=1, device_id=peer, device_id_type=MESH)
        pl.semaphore_wait(barrier, NDEV - 1)
        local = pltpu.make_async_copy(sm_ref, out_ref.at[me], loc_sem)
        local.start()
        copies = [pltpu.make_async_remote_copy(
            src_ref=sm_ref, dst_ref=out_ref.at[me], send_sem=send_sems.at[r], recv_sem=recv_sems.at[r],
            device_id=peer, device_id_type=MESH) for r, peer in enumerate(peers)]
        for cp in copies:
            cp.start()
        for r, peer in enumerate(peers):
            pltpu.make_async_remote_copy(
                src_ref=sm_ref, dst_ref=out_ref.at[_dev_index(*peer)], send_sem=send_sems.at[r], recv_sem=recv_sems.at[r],
                device_id=peer, device_id_type=MESH).wait_recv()
        for cp in copies:
            cp.wait_send()
        local.wait()

    return pl.kernel(
        launch, out_type=jax.ShapeDtypeStruct((NDEV,) + small.shape, F32),
        mesh=plsc.ScalarSubcoreMesh(axis_name="sequencer", num_cores=1), name="exchange_small",
        scratch_types=(pltpu.SemaphoreType.DMA((NDEV - 1,)), pltpu.SemaphoreType.DMA((NDEV - 1,)), pltpu.SemaphoreType.DMA),
        compiler_params=pltpu.CompilerParams(collective_id=4),
    )(small)


def _inproj_fwd(xn, w_pairs, proj, chip, tb):
    t = xn.shape[0]
    nblk, _, nb = w_pairs.shape

    def body(chip_ref, xn_ref, w_ref, proj_in, proj_ref):
        proj_ref[...] = _mm(xn_ref[...], w_ref[0])

    def other(j, chip_ref):
        return j + (j >= chip_ref[0]).astype(jnp.int32)

    return pl.pallas_call(
        body,
        grid_spec=pltpu.PrefetchScalarGridSpec(
            num_scalar_prefetch=1, grid=(t // tb, nblk - 1),
            in_specs=[pl.BlockSpec((tb, D), lambda i, j, chip_ref: (i, 0)),
                      pl.BlockSpec((1, D, nb), lambda i, j, chip_ref: (other(j, chip_ref), 0, 0)),
                      pl.BlockSpec(memory_space=pl.ANY)],
            out_specs=pl.BlockSpec((tb, nb), lambda i, j, chip_ref: (i, other(j, chip_ref)))),
        out_shape=jax.ShapeDtypeStruct((t, DIN), F32),
        input_output_aliases={3: 0},
        name="inproj_fwd", compiler_params=_cp(56),
    )(chip, xn, w_pairs, proj)


def _lower_bound(lb_ref):
    a0 = lb_ref[0:1, :]
    a1 = lb_ref[1:2, :]
    mx = jnp.maximum(a0, a1)
    e0 = jnp.exp(a0 - mx)
    e1 = jnp.exp(a1 - mx)
    return e0 / (e0 + e1)


def _hgrn_chunk_fwd(hq, hf, lb, tril):
    sg = _sigmoid(hf)
    f = lb + (1.0 - lb) * sg
    g = jnp.log(f)
    k = 1.0 - f
    sq = _sigmoid(hq)
    q = hq * sq
    b = _mm_tri_exact(tril, g)
    last_row = lax.broadcasted_iota(jnp.int32, b.shape, 0) == CH - 1
    b_last = jnp.sum(jnp.where(last_row, b, 0.0), axis=0, keepdims=True)
    c = 0.5 * b_last
    eb = jnp.exp(b)
    ea = jnp.exp(b - c)
    ek = jnp.exp(c - b)
    ed = jnp.exp(b_last - b)
    ebl = jnp.exp(b_last)
    return dict(sg=sg, f=f, k=k, sq=sq, q=q, eb=eb, ea=ea, ek=ek, ed=ed, ebl=ebl,
                qe=q * eb, qa=q * ea, ka=k * ek, kd=k * ed)


def _tri(lower):
    r = lax.broadcasted_iota(jnp.int32, (CH, CH), 0)
    c = lax.broadcasted_iota(jnp.int32, (CH, CH), 1)
    return (r >= c) if lower else (c >= r)


def _head_segment(p_ref, rows, j, hg):
    return p_ref[rows, j * HD * hg:(j + 1) * HD * hg]


def _head(a, k):
    return a[:, k * HD:(k + 1) * HD]


def _hgrn_fwd(proj, lbw, rb, hg):
    assert hg == HEADS
    t = proj.shape[0]
    ncb = rb // CH

    def body(p_ref, lb_ref, o_ref, st_ref, s_scr):
        @pl.when(pl.program_id(1) == 0)
        def _():
            s_scr[...] = jnp.zeros_like(s_scr)

        lb = _lower_bound(lb_ref)
        causal = _tri(True)
        tril = _tri3(True)
        heads = range(hg)

        def chunk(cc, carry):
            r0 = pl.multiple_of(cc * CH, CH)
            rows = pl.ds(r0, CH)
            e = _hgrn_chunk_fwd(_head_segment(p_ref, rows, 0, hg), _head_segment(p_ref, rows, 1, hg), lb, tril)
            v = _bf(_head_segment(p_ref, rows, 2, hg))
            sts = [s_scr[k] for k in heads]
            qa, ka, qe, kd = _bf(e["qa"]), _bf(e["ka"]), _bf(e["qe"]), _bf(e["kd"])
            a = [_bf(jnp.where(causal, _mm_nt(_head(qa, k), _head(ka, k)), 0.0)) for k in heads]
            o_inter = [_mm_nt(_head(qe, k), _bf(sts[k])) for k in heads]
            kv = [_mm_tn(_head(v, k), _head(kd, k)) for k in heads]
            o_intra = [_mm(a[k], _head(v, k)) for k in heads]
            for k in heads:
                st_ref[cc, k] = sts[k]
                o_ref[rows, k * HD:(k + 1) * HD] = o_inter[k] + o_intra[k]
                s_scr[k] = sts[k] * _head(e["ebl"], k) + kv[k]
            return carry

        lax.fori_loop(0, ncb, chunk, 0, unroll=4)

    return pl.pallas_call(
        body, grid=(HEADS // hg, t // rb),
        in_specs=[pl.BlockSpec((rb, 3 * HD * hg), lambda h, i: (i, h)), pl.BlockSpec((2, HD * hg), lambda h, i: (0, h))],
        out_specs=[pl.BlockSpec((rb, HD * hg), lambda h, i: (i, h)),
                   pl.BlockSpec((ncb, hg, HD, HD), lambda h, i: (i, h, 0, 0))],
        out_shape=[jax.ShapeDtypeStruct((t, D), F32), jax.ShapeDtypeStruct((t // CH, HEADS, HD, HD), F32)],
        scratch_shapes=[pltpu.VMEM((hg, HD, HD), F32)],
        name="hgrn_fwd", compiler_params=_cp(48),
    )(proj, lbw)


def _hgrn_bwd(proj, lbw, do_raw, states, rb, hg):
    assert hg == HEADS
    t = proj.shape[0]
    nblk = t // rb
    ncb = rb // CH
    wd = HD * hg

    def body(p_ref, lb_ref, do_ref, st_ref, dp_ref, dlb_ref, ds_scr):
        @pl.when(pl.program_id(1) == 0)
        def _():
            ds_scr[...] = jnp.zeros_like(ds_scr)
            dlb_ref[...] = jnp.zeros_like(dlb_ref)

        lb = _lower_bound(lb_ref)
        causal = _tri(True)
        tril = _tri3(True)
        triu = _tri3(False)
        last_row = lax.broadcasted_iota(jnp.int32, (CH, HD * hg), 0) == CH - 1
        row0 = lax.broadcasted_iota(jnp.int32, (8, HD * hg), 0) == 0
        heads = range(hg)
        wide = lambda parts: jnp.concatenate(parts, axis=1)

        def chunk(it, carry):
            cc = ncb - 1 - it
            r0 = pl.multiple_of(cc * CH, CH)
            rows = pl.ds(r0, CH)
            hq = _head_segment(p_ref, rows, 0, hg)
            e = _hgrn_chunk_fwd(hq, _head_segment(p_ref, rows, 1, hg), lb, tril)
            v = _bf(_head_segment(p_ref, rows, 2, hg))
            do = _bf(do_ref[rows, :])
            sts = [st_ref[cc, k] for k in heads]
            dsts = [ds_scr[k] for k in heads]
            dlb_acc = dlb_ref[...]
            qa, ka, qe, kd = _bf(e["qa"]), _bf(e["ka"]), _bf(e["qe"]), _bf(e["kd"])
            a = [_bf(jnp.where(causal, _mm_nt(_head(qa, k), _head(ka, k)), 0.0)) for k in heads]
            da = [_bf(jnp.where(causal, _mm_nt(_head(do, k), _head(v, k)), 0.0)) for k in heads]
            dqe = wide([_mm(_head(do, k), _bf(sts[k])) for k in heads])
            dkd = wide([_mm(_head(v, k), _bf(dsts[k])) for k in heads])
            dv_state = [_mm_nt(_head(kd, k), _bf(dsts[k])) for k in heads]
            ds_new = [_mm_tn(_head(do, k), _head(qe, k)) for k in heads]
            dv_intra = [_mm_tn(a[k], _head(do, k)) for k in heads]
            dqa = wide([_mm(da[k], _head(ka, k)) for k in heads])
            dka = wide([_mm_tn(da[k], _head(qa, k)) for k in heads])
            dv = wide([dv_intra[k] + dv_state[k] for k in heads])
            dbl = e["ebl"] * wide([jnp.sum(sts[k] * dsts[k], axis=0, keepdims=True) for k in heads])
            dq = dqe * e["eb"] + dqa * e["ea"]
            dk = dka * e["ek"] + dkd * e["ed"]
            dkd_kd = dkd * kd.astype(F32)
            db = dqe * qe.astype(F32) + dqa * qa.astype(F32) - dka * ka.astype(F32) - dkd_kd
            db = db + jnp.where(last_row, dbl + jnp.sum(dkd_kd, axis=0, keepdims=True), 0.0)
            dg = _mm_tri_exact(triu, db)
            df = dg / e["f"] - dk
            sg = e["sg"]
            sq = e["sq"]
            dhq = _bf(dq * (sq * (1.0 + hq * (1.0 - sq))))
            dhf = _bf(df * (1.0 - lb) * sg * (1.0 - sg))
            dhi = _bf(dv)
            dlb_new = dlb_acc + jnp.where(row0, jnp.sum(df * (1.0 - sg), axis=0, keepdims=True), 0.0)
            for k in heads:
                ds_scr[k] = ds_new[k] + dsts[k] * _head(e["ebl"], k)
            dp_ref[rows, 0:wd] = dhq
            dp_ref[rows, wd:2 * wd] = dhf
            dp_ref[rows, 2 * wd:3 * wd] = dhi
            dlb_ref[...] = dlb_new
            return carry

        lax.fori_loop(0, ncb, chunk, 0, unroll=2)

    rev = lambda h, i: (nblk - 1 - i, h)
    return pl.pallas_call(
        body, grid=(HEADS // hg, nblk),
        in_specs=[pl.BlockSpec((rb, 3 * HD * hg), rev), pl.BlockSpec((2, HD * hg), lambda h, i: (0, h)),
                  pl.BlockSpec((rb, HD * hg), rev), pl.BlockSpec((ncb, hg, HD, HD), lambda h, i: (nblk - 1 - i, h, 0, 0))],
        out_specs=[pl.BlockSpec((rb, 3 * HD * hg), rev), pl.BlockSpec((8, HD * hg), lambda h, i: (0, h))],
        out_shape=[jax.ShapeDtypeStruct((t, 3 * D), BF16), jax.ShapeDtypeStruct((8, D), F32)],
        scratch_shapes=[pltpu.VMEM((hg, HD, HD), F32)],
        name="hgrn_bwd", compiler_params=_cp(48),
    )(proj, lbw, do_raw, states)


def _kv_variants(tile, odd):
    low = lax.broadcasted_iota(jnp.int32, tile.shape, 1) < 64
    if odd:
        hi = jnp.where(low, 0.0, tile)
        lo = pltpu.roll(hi, 64, 1)
    else:
        lo = jnp.where(low, tile, 0.0)
        hi = pltpu.roll(lo, 64, 1)
    return _bf(lo), _bf(hi)


def _attn_masks(n):
    qi = lax.broadcasted_iota(jnp.int32, (AB, AB), 0)
    kj = lax.broadcasted_iota(jnp.int32, (AB, AB), 1)
    cur = kj <= qi
    return cur, cur | (n > 0), qi <= kj


def _kv_all(prev_ref, cur_ref):
    out = []
    for tl in range(2):
        cols = slice(tl * 128, (tl + 1) * 128)
        tile = jnp.concatenate([prev_ref[:, cols], cur_ref[:, cols]], axis=0)
        out.append(_kv_variants(tile, 0))
        out.append(_kv_variants(tile, 1))
    return out


def _window(a2, cur):
    return jnp.where(cur, a2[:, AB:], a2[:, :AB])


def _attn_softmax(scores, sinks, cur, ok):
    s = [jnp.where(ok, _window(s2, cur) * ATT_SCALE, NEG) for s2 in scores]
    m = [jnp.maximum(jnp.max(si, axis=-1, keepdims=True), sink) for si, sink in zip(s, sinks)]
    p = [jnp.exp(si - mi) for si, mi in zip(s, m)]
    es = [jnp.exp(sink - mi) for sink, mi in zip(sinks, m)]
    inv = [1.0 / (jnp.sum(pi, axis=-1, keepdims=True) + ei) for pi, ei in zip(p, es)]
    return [pi * ii for pi, ii in zip(p, inv)], [ei * ii for ei, ii in zip(es, inv)]


def _spread(pc, cur):
    return jnp.concatenate([jnp.where(cur, 0.0, pc), jnp.where(cur, pc, 0.0)], axis=1)


def _spread_t(pct, cur_t):
    return jnp.concatenate([jnp.where(cur_t, 0.0, pct), jnp.where(cur_t, pct, 0.0)], axis=0)


def _attn_fwd(proj, sinks):
    t = proj.shape[0]
    nb = t // AB

    nsub = 4
    assert nb % nsub == 0

    def body(q_ref, kc_ref, kp_ref, vc_ref, vp_ref, sink_ref, o_ref):
        sinks_v = [sink_ref[0, h] for h in range(QH)]
        heads = [(j, ab) for j in range(8) for ab in range(2)]
        for sb in range(nsub):
            rows = pl.ds(AB * sb, AB)
            before = pl.ds(AB * (sb - 1), AB)
            cur, ok, _ = _attn_masks(nsub * pl.program_id(0) + sb)
            kvars = _kv_all(kp_ref if sb == 0 else kc_ref.at[before, :], kc_ref.at[rows, :])
            vvars = _kv_all(vp_ref if sb == 0 else vc_ref.at[before, :], vc_ref.at[rows, :])
            qps = [_bf(q_ref[rows, 128 * j:128 * (j + 1)]) for j in range(8)]
            scores = [_mm_nt(qps[j], kvars[j // 2][ab]) for j, ab in heads]
            pcs, _ = _attn_softmax(scores, sinks_v, cur, ok)
            parts = [_mm(_bf(_spread(pcs[h], cur)), vvars[j // 2][ab]) for h, (j, ab) in enumerate(heads)]
            for j in range(8):
                o_ref[rows, 128 * j:128 * (j + 1)] = parts[2 * j] + parts[2 * j + 1]

    prev = lambda n: jnp.maximum(nsub * n - 1, 0)
    step = nsub * AB
    return pl.pallas_call(
        body, grid=(nb // nsub,),
        in_specs=[pl.BlockSpec((step, D), lambda n: (n, C_AQ // D)),
                  pl.BlockSpec((step, 256), lambda n: (n, C_AK // 256)),
                  pl.BlockSpec((AB, 256), lambda n: (prev(n), C_AK // 256)),
                  pl.BlockSpec((step, 256), lambda n: (n, C_AV // 256)),
                  pl.BlockSpec((AB, 256), lambda n: (prev(n), C_AV // 256)),
                  pl.BlockSpec(memory_space=pltpu.SMEM)],
        out_specs=pl.BlockSpec((step, D), lambda n: (n, 0)),
        out_shape=jax.ShapeDtypeStruct((t, D), F32),
        name="attn_fwd", compiler_params=_cp(32),
    )(proj, proj, proj, proj, proj, sinks)


def _attn_bwd(proj, sinks, do_a):
    t = proj.shape[0]
    nb = t // AB
    nsub = 2
    assert nb % nsub == 0
    steps = nb // nsub
    step = nsub * AB

    def body(q_ref, kc_ref, kp_ref, vc_ref, vp_ref, do_ref, sink_ref, dq_ref, dkv_ref, dsink_ref, carry):
        n = pl.program_id(0)

        @pl.when(n == 0)
        def _():
            dsink_ref[...] = jnp.zeros_like(dsink_ref)
            carry[...] = jnp.zeros_like(carry)

        def one_block(sb):
            rows = pl.ds(AB * sb, AB)
            before = pl.ds(AB * (sb - 1), AB)
            cur, ok, cur_t = _attn_masks(nsub * n + sb)
            low = lax.broadcasted_iota(jnp.int32, (2 * AB, 128), 1) < 64
            lane = lax.broadcasted_iota(jnp.int32, (8, 128), 1)
            row0 = lax.broadcasted_iota(jnp.int32, (8, 128), 0) == 0
            kvars = _kv_all(kp_ref if sb == 0 else kc_ref.at[before, :], kc_ref.at[rows, :])
            vvars = _kv_all(vp_ref if sb == 0 else vc_ref.at[before, :], vc_ref.at[rows, :])
            qps = [_bf(q_ref[rows, 128 * j:128 * (j + 1)]) for j in range(8)]
            dops = [_bf(do_ref[rows, 128 * j:128 * (j + 1)]) for j in range(8)]
            heads = [(j, ab) for j in range(8) for ab in range(2)]
            scores = [_mm_nt(qps[j], kvars[j // 2][ab]) for j, ab in heads]
            dps = [_mm_nt(dops[j], vvars[j // 2][ab]) for j, ab in heads]
            pcs, pss = _attn_softmax(scores, [sink_ref[0, h] for h in range(QH)], cur, ok)
            dpcs = [_window(dp2, cur) for dp2 in dps]
            rss = [jnp.sum(pc * dpc, axis=-1, keepdims=True) for pc, dpc in zip(pcs, dpcs)]
            dscs = [pc * (dpc - rs) for pc, dpc, rs in zip(pcs, dpcs, rss)]
            dsink = jnp.zeros((8, 128), F32)
            for h in range(QH):
                dsink = dsink + jnp.where(row0 & (lane == h), -jnp.sum(pss[h] * rss[h]), 0.0)
            dq_terms = [_mm(_bf(_spread(dscs[h], cur)), kvars[j // 2][ab]) for h, (j, ab) in enumerate(heads)]
            for j in range(8):
                dq_ref[rows, 128 * j:128 * (j + 1)] = _bf((dq_terms[2 * j] + dq_terms[2 * j + 1]) * ATT_SCALE)
            dsc_t = [_bf(_spread_t(dsc.T, cur_t)) for dsc in dscs]
            pc_t = [_bf(_spread_t(pc.T, cur_t)) for pc in pcs]
            dk_terms = [_mm(dsc_t[h], qps[j]) for h, (j, ab) in enumerate(heads)]
            dv_terms = [_mm(pc_t[h], dops[j]) for h, (j, ab) in enumerate(heads)]
            dk_ab = [[dk_terms[4 * g + ab] + dk_terms[4 * g + 2 + ab] for ab in range(2)] for g in range(4)]
            dv_ab = [[dv_terms[4 * g + ab] + dv_terms[4 * g + 2 + ab] for ab in range(2)] for g in range(4)]
            dkts, dvts = [], []
            for tl in range(2):
                ke, ko = dk_ab[2 * tl], dk_ab[2 * tl + 1]
                ve, vo = dv_ab[2 * tl], dv_ab[2 * tl + 1]
                dkts.append((jnp.where(low, ke[0], 0.0) + pltpu.roll(jnp.where(low, 0.0, ke[1]), 64, 1)
                             + jnp.where(low, 0.0, ko[1]) + pltpu.roll(jnp.where(low, ko[0], 0.0), 64, 1)) * ATT_SCALE)
                dvts.append(jnp.where(low, ve[0], 0.0) + pltpu.roll(jnp.where(low, 0.0, ve[1]), 64, 1)
                            + jnp.where(low, 0.0, vo[1]) + pltpu.roll(jnp.where(low, vo[0], 0.0), 64, 1))
            return dkts, dvts, dsink

        @pl.when(n < steps)
        def _():
            (dk0, dv0, ds0), (dk1, dv1, ds1) = one_block(0), one_block(1)
            first, second = slice(0, AB), slice(AB, 2 * AB)
            for tl in range(2):
                for cols, g0, g1 in ((slice(tl * 128, (tl + 1) * 128), dk0[tl], dk1[tl]),
                                     (slice(256 + tl * 128, 256 + (tl + 1) * 128), dv0[tl], dv1[tl])):
                    dkv_ref[first, cols] = _bf(carry[first, cols])
                    dkv_ref[second, cols] = _bf(carry[second, cols] + g0[first])
                    carry[first, cols] = g0[second] + g1[first]
                    carry[second, cols] = g1[second]
            dsink_ref[...] += ds0 + ds1

        @pl.when(n == steps)
        def _():
            dkv_ref[...] = _bf(carry[...])

    cur = lambda n: jnp.minimum(n, steps - 1)
    last = lambda n: jnp.clip(n - 1, 0, steps - 1)
    prev = lambda n: jnp.clip(nsub * n - 1, 0, nb - 1)
    return pl.pallas_call(
        body, grid=(steps + 1,),
        in_specs=[pl.BlockSpec((step, D), lambda n: (cur(n), C_AQ // D)),
                  pl.BlockSpec((step, 256), lambda n: (cur(n), C_AK // 256)),
                  pl.BlockSpec((AB, 256), lambda n: (prev(n), C_AK // 256)),
                  pl.BlockSpec((step, 256), lambda n: (cur(n), C_AV // 256)),
                  pl.BlockSpec((AB, 256), lambda n: (prev(n), C_AV // 256)),
                  pl.BlockSpec((step, D), lambda n: (cur(n), 0)),
                  pl.BlockSpec(memory_space=pltpu.SMEM)],
        out_specs=[pl.BlockSpec((step, D), lambda n: (cur(n), 0)),
                   pl.BlockSpec((step, 512), lambda n: (last(n), 0)),
                   pl.BlockSpec((8, 128), lambda n: (0, 0))],
        out_shape=[jax.ShapeDtypeStruct((t, D), BF16), jax.ShapeDtypeStruct((t, 512), BF16),
                   jax.ShapeDtypeStruct((8, 128), F32)],
        scratch_shapes=[pltpu.VMEM((step, 512), F32)],
        name="attn_bwd", compiler_params=_cp(40),
    )(proj, proj, proj, proj, proj, do_a, sinks)


def _silu_and_grad(v):
    s = _sigmoid(v)
    return v * s, s * (1.0 + v * (1.0 - s))


def _tail(o_raw, o_a, proj, x2d, tgt, wbh, wba, wout, hnw, fnw, tb):
    t = x2d.shape[0]

    def body(or_ref, oa_ref, hg_ref, ag0, ag1, mh0, mh1, ma0, ma1, x_ref, t_ref, wbh_ref, wba_ref, wout_ref, hnw_ref,
             fnw_ref, dx2_ref, dor_ref, doa_ref, dhg_ref, dagm_ref, gh_ref, ga_ref, mg_ref, dyh_ref, dya_ref, dx2b_ref,
             sums_ref):
        @pl.when(pl.program_id(0) == 0)
        def _():
            sums_ref[...] = jnp.zeros_like(sums_ref)

        halves = lambda a, b: jnp.concatenate([a[...], b[...]], axis=1)
        hnw_v = hnw_ref[...]
        fnw_v = fnw_ref[...]
        o = or_ref[...]
        rs, xhs = [], []
        for h in range(HEADS):
            oh = o[:, h * HD:(h + 1) * HD]
            r = lax.rsqrt(jnp.mean(oh * oh, axis=-1, keepdims=True) + EPS)
            rs.append(r)
            xhs.append(oh * r)
        xh = jnp.concatenate(xhs, axis=1)
        on = xh * hnw_v
        sil_hg, dsil_hg = _silu_and_grad(hg_ref[...])
        gh_b = _bf(on * sil_hg)
        y_h = _mm(gh_b, wbh_ref[...])
        oa = oa_ref[...]
        sil_ag, dsil_ag = _silu_and_grad(halves(ag0, ag1))
        ga_b = _bf(oa * sil_ag)
        y_a = _mm(ga_b, wba_ref[...])
        s_mh = _sigmoid(halves(mh0, mh1))
        s_ma = _sigmoid(halves(ma0, ma1))
        mg_b = _bf(s_mh * y_h + s_ma * y_a)
        x2 = x_ref[...] + _mm(mg_b, wout_ref[...])
        r2 = lax.rsqrt(jnp.mean(x2 * x2, axis=-1, keepdims=True) + EPS)
        xh2 = x2 * r2
        err = xh2 * fnw_v - t_ref[...]
        loss = 0.5 * jnp.sum(jnp.mean(err * err, axis=-1, keepdims=True))
        dy = err * (1.0 / D)
        dfnw = jnp.sum(dy * xh2, axis=0, keepdims=True)
        dxh2 = dy * fnw_v
        dx2 = r2 * (dxh2 - xh2 * jnp.mean(dxh2 * xh2, axis=-1, keepdims=True))
        dx2_ref[...] = dx2
        dx2_b = _bf(dx2)
        dmg = _mm_nt(dx2_b, wout_ref[...])
        dmg_h = dmg * s_mh
        dmg_a = dmg * s_ma
        dyh_b = _bf(dmg_h)
        dya_b = _bf(dmg_a)
        dagm_ref[:, D:2 * D] = _bf(dmg_h * y_h * (1.0 - s_mh))
        dagm_ref[:, 2 * D:3 * D] = _bf(dmg_a * y_a * (1.0 - s_ma))
        dgh = _mm_nt(dyh_b, wbh_ref[...])
        dga = _mm_nt(dya_b, wba_ref[...])
        doa_ref[...] = dga * sil_ag
        dagm_ref[:, 0:D] = _bf(dga * oa * dsil_ag)
        dhg_ref[...] = _bf(dgh * on * dsil_hg)
        don = dgh * sil_hg
        dhnw = jnp.sum(don * xh, axis=0, keepdims=True)
        dxh = don * hnw_v
        dos = []
        for h in range(HEADS):
            sl = slice(h * HD, (h + 1) * HD)
            dos.append(rs[h] * (dxh[:, sl] - xhs[h] * jnp.mean(dxh[:, sl] * xhs[h], axis=-1, keepdims=True)))
        dor_ref[...] = jnp.concatenate(dos, axis=1)
        gh_ref[...] = gh_b
        ga_ref[...] = ga_b
        mg_ref[...] = mg_b
        dyh_ref[...] = dyh_b
        dya_ref[...] = dya_b
        dx2b_ref[...] = dx2_b
        row = lax.broadcasted_iota(jnp.int32, (8, D), 0)
        sums_ref[...] += jnp.where(row == 0, dfnw, 0.0) + jnp.where(row == 1, dhnw, 0.0) + jnp.where(row == 2, loss, 0.0)

    rowblk = lambda c: pl.BlockSpec((tb, D), lambda i: (i, c))
    half = lambda c: pl.BlockSpec((tb, 512), lambda i: (i, c))
    full = lambda shape: pl.BlockSpec(shape, lambda i: (0, 0))
    return pl.pallas_call(
        body, grid=(t // tb,),
        in_specs=[rowblk(0), rowblk(0), rowblk(C_HG // D), half(C_AG // 512), half(C_AG // 512 + 1), half(C_MH // 512),
                  half(C_MH // 512 + 1), half(C_MA // 512), half(C_MA // 512 + 1), rowblk(0), rowblk(0),
                  full((D, D)), full((D, D)), full((D, D)), full((1, D)), full((1, D))],
        out_specs=[rowblk(0), rowblk(0), rowblk(0), rowblk(0), pl.BlockSpec((tb, 3 * D), lambda i: (i, 0))]
        + [rowblk(0)] * 6 + [full((8, D))],
        out_shape=[jax.ShapeDtypeStruct((t, D), F32)] * 3
        + [jax.ShapeDtypeStruct((t, D), BF16), jax.ShapeDtypeStruct((t, 3 * D), BF16)]
        + [jax.ShapeDtypeStruct((t, D), BF16)] * 6 + [jax.ShapeDtypeStruct((8, D), F32)],
        name="tail", compiler_params=_cp(56),
    )(o_raw, o_a, proj, proj, proj, proj, proj, proj, proj, x2d, tgt, wbh, wba, wout, hnw, fnw)


def _wgrad3(gh, dyh, ga, dya, mg, dx2b, tk, after):
    t = dyh.shape[0]

    def body(a0, b0, a1, b1, a2, b2, _, o0, o1, o2):
        @pl.when(pl.program_id(0) == 0)
        def _():
            o0[...] = jnp.zeros_like(o0)
            o1[...] = jnp.zeros_like(o1)
            o2[...] = jnp.zeros_like(o2)

        o0[...] += _mm_tn(a0[...], b0[...])
        o1[...] += _mm_tn(a1[...], b1[...])
        o2[...] += _mm_tn(a2[...], b2[...])

    blk = pl.BlockSpec((tk, D), lambda k: (k, 0))
    out = pl.BlockSpec((D, D), lambda k: (0, 0))
    return pl.pallas_call(
        body, grid=(t // tk,), in_specs=[blk] * 6 + [pl.BlockSpec(memory_space=pl.ANY)], out_specs=[out] * 3,
        out_shape=[jax.ShapeDtypeStruct((D, D), F32)] * 3,
        name="wgrad3", compiler_params=_cp(48),
    )(gh, dyh, ga, dya, mg, dx2b, after)


def _inproj_wgrad(xnt, pieces, nb, col0, dw, name):
    t = xnt.shape[1]
    counts = [p.shape[1] // nb for p in pieces]
    firsts = [sum(counts[:k]) for k in range(len(pieces))]
    assert col0 % nb == 0

    def body(xnt_ref, *refs):
        o_ref = refs[-1]
        j = pl.program_id(0)
        for first, count, p_ref in zip(firsts, counts, refs):
            @pl.when((j >= first) & (j < first + count))
            def _(p_ref=p_ref):
                o_ref[...] = _mm(xnt_ref[...], p_ref[...]).astype(BF16)

    def piece_spec(first, count):
        return pl.BlockSpec((t, nb), lambda j: (0, jnp.clip(j - first, 0, count - 1)))

    carried = [] if dw is None else [dw]
    return pl.pallas_call(
        body, grid=(sum(counts),),
        in_specs=[pl.BlockSpec((D, t), lambda j: (0, 0), pipeline_mode=pl.Buffered(1))]
        + [piece_spec(f, c) for f, c in zip(firsts, counts)] + [pl.BlockSpec(memory_space=pl.ANY)] * len(carried),
        out_specs=pl.BlockSpec((D, nb), lambda j: (0, j + col0 // nb)),
        out_shape=jax.ShapeDtypeStruct((D, DIN), BF16),
        input_output_aliases={1 + len(pieces): 0} if carried else {},
        name=name, compiler_params=_cp(56),
    )(xnt, *pieces, *carried)


def _inproj_dgrad(pieces, w_p, x2d, dx2, norm_w, tb, after):
    t = x2d.shape[0]

    def body(*refs):
        piece_refs = refs[:len(pieces)]
        w_ref, x_ref, dx2_ref, nw_ref, _, gx_ref, dnw_ref = refs[len(pieces):]

        @pl.when(pl.program_id(0) == 0)
        def _():
            dnw_ref[...] = jnp.zeros_like(dnw_ref)

        dxn = None
        off = 0
        for p in piece_refs:
            width = p.shape[1]
            for q in range(w_ref.shape[0]):
                lo, hi = max(off, q * PAIR), min(off + width, (q + 1) * PAIR)
                if lo < hi:
                    term = _mm_nt(p[:, lo - off:hi - off], w_ref[q, :, lo - q * PAIR:hi - q * PAIR])
                    dxn = term if dxn is None else dxn + term
            off += width
        xv = x_ref[...]
        r = lax.rsqrt(jnp.mean(xv * xv, axis=-1, keepdims=True) + EPS)
        xh = xv * r
        dxh = dxn * nw_ref[...]
        gx_ref[...] = dx2_ref[...] + r * (dxh - xh * jnp.mean(dxh * xh, axis=-1, keepdims=True))
        row0 = lax.broadcasted_iota(jnp.int32, (8, D), 0) == 0
        dnw_ref[...] += jnp.where(row0, jnp.sum(dxn * xh, axis=0, keepdims=True), 0.0)

    rowblk = pl.BlockSpec((tb, D), lambda i: (i, 0))
    return pl.pallas_call(
        body, grid=(t // tb,),
        in_specs=[pl.BlockSpec((tb, p.shape[1]), lambda i: (i, 0)) for p in pieces]
        + [pl.BlockSpec(w_p.shape, lambda i: (0, 0, 0), pipeline_mode=pl.Buffered(1)), rowblk, rowblk,
           pl.BlockSpec((1, D), lambda i: (0, 0)), pl.BlockSpec(memory_space=pl.ANY)],
        out_specs=[rowblk, pl.BlockSpec((8, D), lambda i: (0, 0))],
        out_shape=[jax.ShapeDtypeStruct((t, D), F32), jax.ShapeDtypeStruct((8, D), F32)],
        name="inproj_dgrad", compiler_params=_cp(60),
    )(*pieces, w_p, x2d, dx2, norm_w, after)


def _adamw_math(w, g, m, v):
    m = B1 * m + (1.0 - B1) * g
    v = B2 * v + (1.0 - B2) * (g * g)
    m_hat = m / (1.0 - B1 ** STEP)
    v_hat = v / (1.0 - B2 ** STEP)
    delta = -LR * (m_hat / (jnp.sqrt(v_hat) + ADAM_EPS) + WD * w)
    return delta, m, v


def _adamw_shard(recv, sums, chip, w, m, v, rows, name):
    nparts, nr, nc = recv.shape

    def body(chip_ref, own_ref, p_ref, w_ref, m_ref, v_ref, g_ref, d_ref, nm_ref, nv_ref):
        g = own_ref[0].astype(F32)
        for s in range(nparts):
            g = g + p_ref[s].astype(F32)
        d, nm, nv = _adamw_math(w_ref[...], g, m_ref[...], v_ref[...])
        g_ref[...] = g
        d_ref[...] = d
        nm_ref[...] = nm
        nv_ref[...] = nv

    blk = pl.BlockSpec((rows, nc), lambda i, chip_ref: (i, 0))
    return pl.pallas_call(
        body,
        grid_spec=pltpu.PrefetchScalarGridSpec(
            num_scalar_prefetch=1, grid=(nr // rows,),
            in_specs=[pl.BlockSpec((1, rows, nc), lambda i, chip_ref: (chip_ref[0], i, 0)),
                      pl.BlockSpec((nparts, rows, nc), lambda i, chip_ref: (0, i, 0)), blk, blk, blk],
            out_specs=[blk] * 4),
        out_shape=[jax.ShapeDtypeStruct((nr, nc), F32)] * 4,
        name=name, compiler_params=_cp(48),
    )(chip, sums, recv, w, m, v)


def _adamw_sum8(parts, w, m, v, after, name):
    def body(p_ref, w_ref, m_ref, v_ref, _, g_ref, d_ref, nm_ref, nv_ref):
        g = p_ref[0].astype(F32)
        for s in range(1, NDEV):
            g = g + p_ref[s].astype(F32)
        d, nm, nv = _adamw_math(w_ref[...], g, m_ref[...], v_ref[...])
        g_ref[...] = g
        d_ref[...] = d
        nm_ref[...] = nm
        nv_ref[...] = nv

    vm = pl.BlockSpec(memory_space=pltpu.VMEM)
    return pl.pallas_call(
        body, out_shape=[jax.ShapeDtypeStruct(w.shape, F32)] * 4,
        in_specs=[vm, vm, vm, vm, pl.BlockSpec(memory_space=pl.ANY)], out_specs=[vm] * 4, name=name,
    )(parts, w, m, v, after)


SMALL_ROWS = dict(norm_w=0, lower_bound=1, hgrn_norm_w=3, final_norm_w=4, sinks=5, loss=6)


def _pack_small_grads(dnw, dlb, sums, dsink):
    def body(dnw_ref, dlb_ref, sums_ref, dsink_ref, o_ref):
        o_ref[...] = jnp.zeros_like(o_ref)
        o_ref[0:1, :] = dnw_ref[0:1, :]
        o_ref[1:2, :] = dlb_ref[0:1, :]
        o_ref[3:4, :] = sums_ref[1:2, :]
        o_ref[4:5, :] = sums_ref[0:1, :]
        o_ref[5:6, 0:128] = dsink_ref[0:1, :]
        o_ref[6:7, :] = sums_ref[2:3, :]

    return pl.pallas_call(body, out_shape=jax.ShapeDtypeStruct((8, D), F32), name="pack_small_grads")(dnw, dlb, sums, dsink)


def _adamw_small(parts, ws, ms, vs):
    shapes = [a.shape for a in ws]

    def body(p_ref, *refs):
        w, m, v = refs[0:5], refs[5:10], refs[10:15]
        outs = [refs[15 + 5 * i:20 + 5 * i] for i in range(4)]
        loss_ref = refs[35]

        def total(row, width):
            g = p_ref[0, row:row + 1, 0:width]
            for s in range(1, NDEV):
                g = g + p_ref[s, row:row + 1, 0:width]
            return g

        loss_ref[...] = total(6, 128)
        lb = _lower_bound(w[1])
        ga0 = total(1, D) * lb * (1.0 - lb)
        grads = [total(0, D), None, total(3, D), total(4, D), total(5, QH)]
        for i in (0, 2, 3, 4):
            res = (grads[i],) + _adamw_math(w[i][...], grads[i], m[i][...], v[i][...])
            for o, val in zip(outs, res):
                o[i][...] = val
        for r, g in ((0, ga0), (1, -ga0)):
            res = (g,) + _adamw_math(w[1][r:r + 1, :], g, m[1][r:r + 1, :], v[1][r:r + 1, :])
            for o, val in zip(outs, res):
                o[1][r:r + 1, :] = val

    res = pl.pallas_call(
        body, out_shape=[jax.ShapeDtypeStruct(s, F32) for s in shapes] * 4 + [jax.ShapeDtypeStruct((1, 128), F32)],
        name="adamw_small",
    )(parts, *ws, *ms, *vs)
    return [res[5 * i:5 * i + 5] for i in range(4)], res[20][0, 0]


def kernel(x, norm_w, w_in, hgrn_lower_bound, hgrn_norm_w, w_branch_hgrn, attn_sinks, w_branch_attn, w_out, final_norm_w, loss_target, m_norm_w, m_w_in, m_hgrn_lower_bound, m_hgrn_norm_w, m_w_branch_hgrn, m_attn_sinks, m_w_branch_attn, m_w_out, m_final_norm_w, v_norm_w, v_w_in, v_hgrn_lower_bound, v_hgrn_norm_w, v_w_branch_hgrn, v_attn_sinks, v_w_branch_attn, v_w_out, v_final_norm_w):
    t = x.shape[1]
    x2d = x.reshape(t, D)
    tgt = loss_target.reshape(t, D)
    fnw = final_norm_w.reshape(1, D)
    row_blk = min(256, t)
    big_blk = min(512, t)

    chip = (2 * lax.axis_index("x") + lax.axis_index("y")).astype(jnp.int32).reshape(1)
    w_p, xn, xnt, proj_own = _gather_in_projection(w_in[0], x2d, norm_w)
    wbh, wba, wout = (g.reshape(D, D) for g in _gather_square(
        [w_branch_hgrn[0].astype(BF16), w_branch_attn[0].astype(BF16), w_out[0].astype(BF16)], after=w_p))

    proj = _inproj_fwd(xn, w_p, proj_own, chip, min(1024, t))
    o_raw, states = _hgrn_fwd(proj, hgrn_lower_bound, big_blk, HGRN_GROUP)
    o_a = _attn_fwd(proj, attn_sinks)
    (dx2, do_raw, do_a, d_hg, d_agm, gh, ga, mg, dyh, dya, dx2b, sums) = _tail(
        o_raw, o_a, proj, x2d, tgt, wbh, wba, wout, hgrn_norm_w, fnw, row_blk)
    d_aq, d_kv, dsink = _attn_bwd(proj, attn_sinks, do_a)
    d_hgrn, dlb = _hgrn_bwd(proj, hgrn_lower_bound, do_raw, states, big_blk, HGRN_GROUP)
    pieces = (d_hgrn, d_hg, d_aq, d_kv, d_agm)
    dw_cat = _inproj_wgrad(xnt, [d_hgrn], CB, 0, None, "inproj_wgrad_hgrn")
    dw_cat = _inproj_wgrad(xnt, [d_hg, d_aq, d_kv], CB // 2, C_HG, dw_cat, "inproj_wgrad_mid")
    dw_cat = _inproj_wgrad(xnt, [d_agm], CB, C_AG, dw_cat, "inproj_wgrad_gates")

    dwin_r = dw_cat.reshape(D, NDEV, IN_SHARD).transpose(1, 0, 2)
    dwbh, dwba, dwout = _wgrad3(gh, dyh, ga, dya, mg, dx2b, big_blk, after=dw_cat)
    slots = lambda a: a.reshape(NDEV, ROW_SHARD, D).astype(BF16)
    got = _exchange_pair([dwin_r], after=dwbh)
    core = lax.axis_index("c").astype(jnp.int32).reshape(1)
    s_in = _pair_sum(dwin_r, got[0], core, 4 * ROW_SHARD, "pair_sum_w_in")
    rin, = _exchange_chips([s_in])
    rbh, rba, rout = _exchange_square([slots(dwbh), slots(dwba), slots(dwout)])
    grad_x, dnw = _inproj_dgrad(pieces, w_p, x2d, dx2, norm_w, big_blk, after=s_in)
    rsm = _exchange_small(_pack_small_grads(dnw, dlb, sums, dsink))
    g_in, d_in, nm_in, nv_in = _adamw_shard(rin, s_in, chip, w_in[0], m_w_in[0], v_w_in[0], 128, "adamw_w_in")
    g_bh, d_bh, nm_bh, nv_bh = _adamw_sum8(
        rbh, w_branch_hgrn[0], m_w_branch_hgrn[0], v_w_branch_hgrn[0], dnw, "adamw_w_bh")
    g_ba, d_ba, nm_ba, nv_ba = _adamw_sum8(
        rba, w_branch_attn[0], m_w_branch_attn[0], v_w_branch_attn[0], dnw, "adamw_w_ba")
    g_out, d_out, nm_out, nv_out = _adamw_sum8(rout, w_out[0], m_w_out[0], v_w_out[0], dnw, "adamw_w_out")
    (sg, sd, sm, sv), loss = _adamw_small(
        rsm,
        (norm_w, hgrn_lower_bound, hgrn_norm_w, fnw, attn_sinks),
        (m_norm_w, m_hgrn_lower_bound, m_hgrn_norm_w, m_final_norm_w.reshape(1, D), m_attn_sinks),
        (v_norm_w, v_hgrn_lower_bound, v_hgrn_norm_w, v_final_norm_w.reshape(1, D), v_attn_sinks))

    def group(s, w_in_v, bh, ba, out):
        nw, lb, hnw, fn, sinks = s
        return (nw, w_in_v[None], lb, hnw, bh[None], sinks, ba[None], out[None], fn.reshape(D))

    return (loss, grad_x.reshape(1, t, D),
            *group(sg, g_in, g_bh, g_ba, g_out), *group(sd, d_in, d_bh, d_ba, d_out),
            *group(sm, nm_in, nm_bh, nm_ba, nm_out), *group(sv, nv_in, nv_bh, nv_ba, nv_out))
```

```python
import jax
import jax.numpy as jnp
from jax import lax
from jax.experimental import pallas as pl
from jax.experimental.pallas import tpu as pltpu
from jax.experimental.pallas import tpu_sc as plsc

F32 = jnp.float32
BF16 = jnp.bfloat16

D = 1024
DIN = 8704
NDEV = 8
IN_SHARD = DIN // NDEV
PAIR = 2 * IN_SHARD
ROW_SHARD = D // NDEV
HEADS = 8
HD = 128
CH = 64
HGRN_GROUP = 8
QH = 16
AB = 128
EPS = 1e-6
NEG = -1e30
ATT_SCALE = 0.125

C_HG = 3072
C_AQ = 4096
C_AK = 5120
C_AV = 5376
C_AG = 5632
C_MH = 6656
C_MA = 7680
CB = 512
ROW_PARTS = 2

LR = 0.001
B1 = 0.9
B2 = 0.999
ADAM_EPS = 1e-08
WD = 0.01
STEP = 10

MESH = pl.DeviceIdType.MESH


def _cp(vmem_mb):
    return pltpu.CompilerParams(vmem_limit_bytes=vmem_mb * 1024 * 1024)


def _mm(a, b):
    return jnp.dot(a, b, preferred_element_type=F32)


def _mm_nt(a, b):
    return lax.dot_general(a, b, (((1,), (1,)), ((), ())), preferred_element_type=F32)


def _mm_tn(a, b):
    return lax.dot_general(a, b, (((0,), (0,)), ((), ())), preferred_element_type=F32)


def _tri3(lower):
    r = lax.broadcasted_iota(jnp.int32, (CH, 3 * CH), 0)
    c = lax.broadcasted_iota(jnp.int32, (CH, 3 * CH), 1)
    c = jnp.where(c >= 2 * CH, c - 2 * CH, jnp.where(c >= CH, c - CH, c))
    return ((r >= c) if lower else (c >= r)).astype(BF16)


def _mm_tri_exact(tri3, g):
    g1 = g.astype(BF16)
    r1 = g - g1.astype(F32)
    g2 = r1.astype(BF16)
    g3 = (r1 - g2.astype(F32)).astype(BF16)
    return _mm(tri3, jnp.concatenate([g1, g2, g3], axis=0))


def _sigmoid(v):
    return 0.5 * jnp.tanh(0.5 * v) + 0.5


def _bf(v):
    return v.astype(BF16)


def _place():
    x, y, c = lax.axis_index("x"), lax.axis_index("y"), lax.axis_index("c")
    return (x, y, c), (x, y, 1 - c), [(1 - x, y), (x, 1 - y), (1 - x, 1 - y)]


def _dev_index(px, py, pc):
    return 4 * px + 2 * py + pc


def _gather_in_projection(w_in_s, x2d, norm_w):
    half = D // 2
    t = x2d.shape[0]
    prep_rows = min(512, t)
    nprep = t // prep_rows

    def body(win_ref, x_hbm, nw_ref, wp_g, xn_hbm, xnt_hbm, proj_hbm, give, take, mine, xbuf, xnbuf, xntbuf, w_own, pbuf,
             send_sems, recv_sems, loc_sem, swap_sems, in_sems, out_sems, own_sem):
        (x, y, c), sibling, chips = _place()
        give[...] = win_ref[pl.ds(pl.multiple_of(half * (1 - c), half), half), :].astype(BF16)
        swap = pltpu.make_async_remote_copy(src_ref=give, dst_ref=take, send_sem=swap_sems.at[0], recv_sem=swap_sems.at[1],
                                            device_id=sibling, device_id_type=MESH)
        swap.start()
        swap.wait()
        own = win_ref[pl.ds(pl.multiple_of(half * c, half), half), :]
        other = take[...].astype(F32)
        mine[...] = jnp.where(c == 0, jnp.concatenate([own, other], axis=1),
                              jnp.concatenate([other, own], axis=1)).astype(BF16)

        def place(px, py, pc):
            return wp_g.at[2 * px + py, pl.ds(pl.multiple_of(half * pc, half), half), :]

        def copy(kind, origin, to, src=mine):
            return pltpu.make_async_remote_copy(
                src_ref=src, dst_ref=place(*origin), send_sem=send_sems.at[kind], recv_sem=recv_sems.at[kind],
                device_id=to, device_id_type=MESH)

        me = (x, y, c)
        local = pltpu.make_async_copy(mine, place(*me), loc_sem)
        local.start()
        first = [copy(0, me, sibling)] + [copy(1 + j, me, (*chip, c)) for j, chip in enumerate(chips)]
        for cp in first:
            cp.start()

        copy(0, (x, y, 1 - c), me).wait_recv()
        local.wait()
        my_chip = 2 * x + y
        fetch = pltpu.make_async_copy(wp_g.at[my_chip], w_own, own_sem)
        fetch.start()

        def rows_of(i):
            return pl.ds(pl.multiple_of(i * prep_rows, prep_rows), prep_rows)

        def load(i, slot):
            return pltpu.make_async_copy(x_hbm.at[rows_of(i), :], xbuf.at[slot], in_sems.at[slot])

        def stores(i, slot):
            own_cols = pl.ds(pl.multiple_of(my_chip * PAIR, 128), PAIR)
            return (pltpu.make_async_copy(xnbuf.at[slot], xn_hbm.at[rows_of(i), :], out_sems.at[slot, 0]),
                    pltpu.make_async_copy(xntbuf.at[slot], xnt_hbm.at[:, rows_of(i)], out_sems.at[slot, 1]),
                    pltpu.make_async_copy(pbuf.at[slot], proj_hbm.at[rows_of(i), own_cols], out_sems.at[slot, 2]))

        load(0, 0).start()
        fetch.wait()

        def prep(i, carry):
            slot = lax.rem(i, 2)
            load(i, slot).wait()

            @pl.when(i + 1 < nprep)
            def _():
                load(i + 1, 1 - slot).start()

            @pl.when(i >= 2)
            def _():
                for cp in stores(i - 2, slot):
                    cp.wait()

            xv = xbuf[slot]
            xn = (xv * lax.rsqrt(jnp.mean(xv * xv, axis=-1, keepdims=True) + EPS)) * nw_ref[...]
            xn_b = xn.astype(BF16)
            xnbuf[slot] = xn_b
            xntbuf[slot] = xn.T.astype(BF16)
            pbuf[slot] = _mm(xn_b, w_own[...])
            for cp in stores(i, slot):
                cp.start()
            return carry

        lax.fori_loop(0, nprep, prep, 0)
        for i in range(max(nprep - 2, 0), nprep):
            for cp in stores(i, i % 2):
                cp.wait()

        passed = []
        for j, chip in enumerate(chips):
            copy(1 + j, (*chip, c), me).wait_recv()
            cp = copy(4 + j, (*chip, c), sibling, src=place(*chip, c))
            cp.start()
            passed.append(cp)
        for j, chip in enumerate(chips):
            copy(4 + j, (*chip, 1 - c), me).wait_recv()
        for cp in first + passed:
            cp.wait_send()

    vm = pl.BlockSpec(memory_space=pltpu.VMEM)
    hbm = pl.BlockSpec(memory_space=pl.ANY)
    return pl.pallas_call(
        body,
        out_shape=[jax.ShapeDtypeStruct((NDEV // 2, D, PAIR), BF16), jax.ShapeDtypeStruct((t, D), BF16),
                   jax.ShapeDtypeStruct((D, t), BF16), jax.ShapeDtypeStruct((t, DIN), F32)],
        in_specs=[vm, hbm, vm],
        out_specs=[hbm, hbm, hbm, hbm],
        scratch_shapes=[pltpu.VMEM((half, IN_SHARD), BF16), pltpu.VMEM((half, IN_SHARD), BF16),
                        pltpu.VMEM((half, PAIR), BF16),
                        pltpu.VMEM((2, prep_rows, D), F32), pltpu.VMEM((2, prep_rows, D), BF16),
                        pltpu.VMEM((2, D, prep_rows), BF16),
                        pltpu.VMEM((D, PAIR), BF16), pltpu.VMEM((2, prep_rows, PAIR), F32),
                        pltpu.SemaphoreType.DMA((NDEV - 1,)), pltpu.SemaphoreType.DMA((NDEV - 1,)),
                        pltpu.SemaphoreType.DMA, pltpu.SemaphoreType.DMA((2,)),
                        pltpu.SemaphoreType.DMA((2,)), pltpu.SemaphoreType.DMA((2, 3)), pltpu.SemaphoreType.DMA],
        name="gather_in_projection", compiler_params=_cp(56),
    )(w_in_s, x2d, norm_w)


def _gather_square(shards, after):
    n = len(shards)

    def launch(*refs):
        ins, outs = refs[:n], refs[n + 1:2 * n + 1]
        send_sems, recv_sems, loc_sems = refs[2 * n + 1:]
        (x, y, c), _, _ = _place()
        me = _dev_index(x, y, c)
        peers = [(1 - x if r & 4 else x, 1 - y if r & 2 else y, 1 - c if r & 1 else c) for r in range(1, NDEV)]
        barrier = pltpu.get_barrier_semaphore()
        for peer in peers:
            pl.semaphore_signal(barrier, inc=1, device_id=peer, device_id_type=MESH)
        pl.semaphore_wait(barrier, NDEV - 1)
        local = [pltpu.make_async_copy(ins[k], outs[k].at[me], loc_sems.at[k]) for k in range(n)]
        copies = [pltpu.make_async_remote_copy(
            src_ref=ins[k], dst_ref=outs[k].at[me], send_sem=send_sems.at[r, k], recv_sem=recv_sems.at[r, k],
            device_id=peer, device_id_type=MESH) for r, peer in enumerate(peers) for k in range(n)]
        for cp in local + copies:
            cp.start()
        for r, peer in enumerate(peers):
            for k in range(n):
                pltpu.make_async_remote_copy(
                    src_ref=ins[k], dst_ref=outs[k].at[_dev_index(*peer)], send_sem=send_sems.at[r, k],
                    recv_sem=recv_sems.at[r, k], device_id=peer, device_id_type=MESH).wait_recv()
        for cp in copies:
            cp.wait_send()
        for cp in local:
            cp.wait()

    return pl.kernel(
        launch, out_type=[jax.ShapeDtypeStruct((NDEV,) + a.shape, a.dtype) for a in shards],
        mesh=plsc.ScalarSubcoreMesh(axis_name="sequencer", num_cores=1), name="gather_square",
        scratch_types=(pltpu.SemaphoreType.DMA((NDEV - 1, n)), pltpu.SemaphoreType.DMA((NDEV - 1, n)),
                       pltpu.SemaphoreType.DMA((n,))),
        compiler_params=pltpu.CompilerParams(collective_id=2),
    )(*shards, after)


def _exchange_pair(arrs, after):
    n = len(arrs)

    def launch(*refs):
        ins, got = refs[:n], refs[n + 1:2 * n + 1]
        send_sems, recv_sems = refs[2 * n + 1:]
        (x, y, c), sibling, _ = _place()
        barrier = pltpu.get_barrier_semaphore()
        pl.semaphore_signal(barrier, inc=1, device_id=sibling, device_id_type=MESH)
        pl.semaphore_wait(barrier, 1)
        sends = [pltpu.make_async_remote_copy(
            src_ref=ins[k].at[_dev_index(q // 2, q % 2, 1 - c)], dst_ref=got[k].at[q], send_sem=send_sems.at[q, k],
            recv_sem=recv_sems.at[q, k], device_id=sibling, device_id_type=MESH) for q in range(4) for k in range(n)]
        for cp in sends:
            cp.start()
        for cp in sends:
            cp.wait_recv()
        for cp in sends:
            cp.wait_send()

    return pl.kernel(
        launch, out_type=[jax.ShapeDtypeStruct((4,) + a.shape[1:], a.dtype) for a in arrs],
        mesh=plsc.ScalarSubcoreMesh(axis_name="sequencer", num_cores=1), name="exchange_pair",
        scratch_types=(pltpu.SemaphoreType.DMA((4, n)), pltpu.SemaphoreType.DMA((4, n))),
        compiler_params=pltpu.CompilerParams(collective_id=0),
    )(*arrs, after)


def _pair_sum(full, got, core, rows, name):
    _, nr, nc = got.shape

    def body(core_ref, a_ref, b_ref, o_ref):
        o_ref[...] = (a_ref[...].astype(F32) + b_ref[...].astype(F32)).astype(BF16)

    blk = pl.BlockSpec((1, rows, nc), lambda q, i, core_ref: (q, i, 0))
    return pl.pallas_call(
        body,
        grid_spec=pltpu.PrefetchScalarGridSpec(
            num_scalar_prefetch=1, grid=(4, nr // rows),
            in_specs=[pl.BlockSpec((1, rows, nc), lambda q, i, core_ref: (2 * q + core_ref[0], i, 0)), blk],
            out_specs=blk),
        out_shape=jax.ShapeDtypeStruct(got.shape, BF16), name=name,
    )(core, full, got)


def _exchange_chips(sums):
    n = len(sums)

    def launch(*refs):
        ins, outs = refs[:n], refs[n:2 * n]
        send_sems, recv_sems = refs[2 * n:]
        (x, y, c), _, chips = _place()
        barrier = pltpu.get_barrier_semaphore()
        for px, py in chips:
            pl.semaphore_signal(barrier, inc=1, device_id=(px, py, c), device_id_type=MESH)
        pl.semaphore_wait(barrier, len(chips))
        copies = [pltpu.make_async_remote_copy(
            src_ref=ins[k].at[2 * px + py], dst_ref=outs[k].at[j], send_sem=send_sems.at[j, k],
            recv_sem=recv_sems.at[j, k], device_id=(px, py, c), device_id_type=MESH)
            for j, (px, py) in enumerate(chips) for k in range(n)]
        for cp in copies:
            cp.start()
        for cp in copies:
            cp.wait_recv()
        for cp in copies:
            cp.wait_send()

    return pl.kernel(
        launch, out_type=[jax.ShapeDtypeStruct((3,) + a.shape[1:], a.dtype) for a in sums],
        mesh=plsc.ScalarSubcoreMesh(axis_name="sequencer", num_cores=1), name="exchange_chips",
        scratch_types=(pltpu.SemaphoreType.DMA((3, n)), pltpu.SemaphoreType.DMA((3, n))),
        compiler_params=pltpu.CompilerParams(collective_id=1),
    )(*sums)


def _exchange_square(partials):
    n = len(partials)

    def launch(*refs):
        ins, outs = refs[:n], refs[n:2 * n]
        send_sems, recv_sems, loc_sems = refs[2 * n:]
        (x, y, c), _, _ = _place()
        me = _dev_index(x, y, c)
        peers = [(1 - x if r & 4 else x, 1 - y if r & 2 else y, 1 - c if r & 1 else c) for r in range(1, NDEV)]
        barrier = pltpu.get_barrier_semaphore()
        for peer in peers:
            pl.semaphore_signal(barrier, inc=1, device_id=peer, device_id_type=MESH)
        pl.semaphore_wait(barrier, NDEV - 1)
        local = [pltpu.make_async_copy(ins[k].at[me], outs[k].at[me], loc_sems.at[k]) for k in range(n)]
        copies = [pltpu.make_async_remote_copy(
            src_ref=ins[k].at[_dev_index(*peer)], dst_ref=outs[k].at[me], send_sem=send_sems.at[r, k],
            recv_sem=recv_sems.at[r, k], device_id=peer, device_id_type=MESH)
            for r, peer in enumerate(peers) for k in range(n)]
        for cp in local + copies:
            cp.start()
        for r, peer in enumerate(peers):
            for k in range(n):
                pltpu.make_async_remote_copy(
                    src_ref=ins[k].at[me], dst_ref=outs[k].at[_dev_index(*peer)], send_sem=send_sems.at[r, k],
                    recv_sem=recv_sems.at[r, k], device_id=peer, device_id_type=MESH).wait_recv()
        for cp in copies:
            cp.wait_send()
        for cp in local:
            cp.wait()

    return pl.kernel(
        launch, out_type=[jax.ShapeDtypeStruct(a.shape, a.dtype) for a in partials],
        mesh=plsc.ScalarSubcoreMesh(axis_name="sequencer", num_cores=1), name="exchange_square",
        scratch_types=(pltpu.SemaphoreType.DMA((NDEV - 1, n)), pltpu.SemaphoreType.DMA((NDEV - 1, n)),
                       pltpu.SemaphoreType.DMA((n,))),
        compiler_params=pltpu.CompilerParams(collective_id=3),
    )(*partials)


def _exchange_small(small):
    def launch(sm_ref, out_ref, send_sems, recv_sems, loc_sem):
        (x, y, c), _, _ = _place()
        me = _dev_index(x, y, c)
        peers = [(1 - x if r & 4 else x, 1 - y if r & 2 else y, 1 - c if r & 1 else c) for r in range(1, NDEV)]
        barrier = pltpu.get_barrier_semaphore()
        for peer in peers:
            pl.semaphore_signal(barrier, inc=1, device_id=peer, device_id_type=MESH)
        pl.semaphore_wait(barrier, NDEV - 1)
        local = pltpu.make_async_copy(sm_ref, out_ref.at[me], loc_sem)
        local.start()
        copies = [pltpu.make_async_remote_copy(
            src_ref=sm_ref, dst_ref=out_ref.at[me], send_sem=send_sems.at[r], recv_sem=recv_sems.at[r],
            device_id=peer, device_id_type=MESH) for r, peer in enumerate(peers)]
        for cp in copies:
            cp.start()
        for r, peer in enumerate(peers):
            pltpu.make_async_remote_copy(
                src_ref=sm_ref, dst_ref=out_ref.at[_dev_index(*peer)], send_sem=send_sems.at[r], recv_sem=recv_sems.at[r],
                device_id=peer, device_id_type=MESH).wait_recv()
        for cp in copies:
            cp.wait_send()
        local.wait()

    return pl.kernel(
        launch, out_type=jax.ShapeDtypeStruct((NDEV,) + small.shape, F32),
        mesh=plsc.ScalarSubcoreMesh(axis_name="sequencer", num_cores=1), name="exchange_small",
        scratch_types=(pltpu.SemaphoreType.DMA((NDEV - 1,)), pltpu.SemaphoreType.DMA((NDEV - 1,)), pltpu.SemaphoreType.DMA),
        compiler_params=pltpu.CompilerParams(collective_id=4),
    )(small)


def _inproj_fwd(xn, w_pairs, proj, chip, tb):
    t = xn.shape[0]
    nblk, _, nb = w_pairs.shape

    def body(chip_ref, xn_ref, w_ref, proj_in, proj_ref):
        proj_ref[...] = _mm(xn_ref[...], w_ref[0])

    def other(j, chip_ref):
        return j + (j >= chip_ref[0]).astype(jnp.int32)

    return pl.pallas_call(
        body,
        grid_spec=pltpu.PrefetchScalarGridSpec(
            num_scalar_prefetch=1, grid=(t // tb, nblk - 1),
            in_specs=[pl.BlockSpec((tb, D), lambda i, j, chip_ref: (i, 0)),
                      pl.BlockSpec((1, D, nb), lambda i, j, chip_ref: (other(j, chip_ref), 0, 0)),
                      pl.BlockSpec(memory_space=pl.ANY)],
            out_specs=pl.BlockSpec((tb, nb), lambda i, j, chip_ref: (i, other(j, chip_ref)))),
        out_shape=jax.ShapeDtypeStruct((t, DIN), F32),
        input_output_aliases={3: 0},
        name="inproj_fwd", compiler_params=_cp(56),
    )(chip, xn, w_pairs, proj)


def _lower_bound(lb_ref):
    a0 = lb_ref[0:1, :]
    a1 = lb_ref[1:2, :]
    mx = jnp.maximum(a0, a1)
    e0 = jnp.exp(a0 - mx)
    e1 = jnp.exp(a1 - mx)
    return e0 / (e0 + e1)


def _hgrn_chunk_fwd(hq, hf, lb, tril):
    sg = _sigmoid(hf)
    f = lb + (1.0 - lb) * sg
    g = jnp.log(f)
    k = 1.0 - f
    sq = _sigmoid(hq)
    q = hq * sq
    b = _mm_tri_exact(tril, g)
    last_row = lax.broadcasted_iota(jnp.int32, b.shape, 0) == CH - 1
    b_last = jnp.sum(jnp.where(last_row, b, 0.0), axis=0, keepdims=True)
    c = 0.5 * b_last
    eb = jnp.exp(b)
    ea = jnp.exp(b - c)
    ek = jnp.exp(c - b)
    ed = jnp.exp(b_last - b)
    ebl = jnp.exp(b_last)
    return dict(sg=sg, f=f, k=k, sq=sq, q=q, eb=eb, ea=ea, ek=ek, ed=ed, ebl=ebl,
                qe=q * eb, qa=q * ea, ka=k * ek, kd=k * ed)


def _tri(lower):
    r = lax.broadcasted_iota(jnp.int32, (CH, CH), 0)
    c = lax.broadcasted_iota(jnp.int32, (CH, CH), 1)
    return (r >= c) if lower else (c >= r)


def _head_segment(p_ref, rows, j, hg):
    return p_ref[rows, j * HD * hg:(j + 1) * HD * hg]


def _head(a, k):
    return a[:, k * HD:(k + 1) * HD]


def _hgrn_fwd(proj, lbw, rb, hg):
    assert hg == HEADS
    t = proj.shape[0]
    ncb = rb // CH

    def body(p_ref, lb_ref, o_ref, st_ref, s_scr):
        @pl.when(pl.program_id(1) == 0)
        def _():
            s_scr[...] = jnp.zeros_like(s_scr)

        lb = _lower_bound(lb_ref)
        causal = _tri(True)
        tril = _tri3(True)
        heads = range(hg)

        def chunk(cc, carry):
            r0 = pl.multiple_of(cc * CH, CH)
            rows = pl.ds(r0, CH)
            e = _hgrn_chunk_fwd(_head_segment(p_ref, rows, 0, hg), _head_segment(p_ref, rows, 1, hg), lb, tril)
            v = _bf(_head_segment(p_ref, rows, 2, hg))
            sts = [s_scr[k] for k in heads]
            qa, ka, qe, kd = _bf(e["qa"]), _bf(e["ka"]), _bf(e["qe"]), _bf(e["kd"])
            a = [_bf(jnp.where(causal, _mm_nt(_head(qa, k), _head(ka, k)), 0.0)) for k in heads]
            o_inter = [_mm_nt(_head(qe, k), _bf(sts[k])) for k in heads]
            kv = [_mm_tn(_head(v, k), _head(kd, k)) for k in heads]
            o_intra = [_mm(a[k], _head(v, k)) for k in heads]
            for k in heads:
                st_ref[cc, k] = sts[k]
                o_ref[rows, k * HD:(k + 1) * HD] = o_inter[k] + o_intra[k]
                s_scr[k] = sts[k] * _head(e["ebl"], k) + kv[k]
            return carry

        lax.fori_loop(0, ncb, chunk, 0, unroll=4)

    return pl.pallas_call(
        body, grid=(HEADS // hg, t // rb),
        in_specs=[pl.BlockSpec((rb, 3 * HD * hg), lambda h, i: (i, h)), pl.BlockSpec((2, HD * hg), lambda h, i: (0, h))],
        out_specs=[pl.BlockSpec((rb, HD * hg), lambda h, i: (i, h)),
                   pl.BlockSpec((ncb, hg, HD, HD), lambda h, i: (i, h, 0, 0))],
        out_shape=[jax.ShapeDtypeStruct((t, D), F32), jax.ShapeDtypeStruct((t // CH, HEADS, HD, HD), F32)],
        scratch_shapes=[pltpu.VMEM((hg, HD, HD), F32)],
        name="hgrn_fwd", compiler_params=_cp(48),
    )(proj, lbw)


def _hgrn_bwd(proj, lbw, do_raw, states, rb, hg):
    assert hg == HEADS
    t = proj.shape[0]
    nblk = t // rb
    ncb = rb // CH
    wd = HD * hg

    def body(p_ref, lb_ref, do_ref, st_ref, dp_ref, dlb_ref, ds_scr):
        @pl.when(pl.program_id(1) == 0)
        def _():
            ds_scr[...] = jnp.zeros_like(ds_scr)
            dlb_ref[...] = jnp.zeros_like(dlb_ref)

        lb = _lower_bound(lb_ref)
        causal = _tri(True)
        tril = _tri3(True)
        triu = _tri3(False)
        last_row = lax.broadcasted_iota(jnp.int32, (CH, HD * hg), 0) == CH - 1
        row0 = lax.broadcasted_iota(jnp.int32, (8, HD * hg), 0) == 0
        heads = range(hg)
        wide = lambda parts: jnp.concatenate(parts, axis=1)

        def chunk(it, carry):
            cc = ncb - 1 - it
            r0 = pl.multiple_of(cc * CH, CH)
            rows = pl.ds(r0, CH)
            hq = _head_segment(p_ref, rows, 0, hg)
            e = _hgrn_chunk_fwd(hq, _head_segment(p_ref, rows, 1, hg), lb, tril)
            v = _bf(_head_segment(p_ref, rows, 2, hg))
            do = _bf(do_ref[rows, :])
            sts = [st_ref[cc, k] for k in heads]
            dsts = [ds_scr[k] for k in heads]
            dlb_acc = dlb_ref[...]
            qa, ka, qe, kd = _bf(e["qa"]), _bf(e["ka"]), _bf(e["qe"]), _bf(e["kd"])
            a = [_bf(jnp.where(causal, _mm_nt(_head(qa, k), _head(ka, k)), 0.0)) for k in heads]
            da = [_bf(jnp.where(causal, _mm_nt(_head(do, k), _head(v, k)), 0.0)) for k in heads]
            dqe = wide([_mm(_head(do, k), _bf(sts[k])) for k in heads])
            dkd = wide([_mm(_head(v, k), _bf(dsts[k])) for k in heads])
            dv_state = [_mm_nt(_head(kd, k), _bf(dsts[k])) for k in heads]
            ds_new = [_mm_tn(_head(do, k), _head(qe, k)) for k in heads]
            dv_intra = [_mm_tn(a[k], _head(do, k)) for k in heads]
            dqa = wide([_mm(da[k], _head(ka, k)) for k in heads])
            dka = wide([_mm_tn(da[k], _head(qa, k)) for k in heads])
            dv = wide([dv_intra[k] + dv_state[k] for k in heads])
            dbl = e["ebl"] * wide([jnp.sum(sts[k] * dsts[k], axis=0, keepdims=True) for k in heads])
            dq = dqe * e["eb"] + dqa * e["ea"]
            dk = dka * e["ek"] + dkd * e["ed"]
            dkd_kd = dkd * kd.astype(F32)
            db = dqe * qe.astype(F32) + dqa * qa.astype(F32) - dka * ka.astype(F32) - dkd_kd
            db = db + jnp.where(last_row, dbl + jnp.sum(dkd_kd, axis=0, keepdims=True), 0.0)
            dg = _mm_tri_exact(triu, db)
            df = dg / e["f"] - dk
            sg = e["sg"]
            sq = e["sq"]
            dhq = _bf(dq * (sq * (1.0 + hq * (1.0 - sq))))
            dhf = _bf(df * (1.0 - lb) * sg * (1.0 - sg))
            dhi = _bf(dv)
            dlb_new = dlb_acc + jnp.where(row0, jnp.sum(df * (1.0 - sg), axis=0, keepdims=True), 0.0)
            for k in heads:
                ds_scr[k] = ds_new[k] + dsts[k] * _head(e["ebl"], k)
            dp_ref[rows, 0:wd] = dhq
            dp_ref[rows, wd:2 * wd] = dhf
            dp_ref[rows, 2 * wd:3 * wd] = dhi
            dlb_ref[...] = dlb_new
            return carry

        lax.fori_loop(0, ncb, chunk, 0, unroll=2)

    rev = lambda h, i: (nblk - 1 - i, h)
    return pl.pallas_call(
        body, grid=(HEADS // hg, nblk),
        in_specs=[pl.BlockSpec((rb, 3 * HD * hg), rev), pl.BlockSpec((2, HD * hg), lambda h, i: (0, h)),
                  pl.BlockSpec((rb, HD * hg), rev), pl.BlockSpec((ncb, hg, HD, HD), lambda h, i: (nblk - 1 - i, h, 0, 0))],
        out_specs=[pl.BlockSpec((rb, 3 * HD * hg), rev), pl.BlockSpec((8, HD * hg), lambda h, i: (0, h))],
        out_shape=[jax.ShapeDtypeStruct((t, 3 * D), BF16), jax.ShapeDtypeStruct((8, D), F32)],
        scratch_shapes=[pltpu.VMEM((hg, HD, HD), F32)],
        name="hgrn_bwd", compiler_params=_cp(48),
    )(proj, lbw, do_raw, states)


def _kv_variants(tile, odd):
    low = lax.broadcasted_iota(jnp.int32, tile.shape, 1) < 64
    if odd:
        hi = jnp.where(low, 0.0, tile)
        lo = pltpu.roll(hi, 64, 1)
    else:
        lo = jnp.where(low, tile, 0.0)
        hi = pltpu.roll(lo, 64, 1)
    return _bf(lo), _bf(hi)


def _attn_masks(n):
    qi = lax.broadcasted_iota(jnp.int32, (AB, AB), 0)
    kj = lax.broadcasted_iota(jnp.int32, (AB, AB), 1)
    cur = kj <= qi
    return cur, cur | (n > 0), qi <= kj


def _kv_all(prev_ref, cur_ref):
    out = []
    for tl in range(2):
        cols = slice(tl * 128, (tl + 1) * 128)
        tile = jnp.concatenate([prev_ref[:, cols], cur_ref[:, cols]], axis=0)
        out.append(_kv_variants(tile, 0))
        out.append(_kv_variants(tile, 1))
    return out


def _window(a2, cur):
    return jnp.where(cur, a2[:, AB:], a2[:, :AB])


def _attn_softmax(scores, sinks, cur, ok):
    s = [jnp.where(ok, _window(s2, cur) * ATT_SCALE, NEG) for s2 in scores]
    m = [jnp.maximum(jnp.max(si, axis=-1, keepdims=True), sink) for si, sink in zip(s, sinks)]
    p = [jnp.exp(si - mi) for si, mi in zip(s, m)]
    es = [jnp.exp(sink - mi) for sink, mi in zip(sinks, m)]
    inv = [1.0 / (jnp.sum(pi, axis=-1, keepdims=True) + ei) for pi, ei in zip(p, es)]
    return [pi * ii for pi, ii in zip(p, inv)], [ei * ii for ei, ii in zip(es, inv)]


def _spread(pc, cur):
    return jnp.concatenate([jnp.where(cur, 0.0, pc), jnp.where(cur, pc, 0.0)], axis=1)


def _spread_t(pct, cur_t):
    return jnp.concatenate([jnp.where(cur_t, 0.0, pct), jnp.where(cur_t, pct, 0.0)], axis=0)


def _attn_fwd(proj, sinks):
    t = proj.shape[0]
    nb = t // AB

    nsub = 4
    assert nb % nsub == 0

    def body(q_ref, kc_ref, kp_ref, vc_ref, vp_ref, sink_ref, o_ref):
        sinks_v = [sink_ref[0, h] for h in range(QH)]
        heads = [(j, ab) for j in range(8) for ab in range(2)]
        for sb in range(nsub):
            rows = pl.ds(AB * sb, AB)
            before = pl.ds(AB * (sb - 1), AB)
            cur, ok, _ = _attn_masks(nsub * pl.program_id(0) + sb)
            kvars = _kv_all(kp_ref if sb == 0 else kc_ref.at[before, :], kc_ref.at[rows, :])
            vvars = _kv_all(vp_ref if sb == 0 else vc_ref.at[before, :], vc_ref.at[rows, :])
            qps = [_bf(q_ref[rows, 128 * j:128 * (j + 1)]) for j in range(8)]
            scores = [_mm_nt(qps[j], kvars[j // 2][ab]) for j, ab in heads]
            pcs, _ = _attn_softmax(scores, sinks_v, cur, ok)
            parts = [_mm(_bf(_spread(pcs[h], cur)), vvars[j // 2][ab]) for h, (j, ab) in enumerate(heads)]
            for j in range(8):
                o_ref[rows, 128 * j:128 * (j + 1)] = parts[2 * j] + parts[2 * j + 1]

    prev = lambda n: jnp.maximum(nsub * n - 1, 0)
    step = nsub * AB
    return pl.pallas_call(
        body, grid=(nb // nsub,),
        in_specs=[pl.BlockSpec((step, D), lambda n: (n, C_AQ // D)),
                  pl.BlockSpec((step, 256), lambda n: (n, C_AK // 256)),
                  pl.BlockSpec((AB, 256), lambda n: (prev(n), C_AK // 256)),
                  pl.BlockSpec((step, 256), lambda n: (n, C_AV // 256)),
                  pl.BlockSpec((AB, 256), lambda n: (prev(n), C_AV // 256)),
                  pl.BlockSpec(memory_space=pltpu.SMEM)],
        out_specs=pl.BlockSpec((step, D), lambda n: (n, 0)),
        out_shape=jax.ShapeDtypeStruct((t, D), F32),
        name="attn_fwd", compiler_params=_cp(32),
    )(proj, proj, proj, proj, proj, sinks)


def _attn_bwd(proj, sinks, do_a):
    t = proj.shape[0]
    nb = t // AB
    nsub = 2
    assert nb % nsub == 0
    steps = nb // nsub
    step = nsub * AB

    def body(q_ref, kc_ref, kp_ref, vc_ref, vp_ref, do_ref, sink_ref, dq_ref, dkv_ref, dsink_ref, carry):
        n = pl.program_id(0)

        @pl.when(n == 0)
        def _():
            dsink_ref[...] = jnp.zeros_like(dsink_ref)
            carry[...] = jnp.zeros_like(carry)

        def one_block(sb):
            rows = pl.ds(AB * sb, AB)
            before = pl.ds(AB * (sb - 1), AB)
            cur, ok, cur_t = _attn_masks(nsub * n + sb)
            low = lax.broadcasted_iota(jnp.int32, (2 * AB, 128), 1) < 64
            lane = lax.broadcasted_iota(jnp.int32, (8, 128), 1)
            row0 = lax.broadcasted_iota(jnp.int32, (8, 128), 0) == 0
            kvars = _kv_all(kp_ref if sb == 0 else kc_ref.at[before, :], kc_ref.at[rows, :])
            vvars = _kv_all(vp_ref if sb == 0 else vc_ref.at[before, :], vc_ref.at[rows, :])
            qps = [_bf(q_ref[rows, 128 * j:128 * (j + 1)]) for j in range(8)]
            dops = [_bf(do_ref[rows, 128 * j:128 * (j + 1)]) for j in range(8)]
            heads = [(j, ab) for j in range(8) for ab in range(2)]
            scores = [_mm_nt(qps[j], kvars[j // 2][ab]) for j, ab in heads]
            dps = [_mm_nt(dops[j], vvars[j // 2][ab]) for j, ab in heads]
            pcs, pss = _attn_softmax(scores, [sink_ref[0, h] for h in range(QH)], cur, ok)
            dpcs = [_window(dp2, cur) for dp2 in dps]
            rss = [jnp.sum(pc * dpc, axis=-1, keepdims=True) for pc, dpc in zip(pcs, dpcs)]
            dscs = [pc * (dpc - rs) for pc, dpc, rs in zip(pcs, dpcs, rss)]
            dsink = jnp.zeros((8, 128), F32)
            for h in range(QH):
                dsink = dsink + jnp.where(row0 & (lane == h), -jnp.sum(pss[h] * rss[h]), 0.0)
            dq_terms = [_mm(_bf(_spread(dscs[h], cur)), kvars[j // 2][ab]) for h, (j, ab) in enumerate(heads)]
            for j in range(8):
                dq_ref[rows, 128 * j:128 * (j + 1)] = _bf((dq_terms[2 * j] + dq_terms[2 * j + 1]) * ATT_SCALE)
            dsc_t = [_bf(_spread_t(dsc.T, cur_t)) for dsc in dscs]
            pc_t = [_bf(_spread_t(pc.T, cur_t)) for pc in pcs]
            dk_terms = [_mm(dsc_t[h], qps[j]) for h, (j, ab) in enumerate(heads)]
            dv_terms = [_mm(pc_t[h], dops[j]) for h, (j, ab) in enumerate(heads)]
            dk_ab = [[dk_terms[4 * g + ab] + dk_terms[4 * g + 2 + ab] for ab in range(2)] for g in range(4)]
            dv_ab = [[dv_terms[4 * g + ab] + dv_terms[4 * g + 2 + ab] for ab in range(2)] for g in range(4)]
            dkts, dvts = [], []
            for tl in range(2):
                ke, ko = dk_ab[2 * tl], dk_ab[2 * tl + 1]
                ve, vo = dv_ab[2 * tl], dv_ab[2 * tl + 1]
                dkts.append((jnp.where(low, ke[0], 0.0) + pltpu.roll(jnp.where(low, 0.0, ke[1]), 64, 1)
                             + jnp.where(low, 0.0, ko[1]) + pltpu.roll(jnp.where(low, ko[0], 0.0), 64, 1)) * ATT_SCALE)
                dvts.append(jnp.where(low, ve[0], 0.0) + pltpu.roll(jnp.where(low, 0.0, ve[1]), 64, 1)
                            + jnp.where(low, 0.0, vo[1]) + pltpu.roll(jnp.where(low, vo[0], 0.0), 64, 1))
            return dkts, dvts, dsink

        @pl.when(n < steps)
        def _():
            (dk0, dv0, ds0), (dk1, dv1, ds1) = one_block(0), one_block(1)
            first, second = slice(0, AB), slice(AB, 2 * AB)
            for tl in range(2):
                for cols, g0, g1 in ((slice(tl * 128, (tl + 1) * 128), dk0[tl], dk1[tl]),
                                     (slice(256 + tl * 128, 256 + (tl + 1) * 128), dv0[tl], dv1[tl])):
                    dkv_ref[first, cols] = _bf(carry[first, cols])
                    dkv_ref[second, cols] = _bf(carry[second, cols] + g0[first])
                    carry[first, cols] = g0[second] + g1[first]
                    carry[second, cols] = g1[second]
            dsink_ref[...] += ds0 + ds1

        @pl.when(n == steps)
        def _():
            dkv_ref[...] = _bf(carry[...])

    cur = lambda n: jnp.minimum(n, steps - 1)
    last = lambda n: jnp.clip(n - 1, 0, steps - 1)
    prev = lambda n: jnp.clip(nsub * n - 1, 0, nb - 1)
    return pl.pallas_call(
        body, grid=(steps + 1,),
        in_specs=[pl.BlockSpec((step, D), lambda n: (cur(n), C_AQ // D)),
                  pl.BlockSpec((step, 256), lambda n: (cur(n), C_AK // 256)),
                  pl.BlockSpec((AB, 256), lambda n: (prev(n), C_AK // 256)),
                  pl.BlockSpec((step, 256), lambda n: (cur(n), C_AV // 256)),
                  pl.BlockSpec((AB, 256), lambda n: (prev(n), C_AV // 256)),
                  pl.BlockSpec((step, D), lambda n: (cur(n), 0)),
                  pl.BlockSpec(memory_space=pltpu.SMEM)],
        out_specs=[pl.BlockSpec((step, D), lambda n: (cur(n), 0)),
                   pl.BlockSpec((step, 512), lambda n: (last(n), 0)),
                   pl.BlockSpec((8, 128), lambda n: (0, 0))],
        out_shape=[jax.ShapeDtypeStruct((t, D), BF16), jax.ShapeDtypeStruct((t, 512), BF16),
                   jax.ShapeDtypeStruct((8, 128), F32)],
        scratch_shapes=[pltpu.VMEM((step, 512), F32)],
        name="attn_bwd", compiler_params=_cp(40),
    )(proj, proj, proj, proj, proj, do_a, sinks)


def _silu_and_grad(v):
    s = _sigmoid(v)
    sil = v * s
    return sil, s + sil * (1.0 - s)


def _tail(o_raw, o_a, proj, x2d, tgt, wbh, wba, wout, hnw, fnw, tb):
    t = x2d.shape[0]

    def body(or_ref, oa_ref, hg_ref, ag0, ag1, mh0, mh1, ma0, ma1, x_ref, t_ref, wbh_ref, wba_ref, wout_ref, hnw_ref,
             fnw_ref, dx2_ref, dor_ref, doa_ref, dhg_ref, dagm_ref, gh_ref, ga_ref, mg_ref, dyh_ref, dya_ref, dx2b_ref,
             sums_ref):
        @pl.when(pl.program_id(0) == 0)
        def _():
            sums_ref[...] = jnp.zeros_like(sums_ref)

        hr = tb // ROW_PARTS
        parts = [pl.ds(k * hr, hr) for k in range(ROW_PARTS)]
        rows_of = lambda v, k: v[k * hr:(k + 1) * hr]
        whole = lambda vs: jnp.concatenate(vs, axis=0)
        halves = lambda a, b, r: jnp.concatenate([a[r, :], b[r, :]], axis=1)
        hnw_v = hnw_ref[...]
        fnw_v = fnw_ref[...]
        sv = []
        for r in parts:
            o = or_ref[r, :]
            rs, xhs = [], []
            for h in range(HEADS):
                oh = o[:, h * HD:(h + 1) * HD]
                rn = lax.rsqrt(jnp.mean(oh * oh, axis=-1, keepdims=True) + EPS)
                rs.append(rn)
                xhs.append(oh * rn)
            xh = jnp.concatenate(xhs, axis=1)
            on = xh * hnw_v
            sil_hg, dsil_hg = _silu_and_grad(hg_ref[r, :])
            oa = oa_ref[r, :]
            sil_ag, dsil_ag = _silu_and_grad(halves(ag0, ag1, r))
            sv.append(dict(rs=rs, xhs=xhs, xh=xh, on=on, sil_hg=sil_hg, dsil_hg=dsil_hg, oa=oa, sil_ag=sil_ag,
                           dsil_ag=dsil_ag, gh_b=_bf(on * sil_hg), ga_b=_bf(oa * sil_ag)))
        gh_b = whole([s["gh_b"] for s in sv])
        ga_b = whole([s["ga_b"] for s in sv])
        y_h = _mm(gh_b, wbh_ref[...])
        y_a = _mm(ga_b, wba_ref[...])
        for k, r in enumerate(parts):
            s = sv[k]
            s["s_mh"] = _sigmoid(halves(mh0, mh1, r))
            s["s_ma"] = _sigmoid(halves(ma0, ma1, r))
            s["y_h"], s["y_a"] = rows_of(y_h, k), rows_of(y_a, k)
            s["mg_b"] = _bf(s["s_mh"] * s["y_h"] + s["s_ma"] * s["y_a"])
        mg_b = whole([s["mg_b"] for s in sv])
        dx_out = _mm(mg_b, wout_ref[...])
        loss = 0.0
        dfnw = 0.0
        for k, r in enumerate(parts):
            x2 = x_ref[r, :] + rows_of(dx_out, k)
            r2 = lax.rsqrt(jnp.mean(x2 * x2, axis=-1, keepdims=True) + EPS)
            xh2 = x2 * r2
            err = xh2 * fnw_v - t_ref[r, :]
            loss = loss + 0.5 * jnp.sum(jnp.mean(err * err, axis=-1, keepdims=True))
            dfnw = dfnw + jnp.sum(err * xh2, axis=0, keepdims=True) * (1.0 / D)
            dxh2 = err * fnw_v
            dx2 = (r2 * (1.0 / D)) * (dxh2 - xh2 * jnp.mean(dxh2 * xh2, axis=-1, keepdims=True))
            dx2_ref[r, :] = dx2
            sv[k]["dx2_b"] = _bf(dx2)
        dx2_b = whole([s["dx2_b"] for s in sv])
        dmg = _mm_nt(dx2_b, wout_ref[...])
        for k, r in enumerate(parts):
            s = sv[k]
            dmg_h = rows_of(dmg, k) * s["s_mh"]
            dmg_a = rows_of(dmg, k) * s["s_ma"]
            s["dyh_b"], s["dya_b"] = _bf(dmg_h), _bf(dmg_a)
            dagm_ref[r, D:2 * D] = _bf(dmg_h * s["y_h"] * (1.0 - s["s_mh"]))
            dagm_ref[r, 2 * D:3 * D] = _bf(dmg_a * s["y_a"] * (1.0 - s["s_ma"]))
        dyh_b = whole([s["dyh_b"] for s in sv])
        dya_b = whole([s["dya_b"] for s in sv])
        dgh_all = _mm_nt(dyh_b, wbh_ref[...])
        dga_all = _mm_nt(dya_b, wba_ref[...])
        dhnw = 0.0
        for k, r in enumerate(parts):
            s = sv[k]
            dgh, dga = rows_of(dgh_all, k), rows_of(dga_all, k)
            doa_ref[r, :] = dga * s["sil_ag"]
            dagm_ref[r, 0:D] = _bf(dga * s["oa"] * s["dsil_ag"])
            dhg_ref[r, :] = _bf(dgh * s["on"] * s["dsil_hg"])
            don = dgh * s["sil_hg"]
            dhnw = dhnw + jnp.sum(don * s["xh"], axis=0, keepdims=True)
            dxh = don * hnw_v
            dos = []
            for h in range(HEADS):
                sl = slice(h * HD, (h + 1) * HD)
                dos.append(s["rs"][h] * (dxh[:, sl] - s["xhs"][h]
                                         * jnp.mean(dxh[:, sl] * s["xhs"][h], axis=-1, keepdims=True)))
            dor_ref[r, :] = jnp.concatenate(dos, axis=1)
        gh_ref[...] = gh_b
        ga_ref[...] = ga_b
        mg_ref[...] = mg_b
        dyh_ref[...] = dyh_b
        dya_ref[...] = dya_b
        dx2b_ref[...] = dx2_b
        row = lax.broadcasted_iota(jnp.int32, (8, D), 0)
        sums_ref[...] += jnp.where(row == 0, dfnw, 0.0) + jnp.where(row == 1, dhnw, 0.0) + jnp.where(row == 2, loss, 0.0)

    rowblk = lambda c: pl.BlockSpec((tb, D), lambda i: (i, c))
    half = lambda c: pl.BlockSpec((tb, 512), lambda i: (i, c))
    full = lambda shape: pl.BlockSpec(shape, lambda i: (0, 0))
    return pl.pallas_call(
        body, grid=(t // tb,),
        in_specs=[rowblk(0), rowblk(0), rowblk(C_HG // D), half(C_AG // 512), half(C_AG // 512 + 1), half(C_MH // 512),
                  half(C_MH // 512 + 1), half(C_MA // 512), half(C_MA // 512 + 1), rowblk(0), rowblk(0),
                  full((D, D)), full((D, D)), full((D, D)), full((1, D)), full((1, D))],
        out_specs=[rowblk(0), rowblk(0), rowblk(0), rowblk(0), pl.BlockSpec((tb, 3 * D), lambda i: (i, 0))]
        + [rowblk(0)] * 6 + [full((8, D))],
        out_shape=[jax.ShapeDtypeStruct((t, D), F32)] * 3
        + [jax.ShapeDtypeStruct((t, D), BF16), jax.ShapeDtypeStruct((t, 3 * D), BF16)]
        + [jax.ShapeDtypeStruct((t, D), BF16)] * 6 + [jax.ShapeDtypeStruct((8, D), F32)],
        name="tail", compiler_params=_cp(56),
    )(o_raw, o_a, proj, proj, proj, proj, proj, proj, proj, x2d, tgt, wbh, wba, wout, hnw, fnw)


def _wgrad3(gh, dyh, ga, dya, mg, dx2b, tk, after):
    t = dyh.shape[0]

    def body(a0, b0, a1, b1, a2, b2, _, o0, o1, o2):
        @pl.when(pl.program_id(0) == 0)
        def _():
            o0[...] = jnp.zeros_like(o0)
            o1[...] = jnp.zeros_like(o1)
            o2[...] = jnp.zeros_like(o2)

        o0[...] += _mm_tn(a0[...], b0[...])
        o1[...] += _mm_tn(a1[...], b1[...])
        o2[...] += _mm_tn(a2[...], b2[...])

    blk = pl.BlockSpec((tk, D), lambda k: (k, 0))
    out = pl.BlockSpec((D, D), lambda k: (0, 0))
    return pl.pallas_call(
        body, grid=(t // tk,), in_specs=[blk] * 6 + [pl.BlockSpec(memory_space=pl.ANY)], out_specs=[out] * 3,
        out_shape=[jax.ShapeDtypeStruct((D, D), F32)] * 3,
        name="wgrad3", compiler_params=_cp(48),
    )(gh, dyh, ga, dya, mg, dx2b, after)


def _inproj_wgrad(xnt, pieces, nb, col0, dw, name):
    t = xnt.shape[1]
    counts = [p.shape[1] // nb for p in pieces]
    firsts = [sum(counts[:k]) for k in range(len(pieces))]
    assert col0 % nb == 0

    def body(xnt_ref, *refs):
        o_ref = refs[-1]
        j = pl.program_id(0)
        for first, count, p_ref in zip(firsts, counts, refs):
            @pl.when((j >= first) & (j < first + count))
            def _(p_ref=p_ref):
                o_ref[...] = _mm(xnt_ref[...], p_ref[...]).astype(BF16)

    def piece_spec(first, count):
        return pl.BlockSpec((t, nb), lambda j: (0, jnp.clip(j - first, 0, count - 1)))

    carried = [] if dw is None else [dw]
    return pl.pallas_call(
        body, grid=(sum(counts),),
        in_specs=[pl.BlockSpec((D, t), lambda j: (0, 0), pipeline_mode=pl.Buffered(1))]
        + [piece_spec(f, c) for f, c in zip(firsts, counts)] + [pl.BlockSpec(memory_space=pl.ANY)] * len(carried),
        out_specs=pl.BlockSpec((D, nb), lambda j: (0, j + col0 // nb)),
        out_shape=jax.ShapeDtypeStruct((D, DIN), BF16),
        input_output_aliases={1 + len(pieces): 0} if carried else {},
        name=name, compiler_params=_cp(56),
    )(xnt, *pieces, *carried)


def _inproj_dgrad(pieces, w_p, x2d, dx2, norm_w, tb, after):
    t = x2d.shape[0]

    def body(*refs):
        piece_refs = refs[:len(pieces)]
        w_ref, x_ref, dx2_ref, nw_ref, _, gx_ref, dnw_ref = refs[len(pieces):]

        @pl.when(pl.program_id(0) == 0)
        def _():
            dnw_ref[...] = jnp.zeros_like(dnw_ref)

        dxn = None
        off = 0
        for p in piece_refs:
            width = p.shape[1]
            for q in range(w_ref.shape[0]):
                lo, hi = max(off, q * PAIR), min(off + width, (q + 1) * PAIR)
                if lo < hi:
                    term = _mm_nt(p[:, lo - off:hi - off], w_ref[q, :, lo - q * PAIR:hi - q * PAIR])
                    dxn = term if dxn is None else dxn + term
            off += width
        xv = x_ref[...]
        r = lax.rsqrt(jnp.mean(xv * xv, axis=-1, keepdims=True) + EPS)
        xh = xv * r
        dxh = dxn * nw_ref[...]
        gx_ref[...] = dx2_ref[...] + r * (dxh - xh * jnp.mean(dxh * xh, axis=-1, keepdims=True))
        row0 = lax.broadcasted_iota(jnp.int32, (8, D), 0) == 0
        dnw_ref[...] += jnp.where(row0, jnp.sum(dxn * xh, axis=0, keepdims=True), 0.0)

    rowblk = pl.BlockSpec((tb, D), lambda i: (i, 0))
    return pl.pallas_call(
        body, grid=(t // tb,),
        in_specs=[pl.BlockSpec((tb, p.shape[1]), lambda i: (i, 0)) for p in pieces]
        + [pl.BlockSpec(w_p.shape, lambda i: (0, 0, 0), pipeline_mode=pl.Buffered(1)), rowblk, rowblk,
           pl.BlockSpec((1, D), lambda i: (0, 0)), pl.BlockSpec(memory_space=pl.ANY)],
        out_specs=[rowblk, pl.BlockSpec((8, D), lambda i: (0, 0))],
        out_shape=[jax.ShapeDtypeStruct((t, D), F32), jax.ShapeDtypeStruct((8, D), F32)],
        name="inproj_dgrad", compiler_params=_cp(60),
    )(*pieces, w_p, x2d, dx2, norm_w, after)


def _adamw_math(w, g, m, v):
    m = B1 * m + (1.0 - B1) * g
    v = B2 * v + (1.0 - B2) * (g * g)
    m_hat = m / (1.0 - B1 ** STEP)
    v_hat = v / (1.0 - B2 ** STEP)
    delta = -LR * (m_hat / (jnp.sqrt(v_hat) + ADAM_EPS) + WD * w)
    return delta, m, v


def _adamw_shard(recv, sums, chip, w, m, v, rows, name):
    nparts, nr, nc = recv.shape

    def body(chip_ref, own_ref, p_ref, w_ref, m_ref, v_ref, g_ref, d_ref, nm_ref, nv_ref):
        g = own_ref[0].astype(F32)
        for s in range(nparts):
            g = g + p_ref[s].astype(F32)
        d, nm, nv = _adamw_math(w_ref[...], g, m_ref[...], v_ref[...])
        g_ref[...] = g
        d_ref[...] = d
        nm_ref[...] = nm
        nv_ref[...] = nv

    blk = pl.BlockSpec((rows, nc), lambda i, chip_ref: (i, 0))
    return pl.pallas_call(
        body,
        grid_spec=pltpu.PrefetchScalarGridSpec(
            num_scalar_prefetch=1, grid=(nr // rows,),
            in_specs=[pl.BlockSpec((1, rows, nc), lambda i, chip_ref: (chip_ref[0], i, 0)),
                      pl.BlockSpec((nparts, rows, nc), lambda i, chip_ref: (0, i, 0)), blk, blk, blk],
            out_specs=[blk] * 4),
        out_shape=[jax.ShapeDtypeStruct((nr, nc), F32)] * 4,
        name=name, compiler_params=_cp(48),
    )(chip, sums, recv, w, m, v)


def _adamw_sum8(parts, w, m, v, after, name):
    def body(p_ref, w_ref, m_ref, v_ref, _, g_ref, d_ref, nm_ref, nv_ref):
        g = p_ref[0].astype(F32)
        for s in range(1, NDEV):
            g = g + p_ref[s].astype(F32)
        d, nm, nv = _adamw_math(w_ref[...], g, m_ref[...], v_ref[...])
        g_ref[...] = g
        d_ref[...] = d
        nm_ref[...] = nm
        nv_ref[...] = nv

    vm = pl.BlockSpec(memory_space=pltpu.VMEM)
    return pl.pallas_call(
        body, out_shape=[jax.ShapeDtypeStruct(w.shape, F32)] * 4,
        in_specs=[vm, vm, vm, vm, pl.BlockSpec(memory_space=pl.ANY)], out_specs=[vm] * 4, name=name,
    )(parts, w, m, v, after)


SMALL_ROWS = dict(norm_w=0, lower_bound=1, hgrn_norm_w=3, final_norm_w=4, sinks=5, loss=6)


def _pack_small_grads(dnw, dlb, sums, dsink):
    def body(dnw_ref, dlb_ref, sums_ref, dsink_ref, o_ref):
        o_ref[...] = jnp.zeros_like(o_ref)
        o_ref[0:1, :] = dnw_ref[0:1, :]
        o_ref[1:2, :] = dlb_ref[0:1, :]
        o_ref[3:4, :] = sums_ref[1:2, :]
        o_ref[4:5, :] = sums_ref[0:1, :]
        o_ref[5:6, 0:128] = dsink_ref[0:1, :]
        o_ref[6:7, :] = sums_ref[2:3, :]

    return pl.pallas_call(body, out_shape=jax.ShapeDtypeStruct((8, D), F32), name="pack_small_grads")(dnw, dlb, sums, dsink)


def _adamw_small(parts, ws, ms, vs):
    shapes = [a.shape for a in ws]

    def body(p_ref, *refs):
        w, m, v = refs[0:5], refs[5:10], refs[10:15]
        outs = [refs[15 + 5 * i:20 + 5 * i] for i in range(4)]
        loss_ref = refs[35]

        def total(row, width):
            g = p_ref[0, row:row + 1, 0:width]
            for s in range(1, NDEV):
                g = g + p_ref[s, row:row + 1, 0:width]
            return g

        loss_ref[...] = total(6, 128)
        lb = _lower_bound(w[1])
        ga0 = total(1, D) * lb * (1.0 - lb)
        grads = [total(0, D), None, total(3, D), total(4, D), total(5, QH)]
        for i in (0, 2, 3, 4):
            res = (grads[i],) + _adamw_math(w[i][...], grads[i], m[i][...], v[i][...])
            for o, val in zip(outs, res):
                o[i][...] = val
        for r, g in ((0, ga0), (1, -ga0)):
            res = (g,) + _adamw_math(w[1][r:r + 1, :], g, m[1][r:r + 1, :], v[1][r:r + 1, :])
            for o, val in zip(outs, res):
                o[1][r:r + 1, :] = val

    res = pl.pallas_call(
        body, out_shape=[jax.ShapeDtypeStruct(s, F32) for s in shapes] * 4 + [jax.ShapeDtypeStruct((1, 128), F32)],
        name="adamw_small",
    )(parts, *ws, *ms, *vs)
    return [res[5 * i:5 * i + 5] for i in range(4)], res[20][0, 0]


def kernel(x, norm_w, w_in, hgrn_lower_bound, hgrn_norm_w, w_branch_hgrn, attn_sinks, w_branch_attn, w_out, final_norm_w, loss_target, m_norm_w, m_w_in, m_hgrn_lower_bound, m_hgrn_norm_w, m_w_branch_hgrn, m_attn_sinks, m_w_branch_attn, m_w_out, m_final_norm_w, v_norm_w, v_w_in, v_hgrn_lower_bound, v_hgrn_norm_w, v_w_branch_hgrn, v_attn_sinks, v_w_branch_attn, v_w_out, v_final_norm_w):
    t = x.shape[1]
    x2d = x.reshape(t, D)
    tgt = loss_target.reshape(t, D)
    fnw = final_norm_w.reshape(1, D)
    row_blk = min(256, t)
    big_blk = min(512, t)

    chip = (2 * lax.axis_index("x") + lax.axis_index("y")).astype(jnp.int32).reshape(1)
    w_p, xn, xnt, proj_own = _gather_in_projection(w_in[0], x2d, norm_w)
    wbh, wba, wout = (g.reshape(D, D) for g in _gather_square(
        [w_branch_hgrn[0].astype(BF16), w_branch_attn[0].astype(BF16), w_out[0].astype(BF16)], after=w_p))

    proj = _inproj_fwd(xn, w_p, proj_own, chip, min(1024, t))
    o_raw, states = _hgrn_fwd(proj, hgrn_lower_bound, big_blk, HGRN_GROUP)
    o_a = _attn_fwd(proj, attn_sinks)
    (dx2, do_raw, do_a, d_hg, d_agm, gh, ga, mg, dyh, dya, dx2b, sums) = _tail(
        o_raw, o_a, proj, x2d, tgt, wbh, wba, wout, hgrn_norm_w, fnw, row_blk)
    d_aq, d_kv, dsink = _attn_bwd(proj, attn_sinks, do_a)
    d_hgrn, dlb = _hgrn_bwd(proj, hgrn_lower_bound, do_raw, states, big_blk, HGRN_GROUP)
    pieces = (d_hgrn, d_hg, d_aq, d_kv, d_agm)
    dw_cat = _inproj_wgrad(xnt, [d_hgrn], CB, 0, None, "inproj_wgrad_hgrn")
    dw_cat = _inproj_wgrad(xnt, [d_hg, d_aq, d_kv], CB // 2, C_HG, dw_cat, "inproj_wgrad_mid")
    dw_cat = _inproj_wgrad(xnt, [d_agm], CB, C_AG, dw_cat, "inproj_wgrad_gates")

    dwin_r = dw_cat.reshape(D, NDEV, IN_SHARD).transpose(1, 0, 2)
    dwbh, dwba, dwout = _wgrad3(gh, dyh, ga, dya, mg, dx2b, big_blk, after=dw_cat)
    slots = lambda a: a.reshape(NDEV, ROW_SHARD, D).astype(BF16)
    got = _exchange_pair([dwin_r], after=dwbh)
    core = lax.axis_index("c").astype(jnp.int32).reshape(1)
    s_in = _pair_sum(dwin_r, got[0], core, 4 * ROW_SHARD, "pair_sum_w_in")
    rin, = _exchange_chips([s_in])
    rbh, rba, rout = _exchange_square([slots(dwbh), slots(dwba), slots(dwout)])
    grad_x, dnw = _inproj_dgrad(pieces, w_p, x2d, dx2, norm_w, big_blk, after=s_in)
    rsm = _exchange_small(_pack_small_grads(dnw, dlb, sums, dsink))
    g_in, d_in, nm_in, nv_in = _adamw_shard(rin, s_in, chip, w_in[0], m_w_in[0], v_w_in[0], 128, "adamw_w_in")
    g_bh, d_bh, nm_bh, nv_bh = _adamw_sum8(
        rbh, w_branch_hgrn[0], m_w_branch_hgrn[0], v_w_branch_hgrn[0], dnw, "adamw_w_bh")
    g_ba, d_ba, nm_ba, nv_ba = _adamw_sum8(
        rba, w_branch_attn[0], m_w_branch_attn[0], v_w_branch_attn[0], dnw, "adamw_w_ba")
    g_out, d_out, nm_out, nv_out = _adamw_sum8(rout, w_out[0], m_w_out[0], v_w_out[0], dnw, "adamw_w_out")
    (sg, sd, sm, sv), loss = _adamw_small(
        rsm,
        (norm_w, hgrn_lower_bound, hgrn_norm_w, fnw, attn_sinks),
        (m_norm_w, m_hgrn_lower_bound, m_hgrn_norm_w, m_final_norm_w.reshape(1, D), m_attn_sinks),
        (v_norm_w, v_hgrn_lower_bound, v_hgrn_norm_w, v_final_norm_w.reshape(1, D), v_attn_sinks))

    def group(s, w_in_v, bh, ba, out):
        nw, lb, hnw, fn, sinks = s
        return (nw, w_in_v[None], lb, hnw, bh[None], sinks, ba[None], out[None], fn.reshape(D))

    return (loss, grad_x.reshape(1, t, D),
            *group(sg, g_in, g_bh, g_ba, g_out), *group(sd, d_in, d_bh, d_ba, d_out),
            *group(sm, nm_in, nm_bh, nm_ba, nm_out), *group(sv, nv_in, nv_bh, nv_ba, nv_out))
```

```python
import jax
import jax.numpy as jnp
from jax import lax
from jax.experimental import pallas as pl
from jax.experimental.pallas import tpu as pltpu
from jax.experimental.pallas import tpu_sc as plsc

F32 = jnp.float32
BF16 = jnp.bfloat16

D = 1024
DIN = 8704
NDEV = 8
IN_SHARD = DIN // NDEV
PAIR = 2 * IN_SHARD
ROW_SHARD = D // NDEV
HEADS = 8
HD = 128
CH = 64
HGRN_GROUP = 8
QH = 16
AB = 128
EPS = 1e-6
NEG = -1e30
ATT_SCALE = 0.125

C_HG = 3072
C_AQ = 4096
C_AK = 5120
C_AV = 5376
C_AG = 5632
C_MH = 6656
C_MA = 7680
CB = 512
ROW_PARTS = 2

LR = 0.001
B1 = 0.9
B2 = 0.999
ADAM_EPS = 1e-08
WD = 0.01
STEP = 10

MESH = pl.DeviceIdType.MESH


def _cp(vmem_mb):
    return pltpu.CompilerParams(vmem_limit_bytes=vmem_mb * 1024 * 1024)


def _mm(a, b):
    return jnp.dot(a, b, preferred_element_type=F32)


def _mm_nt(a, b):
    return lax.dot_general(a, b, (((1,), (1,)), ((), ())), preferred_element_type=F32)


def _mm_tn(a, b):
    return lax.dot_general(a, b, (((0,), (0,)), ((), ())), preferred_element_type=F32)


def _tri3(lower):
    r = lax.broadcasted_iota(jnp.int32, (CH, 3 * CH), 0)
    c = lax.broadcasted_iota(jnp.int32, (CH, 3 * CH), 1)
    c = jnp.where(c >= 2 * CH, c - 2 * CH, jnp.where(c >= CH, c - CH, c))
    return ((r >= c) if lower else (c >= r)).astype(BF16)


def _mm_tri_exact(tri3, g):
    g1 = g.astype(BF16)
    r1 = g - g1.astype(F32)
    g2 = r1.astype(BF16)
    g3 = (r1 - g2.astype(F32)).astype(BF16)
    return _mm(tri3, jnp.concatenate([g1, g2, g3], axis=0))


def _sigmoid(v):
    return 0.5 * jnp.tanh(0.5 * v) + 0.5


def _bf(v):
    return v.astype(BF16)


def _place():
    x, y, c = lax.axis_index("x"), lax.axis_index("y"), lax.axis_index("c")
    return (x, y, c), (x, y, 1 - c), [(1 - x, y), (x, 1 - y), (1 - x, 1 - y)]


def _dev_index(px, py, pc):
    return 4 * px + 2 * py + pc


def _gather_in_projection(w_in_s, x2d, norm_w):
    half = D // 2
    t = x2d.shape[0]
    prep_rows = min(512, t)
    nprep = t // prep_rows

    def body(win_ref, x_hbm, nw_ref, wp_g, xn_hbm, xnt_hbm, proj_hbm, give, take, mine, xbuf, xnbuf, xntbuf, w_own, pbuf,
             send_sems, recv_sems, loc_sem, swap_sems, in_sems, out_sems, own_sem):
        (x, y, c), sibling, chips = _place()
        give[...] = win_ref[pl.ds(pl.multiple_of(half * (1 - c), half), half), :].astype(BF16)
        swap = pltpu.make_async_remote_copy(src_ref=give, dst_ref=take, send_sem=swap_sems.at[0], recv_sem=swap_sems.at[1],
                                            device_id=sibling, device_id_type=MESH)
        swap.start()
        swap.wait()
        own = win_ref[pl.ds(pl.multiple_of(half * c, half), half), :]
        other = take[...].astype(F32)
        mine[...] = jnp.where(c == 0, jnp.concatenate([own, other], axis=1),
                              jnp.concatenate([other, own], axis=1)).astype(BF16)

        def place(px, py, pc):
            return wp_g.at[2 * px + py, pl.ds(pl.multiple_of(half * pc, half), half), :]

        def copy(kind, origin, to, src=mine):
            return pltpu.make_async_remote_copy(
                src_ref=src, dst_ref=place(*origin), send_sem=send_sems.at[kind], recv_sem=recv_sems.at[kind],
                device_id=to, device_id_type=MESH)

        me = (x, y, c)
        local = pltpu.make_async_copy(mine, place(*me), loc_sem)
        local.start()
        first = [copy(0, me, sibling)] + [copy(1 + j, me, (*chip, c)) for j, chip in enumerate(chips)]
        for cp in first:
            cp.start()

        copy(0, (x, y, 1 - c), me).wait_recv()
        local.wait()
        my_chip = 2 * x + y
        fetch = pltpu.make_async_copy(wp_g.at[my_chip], w_own, own_sem)
        fetch.start()

        def rows_of(i):
            return pl.ds(pl.multiple_of(i * prep_rows, prep_rows), prep_rows)

        def load(i, slot):
            return pltpu.make_async_copy(x_hbm.at[rows_of(i), :], xbuf.at[slot], in_sems.at[slot])

        def stores(i, slot):
            own_cols = pl.ds(pl.multiple_of(my_chip * PAIR, 128), PAIR)
            return (pltpu.make_async_copy(xnbuf.at[slot], xn_hbm.at[rows_of(i), :], out_sems.at[slot, 0]),
                    pltpu.make_async_copy(xntbuf.at[slot], xnt_hbm.at[:, rows_of(i)], out_sems.at[slot, 1]),
                    pltpu.make_async_copy(pbuf.at[slot], proj_hbm.at[rows_of(i), own_cols], out_sems.at[slot, 2]))

        load(0, 0).start()
        fetch.wait()

        def prep(i, carry):
            slot = lax.rem(i, 2)
            load(i, slot).wait()

            @pl.when(i + 1 < nprep)
            def _():
                load(i + 1, 1 - slot).start()

            @pl.when(i >= 2)
            def _():
                for cp in stores(i - 2, slot):
                    cp.wait()

            xv = xbuf[slot]
            xn = (xv * lax.rsqrt(jnp.mean(xv * xv, axis=-1, keepdims=True) + EPS)) * nw_ref[...]
            xn_b = xn.astype(BF16)
            xnbuf[slot] = xn_b
            xntbuf[slot] = xn.T.astype(BF16)
            pbuf[slot] = _mm(xn_b, w_own[...])
            for cp in stores(i, slot):
                cp.start()
            return carry

        lax.fori_loop(0, nprep, prep, 0)
        for i in range(max(nprep - 2, 0), nprep):
            for cp in stores(i, i % 2):
                cp.wait()

        passed = []
        for j, chip in enumerate(chips):
            copy(1 + j, (*chip, c), me).wait_recv()
            cp = copy(4 + j, (*chip, c), sibling, src=place(*chip, c))
            cp.start()
            passed.append(cp)
        for j, chip in enumerate(chips):
            copy(4 + j, (*chip, 1 - c), me).wait_recv()
        for cp in first + passed:
            cp.wait_send()

    vm = pl.BlockSpec(memory_space=pltpu.VMEM)
    hbm = pl.BlockSpec(memory_space=pl.ANY)
    return pl.pallas_call(
        body,
        out_shape=[jax.ShapeDtypeStruct((NDEV // 2, D, PAIR), BF16), jax.ShapeDtypeStruct((t, D), BF16),
                   jax.ShapeDtypeStruct((D, t), BF16), jax.ShapeDtypeStruct((t, DIN), F32)],
        in_specs=[vm, hbm, vm],
        out_specs=[hbm, hbm, hbm, hbm],
        scratch_shapes=[pltpu.VMEM((half, IN_SHARD), BF16), pltpu.VMEM((half, IN_SHARD), BF16),
                        pltpu.VMEM((half, PAIR), BF16),
                        pltpu.VMEM((2, prep_rows, D), F32), pltpu.VMEM((2, prep_rows, D), BF16),
                        pltpu.VMEM((2, D, prep_rows), BF16),
                        pltpu.VMEM((D, PAIR), BF16), pltpu.VMEM((2, prep_rows, PAIR), F32),
                        pltpu.SemaphoreType.DMA((NDEV - 1,)), pltpu.SemaphoreType.DMA((NDEV - 1,)),
                        pltpu.SemaphoreType.DMA, pltpu.SemaphoreType.DMA((2,)),
                        pltpu.SemaphoreType.DMA((2,)), pltpu.SemaphoreType.DMA((2, 3)), pltpu.SemaphoreType.DMA],
        name="gather_in_projection", compiler_params=_cp(56),
    )(w_in_s, x2d, norm_w)


def _gather_square(shards, after):
    n = len(shards)

    def launch(*refs):
        ins, outs = refs[:n], refs[n + 1:2 * n + 1]
        send_sems, recv_sems, loc_sems = refs[2 * n + 1:]
        (x, y, c), _, _ = _place()
        me = _dev_index(x, y, c)
        peers = [(1 - x if r & 4 else x, 1 - y if r & 2 else y, 1 - c if r & 1 else c) for r in range(1, NDEV)]
        barrier = pltpu.get_barrier_semaphore()
        for peer in peers:
            pl.semaphore_signal(barrier, inc=1, device_id=peer, device_id_type=MESH)
        pl.semaphore_wait(barrier, NDEV - 1)
        local = [pltpu.make_async_copy(ins[k], outs[k].at[me], loc_sems.at[k]) for k in range(n)]
        copies = [pltpu.make_async_remote_copy(
            src_ref=ins[k], dst_ref=outs[k].at[me], send_sem=send_sems.at[r, k], recv_sem=recv_sems.at[r, k],
            device_id=peer, device_id_type=MESH) for r, peer in enumerate(peers) for k in range(n)]
        for cp in local + copies:
            cp.start()
        for r, peer in enumerate(peers):
            for k in range(n):
                pltpu.make_async_remote_copy(
                    src_ref=ins[k], dst_ref=outs[k].at[_dev_index(*peer)], send_sem=send_sems.at[r, k],
                    recv_sem=recv_sems.at[r, k], device_id=peer, device_id_type=MESH).wait_recv()
        for cp in copies:
            cp.wait_send()
        for cp in local:
            cp.wait()

    return pl.kernel(
        launch, out_type=[jax.ShapeDtypeStruct((NDEV,) + a.shape, a.dtype) for a in shards],
        mesh=plsc.ScalarSubcoreMesh(axis_name="sequencer", num_cores=1), name="gather_square",
        scratch_types=(pltpu.SemaphoreType.DMA((NDEV - 1, n)), pltpu.SemaphoreType.DMA((NDEV - 1, n)),
                       pltpu.SemaphoreType.DMA((n,))),
        compiler_params=pltpu.CompilerParams(collective_id=2),
    )(*shards, after)


def _exchange_pair(arrs, after):
    n = len(arrs)

    def launch(*refs):
        ins, got = refs[:n], refs[n + 1:2 * n + 1]
        send_sems, recv_sems = refs[2 * n + 1:]
        (x, y, c), sibling, _ = _place()
        barrier = pltpu.get_barrier_semaphore()
        pl.semaphore_signal(barrier, inc=1, device_id=sibling, device_id_type=MESH)
        pl.semaphore_wait(barrier, 1)
        sends = [pltpu.make_async_remote_copy(
            src_ref=ins[k].at[_dev_index(q // 2, q % 2, 1 - c)], dst_ref=got[k].at[q], send_sem=send_sems.at[q, k],
            recv_sem=recv_sems.at[q, k], device_id=sibling, device_id_type=MESH) for q in range(4) for k in range(n)]
        for cp in sends:
            cp.start()
        for cp in sends:
            cp.wait_recv()
        for cp in sends:
            cp.wait_send()

    return pl.kernel(
        launch, out_type=[jax.ShapeDtypeStruct((4,) + a.shape[1:], a.dtype) for a in arrs],
        mesh=plsc.ScalarSubcoreMesh(axis_name="sequencer", num_cores=1), name="exchange_pair",
        scratch_types=(pltpu.SemaphoreType.DMA((4, n)), pltpu.SemaphoreType.DMA((4, n))),
        compiler_params=pltpu.CompilerParams(collective_id=0),
    )(*arrs, after)


def _pair_sum(full, got, core, rows, name):
    _, nr, nc = got.shape

    def body(core_ref, a_ref, b_ref, o_ref):
        o_ref[...] = (a_ref[...].astype(F32) + b_ref[...].astype(F32)).astype(BF16)

    blk = pl.BlockSpec((1, rows, nc), lambda q, i, core_ref: (q, i, 0))
    return pl.pallas_call(
        body,
        grid_spec=pltpu.PrefetchScalarGridSpec(
            num_scalar_prefetch=1, grid=(4, nr // rows),
            in_specs=[pl.BlockSpec((1, rows, nc), lambda q, i, core_ref: (2 * q + core_ref[0], i, 0)), blk],
            out_specs=blk),
        out_shape=jax.ShapeDtypeStruct(got.shape, BF16), name=name,
    )(core, full, got)


def _exchange_chips(sums):
    n = len(sums)

    def launch(*refs):
        ins, outs = refs[:n], refs[n:2 * n]
        send_sems, recv_sems = refs[2 * n:]
        (x, y, c), _, chips = _place()
        barrier = pltpu.get_barrier_semaphore()
        for px, py in chips:
            pl.semaphore_signal(barrier, inc=1, device_id=(px, py, c), device_id_type=MESH)
        pl.semaphore_wait(barrier, len(chips))
        copies = [pltpu.make_async_remote_copy(
            src_ref=ins[k].at[2 * px + py], dst_ref=outs[k].at[j], send_sem=send_sems.at[j, k],
            recv_sem=recv_sems.at[j, k], device_id=(px, py, c), device_id_type=MESH)
            for j, (px, py) in enumerate(chips) for k in range(n)]
        for cp in copies:
            cp.start()
        for cp in copies:
            cp.wait_recv()
        for cp in copies:
            cp.wait_send()

    return pl.kernel(
        launch, out_type=[jax.ShapeDtypeStruct((3,) + a.shape[1:], a.dtype) for a in sums],
        mesh=plsc.ScalarSubcoreMesh(axis_name="sequencer", num_cores=1), name="exchange_chips",
        scratch_types=(pltpu.SemaphoreType.DMA((3, n)), pltpu.SemaphoreType.DMA((3, n))),
        compiler_params=pltpu.CompilerParams(collective_id=1),
    )(*sums)


def _exchange_square(partials):
    n = len(partials)

    def launch(*refs):
        ins, outs = refs[:n], refs[n:2 * n]
        send_sems, recv_sems, loc_sems = refs[2 * n:]
        (x, y, c), _, _ = _place()
        me = _dev_index(x, y, c)
        peers = [(1 - x if r & 4 else x, 1 - y if r & 2 else y, 1 - c if r & 1 else c) for r in range(1, NDEV)]
        barrier = pltpu.get_barrier_semaphore()
        for peer in peers:
            pl.semaphore_signal(barrier, inc=1, device_id=peer, device_id_type=MESH)
        pl.semaphore_wait(barrier, NDEV - 1)
        local = [pltpu.make_async_copy(ins[k].at[me], outs[k].at[me], loc_sems.at[k]) for k in range(n)]
        copies = [pltpu.make_async_remote_copy(
            src_ref=ins[k].at[_dev_index(*peer)], dst_ref=outs[k].at[me], send_sem=send_sems.at[r, k],
            recv_sem=recv_sems.at[r, k], device_id=peer, device_id_type=MESH)
            for r, peer in enumerate(peers) for k in range(n)]
        for cp in local + copies:
            cp.start()
        for r, peer in enumerate(peers):
            for k in range(n):
                pltpu.make_async_remote_copy(
                    src_ref=ins[k].at[me], dst_ref=outs[k].at[_dev_index(*peer)], send_sem=send_sems.at[r, k],
                    recv_sem=recv_sems.at[r, k], device_id=peer, device_id_type=MESH).wait_recv()
        for cp in copies:
            cp.wait_send()
        for cp in local:
            cp.wait()

    return pl.kernel(
        launch, out_type=[jax.ShapeDtypeStruct(a.shape, a.dtype) for a in partials],
        mesh=plsc.ScalarSubcoreMesh(axis_name="sequencer", num_cores=1), name="exchange_square",
        scratch_types=(pltpu.SemaphoreType.DMA((NDEV - 1, n)), pltpu.SemaphoreType.DMA((NDEV - 1, n)),
                       pltpu.SemaphoreType.DMA((n,))),
        compiler_params=pltpu.CompilerParams(collective_id=3),
    )(*partials)


def _exchange_small(small):
    def launch(sm_ref, out_ref, send_sems, recv_sems, loc_sem):
        (x, y, c), _, _ = _place()
        me = _dev_index(x, y, c)
        peers = [(1 - x if r & 4 else x, 1 - y if r & 2 else y, 1 - c if r & 1 else c) for r in range(1, NDEV)]
        barrier = pltpu.get_barrier_semaphore()
        for peer in peers:
            pl.semaphore_signal(barrier, inc=1, device_id=peer, device_id_type=MESH)
        pl.semaphore_wait(barrier, NDEV - 1)
        local = pltpu.make_async_copy(sm_ref, out_ref.at[me], loc_sem)
        local.start()
        copies = [pltpu.make_async_remote_copy(
            src_ref=sm_ref, dst_ref=out_ref.at[me], send_sem=send_sems.at[r], recv_sem=recv_sems.at[r],
            device_id=peer, device_id_type=MESH) for r, peer in enumerate(peers)]
        for cp in copies:
            cp.start()
        for r, peer in enumerate(peers):
            pltpu.make_async_remote_copy(
                src_ref=sm_ref, dst_ref=out_ref.at[_dev_index(*peer)], send_sem=send_sems.at[r], recv_sem=recv_sems.at[r],
                device_id=peer, device_id_type=MESH).wait_recv()
        for cp in copies:
            cp.wait_send()
        local.wait()

    return pl.kernel(
        launch, out_type=jax.ShapeDtypeStruct((NDEV,) + small.shape, F32),
        mesh=plsc.ScalarSubcoreMesh(axis_name="sequencer", num_cores=1), name="exchange_small",
        scratch_types=(pltpu.SemaphoreType.DMA((NDEV - 1,)), pltpu.SemaphoreType.DMA((NDEV - 1,)), pltpu.SemaphoreType.DMA),
        compiler_params=pltpu.CompilerParams(collective_id=4),
    )(small)


def _inproj_fwd(xn, w_pairs, proj, chip, tb):
    t = xn.shape[0]
    nblk, _, nb = w_pairs.shape

    def body(chip_ref, xn_ref, w_ref, proj_in, proj_ref):
        proj_ref[...] = _mm(xn_ref[...], w_ref[0])

    def other(j, chip_ref):
        return j + (j >= chip_ref[0]).astype(jnp.int32)

    return pl.pallas_call(
        body,
        grid_spec=pltpu.PrefetchScalarGridSpec(
            num_scalar_prefetch=1, grid=(t // tb, nblk - 1),
            in_specs=[pl.BlockSpec((tb, D), lambda i, j, chip_ref: (i, 0)),
                      pl.BlockSpec((1, D, nb), lambda i, j, chip_ref: (other(j, chip_ref), 0, 0)),
                      pl.BlockSpec(memory_space=pl.ANY)],
            out_specs=pl.BlockSpec((tb, nb), lambda i, j, chip_ref: (i, other(j, chip_ref)))),
        out_shape=jax.ShapeDtypeStruct((t, DIN), F32),
        input_output_aliases={3: 0},
        name="inproj_fwd", compiler_params=_cp(56),
    )(chip, xn, w_pairs, proj)


def _lower_bound(lb_ref):
    a0 = lb_ref[0:1, :]
    a1 = lb_ref[1:2, :]
    mx = jnp.maximum(a0, a1)
    e0 = jnp.exp(a0 - mx)
    e1 = jnp.exp(a1 - mx)
    return e0 / (e0 + e1)


def _hgrn_chunk_fwd(hq, hf, lb, tril):
    sg = _sigmoid(hf)
    f = lb + (1.0 - lb) * sg
    g = jnp.log(f)
    k = 1.0 - f
    sq = _sigmoid(hq)
    q = hq * sq
    b = _mm_tri_exact(tril, g)
    last_row = lax.broadcasted_iota(jnp.int32, b.shape, 0) == CH - 1
    b_last = jnp.sum(jnp.where(last_row, b, 0.0), axis=0, keepdims=True)
    c = 0.5 * b_last
    eb = jnp.exp(b)
    ea = jnp.exp(b - c)
    ek = jnp.exp(c - b)
    ed = jnp.exp(b_last - b)
    ebl = jnp.exp(b_last)
    return dict(sg=sg, f=f, k=k, sq=sq, q=q, eb=eb, ea=ea, ek=ek, ed=ed, ebl=ebl,
                qe=q * eb, qa=q * ea, ka=k * ek, kd=k * ed)


def _tri(lower):
    r = lax.broadcasted_iota(jnp.int32, (CH, CH), 0)
    c = lax.broadcasted_iota(jnp.int32, (CH, CH), 1)
    return (r >= c) if lower else (c >= r)


def _head_segment(p_ref, rows, j, hg):
    return p_ref[rows, j * HD * hg:(j + 1) * HD * hg]


def _head(a, k):
    return a[:, k * HD:(k + 1) * HD]


def _hgrn_fwd(proj, lbw, rb, hg):
    assert hg == HEADS
    t = proj.shape[0]
    ncb = rb // CH

    def body(p_ref, lb_ref, o_ref, st_ref, s_scr):
        @pl.when(pl.program_id(1) == 0)
        def _():
            s_scr[...] = jnp.zeros_like(s_scr)

        lb = _lower_bound(lb_ref)
        causal = _tri(True)
        tril = _tri3(True)
        heads = range(hg)

        def chunk(cc, carry):
            r0 = pl.multiple_of(cc * CH, CH)
            rows = pl.ds(r0, CH)
            e = _hgrn_chunk_fwd(_head_segment(p_ref, rows, 0, hg), _head_segment(p_ref, rows, 1, hg), lb, tril)
            v = _bf(_head_segment(p_ref, rows, 2, hg))
            sts = [s_scr[k] for k in heads]
            qa, ka, qe, kd = _bf(e["qa"]), _bf(e["ka"]), _bf(e["qe"]), _bf(e["kd"])
            a = [_bf(jnp.where(causal, _mm_nt(_head(qa, k), _head(ka, k)), 0.0)) for k in heads]
            o_inter = [_mm_nt(_head(qe, k), _bf(sts[k])) for k in heads]
            kv = [_mm_tn(_head(v, k), _head(kd, k)) for k in heads]
            o_intra = [_mm(a[k], _head(v, k)) for k in heads]
            for k in heads:
                st_ref[cc, k] = sts[k]
                o_ref[rows, k * HD:(k + 1) * HD] = o_inter[k] + o_intra[k]
                s_scr[k] = sts[k] * _head(e["ebl"], k) + kv[k]
            return carry

        lax.fori_loop(0, ncb, chunk, 0, unroll=4)

    return pl.pallas_call(
        body, grid=(HEADS // hg, t // rb),
        in_specs=[pl.BlockSpec((rb, 3 * HD * hg), lambda h, i: (i, h)), pl.BlockSpec((2, HD * hg), lambda h, i: (0, h))],
        out_specs=[pl.BlockSpec((rb, HD * hg), lambda h, i: (i, h)),
                   pl.BlockSpec((ncb, hg, HD, HD), lambda h, i: (i, h, 0, 0))],
        out_shape=[jax.ShapeDtypeStruct((t, D), F32), jax.ShapeDtypeStruct((t // CH, HEADS, HD, HD), F32)],
        scratch_shapes=[pltpu.VMEM((hg, HD, HD), F32)],
        name="hgrn_fwd", compiler_params=_cp(48),
    )(proj, lbw)


def _hgrn_bwd(proj, lbw, do_raw, states, rb, hg):
    assert hg == HEADS
    t = proj.shape[0]
    nblk = t // rb
    ncb = rb // CH
    wd = HD * hg

    def body(p_ref, lb_ref, do_ref, st_ref, dp_ref, dlb_ref, ds_scr):
        @pl.when(pl.program_id(1) == 0)
        def _():
            ds_scr[...] = jnp.zeros_like(ds_scr)
            dlb_ref[...] = jnp.zeros_like(dlb_ref)

        lb = _lower_bound(lb_ref)
        causal = _tri(True)
        tril = _tri3(True)
        triu = _tri3(False)
        last_row = lax.broadcasted_iota(jnp.int32, (CH, HD * hg), 0) == CH - 1
        row0 = lax.broadcasted_iota(jnp.int32, (8, HD * hg), 0) == 0
        heads = range(hg)
        wide = lambda parts: jnp.concatenate(parts, axis=1)

        def chunk(it, carry):
            cc = ncb - 1 - it
            r0 = pl.multiple_of(cc * CH, CH)
            rows = pl.ds(r0, CH)
            hq = _head_segment(p_ref, rows, 0, hg)
            e = _hgrn_chunk_fwd(hq, _head_segment(p_ref, rows, 1, hg), lb, tril)
            v = _bf(_head_segment(p_ref, rows, 2, hg))
            do = _bf(do_ref[rows, :])
            sts = [st_ref[cc, k] for k in heads]
            dsts = [ds_scr[k] for k in heads]
            dlb_acc = dlb_ref[...]
            qa, ka, qe, kd = _bf(e["qa"]), _bf(e["ka"]), _bf(e["qe"]), _bf(e["kd"])
            a = [_bf(jnp.where(causal, _mm_nt(_head(qa, k), _head(ka, k)), 0.0)) for k in heads]
            da = [_bf(jnp.where(causal, _mm_nt(_head(do, k), _head(v, k)), 0.0)) for k in heads]
            dqe = wide([_mm(_head(do, k), _bf(sts[k])) for k in heads])
            dkd = wide([_mm(_head(v, k), _bf(dsts[k])) for k in heads])
            dv_state = [_mm_nt(_head(kd, k), _bf(dsts[k])) for k in heads]
            ds_new = [_mm_tn(_head(do, k), _head(qe, k)) for k in heads]
            dv_intra = [_mm_tn(a[k], _head(do, k)) for k in heads]
            dqa = wide([_mm(da[k], _head(ka, k)) for k in heads])
            dka = wide([_mm_tn(da[k], _head(qa, k)) for k in heads])
            dv = wide([dv_intra[k] + dv_state[k] for k in heads])
            dbl = e["ebl"] * wide([jnp.sum(sts[k] * dsts[k], axis=0, keepdims=True) for k in heads])
            dq = dqe * e["eb"] + dqa * e["ea"]
            dk = dka * e["ek"] + dkd * e["ed"]
            dkd_kd = dkd * kd.astype(F32)
            db = dqe * qe.astype(F32) + dqa * qa.astype(F32) - dka * ka.astype(F32) - dkd_kd
            db = db + jnp.where(last_row, dbl + jnp.sum(dkd_kd, axis=0, keepdims=True), 0.0)
            dg = _mm_tri_exact(triu, db)
            df = dg / e["f"] - dk
            sg = e["sg"]
            sq = e["sq"]
            dhq = _bf(dq * (sq * (1.0 + hq * (1.0 - sq))))
            dhf = _bf(df * (1.0 - lb) * sg * (1.0 - sg))
            dhi = _bf(dv)
            dlb_new = dlb_acc + jnp.where(row0, jnp.sum(df * (1.0 - sg), axis=0, keepdims=True), 0.0)
            for k in heads:
                ds_scr[k] = ds_new[k] + dsts[k] * _head(e["ebl"], k)
            dp_ref[rows, 0:wd] = dhq
            dp_ref[rows, wd:2 * wd] = dhf
            dp_ref[rows, 2 * wd:3 * wd] = dhi
            dlb_ref[...] = dlb_new
            return carry

        lax.fori_loop(0, ncb, chunk, 0, unroll=4)

    rev = lambda h, i: (nblk - 1 - i, h)
    return pl.pallas_call(
        body, grid=(HEADS // hg, nblk),
        in_specs=[pl.BlockSpec((rb, 3 * HD * hg), rev), pl.BlockSpec((2, HD * hg), lambda h, i: (0, h)),
                  pl.BlockSpec((rb, HD * hg), rev), pl.BlockSpec((ncb, hg, HD, HD), lambda h, i: (nblk - 1 - i, h, 0, 0))],
        out_specs=[pl.BlockSpec((rb, 3 * HD * hg), rev), pl.BlockSpec((8, HD * hg), lambda h, i: (0, h))],
        out_shape=[jax.ShapeDtypeStruct((t, 3 * D), BF16), jax.ShapeDtypeStruct((8, D), F32)],
        scratch_shapes=[pltpu.VMEM((hg, HD, HD), F32)],
        name="hgrn_bwd", compiler_params=_cp(48),
    )(proj, lbw, do_raw, states)


def _kv_variants(tile, odd):
    low = lax.broadcasted_iota(jnp.int32, tile.shape, 1) < 64
    if odd:
        hi = jnp.where(low, 0.0, tile)
        lo = pltpu.roll(hi, 64, 1)
    else:
        lo = jnp.where(low, tile, 0.0)
        hi = pltpu.roll(lo, 64, 1)
    return _bf(lo), _bf(hi)


def _attn_masks(n):
    qi = lax.broadcasted_iota(jnp.int32, (AB, AB), 0)
    kj = lax.broadcasted_iota(jnp.int32, (AB, AB), 1)
    cur = kj <= qi
    return cur, cur | (n > 0), qi <= kj


def _kv_all(prev_ref, cur_ref):
    out = []
    for tl in range(2):
        cols = slice(tl * 128, (tl + 1) * 128)
        tile = jnp.concatenate([prev_ref[:, cols], cur_ref[:, cols]], axis=0)
        out.append(_kv_variants(tile, 0))
        out.append(_kv_variants(tile, 1))
    return out


def _window(a2, cur):
    return jnp.where(cur, a2[:, AB:], a2[:, :AB])


def _attn_softmax(scores, sinks, cur, ok):
    s = [jnp.where(ok, _window(s2, cur) * ATT_SCALE, NEG) for s2 in scores]
    m = [jnp.maximum(jnp.max(si, axis=-1, keepdims=True), sink) for si, sink in zip(s, sinks)]
    p = [jnp.exp(si - mi) for si, mi in zip(s, m)]
    es = [jnp.exp(sink - mi) for sink, mi in zip(sinks, m)]
    inv = [1.0 / (jnp.sum(pi, axis=-1, keepdims=True) + ei) for pi, ei in zip(p, es)]
    return [pi * ii for pi, ii in zip(p, inv)], [ei * ii for ei, ii in zip(es, inv)]


def _spread(pc, cur):
    return jnp.concatenate([jnp.where(cur, 0.0, pc), jnp.where(cur, pc, 0.0)], axis=1)


def _spread_t(pct, cur_t):
    return jnp.concatenate([jnp.where(cur_t, 0.0, pct), jnp.where(cur_t, pct, 0.0)], axis=0)


def _attn_fwd(proj, sinks):
    t = proj.shape[0]
    nb = t // AB

    nsub = 4
    assert nb % nsub == 0

    def body(q_ref, kc_ref, kp_ref, vc_ref, vp_ref, sink_ref, o_ref):
        sinks_v = [sink_ref[0, h] for h in range(QH)]
        heads = [(j, ab) for j in range(8) for ab in range(2)]
        for sb in range(nsub):
            rows = pl.ds(AB * sb, AB)
            before = pl.ds(AB * (sb - 1), AB)
            cur, ok, _ = _attn_masks(nsub * pl.program_id(0) + sb)
            kvars = _kv_all(kp_ref if sb == 0 else kc_ref.at[before, :], kc_ref.at[rows, :])
            vvars = _kv_all(vp_ref if sb == 0 else vc_ref.at[before, :], vc_ref.at[rows, :])
            qps = [_bf(q_ref[rows, 128 * j:128 * (j + 1)]) for j in range(8)]
            scores = [_mm_nt(qps[j], kvars[j // 2][ab]) for j, ab in heads]
            pcs, _ = _attn_softmax(scores, sinks_v, cur, ok)
            parts = [_mm(_bf(_spread(pcs[h], cur)), vvars[j // 2][ab]) for h, (j, ab) in enumerate(heads)]
            for j in range(8):
                o_ref[rows, 128 * j:128 * (j + 1)] = parts[2 * j] + parts[2 * j + 1]

    prev = lambda n: jnp.maximum(nsub * n - 1, 0)
    step = nsub * AB
    return pl.pallas_call(
        body, grid=(nb // nsub,),
        in_specs=[pl.BlockSpec((step, D), lambda n: (n, C_AQ // D)),
                  pl.BlockSpec((step, 256), lambda n: (n, C_AK // 256)),
                  pl.BlockSpec((AB, 256), lambda n: (prev(n), C_AK // 256)),
                  pl.BlockSpec((step, 256), lambda n: (n, C_AV // 256)),
                  pl.BlockSpec((AB, 256), lambda n: (prev(n), C_AV // 256)),
                  pl.BlockSpec(memory_space=pltpu.SMEM)],
        out_specs=pl.BlockSpec((step, D), lambda n: (n, 0)),
        out_shape=jax.ShapeDtypeStruct((t, D), F32),
        name="attn_fwd", compiler_params=_cp(32),
    )(proj, proj, proj, proj, proj, sinks)


def _attn_bwd(proj, sinks, do_a):
    t = proj.shape[0]
    nb = t // AB
    nsub = 2
    assert nb % nsub == 0
    steps = nb // nsub
    step = nsub * AB

    def body(q_ref, kc_ref, kp_ref, vc_ref, vp_ref, do_ref, sink_ref, dq_ref, dkv_ref, dsink_ref, carry):
        n = pl.program_id(0)

        @pl.when(n == 0)
        def _():
            dsink_ref[...] = jnp.zeros_like(dsink_ref)
            carry[...] = jnp.zeros_like(carry)

        def one_block(sb):
            rows = pl.ds(AB * sb, AB)
            before = pl.ds(AB * (sb - 1), AB)
            cur, ok, cur_t = _attn_masks(nsub * n + sb)
            low = lax.broadcasted_iota(jnp.int32, (2 * AB, 128), 1) < 64
            lane = lax.broadcasted_iota(jnp.int32, (8, 128), 1)
            row0 = lax.broadcasted_iota(jnp.int32, (8, 128), 0) == 0
            kvars = _kv_all(kp_ref if sb == 0 else kc_ref.at[before, :], kc_ref.at[rows, :])
            vvars = _kv_all(vp_ref if sb == 0 else vc_ref.at[before, :], vc_ref.at[rows, :])
            qps = [_bf(q_ref[rows, 128 * j:128 * (j + 1)]) for j in range(8)]
            dops = [_bf(do_ref[rows, 128 * j:128 * (j + 1)]) for j in range(8)]
            heads = [(j, ab) for j in range(8) for ab in range(2)]
            scores = [_mm_nt(qps[j], kvars[j // 2][ab]) for j, ab in heads]
            dps = [_mm_nt(dops[j], vvars[j // 2][ab]) for j, ab in heads]
            pcs, pss = _attn_softmax(scores, [sink_ref[0, h] for h in range(QH)], cur, ok)
            dpcs = [_window(dp2, cur) for dp2 in dps]
            rss = [jnp.sum(pc * dpc, axis=-1, keepdims=True) for pc, dpc in zip(pcs, dpcs)]
            dscs = [pc * (dpc - rs) for pc, dpc, rs in zip(pcs, dpcs, rss)]
            dsink = jnp.zeros((8, 128), F32)
            for h in range(QH):
                dsink = dsink + jnp.where(row0 & (lane == h), -jnp.sum(pss[h] * rss[h]), 0.0)
            dq_terms = [_mm(_bf(_spread(dscs[h], cur)), kvars[j // 2][ab]) for h, (j, ab) in enumerate(heads)]
            for j in range(8):
                dq_ref[rows, 128 * j:128 * (j + 1)] = _bf((dq_terms[2 * j] + dq_terms[2 * j + 1]) * ATT_SCALE)
            dsc_t = [_bf(_spread_t(dsc.T, cur_t)) for dsc in dscs]
            pc_t = [_bf(_spread_t(pc.T, cur_t)) for pc in pcs]
            dk_terms = [_mm(dsc_t[h], qps[j]) for h, (j, ab) in enumerate(heads)]
            dv_terms = [_mm(pc_t[h], dops[j]) for h, (j, ab) in enumerate(heads)]
            dk_ab = [[dk_terms[4 * g + ab] + dk_terms[4 * g + 2 + ab] for ab in range(2)] for g in range(4)]
            dv_ab = [[dv_terms[4 * g + ab] + dv_terms[4 * g + 2 + ab] for ab in range(2)] for g in range(4)]
            dkts, dvts = [], []
            for tl in range(2):
                ke, ko = dk_ab[2 * tl], dk_ab[2 * tl + 1]
                ve, vo = dv_ab[2 * tl], dv_ab[2 * tl + 1]
                dkts.append((jnp.where(low, ke[0], 0.0) + pltpu.roll(jnp.where(low, 0.0, ke[1]), 64, 1)
                             + jnp.where(low, 0.0, ko[1]) + pltpu.roll(jnp.where(low, ko[0], 0.0), 64, 1)) * ATT_SCALE)
                dvts.append(jnp.where(low, ve[0], 0.0) + pltpu.roll(jnp.where(low, 0.0, ve[1]), 64, 1)
                            + jnp.where(low, 0.0, vo[1]) + pltpu.roll(jnp.where(low, vo[0], 0.0), 64, 1))
            return dkts, dvts, dsink

        @pl.when(n < steps)
        def _():
            (dk0, dv0, ds0), (dk1, dv1, ds1) = one_block(0), one_block(1)
            first, second = slice(0, AB), slice(AB, 2 * AB)
            for tl in range(2):
                for cols, g0, g1 in ((slice(tl * 128, (tl + 1) * 128), dk0[tl], dk1[tl]),
                                     (slice(256 + tl * 128, 256 + (tl + 1) * 128), dv0[tl], dv1[tl])):
                    dkv_ref[first, cols] = _bf(carry[first, cols])
                    dkv_ref[second, cols] = _bf(carry[second, cols] + g0[first])
                    carry[first, cols] = g0[second] + g1[first]
                    carry[second, cols] = g1[second]
            dsink_ref[...] += ds0 + ds1

        @pl.when(n == steps)
        def _():
            dkv_ref[...] = _bf(carry[...])

    cur = lambda n: jnp.minimum(n, steps - 1)
    last = lambda n: jnp.clip(n - 1, 0, steps - 1)
    prev = lambda n: jnp.clip(nsub * n - 1, 0, nb - 1)
    return pl.pallas_call(
        body, grid=(steps + 1,),
        in_specs=[pl.BlockSpec((step, D), lambda n: (cur(n), C_AQ // D)),
                  pl.BlockSpec((step, 256), lambda n: (cur(n), C_AK // 256)),
                  pl.BlockSpec((AB, 256), lambda n: (prev(n), C_AK // 256)),
                  pl.BlockSpec((step, 256), lambda n: (cur(n), C_AV // 256)),
                  pl.BlockSpec((AB, 256), lambda n: (prev(n), C_AV // 256)),
                  pl.BlockSpec((step, D), lambda n: (cur(n), 0)),
                  pl.BlockSpec(memory_space=pltpu.SMEM)],
        out_specs=[pl.BlockSpec((step, D), lambda n: (cur(n), 0)),
                   pl.BlockSpec((step, 512), lambda n: (last(n), 0)),
                   pl.BlockSpec((8, 128), lambda n: (0, 0))],
        out_shape=[jax.ShapeDtypeStruct((t, D), BF16), jax.ShapeDtypeStruct((t, 512), BF16),
                   jax.ShapeDtypeStruct((8, 128), F32)],
        scratch_shapes=[pltpu.VMEM((step, 512), F32)],
        name="attn_bwd", compiler_params=_cp(40),
    )(proj, proj, proj, proj, proj, do_a, sinks)


def _silu_and_grad(v):
    s = _sigmoid(v)
    sil = v * s
    return sil, s + sil * (1.0 - s)


def _tail(o_raw, o_a, proj, x2d, tgt, wbh, wba, wout, hnw, fnw, tb):
    t = x2d.shape[0]

    def body(or_ref, oa_ref, hg_ref, ag0, ag1, mh0, mh1, ma0, ma1, x_ref, t_ref, wbh_ref, wba_ref, wout_ref, hnw_ref,
             fnw_ref, dx2_ref, dor_ref, doa_ref, dhg_ref, dagm_ref, gh_ref, ga_ref, mg_ref, dyh_ref, dya_ref, dx2b_ref,
             sums_ref):
        @pl.when(pl.program_id(0) == 0)
        def _():
            sums_ref[...] = jnp.zeros_like(sums_ref)

        hr = tb // ROW_PARTS
        parts = [pl.ds(k * hr, hr) for k in range(ROW_PARTS)]
        rows_of = lambda v, k: v[k * hr:(k + 1) * hr]
        whole = lambda vs: jnp.concatenate(vs, axis=0)
        halves = lambda a, b, r: jnp.concatenate([a[r, :], b[r, :]], axis=1)
        hnw_v = hnw_ref[...]
        fnw_v = fnw_ref[...]
        sv = []
        for r in parts:
            o = or_ref[r, :]
            rs, xhs = [], []
            for h in range(HEADS):
                oh = o[:, h * HD:(h + 1) * HD]
                rn = lax.rsqrt(jnp.mean(oh * oh, axis=-1, keepdims=True) + EPS)
                rs.append(rn)
                xhs.append(oh * rn)
            xh = jnp.concatenate(xhs, axis=1)
            on = xh * hnw_v
            sil_hg, dsil_hg = _silu_and_grad(hg_ref[r, :])
            oa = oa_ref[r, :]
            sil_ag, dsil_ag = _silu_and_grad(halves(ag0, ag1, r))
            sv.append(dict(rs=rs, xhs=xhs, xh=xh, on=on, sil_hg=sil_hg, dsil_hg=dsil_hg, oa=oa, sil_ag=sil_ag,
                           dsil_ag=dsil_ag, gh_b=_bf(on * sil_hg), ga_b=_bf(oa * sil_ag)))
        gh_b = whole([s["gh_b"] for s in sv])
        ga_b = whole([s["ga_b"] for s in sv])
        y_h = _mm(gh_b, wbh_ref[...])
        y_a = _mm(ga_b, wba_ref[...])
        for k, r in enumerate(parts):
            s = sv[k]
            s["s_mh"] = _sigmoid(halves(mh0, mh1, r))
            s["s_ma"] = _sigmoid(halves(ma0, ma1, r))
            s["y_h"], s["y_a"] = rows_of(y_h, k), rows_of(y_a, k)
            s["mg_b"] = _bf(s["s_mh"] * s["y_h"] + s["s_ma"] * s["y_a"])
        mg_b = whole([s["mg_b"] for s in sv])
        dx_out = _mm(mg_b, wout_ref[...])
        loss = 0.0
        dfnw = 0.0
        for k, r in enumerate(parts):
            x2 = x_ref[r, :] + rows_of(dx_out, k)
            r2 = lax.rsqrt(jnp.mean(x2 * x2, axis=-1, keepdims=True) + EPS)
            xh2 = x2 * r2
            err = xh2 * fnw_v - t_ref[r, :]
            loss = loss + 0.5 * jnp.sum(jnp.mean(err * err, axis=-1, keepdims=True))
            dfnw = dfnw + jnp.sum(err * xh2, axis=0, keepdims=True) * (1.0 / D)
            dxh2 = err * fnw_v
            dx2 = (r2 * (1.0 / D)) * (dxh2 - xh2 * jnp.mean(dxh2 * xh2, axis=-1, keepdims=True))
            dx2_ref[r, :] = dx2
            sv[k]["dx2_b"] = _bf(dx2)
        dx2_b = whole([s["dx2_b"] for s in sv])
        dmg = _mm_nt(dx2_b, wout_ref[...])
        for k, r in enumerate(parts):
            s = sv[k]
            dmg_h = rows_of(dmg, k) * s["s_mh"]
            dmg_a = rows_of(dmg, k) * s["s_ma"]
            s["dyh_b"], s["dya_b"] = _bf(dmg_h), _bf(dmg_a)
            dagm_ref[r, D:2 * D] = _bf(dmg_h * s["y_h"] * (1.0 - s["s_mh"]))
            dagm_ref[r, 2 * D:3 * D] = _bf(dmg_a * s["y_a"] * (1.0 - s["s_ma"]))
        dyh_b = whole([s["dyh_b"] for s in sv])
        dya_b = whole([s["dya_b"] for s in sv])
        dgh_all = _mm_nt(dyh_b, wbh_ref[...])
        dga_all = _mm_nt(dya_b, wba_ref[...])
        dhnw = 0.0
        for k, r in enumerate(parts):
            s = sv[k]
            dgh, dga = rows_of(dgh_all, k), rows_of(dga_all, k)
            doa_ref[r, :] = dga * s["sil_ag"]
            dagm_ref[r, 0:D] = _bf(dga * s["oa"] * s["dsil_ag"])
            dhg_ref[r, :] = _bf(dgh * s["on"] * s["dsil_hg"])
            don = dgh * s["sil_hg"]
            dhnw = dhnw + jnp.sum(don * s["xh"], axis=0, keepdims=True)
            dxh = don * hnw_v
            dos = []
            for h in range(HEADS):
                sl = slice(h * HD, (h + 1) * HD)
                dos.append(s["rs"][h] * (dxh[:, sl] - s["xhs"][h]
                                         * jnp.mean(dxh[:, sl] * s["xhs"][h], axis=-1, keepdims=True)))
            dor_ref[r, :] = jnp.concatenate(dos, axis=1)
        gh_ref[...] = gh_b
        ga_ref[...] = ga_b
        mg_ref[...] = mg_b
        dyh_ref[...] = dyh_b
        dya_ref[...] = dya_b
        dx2b_ref[...] = dx2_b
        row = lax.broadcasted_iota(jnp.int32, (8, D), 0)
        sums_ref[...] += jnp.where(row == 0, dfnw, 0.0) + jnp.where(row == 1, dhnw, 0.0) + jnp.where(row == 2, loss, 0.0)

    rowblk = lambda c: pl.BlockSpec((tb, D), lambda i: (i, c))
    half = lambda c: pl.BlockSpec((tb, 512), lambda i: (i, c))
    full = lambda shape: pl.BlockSpec(shape, lambda i: (0, 0))
    return pl.pallas_call(
        body, grid=(t // tb,),
        in_specs=[rowblk(0), rowblk(0), rowblk(C_HG // D), half(C_AG // 512), half(C_AG // 512 + 1), half(C_MH // 512),
                  half(C_MH // 512 + 1), half(C_MA // 512), half(C_MA // 512 + 1), rowblk(0), rowblk(0),
                  full((D, D)), full((D, D)), full((D, D)), full((1, D)), full((1, D))],
        out_specs=[rowblk(0), rowblk(0), rowblk(0), rowblk(0), pl.BlockSpec((tb, 3 * D), lambda i: (i, 0))]
        + [rowblk(0)] * 6 + [full((8, D))],
        out_shape=[jax.ShapeDtypeStruct((t, D), F32)] * 3
        + [jax.ShapeDtypeStruct((t, D), BF16), jax.ShapeDtypeStruct((t, 3 * D), BF16)]
        + [jax.ShapeDtypeStruct((t, D), BF16)] * 6 + [jax.ShapeDtypeStruct((8, D), F32)],
        name="tail", compiler_params=_cp(56),
    )(o_raw, o_a, proj, proj, proj, proj, proj, proj, proj, x2d, tgt, wbh, wba, wout, hnw, fnw)


def _wgrad3(gh, dyh, ga, dya, mg, dx2b, tk, after):
    t = dyh.shape[0]

    def body(a0, b0, a1, b1, a2, b2, _, o0, o1, o2):
        @pl.when(pl.program_id(0) == 0)
        def _():
            o0[...] = jnp.zeros_like(o0)
            o1[...] = jnp.zeros_like(o1)
            o2[...] = jnp.zeros_like(o2)

        o0[...] += _mm_tn(a0[...], b0[...])
        o1[...] += _mm_tn(a1[...], b1[...])
        o2[...] += _mm_tn(a2[...], b2[...])

    blk = pl.BlockSpec((tk, D), lambda k: (k, 0))
    out = pl.BlockSpec((D, D), lambda k: (0, 0))
    return pl.pallas_call(
        body, grid=(t // tk,), in_specs=[blk] * 6 + [pl.BlockSpec(memory_space=pl.ANY)], out_specs=[out] * 3,
        out_shape=[jax.ShapeDtypeStruct((D, D), F32)] * 3,
        name="wgrad3", compiler_params=_cp(48),
    )(gh, dyh, ga, dya, mg, dx2b, after)


def _inproj_wgrad(xnt, pieces, nb, col0, dw, name):
    t = xnt.shape[1]
    counts = [p.shape[1] // nb for p in pieces]
    firsts = [sum(counts[:k]) for k in range(len(pieces))]
    assert col0 % nb == 0

    def body(xnt_ref, *refs):
        o_ref = refs[-1]
        j = pl.program_id(0)
        for first, count, p_ref in zip(firsts, counts, refs):
            @pl.when((j >= first) & (j < first + count))
            def _(p_ref=p_ref):
                o_ref[...] = _mm(xnt_ref[...], p_ref[...]).astype(BF16)

    def piece_spec(first, count):
        return pl.BlockSpec((t, nb), lambda j: (0, jnp.clip(j - first, 0, count - 1)))

    carried = [] if dw is None else [dw]
    return pl.pallas_call(
        body, grid=(sum(counts),),
        in_specs=[pl.BlockSpec((D, t), lambda j: (0, 0), pipeline_mode=pl.Buffered(1))]
        + [piece_spec(f, c) for f, c in zip(firsts, counts)] + [pl.BlockSpec(memory_space=pl.ANY)] * len(carried),
        out_specs=pl.BlockSpec((D, nb), lambda j: (0, j + col0 // nb)),
        out_shape=jax.ShapeDtypeStruct((D, DIN), BF16),
        input_output_aliases={1 + len(pieces): 0} if carried else {},
        name=name, compiler_params=_cp(56),
    )(xnt, *pieces, *carried)


def _inproj_dgrad(pieces, w_p, x2d, dx2, norm_w, tb, after):
    t = x2d.shape[0]

    def body(*refs):
        piece_refs = refs[:len(pieces)]
        w_ref, x_ref, dx2_ref, nw_ref, _, gx_ref, dnw_ref = refs[len(pieces):]

        @pl.when(pl.program_id(0) == 0)
        def _():
            dnw_ref[...] = jnp.zeros_like(dnw_ref)

        dxn = None
        off = 0
        for p in piece_refs:
            width = p.shape[1]
            for q in range(w_ref.shape[0]):
                lo, hi = max(off, q * PAIR), min(off + width, (q + 1) * PAIR)
                if lo < hi:
                    term = _mm_nt(p[:, lo - off:hi - off], w_ref[q, :, lo - q * PAIR:hi - q * PAIR])
                    dxn = term if dxn is None else dxn + term
            off += width
        xv = x_ref[...]
        r = lax.rsqrt(jnp.mean(xv * xv, axis=-1, keepdims=True) + EPS)
        xh = xv * r
        dxh = dxn * nw_ref[...]
        gx_ref[...] = dx2_ref[...] + r * (dxh - xh * jnp.mean(dxh * xh, axis=-1, keepdims=True))
        row0 = lax.broadcasted_iota(jnp.int32, (8, D), 0) == 0
        dnw_ref[...] += jnp.where(row0, jnp.sum(dxn * xh, axis=0, keepdims=True), 0.0)

    rowblk = pl.BlockSpec((tb, D), lambda i: (i, 0))
    return pl.pallas_call(
        body, grid=(t // tb,),
        in_specs=[pl.BlockSpec((tb, p.shape[1]), lambda i: (i, 0)) for p in pieces]
        + [pl.BlockSpec(w_p.shape, lambda i: (0, 0, 0), pipeline_mode=pl.Buffered(1)), rowblk, rowblk,
           pl.BlockSpec((1, D), lambda i: (0, 0)), pl.BlockSpec(memory_space=pl.ANY)],
        out_specs=[rowblk, pl.BlockSpec((8, D), lambda i: (0, 0))],
        out_shape=[jax.ShapeDtypeStruct((t, D), F32), jax.ShapeDtypeStruct((8, D), F32)],
        name="inproj_dgrad", compiler_params=_cp(60),
    )(*pieces, w_p, x2d, dx2, norm_w, after)


def _adamw_math(w, g, m, v):
    m = B1 * m + (1.0 - B1) * g
    v = B2 * v + (1.0 - B2) * (g * g)
    m_hat = m / (1.0 - B1 ** STEP)
    v_hat = v / (1.0 - B2 ** STEP)
    delta = -LR * (m_hat / (jnp.sqrt(v_hat) + ADAM_EPS) + WD * w)
    return delta, m, v


def _adamw_shard(recv, sums, chip, w, m, v, rows, name):
    nparts, nr, nc = recv.shape

    def body(chip_ref, own_ref, p_ref, w_ref, m_ref, v_ref, g_ref, d_ref, nm_ref, nv_ref):
        g = own_ref[0].astype(F32)
        for s in range(nparts):
            g = g + p_ref[s].astype(F32)
        d, nm, nv = _adamw_math(w_ref[...], g, m_ref[...], v_ref[...])
        g_ref[...] = g
        d_ref[...] = d
        nm_ref[...] = nm
        nv_ref[...] = nv

    blk = pl.BlockSpec((rows, nc), lambda i, chip_ref: (i, 0))
    return pl.pallas_call(
        body,
        grid_spec=pltpu.PrefetchScalarGridSpec(
            num_scalar_prefetch=1, grid=(nr // rows,),
            in_specs=[pl.BlockSpec((1, rows, nc), lambda i, chip_ref: (chip_ref[0], i, 0)),
                      pl.BlockSpec((nparts, rows, nc), lambda i, chip_ref: (0, i, 0)), blk, blk, blk],
            out_specs=[blk] * 4),
        out_shape=[jax.ShapeDtypeStruct((nr, nc), F32)] * 4,
        name=name, compiler_params=_cp(48),
    )(chip, sums, recv, w, m, v)


def _adamw_sum8(parts, w, m, v, after, name):
    def body(p_ref, w_ref, m_ref, v_ref, _, g_ref, d_ref, nm_ref, nv_ref):
        g = p_ref[0].astype(F32)
        for s in range(1, NDEV):
            g = g + p_ref[s].astype(F32)
        d, nm, nv = _adamw_math(w_ref[...], g, m_ref[...], v_ref[...])
        g_ref[...] = g
        d_ref[...] = d
        nm_ref[...] = nm
        nv_ref[...] = nv

    vm = pl.BlockSpec(memory_space=pltpu.VMEM)
    return pl.pallas_call(
        body, out_shape=[jax.ShapeDtypeStruct(w.shape, F32)] * 4,
        in_specs=[vm, vm, vm, vm, pl.BlockSpec(memory_space=pl.ANY)], out_specs=[vm] * 4, name=name,
    )(parts, w, m, v, after)


SMALL_ROWS = dict(norm_w=0, lower_bound=1, hgrn_norm_w=3, final_norm_w=4, sinks=5, loss=6)


def _pack_small_grads(dnw, dlb, sums, dsink):
    def body(dnw_ref, dlb_ref, sums_ref, dsink_ref, o_ref):
        o_ref[...] = jnp.zeros_like(o_ref)
        o_ref[0:1, :] = dnw_ref[0:1, :]
        o_ref[1:2, :] = dlb_ref[0:1, :]
        o_ref[3:4, :] = sums_ref[1:2, :]
        o_ref[4:5, :] = sums_ref[0:1, :]
        o_ref[5:6, 0:128] = dsink_ref[0:1, :]
        o_ref[6:7, :] = sums_ref[2:3, :]

    return pl.pallas_call(body, out_shape=jax.ShapeDtypeStruct((8, D), F32), name="pack_small_grads")(dnw, dlb, sums, dsink)


def _adamw_small(parts, ws, ms, vs):
    shapes = [a.shape for a in ws]

    def body(p_ref, *refs):
        w, m, v = refs[0:5], refs[5:10], refs[10:15]
        outs = [refs[15 + 5 * i:20 + 5 * i] for i in range(4)]
        loss_ref = refs[35]

        def total(row, width):
            g = p_ref[0, row:row + 1, 0:width]
            for s in range(1, NDEV):
                g = g + p_ref[s, row:row + 1, 0:width]
            return g

        loss_ref[...] = total(6, 128)
        lb = _lower_bound(w[1])
        ga0 = total(1, D) * lb * (1.0 - lb)
        grads = [total(0, D), None, total(3, D), total(4, D), total(5, QH)]
        for i in (0, 2, 3, 4):
            res = (grads[i],) + _adamw_math(w[i][...], grads[i], m[i][...], v[i][...])
            for o, val in zip(outs, res):
                o[i][...] = val
        for r, g in ((0, ga0), (1, -ga0)):
            res = (g,) + _adamw_math(w[1][r:r + 1, :], g, m[1][r:r + 1, :], v[1][r:r + 1, :])
            for o, val in zip(outs, res):
                o[1][r:r + 1, :] = val

    res = pl.pallas_call(
        body, out_shape=[jax.ShapeDtypeStruct(s, F32) for s in shapes] * 4 + [jax.ShapeDtypeStruct((1, 128), F32)],
        name="adamw_small",
    )(parts, *ws, *ms, *vs)
    return [res[5 * i:5 * i + 5] for i in range(4)], res[20][0, 0]


def kernel(x, norm_w, w_in, hgrn_lower_bound, hgrn_norm_w, w_branch_hgrn, attn_sinks, w_branch_attn, w_out, final_norm_w, loss_target, m_norm_w, m_w_in, m_hgrn_lower_bound, m_hgrn_norm_w, m_w_branch_hgrn, m_attn_sinks, m_w_branch_attn, m_w_out, m_final_norm_w, v_norm_w, v_w_in, v_hgrn_lower_bound, v_hgrn_norm_w, v_w_branch_hgrn, v_attn_sinks, v_w_branch_attn, v_w_out, v_final_norm_w):
    t = x.shape[1]
    x2d = x.reshape(t, D)
    tgt = loss_target.reshape(t, D)
    fnw = final_norm_w.reshape(1, D)
    row_blk = min(256, t)
    big_blk = min(512, t)

    chip = (2 * lax.axis_index("x") + lax.axis_index("y")).astype(jnp.int32).reshape(1)
    w_p, xn, xnt, proj_own = _gather_in_projection(w_in[0], x2d, norm_w)
    wbh, wba, wout = (g.reshape(D, D) for g in _gather_square(
        [w_branch_hgrn[0].astype(BF16), w_branch_attn[0].astype(BF16), w_out[0].astype(BF16)], after=w_p))

    proj = _inproj_fwd(xn, w_p, proj_own, chip, min(1024, t))
    o_raw, states = _hgrn_fwd(proj, hgrn_lower_bound, big_blk, HGRN_GROUP)
    o_a = _attn_fwd(proj, attn_sinks)
    (dx2, do_raw, do_a, d_hg, d_agm, gh, ga, mg, dyh, dya, dx2b, sums) = _tail(
        o_raw, o_a, proj, x2d, tgt, wbh, wba, wout, hgrn_norm_w, fnw, row_blk)
    d_aq, d_kv, dsink = _attn_bwd(proj, attn_sinks, do_a)
    d_hgrn, dlb = _hgrn_bwd(proj, hgrn_lower_bound, do_raw, states, big_blk, HGRN_GROUP)
    pieces = (d_hgrn, d_hg, d_aq, d_kv, d_agm)
    dw_cat = _inproj_wgrad(xnt, [d_hgrn], CB, 0, None, "inproj_wgrad_hgrn")
    dw_cat = _inproj_wgrad(xnt, [d_hg, d_aq, d_kv], CB // 2, C_HG, dw_cat, "inproj_wgrad_mid")
    dw_cat = _inproj_wgrad(xnt, [d_agm], CB, C_AG, dw_cat, "inproj_wgrad_gates")

    dwin_r = dw_cat.reshape(D, NDEV, IN_SHARD).transpose(1, 0, 2)
    dwbh, dwba, dwout = _wgrad3(gh, dyh, ga, dya, mg, dx2b, big_blk, after=dw_cat)
    slots = lambda a: a.reshape(NDEV, ROW_SHARD, D).astype(BF16)
    got = _exchange_pair([dwin_r], after=dwbh)
    core = lax.axis_index("c").astype(jnp.int32).reshape(1)
    s_in = _pair_sum(dwin_r, got[0], core, 4 * ROW_SHARD, "pair_sum_w_in")
    rin, = _exchange_chips([s_in])
    rbh, rba, rout = _exchange_square([slots(dwbh), slots(dwba), slots(dwout)])
    grad_x, dnw = _inproj_dgrad(pieces, w_p, x2d, dx2, norm_w, big_blk, after=s_in)
    rsm = _exchange_small(_pack_small_grads(dnw, dlb, sums, dsink))
    g_in, d_in, nm_in, nv_in = _adamw_shard(rin, s_in, chip, w_in[0], m_w_in[0], v_w_in[0], 128, "adamw_w_in")
    g_bh, d_bh, nm_bh, nv_bh = _adamw_sum8(
        rbh, w_branch_hgrn[0], m_w_branch_hgrn[0], v_w_branch_hgrn[0], dnw, "adamw_w_bh")
    g_ba, d_ba, nm_ba, nv_ba = _adamw_sum8(
        rba, w_branch_attn[0], m_w_branch_attn[0], v_w_branch_attn[0], dnw, "adamw_w_ba")
    g_out, d_out, nm_out, nv_out = _adamw_sum8(rout, w_out[0], m_w_out[0], v_w_out[0], dnw, "adamw_w_out")
    (sg, sd, sm, sv), loss = _adamw_small(
        rsm,
        (norm_w, hgrn_lower_bound, hgrn_norm_w, fnw, attn_sinks),
        (m_norm_w, m_hgrn_lower_bound, m_hgrn_norm_w, m_final_norm_w.reshape(1, D), m_attn_sinks),
        (v_norm_w, v_hgrn_lower_bound, v_hgrn_norm_w, v_final_norm_w.reshape(1, D), v_attn_sinks))

    def group(s, w_in_v, bh, ba, out):
        nw, lb, hnw, fn, sinks = s
        return (nw, w_in_v[None], lb, hnw, bh[None], sinks, ba[None], out[None], fn.reshape(D))

    return (loss, grad_x.reshape(1, t, D),
            *group(sg, g_in, g_bh, g_ba, g_out), *group(sd, d_in, d_bh, d_ba, d_out),
            *group(sm, nm_in, nm_bh, nm_ba, nm_out), *group(sv, nv_in, nv_bh, nv_ba, nv_out))
```
